```python
import math
import jax, jax.numpy as jnp
from jax import lax
import numpy as np

D_MODEL = 1024
BATCH = 16
SEQ = 2048
DEPTH = 4

N_MIXERS = 4
BLOCK = 128
EPS = 1e-6
SB_HEADS = 16
SB_HEAD_DIM = D_MODEL // SB_HEADS
SB_WIDTH = SB_HEADS * SB_HEAD_DIM
GM_WIDTH = 2 * D_MODEL
GM_CHUNK = 128
GM_GROUPS = 16
CV_WIDTH = 2 * D_MODEL
CV_KERNEL = 31
FX_HEADS = 16
FX_HEAD_DIM = D_MODEL // FX_HEADS
FX_WIDTH = FX_HEADS * FX_HEAD_DIM

kernel_name = "hybrid_sb_gmlp_conv_fox_trunk"


def _layer_counts():
    return [len(range(kind, DEPTH, N_MIXERS)) for kind in range(N_MIXERS)]


def rmsnorm(x, g):
    xf = x.astype(jnp.float32)
    y = xf * lax.rsqrt(jnp.mean(xf * xf, axis=-1, keepdims=True) + EPS)
    return (y * g.astype(jnp.float32)).astype(x.dtype)


def layernorm(x, g, b):
    xf = x.astype(jnp.float32)
    mu = jnp.mean(xf, axis=-1, keepdims=True)
    xc = xf - mu
    var = jnp.mean(xc * xc, axis=-1, keepdims=True)
    y = xc * lax.rsqrt(var + EPS) * g.astype(jnp.float32) + b.astype(jnp.float32)
    return y.astype(x.dtype)


def stick_breaking_attention(q, k, v):
    B, S, H, Dh = q.shape
    scale = 1.0 / math.sqrt(Dh)
    outs = []
    for t0 in range(0, S, BLOCK):
        kend = t0 + BLOCK
        z = jnp.einsum('bthd,bshd->bhts', q[:, t0:kend], k[:, :kend]).astype(jnp.float32) * scale
        t_idx = t0 + jnp.arange(BLOCK)[:, None]
        s_idx = jnp.arange(kend)[None, :]
        mask = s_idx < t_idx
        log_beta = jax.nn.log_sigmoid(z)
        log_rem = jnp.where(mask, jax.nn.log_sigmoid(-z), 0.0)
        after = lax.cumsum(log_rem, axis=3, reverse=True) - log_rem
        w = jnp.where(mask, jnp.exp(log_beta + after), 0.0)
        outs.append(jnp.einsum('bhts,bshd->bthd', w.astype(v.dtype), v[:, :kend]))
    return jnp.concatenate(outs, axis=1)


def forgetting_attention(q, k, v, log_f):
    B, S, H, Dh = q.shape
    scale = 1.0 / math.sqrt(Dh)
    cum = jnp.cumsum(log_f, axis=1).transpose(0, 2, 1)
    outs = []
    for t0 in range(0, S, BLOCK):
        kend = t0 + BLOCK
        logits = jnp.einsum('bthd,bshd->bhts', q[:, t0:kend], k[:, :kend]).astype(jnp.float32) * scale
        logits = logits + cum[:, :, t0:kend, None] - cum[:, :, None, :kend]
        t_idx = t0 + jnp.arange(BLOCK)[:, None]
        s_idx = jnp.arange(kend)[None, :]
        p = jax.nn.softmax(jnp.where(s_idx <= t_idx, logits, -jnp.inf), axis=-1)
        outs.append(jnp.einsum('bhts,bshd->bthd', p.astype(v.dtype), v[:, :kend]))
    return jnp.concatenate(outs, axis=1)


def mixer_stick_breaking(h, w_in, w_out):
    B, S, _ = h.shape
    proj = jnp.einsum('bsd,de->bse', h, w_in)
    q, k, v, g = jnp.split(proj, 4, axis=-1)
    hs = (B, S, SB_HEADS, SB_HEAD_DIM)
    o = stick_breaking_attention(q.reshape(hs), k.reshape(hs), v.reshape(hs)).reshape(B, S, SB_WIDTH)
    return jnp.einsum('bse,ed->bsd', o * jax.nn.silu(g), w_out)


def mixer_chunked_gmlp(h, w_in, v_ln_g, v_ln_b, w_s, b_s, w_out):
    B, S, _ = h.shape
    proj = jnp.einsum('bsd,de->bse', h, w_in)
    uv, g = proj[..., :2 * GM_WIDTH], proj[..., 2 * GM_WIDTH:]
    uv = jax.nn.gelu(uv)
    u, v = jnp.split(uv, 2, axis=-1)
    v = layernorm(v, v_ln_g, v_ln_b)
    tril = jnp.tril(jnp.ones((GM_CHUNK, GM_CHUNK), w_s.dtype))
    vc = v.reshape(B, S // GM_CHUNK, GM_CHUNK, GM_GROUPS, GM_WIDTH // GM_GROUPS)
    s = jnp.einsum('gts,bnsgc->bntgc', w_s * tril, vc) + b_s.T[None, None, :, :, None]
    o = u * s.reshape(B, S, GM_WIDTH)
    return jnp.einsum('bse,ed->bsd', o * jax.nn.silu(g), w_out)


def mixer_conformer_conv(h, w_in, conv_w, conv_b, ln_g, ln_b, w_out):
    proj = jnp.einsum('bsd,de->bse', h, w_in)
    a, b, g = jnp.split(proj, 3, axis=-1)
    y = a * jax.nn.sigmoid(b)
    y = lax.conv_general_dilated(
        y, conv_w[:, None, :], window_strides=(1,), padding=[(CV_KERNEL - 1, 0)],
        dimension_numbers=('NWC', 'WIO', 'NWC'), feature_group_count=CV_WIDTH) + conv_b
    y = jax.nn.silu(layernorm(y, ln_g, ln_b))
    return jnp.einsum('bse,ed->bsd', y * jax.nn.silu(g), w_out)


def mixer_forgetting(h, w_in, b_f, w_out):
    B, S, _ = h.shape
    proj = jnp.einsum('bsd,de->bse', h, w_in)
    q = proj[..., :FX_WIDTH]
    k = proj[..., FX_WIDTH:2 * FX_WIDTH]
    v = proj[..., 2 * FX_WIDTH:3 * FX_WIDTH]
    g = proj[..., 3 * FX_WIDTH:4 * FX_WIDTH]
    f_logit = proj[..., 4 * FX_WIDTH:].astype(jnp.float32) + b_f.astype(jnp.float32)
    log_f = jax.nn.log_sigmoid(f_logit)
    hs = (B, S, FX_HEADS, FX_HEAD_DIM)
    o = forgetting_attention(q.reshape(hs), k.reshape(hs), v.reshape(hs), log_f).reshape(B, S, FX_WIDTH)
    return jnp.einsum('bse,ed->bsd', o * jax.nn.silu(g), w_out)


def _fwd_setup_inputs(seed: int = 0) -> dict:
    key = jax.random.key(seed)
    ks = iter(jax.random.split(key, 32))
    nA, nB, nC, nD = _layer_counts()
    f32 = jnp.float32

    def nrm(shape, scale):
        return jax.random.normal(next(ks), shape, f32) * scale

    def gain(n, width):
        return 1.0 + nrm((n, width), 0.02)

    d_in = D_MODEL ** -0.5
    return {
        "x": nrm((BATCH, SEQ, D_MODEL), 1.0),
        "a_norm": gain(nA, D_MODEL),
        "a_w_in": nrm((nA, D_MODEL, 4 * SB_WIDTH), d_in),
        "a_w_out": nrm((nA, SB_WIDTH, D_MODEL), SB_WIDTH ** -0.5),
        "b_norm": gain(nB, D_MODEL),
        "b_w_in": nrm((nB, D_MODEL, 3 * GM_WIDTH), d_in),
        "b_v_ln_g": gain(nB, GM_WIDTH),
        "b_v_ln_b": nrm((nB, GM_WIDTH), 0.02),
        "b_w_s": nrm((nB, GM_GROUPS, GM_CHUNK, GM_CHUNK), GM_CHUNK ** -0.5),
        "b_b_s": 1.0 + nrm((nB, GM_GROUPS, GM_CHUNK), 0.1),
        "b_w_out": nrm((nB, GM_WIDTH, D_MODEL), GM_WIDTH ** -0.5),
        "c_norm": gain(nC, D_MODEL),
        "c_w_in": nrm((nC, D_MODEL, 3 * CV_WIDTH), d_in),
        "c_conv_w": nrm((nC, CV_KERNEL, CV_WIDTH), CV_KERNEL ** -0.5),
        "c_conv_b": nrm((nC, CV_WIDTH), 0.02),
        "c_ln_g": gain(nC, CV_WIDTH),
        "c_ln_b": nrm((nC, CV_WIDTH), 0.02),
        "c_w_out": nrm((nC, CV_WIDTH, D_MODEL), CV_WIDTH ** -0.5),
        "d_norm": gain(nD, D_MODEL),
        "d_w_in": nrm((nD, D_MODEL, 4 * FX_WIDTH + FX_HEADS), d_in),
        "d_b_f": 3.0 + nrm((nD, FX_HEADS), 0.5),
        "d_w_out": nrm((nD, FX_WIDTH, D_MODEL), FX_WIDTH ** -0.5),
        "final_norm": 1.0 + nrm((D_MODEL,), 0.02),
    }


def _fwd_reference(x, a_norm, a_w_in, a_w_out, b_norm, b_w_in, b_v_ln_g, b_v_ln_b, b_w_s, b_b_s, b_w_out,
              c_norm, c_w_in, c_conv_w, c_conv_b, c_ln_g, c_ln_b, c_w_out,
              d_norm, d_w_in, d_b_f, d_w_out, final_norm):
    for i in range(DEPTH):
        kind, j = i % N_MIXERS, i // N_MIXERS
        if kind == 0:
            x = x + mixer_stick_breaking(rmsnorm(x, a_norm[j]), a_w_in[j], a_w_out[j])
        elif kind == 1:
            x = x + mixer_chunked_gmlp(rmsnorm(x, b_norm[j]), b_w_in[j], b_v_ln_g[j], b_v_ln_b[j],
                                       b_w_s[j], b_b_s[j], b_w_out[j])
        elif kind == 2:
            x = x + mixer_conformer_conv(rmsnorm(x, c_norm[j]), c_w_in[j], c_conv_w[j], c_conv_b[j],
                                         c_ln_g[j], c_ln_b[j], c_w_out[j])
        else:
            x = x + mixer_forgetting(rmsnorm(x, d_norm[j]), d_w_in[j], d_b_f[j], d_w_out[j])
    return rmsnorm(x, final_norm)


import jax as _jax
import jax.numpy as _jnp

TWIN_FORMAT = 'train_step'
FWD_PARAMS = ['x', 'a_norm', 'a_w_in', 'a_w_out', 'b_norm', 'b_w_in', 'b_v_ln_g', 'b_v_ln_b', 'b_w_s', 'b_b_s', 'b_w_out', 'c_norm', 'c_w_in', 'c_conv_w', 'c_conv_b', 'c_ln_g', 'c_ln_b', 'c_w_out', 'd_norm', 'd_w_in', 'd_b_f', 'd_w_out', 'final_norm']
TWIN_WEIGHTS = ['a_norm', 'a_w_in', 'a_w_out', 'b_norm', 'b_w_in', 'b_v_ln_g', 'b_v_ln_b', 'b_w_s', 'b_b_s', 'b_w_out', 'c_norm', 'c_w_in', 'c_conv_w', 'c_conv_b', 'c_ln_g', 'c_ln_b', 'c_w_out', 'd_norm', 'd_w_in', 'd_b_f', 'd_w_out', 'final_norm']
TWIN_DIFF_INPUT = 'x'
TWIN_INPUTS = ['x', 'a_norm', 'a_w_in', 'a_w_out', 'b_norm', 'b_w_in', 'b_v_ln_g', 'b_v_ln_b', 'b_w_s', 'b_b_s', 'b_w_out', 'c_norm', 'c_w_in', 'c_conv_w', 'c_conv_b', 'c_ln_g', 'c_ln_b', 'c_w_out', 'd_norm', 'd_w_in', 'd_b_f', 'd_w_out', 'final_norm', 'loss_target', 'm_a_norm', 'm_a_w_in', 'm_a_w_out', 'm_b_norm', 'm_b_w_in', 'm_b_v_ln_g', 'm_b_v_ln_b', 'm_b_w_s', 'm_b_b_s', 'm_b_w_out', 'm_c_norm', 'm_c_w_in', 'm_c_conv_w', 'm_c_conv_b', 'm_c_ln_g', 'm_c_ln_b', 'm_c_w_out', 'm_d_norm', 'm_d_w_in', 'm_d_b_f', 'm_d_w_out', 'm_final_norm', 'v_a_norm', 'v_a_w_in', 'v_a_w_out', 'v_b_norm', 'v_b_w_in', 'v_b_v_ln_g', 'v_b_v_ln_b', 'v_b_w_s', 'v_b_b_s', 'v_b_w_out', 'v_c_norm', 'v_c_w_in', 'v_c_conv_w', 'v_c_conv_b', 'v_c_ln_g', 'v_c_ln_b', 'v_c_w_out', 'v_d_norm', 'v_d_w_in', 'v_d_b_f', 'v_d_w_out', 'v_final_norm']
TWIN_OUTPUTS = ['loss', 'grad_x', 'grad_a_norm', 'grad_a_w_in', 'grad_a_w_out', 'grad_b_norm', 'grad_b_w_in', 'grad_b_v_ln_g', 'grad_b_v_ln_b', 'grad_b_w_s', 'grad_b_b_s', 'grad_b_w_out', 'grad_c_norm', 'grad_c_w_in', 'grad_c_conv_w', 'grad_c_conv_b', 'grad_c_ln_g', 'grad_c_ln_b', 'grad_c_w_out', 'grad_d_norm', 'grad_d_w_in', 'grad_d_b_f', 'grad_d_w_out', 'grad_final_norm', 'delta_a_norm', 'delta_a_w_in', 'delta_a_w_out', 'delta_b_norm', 'delta_b_w_in', 'delta_b_v_ln_g', 'delta_b_v_ln_b', 'delta_b_w_s', 'delta_b_b_s', 'delta_b_w_out', 'delta_c_norm', 'delta_c_w_in', 'delta_c_conv_w', 'delta_c_conv_b', 'delta_c_ln_g', 'delta_c_ln_b', 'delta_c_w_out', 'delta_d_norm', 'delta_d_w_in', 'delta_d_b_f', 'delta_d_w_out', 'delta_final_norm', 'new_m_a_norm', 'new_m_a_w_in', 'new_m_a_w_out', 'new_m_b_norm', 'new_m_b_w_in', 'new_m_b_v_ln_g', 'new_m_b_v_ln_b', 'new_m_b_w_s', 'new_m_b_b_s', 'new_m_b_w_out', 'new_m_c_norm', 'new_m_c_w_in', 'new_m_c_conv_w', 'new_m_c_conv_b', 'new_m_c_ln_g', 'new_m_c_ln_b', 'new_m_c_w_out', 'new_m_d_norm', 'new_m_d_w_in', 'new_m_d_b_f', 'new_m_d_w_out', 'new_m_final_norm', 'new_v_a_norm', 'new_v_a_w_in', 'new_v_a_w_out', 'new_v_b_norm', 'new_v_b_w_in', 'new_v_b_v_ln_g', 'new_v_b_v_ln_b', 'new_v_b_w_s', 'new_v_b_b_s', 'new_v_b_w_out', 'new_v_c_norm', 'new_v_c_w_in', 'new_v_c_conv_w', 'new_v_c_conv_b', 'new_v_c_ln_g', 'new_v_c_ln_b', 'new_v_c_w_out', 'new_v_d_norm', 'new_v_d_w_in', 'new_v_d_b_f', 'new_v_d_w_out', 'new_v_final_norm']
TWIN_LEAF_KINDS = {'loss': 'loss', 'grad_x': 'grad_x', 'grad_a_norm': 'grad_w', 'grad_a_w_in': 'grad_w', 'grad_a_w_out': 'grad_w', 'grad_b_norm': 'grad_w', 'grad_b_w_in': 'grad_w', 'grad_b_v_ln_g': 'grad_w', 'grad_b_v_ln_b': 'grad_w', 'grad_b_w_s': 'grad_w', 'grad_b_b_s': 'grad_w', 'grad_b_w_out': 'grad_w', 'grad_c_norm': 'grad_w', 'grad_c_w_in': 'grad_w', 'grad_c_conv_w': 'grad_w', 'grad_c_conv_b': 'grad_w', 'grad_c_ln_g': 'grad_w', 'grad_c_ln_b': 'grad_w', 'grad_c_w_out': 'grad_w', 'grad_d_norm': 'grad_w', 'grad_d_w_in': 'grad_w', 'grad_d_b_f': 'grad_w', 'grad_d_w_out': 'grad_w', 'grad_final_norm': 'grad_w', 'delta_a_norm': 'delta_w', 'delta_a_w_in': 'delta_w', 'delta_a_w_out': 'delta_w', 'delta_b_norm': 'delta_w', 'delta_b_w_in': 'delta_w', 'delta_b_v_ln_g': 'delta_w', 'delta_b_v_ln_b': 'delta_w', 'delta_b_w_s': 'delta_w', 'delta_b_b_s': 'delta_w', 'delta_b_w_out': 'delta_w', 'delta_c_norm': 'delta_w', 'delta_c_w_in': 'delta_w', 'delta_c_conv_w': 'delta_w', 'delta_c_conv_b': 'delta_w', 'delta_c_ln_g': 'delta_w', 'delta_c_ln_b': 'delta_w', 'delta_c_w_out': 'delta_w', 'delta_d_norm': 'delta_w', 'delta_d_w_in': 'delta_w', 'delta_d_b_f': 'delta_w', 'delta_d_w_out': 'delta_w', 'delta_final_norm': 'delta_w', 'new_m_a_norm': 'new_m', 'new_m_a_w_in': 'new_m', 'new_m_a_w_out': 'new_m', 'new_m_b_norm': 'new_m', 'new_m_b_w_in': 'new_m', 'new_m_b_v_ln_g': 'new_m', 'new_m_b_v_ln_b': 'new_m', 'new_m_b_w_s': 'new_m', 'new_m_b_b_s': 'new_m', 'new_m_b_w_out': 'new_m', 'new_m_c_norm': 'new_m', 'new_m_c_w_in': 'new_m', 'new_m_c_conv_w': 'new_m', 'new_m_c_conv_b': 'new_m', 'new_m_c_ln_g': 'new_m', 'new_m_c_ln_b': 'new_m', 'new_m_c_w_out': 'new_m', 'new_m_d_norm': 'new_m', 'new_m_d_w_in': 'new_m', 'new_m_d_b_f': 'new_m', 'new_m_d_w_out': 'new_m', 'new_m_final_norm': 'new_m', 'new_v_a_norm': 'new_v', 'new_v_a_w_in': 'new_v', 'new_v_a_w_out': 'new_v', 'new_v_b_norm': 'new_v', 'new_v_b_w_in': 'new_v', 'new_v_b_v_ln_g': 'new_v', 'new_v_b_v_ln_b': 'new_v', 'new_v_b_w_s': 'new_v', 'new_v_b_b_s': 'new_v', 'new_v_b_w_out': 'new_v', 'new_v_c_norm': 'new_v', 'new_v_c_w_in': 'new_v', 'new_v_c_conv_w': 'new_v', 'new_v_c_conv_b': 'new_v', 'new_v_c_ln_g': 'new_v', 'new_v_c_ln_b': 'new_v', 'new_v_c_w_out': 'new_v', 'new_v_d_norm': 'new_v', 'new_v_d_w_in': 'new_v', 'new_v_d_b_f': 'new_v', 'new_v_d_w_out': 'new_v', 'new_v_final_norm': 'new_v'}


def _forward(args):
    return _fwd_reference(*[args[k] for k in FWD_PARAMS])


def _output_shape():
    out = _jax.eval_shape(lambda: _forward(_fwd_setup_inputs(0)))
    return out.shape, out.dtype

N_MICROBATCH = 1
ADAM_LR = 0.001
ADAM_B1 = 0.9
ADAM_B2 = 0.999
ADAM_EPS = 1e-08
ADAM_WD = 0.01
ADAM_STEP = 10
PER_EXAMPLE_BATCH_AXIS = {'x': 0, 'loss_target': 0}
SHARED_INPUTS = []
_WEIGHT_DTYPES = {'a_norm': _jnp.float32, 'a_w_in': _jnp.float32, 'a_w_out': _jnp.float32, 'b_norm': _jnp.float32, 'b_w_in': _jnp.float32, 'b_v_ln_g': _jnp.float32, 'b_v_ln_b': _jnp.float32, 'b_w_s': _jnp.float32, 'b_b_s': _jnp.float32, 'b_w_out': _jnp.float32, 'c_norm': _jnp.float32, 'c_w_in': _jnp.float32, 'c_conv_w': _jnp.float32, 'c_conv_b': _jnp.float32, 'c_ln_g': _jnp.float32, 'c_ln_b': _jnp.float32, 'c_w_out': _jnp.float32, 'd_norm': _jnp.float32, 'd_w_in': _jnp.float32, 'd_b_f': _jnp.float32, 'd_w_out': _jnp.float32, 'final_norm': _jnp.float32}
MOMENT_SCALE = {'a_norm': 1.293587e-01, 'a_w_in': 6.053360e-02, 'a_w_out': 7.645215e-02, 'b_norm': 1.259810e-01, 'b_w_in': 5.071465e-02, 'b_v_ln_g': 3.175237e-02, 'b_v_ln_b': 3.196231e-02, 'b_w_s': 3.129604e-02, 'b_b_s': 4.462895e-02, 'b_w_out': 7.755809e-02, 'c_norm': 8.150715e-02, 'c_w_in': 3.289254e-02, 'c_conv_w': 3.833627e-02, 'c_conv_b': 8.440535e-02, 'c_ln_g': 4.781584e-02, 'c_ln_b': 3.833015e-02, 'c_w_out': 5.329738e-02, 'd_norm': 5.075633e-02, 'd_w_in': 2.454173e-02, 'd_b_f': 1.023826e-01, 'd_w_out': 2.658354e-02, 'final_norm': 3.196142e+01}


def _to_microbatches(a, axis):
    t = _jnp.moveaxis(a, axis, 0)
    t = t.reshape((N_MICROBATCH, t.shape[0] // N_MICROBATCH) + t.shape[1:])
    return _jnp.moveaxis(t, 1, axis + 1)


def setup_inputs(seed: int = 0) -> dict:
    inp = _fwd_setup_inputs(seed)
    key = _jax.random.fold_in(_jax.random.key(seed), 7919)
    shape, _ = _output_shape()
    out = dict(inp)
    out["loss_target"] = _jax.random.normal(_jax.random.fold_in(key, 0), shape, _jnp.float32)
    for i, name in enumerate(TWIN_WEIGHTS):
        w = inp[name].astype(_jnp.float32)
        if MOMENT_SCALE is None:
            s = _jnp.sqrt(_jnp.mean(_jnp.square(w)) + 1e-30)
        else:
            s = MOMENT_SCALE[name]
        km, kv = _jax.random.split(_jax.random.fold_in(key, i + 1))
        out[name] = w
        out["m_" + name] = s * _jax.random.normal(km, w.shape, _jnp.float32)
        out["v_" + name] = (s * s) * _jax.random.uniform(kv, w.shape, _jnp.float32, 0.5, 1.5)
    if N_MICROBATCH > 1:
        for name, axis in PER_EXAMPLE_BATCH_AXIS.items():
            out[name] = _to_microbatches(out[name], axis)
    return {'x': out['x'], 'a_norm': out['a_norm'], 'a_w_in': out['a_w_in'], 'a_w_out': out['a_w_out'], 'b_norm': out['b_norm'], 'b_w_in': out['b_w_in'], 'b_v_ln_g': out['b_v_ln_g'], 'b_v_ln_b': out['b_v_ln_b'], 'b_w_s': out['b_w_s'], 'b_b_s': out['b_b_s'], 'b_w_out': out['b_w_out'], 'c_norm': out['c_norm'], 'c_w_in': out['c_w_in'], 'c_conv_w': out['c_conv_w'], 'c_conv_b': out['c_conv_b'], 'c_ln_g': out['c_ln_g'], 'c_ln_b': out['c_ln_b'], 'c_w_out': out['c_w_out'], 'd_norm': out['d_norm'], 'd_w_in': out['d_w_in'], 'd_b_f': out['d_b_f'], 'd_w_out': out['d_w_out'], 'final_norm': out['final_norm'], 'loss_target': out['loss_target'], 'm_a_norm': out['m_a_norm'], 'm_a_w_in': out['m_a_w_in'], 'm_a_w_out': out['m_a_w_out'], 'm_b_norm': out['m_b_norm'], 'm_b_w_in': out['m_b_w_in'], 'm_b_v_ln_g': out['m_b_v_ln_g'], 'm_b_v_ln_b': out['m_b_v_ln_b'], 'm_b_w_s': out['m_b_w_s'], 'm_b_b_s': out['m_b_b_s'], 'm_b_w_out': out['m_b_w_out'], 'm_c_norm': out['m_c_norm'], 'm_c_w_in': out['m_c_w_in'], 'm_c_conv_w': out['m_c_conv_w'], 'm_c_conv_b': out['m_c_conv_b'], 'm_c_ln_g': out['m_c_ln_g'], 'm_c_ln_b': out['m_c_ln_b'], 'm_c_w_out': out['m_c_w_out'], 'm_d_norm': out['m_d_norm'], 'm_d_w_in': out['m_d_w_in'], 'm_d_b_f': out['m_d_b_f'], 'm_d_w_out': out['m_d_w_out'], 'm_final_norm': out['m_final_norm'], 'v_a_norm': out['v_a_norm'], 'v_a_w_in': out['v_a_w_in'], 'v_a_w_out': out['v_a_w_out'], 'v_b_norm': out['v_b_norm'], 'v_b_w_in': out['v_b_w_in'], 'v_b_v_ln_g': out['v_b_v_ln_g'], 'v_b_v_ln_b': out['v_b_v_ln_b'], 'v_b_w_s': out['v_b_w_s'], 'v_b_b_s': out['v_b_b_s'], 'v_b_w_out': out['v_b_w_out'], 'v_c_norm': out['v_c_norm'], 'v_c_w_in': out['v_c_w_in'], 'v_c_conv_w': out['v_c_conv_w'], 'v_c_conv_b': out['v_c_conv_b'], 'v_c_ln_g': out['v_c_ln_g'], 'v_c_ln_b': out['v_c_ln_b'], 'v_c_w_out': out['v_c_w_out'], 'v_d_norm': out['v_d_norm'], 'v_d_w_in': out['v_d_w_in'], 'v_d_b_f': out['v_d_b_f'], 'v_d_w_out': out['v_d_w_out'], 'v_final_norm': out['v_final_norm']}


def _loss(weights, diff, rest, loss_target):
    with _jax.named_scope("forward"):
        args = {**rest, TWIN_DIFF_INPUT: diff, **{k: w.astype(_WEIGHT_DTYPES[k]) for k, w in weights.items()}}
        y = _forward(args)
    with _jax.named_scope("loss_head"):
        err = _jnp.square(y.astype(_jnp.float32) - loss_target)
        return 0.5 * _jnp.sum(_jnp.mean(err, axis=-1)) if err.ndim else 0.5 * err


def _adamw(w, g, m, v):
    m = ADAM_B1 * m + (1.0 - ADAM_B1) * g
    v = ADAM_B2 * v + (1.0 - ADAM_B2) * _jnp.square(g)
    m_hat = m / (1.0 - ADAM_B1 ** ADAM_STEP)
    v_hat = v / (1.0 - ADAM_B2 ** ADAM_STEP)
    delta = -ADAM_LR * (m_hat / (_jnp.sqrt(v_hat) + ADAM_EPS) + ADAM_WD * w)
    return delta, m, v


def reference(x, a_norm, a_w_in, a_w_out, b_norm, b_w_in, b_v_ln_g, b_v_ln_b, b_w_s, b_b_s, b_w_out, c_norm, c_w_in, c_conv_w, c_conv_b, c_ln_g, c_ln_b, c_w_out, d_norm, d_w_in, d_b_f, d_w_out, final_norm, loss_target, m_a_norm, m_a_w_in, m_a_w_out, m_b_norm, m_b_w_in, m_b_v_ln_g, m_b_v_ln_b, m_b_w_s, m_b_b_s, m_b_w_out, m_c_norm, m_c_w_in, m_c_conv_w, m_c_conv_b, m_c_ln_g, m_c_ln_b, m_c_w_out, m_d_norm, m_d_w_in, m_d_b_f, m_d_w_out, m_final_norm, v_a_norm, v_a_w_in, v_a_w_out, v_b_norm, v_b_w_in, v_b_v_ln_g, v_b_v_ln_b, v_b_w_s, v_b_b_s, v_b_w_out, v_c_norm, v_c_w_in, v_c_conv_w, v_c_conv_b, v_c_ln_g, v_c_ln_b, v_c_w_out, v_d_norm, v_d_w_in, v_d_b_f, v_d_w_out, v_final_norm):
    given = dict(x=x, a_norm=a_norm, a_w_in=a_w_in, a_w_out=a_w_out, b_norm=b_norm, b_w_in=b_w_in, b_v_ln_g=b_v_ln_g, b_v_ln_b=b_v_ln_b, b_w_s=b_w_s, b_b_s=b_b_s, b_w_out=b_w_out, c_norm=c_norm, c_w_in=c_w_in, c_conv_w=c_conv_w, c_conv_b=c_conv_b, c_ln_g=c_ln_g, c_ln_b=c_ln_b, c_w_out=c_w_out, d_norm=d_norm, d_w_in=d_w_in, d_b_f=d_b_f, d_w_out=d_w_out, final_norm=final_norm, loss_target=loss_target, m_a_norm=m_a_norm, m_a_w_in=m_a_w_in, m_a_w_out=m_a_w_out, m_b_norm=m_b_norm, m_b_w_in=m_b_w_in, m_b_v_ln_g=m_b_v_ln_g, m_b_v_ln_b=m_b_v_ln_b, m_b_w_s=m_b_w_s, m_b_b_s=m_b_b_s, m_b_w_out=m_b_w_out, m_c_norm=m_c_norm, m_c_w_in=m_c_w_in, m_c_conv_w=m_c_conv_w, m_c_conv_b=m_c_conv_b, m_c_ln_g=m_c_ln_g, m_c_ln_b=m_c_ln_b, m_c_w_out=m_c_w_out, m_d_norm=m_d_norm, m_d_w_in=m_d_w_in, m_d_b_f=m_d_b_f, m_d_w_out=m_d_w_out, m_final_norm=m_final_norm, v_a_norm=v_a_norm, v_a_w_in=v_a_w_in, v_a_w_out=v_a_w_out, v_b_norm=v_b_norm, v_b_w_in=v_b_w_in, v_b_v_ln_g=v_b_v_ln_g, v_b_v_ln_b=v_b_v_ln_b, v_b_w_s=v_b_w_s, v_b_b_s=v_b_b_s, v_b_w_out=v_b_w_out, v_c_norm=v_c_norm, v_c_w_in=v_c_w_in, v_c_conv_w=v_c_conv_w, v_c_conv_b=v_c_conv_b, v_c_ln_g=v_c_ln_g, v_c_ln_b=v_c_ln_b, v_c_w_out=v_c_w_out, v_d_norm=v_d_norm, v_d_w_in=v_d_w_in, v_d_b_f=v_d_b_f, v_d_w_out=v_d_w_out, v_final_norm=v_final_norm)
    weights = {n: given[n] for n in TWIN_WEIGHTS}
    shared = {n: given[n] for n in SHARED_INPUTS}
    per_example = {n: given[n] for n in ['x']}
    grad_fn = _jax.value_and_grad(_loss, argnums=(0, 1))

    def one_microbatch(ex, loss_target):
        ex = dict(ex)
        diff = ex.pop(TWIN_DIFF_INPUT)
        return grad_fn(weights, diff, {**shared, **ex}, loss_target)

    if N_MICROBATCH == 1:
        loss, (grad_w, grad_x) = one_microbatch(per_example, given["loss_target"])
    else:
        def body(carry, xs):
            loss_sum, grad_sum = carry
            l_k, (gw_k, gx_k) = one_microbatch(xs[0], xs[1])
            with _jax.named_scope("update"):
                return (loss_sum + l_k, _jax.tree.map(_jnp.add, grad_sum, gw_k)), gx_k

        init = (_jnp.zeros((), _jnp.float32), _jax.tree.map(_jnp.zeros_like, weights))
        (loss, grad_w), grad_x = _jax.lax.scan(body, init, (per_example, given["loss_target"]))
    with _jax.named_scope("update"):
        delta_w, new_m, new_v = {}, {}, {}
        for n in TWIN_WEIGHTS:
            delta_w[n], new_m[n], new_v[n] = _adamw(weights[n], grad_w[n], given["m_" + n], given["v_" + n])
    return (loss, grad_x, *[grad_w[n] for n in TWIN_WEIGHTS], *[delta_w[n] for n in TWIN_WEIGHTS],
            *[new_m[n] for n in TWIN_WEIGHTS], *[new_v[n] for n in TWIN_WEIGHTS])
```

```python
import functools
import math

import jax
import jax.numpy as jnp
from jax import lax
from jax.experimental import pallas as pl
from jax.experimental.pallas import tpu as pltpu

F32 = jnp.float32
BF16 = jnp.bfloat16

EPS = 1e-6
SB_HEADS = 16
CONV_HALO = 32
BLK = 128
LANES = 128
N_DEV = 8
MESH_AXES = ("x", "y", "c")

ADAM_LR = 0.001
ADAM_B1 = 0.9
ADAM_B2 = 0.999
ADAM_EPS = 1e-08
ADAM_WD = 0.01
ADAM_STEP = 10

VMEM_LIMIT = 56 * 1024 * 1024
NEG_BIG = -1e30

_NN = (((1,), (0,)), ((), ()))
_NT = (((1,), (1,)), ((), ()))
_TN = (((0,), (0,)), ((), ()))


def _dot(a, b, dims=_NN):
    return lax.dot_general(a, b, dims, preferred_element_type=F32)


def _split_dot(x, m):
    hi = x.astype(BF16)
    lo = (x - hi.astype(F32)).astype(BF16)
    return _dot(hi, m) + _dot(lo, m)


def _split3_dot(x, m):
    hi = x.astype(BF16)
    r1 = x - hi.astype(F32)
    mid = r1.astype(BF16)
    lo = (r1 - mid.astype(F32)).astype(BF16)
    return _dot(hi, m) + _dot(mid, m) + _dot(lo, m)


def _params(sem=None):
    kw = dict(vmem_limit_bytes=VMEM_LIMIT)
    if sem is not None:
        kw["dimension_semantics"] = sem
    return pltpu.CompilerParams(**kw)


def _sigmoid(x):
    return jax.nn.sigmoid(x)


def _silu(x):
    return x * _sigmoid(x)


def _dsilu(x):
    s = _sigmoid(x)
    return s * (1.0 + x * (1.0 - s))


_GELU_C = math.sqrt(2.0 / math.pi)


def _gelu(x):
    return 0.5 * x * (1.0 + jnp.tanh(_GELU_C * (x + 0.044715 * x * x * x)))


def _dgelu(x):
    th = jnp.tanh(_GELU_C * (x + 0.044715 * x * x * x))
    return 0.5 * (1.0 + th) + 0.5 * x * (1.0 - th * th) * _GELU_C * (1.0 + 3.0 * 0.044715 * x * x)


def _mm(a, b, mode, M, N, K, out_dtype, name, tm, tn, tk, a_spec=None, b_spec=None, o_spec=None, out_shape=None):
    tm, tn, tk = min(tm, M), min(tn, N), min(tk, K)
    assert M % tm == 0 and N % tn == 0 and K % tk == 0, (name, M, N, K, tm, tn, tk)
    nk = K // tk
    dims = {"nn": _NN, "nt": _NT, "tn": _TN}[mode]
    if a_spec is None:
        a_spec = (pl.BlockSpec((tk, tm), lambda i, j, k: (k, i)) if mode == "tn"
                  else pl.BlockSpec((tm, tk), lambda i, j, k: (i, k)))
    if b_spec is None:
        b_spec = (pl.BlockSpec((tn, tk), lambda i, j, k: (j, k)) if mode == "nt"
                  else pl.BlockSpec((tk, tn), lambda i, j, k: (k, j)))
    if o_spec is None:
        o_spec = pl.BlockSpec((tm, tn), lambda i, j, k: (i, j))
    if out_shape is None:
        out_shape = (M, N)

    def body(a_ref, b_ref, o_ref, acc_ref):
        k = pl.program_id(2)

        @pl.when(k == 0)
        def _():
            acc_ref[...] = jnp.zeros_like(acc_ref)

        acc_ref[...] += _dot(a_ref[...].astype(BF16), b_ref[...].astype(BF16), dims)

        @pl.when(k == nk - 1)
        def _():
            o_ref[...] = acc_ref[...].astype(o_ref.dtype)

    return pl.pallas_call(
        body, name=name, grid=(M // tm, N // tn, nk),
        in_specs=[a_spec, b_spec], out_specs=o_spec,
        out_shape=jax.ShapeDtypeStruct(out_shape, out_dtype),
        scratch_shapes=[pltpu.VMEM((tm, tn), F32)],
        compiler_params=_params(("parallel", "parallel", "arbitrary")),
    )(a, b)


def _mm_w_dev(a, w3, name, out_dtype=F32, tm=512):
    M, K = a.shape
    n8 = w3.shape[2]
    tn = n8 if n8 <= 768 else 512
    per = n8 // tn
    b_spec = pl.BlockSpec((None, K, tn), lambda i, j, k: (j // per, 0, j % per))
    return _mm(a, w3, "nn", M, N_DEV * n8, K, out_dtype, name, tm, tn, K, b_spec=b_spec)


def _mm_wT_dev(a, w3, name, out_dtype=F32, tm=512):
    M, N = a.shape
    K, n8 = w3.shape[1], w3.shape[2]
    tk = 256 if n8 % 256 == 0 else n8
    per = n8 // tk
    b_spec = pl.BlockSpec((None, K, tk), lambda i, j, k: (k // per, 0, k % per))
    return _mm(a, w3, "nt", M, K, N, out_dtype, name, tm, K, tk, b_spec=b_spec)


def _mm_grad_dev(h, d, name, out_dtype=BF16):
    T, M = h.shape
    N = d.shape[1]
    n8 = N // N_DEV
    tn = n8 if n8 <= 768 else 512
    per = n8 // tn
    tm = min(M, 1024)
    o_spec = pl.BlockSpec((None, tm, tn), lambda i, j, k: (j // per, i, j % per))
    return _mm(h, d, "tn", M, N, T, out_dtype, name, tm, tn, 512, o_spec=o_spec, out_shape=(N_DEV, M, n8))


def _me():
    x, y, c = lax.axis_index("x"), lax.axis_index("y"), lax.axis_index("c")
    return x, y, c


def _peer(r):
    x, y, c = _me()
    px = 1 - x if (r >> 2) & 1 else x
    py = 1 - y if (r >> 1) & 1 else y
    pc = 1 - c if r & 1 else c
    return (px, py, pc), 4 * px + 2 * py + pc


def _exchange(arrays, kinds, name):
    n = len(arrays)
    out_shapes = []
    for a, kind in zip(arrays, kinds):
        shp = (N_DEV,) + a.shape if kind == "gather" else a.shape
        out_shapes.append(jax.ShapeDtypeStruct(shp, a.dtype))

    def body(*refs):
        ins, outs = refs[:n], refs[n:2 * n]
        send_sems, recv_sems, local_sems = refs[2 * n:]
        x, y, c = _me()
        me = 4 * x + 2 * y + c

        def src(k, pid):
            return ins[k] if kinds[k] == "gather" else ins[k].at[pid]

        local = [pltpu.make_async_copy(src(k, me), outs[k].at[me], local_sems.at[k]) for k in range(n)]
        for cp in local:
            cp.start()
        copies = []
        for r in range(1, N_DEV):
            peer, pid = _peer(r)
            for k in range(n):
                cp = pltpu.make_async_remote_copy(
                    src_ref=src(k, pid), dst_ref=outs[k].at[me],
                    send_sem=send_sems.at[k, r - 1], recv_sem=recv_sems.at[k, r - 1],
                    device_id=peer, device_id_type=pl.DeviceIdType.MESH)
                cp.start()
                copies.append(cp)
        for r in range(1, N_DEV):
            peer, pid = _peer(r)
            for k in range(n):
                pltpu.make_async_remote_copy(
                    src_ref=src(k, pid), dst_ref=outs[k].at[pid],
                    send_sem=send_sems.at[k, r - 1], recv_sem=recv_sems.at[k, r - 1],
                    device_id=peer, device_id_type=pl.DeviceIdType.MESH).wait_recv()
        for cp in copies:
            cp.wait_send()
        for cp in local:
            cp.wait()

    any_spec = pl.BlockSpec(memory_space=pl.ANY)
    return pl.pallas_call(
        body, name=name,
        in_specs=[any_spec] * n, out_specs=[any_spec] * n, out_shape=out_shapes,
        scratch_shapes=[pltpu.SemaphoreType.DMA((n, N_DEV - 1)), pltpu.SemaphoreType.DMA((n, N_DEV - 1)),
                        pltpu.SemaphoreType.DMA((n,))],
    )(*arrays)


def _rmsnorm_fwd(x, g, name):
    T, D = x.shape
    tr = min(256, T)

    def body(x_ref, g_ref, h_ref):
        xv = x_ref[...]
        r = lax.rsqrt(jnp.mean(xv * xv, axis=-1, keepdims=True) + EPS)
        h_ref[...] = (xv * r * g_ref[...]).astype(BF16)

    return pl.pallas_call(
        body, name=name, grid=(T // tr,),
        in_specs=[pl.BlockSpec((tr, D), lambda i: (i, 0)), pl.BlockSpec((1, D), lambda i: (0, 0))],
        out_specs=pl.BlockSpec((tr, D), lambda i: (i, 0)),
        out_shape=jax.ShapeDtypeStruct((T, D), BF16),
        compiler_params=_params(("parallel",)),
    )(x, g)


def _rmsnorm_bwd(x, g, dh, dres, name):
    T, D = x.shape
    tr = min(256, T)

    def body(x_ref, g_ref, dh_ref, dres_ref, dx_ref, dg_ref):
        i = pl.program_id(0)
        xv = x_ref[...]
        r = lax.rsqrt(jnp.mean(xv * xv, axis=-1, keepdims=True) + EPS)
        xh = xv * r
        dhv = dh_ref[...]
        dxh = dhv * g_ref[...]
        dx_ref[...] = dres_ref[...] + r * (dxh - xh * jnp.mean(dxh * xh, axis=-1, keepdims=True))

        @pl.when(i == 0)
        def _():
            dg_ref[...] = jnp.zeros_like(dg_ref)

        dg_ref[...] += jnp.sum(dhv * xh, axis=0, keepdims=True)

    row = pl.BlockSpec((tr, D), lambda i: (i, 0))
    vec = pl.BlockSpec((1, D), lambda i: (0, 0))
    return pl.pallas_call(
        body, name=name, grid=(T // tr,),
        in_specs=[row, vec, row, row], out_specs=[row, vec],
        out_shape=[jax.ShapeDtypeStruct((T, D), F32), jax.ShapeDtypeStruct((1, D), F32)],
        compiler_params=_params(("arbitrary",)),
    )(x, g, dh, dres)


def _loss_head(x, g, target):
    T, D = x.shape
    tr = min(256, T)

    def body(x_ref, g_ref, t_ref, loss_ref, dx_ref, dg_ref):
        i = pl.program_id(0)
        xv = x_ref[...]
        gv = g_ref[...]
        r = lax.rsqrt(jnp.mean(xv * xv, axis=-1, keepdims=True) + EPS)
        xh = xv * r
        diff = xh * gv - t_ref[...]
        dy = diff * (1.0 / D)
        dxh = dy * gv
        dx_ref[...] = r * (dxh - xh * jnp.mean(dxh * xh, axis=-1, keepdims=True))

        @pl.when(i == 0)
        def _():
            dg_ref[...] = jnp.zeros_like(dg_ref)
            loss_ref[...] = jnp.zeros_like(loss_ref)

        dg_ref[...] += jnp.sum(dy * xh, axis=0, keepdims=True)
        part = jnp.sum(jnp.sum(diff * diff, axis=1, keepdims=True), axis=0, keepdims=True)
        loss_ref[...] += (0.5 / D) * part

    row = pl.BlockSpec((tr, D), lambda i: (i, 0))
    vec = pl.BlockSpec((1, D), lambda i: (0, 0))
    return pl.pallas_call(
        body, name="loss_head", grid=(T // tr,),
        in_specs=[row, vec, row],
        out_specs=[pl.BlockSpec((1, 1), lambda i: (0, 0)), row, vec],
        out_shape=[jax.ShapeDtypeStruct((1, 1), F32), jax.ShapeDtypeStruct((T, D), F32),
                   jax.ShapeDtypeStruct((1, D), F32)],
        compiler_params=_params(("arbitrary",)),
    )(x, g, target)


def _adamw(parts, w, m, v, name):
    P, R, C = parts.shape
    tr = R
    for cand in (128, 64, 32, 16, 8):
        if R % cand == 0:
            tr = cand
            break

    def body(p_ref, w_ref, m_ref, v_ref, g_out, d_out, m_out, v_out):
        g = p_ref[0].astype(F32)
        for p in range(1, P):
            g = g + p_ref[p].astype(F32)
        wv = w_ref[...]
        mn = ADAM_B1 * m_ref[...] + (1.0 - ADAM_B1) * g
        vn = ADAM_B2 * v_ref[...] + (1.0 - ADAM_B2) * (g * g)
        m_hat = mn / (1.0 - ADAM_B1 ** ADAM_STEP)
        v_hat = vn / (1.0 - ADAM_B2 ** ADAM_STEP)
        g_out[...] = g
        d_out[...] = -ADAM_LR * (m_hat / (jnp.sqrt(v_hat) + ADAM_EPS) + ADAM_WD * wv)
        m_out[...] = mn
        v_out[...] = vn

    row = pl.BlockSpec((tr, C), lambda i: (i, 0))
    return pl.pallas_call(
        body, name=name, grid=(R // tr,),
        in_specs=[pl.BlockSpec((P, tr, C), lambda i: (0, i, 0)), row, row, row],
        out_specs=[row, row, row, row],
        out_shape=[jax.ShapeDtypeStruct((R, C), F32)] * 4,
        compiler_params=_params(("parallel",)),
    )(parts, w, m, v)


def _sum_parts(parts, name):
    P, R, C = parts.shape
    tr = 128 if R % 128 == 0 else R

    def body(p_ref, o_ref):
        g = p_ref[0]
        for p in range(1, P):
            g = g + p_ref[p]
        o_ref[...] = g

    return pl.pallas_call(
        body, name=name, grid=(R // tr,),
        in_specs=[pl.BlockSpec((P, tr, C), lambda i: (0, i, 0))],
        out_specs=pl.BlockSpec((tr, C), lambda i: (i, 0)),
        out_shape=jax.ShapeDtypeStruct((R, C), F32),
        compiler_params=_params(("parallel",)),
    )(parts)


def _attn_consts(Dh):
    assert Dh & (Dh - 1) == 0 and Dh <= LANES
    lane_head = lax.shift_right_logical(lax.broadcasted_iota(jnp.int32, (1, LANES), 1), Dh.bit_length() - 1)
    row = lax.broadcasted_iota(jnp.int32, (BLK, BLK), 0)
    col = lax.broadcasted_iota(jnp.int32, (BLK, BLK), 1)
    return lane_head, row, col


def _logsig_parts(z):
    sp = jnp.log(1.0 + jnp.exp(-jnp.abs(z)))
    return jnp.minimum(z, 0.0) - sp, -jnp.maximum(z, 0.0) - sp


def _sb_fwd(proj3, W, heads):
    B, S, _ = proj3.shape
    Dh = W // heads
    hpb = LANES // Dh
    P, NQ = W // LANES, S // BLK
    scale = 1.0 / math.sqrt(Dh)

    def body(q_ref, k_ref, v_ref, g_ref, o_ref, y_ref):
        i = pl.program_id(2)
        lane_head, row, col = _attn_consts(Dh)
        tri = col < row
        msuf = (row > col).astype(BF16)
        q = q_ref[...]
        o = jnp.zeros((BLK, LANES), F32)
        for hh in range(hpb):
            hm = lane_head == hh
            qm = jnp.where(hm, q, 0.0).astype(BF16)

            def step(jj, carry):
                rem, acc = carry
                off = pl.multiple_of((i - jj) * BLK, BLK)
                kj = k_ref[pl.ds(off, BLK), :].astype(BF16)
                vj = v_ref[pl.ds(off, BLK), :].astype(BF16)
                z = _dot(qm, kj, _NT) * scale
                lb, lr = _logsig_parts(z)
                msk = jnp.logical_or(tri, jj > 0)
                lr = jnp.where(msk, lr, 0.0)
                after = _split_dot(lr, msuf) + rem
                w = jnp.where(msk, jnp.exp(lb + after), 0.0)
                acc = acc + _dot(w.astype(BF16), vj)
                rem = rem + jnp.sum(lr, axis=1, keepdims=True)
                return rem, acc

            _, acc = lax.fori_loop(0, i + 1, step, (jnp.zeros((BLK, 1), F32), jnp.zeros((BLK, LANES), F32)))
            o = jnp.where(hm, acc, o)
        o_ref[...] = o
        y_ref[...] = (o * _silu(g_ref[...])).astype(BF16)

    blk = lambda sec: pl.BlockSpec((None, BLK, LANES), lambda b, p, i: (b, i, sec * P + p))
    full = lambda sec: pl.BlockSpec((None, S, LANES), lambda b, p, i: (b, 0, sec * P + p))
    out = pl.BlockSpec((None, BLK, LANES), lambda b, p, i: (b, i, p))
    return pl.pallas_call(
        body, name="sb_fwd", grid=(B, P, NQ),
        in_specs=[blk(0), full(1), full(2), blk(3)], out_specs=[out, out],
        out_shape=[jax.ShapeDtypeStruct((B, S, W), F32), jax.ShapeDtypeStruct((B, S, W), BF16)],
        compiler_params=_params(("parallel", "parallel", "arbitrary")),
    )(proj3, proj3, proj3, proj3)


def _sb_bwd(proj3, o, dy, W, heads):
    B, S, _ = proj3.shape
    Dh = W // heads
    hpb = LANES // Dh
    P, NQ = W // LANES, S // BLK
    scale = 1.0 / math.sqrt(Dh)

    def body(q_ref, k_ref, v_ref, g_ref, o_ref, dy_ref, dq_ref, dg_ref, dk_ref, dv_ref, e_ref, sig_ref):
        i = pl.program_id(2)

        @pl.when(i == 0)
        def _():
            dk_ref[...] = jnp.zeros_like(dk_ref)
            dv_ref[...] = jnp.zeros_like(dv_ref)

        lane_head, row, col = _attn_consts(Dh)
        tri = col < row
        msuf = (row > col).astype(BF16)
        mpre = (row < col).astype(BF16)
        q = q_ref[...]
        g = g_ref[...]
        dyv = dy_ref[...].astype(F32)
        do = dyv * _silu(g)
        dg_ref[...] = (dyv * o_ref[...] * _dsilu(g)).astype(dg_ref.dtype)
        dq = jnp.zeros((BLK, LANES), F32)
        for hh in range(hpb):
            hm = lane_head == hh
            qm = jnp.where(hm, q, 0.0).astype(BF16)
            dom = jnp.where(hm, do, 0.0).astype(BF16)

            def weights(jj, rem):
                j = i - jj
                off = pl.multiple_of(j * BLK, BLK)
                kj = k_ref[pl.ds(off, BLK), :].astype(BF16)
                vj = v_ref[pl.ds(off, BLK), :].astype(BF16)
                z = _dot(qm, kj, _NT) * scale
                lb, lr = _logsig_parts(z)
                msk = jnp.logical_or(tri, jj > 0)
                lr = jnp.where(msk, lr, 0.0)
                w = jnp.where(msk, jnp.exp(lb + _split_dot(lr, msuf) + rem), 0.0)
                e_ref[j] = w * _dot(dom, vj, _NT)
                sig_ref[j] = jnp.exp(lb)
                dv_ref[pl.ds(off, BLK), :] += _dot(w.astype(BF16), dom, _TN)
                return rem + jnp.sum(lr, axis=1, keepdims=True)

            lax.fori_loop(0, i + 1, weights, jnp.zeros((BLK, 1), F32))

            def grads(j, carry):
                pre, acc = carry
                off = pl.multiple_of(j * BLK, BLK)
                kj = k_ref[pl.ds(off, BLK), :].astype(BF16)
                e = e_ref[j]
                sig = sig_ref[j]
                before = _split_dot(e, mpre) + pre
                msk = jnp.logical_or(tri, j < i)
                dz = (jnp.where(msk, e * (1.0 - sig) - before * sig, 0.0) * scale).astype(BF16)
                dk_ref[pl.ds(off, BLK), :] += _dot(dz, qm, _TN)
                return pre + jnp.sum(e, axis=1, keepdims=True), acc + _dot(dz, kj)

            _, acc = lax.fori_loop(0, i + 1, grads, (jnp.zeros((BLK, 1), F32), jnp.zeros((BLK, LANES), F32)))
            dq = jnp.where(hm, acc, dq)
        dq_ref[...] = dq.astype(dq_ref.dtype)

    blk = lambda sec: pl.BlockSpec((None, BLK, LANES), lambda b, p, i: (b, i, sec * P + p))
    full = lambda sec: pl.BlockSpec((None, S, LANES), lambda b, p, i: (b, 0, sec * P + p))
    one = pl.BlockSpec((None, BLK, LANES), lambda b, p, i: (b, i, p))
    acc = pl.BlockSpec((None, S, LANES), lambda b, p, i: (b, 0, p))
    return pl.pallas_call(
        body, name="sb_bwd", grid=(B, P, NQ),
        in_specs=[blk(0), full(1), full(2), blk(3), one, one], out_specs=[one, one, acc, acc],
        out_shape=[jax.ShapeDtypeStruct((B, S, W), BF16), jax.ShapeDtypeStruct((B, S, W), BF16),
                   jax.ShapeDtypeStruct((B, S, W), F32), jax.ShapeDtypeStruct((B, S, W), F32)],
        scratch_shapes=[pltpu.VMEM((NQ, BLK, BLK), F32), pltpu.VMEM((NQ, BLK, BLK), F32)],
        compiler_params=_params(("parallel", "parallel", "arbitrary")),
    )(proj3, proj3, proj3, proj3, o, dy)


def _fox_gate_fwd(f_t, b_f):
    B, H, S = f_t.shape

    def body(f_ref, b_ref, c_ref):
        row = lax.broadcasted_iota(jnp.int32, (BLK, BLK), 0)
        col = lax.broadcasted_iota(jnp.int32, (BLK, BLK), 1)
        mpre = (row <= col).astype(BF16)
        carry = jnp.zeros((H, 1), F32)
        for n in range(S // BLK):
            sl = pl.ds(n * BLK, BLK)
            lf, _ = _logsig_parts(f_ref[:, sl] + b_ref[...])
            c_ref[:, sl] = _split3_dot(lf, mpre) + carry
            carry = carry + jnp.sum(lf, axis=1, keepdims=True)

    spec = pl.BlockSpec((None, H, S), lambda b: (b, 0, 0))
    return pl.pallas_call(
        body, name="fox_gate_fwd", grid=(B,),
        in_specs=[spec, pl.BlockSpec((H, 1), lambda b: (0, 0))], out_specs=spec,
        out_shape=jax.ShapeDtypeStruct((B, H, S), F32),
        compiler_params=_params(("parallel",)),
    )(f_t, b_f)


def _fox_gate_bwd(dcum_t, f_t, b_f):
    B, H, S = f_t.shape

    def body(d_ref, f_ref, b_ref, df_ref, db_ref):
        b = pl.program_id(0)

        @pl.when(b == 0)
        def _():
            db_ref[...] = jnp.zeros_like(db_ref)

        row = lax.broadcasted_iota(jnp.int32, (BLK, BLK), 0)
        col = lax.broadcasted_iota(jnp.int32, (BLK, BLK), 1)
        msuf = (row >= col).astype(BF16)
        carry = jnp.zeros((H, 1), F32)
        dbacc = jnp.zeros((H, 1), F32)
        for n in reversed(range(S // BLK)):
            sl = pl.ds(n * BLK, BLK)
            dv = d_ref[:, sl]
            dlf = _split3_dot(dv, msuf) + carry
            carry = carry + jnp.sum(dv, axis=1, keepdims=True)
            df = dlf * _sigmoid(-(f_ref[:, sl] + b_ref[...]))
            df_ref[:, sl] = df
            dbacc = dbacc + jnp.sum(df, axis=1, keepdims=True)
        db_ref[...] += dbacc

    spec = pl.BlockSpec((None, H, S), lambda b: (b, 0, 0))
    vec = pl.BlockSpec((H, 1), lambda b: (0, 0))
    return pl.pallas_call(
        body, name="fox_gate_bwd", grid=(B,),
        in_specs=[spec, spec, vec], out_specs=[spec, vec],
        out_shape=[jax.ShapeDtypeStruct((B, H, S), F32), jax.ShapeDtypeStruct((H, 1), F32)],
        compiler_params=_params(("arbitrary",)),
    )(dcum_t, f_t, b_f)


def _pick_col(block, idx, lane_iota):
    return jnp.sum(jnp.where(lane_iota == idx, block, 0.0), axis=1, keepdims=True)


def _pick_row(block, idx, sub_iota):
    return jnp.sum(jnp.where(sub_iota == idx, block, 0.0), axis=0, keepdims=True)


def _fox_fwd(proj3, cum_c, cum_t, W, heads):
    B, S, _ = proj3.shape
    H = heads
    Dh = W // heads
    hpb = LANES // Dh
    P, NQ = W // LANES, S // BLK
    scale = 1.0 / math.sqrt(Dh)

    def body(q_ref, k_ref, v_ref, g_ref, cc_ref, ct_ref, o_ref, y_ref, lse_ref):
        p = pl.program_id(1)
        i = pl.program_id(2)
        lane_head, row, col = _attn_consts(Dh)
        tri = col <= row
        lane_h = lax.broadcasted_iota(jnp.int32, (1, H), 1)
        sub_h = lax.broadcasted_iota(jnp.int32, (H, 1), 0)
        q = q_ref[...]
        cc = cc_ref[...]
        o = jnp.zeros((BLK, LANES), F32)
        lse = jnp.zeros((BLK, LANES), F32)
        for hh in range(hpb):
            h = p * hpb + hh
            hm = lane_head == hh
            qm = jnp.where(hm, q, 0.0).astype(BF16)
            c_q = _pick_col(cc, h, lane_h)

            def step(j, carry):
                mx, l, acc = carry
                off = pl.multiple_of(j * BLK, BLK)
                kj = k_ref[pl.ds(off, BLK), :].astype(BF16)
                vj = v_ref[pl.ds(off, BLK), :].astype(BF16)
                c_k = _pick_row(ct_ref[:, pl.ds(off, BLK)], h, sub_h)
                s = _dot(qm, kj, _NT) * scale + c_q - c_k
                s = jnp.where(jnp.logical_or(tri, j < i), s, NEG_BIG)
                mx2 = jnp.maximum(mx, jnp.max(s, axis=1, keepdims=True))
                pe = jnp.exp(s - mx2)
                alpha = jnp.exp(mx - mx2)
                l = alpha * l + jnp.sum(pe, axis=1, keepdims=True)
                acc = alpha * acc + _dot(pe.astype(BF16), vj)
                return mx2, l, acc

            mx, l, acc = lax.fori_loop(
                0, i + 1, step,
                (jnp.full((BLK, 1), NEG_BIG, F32), jnp.zeros((BLK, 1), F32), jnp.zeros((BLK, LANES), F32)))
            o = jnp.where(hm, acc / l, o)
            lse = jnp.where(hm, mx + jnp.log(l), lse)
        o_ref[...] = o
        lse_ref[...] = lse
        y_ref[...] = (o * _silu(g_ref[...])).astype(BF16)

    blk = lambda sec: pl.BlockSpec((None, BLK, LANES), lambda b, p, i: (b, i, sec * P + p))
    full = lambda sec: pl.BlockSpec((None, S, LANES), lambda b, p, i: (b, 0, sec * P + p))
    out = pl.BlockSpec((None, BLK, LANES), lambda b, p, i: (b, i, p))
    return pl.pallas_call(
        body, name="fox_fwd", grid=(B, P, NQ),
        in_specs=[blk(0), full(1), full(2), blk(3),
                  pl.BlockSpec((None, BLK, H), lambda b, p, i: (b, i, 0)),
                  pl.BlockSpec((None, H, S), lambda b, p, i: (b, 0, 0))],
        out_specs=[out, out, out],
        out_shape=[jax.ShapeDtypeStruct((B, S, W), F32), jax.ShapeDtypeStruct((B, S, W), BF16),
                   jax.ShapeDtypeStruct((B, S, W), F32)],
        compiler_params=_params(("parallel", "parallel", "arbitrary")),
    )(proj3, proj3, proj3, proj3, cum_c, cum_t)


def _fox_bwd(proj3, cum_c, cum_t, o, lse, dy, W, heads):
    B, S, _ = proj3.shape
    H = heads
    Dh = W // heads
    hpb = LANES // Dh
    P, NQ = W // LANES, S // BLK
    scale = 1.0 / math.sqrt(Dh)

    def body(q_ref, k_ref, v_ref, g_ref, cc_ref, ct_ref, o_ref, lse_ref, dy_ref,
             dq_ref, dg_ref, dk_ref, dv_ref, dc_ref, p_scr, dp_scr):
        p = pl.program_id(1)
        i = pl.program_id(2)

        @pl.when(i == 0)
        def _():
            dk_ref[...] = jnp.zeros_like(dk_ref)
            dv_ref[...] = jnp.zeros_like(dv_ref)
            dc_ref[...] = jnp.zeros_like(dc_ref)

        lane_head, row, col = _attn_consts(Dh)
        tri = col <= row
        lane_h = lax.broadcasted_iota(jnp.int32, (1, H), 1)
        sub_h = lax.broadcasted_iota(jnp.int32, (H, 1), 0)
        lane = lax.broadcasted_iota(jnp.int32, (1, LANES), 1)
        q = q_ref[...]
        g = g_ref[...]
        lsev = lse_ref[...]
        cc = cc_ref[...]
        dyv = dy_ref[...].astype(F32)
        do = dyv * _silu(g)
        dg_ref[...] = (dyv * o_ref[...] * _dsilu(g)).astype(dg_ref.dtype)
        dq = jnp.zeros((BLK, LANES), F32)
        for hh in range(hpb):
            h = p * hpb + hh
            hm = lane_head == hh
            qm = jnp.where(hm, q, 0.0).astype(BF16)
            dom = jnp.where(hm, do, 0.0).astype(BF16)
            lse_h = _pick_col(lsev, hh * Dh, lane)
            c_q = _pick_col(cc, h, lane_h)

            def probs(j, dsum):
                off = pl.multiple_of(j * BLK, BLK)
                kj = k_ref[pl.ds(off, BLK), :].astype(BF16)
                vj = v_ref[pl.ds(off, BLK), :].astype(BF16)
                c_k = _pick_row(ct_ref[:, pl.ds(off, BLK)], h, sub_h)
                s = _dot(qm, kj, _NT) * scale + c_q - c_k
                pr = jnp.where(jnp.logical_or(tri, j < i), jnp.exp(s - lse_h), 0.0)
                dp = _dot(dom, vj, _NT)
                p_scr[j] = pr
                dp_scr[j] = dp
                dv_ref[pl.ds(off, BLK), :] += _dot(pr.astype(BF16), dom, _TN)
                return dsum + jnp.sum(pr * dp, axis=1, keepdims=True)

            dsum = lax.fori_loop(0, i + 1, probs, jnp.zeros((BLK, 1), F32))

            def grads(j, acc):
                off = pl.multiple_of(j * BLK, BLK)
                kj = k_ref[pl.ds(off, BLK), :].astype(BF16)
                ds = p_scr[j] * (dp_scr[j] - dsum)
                dc_ref[hh:hh + 1, pl.ds(off, BLK)] -= jnp.sum(ds, axis=0, keepdims=True)
                dsb = (ds * scale).astype(BF16)
                dk_ref[pl.ds(off, BLK), :] += _dot(dsb, qm, _TN)
                return acc + _dot(dsb, kj)

            acc = lax.fori_loop(0, i + 1, grads, jnp.zeros((BLK, LANES), F32))
            dq = jnp.where(hm, acc, dq)
        dq_ref[...] = dq.astype(dq_ref.dtype)

    blk = lambda sec: pl.BlockSpec((None, BLK, LANES), lambda b, p, i: (b, i, sec * P + p))
    full = lambda sec: pl.BlockSpec((None, S, LANES), lambda b, p, i: (b, 0, sec * P + p))
    one = pl.BlockSpec((None, BLK, LANES), lambda b, p, i: (b, i, p))
    acc = pl.BlockSpec((None, S, LANES), lambda b, p, i: (b, 0, p))
    return pl.pallas_call(
        body, name="fox_bwd", grid=(B, P, NQ),
        in_specs=[blk(0), full(1), full(2), blk(3),
                  pl.BlockSpec((None, BLK, H), lambda b, p, i: (b, i, 0)),
                  pl.BlockSpec((None, H, S), lambda b, p, i: (b, 0, 0)),
                  one, one, one],
        out_specs=[one, one, acc, acc, pl.BlockSpec((None, None, hpb, S), lambda b, p, i: (b, p, 0, 0))],
        out_shape=[jax.ShapeDtypeStruct((B, S, W), BF16), jax.ShapeDtypeStruct((B, S, W), BF16),
                   jax.ShapeDtypeStruct((B, S, W), F32), jax.ShapeDtypeStruct((B, S, W), F32),
                   jax.ShapeDtypeStruct((B, P, hpb, S), F32)],
        scratch_shapes=[pltpu.VMEM((NQ, BLK, BLK), F32), pltpu.VMEM((NQ, BLK, BLK), F32)],
        compiler_params=_params(("parallel", "parallel", "arbitrary")),
    )(proj3, proj3, proj3, proj3, cum_c, cum_t, o, lse, dy)


def _layernorm_rows(v, gamma, beta):
    mu = jnp.mean(v, axis=-1, keepdims=True)
    xc = v - mu
    rstd = lax.rsqrt(jnp.mean(xc * xc, axis=-1, keepdims=True) + EPS)
    xh = xc * rstd
    return xh, rstd, xh * gamma + beta


def _layernorm_rows_bwd(dout, xh, rstd, gamma):
    dxh = dout * gamma
    return rstd * (dxh - jnp.mean(dxh, axis=-1, keepdims=True) - xh * jnp.mean(dxh * xh, axis=-1, keepdims=True))


def _gmlp_fwd(proj, wm, bs_t, ln_g, ln_b, W):
    T = proj.shape[0]
    G = wm.shape[0]
    cg = W // G
    assert cg == LANES

    def body(p_ref, wm_ref, bs_ref, lg_ref, lb_ref, y_ref, vn_ref):
        lane = lax.broadcasted_iota(jnp.int32, (1, LANES), 1)
        _, _, vn = _layernorm_rows(_gelu(p_ref[:, W:2 * W]), lg_ref[...], lb_ref[...])
        vn_ref[...] = vn.astype(BF16)
        bs = bs_ref[...]
        for g in range(G):
            sl = pl.ds(g * cg, cg)
            s = _dot(wm_ref[g], vn_ref[:, sl]) + _pick_col(bs, g, lane)
            gate = p_ref[:, pl.ds(2 * W + g * cg, cg)]
            y_ref[:, sl] = (_gelu(p_ref[:, sl]) * s * _silu(gate)).astype(BF16)

    vec = pl.BlockSpec((1, W), lambda r: (0, 0))
    return pl.pallas_call(
        body, name="gmlp_fwd", grid=(T // BLK,),
        in_specs=[pl.BlockSpec((BLK, 3 * W), lambda r: (r, 0)),
                  pl.BlockSpec((G, BLK, BLK), lambda r: (0, 0, 0)),
                  pl.BlockSpec((BLK, LANES), lambda r: (0, 0)), vec, vec],
        out_specs=pl.BlockSpec((BLK, W), lambda r: (r, 0)),
        out_shape=jax.ShapeDtypeStruct((T, W), BF16),
        scratch_shapes=[pltpu.VMEM((BLK, W), BF16)],
        compiler_params=_params(("parallel",)),
    )(proj, wm, bs_t, ln_g, ln_b)


def _gmlp_bwd(proj, dy, wm, bs_t, ln_g, ln_b, W):
    T = proj.shape[0]
    G = wm.shape[0]
    cg = W // G

    def body(p_ref, dy_ref, wm_ref, bs_ref, lg_ref, lb_ref,
             dp_ref, dwm_ref, dbs_ref, dlg_ref, dlb_ref, vn_ref, dvn_ref):
        r = pl.program_id(0)

        @pl.when(r == 0)
        def _():
            dwm_ref[...] = jnp.zeros_like(dwm_ref)
            dbs_ref[...] = jnp.zeros_like(dbs_ref)
            dlg_ref[...] = jnp.zeros_like(dlg_ref)
            dlb_ref[...] = jnp.zeros_like(dlb_ref)

        lane = lax.broadcasted_iota(jnp.int32, (1, LANES), 1)
        vpre = p_ref[:, W:2 * W]
        gamma = lg_ref[...]
        xh, rstd, vn = _layernorm_rows(_gelu(vpre), gamma, lb_ref[...])
        vn_ref[...] = vn.astype(BF16)
        bs = bs_ref[...]
        dbs = jnp.zeros((BLK, LANES), F32)
        for g in range(G):
            sl = pl.ds(g * cg, cg)
            gsl = pl.ds(2 * W + g * cg, cg)
            vng = vn_ref[:, sl]
            s = _dot(wm_ref[g], vng) + _pick_col(bs, g, lane)
            upre = p_ref[:, sl]
            u = _gelu(upre)
            gate = p_ref[:, gsl]
            dyv = dy_ref[:, sl].astype(F32)
            dp_ref[:, gsl] = (dyv * u * s * _dsilu(gate)).astype(dp_ref.dtype)
            do = dyv * _silu(gate)
            dp_ref[:, sl] = (do * s * _dgelu(upre)).astype(dp_ref.dtype)
            ds = do * u
            dbs = dbs + jnp.where(lane == g, jnp.sum(ds, axis=1, keepdims=True), 0.0)
            dsb = ds.astype(BF16)
            dwm_ref[g] += _dot(dsb, vng, _NT)
            dvn_ref[:, sl] = _dot(wm_ref[g], dsb, _TN)
        dbs_ref[...] += dbs
        dvn = dvn_ref[...]
        dlg_ref[...] += jnp.sum(dvn * xh, axis=0, keepdims=True)
        dlb_ref[...] += jnp.sum(dvn, axis=0, keepdims=True)
        dv = _layernorm_rows_bwd(dvn, xh, rstd, gamma)
        dp_ref[:, W:2 * W] = (dv * _dgelu(vpre)).astype(dp_ref.dtype)

    vec = pl.BlockSpec((1, W), lambda r: (0, 0))
    return pl.pallas_call(
        body, name="gmlp_bwd", grid=(T // BLK,),
        in_specs=[pl.BlockSpec((BLK, 3 * W), lambda r: (r, 0)),
                  pl.BlockSpec((BLK, W), lambda r: (r, 0)),
                  pl.BlockSpec((G, BLK, BLK), lambda r: (0, 0, 0)),
                  pl.BlockSpec((BLK, LANES), lambda r: (0, 0)), vec, vec],
        out_specs=[pl.BlockSpec((BLK, 3 * W), lambda r: (r, 0)),
                   pl.BlockSpec((G, BLK, BLK), lambda r: (0, 0, 0)),
                   pl.BlockSpec((BLK, LANES), lambda r: (0, 0)), vec, vec],
        out_shape=[jax.ShapeDtypeStruct((T, 3 * W), BF16), jax.ShapeDtypeStruct((G, BLK, BLK), F32),
                   jax.ShapeDtypeStruct((BLK, LANES), F32),
                   jax.ShapeDtypeStruct((1, W), F32), jax.ShapeDtypeStruct((1, W), F32)],
        scratch_shapes=[pltpu.VMEM((BLK, W), BF16), pltpu.VMEM((BLK, W), F32)],
        compiler_params=_params(("arbitrary",)),
    )(proj, dy, wm, bs_t, ln_g, ln_b)


def _conv_taps(ext_ref, cw_ref, off, n_taps, first):
    acc = jnp.zeros((BLK, LANES), F32)
    for k in range(n_taps):
        acc = acc + cw_ref[k:k + 1, pl.ds(off, LANES)] * ext_ref[pl.ds(first + k, BLK), pl.ds(off, LANES)]
    return acc


def _fill_glu_ext(ext_ref, halo_ref, cur_ref, W, first_block):
    y0h = halo_ref[:, :W] * _sigmoid(halo_ref[:, W:])
    ext_ref[0:CONV_HALO, :] = jnp.where(first_block, 0.0, y0h)
    ext_ref[CONV_HALO:CONV_HALO + BLK, :] = cur_ref[:, :W] * _sigmoid(cur_ref[:, W:])


def _conv_specs(S, W):
    per = BLK // CONV_HALO
    cur = pl.BlockSpec((None, BLK, 2 * W), lambda b, i: (b, i, 0))
    halo = pl.BlockSpec((None, CONV_HALO, 2 * W), lambda b, i: (b, jnp.maximum(i * per - 1, 0), 0))
    gate = pl.BlockSpec((None, BLK, W), lambda b, i: (b, i, 2))
    return cur, halo, gate


def _conv_fwd(proj3, cw, cb, ln_g, ln_b, W):
    B, S, _ = proj3.shape
    K = cw.shape[0]
    first = CONV_HALO - (K - 1)
    assert first >= 0

    def body(cur_ref, halo_ref, g_ref, cw_ref, cb_ref, lg_ref, lb_ref, y_ref, ext_ref, y1_ref):
        i = pl.program_id(1)
        _fill_glu_ext(ext_ref, halo_ref, cur_ref, W, i == 0)

        def chan(c, _):
            off = pl.multiple_of(c * LANES, LANES)
            y1_ref[:, pl.ds(off, LANES)] = _conv_taps(ext_ref, cw_ref, off, K, first) + cb_ref[:, pl.ds(off, LANES)]
            return 0

        lax.fori_loop(0, W // LANES, chan, 0)
        _, _, ln = _layernorm_rows(y1_ref[...], lg_ref[...], lb_ref[...])
        y_ref[...] = (_silu(ln) * _silu(g_ref[...])).astype(BF16)

    cur, halo, gate = _conv_specs(S, W)
    vec = pl.BlockSpec((1, W), lambda b, i: (0, 0))
    return pl.pallas_call(
        body, name="conv_fwd", grid=(B, S // BLK),
        in_specs=[cur, halo, gate, pl.BlockSpec((K, W), lambda b, i: (0, 0)), vec, vec, vec],
        out_specs=pl.BlockSpec((None, BLK, W), lambda b, i: (b, i, 0)),
        out_shape=jax.ShapeDtypeStruct((B, S, W), BF16),
        scratch_shapes=[pltpu.VMEM((CONV_HALO + BLK, W), F32), pltpu.VMEM((BLK, W), F32)],
        compiler_params=_params(("parallel", "parallel")),
    )(proj3, proj3, proj3, cw, cb, ln_g, ln_b)


def _conv_bwd1(proj3, dy, cw, cb, ln_g, ln_b, W):
    B, S, _ = proj3.shape
    K = cw.shape[0]
    first = CONV_HALO - (K - 1)

    def body(cur_ref, halo_ref, g_ref, dy_ref, cw_ref, cb_ref, lg_ref, lb_ref,
             dy1_ref, dg_ref, dcw_ref, dcb_ref, dlg_ref, dlb_ref, ext_ref, y1_ref):
        b = pl.program_id(0)
        i = pl.program_id(1)

        @pl.when(jnp.logical_and(b == 0, i == 0))
        def _():
            dcw_ref[...] = jnp.zeros_like(dcw_ref)
            dcb_ref[...] = jnp.zeros_like(dcb_ref)
            dlg_ref[...] = jnp.zeros_like(dlg_ref)
            dlb_ref[...] = jnp.zeros_like(dlb_ref)

        _fill_glu_ext(ext_ref, halo_ref, cur_ref, W, i == 0)

        def chan(c, _):
            off = pl.multiple_of(c * LANES, LANES)
            y1_ref[:, pl.ds(off, LANES)] = _conv_taps(ext_ref, cw_ref, off, K, first) + cb_ref[:, pl.ds(off, LANES)]
            return 0

        lax.fori_loop(0, W // LANES, chan, 0)
        gamma = lg_ref[...]
        xh, rstd, ln = _layernorm_rows(y1_ref[...], gamma, lb_ref[...])
        g = g_ref[...]
        dyv = dy_ref[...].astype(F32)
        dg_ref[...] = (dyv * _silu(ln) * _dsilu(g)).astype(dg_ref.dtype)
        dln = dyv * _silu(g) * _dsilu(ln)
        dlg_ref[...] += jnp.sum(dln * xh, axis=0, keepdims=True)
        dlb_ref[...] += jnp.sum(dln, axis=0, keepdims=True)
        dy1 = _layernorm_rows_bwd(dln, xh, rstd, gamma)
        dy1_ref[...] = dy1
        dcb_ref[...] += jnp.sum(dy1, axis=0, keepdims=True)

        def chan_w(c, _):
            off = pl.multiple_of(c * LANES, LANES)
            d = dy1_ref[:, pl.ds(off, LANES)]
            for k in range(K):
                dcw_ref[k:k + 1, pl.ds(off, LANES)] += jnp.sum(
                    d * ext_ref[pl.ds(first + k, BLK), pl.ds(off, LANES)], axis=0, keepdims=True)
            return 0

        lax.fori_loop(0, W // LANES, chan_w, 0)

    cur, halo, gate = _conv_specs(S, W)
    vec = pl.BlockSpec((1, W), lambda b, i: (0, 0))
    taps = pl.BlockSpec((K, W), lambda b, i: (0, 0))
    one = pl.BlockSpec((None, BLK, W), lambda b, i: (b, i, 0))
    return pl.pallas_call(
        body, name="conv_bwd1", grid=(B, S // BLK),
        in_specs=[cur, halo, gate, one, taps, vec, vec, vec],
        out_specs=[one, one, taps, vec, vec, vec],
        out_shape=[jax.ShapeDtypeStruct((B, S, W), F32), jax.ShapeDtypeStruct((B, S, W), BF16),
                   jax.ShapeDtypeStruct((K, W), F32)] + [jax.ShapeDtypeStruct((1, W), F32)] * 3,
        scratch_shapes=[pltpu.VMEM((CONV_HALO + BLK, W), F32), pltpu.VMEM((BLK, W), F32)],
        compiler_params=_params(("arbitrary", "arbitrary")),
    )(proj3, proj3, proj3, dy, cw, cb, ln_g, ln_b)


def _conv_bwd2(proj3, dy1, dgate, cw_rev, W):
    B, S, _ = proj3.shape
    K = cw_rev.shape[0]
    NQ = S // BLK
    per = BLK // CONV_HALO

    def body(cur_ref, d_ref, dnext_ref, dgate_ref, cw_ref, dp_ref, ext_ref, dy0_ref):
        i = pl.program_id(1)
        ext_ref[0:BLK, :] = d_ref[...]
        ext_ref[BLK:BLK + CONV_HALO, :] = jnp.where(i == NQ - 1, 0.0, dnext_ref[...])

        def chan(c, _):
            off = pl.multiple_of(c * LANES, LANES)
            dy0_ref[:, pl.ds(off, LANES)] = _conv_taps(ext_ref, cw_ref, off, K, 0)
            return 0

        lax.fori_loop(0, W // LANES, chan, 0)
        a = cur_ref[:, :W]
        sg = _sigmoid(cur_ref[:, W:])
        dy0 = dy0_ref[...]
        dp_ref[:, 0:W] = (dy0 * sg).astype(dp_ref.dtype)
        dp_ref[:, W:2 * W] = (dy0 * a * sg * (1.0 - sg)).astype(dp_ref.dtype)
        dp_ref[:, 2 * W:3 * W] = dgate_ref[...]

    cur = pl.BlockSpec((None, BLK, 2 * W), lambda b, i: (b, i, 0))
    one = pl.BlockSpec((None, BLK, W), lambda b, i: (b, i, 0))
    nxt = pl.BlockSpec((None, CONV_HALO, W), lambda b, i: (b, jnp.minimum((i + 1) * per, S // CONV_HALO - 1), 0))
    return pl.pallas_call(
        body, name="conv_bwd2", grid=(B, NQ),
        in_specs=[cur, one, nxt, one, pl.BlockSpec((K, W), lambda b, i: (0, 0))],
        out_specs=pl.BlockSpec((None, BLK, 3 * W), lambda b, i: (b, i, 0)),
        out_shape=jax.ShapeDtypeStruct((B, S, 3 * W), BF16),
        scratch_shapes=[pltpu.VMEM((BLK + CONV_HALO, W), F32), pltpu.VMEM((BLK, W), F32)],
        compiler_params=_params(("parallel", "parallel")),
    )(proj3, dy1, dy1, dgate, cw_rev)


def _pack(arrays):
    flat = jnp.concatenate([a.astype(F32).reshape(-1) for a in arrays])
    n = flat.shape[0]
    pad = (-n) % (8 * LANES)
    if pad:
        flat = jnp.concatenate([flat, jnp.zeros((pad,), F32)])
    return flat.reshape(-1, LANES)


def _unpack(packed, shapes, lead=()):
    flat = packed.reshape(lead + (-1,))
    out, off = [], 0
    for shp in shapes:
        n = math.prod(shp)
        out.append(flat[..., off:off + n].reshape(lead + tuple(shp)))
        off += n
    return out


def _cols_from_dev(g):
    g = jnp.moveaxis(g, 0, -2)
    return g.reshape(g.shape[:-2] + (g.shape[-2] * g.shape[-1],))


def _my_cols(full, me):
    n8 = full.shape[-1] // N_DEV
    return lax.dynamic_slice_in_dim(full, me * n8, n8, axis=full.ndim - 1)


def kernel(x, a_norm, a_w_in, a_w_out, b_norm, b_w_in, b_v_ln_g, b_v_ln_b, b_w_s, b_b_s, b_w_out, c_norm, c_w_in, c_conv_w, c_conv_b, c_ln_g, c_ln_b, c_w_out, d_norm, d_w_in, d_b_f, d_w_out, final_norm, loss_target, m_a_norm, m_a_w_in, m_a_w_out, m_b_norm, m_b_w_in, m_b_v_ln_g, m_b_v_ln_b, m_b_w_s, m_b_b_s, m_b_w_out, m_c_norm, m_c_w_in, m_c_conv_w, m_c_conv_b, m_c_ln_g, m_c_ln_b, m_c_w_out, m_d_norm, m_d_w_in, m_d_b_f, m_d_w_out, m_final_norm, v_a_norm, v_a_w_in, v_a_w_out, v_b_norm, v_b_w_in, v_b_v_ln_g, v_b_v_ln_b, v_b_w_s, v_b_b_s, v_b_w_out, v_c_norm, v_c_w_in, v_c_conv_w, v_c_conv_b, v_c_ln_g, v_c_ln_b, v_c_w_out, v_d_norm, v_d_w_in, v_d_b_f, v_d_w_out, v_final_norm):
    B, S, D = x.shape
    T = B * S
    xi, yi, ci = _me()
    me = 4 * xi + 2 * yi + ci

    G = b_w_s.shape[1]
    KC = c_conv_w.shape[1]
    H_D = d_b_f.shape[1]
    W_A = a_w_out.shape[1] * N_DEV
    W_B = b_w_out.shape[1] * N_DEV
    W_C = c_w_out.shape[1] * N_DEV
    W_D = d_w_out.shape[1] * N_DEV
    N_D = d_w_in.shape[2] * N_DEV
    N_D_PAD = -(-N_D // (3 * LANES)) * (3 * LANES)

    big_names = ["a_w_in", "a_w_out", "b_w_in", "b_w_out", "c_w_in", "c_w_out", "d_w_in", "d_w_out"]
    big_w = dict(a_w_in=a_w_in[0], a_w_out=a_w_out[0], b_w_in=b_w_in[0], b_w_out=b_w_out[0],
                 c_w_in=c_w_in[0], c_w_out=c_w_out[0], d_w_in=d_w_in[0], d_w_out=d_w_out[0])
    small_sharded = [b_norm, b_v_ln_g, b_v_ln_b, c_norm, c_conv_w, c_conv_b, c_ln_g, c_ln_b, d_norm]
    gathered = _exchange([big_w[n].astype(BF16) for n in big_names] + [_pack(small_sharded)],
                         ["gather"] * (len(big_names) + 1), "gather_weights")
    wg = dict(zip(big_names, gathered[:-1]))
    (b_norm_f, b_lg_f, b_lb_f, c_norm_f, c_cw_f, c_cb_f, c_lg_f, c_lb_f, d_norm_f) = [
        _cols_from_dev(t) for t in _unpack(gathered[-1], [s.shape for s in small_sharded], lead=(N_DEV,))]
    c_cw_f = c_cw_f[0]
    a_w_out_f = wg["a_w_out"].reshape(W_A, D)
    b_w_out_f = wg["b_w_out"].reshape(W_B, D)
    c_w_out_f = wg["c_w_out"].reshape(W_C, D)
    d_w_out_f = wg["d_w_out"].reshape(W_D, D)
    d_w_in_f = jnp.pad(_cols_from_dev(wg["d_w_in"]), ((0, 0), (0, N_D_PAD - N_D)))

    wm = jnp.tril(b_w_s[0]).astype(BF16)
    bs_t = jnp.pad(b_b_s[0].T, ((0, 0), (0, LANES - G)))

    x0 = x.reshape(T, D)
    h_a = _rmsnorm_fwd(x0, a_norm, "rms_a")
    proj_a = _mm_w_dev(h_a, wg["a_w_in"], "proj_a").reshape(B, S, 4 * W_A)
    o_a, y_a = _sb_fwd(proj_a, W_A, SB_HEADS)
    y_a = y_a.reshape(T, W_A)
    x1 = x0 + _mm(y_a, a_w_out_f, "nn", T, D, W_A, F32, "out_a", 512, D, W_A)
    h_b = _rmsnorm_fwd(x1, b_norm_f, "rms_b")
    proj_b = _mm_w_dev(h_b, wg["b_w_in"], "proj_b")
    y_b = _gmlp_fwd(proj_b, wm, bs_t, b_lg_f, b_lb_f, W_B)
    x2 = x1 + _mm(y_b, b_w_out_f, "nn", T, D, W_B, F32, "out_b", 512, D, W_B)
    h_c = _rmsnorm_fwd(x2, c_norm_f, "rms_c")
    proj_c = _mm_w_dev(h_c, wg["c_w_in"], "proj_c").reshape(B, S, 3 * W_C)
    y_c = _conv_fwd(proj_c, c_cw_f, c_cb_f, c_lg_f, c_lb_f, W_C).reshape(T, W_C)
    x3 = x2 + _mm(y_c, c_w_out_f, "nn", T, D, W_C, F32, "out_c", 512, D, W_C)
    h_d = _rmsnorm_fwd(x3, d_norm_f, "rms_d")
    proj_d = _mm(h_d, d_w_in_f, "nn", T, N_D_PAD, D, F32, "proj_d", 512, 384, D).reshape(B, S, N_D_PAD)
    f_t = jnp.swapaxes(proj_d[:, :, 4 * W_D:4 * W_D + H_D], 1, 2)
    b_f_col = d_b_f.reshape(H_D, 1)
    cum_t = _fox_gate_fwd(f_t, b_f_col)
    cum_c = jnp.swapaxes(cum_t, 1, 2)
    o_d, y_d, lse_d = _fox_fwd(proj_d, cum_c, cum_t, W_D, H_D)
    y_d = y_d.reshape(T, W_D)
    x4 = x3 + _mm(y_d, d_w_out_f, "nn", T, D, W_D, F32, "out_d", 512, D, W_D)

    loss_part, dx, g_final = _loss_head(x4, final_norm.reshape(1, D), loss_target.reshape(T, D))
    loss = lax.psum(loss_part[0, 0], MESH_AXES)

    dy_d = _mm(dx, d_w_out_f, "nt", T, W_D, D, BF16, "dy_d", 512, W_D, D).reshape(B, S, W_D)
    gw_d_out = _mm(y_d, dx, "tn", W_D, D, T, BF16, "gw_d_out", W_D, D, 512).reshape(N_DEV, W_D // N_DEV, D)
    dq, dg, dk, dv, dcum = _fox_bwd(proj_d, cum_c, cum_t, o_d, lse_d, dy_d, W_D, H_D)
    df_t, g_b_f = _fox_gate_bwd(dcum.reshape(B, H_D, S), f_t, b_f_col)
    dproj_d = jnp.concatenate(
        [dq, dk.astype(BF16), dv.astype(BF16), dg, jnp.swapaxes(df_t, 1, 2).astype(BF16),
         jnp.zeros((B, S, N_D_PAD - N_D), BF16)], axis=-1).reshape(T, N_D_PAD)
    gw_d_in_full = _mm(h_d, dproj_d, "tn", D, N_D_PAD, T, BF16, "gw_d_in", D, 384, 512)
    gw_d_in = jnp.moveaxis(gw_d_in_full[:, :N_D].reshape(D, N_DEV, N_D // N_DEV), 1, 0)
    dh = _mm(dproj_d, d_w_in_f, "nt", T, D, N_D_PAD, F32, "dh_d", 512, D, 384)
    dx, g_d_norm = _rmsnorm_bwd(x3, d_norm_f, dh, dx, "rms_bwd_d")

    dy_c = _mm(dx, c_w_out_f, "nt", T, W_C, D, BF16, "dy_c", 512, W_C, D).reshape(B, S, W_C)
    gw_c_out = _mm(y_c, dx, "tn", W_C, D, T, BF16, "gw_c_out", 1024, D, 512).reshape(N_DEV, W_C // N_DEV, D)
    dy1, dgate_c, g_c_cw, g_c_cb, g_c_lg, g_c_lb = _conv_bwd1(proj_c, dy_c, c_cw_f, c_cb_f, c_lg_f, c_lb_f, W_C)
    dproj_c = _conv_bwd2(proj_c, dy1, dgate_c, c_cw_f[::-1], W_C).reshape(T, 3 * W_C)
    gw_c_in = _mm_grad_dev(h_c, dproj_c, "gw_c_in")
    dh = _mm_wT_dev(dproj_c, wg["c_w_in"], "dh_c")
    dx, g_c_norm = _rmsnorm_bwd(x2, c_norm_f, dh, dx, "rms_bwd_c")

    dy_b = _mm(dx, b_w_out_f, "nt", T, W_B, D, BF16, "dy_b", 512, W_B, D)
    gw_b_out = _mm(y_b, dx, "tn", W_B, D, T, BF16, "gw_b_out", 1024, D, 512).reshape(N_DEV, W_B // N_DEV, D)
    dproj_b, g_wm, g_bs_t, g_b_lg, g_b_lb = _gmlp_bwd(proj_b, dy_b, wm, bs_t, b_lg_f, b_lb_f, W_B)
    g_b_w_s = jnp.tril(g_wm)
    g_b_b_s = g_bs_t[:, :G].T
    gw_b_in = _mm_grad_dev(h_b, dproj_b, "gw_b_in")
    dh = _mm_wT_dev(dproj_b, wg["b_w_in"], "dh_b")
    dx, g_b_norm = _rmsnorm_bwd(x1, b_norm_f, dh, dx, "rms_bwd_b")

    dy_a = _mm(dx, a_w_out_f, "nt", T, W_A, D, BF16, "dy_a", 512, W_A, D).reshape(B, S, W_A)
    gw_a_out = _mm(y_a, dx, "tn", W_A, D, T, BF16, "gw_a_out", W_A, D, 512).reshape(N_DEV, W_A // N_DEV, D)
    dq, dg, dk, dv = _sb_bwd(proj_a, o_a, dy_a, W_A, SB_HEADS)
    dproj_a = jnp.concatenate([dq, dk.astype(BF16), dv.astype(BF16), dg], axis=-1).reshape(T, 4 * W_A)
    gw_a_in = _mm_grad_dev(h_a, dproj_a, "gw_a_in")
    dh = _mm_wT_dev(dproj_a, wg["a_w_in"], "dh_a")
    dx, g_a_norm = _rmsnorm_bwd(x0, a_norm, dh, dx, "rms_bwd_a")
    grad_x = dx.reshape(B, S, D)

    big_g = dict(a_w_in=gw_a_in, a_w_out=gw_a_out, b_w_in=gw_b_in, b_w_out=gw_b_out,
                 c_w_in=gw_c_in, c_w_out=gw_c_out, d_w_in=gw_d_in, d_w_out=gw_d_out)
    small_full = [g_a_norm, g_b_norm, g_b_lg, g_b_lb, g_b_w_s, g_b_b_s, g_c_norm, g_c_cw, g_c_cb, g_c_lg, g_c_lb,
                  g_d_norm, g_b_f, g_final]
    parts = _exchange([big_g[n] for n in big_names] + [_pack(small_full)],
                      ["scatter"] * len(big_names) + ["gather"], "exchange_grads")
    small_sum = _sum_parts(parts[-1], "sum_small")
    (s_a_norm, s_b_norm, s_b_lg, s_b_lb, s_b_w_s, s_b_b_s, s_c_norm, s_c_cw, s_c_cb, s_c_lg, s_c_lb,
     s_d_norm, s_b_f, s_final) = _unpack(small_sum, [g.shape for g in small_full])

    weights = dict(a_norm=a_norm, a_w_in=a_w_in, a_w_out=a_w_out, b_norm=b_norm, b_w_in=b_w_in, b_v_ln_g=b_v_ln_g,
                   b_v_ln_b=b_v_ln_b, b_w_s=b_w_s, b_b_s=b_b_s, b_w_out=b_w_out, c_norm=c_norm, c_w_in=c_w_in,
                   c_conv_w=c_conv_w, c_conv_b=c_conv_b, c_ln_g=c_ln_g, c_ln_b=c_ln_b, c_w_out=c_w_out,
                   d_norm=d_norm, d_w_in=d_w_in, d_b_f=d_b_f, d_w_out=d_w_out, final_norm=final_norm)
    mom_m = dict(a_norm=m_a_norm, a_w_in=m_a_w_in, a_w_out=m_a_w_out, b_norm=m_b_norm, b_w_in=m_b_w_in,
                 b_v_ln_g=m_b_v_ln_g, b_v_ln_b=m_b_v_ln_b, b_w_s=m_b_w_s, b_b_s=m_b_b_s, b_w_out=m_b_w_out,
                 c_norm=m_c_norm, c_w_in=m_c_w_in, c_conv_w=m_c_conv_w, c_conv_b=m_c_conv_b, c_ln_g=m_c_ln_g,
                 c_ln_b=m_c_ln_b, c_w_out=m_c_w_out, d_norm=m_d_norm, d_w_in=m_d_w_in, d_b_f=m_d_b_f,
                 d_w_out=m_d_w_out, final_norm=m_final_norm)
    mom_v = dict(a_norm=v_a_norm, a_w_in=v_a_w_in, a_w_out=v_a_w_out, b_norm=v_b_norm, b_w_in=v_b_w_in,
                 b_v_ln_g=v_b_v_ln_g, b_v_ln_b=v_b_v_ln_b, b_w_s=v_b_w_s, b_b_s=v_b_b_s, b_w_out=v_b_w_out,
                 c_norm=v_c_norm, c_w_in=v_c_w_in, c_conv_w=v_c_conv_w, c_conv_b=v_c_conv_b, c_ln_g=v_c_ln_g,
                 c_ln_b=v_c_ln_b, c_w_out=v_c_w_out, d_norm=v_d_norm, d_w_in=v_d_w_in, d_b_f=v_d_b_f,
                 d_w_out=v_d_w_out, final_norm=v_final_norm)
    order = list(weights)
    grads, deltas, new_m, new_v = {}, {}, {}, {}

    for n, part in zip(big_names, parts[:-1]):
        shp = weights[n].shape
        R, C = shp[1], shp[2]
        res = _adamw(part, weights[n].reshape(R, C), mom_m[n].reshape(R, C), mom_v[n].reshape(R, C), "adamw_" + n)
        grads[n], deltas[n], new_m[n], new_v[n] = [r.reshape(shp) for r in res]

    small_g = dict(
        a_norm=s_a_norm, b_norm=_my_cols(s_b_norm, me), b_v_ln_g=_my_cols(s_b_lg, me),
        b_v_ln_b=_my_cols(s_b_lb, me), b_w_s=s_b_w_s[None], b_b_s=s_b_b_s[None], c_norm=_my_cols(s_c_norm, me),
        c_conv_w=_my_cols(s_c_cw, me)[None], c_conv_b=_my_cols(s_c_cb, me), c_ln_g=_my_cols(s_c_lg, me),
        c_ln_b=_my_cols(s_c_lb, me), d_norm=_my_cols(s_d_norm, me), d_b_f=s_b_f.reshape(1, H_D),
        final_norm=s_final.reshape(D))
    small_names = list(small_g)
    sg_p = _pack([small_g[n] for n in small_names])
    res = _adamw(sg_p[None], _pack([weights[n] for n in small_names]), _pack([mom_m[n] for n in small_names]),
                 _pack([mom_v[n] for n in small_names]), "adamw_small")
    shapes = [weights[n].shape for n in small_names]
    for dst, r in zip((grads, deltas, new_m, new_v), res):
        for n, val in zip(small_names, _unpack(r, shapes)):
            dst[n] = val

    return (loss, grad_x, *[grads[n] for n in order], *[deltas[n] for n in order],
            *[new_m[n] for n in order], *[new_v[n] for n in order])
```

```python
import functools
import math

import jax
import jax.numpy as jnp
from jax import lax
from jax.experimental import pallas as pl
from jax.experimental.pallas import tpu as pltpu

F32 = jnp.float32
BF16 = jnp.bfloat16

EPS = 1e-6
SB_HEADS = 16
CONV_HALO = 32
BLK = 128
ATT_TK = 256
ATT_TQ = 256
LANES = 128
N_DEV = 8
MESH_AXES = ("x", "y", "c")

ADAM_LR = 0.001
ADAM_B1 = 0.9
ADAM_B2 = 0.999
ADAM_EPS = 1e-08
ADAM_WD = 0.01
ADAM_STEP = 10

VMEM_LIMIT = 56 * 1024 * 1024
NEG_BIG = -1e30

_NN = (((1,), (0,)), ((), ()))
_NT = (((1,), (1,)), ((), ()))
_TN = (((0,), (0,)), ((), ()))


def _dot(a, b, dims=_NN):
    return lax.dot_general(a, b, dims, preferred_element_type=F32)


def _split_dot(x, m):
    hi = x.astype(BF16)
    lo = (x - hi.astype(F32)).astype(BF16)
    return _dot(hi, m) + _dot(lo, m)


def _split3_dot(x, m):
    hi = x.astype(BF16)
    r1 = x - hi.astype(F32)
    mid = r1.astype(BF16)
    lo = (r1 - mid.astype(F32)).astype(BF16)
    return _dot(hi, m) + _dot(mid, m) + _dot(lo, m)


def _params(sem=None):
    kw = dict(vmem_limit_bytes=VMEM_LIMIT)
    if sem is not None:
        kw["dimension_semantics"] = sem
    return pltpu.CompilerParams(**kw)


def _sigmoid(x):
    return jax.nn.sigmoid(x)


def _silu(x):
    return x * _sigmoid(x)


def _dsilu(x):
    s = _sigmoid(x)
    return s * (1.0 + x * (1.0 - s))


_GELU_C = math.sqrt(2.0 / math.pi)


def _gelu(x):
    return 0.5 * x * (1.0 + jnp.tanh(_GELU_C * (x + 0.044715 * x * x * x)))


def _dgelu(x):
    th = jnp.tanh(_GELU_C * (x + 0.044715 * x * x * x))
    return 0.5 * (1.0 + th) + 0.5 * x * (1.0 - th * th) * _GELU_C * (1.0 + 3.0 * 0.044715 * x * x)


def _mm(a, b, mode, M, N, K, out_dtype, name, tm, tn, tk, a_spec=None, b_spec=None, o_spec=None, out_shape=None):
    tm, tn, tk = min(tm, M), min(tn, N), min(tk, K)
    assert M % tm == 0 and N % tn == 0 and K % tk == 0, (name, M, N, K, tm, tn, tk)
    nk = K // tk
    dims = {"nn": _NN, "nt": _NT, "tn": _TN}[mode]
    if a_spec is None:
        a_spec = (pl.BlockSpec((tk, tm), lambda i, j, k: (k, i)) if mode == "tn"
                  else pl.BlockSpec((tm, tk), lambda i, j, k: (i, k)))
    if b_spec is None:
        b_spec = (pl.BlockSpec((tn, tk), lambda i, j, k: (j, k)) if mode == "nt"
                  else pl.BlockSpec((tk, tn), lambda i, j, k: (k, j)))
    if o_spec is None:
        o_spec = pl.BlockSpec((tm, tn), lambda i, j, k: (i, j))
    if out_shape is None:
        out_shape = (M, N)

    def body(a_ref, b_ref, o_ref, acc_ref):
        k = pl.program_id(2)

        @pl.when(k == 0)
        def _():
            acc_ref[...] = jnp.zeros_like(acc_ref)

        acc_ref[...] += _dot(a_ref[...].astype(BF16), b_ref[...].astype(BF16), dims)

        @pl.when(k == nk - 1)
        def _():
            o_ref[...] = acc_ref[...].astype(o_ref.dtype)

    return pl.pallas_call(
        body, name=name, grid=(M // tm, N // tn, nk),
        in_specs=[a_spec, b_spec], out_specs=o_spec,
        out_shape=jax.ShapeDtypeStruct(out_shape, out_dtype),
        scratch_shapes=[pltpu.VMEM((tm, tn), F32)],
        compiler_params=_params(("parallel", "parallel", "arbitrary")),
    )(a, b)


def _mm_w_dev(a, w3, name, out_dtype=F32, tm=512):
    M, K = a.shape
    n8 = w3.shape[2]
    tn = n8 if n8 <= 768 else 512
    per = n8 // tn
    b_spec = pl.BlockSpec((None, K, tn), lambda i, j, k: (j // per, 0, j % per))
    return _mm(a, w3, "nn", M, N_DEV * n8, K, out_dtype, name, tm, tn, K, b_spec=b_spec)


def _mm_wT_dev(a, w3, name, out_dtype=F32, tm=512):
    M, N = a.shape
    K, n8 = w3.shape[1], w3.shape[2]
    tk = 256 if n8 % 256 == 0 else n8
    per = n8 // tk
    b_spec = pl.BlockSpec((None, K, tk), lambda i, j, k: (k // per, 0, k % per))
    return _mm(a, w3, "nt", M, K, N, out_dtype, name, tm, K, tk, b_spec=b_spec)


def _mm_grad_dev(h, d, name, out_dtype=BF16):
    T, M = h.shape
    N = d.shape[1]
    n8 = N // N_DEV
    tn = n8 if n8 <= 768 else 512
    per = n8 // tn
    tm = min(M, 1024)
    o_spec = pl.BlockSpec((None, tm, tn), lambda i, j, k: (j // per, i, j % per))
    return _mm(h, d, "tn", M, N, T, out_dtype, name, tm, tn, 512, o_spec=o_spec, out_shape=(N_DEV, M, n8))


def _me():
    x, y, c = lax.axis_index("x"), lax.axis_index("y"), lax.axis_index("c")
    return x, y, c


def _peer(r):
    x, y, c = _me()
    px = 1 - x if (r >> 2) & 1 else x
    py = 1 - y if (r >> 1) & 1 else y
    pc = 1 - c if r & 1 else c
    return (px, py, pc), 4 * px + 2 * py + pc


def _exchange(arrays, kinds, name):
    n = len(arrays)
    out_shapes = []
    for a, kind in zip(arrays, kinds):
        shp = (N_DEV,) + a.shape if kind == "gather" else a.shape
        out_shapes.append(jax.ShapeDtypeStruct(shp, a.dtype))

    def body(*refs):
        ins, outs = refs[:n], refs[n:2 * n]
        send_sems, recv_sems, local_sems = refs[2 * n:]
        x, y, c = _me()
        me = 4 * x + 2 * y + c

        def src(k, pid):
            return ins[k] if kinds[k] == "gather" else ins[k].at[pid]

        local = [pltpu.make_async_copy(src(k, me), outs[k].at[me], local_sems.at[k]) for k in range(n)]
        for cp in local:
            cp.start()
        copies = []
        for r in range(1, N_DEV):
            peer, pid = _peer(r)
            for k in range(n):
                cp = pltpu.make_async_remote_copy(
                    src_ref=src(k, pid), dst_ref=outs[k].at[me],
                    send_sem=send_sems.at[k, r - 1], recv_sem=recv_sems.at[k, r - 1],
                    device_id=peer, device_id_type=pl.DeviceIdType.MESH)
                cp.start()
                copies.append(cp)
        for r in range(1, N_DEV):
            peer, pid = _peer(r)
            for k in range(n):
                pltpu.make_async_remote_copy(
                    src_ref=src(k, pid), dst_ref=outs[k].at[pid],
                    send_sem=send_sems.at[k, r - 1], recv_sem=recv_sems.at[k, r - 1],
                    device_id=peer, device_id_type=pl.DeviceIdType.MESH).wait_recv()
        for cp in copies:
            cp.wait_send()
        for cp in local:
            cp.wait()

    any_spec = pl.BlockSpec(memory_space=pl.ANY)
    return pl.pallas_call(
        body, name=name,
        in_specs=[any_spec] * n, out_specs=[any_spec] * n, out_shape=out_shapes,
        scratch_shapes=[pltpu.SemaphoreType.DMA((n, N_DEV - 1)), pltpu.SemaphoreType.DMA((n, N_DEV - 1)),
                        pltpu.SemaphoreType.DMA((n,))],
    )(*arrays)


def _rmsnorm_fwd(x, g, name):
    T, D = x.shape
    tr = min(256, T)

    def body(x_ref, g_ref, h_ref):
        xv = x_ref[...]
        r = lax.rsqrt(jnp.mean(xv * xv, axis=-1, keepdims=True) + EPS)
        h_ref[...] = (xv * r * g_ref[...]).astype(BF16)

    return pl.pallas_call(
        body, name=name, grid=(T // tr,),
        in_specs=[pl.BlockSpec((tr, D), lambda i: (i, 0)), pl.BlockSpec((1, D), lambda i: (0, 0))],
        out_specs=pl.BlockSpec((tr, D), lambda i: (i, 0)),
        out_shape=jax.ShapeDtypeStruct((T, D), BF16),
        compiler_params=_params(("parallel",)),
    )(x, g)


def _rmsnorm_bwd(x, g, dh, dres, name):
    T, D = x.shape
    tr = min(256, T)

    def body(x_ref, g_ref, dh_ref, dres_ref, dx_ref, dg_ref):
        i = pl.program_id(0)
        xv = x_ref[...]
        r = lax.rsqrt(jnp.mean(xv * xv, axis=-1, keepdims=True) + EPS)
        xh = xv * r
        dhv = dh_ref[...]
        dxh = dhv * g_ref[...]
        dx_ref[...] = dres_ref[...] + r * (dxh - xh * jnp.mean(dxh * xh, axis=-1, keepdims=True))

        @pl.when(i == 0)
        def _():
            dg_ref[...] = jnp.zeros_like(dg_ref)

        dg_ref[...] += jnp.sum(dhv * xh, axis=0, keepdims=True)

    row = pl.BlockSpec((tr, D), lambda i: (i, 0))
    vec = pl.BlockSpec((1, D), lambda i: (0, 0))
    return pl.pallas_call(
        body, name=name, grid=(T // tr,),
        in_specs=[row, vec, row, row], out_specs=[row, vec],
        out_shape=[jax.ShapeDtypeStruct((T, D), F32), jax.ShapeDtypeStruct((1, D), F32)],
        compiler_params=_params(("arbitrary",)),
    )(x, g, dh, dres)


def _loss_head(x, g, target):
    T, D = x.shape
    tr = min(256, T)

    def body(x_ref, g_ref, t_ref, loss_ref, dx_ref, dg_ref):
        i = pl.program_id(0)
        xv = x_ref[...]
        gv = g_ref[...]
        r = lax.rsqrt(jnp.mean(xv * xv, axis=-1, keepdims=True) + EPS)
        xh = xv * r
        diff = xh * gv - t_ref[...]
        dy = diff * (1.0 / D)
        dxh = dy * gv
        dx_ref[...] = r * (dxh - xh * jnp.mean(dxh * xh, axis=-1, keepdims=True))

        @pl.when(i == 0)
        def _():
            dg_ref[...] = jnp.zeros_like(dg_ref)
            loss_ref[...] = jnp.zeros_like(loss_ref)

        dg_ref[...] += jnp.sum(dy * xh, axis=0, keepdims=True)
        part = jnp.sum(jnp.sum(diff * diff, axis=1, keepdims=True), axis=0, keepdims=True)
        loss_ref[...] += (0.5 / D) * part

    row = pl.BlockSpec((tr, D), lambda i: (i, 0))
    vec = pl.BlockSpec((1, D), lambda i: (0, 0))
    return pl.pallas_call(
        body, name="loss_head", grid=(T // tr,),
        in_specs=[row, vec, row],
        out_specs=[pl.BlockSpec((1, 1), lambda i: (0, 0)), row, vec],
        out_shape=[jax.ShapeDtypeStruct((1, 1), F32), jax.ShapeDtypeStruct((T, D), F32),
                   jax.ShapeDtypeStruct((1, D), F32)],
        compiler_params=_params(("arbitrary",)),
    )(x, g, target)


def _adamw(parts, w, m, v, name):
    P, R, C = parts.shape
    tr = R
    for cand in (128, 64, 32, 16, 8):
        if R % cand == 0:
            tr = cand
            break

    def body(p_ref, w_ref, m_ref, v_ref, g_out, d_out, m_out, v_out):
        g = p_ref[0].astype(F32)
        for p in range(1, P):
            g = g + p_ref[p].astype(F32)
        wv = w_ref[...]
        mn = ADAM_B1 * m_ref[...] + (1.0 - ADAM_B1) * g
        vn = ADAM_B2 * v_ref[...] + (1.0 - ADAM_B2) * (g * g)
        m_hat = mn / (1.0 - ADAM_B1 ** ADAM_STEP)
        v_hat = vn / (1.0 - ADAM_B2 ** ADAM_STEP)
        g_out[...] = g
        d_out[...] = -ADAM_LR * (m_hat / (jnp.sqrt(v_hat) + ADAM_EPS) + ADAM_WD * wv)
        m_out[...] = mn
        v_out[...] = vn

    row = pl.BlockSpec((tr, C), lambda i: (i, 0))
    return pl.pallas_call(
        body, name=name, grid=(R // tr,),
        in_specs=[pl.BlockSpec((P, tr, C), lambda i: (0, i, 0)), row, row, row],
        out_specs=[row, row, row, row],
        out_shape=[jax.ShapeDtypeStruct((R, C), F32)] * 4,
        compiler_params=_params(("parallel",)),
    )(parts, w, m, v)


def _sum_parts(parts, name):
    P, R, C = parts.shape
    tr = 128 if R % 128 == 0 else R

    def body(p_ref, o_ref):
        g = p_ref[0]
        for p in range(1, P):
            g = g + p_ref[p]
        o_ref[...] = g

    return pl.pallas_call(
        body, name=name, grid=(R // tr,),
        in_specs=[pl.BlockSpec((P, tr, C), lambda i: (0, i, 0))],
        out_specs=pl.BlockSpec((tr, C), lambda i: (i, 0)),
        out_shape=jax.ShapeDtypeStruct((R, C), F32),
        compiler_params=_params(("parallel",)),
    )(parts)


def _lane_head(Dh):
    assert Dh & (Dh - 1) == 0 and Dh <= LANES
    return lax.shift_right_logical(lax.broadcasted_iota(jnp.int32, (1, LANES), 1), Dh.bit_length() - 1)


def _stack_heads(x, lane_head, hpb):
    return jnp.concatenate([jnp.where(lane_head == h, x, 0.0) for h in range(hpb)], axis=0)


def _unstack_heads(acc, lane_head, hpb):
    TQ = acc.shape[0] // hpb
    out = acc[0:TQ]
    for h in range(1, hpb):
        out = jnp.where(lane_head == h, acc[h * TQ:(h + 1) * TQ], out)
    return out


def _key_tile(S):
    return ATT_TK if S % ATT_TK == 0 else BLK


def _query_tile(S):
    return ATT_TQ if S % ATT_TQ == 0 else BLK


def _causal_iotas(RS, TK, TQ):
    assert TQ & (TQ - 1) == 0 and TK % TQ == 0
    trow = jnp.bitwise_and(lax.broadcasted_iota(jnp.int32, (RS, TK), 0), TQ - 1)
    col = lax.broadcasted_iota(jnp.int32, (RS, TK), 1)
    return trow, col


def _tri(TK, op):
    r = lax.broadcasted_iota(jnp.int32, (TK, TK), 0)
    c = lax.broadcasted_iota(jnp.int32, (TK, TK), 1)
    return op(r, c).astype(BF16)


def _logsig_parts(z):
    sp = jnp.log(1.0 + jnp.exp(-jnp.abs(z)))
    return jnp.minimum(z, 0.0) - sp, -jnp.maximum(z, 0.0) - sp


def _sb_fwd(proj3, W, heads):
    B, S, _ = proj3.shape
    Dh = W // heads
    hpb = LANES // Dh
    P, TQ = W // LANES, _query_tile(S)
    NQ = S // TQ
    scale = 1.0 / math.sqrt(Dh)

    TK = _key_tile(S)
    RS = hpb * TQ

    def body(q_ref, k_ref, v_ref, g_ref, o_ref, y_ref):
        i = pl.program_id(2)
        lane_head = _lane_head(Dh)
        trow, col = _causal_iotas(RS, TK, TQ)
        msuf = _tri(TK, lambda r, c: r > c)
        qs = _stack_heads(q_ref[...], lane_head, hpb).astype(BF16)
        nt = (i * TQ) // TK + 1

        def step(jj, carry):
            rem, acc = carry
            jt = nt - 1 - jj
            off = pl.multiple_of(jt * TK, TK)
            kj = k_ref[pl.ds(off, TK), :].astype(BF16)
            vj = v_ref[pl.ds(off, TK), :].astype(BF16)
            z = _dot(qs, kj, _NT) * scale
            lb, lr = _logsig_parts(z)
            msk = col + (jt * TK - i * TQ) < trow
            lr = jnp.where(msk, lr, 0.0)
            w = jnp.where(msk, jnp.exp(lb + _split_dot(lr, msuf) + rem), 0.0)
            return rem + jnp.sum(lr, axis=1, keepdims=True), acc + _dot(w.astype(BF16), vj)

        _, acc = lax.fori_loop(0, nt, step, (jnp.zeros((RS, 1), F32), jnp.zeros((RS, LANES), F32)))
        o = _unstack_heads(acc, lane_head, hpb)
        o_ref[...] = o
        y_ref[...] = (o * _silu(g_ref[...])).astype(BF16)

    blk = lambda sec: pl.BlockSpec((None, TQ, LANES), lambda b, p, i: (b, i, sec * P + p))
    full = lambda sec: pl.BlockSpec((None, S, LANES), lambda b, p, i: (b, 0, sec * P + p))
    out = pl.BlockSpec((None, TQ, LANES), lambda b, p, i: (b, i, p))
    return pl.pallas_call(
        body, name="sb_fwd", grid=(B, P, NQ),
        in_specs=[blk(0), full(1), full(2), blk(3)], out_specs=[out, out],
        out_shape=[jax.ShapeDtypeStruct((B, S, W), F32), jax.ShapeDtypeStruct((B, S, W), BF16)],
        compiler_params=_params(("parallel", "parallel", "arbitrary")),
    )(proj3, proj3, proj3, proj3)


def _sb_bwd(proj3, o, dy, W, heads):
    B, S, _ = proj3.shape
    Dh = W // heads
    hpb = LANES // Dh
    P, TQ = W // LANES, _query_tile(S)
    NQ = S // TQ
    scale = 1.0 / math.sqrt(Dh)

    TK = _key_tile(S)
    RS = hpb * TQ

    def body(q_ref, k_ref, v_ref, g_ref, o_ref, dy_ref, dq_ref, dg_ref, dk_ref, dv_ref, e_ref, sig_ref):
        i = pl.program_id(2)

        @pl.when(i == 0)
        def _():
            dk_ref[...] = jnp.zeros_like(dk_ref)
            dv_ref[...] = jnp.zeros_like(dv_ref)

        lane_head = _lane_head(Dh)
        trow, col = _causal_iotas(RS, TK, TQ)
        msuf = _tri(TK, lambda r, c: r > c)
        mpre = _tri(TK, lambda r, c: r < c)
        g = g_ref[...]
        dyv = dy_ref[...].astype(F32)
        dg_ref[...] = (dyv * o_ref[...] * _dsilu(g)).astype(dg_ref.dtype)
        qs = _stack_heads(q_ref[...], lane_head, hpb).astype(BF16)
        dos = _stack_heads(dyv * _silu(g), lane_head, hpb).astype(BF16)
        nt = (i * TQ) // TK + 1

        def weights(jj, rem):
            jt = nt - 1 - jj
            off = pl.multiple_of(jt * TK, TK)
            kj = k_ref[pl.ds(off, TK), :].astype(BF16)
            vj = v_ref[pl.ds(off, TK), :].astype(BF16)
            z = _dot(qs, kj, _NT) * scale
            lb, lr = _logsig_parts(z)
            msk = col + (jt * TK - i * TQ) < trow
            lr = jnp.where(msk, lr, 0.0)
            w = jnp.where(msk, jnp.exp(lb + _split_dot(lr, msuf) + rem), 0.0)
            e_ref[jt] = w * _dot(dos, vj, _NT)
            sig_ref[jt] = jnp.exp(lb)
            dv_ref[pl.ds(off, TK), :] += _dot(w.astype(BF16), dos, _TN)
            return rem + jnp.sum(lr, axis=1, keepdims=True)

        lax.fori_loop(0, nt, weights, jnp.zeros((RS, 1), F32))

        def grads(jt, carry):
            pre, acc = carry
            off = pl.multiple_of(jt * TK, TK)
            kj = k_ref[pl.ds(off, TK), :].astype(BF16)
            e = e_ref[jt]
            sig = sig_ref[jt]
            before = _split_dot(e, mpre) + pre
            msk = col + (jt * TK - i * TQ) < trow
            dz = (jnp.where(msk, e * (1.0 - sig) - before * sig, 0.0) * scale).astype(BF16)
            dk_ref[pl.ds(off, TK), :] += _dot(dz, qs, _TN)
            return pre + jnp.sum(e, axis=1, keepdims=True), acc + _dot(dz, kj)

        _, acc = lax.fori_loop(0, nt, grads, (jnp.zeros((RS, 1), F32), jnp.zeros((RS, LANES), F32)))
        dq_ref[...] = _unstack_heads(acc, lane_head, hpb).astype(dq_ref.dtype)

    blk = lambda sec: pl.BlockSpec((None, TQ, LANES), lambda b, p, i: (b, i, sec * P + p))
    full = lambda sec: pl.BlockSpec((None, S, LANES), lambda b, p, i: (b, 0, sec * P + p))
    one = pl.BlockSpec((None, TQ, LANES), lambda b, p, i: (b, i, p))
    acc = pl.BlockSpec((None, S, LANES), lambda b, p, i: (b, 0, p))
    return pl.pallas_call(
        body, name="sb_bwd", grid=(B, P, NQ),
        in_specs=[blk(0), full(1), full(2), blk(3), one, one], out_specs=[one, one, acc, acc],
        out_shape=[jax.ShapeDtypeStruct((B, S, W), BF16), jax.ShapeDtypeStruct((B, S, W), BF16),
                   jax.ShapeDtypeStruct((B, S, W), F32), jax.ShapeDtypeStruct((B, S, W), F32)],
        scratch_shapes=[pltpu.VMEM((S // TK, RS, TK), F32), pltpu.VMEM((S // TK, RS, TK), F32)],
        compiler_params=_params(("parallel", "parallel", "arbitrary")),
    )(proj3, proj3, proj3, proj3, o, dy)


def _fox_gate_fwd(f_t, b_f):
    B, H, S = f_t.shape

    def body(f_ref, b_ref, c_ref):
        row = lax.broadcasted_iota(jnp.int32, (BLK, BLK), 0)
        col = lax.broadcasted_iota(jnp.int32, (BLK, BLK), 1)
        mpre = (row <= col).astype(BF16)
        carry = jnp.zeros((H, 1), F32)
        for n in range(S // BLK):
            sl = pl.ds(n * BLK, BLK)
            lf, _ = _logsig_parts(f_ref[:, sl] + b_ref[...])
            c_ref[:, sl] = _split3_dot(lf, mpre) + carry
            carry = carry + jnp.sum(lf, axis=1, keepdims=True)

    spec = pl.BlockSpec((None, H, S), lambda b: (b, 0, 0))
    return pl.pallas_call(
        body, name="fox_gate_fwd", grid=(B,),
        in_specs=[spec, pl.BlockSpec((H, 1), lambda b: (0, 0))], out_specs=spec,
        out_shape=jax.ShapeDtypeStruct((B, H, S), F32),
        compiler_params=_params(("parallel",)),
    )(f_t, b_f)


def _fox_gate_bwd(dcum_t, f_t, b_f):
    B, H, S = f_t.shape

    def body(d_ref, f_ref, b_ref, df_ref, db_ref):
        b = pl.program_id(0)

        @pl.when(b == 0)
        def _():
            db_ref[...] = jnp.zeros_like(db_ref)

        row = lax.broadcasted_iota(jnp.int32, (BLK, BLK), 0)
        col = lax.broadcasted_iota(jnp.int32, (BLK, BLK), 1)
        msuf = (row >= col).astype(BF16)
        carry = jnp.zeros((H, 1), F32)
        dbacc = jnp.zeros((H, 1), F32)
        for n in reversed(range(S // BLK)):
            sl = pl.ds(n * BLK, BLK)
            dv = d_ref[:, sl]
            dlf = _split3_dot(dv, msuf) + carry
            carry = carry + jnp.sum(dv, axis=1, keepdims=True)
            df = dlf * _sigmoid(-(f_ref[:, sl] + b_ref[...]))
            df_ref[:, sl] = df
            dbacc = dbacc + jnp.sum(df, axis=1, keepdims=True)
        db_ref[...] += dbacc

    spec = pl.BlockSpec((None, H, S), lambda b: (b, 0, 0))
    vec = pl.BlockSpec((H, 1), lambda b: (0, 0))
    return pl.pallas_call(
        body, name="fox_gate_bwd", grid=(B,),
        in_specs=[spec, spec, vec], out_specs=[spec, vec],
        out_shape=[jax.ShapeDtypeStruct((B, H, S), F32), jax.ShapeDtypeStruct((H, 1), F32)],
        compiler_params=_params(("arbitrary",)),
    )(dcum_t, f_t, b_f)


def _pick_col(block, idx, lane_iota):
    return jnp.sum(jnp.where(lane_iota == idx, block, 0.0), axis=1, keepdims=True)


def _pick_row(block, idx, sub_iota):
    return jnp.sum(jnp.where(sub_iota == idx, block, 0.0), axis=0, keepdims=True)


def _fox_fwd(proj3, cum_c, cum_t, W, heads):
    B, S, _ = proj3.shape
    H = heads
    Dh = W // heads
    hpb = LANES // Dh
    P, TQ = W // LANES, _query_tile(S)
    NQ = S // TQ
    scale = 1.0 / math.sqrt(Dh)

    TK = _key_tile(S)
    RS = hpb * TQ

    def body(q_ref, k_ref, v_ref, g_ref, cc_ref, ct_ref, o_ref, y_ref, lse_ref):
        p = pl.program_id(1)
        i = pl.program_id(2)
        lane_head = _lane_head(Dh)
        trow, col = _causal_iotas(RS, TK, TQ)
        lane_h = lax.broadcasted_iota(jnp.int32, (1, H), 1)
        sub_h = lax.broadcasted_iota(jnp.int32, (H, 1), 0)
        qs = _stack_heads(q_ref[...], lane_head, hpb).astype(BF16)
        cc = cc_ref[...]
        c_q = jnp.concatenate([_pick_col(cc, p * hpb + h, lane_h) for h in range(hpb)], axis=0)
        nt = (i * TQ + TQ - 1) // TK + 1

        def step(jt, carry):
            mx, l, acc = carry
            off = pl.multiple_of(jt * TK, TK)
            kj = k_ref[pl.ds(off, TK), :].astype(BF16)
            vj = v_ref[pl.ds(off, TK), :].astype(BF16)
            ctb = ct_ref[:, pl.ds(off, TK)]
            z = _dot(qs, kj, _NT) * scale + c_q
            s = jnp.concatenate([z[h * TQ:(h + 1) * TQ] - _pick_row(ctb, p * hpb + h, sub_h) for h in range(hpb)],
                                axis=0)
            s = jnp.where(col + (jt * TK - i * TQ) <= trow, s, NEG_BIG)
            mx2 = jnp.maximum(mx, jnp.max(s, axis=1, keepdims=True))
            pe = jnp.exp(s - mx2)
            alpha = jnp.exp(mx - mx2)
            return (mx2, alpha * l + jnp.sum(pe, axis=1, keepdims=True), alpha * acc + _dot(pe.astype(BF16), vj))

        mx, l, acc = lax.fori_loop(
            0, nt, step, (jnp.full((RS, 1), NEG_BIG, F32), jnp.zeros((RS, 1), F32), jnp.zeros((RS, LANES), F32)))
        o = _unstack_heads(acc / l, lane_head, hpb)
        o_ref[...] = o
        lse_ref[...] = _unstack_heads(jnp.broadcast_to(mx + jnp.log(l), (RS, LANES)), lane_head, hpb)
        y_ref[...] = (o * _silu(g_ref[...])).astype(BF16)

    blk = lambda sec: pl.BlockSpec((None, TQ, LANES), lambda b, p, i: (b, i, sec * P + p))
    full = lambda sec: pl.BlockSpec((None, S, LANES), lambda b, p, i: (b, 0, sec * P + p))
    out = pl.BlockSpec((None, TQ, LANES), lambda b, p, i: (b, i, p))
    return pl.pallas_call(
        body, name="fox_fwd", grid=(B, P, NQ),
        in_specs=[blk(0), full(1), full(2), blk(3),
                  pl.BlockSpec((None, TQ, H), lambda b, p, i: (b, i, 0)),
                  pl.BlockSpec((None, H, S), lambda b, p, i: (b, 0, 0))],
        out_specs=[out, out, out],
        out_shape=[jax.ShapeDtypeStruct((B, S, W), F32), jax.ShapeDtypeStruct((B, S, W), BF16),
                   jax.ShapeDtypeStruct((B, S, W), F32)],
        compiler_params=_params(("parallel", "parallel", "arbitrary")),
    )(proj3, proj3, proj3, proj3, cum_c, cum_t)


def _fox_bwd(proj3, cum_c, cum_t, o, lse, dy, W, heads):
    B, S, _ = proj3.shape
    H = heads
    Dh = W // heads
    hpb = LANES // Dh
    P, TQ = W // LANES, _query_tile(S)
    NQ = S // TQ
    scale = 1.0 / math.sqrt(Dh)

    TK = _key_tile(S)
    RS = hpb * TQ

    def body(q_ref, k_ref, v_ref, g_ref, cc_ref, ct_ref, o_ref, lse_ref, dy_ref,
             dq_ref, dg_ref, dk_ref, dv_ref, dc_ref, p_scr, dp_scr):
        p = pl.program_id(1)
        i = pl.program_id(2)

        @pl.when(i == 0)
        def _():
            dk_ref[...] = jnp.zeros_like(dk_ref)
            dv_ref[...] = jnp.zeros_like(dv_ref)
            dc_ref[...] = jnp.zeros_like(dc_ref)

        lane_head = _lane_head(Dh)
        trow, col = _causal_iotas(RS, TK, TQ)
        lane_h = lax.broadcasted_iota(jnp.int32, (1, H), 1)
        sub_h = lax.broadcasted_iota(jnp.int32, (H, 1), 0)
        lane = lax.broadcasted_iota(jnp.int32, (1, LANES), 1)
        g = g_ref[...]
        lsev = lse_ref[...]
        cc = cc_ref[...]
        dyv = dy_ref[...].astype(F32)
        dg_ref[...] = (dyv * o_ref[...] * _dsilu(g)).astype(dg_ref.dtype)
        qs = _stack_heads(q_ref[...], lane_head, hpb).astype(BF16)
        dos = _stack_heads(dyv * _silu(g), lane_head, hpb).astype(BF16)
        c_q = jnp.concatenate([_pick_col(cc, p * hpb + h, lane_h) for h in range(hpb)], axis=0)
        c_q = c_q - jnp.concatenate([_pick_col(lsev, h * Dh, lane) for h in range(hpb)], axis=0)
        nt = (i * TQ + TQ - 1) // TK + 1

        def probs(jt, dsum):
            off = pl.multiple_of(jt * TK, TK)
            kj = k_ref[pl.ds(off, TK), :].astype(BF16)
            vj = v_ref[pl.ds(off, TK), :].astype(BF16)
            ctb = ct_ref[:, pl.ds(off, TK)]
            z = _dot(qs, kj, _NT) * scale + c_q
            s = jnp.concatenate([z[h * TQ:(h + 1) * TQ] - _pick_row(ctb, p * hpb + h, sub_h) for h in range(hpb)],
                                axis=0)
            pr = jnp.where(col + (jt * TK - i * TQ) <= trow, jnp.exp(s), 0.0)
            dp = _dot(dos, vj, _NT)
            p_scr[jt] = pr
            dp_scr[jt] = dp
            dv_ref[pl.ds(off, TK), :] += _dot(pr.astype(BF16), dos, _TN)
            return dsum + jnp.sum(pr * dp, axis=1, keepdims=True)

        dsum = lax.fori_loop(0, nt, probs, jnp.zeros((RS, 1), F32))

        def grads(jt, acc):
            off = pl.multiple_of(jt * TK, TK)
            kj = k_ref[pl.ds(off, TK), :].astype(BF16)
            ds = p_scr[jt] * (dp_scr[jt] - dsum)
            for h in range(hpb):
                dc_ref[h:h + 1, pl.ds(off, TK)] -= jnp.sum(ds[h * TQ:(h + 1) * TQ], axis=0, keepdims=True)
            dsb = (ds * scale).astype(BF16)
            dk_ref[pl.ds(off, TK), :] += _dot(dsb, qs, _TN)
            return acc + _dot(dsb, kj)

        acc = lax.fori_loop(0, nt, grads, jnp.zeros((RS, LANES), F32))
        dq_ref[...] = _unstack_heads(acc, lane_head, hpb).astype(dq_ref.dtype)

    blk = lambda sec: pl.BlockSpec((None, TQ, LANES), lambda b, p, i: (b, i, sec * P + p))
    full = lambda sec: pl.BlockSpec((None, S, LANES), lambda b, p, i: (b, 0, sec * P + p))
    one = pl.BlockSpec((None, TQ, LANES), lambda b, p, i: (b, i, p))
    acc = pl.BlockSpec((None, S, LANES), lambda b, p, i: (b, 0, p))
    return pl.pallas_call(
        body, name="fox_bwd", grid=(B, P, NQ),
        in_specs=[blk(0), full(1), full(2), blk(3),
                  pl.BlockSpec((None, TQ, H), lambda b, p, i: (b, i, 0)),
                  pl.BlockSpec((None, H, S), lambda b, p, i: (b, 0, 0)),
                  one, one, one],
        out_specs=[one, one, acc, acc, pl.BlockSpec((None, None, hpb, S), lambda b, p, i: (b, p, 0, 0))],
        out_shape=[jax.ShapeDtypeStruct((B, S, W), BF16), jax.ShapeDtypeStruct((B, S, W), BF16),
                   jax.ShapeDtypeStruct((B, S, W), F32), jax.ShapeDtypeStruct((B, S, W), F32),
                   jax.ShapeDtypeStruct((B, P, hpb, S), F32)],
        scratch_shapes=[pltpu.VMEM((S // TK, RS, TK), F32), pltpu.VMEM((S // TK, RS, TK), F32)],
        compiler_params=_params(("parallel", "parallel", "arbitrary")),
    )(proj3, proj3, proj3, proj3, cum_c, cum_t, o, lse, dy)


def _layernorm_rows(v, gamma, beta):
    mu = jnp.mean(v, axis=-1, keepdims=True)
    xc = v - mu
    rstd = lax.rsqrt(jnp.mean(xc * xc, axis=-1, keepdims=True) + EPS)
    xh = xc * rstd
    return xh, rstd, xh * gamma + beta


def _layernorm_rows_bwd(dout, xh, rstd, gamma):
    dxh = dout * gamma
    return rstd * (dxh - jnp.mean(dxh, axis=-1, keepdims=True) - xh * jnp.mean(dxh * xh, axis=-1, keepdims=True))


def _gmlp_fwd(proj, wm, bs_t, ln_g, ln_b, W):
    T = proj.shape[0]
    G = wm.shape[0]
    cg = W // G
    assert cg == LANES

    def body(p_ref, wm_ref, bs_ref, lg_ref, lb_ref, y_ref, vn_ref):
        lane = lax.broadcasted_iota(jnp.int32, (1, LANES), 1)
        _, _, vn = _layernorm_rows(_gelu(p_ref[:, W:2 * W]), lg_ref[...], lb_ref[...])
        vn_ref[...] = vn.astype(BF16)
        bs = bs_ref[...]
        for g in range(G):
            sl = pl.ds(g * cg, cg)
            s = _dot(wm_ref[g], vn_ref[:, sl]) + _pick_col(bs, g, lane)
            gate = p_ref[:, pl.ds(2 * W + g * cg, cg)]
            y_ref[:, sl] = (_gelu(p_ref[:, sl]) * s * _silu(gate)).astype(BF16)

    vec = pl.BlockSpec((1, W), lambda r: (0, 0))
    return pl.pallas_call(
        body, name="gmlp_fwd", grid=(T // BLK,),
        in_specs=[pl.BlockSpec((BLK, 3 * W), lambda r: (r, 0)),
                  pl.BlockSpec((G, BLK, BLK), lambda r: (0, 0, 0)),
                  pl.BlockSpec((BLK, LANES), lambda r: (0, 0)), vec, vec],
        out_specs=pl.BlockSpec((BLK, W), lambda r: (r, 0)),
        out_shape=jax.ShapeDtypeStruct((T, W), BF16),
        scratch_shapes=[pltpu.VMEM((BLK, W), BF16)],
        compiler_params=_params(("parallel",)),
    )(proj, wm, bs_t, ln_g, ln_b)


def _gmlp_bwd(proj, dy, wm, bs_t, ln_g, ln_b, W):
    T = proj.shape[0]
    G = wm.shape[0]
    cg = W // G

    def body(p_ref, dy_ref, wm_ref, bs_ref, lg_ref, lb_ref,
             dp_ref, dwm_ref, dbs_ref, dlg_ref, dlb_ref, vn_ref, dvn_ref):
        r = pl.program_id(0)

        @pl.when(r == 0)
        def _():
            dwm_ref[...] = jnp.zeros_like(dwm_ref)
            dbs_ref[...] = jnp.zeros_like(dbs_ref)
            dlg_ref[...] = jnp.zeros_like(dlg_ref)
            dlb_ref[...] = jnp.zeros_like(dlb_ref)

        lane = lax.broadcasted_iota(jnp.int32, (1, LANES), 1)
        vpre = p_ref[:, W:2 * W]
        gamma = lg_ref[...]
        xh, rstd, vn = _layernorm_rows(_gelu(vpre), gamma, lb_ref[...])
        vn_ref[...] = vn.astype(BF16)
        bs = bs_ref[...]
        dbs = jnp.zeros((BLK, LANES), F32)
        for g in range(G):
            sl = pl.ds(g * cg, cg)
            gsl = pl.ds(2 * W + g * cg, cg)
            vng = vn_ref[:, sl]
            s = _dot(wm_ref[g], vng) + _pick_col(bs, g, lane)
            upre = p_ref[:, sl]
            u = _gelu(upre)
            gate = p_ref[:, gsl]
            dyv = dy_ref[:, sl].astype(F32)
            dp_ref[:, gsl] = (dyv * u * s * _dsilu(gate)).astype(dp_ref.dtype)
            do = dyv * _silu(gate)
            dp_ref[:, sl] = (do * s * _dgelu(upre)).astype(dp_ref.dtype)
            ds = do * u
            dbs = dbs + jnp.where(lane == g, jnp.sum(ds, axis=1, keepdims=True), 0.0)
            dsb = ds.astype(BF16)
            dwm_ref[g] += _dot(dsb, vng, _NT)
            dvn_ref[:, sl] = _dot(wm_ref[g], dsb, _TN)
        dbs_ref[...] += dbs
        dvn = dvn_ref[...]
        dlg_ref[...] += jnp.sum(dvn * xh, axis=0, keepdims=True)
        dlb_ref[...] += jnp.sum(dvn, axis=0, keepdims=True)
        dv = _layernorm_rows_bwd(dvn, xh, rstd, gamma)
        dp_ref[:, W:2 * W] = (dv * _dgelu(vpre)).astype(dp_ref.dtype)

    vec = pl.BlockSpec((1, W), lambda r: (0, 0))
    return pl.pallas_call(
        body, name="gmlp_bwd", grid=(T // BLK,),
        in_specs=[pl.BlockSpec((BLK, 3 * W), lambda r: (r, 0)),
                  pl.BlockSpec((BLK, W), lambda r: (r, 0)),
                  pl.BlockSpec((G, BLK, BLK), lambda r: (0, 0, 0)),
                  pl.BlockSpec((BLK, LANES), lambda r: (0, 0)), vec, vec],
        out_specs=[pl.BlockSpec((BLK, 3 * W), lambda r: (r, 0)),
                   pl.BlockSpec((G, BLK, BLK), lambda r: (0, 0, 0)),
                   pl.BlockSpec((BLK, LANES), lambda r: (0, 0)), vec, vec],
        out_shape=[jax.ShapeDtypeStruct((T, 3 * W), BF16), jax.ShapeDtypeStruct((G, BLK, BLK), F32),
                   jax.ShapeDtypeStruct((BLK, LANES), F32),
                   jax.ShapeDtypeStruct((1, W), F32), jax.ShapeDtypeStruct((1, W), F32)],
        scratch_shapes=[pltpu.VMEM((BLK, W), BF16), pltpu.VMEM((BLK, W), F32)],
        compiler_params=_params(("arbitrary",)),
    )(proj, dy, wm, bs_t, ln_g, ln_b)


def _conv_taps(ext_ref, cw_ref, off, n_taps, first):
    acc = jnp.zeros((BLK, LANES), F32)
    for k in range(n_taps):
        acc = acc + cw_ref[k:k + 1, pl.ds(off, LANES)] * ext_ref[pl.ds(first + k, BLK), pl.ds(off, LANES)]
    return acc


def _fill_glu_ext(ext_ref, halo_ref, cur_ref, W, first_block):
    y0h = halo_ref[:, :W] * _sigmoid(halo_ref[:, W:])
    ext_ref[0:CONV_HALO, :] = jnp.where(first_block, 0.0, y0h)
    ext_ref[CONV_HALO:CONV_HALO + BLK, :] = cur_ref[:, :W] * _sigmoid(cur_ref[:, W:])


def _conv_specs(S, W):
    per = BLK // CONV_HALO
    cur = pl.BlockSpec((None, BLK, 2 * W), lambda b, i: (b, i, 0))
    halo = pl.BlockSpec((None, CONV_HALO, 2 * W), lambda b, i: (b, jnp.maximum(i * per - 1, 0), 0))
    gate = pl.BlockSpec((None, BLK, W), lambda b, i: (b, i, 2))
    return cur, halo, gate


def _conv_fwd(proj3, cw, cb, ln_g, ln_b, W):
    B, S, _ = proj3.shape
    K = cw.shape[0]
    first = CONV_HALO - (K - 1)
    assert first >= 0

    def body(cur_ref, halo_ref, g_ref, cw_ref, cb_ref, lg_ref, lb_ref, y_ref, ext_ref, y1_ref):
        i = pl.program_id(1)
        _fill_glu_ext(ext_ref, halo_ref, cur_ref, W, i == 0)

        def chan(c, _):
            off = pl.multiple_of(c * LANES, LANES)
            y1_ref[:, pl.ds(off, LANES)] = _conv_taps(ext_ref, cw_ref, off, K, first) + cb_ref[:, pl.ds(off, LANES)]
            return 0

        lax.fori_loop(0, W // LANES, chan, 0)
        _, _, ln = _layernorm_rows(y1_ref[...], lg_ref[...], lb_ref[...])
        y_ref[...] = (_silu(ln) * _silu(g_ref[...])).astype(BF16)

    cur, halo, gate = _conv_specs(S, W)
    vec = pl.BlockSpec((1, W), lambda b, i: (0, 0))
    return pl.pallas_call(
        body, name="conv_fwd", grid=(B, S // BLK),
        in_specs=[cur, halo, gate, pl.BlockSpec((K, W), lambda b, i: (0, 0)), vec, vec, vec],
        out_specs=pl.BlockSpec((None, BLK, W), lambda b, i: (b, i, 0)),
        out_shape=jax.ShapeDtypeStruct((B, S, W), BF16),
        scratch_shapes=[pltpu.VMEM((CONV_HALO + BLK, W), F32), pltpu.VMEM((BLK, W), F32)],
        compiler_params=_params(("parallel", "parallel")),
    )(proj3, proj3, proj3, cw, cb, ln_g, ln_b)


def _conv_bwd1(proj3, dy, cw, cb, ln_g, ln_b, W):
    B, S, _ = proj3.shape
    K = cw.shape[0]
    first = CONV_HALO - (K - 1)

    def body(cur_ref, halo_ref, g_ref, dy_ref, cw_ref, cb_ref, lg_ref, lb_ref,
             dy1_ref, dg_ref, dcw_ref, dcb_ref, dlg_ref, dlb_ref, ext_ref, y1_ref):
        b = pl.program_id(0)
        i = pl.program_id(1)

        @pl.when(jnp.logical_and(b == 0, i == 0))
        def _():
            dcw_ref[...] = jnp.zeros_like(dcw_ref)
            dcb_ref[...] = jnp.zeros_like(dcb_ref)
            dlg_ref[...] = jnp.zeros_like(dlg_ref)
            dlb_ref[...] = jnp.zeros_like(dlb_ref)

        _fill_glu_ext(ext_ref, halo_ref, cur_ref, W, i == 0)

        def chan(c, _):
            off = pl.multiple_of(c * LANES, LANES)
            y1_ref[:, pl.ds(off, LANES)] = _conv_taps(ext_ref, cw_ref, off, K, first) + cb_ref[:, pl.ds(off, LANES)]
            return 0

        lax.fori_loop(0, W // LANES, chan, 0)
        gamma = lg_ref[...]
        xh, rstd, ln = _layernorm_rows(y1_ref[...], gamma, lb_ref[...])
        g = g_ref[...]
        dyv = dy_ref[...].astype(F32)
        dg_ref[...] = (dyv * _silu(ln) * _dsilu(g)).astype(dg_ref.dtype)
        dln = dyv * _silu(g) * _dsilu(ln)
        dlg_ref[...] += jnp.sum(dln * xh, axis=0, keepdims=True)
        dlb_ref[...] += jnp.sum(dln, axis=0, keepdims=True)
        dy1 = _layernorm_rows_bwd(dln, xh, rstd, gamma)
        dy1_ref[...] = dy1
        dcb_ref[...] += jnp.sum(dy1, axis=0, keepdims=True)

        def chan_w(c, _):
            off = pl.multiple_of(c * LANES, LANES)
            d = dy1_ref[:, pl.ds(off, LANES)]
            for k in range(K):
                dcw_ref[k:k + 1, pl.ds(off, LANES)] += jnp.sum(
                    d * ext_ref[pl.ds(first + k, BLK), pl.ds(off, LANES)], axis=0, keepdims=True)
            return 0

        lax.fori_loop(0, W // LANES, chan_w, 0)

    cur, halo, gate = _conv_specs(S, W)
    vec = pl.BlockSpec((1, W), lambda b, i: (0, 0))
    taps = pl.BlockSpec((K, W), lambda b, i: (0, 0))
    one = pl.BlockSpec((None, BLK, W), lambda b, i: (b, i, 0))
    return pl.pallas_call(
        body, name="conv_bwd1", grid=(B, S // BLK),
        in_specs=[cur, halo, gate, one, taps, vec, vec, vec],
        out_specs=[one, one, taps, vec, vec, vec],
        out_shape=[jax.ShapeDtypeStruct((B, S, W), F32), jax.ShapeDtypeStruct((B, S, W), BF16),
                   jax.ShapeDtypeStruct((K, W), F32)] + [jax.ShapeDtypeStruct((1, W), F32)] * 3,
        scratch_shapes=[pltpu.VMEM((CONV_HALO + BLK, W), F32), pltpu.VMEM((BLK, W), F32)],
        compiler_params=_params(("arbitrary", "arbitrary")),
    )(proj3, proj3, proj3, dy, cw, cb, ln_g, ln_b)


def _conv_bwd2(proj3, dy1, dgate, cw_rev, W):
    B, S, _ = proj3.shape
    K = cw_rev.shape[0]
    NQ = S // BLK
    per = BLK // CONV_HALO

    def body(cur_ref, d_ref, dnext_ref, dgate_ref, cw_ref, dp_ref, ext_ref, dy0_ref):
        i = pl.program_id(1)
        ext_ref[0:BLK, :] = d_ref[...]
        ext_ref[BLK:BLK + CONV_HALO, :] = jnp.where(i == NQ - 1, 0.0, dnext_ref[...])

        def chan(c, _):
            off = pl.multiple_of(c * LANES, LANES)
            dy0_ref[:, pl.ds(off, LANES)] = _conv_taps(ext_ref, cw_ref, off, K, 0)
            return 0

        lax.fori_loop(0, W // LANES, chan, 0)
        a = cur_ref[:, :W]
        sg = _sigmoid(cur_ref[:, W:])
        dy0 = dy0_ref[...]
        dp_ref[:, 0:W] = (dy0 * sg).astype(dp_ref.dtype)
        dp_ref[:, W:2 * W] = (dy0 * a * sg * (1.0 - sg)).astype(dp_ref.dtype)
        dp_ref[:, 2 * W:3 * W] = dgate_ref[...]

    cur = pl.BlockSpec((None, BLK, 2 * W), lambda b, i: (b, i, 0))
    one = pl.BlockSpec((None, BLK, W), lambda b, i: (b, i, 0))
    nxt = pl.BlockSpec((None, CONV_HALO, W), lambda b, i: (b, jnp.minimum((i + 1) * per, S // CONV_HALO - 1), 0))
    return pl.pallas_call(
        body, name="conv_bwd2", grid=(B, NQ),
        in_specs=[cur, one, nxt, one, pl.BlockSpec((K, W), lambda b, i: (0, 0))],
        out_specs=pl.BlockSpec((None, BLK, 3 * W), lambda b, i: (b, i, 0)),
        out_shape=jax.ShapeDtypeStruct((B, S, 3 * W), BF16),
        scratch_shapes=[pltpu.VMEM((BLK + CONV_HALO, W), F32), pltpu.VMEM((BLK, W), F32)],
        compiler_params=_params(("parallel", "parallel")),
    )(proj3, dy1, dy1, dgate, cw_rev)


def _pack(arrays):
    flat = jnp.concatenate([a.astype(F32).reshape(-1) for a in arrays])
    n = flat.shape[0]
    pad = (-n) % (8 * LANES)
    if pad:
        flat = jnp.concatenate([flat, jnp.zeros((pad,), F32)])
    return flat.reshape(-1, LANES)


def _unpack(packed, shapes, lead=()):
    flat = packed.reshape(lead + (-1,))
    out, off = [], 0
    for shp in shapes:
        n = math.prod(shp)
        out.append(flat[..., off:off + n].reshape(lead + tuple(shp)))
        off += n
    return out


def _cols_from_dev(g):
    g = jnp.moveaxis(g, 0, -2)
    return g.reshape(g.shape[:-2] + (g.shape[-2] * g.shape[-1],))


def _my_cols(full, me):
    n8 = full.shape[-1] // N_DEV
    return lax.dynamic_slice_in_dim(full, me * n8, n8, axis=full.ndim - 1)


def kernel(x, a_norm, a_w_in, a_w_out, b_norm, b_w_in, b_v_ln_g, b_v_ln_b, b_w_s, b_b_s, b_w_out, c_norm, c_w_in, c_conv_w, c_conv_b, c_ln_g, c_ln_b, c_w_out, d_norm, d_w_in, d_b_f, d_w_out, final_norm, loss_target, m_a_norm, m_a_w_in, m_a_w_out, m_b_norm, m_b_w_in, m_b_v_ln_g, m_b_v_ln_b, m_b_w_s, m_b_b_s, m_b_w_out, m_c_norm, m_c_w_in, m_c_conv_w, m_c_conv_b, m_c_ln_g, m_c_ln_b, m_c_w_out, m_d_norm, m_d_w_in, m_d_b_f, m_d_w_out, m_final_norm, v_a_norm, v_a_w_in, v_a_w_out, v_b_norm, v_b_w_in, v_b_v_ln_g, v_b_v_ln_b, v_b_w_s, v_b_b_s, v_b_w_out, v_c_norm, v_c_w_in, v_c_conv_w, v_c_conv_b, v_c_ln_g, v_c_ln_b, v_c_w_out, v_d_norm, v_d_w_in, v_d_b_f, v_d_w_out, v_final_norm):
    B, S, D = x.shape
    T = B * S
    xi, yi, ci = _me()
    me = 4 * xi + 2 * yi + ci

    G = b_w_s.shape[1]
    KC = c_conv_w.shape[1]
    H_D = d_b_f.shape[1]
    W_A = a_w_out.shape[1] * N_DEV
    W_B = b_w_out.shape[1] * N_DEV
    W_C = c_w_out.shape[1] * N_DEV
    W_D = d_w_out.shape[1] * N_DEV
    N_D = d_w_in.shape[2] * N_DEV
    N_D_PAD = -(-N_D // (3 * LANES)) * (3 * LANES)

    big_names = ["a_w_in", "a_w_out", "b_w_in", "b_w_out", "c_w_in", "c_w_out", "d_w_in", "d_w_out"]
    big_w = dict(a_w_in=a_w_in[0], a_w_out=a_w_out[0], b_w_in=b_w_in[0], b_w_out=b_w_out[0],
                 c_w_in=c_w_in[0], c_w_out=c_w_out[0], d_w_in=d_w_in[0], d_w_out=d_w_out[0])
    small_sharded = [b_norm, b_v_ln_g, b_v_ln_b, c_norm, c_conv_w, c_conv_b, c_ln_g, c_ln_b, d_norm]
    gathered = _exchange([big_w[n].astype(BF16) for n in big_names] + [_pack(small_sharded)],
                         ["gather"] * (len(big_names) + 1), "gather_weights")
    wg = dict(zip(big_names, gathered[:-1]))
    (b_norm_f, b_lg_f, b_lb_f, c_norm_f, c_cw_f, c_cb_f, c_lg_f, c_lb_f, d_norm_f) = [
        _cols_from_dev(t) for t in _unpack(gathered[-1], [s.shape for s in small_sharded], lead=(N_DEV,))]
    c_cw_f = c_cw_f[0]
    a_w_out_f = wg["a_w_out"].reshape(W_A, D)
    b_w_out_f = wg["b_w_out"].reshape(W_B, D)
    c_w_out_f = wg["c_w_out"].reshape(W_C, D)
    d_w_out_f = wg["d_w_out"].reshape(W_D, D)
    d_w_in_f = jnp.pad(_cols_from_dev(wg["d_w_in"]), ((0, 0), (0, N_D_PAD - N_D)))

    wm = jnp.tril(b_w_s[0]).astype(BF16)
    bs_t = jnp.pad(b_b_s[0].T, ((0, 0), (0, LANES - G)))

    x0 = x.reshape(T, D)
    h_a = _rmsnorm_fwd(x0, a_norm, "rms_a")
    proj_a = _mm_w_dev(h_a, wg["a_w_in"], "proj_a").reshape(B, S, 4 * W_A)
    o_a, y_a = _sb_fwd(proj_a, W_A, SB_HEADS)
    y_a = y_a.reshape(T, W_A)
    x1 = x0 + _mm(y_a, a_w_out_f, "nn", T, D, W_A, F32, "out_a", 512, D, W_A)
    h_b = _rmsnorm_fwd(x1, b_norm_f, "rms_b")
    proj_b = _mm_w_dev(h_b, wg["b_w_in"], "proj_b")
    y_b = _gmlp_fwd(proj_b, wm, bs_t, b_lg_f, b_lb_f, W_B)
    x2 = x1 + _mm(y_b, b_w_out_f, "nn", T, D, W_B, F32, "out_b", 512, D, W_B)
    h_c = _rmsnorm_fwd(x2, c_norm_f, "rms_c")
    proj_c = _mm_w_dev(h_c, wg["c_w_in"], "proj_c").reshape(B, S, 3 * W_C)
    y_c = _conv_fwd(proj_c, c_cw_f, c_cb_f, c_lg_f, c_lb_f, W_C).reshape(T, W_C)
    x3 = x2 + _mm(y_c, c_w_out_f, "nn", T, D, W_C, F32, "out_c", 512, D, W_C)
    h_d = _rmsnorm_fwd(x3, d_norm_f, "rms_d")
    proj_d = _mm(h_d, d_w_in_f, "nn", T, N_D_PAD, D, F32, "proj_d", 512, 384, D).reshape(B, S, N_D_PAD)
    f_t = jnp.swapaxes(proj_d[:, :, 4 * W_D:4 * W_D + H_D], 1, 2)
    b_f_col = d_b_f.reshape(H_D, 1)
    cum_t = _fox_gate_fwd(f_t, b_f_col)
    cum_c = jnp.swapaxes(cum_t, 1, 2)
    o_d, y_d, lse_d = _fox_fwd(proj_d, cum_c, cum_t, W_D, H_D)
    y_d = y_d.reshape(T, W_D)
    x4 = x3 + _mm(y_d, d_w_out_f, "nn", T, D, W_D, F32, "out_d", 512, D, W_D)

    loss_part, dx, g_final = _loss_head(x4, final_norm.reshape(1, D), loss_target.reshape(T, D))
    loss = lax.psum(loss_part[0, 0], MESH_AXES)

    dy_d = _mm(dx, d_w_out_f, "nt", T, W_D, D, BF16, "dy_d", 512, W_D, D).reshape(B, S, W_D)
    gw_d_out = _mm(y_d, dx, "tn", W_D, D, T, BF16, "gw_d_out", W_D, D, 512).reshape(N_DEV, W_D // N_DEV, D)
    dq, dg, dk, dv, dcum = _fox_bwd(proj_d, cum_c, cum_t, o_d, lse_d, dy_d, W_D, H_D)
    df_t, g_b_f = _fox_gate_bwd(dcum.reshape(B, H_D, S), f_t, b_f_col)
    dproj_d = jnp.concatenate(
        [dq, dk.astype(BF16), dv.astype(BF16), dg, jnp.swapaxes(df_t, 1, 2).astype(BF16),
         jnp.zeros((B, S, N_D_PAD - N_D), BF16)], axis=-1).reshape(T, N_D_PAD)
    gw_d_in_full = _mm(h_d, dproj_d, "tn", D, N_D_PAD, T, BF16, "gw_d_in", D, 384, 512)
    gw_d_in = jnp.moveaxis(gw_d_in_full[:, :N_D].reshape(D, N_DEV, N_D // N_DEV), 1, 0)
    dh = _mm(dproj_d, d_w_in_f, "nt", T, D, N_D_PAD, F32, "dh_d", 512, D, 384)
    dx, g_d_norm = _rmsnorm_bwd(x3, d_norm_f, dh, dx, "rms_bwd_d")

    dy_c = _mm(dx, c_w_out_f, "nt", T, W_C, D, BF16, "dy_c", 512, W_C, D).reshape(B, S, W_C)
    gw_c_out = _mm(y_c, dx, "tn", W_C, D, T, BF16, "gw_c_out", 1024, D, 512).reshape(N_DEV, W_C // N_DEV, D)
    dy1, dgate_c, g_c_cw, g_c_cb, g_c_lg, g_c_lb = _conv_bwd1(proj_c, dy_c, c_cw_f, c_cb_f, c_lg_f, c_lb_f, W_C)
    dproj_c = _conv_bwd2(proj_c, dy1, dgate_c, c_cw_f[::-1], W_C).reshape(T, 3 * W_C)
    gw_c_in = _mm_grad_dev(h_c, dproj_c, "gw_c_in")
    dh = _mm_wT_dev(dproj_c, wg["c_w_in"], "dh_c")
    dx, g_c_norm = _rmsnorm_bwd(x2, c_norm_f, dh, dx, "rms_bwd_c")

    dy_b = _mm(dx, b_w_out_f, "nt", T, W_B, D, BF16, "dy_b", 512, W_B, D)
    gw_b_out = _mm(y_b, dx, "tn", W_B, D, T, BF16, "gw_b_out", 1024, D, 512).reshape(N_DEV, W_B // N_DEV, D)
    dproj_b, g_wm, g_bs_t, g_b_lg, g_b_lb = _gmlp_bwd(proj_b, dy_b, wm, bs_t, b_lg_f, b_lb_f, W_B)
    g_b_w_s = jnp.tril(g_wm)
    g_b_b_s = g_bs_t[:, :G].T
    gw_b_in = _mm_grad_dev(h_b, dproj_b, "gw_b_in")
    dh = _mm_wT_dev(dproj_b, wg["b_w_in"], "dh_b")
    dx, g_b_norm = _rmsnorm_bwd(x1, b_norm_f, dh, dx, "rms_bwd_b")

    dy_a = _mm(dx, a_w_out_f, "nt", T, W_A, D, BF16, "dy_a", 512, W_A, D).reshape(B, S, W_A)
    gw_a_out = _mm(y_a, dx, "tn", W_A, D, T, BF16, "gw_a_out", W_A, D, 512).reshape(N_DEV, W_A // N_DEV, D)
    dq, dg, dk, dv = _sb_bwd(proj_a, o_a, dy_a, W_A, SB_HEADS)
    dproj_a = jnp.concatenate([dq, dk.astype(BF16), dv.astype(BF16), dg], axis=-1).reshape(T, 4 * W_A)
    gw_a_in = _mm_grad_dev(h_a, dproj_a, "gw_a_in")
    dh = _mm_wT_dev(dproj_a, wg["a_w_in"], "dh_a")
    dx, g_a_norm = _rmsnorm_bwd(x0, a_norm, dh, dx, "rms_bwd_a")
    grad_x = dx.reshape(B, S, D)

    big_g = dict(a_w_in=gw_a_in, a_w_out=gw_a_out, b_w_in=gw_b_in, b_w_out=gw_b_out,
                 c_w_in=gw_c_in, c_w_out=gw_c_out, d_w_in=gw_d_in, d_w_out=gw_d_out)
    small_full = [g_a_norm, g_b_norm, g_b_lg, g_b_lb, g_b_w_s, g_b_b_s, g_c_norm, g_c_cw, g_c_cb, g_c_lg, g_c_lb,
                  g_d_norm, g_b_f, g_final]
    parts = _exchange([big_g[n] for n in big_names] + [_pack(small_full)],
                      ["scatter"] * len(big_names) + ["gather"], "exchange_grads")
    small_sum = _sum_parts(parts[-1], "sum_small")
    (s_a_norm, s_b_norm, s_b_lg, s_b_lb, s_b_w_s, s_b_b_s, s_c_norm, s_c_cw, s_c_cb, s_c_lg, s_c_lb,
     s_d_norm, s_b_f, s_final) = _unpack(small_sum, [g.shape for g in small_full])

    weights = dict(a_norm=a_norm, a_w_in=a_w_in, a_w_out=a_w_out, b_norm=b_norm, b_w_in=b_w_in, b_v_ln_g=b_v_ln_g,
                   b_v_ln_b=b_v_ln_b, b_w_s=b_w_s, b_b_s=b_b_s, b_w_out=b_w_out, c_norm=c_norm, c_w_in=c_w_in,
                   c_conv_w=c_conv_w, c_conv_b=c_conv_b, c_ln_g=c_ln_g, c_ln_b=c_ln_b, c_w_out=c_w_out,
                   d_norm=d_norm, d_w_in=d_w_in, d_b_f=d_b_f, d_w_out=d_w_out, final_norm=final_norm)
    mom_m = dict(a_norm=m_a_norm, a_w_in=m_a_w_in, a_w_out=m_a_w_out, b_norm=m_b_norm, b_w_in=m_b_w_in,
                 b_v_ln_g=m_b_v_ln_g, b_v_ln_b=m_b_v_ln_b, b_w_s=m_b_w_s, b_b_s=m_b_b_s, b_w_out=m_b_w_out,
                 c_norm=m_c_norm, c_w_in=m_c_w_in, c_conv_w=m_c_conv_w, c_conv_b=m_c_conv_b, c_ln_g=m_c_ln_g,
                 c_ln_b=m_c_ln_b, c_w_out=m_c_w_out, d_norm=m_d_norm, d_w_in=m_d_w_in, d_b_f=m_d_b_f,
                 d_w_out=m_d_w_out, final_norm=m_final_norm)
    mom_v = dict(a_norm=v_a_norm, a_w_in=v_a_w_in, a_w_out=v_a_w_out, b_norm=v_b_norm, b_w_in=v_b_w_in,
                 b_v_ln_g=v_b_v_ln_g, b_v_ln_b=v_b_v_ln_b, b_w_s=v_b_w_s, b_b_s=v_b_b_s, b_w_out=v_b_w_out,
                 c_norm=v_c_norm, c_w_in=v_c_w_in, c_conv_w=v_c_conv_w, c_conv_b=v_c_conv_b, c_ln_g=v_c_ln_g,
                 c_ln_b=v_c_ln_b, c_w_out=v_c_w_out, d_norm=v_d_norm, d_w_in=v_d_w_in, d_b_f=v_d_b_f,
                 d_w_out=v_d_w_out, final_norm=v_final_norm)
    order = list(weights)
    grads, deltas, new_m, new_v = {}, {}, {}, {}

    for n, part in zip(big_names, parts[:-1]):
        shp = weights[n].shape
        R, C = shp[1], shp[2]
        res = _adamw(part, weights[n].reshape(R, C), mom_m[n].reshape(R, C), mom_v[n].reshape(R, C), "adamw_" + n)
        grads[n], deltas[n], new_m[n], new_v[n] = [r.reshape(shp) for r in res]

    small_g = dict(
        a_norm=s_a_norm, b_norm=_my_cols(s_b_norm, me), b_v_ln_g=_my_cols(s_b_lg, me),
        b_v_ln_b=_my_cols(s_b_lb, me), b_w_s=s_b_w_s[None], b_b_s=s_b_b_s[None], c_norm=_my_cols(s_c_norm, me),
        c_conv_w=_my_cols(s_c_cw, me)[None], c_conv_b=_my_cols(s_c_cb, me), c_ln_g=_my_cols(s_c_lg, me),
        c_ln_b=_my_cols(s_c_lb, me), d_norm=_my_cols(s_d_norm, me), d_b_f=s_b_f.reshape(1, H_D),
        final_norm=s_final.reshape(D))
    small_names = list(small_g)
    sg_p = _pack([small_g[n] for n in small_names])
    res = _adamw(sg_p[None], _pack([weights[n] for n in small_names]), _pack([mom_m[n] for n in small_names]),
                 _pack([mom_v[n] for n in small_names]), "adamw_small")
    shapes = [weights[n].shape for n in small_names]
    for dst, r in zip((grads, deltas, new_m, new_v), res):
        for n, val in zip(small_names, _unpack(r, shapes)):
            dst[n] = val

    return (loss, grad_x, *[grads[n] for n in order], *[deltas[n] for n in order],
            *[new_m[n] for n in order], *[new_v[n] for n in order])
```

```python
import functools
import math

import jax
import jax.numpy as jnp
from jax import lax
from jax.experimental import pallas as pl
from jax.experimental.pallas import tpu as pltpu

F32 = jnp.float32
BF16 = jnp.bfloat16

EPS = 1e-6
SB_HEADS = 16
CONV_HALO = 32
BLK = 128
ATT_TK = 256
ATT_TQ = 256
LANES = 128
N_DEV = 8
MESH_AXES = ("x", "y", "c")

ADAM_LR = 0.001
ADAM_B1 = 0.9
ADAM_B2 = 0.999
ADAM_EPS = 1e-08
ADAM_WD = 0.01
ADAM_STEP = 10

VMEM_LIMIT = 56 * 1024 * 1024
NEG_BIG = -1e30

_NN = (((1,), (0,)), ((), ()))
_NT = (((1,), (1,)), ((), ()))
_TN = (((0,), (0,)), ((), ()))


def _dot(a, b, dims=_NN):
    return lax.dot_general(a, b, dims, preferred_element_type=F32)


def _split_dot(x, m):
    hi = x.astype(BF16)
    lo = (x - hi.astype(F32)).astype(BF16)
    return _dot(hi, m) + _dot(lo, m)


def _split3_dot(x, m):
    hi = x.astype(BF16)
    r1 = x - hi.astype(F32)
    mid = r1.astype(BF16)
    lo = (r1 - mid.astype(F32)).astype(BF16)
    return _dot(hi, m) + _dot(mid, m) + _dot(lo, m)


def _params(sem=None):
    kw = dict(vmem_limit_bytes=VMEM_LIMIT)
    if sem is not None:
        kw["dimension_semantics"] = sem
    return pltpu.CompilerParams(**kw)


def _sigmoid(x):
    return jax.nn.sigmoid(x)


def _silu(x):
    return x * _sigmoid(x)


def _dsilu(x):
    s = _sigmoid(x)
    return s * (1.0 + x * (1.0 - s))


_GELU_C = math.sqrt(2.0 / math.pi)


def _gelu(x):
    return 0.5 * x * (1.0 + jnp.tanh(_GELU_C * (x + 0.044715 * x * x * x)))


def _dgelu(x):
    th = jnp.tanh(_GELU_C * (x + 0.044715 * x * x * x))
    return 0.5 * (1.0 + th) + 0.5 * x * (1.0 - th * th) * _GELU_C * (1.0 + 3.0 * 0.044715 * x * x)


def _mm(a, b, mode, M, N, K, out_dtype, name, tm, tn, tk, a_spec=None, b_spec=None, o_spec=None, out_shape=None):
    tm, tn, tk = min(tm, M), min(tn, N), min(tk, K)
    assert M % tm == 0 and N % tn == 0 and K % tk == 0, (name, M, N, K, tm, tn, tk)
    nk = K // tk
    dims = {"nn": _NN, "nt": _NT, "tn": _TN}[mode]
    if a_spec is None:
        a_spec = (pl.BlockSpec((tk, tm), lambda i, j, k: (k, i)) if mode == "tn"
                  else pl.BlockSpec((tm, tk), lambda i, j, k: (i, k)))
    if b_spec is None:
        b_spec = (pl.BlockSpec((tn, tk), lambda i, j, k: (j, k)) if mode == "nt"
                  else pl.BlockSpec((tk, tn), lambda i, j, k: (k, j)))
    if o_spec is None:
        o_spec = pl.BlockSpec((tm, tn), lambda i, j, k: (i, j))
    if out_shape is None:
        out_shape = (M, N)

    def body(a_ref, b_ref, o_ref, acc_ref):
        k = pl.program_id(2)

        @pl.when(k == 0)
        def _():
            acc_ref[...] = jnp.zeros_like(acc_ref)

        acc_ref[...] += _dot(a_ref[...].astype(BF16), b_ref[...].astype(BF16), dims)

        @pl.when(k == nk - 1)
        def _():
            o_ref[...] = acc_ref[...].astype(o_ref.dtype)

    return pl.pallas_call(
        body, name=name, grid=(M // tm, N // tn, nk),
        in_specs=[a_spec, b_spec], out_specs=o_spec,
        out_shape=jax.ShapeDtypeStruct(out_shape, out_dtype),
        scratch_shapes=[pltpu.VMEM((tm, tn), F32)],
        compiler_params=_params(("parallel", "parallel", "arbitrary")),
    )(a, b)


def _mm_w_dev(a, w3, name, out_dtype=F32, tm=1024):
    M, K = a.shape
    n8 = w3.shape[2]
    tn = n8 if n8 <= 768 else 512
    per = n8 // tn
    b_spec = pl.BlockSpec((None, K, tn), lambda i, j, k: (j // per, 0, j % per))
    return _mm(a, w3, "nn", M, N_DEV * n8, K, out_dtype, name, tm, tn, K, b_spec=b_spec)


def _mm_wT_dev(a, w3, name, out_dtype=F32, tm=512):
    M, N = a.shape
    K, n8 = w3.shape[1], w3.shape[2]
    tk = n8 if n8 <= 768 else 512
    per = n8 // tk
    b_spec = pl.BlockSpec((None, K, tk), lambda i, j, k: (k // per, 0, k % per))
    return _mm(a, w3, "nt", M, K, N, out_dtype, name, tm, K, tk, b_spec=b_spec)


def _mm_grad_dev(h, d, name, out_dtype=BF16):
    T, M = h.shape
    N = d.shape[1]
    n8 = N // N_DEV
    tn = n8 if n8 <= 768 else 512
    per = n8 // tn
    tm = min(M, 1024)
    o_spec = pl.BlockSpec((None, tm, tn), lambda i, j, k: (j // per, i, j % per))
    return _mm(h, d, "tn", M, N, T, out_dtype, name, tm, tn, 512, o_spec=o_spec, out_shape=(N_DEV, M, n8))


def _me():
    x, y, c = lax.axis_index("x"), lax.axis_index("y"), lax.axis_index("c")
    return x, y, c


def _peer(r):
    x, y, c = _me()
    px = 1 - x if (r >> 2) & 1 else x
    py = 1 - y if (r >> 1) & 1 else y
    pc = 1 - c if r & 1 else c
    return (px, py, pc), 4 * px + 2 * py + pc


class _Exchange:
    def __init__(self, arrays, kinds):
        self.arrays, self.kinds, self.n = list(arrays), list(kinds), len(arrays)
        self.out_shapes = [
            jax.ShapeDtypeStruct((N_DEV,) + a.shape if kind == "gather" else a.shape, a.dtype)
            for a, kind in zip(arrays, kinds)]
        self.specs = [pl.BlockSpec(memory_space=pl.ANY)] * self.n
        self.sems = [pltpu.SemaphoreType.DMA((self.n, N_DEV - 1)), pltpu.SemaphoreType.DMA((self.n, N_DEV - 1)),
                     pltpu.SemaphoreType.DMA((self.n,))]

    def _copies(self, ins, outs, sems, receiving):
        send_sems, recv_sems, local_sems = sems
        x, y, c = _me()
        me = 4 * x + 2 * y + c

        def src(k, pid):
            return ins[k] if self.kinds[k] == "gather" else ins[k].at[pid]

        local = [pltpu.make_async_copy(src(k, me), outs[k].at[me], local_sems.at[k]) for k in range(self.n)]
        remote = []
        for r in range(1, N_DEV):
            peer, pid = _peer(r)
            for k in range(self.n):
                remote.append(pltpu.make_async_remote_copy(
                    src_ref=src(k, pid), dst_ref=outs[k].at[pid if receiving else me],
                    send_sem=send_sems.at[k, r - 1], recv_sem=recv_sems.at[k, r - 1],
                    device_id=peer, device_id_type=pl.DeviceIdType.MESH))
        return local, remote

    def start(self, ins, outs, sems):
        local, remote = self._copies(ins, outs, sems, False)
        for cp in local + remote:
            cp.start()

    def wait(self, ins, outs, sems):
        local, remote = self._copies(ins, outs, sems, True)
        for cp in remote:
            cp.wait_recv()
        for cp in remote:
            cp.wait_send()
        for cp in local:
            cp.wait()

    def run(self, name):
        n = self.n

        def body(*refs):
            ins, outs, sems = refs[:n], refs[n:2 * n], refs[2 * n:]
            self.start(ins, outs, sems)
            self.wait(ins, outs, sems)

        return pl.pallas_call(
            body, name=name, in_specs=self.specs, out_specs=self.specs, out_shape=self.out_shapes,
            scratch_shapes=self.sems,
        )(*self.arrays)


def _call_hosting(body, exch, name, grid, in_specs, out_specs, out_shape, scratch_shapes, args):
    if exch is None:
        res = pl.pallas_call(
            body, name=name, grid=grid, in_specs=list(in_specs), out_specs=list(out_specs),
            out_shape=list(out_shape), scratch_shapes=list(scratch_shapes),
            compiler_params=_params(("arbitrary",) * len(grid)))(*args)
        return res, []
    n_in, n_out, n_scr, nc = len(in_specs), len(out_specs), len(scratch_shapes), exch.n

    def full_body(*refs):
        ins, refs = refs[:n_in], refs[n_in:]
        cins, refs = refs[:nc], refs[nc:]
        outs, refs = refs[:n_out], refs[n_out:]
        couts, refs = refs[:nc], refs[nc:]
        scr, sems = refs[:n_scr], refs[n_scr:]
        ids = [pl.program_id(a) for a in range(len(grid))]
        first = functools.reduce(jnp.logical_and, [i == 0 for i in ids])
        last = functools.reduce(jnp.logical_and, [i == g - 1 for i, g in zip(ids, grid)])

        @pl.when(first)
        def _():
            exch.start(cins, couts, sems)

        body(*ins, *outs, *scr)

        @pl.when(last)
        def _():
            exch.wait(cins, couts, sems)

    res = pl.pallas_call(
        full_body, name=name, grid=grid,
        in_specs=list(in_specs) + exch.specs, out_specs=list(out_specs) + exch.specs,
        out_shape=list(out_shape) + exch.out_shapes,
        scratch_shapes=list(scratch_shapes) + exch.sems,
        compiler_params=_params(("arbitrary",) * len(grid)),
    )(*args, *exch.arrays)
    return res[:n_out], res[n_out:]


def _rmsnorm_fwd(x, g, name):
    T, D = x.shape
    tr = min(256, T)

    def body(x_ref, g_ref, h_ref):
        xv = x_ref[...]
        r = lax.rsqrt(jnp.mean(xv * xv, axis=-1, keepdims=True) + EPS)
        h_ref[...] = (xv * r * g_ref[...]).astype(BF16)

    return pl.pallas_call(
        body, name=name, grid=(T // tr,),
        in_specs=[pl.BlockSpec((tr, D), lambda i: (i, 0)), pl.BlockSpec((1, D), lambda i: (0, 0))],
        out_specs=pl.BlockSpec((tr, D), lambda i: (i, 0)),
        out_shape=jax.ShapeDtypeStruct((T, D), BF16),
        compiler_params=_params(("parallel",)),
    )(x, g)


def _rmsnorm_bwd(x, g, dh, dres, name):
    T, D = x.shape
    tr = min(256, T)

    def body(x_ref, g_ref, dh_ref, dres_ref, dx_ref, dg_ref):
        i = pl.program_id(0)
        xv = x_ref[...]
        r = lax.rsqrt(jnp.mean(xv * xv, axis=-1, keepdims=True) + EPS)
        xh = xv * r
        dhv = dh_ref[...]
        dxh = dhv * g_ref[...]
        dx_ref[...] = dres_ref[...] + r * (dxh - xh * jnp.mean(dxh * xh, axis=-1, keepdims=True))

        @pl.when(i == 0)
        def _():
            dg_ref[...] = jnp.zeros_like(dg_ref)

        dg_ref[...] += jnp.sum(dhv * xh, axis=0, keepdims=True)

    row = pl.BlockSpec((tr, D), lambda i: (i, 0))
    vec = pl.BlockSpec((1, D), lambda i: (0, 0))
    return pl.pallas_call(
        body, name=name, grid=(T // tr,),
        in_specs=[row, vec, row, row], out_specs=[row, vec],
        out_shape=[jax.ShapeDtypeStruct((T, D), F32), jax.ShapeDtypeStruct((1, D), F32)],
        compiler_params=_params(("arbitrary",)),
    )(x, g, dh, dres)


def _loss_head(x, g, target):
    T, D = x.shape
    tr = min(256, T)

    def body(x_ref, g_ref, t_ref, loss_ref, dx_ref, dg_ref):
        i = pl.program_id(0)
        xv = x_ref[...]
        gv = g_ref[...]
        r = lax.rsqrt(jnp.mean(xv * xv, axis=-1, keepdims=True) + EPS)
        xh = xv * r
        diff = xh * gv - t_ref[...]
        dy = diff * (1.0 / D)
        dxh = dy * gv
        dx_ref[...] = r * (dxh - xh * jnp.mean(dxh * xh, axis=-1, keepdims=True))

        @pl.when(i == 0)
        def _():
            dg_ref[...] = jnp.zeros_like(dg_ref)
            loss_ref[...] = jnp.zeros_like(loss_ref)

        dg_ref[...] += jnp.sum(dy * xh, axis=0, keepdims=True)
        part = jnp.sum(jnp.sum(diff * diff, axis=1, keepdims=True), axis=0, keepdims=True)
        loss_ref[...] += (0.5 / D) * part

    row = pl.BlockSpec((tr, D), lambda i: (i, 0))
    vec = pl.BlockSpec((1, D), lambda i: (0, 0))
    return pl.pallas_call(
        body, name="loss_head", grid=(T // tr,),
        in_specs=[row, vec, row],
        out_specs=[pl.BlockSpec((1, 1), lambda i: (0, 0)), row, vec],
        out_shape=[jax.ShapeDtypeStruct((1, 1), F32), jax.ShapeDtypeStruct((T, D), F32),
                   jax.ShapeDtypeStruct((1, D), F32)],
        compiler_params=_params(("arbitrary",)),
    )(x, g, target)


def _adamw(parts, w, m, v, name):
    P, R, C = parts.shape
    tr = R
    for cand in (128, 64, 32, 16, 8):
        if R % cand == 0:
            tr = cand
            break

    def body(p_ref, w_ref, m_ref, v_ref, g_out, d_out, m_out, v_out):
        g = p_ref[0].astype(F32)
        for p in range(1, P):
            g = g + p_ref[p].astype(F32)
        wv = w_ref[...]
        mn = ADAM_B1 * m_ref[...] + (1.0 - ADAM_B1) * g
        vn = ADAM_B2 * v_ref[...] + (1.0 - ADAM_B2) * (g * g)
        m_hat = mn / (1.0 - ADAM_B1 ** ADAM_STEP)
        v_hat = vn / (1.0 - ADAM_B2 ** ADAM_STEP)
        g_out[...] = g
        d_out[...] = -ADAM_LR * (m_hat / (jnp.sqrt(v_hat) + ADAM_EPS) + ADAM_WD * wv)
        m_out[...] = mn
        v_out[...] = vn

    row = pl.BlockSpec((tr, C), lambda i: (i, 0))
    return pl.pallas_call(
        body, name=name, grid=(R // tr,),
        in_specs=[pl.BlockSpec((P, tr, C), lambda i: (0, i, 0)), row, row, row],
        out_specs=[row, row, row, row],
        out_shape=[jax.ShapeDtypeStruct((R, C), F32)] * 4,
        compiler_params=_params(("parallel",)),
    )(parts, w, m, v)


def _sum_parts(parts, name):
    P, R, C = parts.shape
    tr = 128 if R % 128 == 0 else R

    def body(p_ref, o_ref):
        g = p_ref[0]
        for p in range(1, P):
            g = g + p_ref[p]
        o_ref[...] = g

    return pl.pallas_call(
        body, name=name, grid=(R // tr,),
        in_specs=[pl.BlockSpec((P, tr, C), lambda i: (0, i, 0))],
        out_specs=pl.BlockSpec((tr, C), lambda i: (i, 0)),
        out_shape=jax.ShapeDtypeStruct((R, C), F32),
        compiler_params=_params(("parallel",)),
    )(parts)


def _lane_head(Dh):
    assert Dh & (Dh - 1) == 0 and Dh <= LANES
    return lax.shift_right_logical(lax.broadcasted_iota(jnp.int32, (1, LANES), 1), Dh.bit_length() - 1)


def _stack_heads(x, lane_head, hpb):
    return jnp.concatenate([jnp.where(lane_head == h, x, 0.0) for h in range(hpb)], axis=0)


def _unstack_heads(acc, lane_head, hpb):
    TQ = acc.shape[0] // hpb
    out = acc[0:TQ]
    for h in range(1, hpb):
        out = jnp.where(lane_head == h, acc[h * TQ:(h + 1) * TQ], out)
    return out


def _key_tile(S):
    return ATT_TK if S % ATT_TK == 0 else BLK


def _query_tile(S):
    return ATT_TQ if S % ATT_TQ == 0 else BLK


def _causal_iotas(RS, TK, TQ):
    assert TQ & (TQ - 1) == 0 and TK % TQ == 0
    trow = jnp.bitwise_and(lax.broadcasted_iota(jnp.int32, (RS, TK), 0), TQ - 1)
    col = lax.broadcasted_iota(jnp.int32, (RS, TK), 1)
    return trow, col


def _tri(TK, op):
    r = lax.broadcasted_iota(jnp.int32, (TK, TK), 0)
    c = lax.broadcasted_iota(jnp.int32, (TK, TK), 1)
    return op(r, c).astype(BF16)


def _logsig_parts(z):
    sp = jnp.log(1.0 + jnp.exp(-jnp.abs(z)))
    return jnp.minimum(z, 0.0) - sp, -jnp.maximum(z, 0.0) - sp


def _sb_fwd(proj3, W, heads, exch):
    B, S, _ = proj3.shape
    Dh = W // heads
    hpb = LANES // Dh
    P, TQ = W // LANES, _query_tile(S)
    NQ = S // TQ
    scale = 1.0 / math.sqrt(Dh)

    TK = _key_tile(S)
    RS = hpb * TQ

    def body(q_ref, k_ref, v_ref, g_ref, o_ref, y_ref):
        i = pl.program_id(2)
        lane_head = _lane_head(Dh)
        trow, col = _causal_iotas(RS, TK, TQ)
        msuf = _tri(TK, lambda r, c: r > c)
        qs = _stack_heads(q_ref[...], lane_head, hpb).astype(BF16)
        nt = (i * TQ) // TK + 1

        def step(jj, carry):
            rem, acc = carry
            jt = nt - 1 - jj
            off = pl.multiple_of(jt * TK, TK)
            kj = k_ref[pl.ds(off, TK), :].astype(BF16)
            vj = v_ref[pl.ds(off, TK), :].astype(BF16)
            z = _dot(qs, kj, _NT) * scale
            lb, lr = _logsig_parts(z)
            msk = col + (jt * TK - i * TQ) < trow
            lr = jnp.where(msk, lr, 0.0)
            w = jnp.where(msk, jnp.exp(lb + _split_dot(lr, msuf) + rem), 0.0)
            return rem + jnp.sum(lr, axis=1, keepdims=True), acc + _dot(w.astype(BF16), vj)

        _, acc = lax.fori_loop(0, nt, step, (jnp.zeros((RS, 1), F32), jnp.zeros((RS, LANES), F32)))
        o = _unstack_heads(acc, lane_head, hpb)
        o_ref[...] = o
        y_ref[...] = (o * _silu(g_ref[...])).astype(BF16)

    blk = lambda sec: pl.BlockSpec((None, TQ, LANES), lambda b, p, i: (b, i, sec * P + p))
    full = lambda sec: pl.BlockSpec((None, S, LANES), lambda b, p, i: (b, 0, sec * P + p))
    out = pl.BlockSpec((None, TQ, LANES), lambda b, p, i: (b, i, p))
    return _call_hosting(
        body, exch, "sb_fwd", (B, P, NQ), [blk(0), full(1), full(2), blk(3)], [out, out],
        [jax.ShapeDtypeStruct((B, S, W), F32), jax.ShapeDtypeStruct((B, S, W), BF16)], [],
        (proj3, proj3, proj3, proj3))


def _sb_bwd(proj3, o, dy, W, heads, exch):
    B, S, _ = proj3.shape
    Dh = W // heads
    hpb = LANES // Dh
    P, TQ = W // LANES, _query_tile(S)
    NQ = S // TQ
    scale = 1.0 / math.sqrt(Dh)

    TK = _key_tile(S)
    RS = hpb * TQ

    def body(q_ref, k_ref, v_ref, g_ref, o_ref, dy_ref, dq_ref, dg_ref, dk_ref, dv_ref, e_ref, sig_ref):
        i = pl.program_id(2)

        @pl.when(i == 0)
        def _():
            dk_ref[...] = jnp.zeros_like(dk_ref)
            dv_ref[...] = jnp.zeros_like(dv_ref)

        lane_head = _lane_head(Dh)
        trow, col = _causal_iotas(RS, TK, TQ)
        msuf = _tri(TK, lambda r, c: r > c)
        mpre = _tri(TK, lambda r, c: r < c)
        g = g_ref[...]
        dyv = dy_ref[...].astype(F32)
        dg_ref[...] = (dyv * o_ref[...] * _dsilu(g)).astype(dg_ref.dtype)
        qs = _stack_heads(q_ref[...], lane_head, hpb).astype(BF16)
        dos = _stack_heads(dyv * _silu(g), lane_head, hpb).astype(BF16)
        nt = (i * TQ) // TK + 1

        def weights(jj, rem):
            jt = nt - 1 - jj
            off = pl.multiple_of(jt * TK, TK)
            kj = k_ref[pl.ds(off, TK), :].astype(BF16)
            vj = v_ref[pl.ds(off, TK), :].astype(BF16)
            z = _dot(qs, kj, _NT) * scale
            lb, lr = _logsig_parts(z)
            msk = col + (jt * TK - i * TQ) < trow
            lr = jnp.where(msk, lr, 0.0)
            w = jnp.where(msk, jnp.exp(lb + _split_dot(lr, msuf) + rem), 0.0)
            e_ref[jt] = w * _dot(dos, vj, _NT)
            sig_ref[jt] = jnp.exp(lb)
            dv_ref[pl.ds(off, TK), :] += _dot(w.astype(BF16), dos, _TN)
            return rem + jnp.sum(lr, axis=1, keepdims=True)

        lax.fori_loop(0, nt, weights, jnp.zeros((RS, 1), F32))

        def grads(jt, carry):
            pre, acc = carry
            off = pl.multiple_of(jt * TK, TK)
            kj = k_ref[pl.ds(off, TK), :].astype(BF16)
            e = e_ref[jt]
            sig = sig_ref[jt]
            before = _split_dot(e, mpre) + pre
            msk = col + (jt * TK - i * TQ) < trow
            dz = (jnp.where(msk, e * (1.0 - sig) - before * sig, 0.0) * scale).astype(BF16)
            dk_ref[pl.ds(off, TK), :] += _dot(dz, qs, _TN)
            return pre + jnp.sum(e, axis=1, keepdims=True), acc + _dot(dz, kj)

        _, acc = lax.fori_loop(0, nt, grads, (jnp.zeros((RS, 1), F32), jnp.zeros((RS, LANES), F32)))
        dq_ref[...] = _unstack_heads(acc, lane_head, hpb).astype(dq_ref.dtype)

    blk = lambda sec: pl.BlockSpec((None, TQ, LANES), lambda b, p, i: (b, i, sec * P + p))
    full = lambda sec: pl.BlockSpec((None, S, LANES), lambda b, p, i: (b, 0, sec * P + p))
    one = pl.BlockSpec((None, TQ, LANES), lambda b, p, i: (b, i, p))
    acc = pl.BlockSpec((None, S, LANES), lambda b, p, i: (b, 0, p))
    return _call_hosting(
        body, exch, "sb_bwd", (B, P, NQ), [blk(0), full(1), full(2), blk(3), one, one], [one, one, acc, acc],
        [jax.ShapeDtypeStruct((B, S, W), BF16), jax.ShapeDtypeStruct((B, S, W), BF16),
         jax.ShapeDtypeStruct((B, S, W), F32), jax.ShapeDtypeStruct((B, S, W), F32)],
        [pltpu.VMEM((S // TK, RS, TK), F32), pltpu.VMEM((S // TK, RS, TK), F32)],
        (proj3, proj3, proj3, proj3, o, dy))


def _fox_gate_fwd(f_t, b_f):
    B, H, S = f_t.shape

    def body(f_ref, b_ref, c_ref):
        row = lax.broadcasted_iota(jnp.int32, (BLK, BLK), 0)
        col = lax.broadcasted_iota(jnp.int32, (BLK, BLK), 1)
        mpre = (row <= col).astype(BF16)
        carry = jnp.zeros((H, 1), F32)
        for n in range(S // BLK):
            sl = pl.ds(n * BLK, BLK)
            lf, _ = _logsig_parts(f_ref[:, sl] + b_ref[...])
            c_ref[:, sl] = _split3_dot(lf, mpre) + carry
            carry = carry + jnp.sum(lf, axis=1, keepdims=True)

    spec = pl.BlockSpec((None, H, S), lambda b: (b, 0, 0))
    return pl.pallas_call(
        body, name="fox_gate_fwd", grid=(B,),
        in_specs=[spec, pl.BlockSpec((H, 1), lambda b: (0, 0))], out_specs=spec,
        out_shape=jax.ShapeDtypeStruct((B, H, S), F32),
        compiler_params=_params(("parallel",)),
    )(f_t, b_f)


def _fox_gate_bwd(dcum_t, f_t, b_f):
    B, H, S = f_t.shape

    def body(d_ref, f_ref, b_ref, df_ref, db_ref):
        b = pl.program_id(0)

        @pl.when(b == 0)
        def _():
            db_ref[...] = jnp.zeros_like(db_ref)

        row = lax.broadcasted_iota(jnp.int32, (BLK, BLK), 0)
        col = lax.broadcasted_iota(jnp.int32, (BLK, BLK), 1)
        msuf = (row >= col).astype(BF16)
        carry = jnp.zeros((H, 1), F32)
        dbacc = jnp.zeros((H, 1), F32)
        for n in reversed(range(S // BLK)):
            sl = pl.ds(n * BLK, BLK)
            dv = d_ref[:, sl]
            dlf = _split3_dot(dv, msuf) + carry
            carry = carry + jnp.sum(dv, axis=1, keepdims=True)
            df = dlf * _sigmoid(-(f_ref[:, sl] + b_ref[...]))
            df_ref[:, sl] = df
            dbacc = dbacc + jnp.sum(df, axis=1, keepdims=True)
        db_ref[...] += dbacc

    spec = pl.BlockSpec((None, H, S), lambda b: (b, 0, 0))
    vec = pl.BlockSpec((H, 1), lambda b: (0, 0))
    return pl.pallas_call(
        body, name="fox_gate_bwd", grid=(B,),
        in_specs=[spec, spec, vec], out_specs=[spec, vec],
        out_shape=[jax.ShapeDtypeStruct((B, H, S), F32), jax.ShapeDtypeStruct((H, 1), F32)],
        compiler_params=_params(("arbitrary",)),
    )(dcum_t, f_t, b_f)


def _pick_col(block, idx, lane_iota):
    return jnp.sum(jnp.where(lane_iota == idx, block, 0.0), axis=1, keepdims=True)


def _pick_row(block, idx, sub_iota):
    return jnp.sum(jnp.where(sub_iota == idx, block, 0.0), axis=0, keepdims=True)


def _fox_fwd(proj3, cum_c, cum_t, W, heads):
    B, S, _ = proj3.shape
    H = heads
    Dh = W // heads
    hpb = LANES // Dh
    P, TQ = W // LANES, _query_tile(S)
    NQ = S // TQ
    scale = 1.0 / math.sqrt(Dh)

    TK = _key_tile(S)
    RS = hpb * TQ

    def body(q_ref, k_ref, v_ref, g_ref, cc_ref, ct_ref, o_ref, y_ref, lse_ref):
        p = pl.program_id(1)
        i = pl.program_id(2)
        lane_head = _lane_head(Dh)
        trow, col = _causal_iotas(RS, TK, TQ)
        lane_h = lax.broadcasted_iota(jnp.int32, (1, H), 1)
        sub_h = lax.broadcasted_iota(jnp.int32, (H, 1), 0)
        qs = _stack_heads(q_ref[...], lane_head, hpb).astype(BF16)
        cc = cc_ref[...]
        c_q = jnp.concatenate([_pick_col(cc, p * hpb + h, lane_h) for h in range(hpb)], axis=0)
        nt = (i * TQ + TQ - 1) // TK + 1

        def step(jt, carry):
            mx, l, acc = carry
            off = pl.multiple_of(jt * TK, TK)
            kj = k_ref[pl.ds(off, TK), :].astype(BF16)
            vj = v_ref[pl.ds(off, TK), :].astype(BF16)
            ctb = ct_ref[:, pl.ds(off, TK)]
            z = _dot(qs, kj, _NT) * scale + c_q
            s = jnp.concatenate([z[h * TQ:(h + 1) * TQ] - _pick_row(ctb, p * hpb + h, sub_h) for h in range(hpb)],
                                axis=0)
            s = jnp.where(col + (jt * TK - i * TQ) <= trow, s, NEG_BIG)
            mx2 = jnp.maximum(mx, jnp.max(s, axis=1, keepdims=True))
            pe = jnp.exp(s - mx2)
            alpha = jnp.exp(mx - mx2)
            return (mx2, alpha * l + jnp.sum(pe, axis=1, keepdims=True), alpha * acc + _dot(pe.astype(BF16), vj))

        mx, l, acc = lax.fori_loop(
            0, nt, step, (jnp.full((RS, 1), NEG_BIG, F32), jnp.zeros((RS, 1), F32), jnp.zeros((RS, LANES), F32)))
        o = _unstack_heads(acc / l, lane_head, hpb)
        o_ref[...] = o
        lse_ref[...] = _unstack_heads(jnp.broadcast_to(mx + jnp.log(l), (RS, LANES)), lane_head, hpb)
        y_ref[...] = (o * _silu(g_ref[...])).astype(BF16)

    blk = lambda sec: pl.BlockSpec((None, TQ, LANES), lambda b, p, i: (b, i, sec * P + p))
    full = lambda sec: pl.BlockSpec((None, S, LANES), lambda b, p, i: (b, 0, sec * P + p))
    out = pl.BlockSpec((None, TQ, LANES), lambda b, p, i: (b, i, p))
    return pl.pallas_call(
        body, name="fox_fwd", grid=(B, P, NQ),
        in_specs=[blk(0), full(1), full(2), blk(3),
                  pl.BlockSpec((None, TQ, H), lambda b, p, i: (b, i, 0)),
                  pl.BlockSpec((None, H, S), lambda b, p, i: (b, 0, 0))],
        out_specs=[out, out, out],
        out_shape=[jax.ShapeDtypeStruct((B, S, W), F32), jax.ShapeDtypeStruct((B, S, W), BF16),
                   jax.ShapeDtypeStruct((B, S, W), F32)],
        compiler_params=_params(("parallel", "parallel", "arbitrary")),
    )(proj3, proj3, proj3, proj3, cum_c, cum_t)


def _fox_bwd(proj3, cum_c, cum_t, o, lse, dy, W, heads):
    B, S, _ = proj3.shape
    H = heads
    Dh = W // heads
    hpb = LANES // Dh
    P, TQ = W // LANES, _query_tile(S)
    NQ = S // TQ
    scale = 1.0 / math.sqrt(Dh)

    TK = _key_tile(S)
    RS = hpb * TQ

    def body(q_ref, k_ref, v_ref, g_ref, cc_ref, ct_ref, o_ref, lse_ref, dy_ref,
             dq_ref, dg_ref, dk_ref, dv_ref, dc_ref, p_scr, dp_scr):
        p = pl.program_id(1)
        i = pl.program_id(2)

        @pl.when(i == 0)
        def _():
            dk_ref[...] = jnp.zeros_like(dk_ref)
            dv_ref[...] = jnp.zeros_like(dv_ref)
            dc_ref[...] = jnp.zeros_like(dc_ref)

        lane_head = _lane_head(Dh)
        trow, col = _causal_iotas(RS, TK, TQ)
        lane_h = lax.broadcasted_iota(jnp.int32, (1, H), 1)
        sub_h = lax.broadcasted_iota(jnp.int32, (H, 1), 0)
        lane = lax.broadcasted_iota(jnp.int32, (1, LANES), 1)
        g = g_ref[...]
        lsev = lse_ref[...]
        cc = cc_ref[...]
        dyv = dy_ref[...].astype(F32)
        dg_ref[...] = (dyv * o_ref[...] * _dsilu(g)).astype(dg_ref.dtype)
        qs = _stack_heads(q_ref[...], lane_head, hpb).astype(BF16)
        dos = _stack_heads(dyv * _silu(g), lane_head, hpb).astype(BF16)
        c_q = jnp.concatenate([_pick_col(cc, p * hpb + h, lane_h) for h in range(hpb)], axis=0)
        c_q = c_q - jnp.concatenate([_pick_col(lsev, h * Dh, lane) for h in range(hpb)], axis=0)
        nt = (i * TQ + TQ - 1) // TK + 1

        def probs(jt, dsum):
            off = pl.multiple_of(jt * TK, TK)
            kj = k_ref[pl.ds(off, TK), :].astype(BF16)
            vj = v_ref[pl.ds(off, TK), :].astype(BF16)
            ctb = ct_ref[:, pl.ds(off, TK)]
            z = _dot(qs, kj, _NT) * scale + c_q
            s = jnp.concatenate([z[h * TQ:(h + 1) * TQ] - _pick_row(ctb, p * hpb + h, sub_h) for h in range(hpb)],
                                axis=0)
            pr = jnp.where(col + (jt * TK - i * TQ) <= trow, jnp.exp(s), 0.0)
            dp = _dot(dos, vj, _NT)
            p_scr[jt] = pr
            dp_scr[jt] = dp
            dv_ref[pl.ds(off, TK), :] += _dot(pr.astype(BF16), dos, _TN)
            return dsum + jnp.sum(pr * dp, axis=1, keepdims=True)

        dsum = lax.fori_loop(0, nt, probs, jnp.zeros((RS, 1), F32))

        def grads(jt, acc):
            off = pl.multiple_of(jt * TK, TK)
            kj = k_ref[pl.ds(off, TK), :].astype(BF16)
            ds = p_scr[jt] * (dp_scr[jt] - dsum)
            for h in range(hpb):
                dc_ref[h:h + 1, pl.ds(off, TK)] -= jnp.sum(ds[h * TQ:(h + 1) * TQ], axis=0, keepdims=True)
            dsb = (ds * scale).astype(BF16)
            dk_ref[pl.ds(off, TK), :] += _dot(dsb, qs, _TN)
            return acc + _dot(dsb, kj)

        acc = lax.fori_loop(0, nt, grads, jnp.zeros((RS, LANES), F32))
        dq_ref[...] = _unstack_heads(acc, lane_head, hpb).astype(dq_ref.dtype)

    blk = lambda sec: pl.BlockSpec((None, TQ, LANES), lambda b, p, i: (b, i, sec * P + p))
    full = lambda sec: pl.BlockSpec((None, S, LANES), lambda b, p, i: (b, 0, sec * P + p))
    one = pl.BlockSpec((None, TQ, LANES), lambda b, p, i: (b, i, p))
    acc = pl.BlockSpec((None, S, LANES), lambda b, p, i: (b, 0, p))
    return pl.pallas_call(
        body, name="fox_bwd", grid=(B, P, NQ),
        in_specs=[blk(0), full(1), full(2), blk(3),
                  pl.BlockSpec((None, TQ, H), lambda b, p, i: (b, i, 0)),
                  pl.BlockSpec((None, H, S), lambda b, p, i: (b, 0, 0)),
                  one, one, one],
        out_specs=[one, one, acc, acc, pl.BlockSpec((None, None, hpb, S), lambda b, p, i: (b, p, 0, 0))],
        out_shape=[jax.ShapeDtypeStruct((B, S, W), BF16), jax.ShapeDtypeStruct((B, S, W), BF16),
                   jax.ShapeDtypeStruct((B, S, W), F32), jax.ShapeDtypeStruct((B, S, W), F32),
                   jax.ShapeDtypeStruct((B, P, hpb, S), F32)],
        scratch_shapes=[pltpu.VMEM((S // TK, RS, TK), F32), pltpu.VMEM((S // TK, RS, TK), F32)],
        compiler_params=_params(("parallel", "parallel", "arbitrary")),
    )(proj3, proj3, proj3, proj3, cum_c, cum_t, o, lse, dy)


def _layernorm_rows(v, gamma, beta):
    mu = jnp.mean(v, axis=-1, keepdims=True)
    xc = v - mu
    rstd = lax.rsqrt(jnp.mean(xc * xc, axis=-1, keepdims=True) + EPS)
    xh = xc * rstd
    return xh, rstd, xh * gamma + beta


def _layernorm_rows_bwd(dout, xh, rstd, gamma):
    dxh = dout * gamma
    return rstd * (dxh - jnp.mean(dxh, axis=-1, keepdims=True) - xh * jnp.mean(dxh * xh, axis=-1, keepdims=True))


def _gmlp_fwd(proj, wm, bs_t, ln_g, ln_b, W):
    T = proj.shape[0]
    G = wm.shape[0]
    cg = W // G
    assert cg == LANES

    def body(p_ref, wm_ref, bs_ref, lg_ref, lb_ref, y_ref, vn_ref):
        lane = lax.broadcasted_iota(jnp.int32, (1, LANES), 1)
        _, _, vn = _layernorm_rows(_gelu(p_ref[:, W:2 * W]), lg_ref[...], lb_ref[...])
        vn_ref[...] = vn.astype(BF16)
        bs = bs_ref[...]
        for g in range(G):
            sl = pl.ds(g * cg, cg)
            s = _dot(wm_ref[g], vn_ref[:, sl]) + _pick_col(bs, g, lane)
            gate = p_ref[:, pl.ds(2 * W + g * cg, cg)]
            y_ref[:, sl] = (_gelu(p_ref[:, sl]) * s * _silu(gate)).astype(BF16)

    vec = pl.BlockSpec((1, W), lambda r: (0, 0))
    return pl.pallas_call(
        body, name="gmlp_fwd", grid=(T // BLK,),
        in_specs=[pl.BlockSpec((BLK, 3 * W), lambda r: (r, 0)),
                  pl.BlockSpec((G, BLK, BLK), lambda r: (0, 0, 0)),
                  pl.BlockSpec((BLK, LANES), lambda r: (0, 0)), vec, vec],
        out_specs=pl.BlockSpec((BLK, W), lambda r: (r, 0)),
        out_shape=jax.ShapeDtypeStruct((T, W), BF16),
        scratch_shapes=[pltpu.VMEM((BLK, W), BF16)],
        compiler_params=_params(("parallel",)),
    )(proj, wm, bs_t, ln_g, ln_b)


def _gmlp_bwd(proj, dy, wm, bs_t, ln_g, ln_b, W):
    T = proj.shape[0]
    G = wm.shape[0]
    cg = W // G

    def body(p_ref, dy_ref, wm_ref, bs_ref, lg_ref, lb_ref,
             dp_ref, dwm_ref, dbs_ref, dlg_ref, dlb_ref, vn_ref, dvn_ref):
        r = pl.program_id(0)

        @pl.when(r == 0)
        def _():
            dwm_ref[...] = jnp.zeros_like(dwm_ref)
            dbs_ref[...] = jnp.zeros_like(dbs_ref)
            dlg_ref[...] = jnp.zeros_like(dlg_ref)
            dlb_ref[...] = jnp.zeros_like(dlb_ref)

        lane = lax.broadcasted_iota(jnp.int32, (1, LANES), 1)
        vpre = p_ref[:, W:2 * W]
        gamma = lg_ref[...]
        xh, rstd, vn = _layernorm_rows(_gelu(vpre), gamma, lb_ref[...])
        vn_ref[...] = vn.astype(BF16)
        bs = bs_ref[...]
        dbs = jnp.zeros((BLK, LANES), F32)
        for g in range(G):
            sl = pl.ds(g * cg, cg)
            gsl = pl.ds(2 * W + g * cg, cg)
            vng = vn_ref[:, sl]
            s = _dot(wm_ref[g], vng) + _pick_col(bs, g, lane)
            upre = p_ref[:, sl]
            u = _gelu(upre)
            gate = p_ref[:, gsl]
            dyv = dy_ref[:, sl].astype(F32)
            dp_ref[:, gsl] = (dyv * u * s * _dsilu(gate)).astype(dp_ref.dtype)
            do = dyv * _silu(gate)
            dp_ref[:, sl] = (do * s * _dgelu(upre)).astype(dp_ref.dtype)
            ds = do * u
            dbs = dbs + jnp.where(lane == g, jnp.sum(ds, axis=1, keepdims=True), 0.0)
            dsb = ds.astype(BF16)
            dwm_ref[g] += _dot(dsb, vng, _NT)
            dvn_ref[:, sl] = _dot(wm_ref[g], dsb, _TN)
        dbs_ref[...] += dbs
        dvn = dvn_ref[...]
        dlg_ref[...] += jnp.sum(dvn * xh, axis=0, keepdims=True)
        dlb_ref[...] += jnp.sum(dvn, axis=0, keepdims=True)
        dv = _layernorm_rows_bwd(dvn, xh, rstd, gamma)
        dp_ref[:, W:2 * W] = (dv * _dgelu(vpre)).astype(dp_ref.dtype)

    vec = pl.BlockSpec((1, W), lambda r: (0, 0))
    return pl.pallas_call(
        body, name="gmlp_bwd", grid=(T // BLK,),
        in_specs=[pl.BlockSpec((BLK, 3 * W), lambda r: (r, 0)),
                  pl.BlockSpec((BLK, W), lambda r: (r, 0)),
                  pl.BlockSpec((G, BLK, BLK), lambda r: (0, 0, 0)),
                  pl.BlockSpec((BLK, LANES), lambda r: (0, 0)), vec, vec],
        out_specs=[pl.BlockSpec((BLK, 3 * W), lambda r: (r, 0)),
                   pl.BlockSpec((G, BLK, BLK), lambda r: (0, 0, 0)),
                   pl.BlockSpec((BLK, LANES), lambda r: (0, 0)), vec, vec],
        out_shape=[jax.ShapeDtypeStruct((T, 3 * W), BF16), jax.ShapeDtypeStruct((G, BLK, BLK), F32),
                   jax.ShapeDtypeStruct((BLK, LANES), F32),
                   jax.ShapeDtypeStruct((1, W), F32), jax.ShapeDtypeStruct((1, W), F32)],
        scratch_shapes=[pltpu.VMEM((BLK, W), BF16), pltpu.VMEM((BLK, W), F32)],
        compiler_params=_params(("arbitrary",)),
    )(proj, dy, wm, bs_t, ln_g, ln_b)


def _conv_taps(ext_ref, cw_ref, off, n_taps, first):
    acc = jnp.zeros((BLK, LANES), F32)
    for k in range(n_taps):
        acc = acc + cw_ref[k:k + 1, pl.ds(off, LANES)] * ext_ref[pl.ds(first + k, BLK), pl.ds(off, LANES)]
    return acc


def _fill_glu_ext(ext_ref, halo_ref, cur_ref, W, first_block):
    y0h = halo_ref[:, :W] * _sigmoid(halo_ref[:, W:])
    ext_ref[0:CONV_HALO, :] = jnp.where(first_block, 0.0, y0h)
    ext_ref[CONV_HALO:CONV_HALO + BLK, :] = cur_ref[:, :W] * _sigmoid(cur_ref[:, W:])


def _conv_specs(S, W):
    per = BLK // CONV_HALO
    cur = pl.BlockSpec((None, BLK, 2 * W), lambda b, i: (b, i, 0))
    halo = pl.BlockSpec((None, CONV_HALO, 2 * W), lambda b, i: (b, jnp.maximum(i * per - 1, 0), 0))
    gate = pl.BlockSpec((None, BLK, W), lambda b, i: (b, i, 2))
    return cur, halo, gate


def _conv_fwd(proj3, cw, cb, ln_g, ln_b, W):
    B, S, _ = proj3.shape
    K = cw.shape[0]
    first = CONV_HALO - (K - 1)
    assert first >= 0

    def body(cur_ref, halo_ref, g_ref, cw_ref, cb_ref, lg_ref, lb_ref, y_ref, ext_ref, y1_ref):
        i = pl.program_id(1)
        _fill_glu_ext(ext_ref, halo_ref, cur_ref, W, i == 0)

        def chan(c, _):
            off = pl.multiple_of(c * LANES, LANES)
            y1_ref[:, pl.ds(off, LANES)] = _conv_taps(ext_ref, cw_ref, off, K, first) + cb_ref[:, pl.ds(off, LANES)]
            return 0

        lax.fori_loop(0, W // LANES, chan, 0)
        _, _, ln = _layernorm_rows(y1_ref[...], lg_ref[...], lb_ref[...])
        y_ref[...] = (_silu(ln) * _silu(g_ref[...])).astype(BF16)

    cur, halo, gate = _conv_specs(S, W)
    vec = pl.BlockSpec((1, W), lambda b, i: (0, 0))
    return pl.pallas_call(
        body, name="conv_fwd", grid=(B, S // BLK),
        in_specs=[cur, halo, gate, pl.BlockSpec((K, W), lambda b, i: (0, 0)), vec, vec, vec],
        out_specs=pl.BlockSpec((None, BLK, W), lambda b, i: (b, i, 0)),
        out_shape=jax.ShapeDtypeStruct((B, S, W), BF16),
        scratch_shapes=[pltpu.VMEM((CONV_HALO + BLK, W), F32), pltpu.VMEM((BLK, W), F32)],
        compiler_params=_params(("parallel", "parallel")),
    )(proj3, proj3, proj3, cw, cb, ln_g, ln_b)


def _conv_bwd1(proj3, dy, cw, cb, ln_g, ln_b, W, exch):
    B, S, _ = proj3.shape
    K = cw.shape[0]
    first = CONV_HALO - (K - 1)

    def body(cur_ref, halo_ref, g_ref, dy_ref, cw_ref, cb_ref, lg_ref, lb_ref,
             dy1_ref, dg_ref, dcw_ref, dcb_ref, dlg_ref, dlb_ref, ext_ref, y1_ref):
        b = pl.program_id(0)
        i = pl.program_id(1)

        @pl.when(jnp.logical_and(b == 0, i == 0))
        def _():
            dcw_ref[...] = jnp.zeros_like(dcw_ref)
            dcb_ref[...] = jnp.zeros_like(dcb_ref)
            dlg_ref[...] = jnp.zeros_like(dlg_ref)
            dlb_ref[...] = jnp.zeros_like(dlb_ref)

        _fill_glu_ext(ext_ref, halo_ref, cur_ref, W, i == 0)

        def chan(c, _):
            off = pl.multiple_of(c * LANES, LANES)
            y1_ref[:, pl.ds(off, LANES)] = _conv_taps(ext_ref, cw_ref, off, K, first) + cb_ref[:, pl.ds(off, LANES)]
            return 0

        lax.fori_loop(0, W // LANES, chan, 0)
        gamma = lg_ref[...]
        xh, rstd, ln = _layernorm_rows(y1_ref[...], gamma, lb_ref[...])
        g = g_ref[...]
        dyv = dy_ref[...].astype(F32)
        dg_ref[...] = (dyv * _silu(ln) * _dsilu(g)).astype(dg_ref.dtype)
        dln = dyv * _silu(g) * _dsilu(ln)
        dlg_ref[...] += jnp.sum(dln * xh, axis=0, keepdims=True)
        dlb_ref[...] += jnp.sum(dln, axis=0, keepdims=True)
        dy1 = _layernorm_rows_bwd(dln, xh, rstd, gamma)
        dy1_ref[...] = dy1
        dcb_ref[...] += jnp.sum(dy1, axis=0, keepdims=True)

        def chan_w(c, _):
            off = pl.multiple_of(c * LANES, LANES)
            d = dy1_ref[:, pl.ds(off, LANES)]
            for k in range(K):
                dcw_ref[k:k + 1, pl.ds(off, LANES)] += jnp.sum(
                    d * ext_ref[pl.ds(first + k, BLK), pl.ds(off, LANES)], axis=0, keepdims=True)
            return 0

        lax.fori_loop(0, W // LANES, chan_w, 0)

    cur, halo, gate = _conv_specs(S, W)
    vec = pl.BlockSpec((1, W), lambda b, i: (0, 0))
    taps = pl.BlockSpec((K, W), lambda b, i: (0, 0))
    one = pl.BlockSpec((None, BLK, W), lambda b, i: (b, i, 0))
    return _call_hosting(
        body, exch, "conv_bwd1", (B, S // BLK), [cur, halo, gate, one, taps, vec, vec, vec],
        [one, one, taps, vec, vec, vec],
        [jax.ShapeDtypeStruct((B, S, W), F32), jax.ShapeDtypeStruct((B, S, W), BF16),
         jax.ShapeDtypeStruct((K, W), F32)] + [jax.ShapeDtypeStruct((1, W), F32)] * 3,
        [pltpu.VMEM((CONV_HALO + BLK, W), F32), pltpu.VMEM((BLK, W), F32)],
        (proj3, proj3, proj3, dy, cw, cb, ln_g, ln_b))


def _conv_bwd2(proj3, dy1, dgate, cw_rev, W):
    B, S, _ = proj3.shape
    K = cw_rev.shape[0]
    NQ = S // BLK
    per = BLK // CONV_HALO

    def body(cur_ref, d_ref, dnext_ref, dgate_ref, cw_ref, dp_ref, ext_ref, dy0_ref):
        i = pl.program_id(1)
        ext_ref[0:BLK, :] = d_ref[...]
        ext_ref[BLK:BLK + CONV_HALO, :] = jnp.where(i == NQ - 1, 0.0, dnext_ref[...])

        def chan(c, _):
            off = pl.multiple_of(c * LANES, LANES)
            dy0_ref[:, pl.ds(off, LANES)] = _conv_taps(ext_ref, cw_ref, off, K, 0)
            return 0

        lax.fori_loop(0, W // LANES, chan, 0)
        a = cur_ref[:, :W]
        sg = _sigmoid(cur_ref[:, W:])
        dy0 = dy0_ref[...]
        dp_ref[:, 0:W] = (dy0 * sg).astype(dp_ref.dtype)
        dp_ref[:, W:2 * W] = (dy0 * a * sg * (1.0 - sg)).astype(dp_ref.dtype)
        dp_ref[:, 2 * W:3 * W] = dgate_ref[...]

    cur = pl.BlockSpec((None, BLK, 2 * W), lambda b, i: (b, i, 0))
    one = pl.BlockSpec((None, BLK, W), lambda b, i: (b, i, 0))
    nxt = pl.BlockSpec((None, CONV_HALO, W), lambda b, i: (b, jnp.minimum((i + 1) * per, S // CONV_HALO - 1), 0))
    return pl.pallas_call(
        body, name="conv_bwd2", grid=(B, NQ),
        in_specs=[cur, one, nxt, one, pl.BlockSpec((K, W), lambda b, i: (0, 0))],
        out_specs=pl.BlockSpec((None, BLK, 3 * W), lambda b, i: (b, i, 0)),
        out_shape=jax.ShapeDtypeStruct((B, S, 3 * W), BF16),
        scratch_shapes=[pltpu.VMEM((BLK + CONV_HALO, W), F32), pltpu.VMEM((BLK, W), F32)],
        compiler_params=_params(("parallel", "parallel")),
    )(proj3, dy1, dy1, dgate, cw_rev)


def _pack(arrays):
    flat = jnp.concatenate([a.astype(F32).reshape(-1) for a in arrays])
    n = flat.shape[0]
    pad = (-n) % (8 * LANES)
    if pad:
        flat = jnp.concatenate([flat, jnp.zeros((pad,), F32)])
    return flat.reshape(-1, LANES)


def _unpack(packed, shapes, lead=()):
    flat = packed.reshape(lead + (-1,))
    out, off = [], 0
    for shp in shapes:
        n = math.prod(shp)
        out.append(flat[..., off:off + n].reshape(lead + tuple(shp)))
        off += n
    return out


def _cols_from_dev(g):
    g = jnp.moveaxis(g, 0, -2)
    return g.reshape(g.shape[:-2] + (g.shape[-2] * g.shape[-1],))


def _my_cols(full, me):
    n8 = full.shape[-1] // N_DEV
    return lax.dynamic_slice_in_dim(full, me * n8, n8, axis=full.ndim - 1)


def kernel(x, a_norm, a_w_in, a_w_out, b_norm, b_w_in, b_v_ln_g, b_v_ln_b, b_w_s, b_b_s, b_w_out, c_norm, c_w_in, c_conv_w, c_conv_b, c_ln_g, c_ln_b, c_w_out, d_norm, d_w_in, d_b_f, d_w_out, final_norm, loss_target, m_a_norm, m_a_w_in, m_a_w_out, m_b_norm, m_b_w_in, m_b_v_ln_g, m_b_v_ln_b, m_b_w_s, m_b_b_s, m_b_w_out, m_c_norm, m_c_w_in, m_c_conv_w, m_c_conv_b, m_c_ln_g, m_c_ln_b, m_c_w_out, m_d_norm, m_d_w_in, m_d_b_f, m_d_w_out, m_final_norm, v_a_norm, v_a_w_in, v_a_w_out, v_b_norm, v_b_w_in, v_b_v_ln_g, v_b_v_ln_b, v_b_w_s, v_b_b_s, v_b_w_out, v_c_norm, v_c_w_in, v_c_conv_w, v_c_conv_b, v_c_ln_g, v_c_ln_b, v_c_w_out, v_d_norm, v_d_w_in, v_d_b_f, v_d_w_out, v_final_norm):
    B, S, D = x.shape
    T = B * S
    xi, yi, ci = _me()
    me = 4 * xi + 2 * yi + ci

    G = b_w_s.shape[1]
    KC = c_conv_w.shape[1]
    H_D = d_b_f.shape[1]
    W_A = a_w_out.shape[1] * N_DEV
    W_B = b_w_out.shape[1] * N_DEV
    W_C = c_w_out.shape[1] * N_DEV
    W_D = d_w_out.shape[1] * N_DEV
    N_D = d_w_in.shape[2] * N_DEV
    N_D_PAD = -(-N_D // (3 * LANES)) * (3 * LANES)

    big_names = ["a_w_in", "a_w_out", "b_w_in", "b_w_out", "c_w_in", "c_w_out", "d_w_in", "d_w_out"]
    big_w = dict(a_w_in=a_w_in[0], a_w_out=a_w_out[0], b_w_in=b_w_in[0], b_w_out=b_w_out[0],
                 c_w_in=c_w_in[0], c_w_out=c_w_out[0], d_w_in=d_w_in[0], d_w_out=d_w_out[0])
    small_sharded = [b_norm, b_v_ln_g, b_v_ln_b, c_norm, c_conv_w, c_conv_b, c_ln_g, c_ln_b, d_norm]
    first_names, later_names = big_names[:2], big_names[2:]
    gathered = _Exchange([big_w[n].astype(BF16) for n in first_names] + [_pack(small_sharded)],
                         ["gather"] * (len(first_names) + 1)).run("gather_first")
    wg = dict(zip(first_names, gathered[:-1]))
    (b_norm_f, b_lg_f, b_lb_f, c_norm_f, c_cw_f, c_cb_f, c_lg_f, c_lb_f, d_norm_f) = [
        _cols_from_dev(t) for t in _unpack(gathered[-1], [s.shape for s in small_sharded], lead=(N_DEV,))]
    c_cw_f = c_cw_f[0]
    a_w_out_f = wg["a_w_out"].reshape(W_A, D)

    wm = jnp.tril(b_w_s[0]).astype(BF16)
    bs_t = jnp.pad(b_b_s[0].T, ((0, 0), (0, LANES - G)))

    x0 = x.reshape(T, D)
    h_a = _rmsnorm_fwd(x0, a_norm, "rms_a")
    proj_a = _mm_w_dev(h_a, wg["a_w_in"], "proj_a").reshape(B, S, 4 * W_A)
    (o_a, y_a), later = _sb_fwd(proj_a, W_A, SB_HEADS,
                                _Exchange([big_w[n].astype(BF16) for n in later_names], ["gather"] * len(later_names)))
    wg.update(zip(later_names, later))
    b_w_out_f = wg["b_w_out"].reshape(W_B, D)
    c_w_out_f = wg["c_w_out"].reshape(W_C, D)
    d_w_out_f = wg["d_w_out"].reshape(W_D, D)
    d_w_in_f = jnp.pad(_cols_from_dev(wg["d_w_in"]), ((0, 0), (0, N_D_PAD - N_D)))
    y_a = y_a.reshape(T, W_A)
    x1 = x0 + _mm(y_a, a_w_out_f, "nn", T, D, W_A, F32, "out_a", 512, D, W_A)
    h_b = _rmsnorm_fwd(x1, b_norm_f, "rms_b")
    proj_b = _mm_w_dev(h_b, wg["b_w_in"], "proj_b")
    y_b = _gmlp_fwd(proj_b, wm, bs_t, b_lg_f, b_lb_f, W_B)
    x2 = x1 + _mm(y_b, b_w_out_f, "nn", T, D, W_B, F32, "out_b", 512, D, W_B)
    h_c = _rmsnorm_fwd(x2, c_norm_f, "rms_c")
    proj_c = _mm_w_dev(h_c, wg["c_w_in"], "proj_c").reshape(B, S, 3 * W_C)
    y_c = _conv_fwd(proj_c, c_cw_f, c_cb_f, c_lg_f, c_lb_f, W_C).reshape(T, W_C)
    x3 = x2 + _mm(y_c, c_w_out_f, "nn", T, D, W_C, F32, "out_c", 512, D, W_C)
    h_d = _rmsnorm_fwd(x3, d_norm_f, "rms_d")
    proj_d = _mm(h_d, d_w_in_f, "nn", T, N_D_PAD, D, F32, "proj_d", 1024, 384, D).reshape(B, S, N_D_PAD)
    f_t = jnp.swapaxes(proj_d[:, :, 4 * W_D:4 * W_D + H_D], 1, 2)
    b_f_col = d_b_f.reshape(H_D, 1)
    cum_t = _fox_gate_fwd(f_t, b_f_col)
    cum_c = jnp.swapaxes(cum_t, 1, 2)
    o_d, y_d, lse_d = _fox_fwd(proj_d, cum_c, cum_t, W_D, H_D)
    y_d = y_d.reshape(T, W_D)
    x4 = x3 + _mm(y_d, d_w_out_f, "nn", T, D, W_D, F32, "out_d", 512, D, W_D)

    loss_part, dx, g_final = _loss_head(x4, final_norm.reshape(1, D), loss_target.reshape(T, D))
    loss = lax.psum(loss_part[0, 0], MESH_AXES)

    dy_d = _mm(dx, d_w_out_f, "nt", T, W_D, D, BF16, "dy_d", 512, W_D, D).reshape(B, S, W_D)
    gw_d_out = _mm(y_d, dx, "tn", W_D, D, T, BF16, "gw_d_out", W_D, D, 512).reshape(N_DEV, W_D // N_DEV, D)
    dq, dg, dk, dv, dcum = _fox_bwd(proj_d, cum_c, cum_t, o_d, lse_d, dy_d, W_D, H_D)
    df_t, g_b_f = _fox_gate_bwd(dcum.reshape(B, H_D, S), f_t, b_f_col)
    dproj_d = jnp.concatenate(
        [dq, dk.astype(BF16), dv.astype(BF16), dg, jnp.swapaxes(df_t, 1, 2).astype(BF16),
         jnp.zeros((B, S, N_D_PAD - N_D), BF16)], axis=-1).reshape(T, N_D_PAD)
    gw_d_in_full = _mm(h_d, dproj_d, "tn", D, N_D_PAD, T, BF16, "gw_d_in", D, 384, 512)
    gw_d_in = jnp.moveaxis(gw_d_in_full[:, :N_D].reshape(D, N_DEV, N_D // N_DEV), 1, 0)
    dh = _mm(dproj_d, d_w_in_f, "nt", T, D, N_D_PAD, F32, "dh_d", 512, D, N_D_PAD // 3)
    dx, g_d_norm = _rmsnorm_bwd(x3, d_norm_f, dh, dx, "rms_bwd_d")

    dy_c = _mm(dx, c_w_out_f, "nt", T, W_C, D, BF16, "dy_c", 512, W_C, D).reshape(B, S, W_C)
    gw_c_out = _mm(y_c, dx, "tn", W_C, D, T, BF16, "gw_c_out", 1024, D, 512).reshape(N_DEV, W_C // N_DEV, D)
    (dy1, dgate_c, g_c_cw, g_c_cb, g_c_lg, g_c_lb), parts_d = _conv_bwd1(
        proj_c, dy_c, c_cw_f, c_cb_f, c_lg_f, c_lb_f, W_C, _Exchange([gw_d_in, gw_d_out], ["scatter"] * 2))
    dproj_c = _conv_bwd2(proj_c, dy1, dgate_c, c_cw_f[::-1], W_C).reshape(T, 3 * W_C)
    gw_c_in = _mm_grad_dev(h_c, dproj_c, "gw_c_in")
    dh = _mm_wT_dev(dproj_c, wg["c_w_in"], "dh_c")
    dx, g_c_norm = _rmsnorm_bwd(x2, c_norm_f, dh, dx, "rms_bwd_c")

    dy_b = _mm(dx, b_w_out_f, "nt", T, W_B, D, BF16, "dy_b", 512, W_B, D)
    gw_b_out = _mm(y_b, dx, "tn", W_B, D, T, BF16, "gw_b_out", 1024, D, 512).reshape(N_DEV, W_B // N_DEV, D)
    dproj_b, g_wm, g_bs_t, g_b_lg, g_b_lb = _gmlp_bwd(proj_b, dy_b, wm, bs_t, b_lg_f, b_lb_f, W_B)
    g_b_w_s = jnp.tril(g_wm)
    g_b_b_s = g_bs_t[:, :G].T
    gw_b_in = _mm_grad_dev(h_b, dproj_b, "gw_b_in")
    dh = _mm_wT_dev(dproj_b, wg["b_w_in"], "dh_b")
    dx, g_b_norm = _rmsnorm_bwd(x1, b_norm_f, dh, dx, "rms_bwd_b")

    dy_a = _mm(dx, a_w_out_f, "nt", T, W_A, D, BF16, "dy_a", 512, W_A, D).reshape(B, S, W_A)
    gw_a_out = _mm(y_a, dx, "tn", W_A, D, T, BF16, "gw_a_out", W_A, D, 512).reshape(N_DEV, W_A // N_DEV, D)
    (dq, dg, dk, dv), parts_cb = _sb_bwd(
        proj_a, o_a, dy_a, W_A, SB_HEADS, _Exchange([gw_c_in, gw_c_out, gw_b_in, gw_b_out], ["scatter"] * 4))
    dproj_a = jnp.concatenate([dq, dk.astype(BF16), dv.astype(BF16), dg], axis=-1).reshape(T, 4 * W_A)
    gw_a_in = _mm_grad_dev(h_a, dproj_a, "gw_a_in")
    dh = _mm_wT_dev(dproj_a, wg["a_w_in"], "dh_a")
    dx, g_a_norm = _rmsnorm_bwd(x0, a_norm, dh, dx, "rms_bwd_a")
    grad_x = dx.reshape(B, S, D)

    small_full = [g_a_norm, g_b_norm, g_b_lg, g_b_lb, g_b_w_s, g_b_b_s, g_c_norm, g_c_cw, g_c_cb, g_c_lg, g_c_lb,
                  g_d_norm, g_b_f, g_final]
    parts_a = _Exchange([gw_a_in, gw_a_out, _pack(small_full)], ["scatter", "scatter", "gather"]).run("exchange_last")
    big_parts = dict(a_w_in=parts_a[0], a_w_out=parts_a[1], b_w_in=parts_cb[2], b_w_out=parts_cb[3],
                     c_w_in=parts_cb[0], c_w_out=parts_cb[1], d_w_in=parts_d[0], d_w_out=parts_d[1])
    small_sum = _sum_parts(parts_a[-1], "sum_small")
    (s_a_norm, s_b_norm, s_b_lg, s_b_lb, s_b_w_s, s_b_b_s, s_c_norm, s_c_cw, s_c_cb, s_c_lg, s_c_lb,
     s_d_norm, s_b_f, s_final) = _unpack(small_sum, [g.shape for g in small_full])

    weights = dict(a_norm=a_norm, a_w_in=a_w_in, a_w_out=a_w_out, b_norm=b_norm, b_w_in=b_w_in, b_v_ln_g=b_v_ln_g,
                   b_v_ln_b=b_v_ln_b, b_w_s=b_w_s, b_b_s=b_b_s, b_w_out=b_w_out, c_norm=c_norm, c_w_in=c_w_in,
                   c_conv_w=c_conv_w, c_conv_b=c_conv_b, c_ln_g=c_ln_g, c_ln_b=c_ln_b, c_w_out=c_w_out,
                   d_norm=d_norm, d_w_in=d_w_in, d_b_f=d_b_f, d_w_out=d_w_out, final_norm=final_norm)
    mom_m = dict(a_norm=m_a_norm, a_w_in=m_a_w_in, a_w_out=m_a_w_out, b_norm=m_b_norm, b_w_in=m_b_w_in,
                 b_v_ln_g=m_b_v_ln_g, b_v_ln_b=m_b_v_ln_b, b_w_s=m_b_w_s, b_b_s=m_b_b_s, b_w_out=m_b_w_out,
                 c_norm=m_c_norm, c_w_in=m_c_w_in, c_conv_w=m_c_conv_w, c_conv_b=m_c_conv_b, c_ln_g=m_c_ln_g,
                 c_ln_b=m_c_ln_b, c_w_out=m_c_w_out, d_norm=m_d_norm, d_w_in=m_d_w_in, d_b_f=m_d_b_f,
                 d_w_out=m_d_w_out, final_norm=m_final_norm)
    mom_v = dict(a_norm=v_a_norm, a_w_in=v_a_w_in, a_w_out=v_a_w_out, b_norm=v_b_norm, b_w_in=v_b_w_in,
                 b_v_ln_g=v_b_v_ln_g, b_v_ln_b=v_b_v_ln_b, b_w_s=v_b_w_s, b_b_s=v_b_b_s, b_w_out=v_b_w_out,
                 c_norm=v_c_norm, c_w_in=v_c_w_in, c_conv_w=v_c_conv_w, c_conv_b=v_c_conv_b, c_ln_g=v_c_ln_g,
                 c_ln_b=v_c_ln_b, c_w_out=v_c_w_out, d_norm=v_d_norm, d_w_in=v_d_w_in, d_b_f=v_d_b_f,
                 d_w_out=v_d_w_out, final_norm=v_final_norm)
    order = list(weights)
    grads, deltas, new_m, new_v = {}, {}, {}, {}

    for n in big_names:
        part = big_parts[n]
        shp = weights[n].shape
        R, C = shp[1], shp[2]
        res = _adamw(part, weights[n].reshape(R, C), mom_m[n].reshape(R, C), mom_v[n].reshape(R, C), "adamw_" + n)
        grads[n], deltas[n], new_m[n], new_v[n] = [r.reshape(shp) for r in res]

    small_g = dict(
        a_norm=s_a_norm, b_norm=_my_cols(s_b_norm, me), b_v_ln_g=_my_cols(s_b_lg, me),
        b_v_ln_b=_my_cols(s_b_lb, me), b_w_s=s_b_w_s[None], b_b_s=s_b_b_s[None], c_norm=_my_cols(s_c_norm, me),
        c_conv_w=_my_cols(s_c_cw, me)[None], c_conv_b=_my_cols(s_c_cb, me), c_ln_g=_my_cols(s_c_lg, me),
        c_ln_b=_my_cols(s_c_lb, me), d_norm=_my_cols(s_d_norm, me), d_b_f=s_b_f.reshape(1, H_D),
        final_norm=s_final.reshape(D))
    small_names = list(small_g)
    sg_p = _pack([small_g[n] for n in small_names])
    res = _adamw(sg_p[None], _pack([weights[n] for n in small_names]), _pack([mom_m[n] for n in small_names]),
                 _pack([mom_v[n] for n in small_names]), "adamw_small")
    shapes = [weights[n].shape for n in small_names]
    for dst, r in zip((grads, deltas, new_m, new_v), res):
        for n, val in zip(small_names, _unpack(r, shapes)):
            dst[n] = val

    return (loss, grad_x, *[grads[n] for n in order], *[deltas[n] for n in order],
            *[new_m[n] for n in order], *[new_v[n] for n in order])
```

```python
import functools
import math

import jax
import jax.numpy as jnp
from jax import lax
from jax.experimental import pallas as pl
from jax.experimental.pallas import tpu as pltpu

F32 = jnp.float32
BF16 = jnp.bfloat16

EPS = 1e-6
SB_HEADS = 16
CONV_HALO = 32
BLK = 128
ATT_TK = 256
ATT_TQ = 256
ATT_GP = 2
LANES = 128
N_DEV = 8
MESH_AXES = ("x", "y", "c")

ADAM_LR = 0.001
ADAM_B1 = 0.9
ADAM_B2 = 0.999
ADAM_EPS = 1e-08
ADAM_WD = 0.01
ADAM_STEP = 10

VMEM_LIMIT = 56 * 1024 * 1024
NEG_BIG = -1e30

_NN = (((1,), (0,)), ((), ()))
_NT = (((1,), (1,)), ((), ()))
_TN = (((0,), (0,)), ((), ()))


def _dot(a, b, dims=_NN):
    return lax.dot_general(a, b, dims, preferred_element_type=F32)


def _split_dot(x, m):
    hi = x.astype(BF16)
    lo = (x - hi.astype(F32)).astype(BF16)
    return _dot(hi, m) + _dot(lo, m)


def _split3_dot(x, m):
    hi = x.astype(BF16)
    r1 = x - hi.astype(F32)
    mid = r1.astype(BF16)
    lo = (r1 - mid.astype(F32)).astype(BF16)
    return _dot(hi, m) + _dot(mid, m) + _dot(lo, m)


def _params(sem=None):
    kw = dict(vmem_limit_bytes=VMEM_LIMIT)
    if sem is not None:
        kw["dimension_semantics"] = sem
    return pltpu.CompilerParams(**kw)


def _sigmoid(x):
    return jax.nn.sigmoid(x)


def _silu(x):
    return x * _sigmoid(x)


def _dsilu(x):
    s = _sigmoid(x)
    return s * (1.0 + x * (1.0 - s))


_GELU_C = math.sqrt(2.0 / math.pi)


def _gelu(x):
    return 0.5 * x * (1.0 + jnp.tanh(_GELU_C * (x + 0.044715 * x * x * x)))


def _dgelu(x):
    th = jnp.tanh(_GELU_C * (x + 0.044715 * x * x * x))
    return 0.5 * (1.0 + th) + 0.5 * x * (1.0 - th * th) * _GELU_C * (1.0 + 3.0 * 0.044715 * x * x)


def _mm(a, b, mode, M, N, K, out_dtype, name, tm, tn, tk, a_spec=None, b_spec=None, o_spec=None, out_shape=None,
        exch=None):
    tm, tn, tk = min(tm, M), min(tn, N), min(tk, K)
    assert M % tm == 0 and N % tn == 0 and K % tk == 0, (name, M, N, K, tm, tn, tk)
    nk = K // tk
    dims = {"nn": _NN, "nt": _NT, "tn": _TN}[mode]
    if a_spec is None:
        a_spec = (pl.BlockSpec((tk, tm), lambda i, j, k: (k, i)) if mode == "tn"
                  else pl.BlockSpec((tm, tk), lambda i, j, k: (i, k)))
    if b_spec is None:
        b_spec = (pl.BlockSpec((tn, tk), lambda i, j, k: (j, k)) if mode == "nt"
                  else pl.BlockSpec((tk, tn), lambda i, j, k: (k, j)))
    if o_spec is None:
        o_spec = pl.BlockSpec((tm, tn), lambda i, j, k: (i, j))
    if out_shape is None:
        out_shape = (M, N)

    def body(a_ref, b_ref, o_ref, acc_ref):
        k = pl.program_id(2)

        @pl.when(k == 0)
        def _():
            acc_ref[...] = jnp.zeros_like(acc_ref)

        acc_ref[...] += _dot(a_ref[...].astype(BF16), b_ref[...].astype(BF16), dims)

        @pl.when(k == nk - 1)
        def _():
            o_ref[...] = acc_ref[...].astype(o_ref.dtype)

    if exch is None:
        return pl.pallas_call(
            body, name=name, grid=(M // tm, N // tn, nk),
            in_specs=[a_spec, b_spec], out_specs=o_spec,
            out_shape=jax.ShapeDtypeStruct(out_shape, out_dtype),
            scratch_shapes=[pltpu.VMEM((tm, tn), F32)],
            compiler_params=_params(("parallel", "parallel", "arbitrary")),
        )(a, b)
    (out,), moved = _call_hosting(
        body, exch, name, (M // tm, N // tn, nk), [a_spec, b_spec], [o_spec],
        [jax.ShapeDtypeStruct(out_shape, out_dtype)], [pltpu.VMEM((tm, tn), F32)], (a, b))
    return out, moved


def _mm_w_dev(a, w3, name, out_dtype=F32, tm=1024):
    M, K = a.shape
    n8 = w3.shape[2]
    tn = n8 if n8 <= 768 else 512
    per = n8 // tn
    b_spec = pl.BlockSpec((None, K, tn), lambda i, j, k: (j // per, 0, j % per))
    return _mm(a, w3, "nn", M, N_DEV * n8, K, out_dtype, name, tm, tn, K, b_spec=b_spec)


def _mm_wT_dev(a, w3, name, out_dtype=F32, tm=512, exch=None):
    M, N = a.shape
    K, n8 = w3.shape[1], w3.shape[2]
    tk = n8 if n8 <= 768 else 512
    per = n8 // tk
    b_spec = pl.BlockSpec((None, K, tk), lambda i, j, k: (k // per, 0, k % per))
    return _mm(a, w3, "nt", M, K, N, out_dtype, name, tm, K, tk, b_spec=b_spec, exch=exch)


def _mm_grad_dev(h, d, name, out_dtype=BF16):
    T, M = h.shape
    N = d.shape[1]
    n8 = N // N_DEV
    tn = n8 if n8 <= 768 else 512
    per = n8 // tn
    tm = min(M, 1024)
    o_spec = pl.BlockSpec((None, tm, tn), lambda i, j, k: (j // per, i, j % per))
    return _mm(h, d, "tn", M, N, T, out_dtype, name, tm, tn, 512, o_spec=o_spec, out_shape=(N_DEV, M, n8))


def _me():
    x, y, c = lax.axis_index("x"), lax.axis_index("y"), lax.axis_index("c")
    return x, y, c


def _peer(r):
    x, y, c = _me()
    px = 1 - x if (r >> 2) & 1 else x
    py = 1 - y if (r >> 1) & 1 else y
    pc = 1 - c if r & 1 else c
    return (px, py, pc), 4 * px + 2 * py + pc


class _Exchange:
    def __init__(self, arrays, kinds):
        self.arrays, self.kinds, self.n = list(arrays), list(kinds), len(arrays)
        self.out_shapes = [
            jax.ShapeDtypeStruct((N_DEV,) + a.shape if kind == "gather" else a.shape, a.dtype)
            for a, kind in zip(arrays, kinds)]
        self.specs = [pl.BlockSpec(memory_space=pl.ANY)] * self.n
        self.sems = [pltpu.SemaphoreType.DMA((self.n, N_DEV - 1)), pltpu.SemaphoreType.DMA((self.n, N_DEV - 1)),
                     pltpu.SemaphoreType.DMA((self.n,))]

    def _copies(self, ins, outs, sems, receiving):
        send_sems, recv_sems, local_sems = sems
        x, y, c = _me()
        me = 4 * x + 2 * y + c

        def src(k, pid):
            return ins[k] if self.kinds[k] == "gather" else ins[k].at[pid]

        local = [pltpu.make_async_copy(src(k, me), outs[k].at[me], local_sems.at[k]) for k in range(self.n)]
        remote = []
        for r in range(1, N_DEV):
            peer, pid = _peer(r)
            for k in range(self.n):
                remote.append(pltpu.make_async_remote_copy(
                    src_ref=src(k, pid), dst_ref=outs[k].at[pid if receiving else me],
                    send_sem=send_sems.at[k, r - 1], recv_sem=recv_sems.at[k, r - 1],
                    device_id=peer, device_id_type=pl.DeviceIdType.MESH))
        return local, remote

    def start(self, ins, outs, sems):
        local, remote = self._copies(ins, outs, sems, False)
        for cp in local + remote:
            cp.start()

    def wait(self, ins, outs, sems):
        local, remote = self._copies(ins, outs, sems, True)
        for cp in remote:
            cp.wait_recv()
        for cp in remote:
            cp.wait_send()
        for cp in local:
            cp.wait()

    def run(self, name):
        n = self.n

        def body(*refs):
            ins, outs, sems = refs[:n], refs[n:2 * n], refs[2 * n:]
            self.start(ins, outs, sems)
            self.wait(ins, outs, sems)

        return pl.pallas_call(
            body, name=name, in_specs=self.specs, out_specs=self.specs, out_shape=self.out_shapes,
            scratch_shapes=self.sems,
        )(*self.arrays)


def _call_hosting(body, exch, name, grid, in_specs, out_specs, out_shape, scratch_shapes, args):
    if exch is None:
        res = pl.pallas_call(
            body, name=name, grid=grid, in_specs=list(in_specs), out_specs=list(out_specs),
            out_shape=list(out_shape), scratch_shapes=list(scratch_shapes),
            compiler_params=_params(("arbitrary",) * len(grid)))(*args)
        return res, []
    n_in, n_out, n_scr, nc = len(in_specs), len(out_specs), len(scratch_shapes), exch.n

    def full_body(*refs):
        ins, refs = refs[:n_in], refs[n_in:]
        cins, refs = refs[:nc], refs[nc:]
        outs, refs = refs[:n_out], refs[n_out:]
        couts, refs = refs[:nc], refs[nc:]
        scr, sems = refs[:n_scr], refs[n_scr:]
        ids = [pl.program_id(a) for a in range(len(grid))]
        first = functools.reduce(jnp.logical_and, [i == 0 for i in ids])
        last = functools.reduce(jnp.logical_and, [i == g - 1 for i, g in zip(ids, grid)])

        @pl.when(first)
        def _():
            exch.start(cins, couts, sems)

        body(*ins, *outs, *scr)

        @pl.when(last)
        def _():
            exch.wait(cins, couts, sems)

    res = pl.pallas_call(
        full_body, name=name, grid=grid,
        in_specs=list(in_specs) + exch.specs, out_specs=list(out_specs) + exch.specs,
        out_shape=list(out_shape) + exch.out_shapes,
        scratch_shapes=list(scratch_shapes) + exch.sems,
        compiler_params=_params(("arbitrary",) * len(grid)),
    )(*args, *exch.arrays)
    return res[:n_out], res[n_out:]


def _rmsnorm_fwd(x, g, name):
    T, D = x.shape
    tr = min(256, T)

    def body(x_ref, g_ref, h_ref):
        xv = x_ref[...]
        r = lax.rsqrt(jnp.mean(xv * xv, axis=-1, keepdims=True) + EPS)
        h_ref[...] = (xv * r * g_ref[...]).astype(BF16)

    return pl.pallas_call(
        body, name=name, grid=(T // tr,),
        in_specs=[pl.BlockSpec((tr, D), lambda i: (i, 0)), pl.BlockSpec((1, D), lambda i: (0, 0))],
        out_specs=pl.BlockSpec((tr, D), lambda i: (i, 0)),
        out_shape=jax.ShapeDtypeStruct((T, D), BF16),
        compiler_params=_params(("parallel",)),
    )(x, g)


def _rmsnorm_bwd(x, g, dh, dres, name):
    T, D = x.shape
    tr = min(256, T)

    def body(x_ref, g_ref, dh_ref, dres_ref, dx_ref, dg_ref):
        i = pl.program_id(0)
        xv = x_ref[...]
        r = lax.rsqrt(jnp.mean(xv * xv, axis=-1, keepdims=True) + EPS)
        xh = xv * r
        dhv = dh_ref[...]
        dxh = dhv * g_ref[...]
        dx_ref[...] = dres_ref[...] + r * (dxh - xh * jnp.mean(dxh * xh, axis=-1, keepdims=True))

        @pl.when(i == 0)
        def _():
            dg_ref[...] = jnp.zeros_like(dg_ref)

        dg_ref[...] += jnp.sum(dhv * xh, axis=0, keepdims=True)

    row = pl.BlockSpec((tr, D), lambda i: (i, 0))
    vec = pl.BlockSpec((1, D), lambda i: (0, 0))
    return pl.pallas_call(
        body, name=name, grid=(T // tr,),
        in_specs=[row, vec, row, row], out_specs=[row, vec],
        out_shape=[jax.ShapeDtypeStruct((T, D), F32), jax.ShapeDtypeStruct((1, D), F32)],
        compiler_params=_params(("arbitrary",)),
    )(x, g, dh, dres)


def _loss_head(x, g, target):
    T, D = x.shape
    tr = min(256, T)

    def body(x_ref, g_ref, t_ref, loss_ref, dx_ref, dg_ref):
        i = pl.program_id(0)
        xv = x_ref[...]
        gv = g_ref[...]
        r = lax.rsqrt(jnp.mean(xv * xv, axis=-1, keepdims=True) + EPS)
        xh = xv * r
        diff = xh * gv - t_ref[...]
        dy = diff * (1.0 / D)
        dxh = dy * gv
        dx_ref[...] = r * (dxh - xh * jnp.mean(dxh * xh, axis=-1, keepdims=True))

        @pl.when(i == 0)
        def _():
            dg_ref[...] = jnp.zeros_like(dg_ref)
            loss_ref[...] = jnp.zeros_like(loss_ref)

        dg_ref[...] += jnp.sum(dy * xh, axis=0, keepdims=True)
        part = jnp.sum(jnp.sum(diff * diff, axis=1, keepdims=True), axis=0, keepdims=True)
        loss_ref[...] += (0.5 / D) * part

    row = pl.BlockSpec((tr, D), lambda i: (i, 0))
    vec = pl.BlockSpec((1, D), lambda i: (0, 0))
    return pl.pallas_call(
        body, name="loss_head", grid=(T // tr,),
        in_specs=[row, vec, row],
        out_specs=[pl.BlockSpec((1, 1), lambda i: (0, 0)), row, vec],
        out_shape=[jax.ShapeDtypeStruct((1, 1), F32), jax.ShapeDtypeStruct((T, D), F32),
                   jax.ShapeDtypeStruct((1, D), F32)],
        compiler_params=_params(("arbitrary",)),
    )(x, g, target)


def _adamw(parts, w, m, v, name):
    P, R, C = parts.shape
    tr = R
    for cand in (128, 64, 32, 16, 8):
        if R % cand == 0:
            tr = cand
            break

    def body(p_ref, w_ref, m_ref, v_ref, g_out, d_out, m_out, v_out):
        g = p_ref[0].astype(F32)
        for p in range(1, P):
            g = g + p_ref[p].astype(F32)
        wv = w_ref[...]
        mn = ADAM_B1 * m_ref[...] + (1.0 - ADAM_B1) * g
        vn = ADAM_B2 * v_ref[...] + (1.0 - ADAM_B2) * (g * g)
        m_hat = mn / (1.0 - ADAM_B1 ** ADAM_STEP)
        v_hat = vn / (1.0 - ADAM_B2 ** ADAM_STEP)
        g_out[...] = g
        d_out[...] = -ADAM_LR * (m_hat / (jnp.sqrt(v_hat) + ADAM_EPS) + ADAM_WD * wv)
        m_out[...] = mn
        v_out[...] = vn

    row = pl.BlockSpec((tr, C), lambda i: (i, 0))
    return pl.pallas_call(
        body, name=name, grid=(R // tr,),
        in_specs=[pl.BlockSpec((P, tr, C), lambda i: (0, i, 0)), row, row, row],
        out_specs=[row, row, row, row],
        out_shape=[jax.ShapeDtypeStruct((R, C), F32)] * 4,
        compiler_params=_params(("parallel",)),
    )(parts, w, m, v)


def _sum_parts(parts, name):
    P, R, C = parts.shape
    tr = 128 if R % 128 == 0 else R

    def body(p_ref, o_ref):
        g = p_ref[0]
        for p in range(1, P):
            g = g + p_ref[p]
        o_ref[...] = g

    return pl.pallas_call(
        body, name=name, grid=(R // tr,),
        in_specs=[pl.BlockSpec((P, tr, C), lambda i: (0, i, 0))],
        out_specs=pl.BlockSpec((tr, C), lambda i: (i, 0)),
        out_shape=jax.ShapeDtypeStruct((R, C), F32),
        compiler_params=_params(("parallel",)),
    )(parts)


def _lane_head(Dh):
    assert Dh & (Dh - 1) == 0 and Dh <= LANES
    return lax.shift_right_logical(lax.broadcasted_iota(jnp.int32, (1, LANES), 1), Dh.bit_length() - 1)


def _stack_heads(x, lane_head, hpb):
    return jnp.concatenate([jnp.where(lane_head == h, x, 0.0) for h in range(hpb)], axis=0)


def _unstack_heads(acc, lane_head, hpb):
    TQ = acc.shape[0] // hpb
    out = acc[0:TQ]
    for h in range(1, hpb):
        out = jnp.where(lane_head == h, acc[h * TQ:(h + 1) * TQ], out)
    return out


def _key_tile(S):
    return ATT_TK if S % ATT_TK == 0 else BLK


def _query_tile(S):
    return ATT_TQ if S % ATT_TQ == 0 else BLK


def _lane_groups(P):
    return ATT_GP if P % ATT_GP == 0 else 1


def _lanes(u):
    return slice(u * LANES, (u + 1) * LANES)


def _causal_iotas(RS, TK, TQ):
    assert TQ & (TQ - 1) == 0 and TK % TQ == 0
    trow = jnp.bitwise_and(lax.broadcasted_iota(jnp.int32, (RS, TK), 0), TQ - 1)
    col = lax.broadcasted_iota(jnp.int32, (RS, TK), 1)
    return trow, col


def _tri(TK, op):
    r = lax.broadcasted_iota(jnp.int32, (TK, TK), 0)
    c = lax.broadcasted_iota(jnp.int32, (TK, TK), 1)
    return op(r, c).astype(BF16)


def _logsig_parts(z):
    sp = jnp.log(1.0 + jnp.exp(-jnp.abs(z)))
    return jnp.minimum(z, 0.0) - sp, -jnp.maximum(z, 0.0) - sp


def _sb_fwd(proj3, W, heads, exch):
    B, S, _ = proj3.shape
    Dh = W // heads
    hpb = LANES // Dh
    P, TQ = W // LANES, _query_tile(S)
    NQ = S // TQ
    scale = 1.0 / math.sqrt(Dh)

    TK = _key_tile(S)
    RS = hpb * TQ
    GP = _lane_groups(P)
    PG = P // GP

    def body(q_ref, k_ref, v_ref, g_ref, o_ref, y_ref):
        i = pl.program_id(2)
        lane_head = _lane_head(Dh)
        trow, col = _causal_iotas(RS, TK, TQ)
        msuf = _tri(TK, lambda r, c: r > c)
        qs = [(_stack_heads(q_ref[:, _lanes(u)], lane_head, hpb) * scale).astype(BF16) for u in range(GP)]
        nt = (i * TQ) // TK + 1

        def tile(jt, carry, masked):
            off = pl.multiple_of(jt * TK, TK)
            if masked:
                msk = col + (jt * TK - i * TQ) < trow
            out = []
            for u, (rem, acc) in enumerate(carry):
                kj = k_ref[pl.ds(off, TK), _lanes(u)].astype(BF16)
                vj = v_ref[pl.ds(off, TK), _lanes(u)].astype(BF16)
                lb, lr = _logsig_parts(_dot(qs[u], kj, _NT))
                if masked:
                    lr = jnp.where(msk, lr, 0.0)
                w = jnp.exp(lb + _split_dot(lr, msuf) + rem)
                if masked:
                    w = jnp.where(msk, w, 0.0)
                out.append((rem + jnp.sum(lr, axis=1, keepdims=True), acc + _dot(w.astype(BF16), vj)))
            return tuple(out)

        zero = (jnp.zeros((RS, 1), F32), jnp.zeros((RS, LANES), F32))
        carry = tile(nt - 1, (zero,) * GP, True)
        carry = lax.fori_loop(1, nt, lambda jj, c: tile(nt - 1 - jj, c, False), carry)
        for u in range(GP):
            o = _unstack_heads(carry[u][1], lane_head, hpb)
            o_ref[:, _lanes(u)] = o
            y_ref[:, _lanes(u)] = (o * _silu(g_ref[:, _lanes(u)])).astype(BF16)

    LW = GP * LANES
    blk = lambda sec: pl.BlockSpec((None, TQ, LW), lambda b, p, i: (b, i, sec * PG + p))
    full = lambda sec: pl.BlockSpec((None, S, LW), lambda b, p, i: (b, 0, sec * PG + p))
    out = pl.BlockSpec((None, TQ, LW), lambda b, p, i: (b, i, p))
    return _call_hosting(
        body, exch, "sb_fwd", (B, PG, NQ), [blk(0), full(1), full(2), blk(3)], [out, out],
        [jax.ShapeDtypeStruct((B, S, W), F32), jax.ShapeDtypeStruct((B, S, W), BF16)], [],
        (proj3, proj3, proj3, proj3))


def _sb_bwd(proj3, o, dy, W, heads, exch):
    B, S, _ = proj3.shape
    Dh = W // heads
    hpb = LANES // Dh
    P, TQ = W // LANES, _query_tile(S)
    NQ = S // TQ
    scale = 1.0 / math.sqrt(Dh)

    TK = _key_tile(S)
    RS = hpb * TQ

    def body(q_ref, k_ref, v_ref, g_ref, o_ref, dy_ref, dq_ref, dg_ref, dk_ref, dv_ref, u_ref, sig_ref, es_ref):
        i = pl.program_id(2)

        @pl.when(i == 0)
        def _():
            dk_ref[...] = jnp.zeros_like(dk_ref)
            dv_ref[...] = jnp.zeros_like(dv_ref)

        lane_head = _lane_head(Dh)
        trow, col = _causal_iotas(RS, TK, TQ)
        msuf = _tri(TK, lambda r, c: r > c)
        mpre = _tri(TK, lambda r, c: r < c)
        g = g_ref[...]
        dyv = dy_ref[...].astype(F32)
        dg_ref[...] = (dyv * o_ref[...] * _dsilu(g)).astype(dg_ref.dtype)
        qs = (_stack_heads(q_ref[...], lane_head, hpb) * scale).astype(BF16)
        dos = _stack_heads(dyv * _silu(g), lane_head, hpb).astype(BF16)
        nt = (i * TQ) // TK + 1

        def weights(jt, rem, masked):
            off = pl.multiple_of(jt * TK, TK)
            kj = k_ref[pl.ds(off, TK), :].astype(BF16)
            vj = v_ref[pl.ds(off, TK), :].astype(BF16)
            lb, lr = _logsig_parts(_dot(qs, kj, _NT))
            if masked:
                msk = col + (jt * TK - i * TQ) < trow
                lr = jnp.where(msk, lr, 0.0)
            w = jnp.exp(lb + _split_dot(lr, msuf) + rem)
            if masked:
                w = jnp.where(msk, w, 0.0)
            e = w * _dot(dos, vj, _NT)
            sig = jnp.exp(lb)
            u = e * (1.0 - sig) - _split_dot(e, mpre) * sig
            if masked:
                u = jnp.where(msk, u, 0.0)
                sig = jnp.where(msk, sig, 0.0)
            u_ref[jt] = u
            sig_ref[jt] = sig
            es_ref[jt] = jnp.sum(e, axis=1, keepdims=True)
            dv_ref[pl.ds(off, TK), :] += _dot(w.astype(BF16), dos, _TN)
            return rem + jnp.sum(lr, axis=1, keepdims=True)

        rem = weights(nt - 1, jnp.zeros((RS, 1), F32), True)
        lax.fori_loop(1, nt, lambda jj, r: weights(nt - 1 - jj, r, False), rem)

        def grads(jt, carry):
            pre, acc = carry
            off = pl.multiple_of(jt * TK, TK)
            kj = k_ref[pl.ds(off, TK), :].astype(BF16)
            dz = (u_ref[jt] - pre * sig_ref[jt]).astype(BF16)
            dk_ref[pl.ds(off, TK), :] += _dot(dz, qs, _TN)
            return pre + es_ref[jt], acc + _dot(dz, kj)

        _, acc = lax.fori_loop(0, nt, grads, (jnp.zeros((RS, 1), F32), jnp.zeros((RS, LANES), F32)))
        dq_ref[...] = (_unstack_heads(acc, lane_head, hpb) * scale).astype(dq_ref.dtype)

    blk = lambda sec: pl.BlockSpec((None, TQ, LANES), lambda b, p, i: (b, i, sec * P + p))
    full = lambda sec: pl.BlockSpec((None, S, LANES), lambda b, p, i: (b, 0, sec * P + p))
    one = pl.BlockSpec((None, TQ, LANES), lambda b, p, i: (b, i, p))
    acc = pl.BlockSpec((None, S, LANES), lambda b, p, i: (b, 0, p))
    return _call_hosting(
        body, exch, "sb_bwd", (B, P, NQ), [blk(0), full(1), full(2), blk(3), one, one], [one, one, acc, acc],
        [jax.ShapeDtypeStruct((B, S, W), BF16), jax.ShapeDtypeStruct((B, S, W), BF16),
         jax.ShapeDtypeStruct((B, S, W), F32), jax.ShapeDtypeStruct((B, S, W), F32)],
        [pltpu.VMEM((S // TK, RS, TK), F32), pltpu.VMEM((S // TK, RS, TK), F32), pltpu.VMEM((S // TK, RS, 1), F32)],
        (proj3, proj3, proj3, proj3, o, dy))


def _fox_gate_fwd(f_t, b_f):
    B, H, S = f_t.shape

    def body(f_ref, b_ref, c_ref):
        row = lax.broadcasted_iota(jnp.int32, (BLK, BLK), 0)
        col = lax.broadcasted_iota(jnp.int32, (BLK, BLK), 1)
        mpre = (row <= col).astype(BF16)
        carry = jnp.zeros((H, 1), F32)
        for n in range(S // BLK):
            sl = pl.ds(n * BLK, BLK)
            lf, _ = _logsig_parts(f_ref[:, sl] + b_ref[...])
            c_ref[:, sl] = _split3_dot(lf, mpre) + carry
            carry = carry + jnp.sum(lf, axis=1, keepdims=True)

    spec = pl.BlockSpec((None, H, S), lambda b: (b, 0, 0))
    return pl.pallas_call(
        body, name="fox_gate_fwd", grid=(B,),
        in_specs=[spec, pl.BlockSpec((H, 1), lambda b: (0, 0))], out_specs=spec,
        out_shape=jax.ShapeDtypeStruct((B, H, S), F32),
        compiler_params=_params(("parallel",)),
    )(f_t, b_f)


def _fox_gate_bwd(dcum_t, f_t, b_f):
    B, H, S = f_t.shape

    def body(d_ref, f_ref, b_ref, df_ref, db_ref):
        b = pl.program_id(0)

        @pl.when(b == 0)
        def _():
            db_ref[...] = jnp.zeros_like(db_ref)

        row = lax.broadcasted_iota(jnp.int32, (BLK, BLK), 0)
        col = lax.broadcasted_iota(jnp.int32, (BLK, BLK), 1)
        msuf = (row >= col).astype(BF16)
        carry = jnp.zeros((H, 1), F32)
        dbacc = jnp.zeros((H, 1), F32)
        for n in reversed(range(S // BLK)):
            sl = pl.ds(n * BLK, BLK)
            dv = d_ref[:, sl]
            dlf = _split3_dot(dv, msuf) + carry
            carry = carry + jnp.sum(dv, axis=1, keepdims=True)
            df = dlf * _sigmoid(-(f_ref[:, sl] + b_ref[...]))
            df_ref[:, sl] = df
            dbacc = dbacc + jnp.sum(df, axis=1, keepdims=True)
        db_ref[...] += dbacc

    spec = pl.BlockSpec((None, H, S), lambda b: (b, 0, 0))
    vec = pl.BlockSpec((H, 1), lambda b: (0, 0))
    return pl.pallas_call(
        body, name="fox_gate_bwd", grid=(B,),
        in_specs=[spec, spec, vec], out_specs=[spec, vec],
        out_shape=[jax.ShapeDtypeStruct((B, H, S), F32), jax.ShapeDtypeStruct((H, 1), F32)],
        compiler_params=_params(("arbitrary",)),
    )(dcum_t, f_t, b_f)


def _pick_col(block, idx, lane_iota):
    return jnp.sum(jnp.where(lane_iota == idx, block, 0.0), axis=1, keepdims=True)


def _pick_row(block, idx, sub_iota):
    return jnp.sum(jnp.where(sub_iota == idx, block, 0.0), axis=0, keepdims=True)


def _fox_fwd(proj3, cum_c, cum_t, W, heads):
    B, S, _ = proj3.shape
    H = heads
    Dh = W // heads
    hpb = LANES // Dh
    P, TQ = W // LANES, _query_tile(S)
    NQ = S // TQ
    scale = 1.0 / math.sqrt(Dh)

    TK = _key_tile(S)
    RS = hpb * TQ

    def body(q_ref, k_ref, v_ref, g_ref, cc_ref, ct_ref, o_ref, y_ref, lse_ref):
        p = pl.program_id(1)
        i = pl.program_id(2)
        lane_head = _lane_head(Dh)
        trow, col = _causal_iotas(RS, TK, TQ)
        lane_h = lax.broadcasted_iota(jnp.int32, (1, H), 1)
        sub_h = lax.broadcasted_iota(jnp.int32, (H, 1), 0)
        qs = (_stack_heads(q_ref[...], lane_head, hpb) * scale).astype(BF16)
        cc = cc_ref[...]
        c_q = jnp.concatenate([_pick_col(cc, p * hpb + h, lane_h) for h in range(hpb)], axis=0)
        nt = (i * TQ + TQ - 1) // TK + 1

        def tile(jt, carry, masked):
            mx, l, acc = carry
            off = pl.multiple_of(jt * TK, TK)
            kj = k_ref[pl.ds(off, TK), :].astype(BF16)
            vj = v_ref[pl.ds(off, TK), :].astype(BF16)
            ctb = ct_ref[:, pl.ds(off, TK)]
            z = _dot(qs, kj, _NT) + c_q
            s = jnp.concatenate([z[h * TQ:(h + 1) * TQ] - _pick_row(ctb, p * hpb + h, sub_h) for h in range(hpb)],
                                axis=0)
            if masked:
                s = jnp.where(col + (jt * TK - i * TQ) <= trow, s, NEG_BIG)
            mx2 = jnp.maximum(mx, jnp.max(s, axis=1, keepdims=True))
            pe = jnp.exp(s - mx2)
            alpha = jnp.exp(mx - mx2)
            return (mx2, alpha * l + jnp.sum(pe, axis=1, keepdims=True), alpha * acc + _dot(pe.astype(BF16), vj))

        carry = lax.fori_loop(
            0, nt - 1, lambda jt, c: tile(jt, c, False),
            (jnp.full((RS, 1), NEG_BIG, F32), jnp.zeros((RS, 1), F32), jnp.zeros((RS, LANES), F32)))
        mx, l, acc = tile(nt - 1, carry, True)
        o = _unstack_heads(acc / l, lane_head, hpb)
        o_ref[...] = o
        lse_ref[...] = _unstack_heads(jnp.broadcast_to(mx + jnp.log(l), (RS, LANES)), lane_head, hpb)
        y_ref[...] = (o * _silu(g_ref[...])).astype(BF16)

    blk = lambda sec: pl.BlockSpec((None, TQ, LANES), lambda b, p, i: (b, i, sec * P + p))
    full = lambda sec: pl.BlockSpec((None, S, LANES), lambda b, p, i: (b, 0, sec * P + p))
    out = pl.BlockSpec((None, TQ, LANES), lambda b, p, i: (b, i, p))
    return pl.pallas_call(
        body, name="fox_fwd", grid=(B, P, NQ),
        in_specs=[blk(0), full(1), full(2), blk(3),
                  pl.BlockSpec((None, TQ, H), lambda b, p, i: (b, i, 0)),
                  pl.BlockSpec((None, H, S), lambda b, p, i: (b, 0, 0))],
        out_specs=[out, out, out],
        out_shape=[jax.ShapeDtypeStruct((B, S, W), F32), jax.ShapeDtypeStruct((B, S, W), BF16),
                   jax.ShapeDtypeStruct((B, S, W), F32)],
        compiler_params=_params(("parallel", "parallel", "arbitrary")),
    )(proj3, proj3, proj3, proj3, cum_c, cum_t)


def _fox_bwd(proj3, cum_c, cum_t, o, lse, dy, W, heads):
    B, S, _ = proj3.shape
    H = heads
    Dh = W // heads
    hpb = LANES // Dh
    P, TQ = W // LANES, _query_tile(S)
    NQ = S // TQ
    scale = 1.0 / math.sqrt(Dh)

    TK = _key_tile(S)
    RS = hpb * TQ

    def body(q_ref, k_ref, v_ref, g_ref, cc_ref, ct_ref, o_ref, lse_ref, dy_ref,
             dq_ref, dg_ref, dk_ref, dv_ref, dc_ref, p_scr, dp_scr):
        p = pl.program_id(1)
        i = pl.program_id(2)

        @pl.when(i == 0)
        def _():
            dk_ref[...] = jnp.zeros_like(dk_ref)
            dv_ref[...] = jnp.zeros_like(dv_ref)
            dc_ref[...] = jnp.zeros_like(dc_ref)

        lane_head = _lane_head(Dh)
        trow, col = _causal_iotas(RS, TK, TQ)
        lane_h = lax.broadcasted_iota(jnp.int32, (1, H), 1)
        sub_h = lax.broadcasted_iota(jnp.int32, (H, 1), 0)
        lane = lax.broadcasted_iota(jnp.int32, (1, LANES), 1)
        g = g_ref[...]
        lsev = lse_ref[...]
        cc = cc_ref[...]
        dyv = dy_ref[...].astype(F32)
        dg_ref[...] = (dyv * o_ref[...] * _dsilu(g)).astype(dg_ref.dtype)
        qs = (_stack_heads(q_ref[...], lane_head, hpb) * scale).astype(BF16)
        dos = _stack_heads(dyv * _silu(g), lane_head, hpb).astype(BF16)
        c_q = jnp.concatenate([_pick_col(cc, p * hpb + h, lane_h) for h in range(hpb)], axis=0)
        c_q = c_q - jnp.concatenate([_pick_col(lsev, h * Dh, lane) for h in range(hpb)], axis=0)
        nt = (i * TQ + TQ - 1) // TK + 1

        def probs(jt, dsum, masked):
            off = pl.multiple_of(jt * TK, TK)
            kj = k_ref[pl.ds(off, TK), :].astype(BF16)
            vj = v_ref[pl.ds(off, TK), :].astype(BF16)
            ctb = ct_ref[:, pl.ds(off, TK)]
            z = _dot(qs, kj, _NT) + c_q
            s = jnp.concatenate([z[h * TQ:(h + 1) * TQ] - _pick_row(ctb, p * hpb + h, sub_h) for h in range(hpb)],
                                axis=0)
            pr = jnp.exp(s)
            if masked:
                pr = jnp.where(col + (jt * TK - i * TQ) <= trow, pr, 0.0)
            dp = _dot(dos, vj, _NT)
            p_scr[jt] = pr
            dp_scr[jt] = dp
            dv_ref[pl.ds(off, TK), :] += _dot(pr.astype(BF16), dos, _TN)
            return dsum + jnp.sum(pr * dp, axis=1, keepdims=True)

        dsum = lax.fori_loop(0, nt - 1, lambda jt, d: probs(jt, d, False), jnp.zeros((RS, 1), F32))
        dsum = probs(nt - 1, dsum, True)

        def grads(jt, acc):
            off = pl.multiple_of(jt * TK, TK)
            kj = k_ref[pl.ds(off, TK), :].astype(BF16)
            ds = p_scr[jt] * (dp_scr[jt] - dsum)
            for h in range(hpb):
                dc_ref[h:h + 1, pl.ds(off, TK)] -= jnp.sum(ds[h * TQ:(h + 1) * TQ], axis=0, keepdims=True)
            dsb = ds.astype(BF16)
            dk_ref[pl.ds(off, TK), :] += _dot(dsb, qs, _TN)
            return acc + _dot(dsb, kj)

        acc = lax.fori_loop(0, nt, grads, jnp.zeros((RS, LANES), F32))
        dq_ref[...] = (_unstack_heads(acc, lane_head, hpb) * scale).astype(dq_ref.dtype)

    blk = lambda sec: pl.BlockSpec((None, TQ, LANES), lambda b, p, i: (b, i, sec * P + p))
    full = lambda sec: pl.BlockSpec((None, S, LANES), lambda b, p, i: (b, 0, sec * P + p))
    one = pl.BlockSpec((None, TQ, LANES), lambda b, p, i: (b, i, p))
    acc = pl.BlockSpec((None, S, LANES), lambda b, p, i: (b, 0, p))
    return pl.pallas_call(
        body, name="fox_bwd", grid=(B, P, NQ),
        in_specs=[blk(0), full(1), full(2), blk(3),
                  pl.BlockSpec((None, TQ, H), lambda b, p, i: (b, i, 0)),
                  pl.BlockSpec((None, H, S), lambda b, p, i: (b, 0, 0)),
                  one, one, one],
        out_specs=[one, one, acc, acc, pl.BlockSpec((None, None, hpb, S), lambda b, p, i: (b, p, 0, 0))],
        out_shape=[jax.ShapeDtypeStruct((B, S, W), BF16), jax.ShapeDtypeStruct((B, S, W), BF16),
                   jax.ShapeDtypeStruct((B, S, W), F32), jax.ShapeDtypeStruct((B, S, W), F32),
                   jax.ShapeDtypeStruct((B, P, hpb, S), F32)],
        scratch_shapes=[pltpu.VMEM((S // TK, RS, TK), F32), pltpu.VMEM((S // TK, RS, TK), F32)],
        compiler_params=_params(("parallel", "parallel", "arbitrary")),
    )(proj3, proj3, proj3, proj3, cum_c, cum_t, o, lse, dy)


def _layernorm_rows(v, gamma, beta):
    mu = jnp.mean(v, axis=-1, keepdims=True)
    xc = v - mu
    rstd = lax.rsqrt(jnp.mean(xc * xc, axis=-1, keepdims=True) + EPS)
    xh = xc * rstd
    return xh, rstd, xh * gamma + beta


def _layernorm_rows_bwd(dout, xh, rstd, gamma):
    dxh = dout * gamma
    return rstd * (dxh - jnp.mean(dxh, axis=-1, keepdims=True) - xh * jnp.mean(dxh * xh, axis=-1, keepdims=True))


def _gmlp_fwd(proj, wm, bs_t, ln_g, ln_b, W):
    T = proj.shape[0]
    G = wm.shape[0]
    cg = W // G
    assert cg == LANES

    def body(p_ref, wm_ref, bs_ref, lg_ref, lb_ref, y_ref, vn_ref):
        lane = lax.broadcasted_iota(jnp.int32, (1, LANES), 1)
        _, _, vn = _layernorm_rows(_gelu(p_ref[:, W:2 * W]), lg_ref[...], lb_ref[...])
        vn_ref[...] = vn.astype(BF16)
        bs = bs_ref[...]
        for g in range(G):
            sl = pl.ds(g * cg, cg)
            s = _dot(wm_ref[g], vn_ref[:, sl]) + _pick_col(bs, g, lane)
            gate = p_ref[:, pl.ds(2 * W + g * cg, cg)]
            y_ref[:, sl] = (_gelu(p_ref[:, sl]) * s * _silu(gate)).astype(BF16)

    vec = pl.BlockSpec((1, W), lambda r: (0, 0))
    return pl.pallas_call(
        body, name="gmlp_fwd", grid=(T // BLK,),
        in_specs=[pl.BlockSpec((BLK, 3 * W), lambda r: (r, 0)),
                  pl.BlockSpec((G, BLK, BLK), lambda r: (0, 0, 0)),
                  pl.BlockSpec((BLK, LANES), lambda r: (0, 0)), vec, vec],
        out_specs=pl.BlockSpec((BLK, W), lambda r: (r, 0)),
        out_shape=jax.ShapeDtypeStruct((T, W), BF16),
        scratch_shapes=[pltpu.VMEM((BLK, W), BF16)],
        compiler_params=_params(("parallel",)),
    )(proj, wm, bs_t, ln_g, ln_b)


def _gmlp_bwd(proj, dy, wm, bs_t, ln_g, ln_b, W):
    T = proj.shape[0]
    G = wm.shape[0]
    cg = W // G

    def body(p_ref, dy_ref, wm_ref, bs_ref, lg_ref, lb_ref,
             dp_ref, dwm_ref, dbs_ref, dlg_ref, dlb_ref, vn_ref, dvn_ref):
        r = pl.program_id(0)

        @pl.when(r == 0)
        def _():
            dwm_ref[...] = jnp.zeros_like(dwm_ref)
            dbs_ref[...] = jnp.zeros_like(dbs_ref)
            dlg_ref[...] = jnp.zeros_like(dlg_ref)
            dlb_ref[...] = jnp.zeros_like(dlb_ref)

        lane = lax.broadcasted_iota(jnp.int32, (1, LANES), 1)
        vpre = p_ref[:, W:2 * W]
        gamma = lg_ref[...]
        xh, rstd, vn = _layernorm_rows(_gelu(vpre), gamma, lb_ref[...])
        vn_ref[...] = vn.astype(BF16)
        bs = bs_ref[...]
        dbs = jnp.zeros((BLK, LANES), F32)
        for g in range(G):
            sl = pl.ds(g * cg, cg)
            gsl = pl.ds(2 * W + g * cg, cg)
            vng = vn_ref[:, sl]
            s = _dot(wm_ref[g], vng) + _pick_col(bs, g, lane)
            upre = p_ref[:, sl]
            u = _gelu(upre)
            gate = p_ref[:, gsl]
            dyv = dy_ref[:, sl].astype(F32)
            dp_ref[:, gsl] = (dyv * u * s * _dsilu(gate)).astype(dp_ref.dtype)
            do = dyv * _silu(gate)
            dp_ref[:, sl] = (do * s * _dgelu(upre)).astype(dp_ref.dtype)
            ds = do * u
            dbs = dbs + jnp.where(lane == g, jnp.sum(ds, axis=1, keepdims=True), 0.0)
            dsb = ds.astype(BF16)
            dwm_ref[g] += _dot(dsb, vng, _NT)
            dvn_ref[:, sl] = _dot(wm_ref[g], dsb, _TN)
        dbs_ref[...] += dbs
        dvn = dvn_ref[...]
        dlg_ref[...] += jnp.sum(dvn * xh, axis=0, keepdims=True)
        dlb_ref[...] += jnp.sum(dvn, axis=0, keepdims=True)
        dv = _layernorm_rows_bwd(dvn, xh, rstd, gamma)
        dp_ref[:, W:2 * W] = (dv * _dgelu(vpre)).astype(dp_ref.dtype)

    vec = pl.BlockSpec((1, W), lambda r: (0, 0))
    return pl.pallas_call(
        body, name="gmlp_bwd", grid=(T // BLK,),
        in_specs=[pl.BlockSpec((BLK, 3 * W), lambda r: (r, 0)),
                  pl.BlockSpec((BLK, W), lambda r: (r, 0)),
                  pl.BlockSpec((G, BLK, BLK), lambda r: (0, 0, 0)),
                  pl.BlockSpec((BLK, LANES), lambda r: (0, 0)), vec, vec],
        out_specs=[pl.BlockSpec((BLK, 3 * W), lambda r: (r, 0)),
                   pl.BlockSpec((G, BLK, BLK), lambda r: (0, 0, 0)),
                   pl.BlockSpec((BLK, LANES), lambda r: (0, 0)), vec, vec],
        out_shape=[jax.ShapeDtypeStruct((T, 3 * W), BF16), jax.ShapeDtypeStruct((G, BLK, BLK), F32),
                   jax.ShapeDtypeStruct((BLK, LANES), F32),
                   jax.ShapeDtypeStruct((1, W), F32), jax.ShapeDtypeStruct((1, W), F32)],
        scratch_shapes=[pltpu.VMEM((BLK, W), BF16), pltpu.VMEM((BLK, W), F32)],
        compiler_params=_params(("arbitrary",)),
    )(proj, dy, wm, bs_t, ln_g, ln_b)


SUBLANES = 8
SHIFT_ROWS = CONV_HALO + BLK - SUBLANES


def _shift_rows(ext_ref, sh_ref, off):
    for r in range(1, SUBLANES):
        sh_ref[r - 1] = ext_ref[pl.ds(r, SHIFT_ROWS), pl.ds(off, LANES)]


def _rows_from(ext_ref, sh_ref, off, start):
    r = start % SUBLANES
    if r == 0:
        return ext_ref[pl.ds(start, BLK), pl.ds(off, LANES)]
    return sh_ref[r - 1, pl.ds(start - r, BLK), :]


def _conv_taps(ext_ref, sh_ref, cw_ref, off, n_taps, first):
    acc = jnp.zeros((BLK, LANES), F32)
    for k in range(n_taps):
        acc = acc + cw_ref[k:k + 1, pl.ds(off, LANES)] * _rows_from(ext_ref, sh_ref, off, first + k)
    return acc


def _fill_glu_ext(ext_ref, halo_ref, cur_ref, W, first_block):
    y0h = halo_ref[:, :W] * _sigmoid(halo_ref[:, W:])
    ext_ref[0:CONV_HALO, :] = jnp.where(first_block, 0.0, y0h)
    ext_ref[CONV_HALO:CONV_HALO + BLK, :] = cur_ref[:, :W] * _sigmoid(cur_ref[:, W:])


def _conv_specs(S, W):
    per = BLK // CONV_HALO
    cur = pl.BlockSpec((None, BLK, 2 * W), lambda b, i: (b, i, 0))
    halo = pl.BlockSpec((None, CONV_HALO, 2 * W), lambda b, i: (b, jnp.maximum(i * per - 1, 0), 0))
    gate = pl.BlockSpec((None, BLK, W), lambda b, i: (b, i, 2))
    return cur, halo, gate


def _conv_fwd(proj3, cw, cb, ln_g, ln_b, W):
    B, S, _ = proj3.shape
    K = cw.shape[0]
    first = CONV_HALO - (K - 1)
    assert first >= 0

    def body(cur_ref, halo_ref, g_ref, cw_ref, cb_ref, lg_ref, lb_ref, y_ref, ext_ref, y1_ref, sh_ref):
        i = pl.program_id(1)
        _fill_glu_ext(ext_ref, halo_ref, cur_ref, W, i == 0)

        def chan(c, _):
            off = pl.multiple_of(c * LANES, LANES)
            _shift_rows(ext_ref, sh_ref, off)
            y1_ref[:, pl.ds(off, LANES)] = (_conv_taps(ext_ref, sh_ref, cw_ref, off, K, first)
                                            + cb_ref[:, pl.ds(off, LANES)])
            return 0

        lax.fori_loop(0, W // LANES, chan, 0)
        _, _, ln = _layernorm_rows(y1_ref[...], lg_ref[...], lb_ref[...])
        y_ref[...] = (_silu(ln) * _silu(g_ref[...])).astype(BF16)

    cur, halo, gate = _conv_specs(S, W)
    vec = pl.BlockSpec((1, W), lambda b, i: (0, 0))
    return pl.pallas_call(
        body, name="conv_fwd", grid=(B, S // BLK),
        in_specs=[cur, halo, gate, pl.BlockSpec((K, W), lambda b, i: (0, 0)), vec, vec, vec],
        out_specs=pl.BlockSpec((None, BLK, W), lambda b, i: (b, i, 0)),
        out_shape=jax.ShapeDtypeStruct((B, S, W), BF16),
        scratch_shapes=[pltpu.VMEM((CONV_HALO + BLK, W), F32), pltpu.VMEM((BLK, W), F32),
                        pltpu.VMEM((SUBLANES - 1, SHIFT_ROWS, LANES), F32)],
        compiler_params=_params(("parallel", "parallel")),
    )(proj3, proj3, proj3, cw, cb, ln_g, ln_b)


def _conv_bwd1(proj3, dy, cw, cb, ln_g, ln_b, W, exch):
    B, S, _ = proj3.shape
    K = cw.shape[0]
    first = CONV_HALO - (K - 1)

    def body(cur_ref, halo_ref, g_ref, dy_ref, cw_ref, cb_ref, lg_ref, lb_ref,
             dy1_ref, dg_ref, dcw_ref, dcb_ref, dlg_ref, dlb_ref, ext_ref, y1_ref, sh_ref):
        b = pl.program_id(0)
        i = pl.program_id(1)

        @pl.when(jnp.logical_and(b == 0, i == 0))
        def _():
            dcw_ref[...] = jnp.zeros_like(dcw_ref)
            dcb_ref[...] = jnp.zeros_like(dcb_ref)
            dlg_ref[...] = jnp.zeros_like(dlg_ref)
            dlb_ref[...] = jnp.zeros_like(dlb_ref)

        _fill_glu_ext(ext_ref, halo_ref, cur_ref, W, i == 0)

        def chan(c, _):
            off = pl.multiple_of(c * LANES, LANES)
            _shift_rows(ext_ref, sh_ref.at[c], off)
            y1_ref[:, pl.ds(off, LANES)] = (_conv_taps(ext_ref, sh_ref.at[c], cw_ref, off, K, first)
                                            + cb_ref[:, pl.ds(off, LANES)])
            return 0

        lax.fori_loop(0, W // LANES, chan, 0)
        gamma = lg_ref[...]
        xh, rstd, ln = _layernorm_rows(y1_ref[...], gamma, lb_ref[...])
        g = g_ref[...]
        dyv = dy_ref[...].astype(F32)
        dg_ref[...] = (dyv * _silu(ln) * _dsilu(g)).astype(dg_ref.dtype)
        dln = dyv * _silu(g) * _dsilu(ln)
        dlg_ref[...] += jnp.sum(dln * xh, axis=0, keepdims=True)
        dlb_ref[...] += jnp.sum(dln, axis=0, keepdims=True)
        dy1 = _layernorm_rows_bwd(dln, xh, rstd, gamma)
        dy1_ref[...] = dy1
        dcb_ref[...] += jnp.sum(dy1, axis=0, keepdims=True)

        def chan_w(c, _):
            off = pl.multiple_of(c * LANES, LANES)
            d = dy1_ref[:, pl.ds(off, LANES)]
            for k in range(K):
                dcw_ref[k:k + 1, pl.ds(off, LANES)] += jnp.sum(
                    d * _rows_from(ext_ref, sh_ref.at[c], off, first + k), axis=0, keepdims=True)
            return 0

        lax.fori_loop(0, W // LANES, chan_w, 0)

    cur, halo, gate = _conv_specs(S, W)
    vec = pl.BlockSpec((1, W), lambda b, i: (0, 0))
    taps = pl.BlockSpec((K, W), lambda b, i: (0, 0))
    one = pl.BlockSpec((None, BLK, W), lambda b, i: (b, i, 0))
    return _call_hosting(
        body, exch, "conv_bwd1", (B, S // BLK), [cur, halo, gate, one, taps, vec, vec, vec],
        [one, one, taps, vec, vec, vec],
        [jax.ShapeDtypeStruct((B, S, W), F32), jax.ShapeDtypeStruct((B, S, W), BF16),
         jax.ShapeDtypeStruct((K, W), F32)] + [jax.ShapeDtypeStruct((1, W), F32)] * 3,
        [pltpu.VMEM((CONV_HALO + BLK, W), F32), pltpu.VMEM((BLK, W), F32),
         pltpu.VMEM((W // LANES, SUBLANES - 1, SHIFT_ROWS, LANES), F32)],
        (proj3, proj3, proj3, dy, cw, cb, ln_g, ln_b))


def _conv_bwd2(proj3, dy1, dgate, cw_rev, W):
    B, S, _ = proj3.shape
    K = cw_rev.shape[0]
    NQ = S // BLK
    per = BLK // CONV_HALO

    def body(cur_ref, d_ref, dnext_ref, dgate_ref, cw_ref, dp_ref, ext_ref, dy0_ref, sh_ref):
        i = pl.program_id(1)
        ext_ref[0:BLK, :] = d_ref[...]
        ext_ref[BLK:BLK + CONV_HALO, :] = jnp.where(i == NQ - 1, 0.0, dnext_ref[...])

        def chan(c, _):
            off = pl.multiple_of(c * LANES, LANES)
            _shift_rows(ext_ref, sh_ref, off)
            dy0_ref[:, pl.ds(off, LANES)] = _conv_taps(ext_ref, sh_ref, cw_ref, off, K, 0)
            return 0

        lax.fori_loop(0, W // LANES, chan, 0)
        a = cur_ref[:, :W]
        sg = _sigmoid(cur_ref[:, W:])
        dy0 = dy0_ref[...]
        dp_ref[:, 0:W] = (dy0 * sg).astype(dp_ref.dtype)
        dp_ref[:, W:2 * W] = (dy0 * a * sg * (1.0 - sg)).astype(dp_ref.dtype)
        dp_ref[:, 2 * W:3 * W] = dgate_ref[...]

    cur = pl.BlockSpec((None, BLK, 2 * W), lambda b, i: (b, i, 0))
    one = pl.BlockSpec((None, BLK, W), lambda b, i: (b, i, 0))
    nxt = pl.BlockSpec((None, CONV_HALO, W), lambda b, i: (b, jnp.minimum((i + 1) * per, S // CONV_HALO - 1), 0))
    return pl.pallas_call(
        body, name="conv_bwd2", grid=(B, NQ),
        in_specs=[cur, one, nxt, one, pl.BlockSpec((K, W), lambda b, i: (0, 0))],
        out_specs=pl.BlockSpec((None, BLK, 3 * W), lambda b, i: (b, i, 0)),
        out_shape=jax.ShapeDtypeStruct((B, S, 3 * W), BF16),
        scratch_shapes=[pltpu.VMEM((BLK + CONV_HALO, W), F32), pltpu.VMEM((BLK, W), F32),
                        pltpu.VMEM((SUBLANES - 1, SHIFT_ROWS, LANES), F32)],
        compiler_params=_params(("parallel", "parallel")),
    )(proj3, dy1, dy1, dgate, cw_rev)


def _pack(arrays):
    flat = jnp.concatenate([a.astype(F32).reshape(-1) for a in arrays])
    n = flat.shape[0]
    pad = (-n) % (8 * LANES)
    if pad:
        flat = jnp.concatenate([flat, jnp.zeros((pad,), F32)])
    return flat.reshape(-1, LANES)


def _unpack(packed, shapes, lead=()):
    flat = packed.reshape(lead + (-1,))
    out, off = [], 0
    for shp in shapes:
        n = math.prod(shp)
        out.append(flat[..., off:off + n].reshape(lead + tuple(shp)))
        off += n
    return out


def _cols_from_dev(g):
    g = jnp.moveaxis(g, 0, -2)
    return g.reshape(g.shape[:-2] + (g.shape[-2] * g.shape[-1],))


def _my_cols(full, me):
    n8 = full.shape[-1] // N_DEV
    return lax.dynamic_slice_in_dim(full, me * n8, n8, axis=full.ndim - 1)


def kernel(x, a_norm, a_w_in, a_w_out, b_norm, b_w_in, b_v_ln_g, b_v_ln_b, b_w_s, b_b_s, b_w_out, c_norm, c_w_in, c_conv_w, c_conv_b, c_ln_g, c_ln_b, c_w_out, d_norm, d_w_in, d_b_f, d_w_out, final_norm, loss_target, m_a_norm, m_a_w_in, m_a_w_out, m_b_norm, m_b_w_in, m_b_v_ln_g, m_b_v_ln_b, m_b_w_s, m_b_b_s, m_b_w_out, m_c_norm, m_c_w_in, m_c_conv_w, m_c_conv_b, m_c_ln_g, m_c_ln_b, m_c_w_out, m_d_norm, m_d_w_in, m_d_b_f, m_d_w_out, m_final_norm, v_a_norm, v_a_w_in, v_a_w_out, v_b_norm, v_b_w_in, v_b_v_ln_g, v_b_v_ln_b, v_b_w_s, v_b_b_s, v_b_w_out, v_c_norm, v_c_w_in, v_c_conv_w, v_c_conv_b, v_c_ln_g, v_c_ln_b, v_c_w_out, v_d_norm, v_d_w_in, v_d_b_f, v_d_w_out, v_final_norm):
    B, S, D = x.shape
    T = B * S
    xi, yi, ci = _me()
    me = 4 * xi + 2 * yi + ci

    G = b_w_s.shape[1]
    KC = c_conv_w.shape[1]
    H_D = d_b_f.shape[1]
    W_A = a_w_out.shape[1] * N_DEV
    W_B = b_w_out.shape[1] * N_DEV
    W_C = c_w_out.shape[1] * N_DEV
    W_D = d_w_out.shape[1] * N_DEV
    N_D = d_w_in.shape[2] * N_DEV
    N_D_PAD = -(-N_D // (3 * LANES)) * (3 * LANES)

    big_names = ["a_w_in", "a_w_out", "b_w_in", "b_w_out", "c_w_in", "c_w_out", "d_w_in", "d_w_out"]
    big_w = dict(a_w_in=a_w_in[0], a_w_out=a_w_out[0], b_w_in=b_w_in[0], b_w_out=b_w_out[0],
                 c_w_in=c_w_in[0], c_w_out=c_w_out[0], d_w_in=d_w_in[0], d_w_out=d_w_out[0])
    small_sharded = [b_norm, b_v_ln_g, b_v_ln_b, c_norm, c_conv_w, c_conv_b, c_ln_g, c_ln_b, d_norm]
    first_names, later_names = big_names[:1], big_names[1:]
    gathered = _Exchange([big_w[n].astype(BF16) for n in first_names] + [_pack(small_sharded)],
                         ["gather"] * (len(first_names) + 1)).run("gather_first")
    wg = dict(zip(first_names, gathered[:-1]))
    (b_norm_f, b_lg_f, b_lb_f, c_norm_f, c_cw_f, c_cb_f, c_lg_f, c_lb_f, d_norm_f) = [
        _cols_from_dev(t) for t in _unpack(gathered[-1], [s.shape for s in small_sharded], lead=(N_DEV,))]
    c_cw_f = c_cw_f[0]

    wm =jnp.tril(b_w_s[0]).astype(BF16)
    bs_t = jnp.pad(b_b_s[0].T, ((0, 0), (0, LANES - G)))

    x0 = x.reshape(T, D)
    h_a = _rmsnorm_fwd(x0, a_norm, "rms_a")
    proj_a = _mm_w_dev(h_a, wg["a_w_in"], "proj_a").reshape(B, S, 4 * W_A)
    (o_a, y_a), later = _sb_fwd(proj_a, W_A, SB_HEADS,
                                _Exchange([big_w[n].astype(BF16) for n in later_names], ["gather"] * len(later_names)))
    wg.update(zip(later_names, later))
    a_w_out_f = wg["a_w_out"].reshape(W_A, D)
    b_w_out_f = wg["b_w_out"].reshape(W_B, D)
    c_w_out_f = wg["c_w_out"].reshape(W_C, D)
    d_w_out_f = wg["d_w_out"].reshape(W_D, D)
    d_w_in_f = jnp.pad(_cols_from_dev(wg["d_w_in"]), ((0, 0), (0, N_D_PAD - N_D)))
    y_a = y_a.reshape(T, W_A)
    x1 = x0 + _mm(y_a, a_w_out_f, "nn", T, D, W_A, F32, "out_a", 512, D, W_A)
    h_b = _rmsnorm_fwd(x1, b_norm_f, "rms_b")
    proj_b = _mm_w_dev(h_b, wg["b_w_in"], "proj_b")
    y_b = _gmlp_fwd(proj_b, wm, bs_t, b_lg_f, b_lb_f, W_B)
    x2 = x1 + _mm(y_b, b_w_out_f, "nn", T, D, W_B, F32, "out_b", 512, D, W_B)
    h_c = _rmsnorm_fwd(x2, c_norm_f, "rms_c")
    proj_c = _mm_w_dev(h_c, wg["c_w_in"], "proj_c").reshape(B, S, 3 * W_C)
    y_c = _conv_fwd(proj_c, c_cw_f, c_cb_f, c_lg_f, c_lb_f, W_C).reshape(T, W_C)
    x3 = x2 + _mm(y_c, c_w_out_f, "nn", T, D, W_C, F32, "out_c", 512, D, W_C)
    h_d = _rmsnorm_fwd(x3, d_norm_f, "rms_d")
    proj_d = _mm(h_d, d_w_in_f, "nn", T, N_D_PAD, D, F32, "proj_d", 1024, 384, D).reshape(B, S, N_D_PAD)
    f_t = jnp.swapaxes(proj_d[:, :, 4 * W_D:4 * W_D + H_D], 1, 2)
    b_f_col = d_b_f.reshape(H_D, 1)
    cum_t = _fox_gate_fwd(f_t, b_f_col)
    cum_c = jnp.swapaxes(cum_t, 1, 2)
    o_d, y_d, lse_d = _fox_fwd(proj_d, cum_c, cum_t, W_D, H_D)
    y_d = y_d.reshape(T, W_D)
    x4 = x3 + _mm(y_d, d_w_out_f, "nn", T, D, W_D, F32, "out_d", 512, D, W_D)

    loss_part, dx, g_final = _loss_head(x4, final_norm.reshape(1, D), loss_target.reshape(T, D))
    loss = lax.psum(loss_part[0, 0], MESH_AXES)

    dy_d = _mm(dx, d_w_out_f, "nt", T, W_D, D, BF16, "dy_d", 512, W_D, D).reshape(B, S, W_D)
    gw_d_out = _mm(y_d, dx, "tn", W_D, D, T, BF16, "gw_d_out", W_D, D, 512).reshape(N_DEV, W_D // N_DEV, D)
    dq, dg, dk, dv, dcum = _fox_bwd(proj_d, cum_c, cum_t, o_d, lse_d, dy_d, W_D, H_D)
    df_t, g_b_f = _fox_gate_bwd(dcum.reshape(B, H_D, S), f_t, b_f_col)
    dproj_d = jnp.concatenate(
        [dq, dk.astype(BF16), dv.astype(BF16), dg, jnp.swapaxes(df_t, 1, 2).astype(BF16),
         jnp.zeros((B, S, N_D_PAD - N_D), BF16)], axis=-1).reshape(T, N_D_PAD)
    gw_d_in_full = _mm(h_d, dproj_d, "tn", D, N_D_PAD, T, BF16, "gw_d_in", D, 384, 512)
    gw_d_in = jnp.moveaxis(gw_d_in_full[:, :N_D].reshape(D, N_DEV, N_D // N_DEV), 1, 0)
    dh = _mm(dproj_d, d_w_in_f, "nt", T, D, N_D_PAD, F32, "dh_d", 512, D, N_D_PAD // 3)
    dx, g_d_norm = _rmsnorm_bwd(x3, d_norm_f, dh, dx, "rms_bwd_d")

    dy_c = _mm(dx, c_w_out_f, "nt", T, W_C, D, BF16, "dy_c", 512, W_C, D).reshape(B, S, W_C)
    gw_c_out = _mm(y_c, dx, "tn", W_C, D, T, BF16, "gw_c_out", 1024, D, 512).reshape(N_DEV, W_C // N_DEV, D)
    (dy1, dgate_c, g_c_cw, g_c_cb, g_c_lg, g_c_lb), parts_d = _conv_bwd1(
        proj_c, dy_c, c_cw_f, c_cb_f, c_lg_f, c_lb_f, W_C, _Exchange([gw_d_in, gw_d_out], ["scatter"] * 2))
    dproj_c = _conv_bwd2(proj_c, dy1, dgate_c, c_cw_f[::-1], W_C).reshape(T, 3 * W_C)
    gw_c_in = _mm_grad_dev(h_c, dproj_c, "gw_c_in")
    dh = _mm_wT_dev(dproj_c, wg["c_w_in"], "dh_c")
    dx, g_c_norm = _rmsnorm_bwd(x2, c_norm_f, dh, dx, "rms_bwd_c")

    dy_b = _mm(dx, b_w_out_f, "nt", T, W_B, D, BF16, "dy_b", 512, W_B, D)
    gw_b_out = _mm(y_b, dx, "tn", W_B, D, T, BF16, "gw_b_out", 1024, D, 512).reshape(N_DEV, W_B // N_DEV, D)
    dproj_b, g_wm, g_bs_t, g_b_lg, g_b_lb = _gmlp_bwd(proj_b, dy_b, wm, bs_t, b_lg_f, b_lb_f, W_B)
    g_b_w_s = jnp.tril(g_wm)
    g_b_b_s = g_bs_t[:, :G].T
    gw_b_in = _mm_grad_dev(h_b, dproj_b, "gw_b_in")
    dh = _mm_wT_dev(dproj_b, wg["b_w_in"], "dh_b")
    dx, g_b_norm = _rmsnorm_bwd(x1, b_norm_f, dh, dx, "rms_bwd_b")

    dy_a = _mm(dx, a_w_out_f, "nt", T, W_A, D, BF16, "dy_a", 512, W_A, D).reshape(B, S, W_A)
    gw_a_out = _mm(y_a, dx, "tn", W_A, D, T, BF16, "gw_a_out", W_A, D, 512).reshape(N_DEV, W_A // N_DEV, D)
    small_full = [g_b_norm, g_b_lg, g_b_lb, g_b_w_s, g_b_b_s, g_c_norm, g_c_cw, g_c_cb, g_c_lg, g_c_lb,
                  g_d_norm, g_b_f, g_final]
    (dq, dg, dk, dv), parts_s = _sb_bwd(
        proj_a, o_a, dy_a, W_A, SB_HEADS,
        _Exchange([gw_c_in, gw_c_out, gw_b_in, gw_b_out, gw_a_out, _pack(small_full)], ["scatter"] * 5 + ["gather"]))
    dproj_a = jnp.concatenate([dq, dk.astype(BF16), dv.astype(BF16), dg], axis=-1).reshape(T, 4 * W_A)
    gw_a_in = _mm_grad_dev(h_a, dproj_a, "gw_a_in")
    dh, parts_a = _mm_wT_dev(dproj_a, wg["a_w_in"], "dh_a", exch=_Exchange([gw_a_in], ["scatter"]))
    dx, g_a_norm = _rmsnorm_bwd(x0, a_norm, dh, dx, "rms_bwd_a")
    grad_x = dx.reshape(B, S, D)

    (parts_n,) = _Exchange([_pack([g_a_norm])], ["gather"]).run("exchange_last")
    big_parts = dict(a_w_in=parts_a[0], a_w_out=parts_s[4], b_w_in=parts_s[2], b_w_out=parts_s[3],
                     c_w_in=parts_s[0], c_w_out=parts_s[1], d_w_in=parts_d[0], d_w_out=parts_d[1])
    (s_b_norm, s_b_lg, s_b_lb, s_b_w_s, s_b_b_s, s_c_norm, s_c_cw, s_c_cb, s_c_lg, s_c_lb,
     s_d_norm, s_b_f, s_final) = _unpack(_sum_parts(parts_s[5], "sum_small"), [g.shape for g in small_full])
    (s_a_norm,) = _unpack(_sum_parts(parts_n, "sum_a_norm"), [g_a_norm.shape])

    weights = dict(a_norm=a_norm, a_w_in=a_w_in, a_w_out=a_w_out, b_norm=b_norm, b_w_in=b_w_in, b_v_ln_g=b_v_ln_g,
                   b_v_ln_b=b_v_ln_b, b_w_s=b_w_s, b_b_s=b_b_s, b_w_out=b_w_out, c_norm=c_norm, c_w_in=c_w_in,
                   c_conv_w=c_conv_w, c_conv_b=c_conv_b, c_ln_g=c_ln_g, c_ln_b=c_ln_b, c_w_out=c_w_out,
                   d_norm=d_norm, d_w_in=d_w_in, d_b_f=d_b_f, d_w_out=d_w_out, final_norm=final_norm)
    mom_m = dict(a_norm=m_a_norm, a_w_in=m_a_w_in, a_w_out=m_a_w_out, b_norm=m_b_norm, b_w_in=m_b_w_in,
                 b_v_ln_g=m_b_v_ln_g, b_v_ln_b=m_b_v_ln_b, b_w_s=m_b_w_s, b_b_s=m_b_b_s, b_w_out=m_b_w_out,
                 c_norm=m_c_norm, c_w_in=m_c_w_in, c_conv_w=m_c_conv_w, c_conv_b=m_c_conv_b, c_ln_g=m_c_ln_g,
                 c_ln_b=m_c_ln_b, c_w_out=m_c_w_out, d_norm=m_d_norm, d_w_in=m_d_w_in, d_b_f=m_d_b_f,
                 d_w_out=m_d_w_out, final_norm=m_final_norm)
    mom_v = dict(a_norm=v_a_norm, a_w_in=v_a_w_in, a_w_out=v_a_w_out, b_norm=v_b_norm, b_w_in=v_b_w_in,
                 b_v_ln_g=v_b_v_ln_g, b_v_ln_b=v_b_v_ln_b, b_w_s=v_b_w_s, b_b_s=v_b_b_s, b_w_out=v_b_w_out,
                 c_norm=v_c_norm, c_w_in=v_c_w_in, c_conv_w=v_c_conv_w, c_conv_b=v_c_conv_b, c_ln_g=v_c_ln_g,
                 c_ln_b=v_c_ln_b, c_w_out=v_c_w_out, d_norm=v_d_norm, d_w_in=v_d_w_in, d_b_f=v_d_b_f,
                 d_w_out=v_d_w_out, final_norm=v_final_norm)
    order = list(weights)
    grads, deltas, new_m, new_v = {}, {}, {}, {}

    for n in big_names:
        part = big_parts[n]
        shp = weights[n].shape
        R, C = shp[1], shp[2]
        res = _adamw(part, weights[n].reshape(R, C), mom_m[n].reshape(R, C), mom_v[n].reshape(R, C), "adamw_" + n)
        grads[n], deltas[n], new_m[n], new_v[n] = [r.reshape(shp) for r in res]

    small_g = dict(
        a_norm=s_a_norm, b_norm=_my_cols(s_b_norm, me), b_v_ln_g=_my_cols(s_b_lg, me),
        b_v_ln_b=_my_cols(s_b_lb, me), b_w_s=s_b_w_s[None], b_b_s=s_b_b_s[None], c_norm=_my_cols(s_c_norm, me),
        c_conv_w=_my_cols(s_c_cw, me)[None], c_conv_b=_my_cols(s_c_cb, me), c_ln_g=_my_cols(s_c_lg, me),
        c_ln_b=_my_cols(s_c_lb, me), d_norm=_my_cols(s_d_norm, me), d_b_f=s_b_f.reshape(1, H_D),
        final_norm=s_final.reshape(D))
    small_names = list(small_g)
    sg_p = _pack([small_g[n] for n in small_names])
    res = _adamw(sg_p[None], _pack([weights[n] for n in small_names]), _pack([mom_m[n] for n in small_names]),
                 _pack([mom_v[n] for n in small_names]), "adamw_small")
    shapes = [weights[n].shape for n in small_names]
    for dst, r in zip((grads, deltas, new_m, new_v), res):
        for n, val in zip(small_names, _unpack(r, shapes)):
            dst[n] = val

    return (loss, grad_x, *[grads[n] for n in order], *[deltas[n] for n in order],
            *[new_m[n] for n in order], *[new_v[n] for n in order])
```

```python
import functools
import math

import jax
import jax.numpy as jnp
from jax import lax
from jax.experimental import pallas as pl
from jax.experimental.pallas import tpu as pltpu

F32 = jnp.float32
BF16 = jnp.bfloat16

EPS = 1e-6
SB_HEADS = 16
CONV_HALO = 32
BLK = 128
ATT_TK = 256
ATT_TQ = 512
ATT_GP = 2
LANES = 128
N_DEV = 8
MESH_AXES = ("x", "y", "c")

ADAM_LR = 0.001
ADAM_B1 = 0.9
ADAM_B2 = 0.999
ADAM_EPS = 1e-08
ADAM_WD = 0.01
ADAM_STEP = 10

VMEM_LIMIT = 56 * 1024 * 1024
NEG_BIG = -1e30

_NN = (((1,), (0,)), ((), ()))
_NT = (((1,), (1,)), ((), ()))
_TN = (((0,), (0,)), ((), ()))


def _dot(a, b, dims=_NN):
    return lax.dot_general(a, b, dims, preferred_element_type=F32)


def _split_dot(x, m):
    hi = x.astype(BF16)
    lo = (x - hi.astype(F32)).astype(BF16)
    return _dot(hi, m) + _dot(lo, m)


def _split3_dot(x, m):
    hi = x.astype(BF16)
    r1 = x - hi.astype(F32)
    mid = r1.astype(BF16)
    lo = (r1 - mid.astype(F32)).astype(BF16)
    return _dot(hi, m) + _dot(mid, m) + _dot(lo, m)


def _params(sem=None):
    kw = dict(vmem_limit_bytes=VMEM_LIMIT)
    if sem is not None:
        kw["dimension_semantics"] = sem
    return pltpu.CompilerParams(**kw)


def _sigmoid(x):
    return jax.nn.sigmoid(x)


def _silu(x):
    return x * _sigmoid(x)


def _dsilu(x):
    s = _sigmoid(x)
    return s * (1.0 + x * (1.0 - s))


_GELU_C = math.sqrt(2.0 / math.pi)


def _gelu(x):
    return 0.5 * x * (1.0 + jnp.tanh(_GELU_C * (x + 0.044715 * x * x * x)))


def _dgelu(x):
    th = jnp.tanh(_GELU_C * (x + 0.044715 * x * x * x))
    return 0.5 * (1.0 + th) + 0.5 * x * (1.0 - th * th) * _GELU_C * (1.0 + 3.0 * 0.044715 * x * x)


def _mm(a, b, mode, M, N, K, out_dtype, name, tm, tn, tk, a_spec=None, b_spec=None, o_spec=None, out_shape=None,
        exch=None, res=None):
    tm, tn, tk = min(tm, M), min(tn, N), min(tk, K)
    assert M % tm == 0 and N % tn == 0 and K % tk == 0, (name, M, N, K, tm, tn, tk)
    nk = K // tk
    dims = {"nn": _NN, "nt": _NT, "tn": _TN}[mode]
    if a_spec is None:
        a_spec = (pl.BlockSpec((tk, tm), lambda i, j, k: (k, i)) if mode == "tn"
                  else pl.BlockSpec((tm, tk), lambda i, j, k: (i, k)))
    if b_spec is None:
        b_spec = (pl.BlockSpec((tn, tk), lambda i, j, k: (j, k)) if mode == "nt"
                  else pl.BlockSpec((tk, tn), lambda i, j, k: (k, j)))
    if o_spec is None:
        o_spec = pl.BlockSpec((tm, tn), lambda i, j, k: (i, j))
    if out_shape is None:
        out_shape = (M, N)

    def body(a_ref, b_ref, *rest):
        res_ref = rest[0] if res is not None else None
        o_ref, acc_ref = rest[-2:]
        k = pl.program_id(2)

        @pl.when(k == 0)
        def _():
            acc_ref[...] = jnp.zeros_like(acc_ref) if res_ref is None else res_ref[...].astype(F32)

        acc_ref[...] += _dot(a_ref[...].astype(BF16), b_ref[...].astype(BF16), dims)

        @pl.when(k == nk - 1)
        def _():
            o_ref[...] = acc_ref[...].astype(o_ref.dtype)

    in_specs, args = [a_spec, b_spec], (a, b)
    if res is not None:
        in_specs, args = in_specs + [o_spec], args + (res,)
    if exch is None:
        return pl.pallas_call(
            body, name=name, grid=(M // tm, N // tn, nk),
            in_specs=in_specs, out_specs=o_spec,
            out_shape=jax.ShapeDtypeStruct(out_shape, out_dtype),
            scratch_shapes=[pltpu.VMEM((tm, tn), F32)],
            compiler_params=_params(("parallel", "parallel", "arbitrary")),
        )(*args)
    (out,), moved = _call_hosting(
        body, exch, name, (M // tm, N // tn, nk), in_specs, [o_spec],
        [jax.ShapeDtypeStruct(out_shape, out_dtype)], [pltpu.VMEM((tm, tn), F32)], args)
    return out, moved


def _mm_w_dev(a, w3, name, out_dtype=F32, tm=1024):
    M, K = a.shape
    n8 = w3.shape[2]
    tn = n8 if n8 <= 768 else 512
    per = n8 // tn
    b_spec = pl.BlockSpec((None, K, tn), lambda i, j, k: (j // per, 0, j % per))
    return _mm(a, w3, "nn", M, N_DEV * n8, K, out_dtype, name, tm, tn, K, b_spec=b_spec)


def _sectioned_spec(d4, t_rows, t_cols, rows_axis, cols_axis):
    _, _, S, W = d4.shape
    assert S % t_rows == 0 and W % t_cols == 0
    rb, cb = S // t_rows, W // t_cols

    def index(*g):
        r, c = g[rows_axis], g[cols_axis]
        return (r // rb, c // cb, r % rb, c % cb)

    return pl.BlockSpec((None, None, t_rows, t_cols), index)


def _mm_wT_dev(a, w3, name, out_dtype=F32, tm=512, exch=None):
    K, n8 = w3.shape[1], w3.shape[2]
    tk = n8 if n8 <= 768 else 512
    per = n8 // tk
    b_spec = pl.BlockSpec((None, K, tk), lambda i, j, k: (k // per, 0, k % per))
    if a.ndim == 4:
        M, N = a.shape[0] * a.shape[2], a.shape[1] * a.shape[3]
        a_spec = _sectioned_spec(a, tm, tk, 0, 2)
    else:
        (M, N), a_spec = a.shape, None
    return _mm(a, w3, "nt", M, K, N, out_dtype, name, tm, K, tk, a_spec=a_spec, b_spec=b_spec, exch=exch)


def _mm_grad_dev(h, d, name, out_dtype=BF16):
    T, M = h.shape
    N = d.shape[1] * d.shape[3] if d.ndim == 4 else d.shape[1]
    n8 = N // N_DEV
    tn = n8 if n8 <= 768 else 512
    per = n8 // tn
    tm = min(M, 1024)
    o_spec = pl.BlockSpec((None, tm, tn), lambda i, j, k: (j // per, i, j % per))
    b_spec = _sectioned_spec(d, 512, tn, 2, 1) if d.ndim == 4 else None
    return _mm(h, d, "tn", M, N, T, out_dtype, name, tm, tn, 512, b_spec=b_spec, o_spec=o_spec,
               out_shape=(N_DEV, M, n8))


def _me():
    x, y, c = lax.axis_index("x"), lax.axis_index("y"), lax.axis_index("c")
    return x, y, c


def _peer(r):
    x, y, c = _me()
    px = 1 - x if (r >> 2) & 1 else x
    py = 1 - y if (r >> 1) & 1 else y
    pc = 1 - c if r & 1 else c
    return (px, py, pc), 4 * px + 2 * py + pc


class _Exchange:
    def __init__(self, arrays, kinds):
        self.arrays, self.kinds, self.n = list(arrays), list(kinds), len(arrays)
        self.out_shapes = [
            jax.ShapeDtypeStruct((N_DEV,) + a.shape if kind == "gather" else a.shape, a.dtype)
            for a, kind in zip(arrays, kinds)]
        self.specs = [pl.BlockSpec(memory_space=pl.ANY)] * self.n
        self.sems = [pltpu.SemaphoreType.DMA((self.n, N_DEV - 1)), pltpu.SemaphoreType.DMA((self.n, N_DEV - 1)),
                     pltpu.SemaphoreType.DMA((self.n,))]

    def _copies(self, ins, outs, sems, receiving):
        send_sems, recv_sems, local_sems = sems
        x, y, c = _me()
        me = 4 * x + 2 * y + c

        def src(k, pid):
            return ins[k] if self.kinds[k] == "gather" else ins[k].at[pid]

        local = [pltpu.make_async_copy(src(k, me), outs[k].at[me], local_sems.at[k]) for k in range(self.n)]
        remote = []
        for r in range(1, N_DEV):
            peer, pid = _peer(r)
            for k in range(self.n):
                remote.append(pltpu.make_async_remote_copy(
                    src_ref=src(k, pid), dst_ref=outs[k].at[pid if receiving else me],
                    send_sem=send_sems.at[k, r - 1], recv_sem=recv_sems.at[k, r - 1],
                    device_id=peer, device_id_type=pl.DeviceIdType.MESH))
        return local, remote

    def start(self, ins, outs, sems):
        local, remote = self._copies(ins, outs, sems, False)
        for cp in local + remote:
            cp.start()

    def wait(self, ins, outs, sems):
        local, remote = self._copies(ins, outs, sems, True)
        for cp in remote:
            cp.wait_recv()
        for cp in remote:
            cp.wait_send()
        for cp in local:
            cp.wait()

    def run(self, name):
        n = self.n

        def body(*refs):
            ins, outs, sems = refs[:n], refs[n:2 * n], refs[2 * n:]
            self.start(ins, outs, sems)
            self.wait(ins, outs, sems)

        return pl.pallas_call(
            body, name=name, in_specs=self.specs, out_specs=self.specs, out_shape=self.out_shapes,
            scratch_shapes=self.sems,
        )(*self.arrays)


def _call_hosting(body, exch, name, grid, in_specs, out_specs, out_shape, scratch_shapes, args):
    if exch is None:
        res = pl.pallas_call(
            body, name=name, grid=grid, in_specs=list(in_specs), out_specs=list(out_specs),
            out_shape=list(out_shape), scratch_shapes=list(scratch_shapes),
            compiler_params=_params(("arbitrary",) * len(grid)))(*args)
        return res, []
    n_in, n_out, n_scr, nc = len(in_specs), len(out_specs), len(scratch_shapes), exch.n

    def full_body(*refs):
        ins, refs = refs[:n_in], refs[n_in:]
        cins, refs = refs[:nc], refs[nc:]
        outs, refs = refs[:n_out], refs[n_out:]
        couts, refs = refs[:nc], refs[nc:]
        scr, sems = refs[:n_scr], refs[n_scr:]
        ids = [pl.program_id(a) for a in range(len(grid))]
        first = functools.reduce(jnp.logical_and, [i == 0 for i in ids])
        last = functools.reduce(jnp.logical_and, [i == g - 1 for i, g in zip(ids, grid)])

        @pl.when(first)
        def _():
            exch.start(cins, couts, sems)

        body(*ins, *outs, *scr)

        @pl.when(last)
        def _():
            exch.wait(cins, couts, sems)

    res = pl.pallas_call(
        full_body, name=name, grid=grid,
        in_specs=list(in_specs) + exch.specs, out_specs=list(out_specs) + exch.specs,
        out_shape=list(out_shape) + exch.out_shapes,
        scratch_shapes=list(scratch_shapes) + exch.sems,
        compiler_params=_params(("arbitrary",) * len(grid)),
    )(*args, *exch.arrays)
    return res[:n_out], res[n_out:]


def _rmsnorm_fwd(x, g, name):
    T, D = x.shape
    tr = min(256, T)

    def body(x_ref, g_ref, h_ref):
        xv = x_ref[...]
        r = lax.rsqrt(jnp.mean(xv * xv, axis=-1, keepdims=True) + EPS)
        h_ref[...] = (xv * r * g_ref[...]).astype(BF16)

    return pl.pallas_call(
        body, name=name, grid=(T // tr,),
        in_specs=[pl.BlockSpec((tr, D), lambda i: (i, 0)), pl.BlockSpec((1, D), lambda i: (0, 0))],
        out_specs=pl.BlockSpec((tr, D), lambda i: (i, 0)),
        out_shape=jax.ShapeDtypeStruct((T, D), BF16),
        compiler_params=_params(("parallel",)),
    )(x, g)


def _rmsnorm_bwd(x, g, dh, dres, name):
    T, D = x.shape
    tr = min(256, T)

    def body(x_ref, g_ref, dh_ref, dres_ref, dx_ref, dg_ref):
        i = pl.program_id(0)
        xv = x_ref[...]
        r = lax.rsqrt(jnp.mean(xv * xv, axis=-1, keepdims=True) + EPS)
        xh = xv * r
        dhv = dh_ref[...]
        dxh = dhv * g_ref[...]
        dx_ref[...] = dres_ref[...] + r * (dxh - xh * jnp.mean(dxh * xh, axis=-1, keepdims=True))

        @pl.when(i == 0)
        def _():
            dg_ref[...] = jnp.zeros_like(dg_ref)

        dg_ref[...] += jnp.sum(dhv * xh, axis=0, keepdims=True)

    row = pl.BlockSpec((tr, D), lambda i: (i, 0))
    vec = pl.BlockSpec((1, D), lambda i: (0, 0))
    return pl.pallas_call(
        body, name=name, grid=(T // tr,),
        in_specs=[row, vec, row, row], out_specs=[row, vec],
        out_shape=[jax.ShapeDtypeStruct((T, D), F32), jax.ShapeDtypeStruct((1, D), F32)],
        compiler_params=_params(("arbitrary",)),
    )(x, g, dh, dres)


def _loss_head(x, g, target):
    T, D = x.shape
    tr = min(256, T)

    def body(x_ref, g_ref, t_ref, loss_ref, dx_ref, dg_ref):
        i = pl.program_id(0)
        xv = x_ref[...]
        gv = g_ref[...]
        r = lax.rsqrt(jnp.mean(xv * xv, axis=-1, keepdims=True) + EPS)
        xh = xv * r
        diff = xh * gv - t_ref[...]
        dy = diff * (1.0 / D)
        dxh = dy * gv
        dx_ref[...] = r * (dxh - xh * jnp.mean(dxh * xh, axis=-1, keepdims=True))

        @pl.when(i == 0)
        def _():
            dg_ref[...] = jnp.zeros_like(dg_ref)
            loss_ref[...] = jnp.zeros_like(loss_ref)

        dg_ref[...] += jnp.sum(dy * xh, axis=0, keepdims=True)
        part = jnp.sum(jnp.sum(diff * diff, axis=1, keepdims=True), axis=0, keepdims=True)
        loss_ref[...] += (0.5 / D) * part

    row = pl.BlockSpec((tr, D), lambda i: (i, 0))
    vec = pl.BlockSpec((1, D), lambda i: (0, 0))
    return pl.pallas_call(
        body, name="loss_head", grid=(T // tr,),
        in_specs=[row, vec, row],
        out_specs=[pl.BlockSpec((1, 1), lambda i: (0, 0)), row, vec],
        out_shape=[jax.ShapeDtypeStruct((1, 1), F32), jax.ShapeDtypeStruct((T, D), F32),
                   jax.ShapeDtypeStruct((1, D), F32)],
        compiler_params=_params(("arbitrary",)),
    )(x, g, target)


ELEMS_PER_STEP = 1 << 20


def _row_tile(R, per_row):
    best = None
    for tr in range(8, R + 1, 8):
        if R % tr == 0 and tr * per_row <= ELEMS_PER_STEP:
            best = tr
    return best if best is not None else R


def _adamw(parts, w, m, v, name):
    P, R, C = parts.shape
    tr = _row_tile(R, P * C)

    def body(p_ref, w_ref, m_ref, v_ref, g_out, d_out, m_out, v_out):
        g = p_ref[0].astype(F32)
        for p in range(1, P):
            g = g + p_ref[p].astype(F32)
        wv = w_ref[...]
        mn = ADAM_B1 * m_ref[...] + (1.0 - ADAM_B1) * g
        vn = ADAM_B2 * v_ref[...] + (1.0 - ADAM_B2) * (g * g)
        m_hat = mn / (1.0 - ADAM_B1 ** ADAM_STEP)
        v_hat = vn / (1.0 - ADAM_B2 ** ADAM_STEP)
        g_out[...] = g
        d_out[...] = -ADAM_LR * (m_hat / (jnp.sqrt(v_hat) + ADAM_EPS) + ADAM_WD * wv)
        m_out[...] = mn
        v_out[...] = vn

    row = pl.BlockSpec((tr, C), lambda i: (i, 0))
    return pl.pallas_call(
        body, name=name, grid=(R // tr,),
        in_specs=[pl.BlockSpec((P, tr, C), lambda i: (0, i, 0)), row, row, row],
        out_specs=[row, row, row, row],
        out_shape=[jax.ShapeDtypeStruct((R, C), F32)] * 4,
        compiler_params=_params(("parallel",)),
    )(parts, w, m, v)


def _sum_parts(parts, name):
    P, R, C = parts.shape
    tr = _row_tile(R, P * C)

    def body(p_ref, o_ref):
        g = p_ref[0]
        for p in range(1, P):
            g = g + p_ref[p]
        o_ref[...] = g

    return pl.pallas_call(
        body, name=name, grid=(R // tr,),
        in_specs=[pl.BlockSpec((P, tr, C), lambda i: (0, i, 0))],
        out_specs=pl.BlockSpec((tr, C), lambda i: (i, 0)),
        out_shape=jax.ShapeDtypeStruct((R, C), F32),
        compiler_params=_params(("parallel",)),
    )(parts)


def _lane_head(Dh):
    assert Dh & (Dh - 1) == 0 and Dh <= LANES
    return lax.shift_right_logical(lax.broadcasted_iota(jnp.int32, (1, LANES), 1), Dh.bit_length() - 1)


def _stack_heads(x, lane_head, hpb):
    return jnp.concatenate([jnp.where(lane_head == h, x, 0.0) for h in range(hpb)], axis=0)


def _unstack_heads(acc, lane_head, hpb):
    TQ = acc.shape[0] // hpb
    out = acc[0:TQ]
    for h in range(1, hpb):
        out = jnp.where(lane_head == h, acc[h * TQ:(h + 1) * TQ], out)
    return out


def _key_tile(S):
    return ATT_TK if S % ATT_TK == 0 else BLK


def _query_tile(S):
    return ATT_TQ if S % ATT_TQ == 0 else BLK


def _lane_groups(P):
    return ATT_GP if P % ATT_GP == 0 else 1


def _lanes(u):
    return slice(u * LANES, (u + 1) * LANES)


def _causal_iotas(RS, TK, TQ):
    assert TQ & (TQ - 1) == 0 and (TK % TQ == 0 or TQ % TK == 0)
    trow = jnp.bitwise_and(lax.broadcasted_iota(jnp.int32, (RS, TK), 0), TQ - 1)
    col = lax.broadcasted_iota(jnp.int32, (RS, TK), 1)
    return trow, col


def _tri(TK, op):
    r = lax.broadcasted_iota(jnp.int32, (TK, TK), 0)
    c = lax.broadcasted_iota(jnp.int32, (TK, TK), 1)
    return op(r, c).astype(BF16)


def _logsig_parts(z):
    sp = jnp.log(1.0 + jnp.exp(-jnp.abs(z)))
    return jnp.minimum(z, 0.0) - sp, -jnp.maximum(z, 0.0) - sp


def _sb_fwd(proj3, W, heads, exch):
    B, S, _ = proj3.shape
    Dh = W // heads
    hpb = LANES // Dh
    P, TQ = W // LANES, _query_tile(S)
    NQ = S // TQ
    scale = 1.0 / math.sqrt(Dh)

    TK = _key_tile(S)
    RS = hpb * TQ
    NM = max(1, TQ // TK)
    GP = _lane_groups(P)
    PG = P // GP

    def body(q_ref, k_ref, v_ref, g_ref, o_ref, y_ref):
        i = pl.program_id(2)
        lane_head = _lane_head(Dh)
        trow, col = _causal_iotas(RS, TK, TQ)
        msuf = _tri(TK, lambda r, c: r > c)
        qs = [(_stack_heads(q_ref[:, _lanes(u)], lane_head, hpb) * scale).astype(BF16) for u in range(GP)]
        nt = (i * TQ + TQ - 2) // TK + 1

        def tile(jt, carry, masked):
            off = pl.multiple_of(jt * TK, TK)
            if masked:
                msk = col + (jt * TK - i * TQ) < trow
            out = []
            for u, (rem, acc) in enumerate(carry):
                kj = k_ref[pl.ds(off, TK), _lanes(u)].astype(BF16)
                vj = v_ref[pl.ds(off, TK), _lanes(u)].astype(BF16)
                lb, lr = _logsig_parts(_dot(qs[u], kj, _NT))
                if masked:
                    lr = jnp.where(msk, lr, 0.0)
                w = jnp.exp(lb + _split_dot(lr, msuf) + rem)
                if masked:
                    w = jnp.where(msk, w, 0.0)
                out.append((rem + jnp.sum(lr, axis=1, keepdims=True), acc + _dot(w.astype(BF16), vj)))
            return tuple(out)

        zero = (jnp.zeros((RS, 1), F32), jnp.zeros((RS, LANES), F32))
        carry = (zero,) * GP
        for m in range(NM):
            carry = tile(nt - 1 - m, carry, True)
        carry = lax.fori_loop(NM, nt, lambda jj, c: tile(nt - 1 - jj, c, False), carry)
        for u in range(GP):
            o = _unstack_heads(carry[u][1], lane_head, hpb)
            o_ref[:, _lanes(u)] = o
            y_ref[:, _lanes(u)] = (o * _silu(g_ref[:, _lanes(u)])).astype(BF16)

    LW = GP * LANES
    blk = lambda sec: pl.BlockSpec((None, TQ, LW), lambda b, p, i: (b, i, sec * PG + p))
    full = lambda sec: pl.BlockSpec((None, S, LW), lambda b, p, i: (b, 0, sec * PG + p))
    out = pl.BlockSpec((None, TQ, LW), lambda b, p, i: (b, i, p))
    return _call_hosting(
        body, exch, "sb_fwd", (B, PG, NQ), [blk(0), full(1), full(2), blk(3)], [out, out],
        [jax.ShapeDtypeStruct((B, S, W), F32), jax.ShapeDtypeStruct((B, S, W), BF16)], [],
        (proj3, proj3, proj3, proj3))


def _sb_bwd(proj3, o, dy, W, heads, exch):
    B, S, _ = proj3.shape
    Dh = W // heads
    hpb = LANES // Dh
    P, TQ = W // LANES, _query_tile(S)
    NQ = S // TQ
    scale = 1.0 / math.sqrt(Dh)

    TK = _key_tile(S)
    RS = hpb * TQ
    NM = max(1, TQ // TK)

    def body(q_ref, k_ref, v_ref, g_ref, o_ref, dy_ref, dp_ref, dk_ref, dv_ref, u_ref, sig_ref, es_ref):
        i = pl.program_id(2)
        rows = pl.ds(pl.multiple_of(i * TQ, TQ), TQ)

        @pl.when(i == 0)
        def _():
            dk_ref[...] = jnp.zeros_like(dk_ref)
            dv_ref[...] = jnp.zeros_like(dv_ref)

        lane_head = _lane_head(Dh)
        trow, col = _causal_iotas(RS, TK, TQ)
        msuf = _tri(TK, lambda r, c: r > c)
        mpre = _tri(TK, lambda r, c: r < c)
        g = g_ref[...]
        dyv = dy_ref[...].astype(F32)
        dp_ref[3, rows, :] = (dyv * o_ref[...] * _dsilu(g)).astype(dp_ref.dtype)
        qs = (_stack_heads(q_ref[...], lane_head, hpb) * scale).astype(BF16)
        dos = _stack_heads(dyv * _silu(g), lane_head, hpb).astype(BF16)
        nt = (i * TQ + TQ - 2) // TK + 1

        def weights(jt, rem, masked):
            off = pl.multiple_of(jt * TK, TK)
            kj = k_ref[pl.ds(off, TK), :].astype(BF16)
            vj = v_ref[pl.ds(off, TK), :].astype(BF16)
            lb, lr = _logsig_parts(_dot(qs, kj, _NT))
            if masked:
                msk = col + (jt * TK - i * TQ) < trow
                lr = jnp.where(msk, lr, 0.0)
            w = jnp.exp(lb + _split_dot(lr, msuf) + rem)
            if masked:
                w = jnp.where(msk, w, 0.0)
            e = w * _dot(dos, vj, _NT)
            sig = jnp.exp(lb)
            u = e * (1.0 - sig) - _split_dot(e, mpre) * sig
            if masked:
                u = jnp.where(msk, u, 0.0)
                sig = jnp.where(msk, sig, 0.0)
            u_ref[jt] = u
            sig_ref[jt] = sig
            es_ref[jt] = jnp.sum(e, axis=1, keepdims=True)
            dv_ref[pl.ds(off, TK), :] += _dot(w.astype(BF16), dos, _TN)
            return rem + jnp.sum(lr, axis=1, keepdims=True)

        rem = jnp.zeros((RS, 1), F32)
        for m in range(NM):
            rem = weights(nt - 1 - m, rem, True)
        lax.fori_loop(NM, nt, lambda jj, r: weights(nt - 1 - jj, r, False), rem)

        def grads(jt, carry):
            pre, acc = carry
            off = pl.multiple_of(jt * TK, TK)
            kj = k_ref[pl.ds(off, TK), :].astype(BF16)
            dz = (u_ref[jt] - pre * sig_ref[jt]).astype(BF16)
            dk_ref[pl.ds(off, TK), :] += _dot(dz, qs, _TN)
            return pre + es_ref[jt], acc + _dot(dz, kj)

        _, acc = lax.fori_loop(0, nt, grads, (jnp.zeros((RS, 1), F32), jnp.zeros((RS, LANES), F32)))
        dp_ref[0, rows, :] = (_unstack_heads(acc, lane_head, hpb) * scale).astype(dp_ref.dtype)

        @pl.when(i == NQ - 1)
        def _():
            dp_ref[1] = dk_ref[...].astype(dp_ref.dtype)
            dp_ref[2] = dv_ref[...].astype(dp_ref.dtype)

    blk = lambda sec: pl.BlockSpec((None, TQ, LANES), lambda b, p, i: (b, i, sec * P + p))
    full = lambda sec: pl.BlockSpec((None, S, LANES), lambda b, p, i: (b, 0, sec * P + p))
    one = pl.BlockSpec((None, TQ, LANES), lambda b, p, i: (b, i, p))
    (dproj,), moved = _call_hosting(
        body, exch, "sb_bwd", (B, P, NQ), [blk(0), full(1), full(2), blk(3), one, one],
        [pl.BlockSpec((None, 4, S, LANES), lambda b, p, i: (b, 0, 0, p))],
        [jax.ShapeDtypeStruct((B, 4, S, W), BF16)],
        [pltpu.VMEM((S, LANES), F32), pltpu.VMEM((S, LANES), F32),
         pltpu.VMEM((S // TK, RS, TK), F32), pltpu.VMEM((S // TK, RS, TK), F32), pltpu.VMEM((S // TK, RS, 1), F32)],
        (proj3, proj3, proj3, proj3, o, dy))
    return dproj, moved


def _fox_gate_fwd(f_t, b_f):
    B, H, S = f_t.shape

    def body(f_ref, b_ref, c_ref):
        row = lax.broadcasted_iota(jnp.int32, (BLK, BLK), 0)
        col = lax.broadcasted_iota(jnp.int32, (BLK, BLK), 1)
        mpre = (row <= col).astype(BF16)
        carry = jnp.zeros((H, 1), F32)
        for n in range(S // BLK):
            sl = pl.ds(n * BLK, BLK)
            lf, _ = _logsig_parts(f_ref[:, sl] + b_ref[...])
            c_ref[:, sl] = _split3_dot(lf, mpre) + carry
            carry = carry + jnp.sum(lf, axis=1, keepdims=True)

    spec = pl.BlockSpec((None, H, S), lambda b: (b, 0, 0))
    return pl.pallas_call(
        body, name="fox_gate_fwd", grid=(B,),
        in_specs=[spec, pl.BlockSpec((H, 1), lambda b: (0, 0))], out_specs=spec,
        out_shape=jax.ShapeDtypeStruct((B, H, S), F32),
        compiler_params=_params(("parallel",)),
    )(f_t, b_f)


def _fox_gate_bwd(dcum_t, f_t, b_f):
    B, H, S = f_t.shape

    def body(d_ref, f_ref, b_ref, df_ref, db_ref):
        b = pl.program_id(0)

        @pl.when(b == 0)
        def _():
            db_ref[...] = jnp.zeros_like(db_ref)

        row = lax.broadcasted_iota(jnp.int32, (BLK, BLK), 0)
        col = lax.broadcasted_iota(jnp.int32, (BLK, BLK), 1)
        msuf = (row >= col).astype(BF16)
        carry = jnp.zeros((H, 1), F32)
        dbacc = jnp.zeros((H, 1), F32)
        for n in reversed(range(S // BLK)):
            sl = pl.ds(n * BLK, BLK)
            dv = d_ref[:, sl]
            dlf = _split3_dot(dv, msuf) + carry
            carry = carry + jnp.sum(dv, axis=1, keepdims=True)
            df = dlf * _sigmoid(-(f_ref[:, sl] + b_ref[...]))
            df_ref[:, sl] = df
            dbacc = dbacc + jnp.sum(df, axis=1, keepdims=True)
        db_ref[...] += dbacc

    spec = pl.BlockSpec((None, H, S), lambda b: (b, 0, 0))
    vec = pl.BlockSpec((H, 1), lambda b: (0, 0))
    return pl.pallas_call(
        body, name="fox_gate_bwd", grid=(B,),
        in_specs=[spec, spec, vec], out_specs=[spec, vec],
        out_shape=[jax.ShapeDtypeStruct((B, H, S), F32), jax.ShapeDtypeStruct((H, 1), F32)],
        compiler_params=_params(("arbitrary",)),
    )(dcum_t, f_t, b_f)


def _pick_col(block, idx, lane_iota):
    return jnp.sum(jnp.where(lane_iota == idx, block, 0.0), axis=1, keepdims=True)


def _pick_row(block, idx, sub_iota):
    return jnp.sum(jnp.where(sub_iota == idx, block, 0.0), axis=0, keepdims=True)


def _fox_fwd(proj3, cum_c, cum_t, W, heads):
    B, S, _ = proj3.shape
    H = heads
    Dh = W // heads
    hpb = LANES // Dh
    P, TQ = W // LANES, _query_tile(S)
    NQ = S // TQ
    scale = 1.0 / math.sqrt(Dh)

    TK = _key_tile(S)
    RS = hpb * TQ
    NM = max(1, TQ // TK)

    def body(q_ref, k_ref, v_ref, g_ref, cc_ref, ct_ref, o_ref, y_ref, lse_ref):
        p = pl.program_id(1)
        i = pl.program_id(2)
        lane_head = _lane_head(Dh)
        trow, col = _causal_iotas(RS, TK, TQ)
        lane_h = lax.broadcasted_iota(jnp.int32, (1, H), 1)
        sub_h = lax.broadcasted_iota(jnp.int32, (H, 1), 0)
        qs = (_stack_heads(q_ref[...], lane_head, hpb) * scale).astype(BF16)
        cc = cc_ref[...]
        c_q = jnp.concatenate([_pick_col(cc, p * hpb + h, lane_h) for h in range(hpb)], axis=0)
        nt = (i * TQ + TQ - 1) // TK + 1

        def tile(jt, carry, masked):
            mx, l, acc = carry
            off = pl.multiple_of(jt * TK, TK)
            kj = k_ref[pl.ds(off, TK), :].astype(BF16)
            vj = v_ref[pl.ds(off, TK), :].astype(BF16)
            ctb = ct_ref[:, pl.ds(off, TK)]
            z = _dot(qs, kj, _NT) + c_q
            s = jnp.concatenate([z[h * TQ:(h + 1) * TQ] - _pick_row(ctb, p * hpb + h, sub_h) for h in range(hpb)],
                                axis=0)
            if masked:
                s = jnp.where(col + (jt * TK - i * TQ) <= trow, s, NEG_BIG)
            mx2 = jnp.maximum(mx, jnp.max(s, axis=1, keepdims=True))
            pe = jnp.exp(s - mx2)
            alpha = jnp.exp(mx - mx2)
            return (mx2, alpha * l + jnp.sum(pe, axis=1, keepdims=True), alpha * acc + _dot(pe.astype(BF16), vj))

        carry = lax.fori_loop(
            0, nt - NM, lambda jt, c: tile(jt, c, False),
            (jnp.full((RS, 1), NEG_BIG, F32), jnp.zeros((RS, 1), F32), jnp.zeros((RS, LANES), F32)))
        for m in reversed(range(NM)):
            carry = tile(nt - 1 - m, carry, True)
        mx, l, acc = carry
        o = _unstack_heads(acc / l, lane_head, hpb)
        o_ref[...] = o
        lse_ref[...] = _unstack_heads(jnp.broadcast_to(mx + jnp.log(l), (RS, LANES)), lane_head, hpb)
        y_ref[...] = (o * _silu(g_ref[...])).astype(BF16)

    blk = lambda sec: pl.BlockSpec((None, TQ, LANES), lambda b, p, i: (b, i, sec * P + p))
    full = lambda sec: pl.BlockSpec((None, S, LANES), lambda b, p, i: (b, 0, sec * P + p))
    out = pl.BlockSpec((None, TQ, LANES), lambda b, p, i: (b, i, p))
    return pl.pallas_call(
        body, name="fox_fwd", grid=(B, P, NQ),
        in_specs=[blk(0), full(1), full(2), blk(3),
                  pl.BlockSpec((None, TQ, H), lambda b, p, i: (b, i, 0)),
                  pl.BlockSpec((None, H, S), lambda b, p, i: (b, 0, 0))],
        out_specs=[out, out, out],
        out_shape=[jax.ShapeDtypeStruct((B, S, W), F32), jax.ShapeDtypeStruct((B, S, W), BF16),
                   jax.ShapeDtypeStruct((B, S, W), F32)],
        compiler_params=_params(("parallel", "parallel", "arbitrary")),
    )(proj3, proj3, proj3, proj3, cum_c, cum_t)


def _fox_bwd(proj3, cum_c, cum_t, o, lse, dy, W, heads):
    B, S, _ = proj3.shape
    H = heads
    Dh = W // heads
    hpb = LANES // Dh
    P, TQ = W // LANES, _query_tile(S)
    NQ = S // TQ
    scale = 1.0 / math.sqrt(Dh)

    TK = _key_tile(S)
    RS = hpb * TQ
    NM = max(1, TQ // TK)

    def body(q_ref, k_ref, v_ref, g_ref, cc_ref, ct_ref, o_ref, lse_ref, dy_ref,
             dpj_ref, dc_ref, dk_ref, dv_ref, p_scr, dp_scr):
        p = pl.program_id(1)
        i = pl.program_id(2)
        rows = pl.ds(pl.multiple_of(i * TQ, TQ), TQ)

        @pl.when(i == 0)
        def _():
            dk_ref[...] = jnp.zeros_like(dk_ref)
            dv_ref[...] = jnp.zeros_like(dv_ref)
            dc_ref[...] = jnp.zeros_like(dc_ref)

        lane_head = _lane_head(Dh)
        trow, col = _causal_iotas(RS, TK, TQ)
        lane_h = lax.broadcasted_iota(jnp.int32, (1, H), 1)
        sub_h = lax.broadcasted_iota(jnp.int32, (H, 1), 0)
        lane = lax.broadcasted_iota(jnp.int32, (1, LANES), 1)
        g = g_ref[...]
        lsev = lse_ref[...]
        cc = cc_ref[...]
        dyv = dy_ref[...].astype(F32)
        dpj_ref[3, rows, :] = (dyv * o_ref[...] * _dsilu(g)).astype(dpj_ref.dtype)
        qs = (_stack_heads(q_ref[...], lane_head, hpb) * scale).astype(BF16)
        dos = _stack_heads(dyv * _silu(g), lane_head, hpb).astype(BF16)
        c_q = jnp.concatenate([_pick_col(cc, p * hpb + h, lane_h) for h in range(hpb)], axis=0)
        c_q = c_q -jnp.concatenate([_pick_col(lsev, h * Dh, lane) for h in range(hpb)], axis=0)
        nt = (i * TQ + TQ - 1) // TK + 1

        def probs(jt, dsum, masked):
            off = pl.multiple_of(jt * TK, TK)
            kj = k_ref[pl.ds(off, TK), :].astype(BF16)
            vj = v_ref[pl.ds(off, TK), :].astype(BF16)
            ctb = ct_ref[:, pl.ds(off, TK)]
            z = _dot(qs, kj, _NT) + c_q
            s = jnp.concatenate([z[h * TQ:(h + 1) * TQ] - _pick_row(ctb, p * hpb + h, sub_h) for h in range(hpb)],
                                axis=0)
            pr = jnp.exp(s)
            if masked:
                pr = jnp.where(col + (jt * TK - i * TQ) <= trow, pr, 0.0)
            dp = _dot(dos, vj, _NT)
            p_scr[jt] = pr
            dp_scr[jt] = dp
            dv_ref[pl.ds(off, TK), :] += _dot(pr.astype(BF16), dos, _TN)
            return dsum + jnp.sum(pr * dp, axis=1, keepdims=True)

        dsum = lax.fori_loop(0, nt - NM, lambda jt, d: probs(jt, d, False), jnp.zeros((RS, 1), F32))
        for m in reversed(range(NM)):
            dsum = probs(nt - 1 - m, dsum, True)

        def grads(jt, acc):
            off = pl.multiple_of(jt * TK, TK)
            kj = k_ref[pl.ds(off, TK), :].astype(BF16)
            ds = p_scr[jt] * (dp_scr[jt] - dsum)
            for h in range(hpb):
                dc_ref[h:h + 1, pl.ds(off, TK)] -= jnp.sum(ds[h * TQ:(h + 1) * TQ], axis=0, keepdims=True)
            dsb = ds.astype(BF16)
            dk_ref[pl.ds(off, TK), :] += _dot(dsb, qs, _TN)
            return acc + _dot(dsb, kj)

        acc = lax.fori_loop(0, nt, grads, jnp.zeros((RS, LANES), F32))
        dpj_ref[0, rows, :] = (_unstack_heads(acc, lane_head, hpb) * scale).astype(dpj_ref.dtype)

        @pl.when(i == NQ - 1)
        def _():
            dpj_ref[1] = dk_ref[...].astype(dpj_ref.dtype)
            dpj_ref[2] = dv_ref[...].astype(dpj_ref.dtype)

    blk = lambda sec: pl.BlockSpec((None, TQ, LANES), lambda b, p, i: (b, i, sec * P + p))
    full = lambda sec: pl.BlockSpec((None, S, LANES), lambda b, p, i: (b, 0, sec * P + p))
    one = pl.BlockSpec((None, TQ, LANES), lambda b, p, i: (b, i, p))
    return pl.pallas_call(
        body, name="fox_bwd", grid=(B, P, NQ),
        in_specs=[blk(0), full(1), full(2), blk(3),
                  pl.BlockSpec((None, TQ, H), lambda b, p, i: (b, i, 0)),
                  pl.BlockSpec((None, H, S), lambda b, p, i: (b, 0, 0)),
                  one, one, one],
        out_specs=[pl.BlockSpec((None, 4, S, LANES), lambda b, p, i: (b, 0, 0, p)),
                   pl.BlockSpec((None, None, hpb, S), lambda b, p, i: (b, p, 0, 0))],
        out_shape=[jax.ShapeDtypeStruct((B, 4, S, W), BF16), jax.ShapeDtypeStruct((B, P, hpb, S), F32)],
        scratch_shapes=[pltpu.VMEM((S, LANES), F32), pltpu.VMEM((S, LANES), F32),
                        pltpu.VMEM((S // TK, RS, TK), F32), pltpu.VMEM((S // TK, RS, TK), F32)],
        compiler_params=_params(("parallel", "parallel", "arbitrary")),
    )(proj3, proj3, proj3, proj3, cum_c, cum_t, o, lse, dy)


def _layernorm_rows(v, gamma, beta):
    mu = jnp.mean(v, axis=-1, keepdims=True)
    xc = v - mu
    rstd = lax.rsqrt(jnp.mean(xc * xc, axis=-1, keepdims=True) + EPS)
    xh = xc * rstd
    return xh, rstd, xh * gamma + beta


def _layernorm_rows_bwd(dout, xh, rstd, gamma):
    dxh = dout * gamma
    return rstd * (dxh - jnp.mean(dxh, axis=-1, keepdims=True) - xh * jnp.mean(dxh * xh, axis=-1, keepdims=True))


def _gmlp_fwd(proj, wm, bs_t, ln_g, ln_b, W):
    T = proj.shape[0]
    G = wm.shape[0]
    cg = W // G
    assert cg == LANES

    def body(p_ref, wm_ref, bs_ref, lg_ref, lb_ref, y_ref, vn_ref):
        lane = lax.broadcasted_iota(jnp.int32, (1, LANES), 1)
        _, _, vn = _layernorm_rows(_gelu(p_ref[:, W:2 * W]), lg_ref[...], lb_ref[...])
        vn_ref[...] = vn.astype(BF16)
        bs = bs_ref[...]
        for g in range(G):
            sl = pl.ds(g * cg, cg)
            s = _dot(wm_ref[g], vn_ref[:, sl]) + _pick_col(bs, g, lane)
            gate = p_ref[:, pl.ds(2 * W + g * cg, cg)]
            y_ref[:, sl] = (_gelu(p_ref[:, sl]) * s * _silu(gate)).astype(BF16)

    vec = pl.BlockSpec((1, W), lambda r: (0, 0))
    return pl.pallas_call(
        body, name="gmlp_fwd", grid=(T // BLK,),
        in_specs=[pl.BlockSpec((BLK, 3 * W), lambda r: (r, 0)),
                  pl.BlockSpec((G, BLK, BLK), lambda r: (0, 0, 0)),
                  pl.BlockSpec((BLK, LANES), lambda r: (0, 0)), vec, vec],
        out_specs=pl.BlockSpec((BLK, W), lambda r: (r, 0)),
        out_shape=jax.ShapeDtypeStruct((T, W), BF16),
        scratch_shapes=[pltpu.VMEM((BLK, W), BF16)],
        compiler_params=_params(("parallel",)),
    )(proj, wm, bs_t, ln_g, ln_b)


def _gmlp_bwd(proj, dy, wm, bs_t, ln_g, ln_b, W):
    T = proj.shape[0]
    G = wm.shape[0]
    cg = W // G

    def body(p_ref, dy_ref, wm_ref, bs_ref, lg_ref, lb_ref,
             dp_ref, dwm_ref, dbs_ref, dlg_ref, dlb_ref, vn_ref, dvn_ref):
        r = pl.program_id(0)

        @pl.when(r == 0)
        def _():
            dwm_ref[...] = jnp.zeros_like(dwm_ref)
            dbs_ref[...] = jnp.zeros_like(dbs_ref)
            dlg_ref[...] = jnp.zeros_like(dlg_ref)
            dlb_ref[...] = jnp.zeros_like(dlb_ref)

        lane = lax.broadcasted_iota(jnp.int32, (1, LANES), 1)
        vpre = p_ref[:, W:2 * W]
        gamma = lg_ref[...]
        xh, rstd, vn = _layernorm_rows(_gelu(vpre), gamma, lb_ref[...])
        vn_ref[...] = vn.astype(BF16)
        bs = bs_ref[...]
        dbs = jnp.zeros((BLK, LANES), F32)
        for g in range(G):
            sl = pl.ds(g * cg, cg)
            gsl = pl.ds(2 * W + g * cg, cg)
            vng = vn_ref[:, sl]
            s = _dot(wm_ref[g], vng) + _pick_col(bs, g, lane)
            upre = p_ref[:, sl]
            u = _gelu(upre)
            gate = p_ref[:, gsl]
            dyv = dy_ref[:, sl].astype(F32)
            dp_ref[:, gsl] = (dyv * u * s * _dsilu(gate)).astype(dp_ref.dtype)
            do = dyv * _silu(gate)
            dp_ref[:, sl] = (do * s * _dgelu(upre)).astype(dp_ref.dtype)
            ds = do * u
            dbs = dbs + jnp.where(lane == g, jnp.sum(ds, axis=1, keepdims=True), 0.0)
            dsb = ds.astype(BF16)
            dwm_ref[g] += _dot(dsb, vng, _NT)
            dvn_ref[:, sl] = _dot(wm_ref[g], dsb, _TN)
        dbs_ref[...] += dbs
        dvn = dvn_ref[...]
        dlg_ref[...] += jnp.sum(dvn * xh, axis=0, keepdims=True)
        dlb_ref[...] += jnp.sum(dvn, axis=0, keepdims=True)
        dv = _layernorm_rows_bwd(dvn, xh, rstd, gamma)
        dp_ref[:, W:2 * W] = (dv * _dgelu(vpre)).astype(dp_ref.dtype)

    vec = pl.BlockSpec((1, W), lambda r: (0, 0))
    return pl.pallas_call(
        body, name="gmlp_bwd", grid=(T // BLK,),
        in_specs=[pl.BlockSpec((BLK, 3 * W), lambda r: (r, 0)),
                  pl.BlockSpec((BLK, W), lambda r: (r, 0)),
                  pl.BlockSpec((G, BLK, BLK), lambda r: (0, 0, 0)),
                  pl.BlockSpec((BLK, LANES), lambda r: (0, 0)), vec, vec],
        out_specs=[pl.BlockSpec((BLK, 3 * W), lambda r: (r, 0)),
                   pl.BlockSpec((G, BLK, BLK), lambda r: (0, 0, 0)),
                   pl.BlockSpec((BLK, LANES), lambda r: (0, 0)), vec, vec],
        out_shape=[jax.ShapeDtypeStruct((T, 3 * W), BF16), jax.ShapeDtypeStruct((G, BLK, BLK), F32),
                   jax.ShapeDtypeStruct((BLK, LANES), F32),
                   jax.ShapeDtypeStruct((1, W), F32), jax.ShapeDtypeStruct((1, W), F32)],
        scratch_shapes=[pltpu.VMEM((BLK, W), BF16), pltpu.VMEM((BLK, W), F32)],
        compiler_params=_params(("arbitrary",)),
    )(proj, dy, wm, bs_t, ln_g, ln_b)


SUBLANES = 8
SHIFT_ROWS = CONV_HALO + BLK - SUBLANES


def _shift_rows(ext_ref, sh_ref, off):
    for r in range(1, SUBLANES):
        sh_ref[r - 1] = ext_ref[pl.ds(r, SHIFT_ROWS), pl.ds(off, LANES)]


def _rows_from(ext_ref, sh_ref, off, start):
    r = start % SUBLANES
    if r == 0:
        return ext_ref[pl.ds(start, BLK), pl.ds(off, LANES)]
    return sh_ref[r - 1, pl.ds(start - r, BLK), :]


def _conv_taps(ext_ref, sh_ref, cw_ref, off, n_taps, first):
    acc = jnp.zeros((BLK, LANES), F32)
    for k in range(n_taps):
        acc = acc + cw_ref[k:k + 1, pl.ds(off, LANES)] * _rows_from(ext_ref, sh_ref, off, first + k)
    return acc


def _fill_glu_ext(ext_ref, halo_ref, cur_ref, W, first_block):
    y0h = halo_ref[:, :W] * _sigmoid(halo_ref[:, W:])
    ext_ref[0:CONV_HALO, :] = jnp.where(first_block, 0.0, y0h)
    ext_ref[CONV_HALO:CONV_HALO + BLK, :] = cur_ref[:, :W] * _sigmoid(cur_ref[:, W:])


def _conv_specs(S, W):
    per = BLK // CONV_HALO
    cur = pl.BlockSpec((None, BLK, 2 * W), lambda b, i: (b, i, 0))
    halo = pl.BlockSpec((None, CONV_HALO, 2 * W), lambda b, i: (b, jnp.maximum(i * per - 1, 0), 0))
    gate = pl.BlockSpec((None, BLK, W), lambda b, i: (b, i, 2))
    return cur, halo, gate


def _conv_fwd(proj3, cw, cb, ln_g, ln_b, W):
    B, S, _ = proj3.shape
    K = cw.shape[0]
    first = CONV_HALO - (K - 1)
    assert first >= 0

    def body(cur_ref, halo_ref, g_ref, cw_ref, cb_ref, lg_ref, lb_ref, y_ref, ext_ref, y1_ref, sh_ref):
        i = pl.program_id(1)
        _fill_glu_ext(ext_ref, halo_ref, cur_ref, W, i == 0)

        def chan(c, _):
            off = pl.multiple_of(c * LANES, LANES)
            _shift_rows(ext_ref, sh_ref, off)
            y1_ref[:, pl.ds(off, LANES)] = (_conv_taps(ext_ref, sh_ref, cw_ref, off, K, first)
                                            + cb_ref[:, pl.ds(off, LANES)])
            return 0

        lax.fori_loop(0, W // LANES, chan, 0)
        _, _, ln = _layernorm_rows(y1_ref[...], lg_ref[...], lb_ref[...])
        y_ref[...] = (_silu(ln) * _silu(g_ref[...])).astype(BF16)

    cur, halo, gate = _conv_specs(S, W)
    vec = pl.BlockSpec((1, W), lambda b, i: (0, 0))
    return pl.pallas_call(
        body, name="conv_fwd", grid=(B, S // BLK),
        in_specs=[cur, halo, gate, pl.BlockSpec((K, W), lambda b, i: (0, 0)), vec, vec, vec],
        out_specs=pl.BlockSpec((None, BLK, W), lambda b, i: (b, i, 0)),
        out_shape=jax.ShapeDtypeStruct((B, S, W), BF16),
        scratch_shapes=[pltpu.VMEM((CONV_HALO + BLK, W), F32), pltpu.VMEM((BLK, W), F32),
                        pltpu.VMEM((SUBLANES - 1, SHIFT_ROWS, LANES), F32)],
        compiler_params=_params(("parallel", "parallel")),
    )(proj3, proj3, proj3, cw, cb, ln_g, ln_b)


def _conv_bwd1(proj3, dy, cw, cb, ln_g, ln_b, W, exch):
    B, S, _ = proj3.shape
    K = cw.shape[0]
    first = CONV_HALO - (K - 1)

    def body(cur_ref, halo_ref, g_ref, dy_ref, cw_ref, cb_ref, lg_ref, lb_ref,
             dy1_ref, dg_ref, dcw_ref, dcb_ref, dlg_ref, dlb_ref, ext_ref, y1_ref, sh_ref):
        b = pl.program_id(0)
        i = pl.program_id(1)

        @pl.when(jnp.logical_and(b == 0, i == 0))
        def _():
            dcw_ref[...] = jnp.zeros_like(dcw_ref)
            dcb_ref[...] = jnp.zeros_like(dcb_ref)
            dlg_ref[...] = jnp.zeros_like(dlg_ref)
            dlb_ref[...] = jnp.zeros_like(dlb_ref)

        _fill_glu_ext(ext_ref, halo_ref, cur_ref, W, i == 0)

        def chan(c, _):
            off = pl.multiple_of(c * LANES, LANES)
            _shift_rows(ext_ref, sh_ref.at[c], off)
            y1_ref[:, pl.ds(off, LANES)] = (_conv_taps(ext_ref, sh_ref.at[c], cw_ref, off, K, first)
                                            + cb_ref[:, pl.ds(off, LANES)])
            return 0

        lax.fori_loop(0, W // LANES, chan, 0)
        gamma = lg_ref[...]
        xh, rstd, ln = _layernorm_rows(y1_ref[...], gamma, lb_ref[...])
        g = g_ref[...]
        dyv = dy_ref[...].astype(F32)
        dg_ref[...] = (dyv * _silu(ln) * _dsilu(g)).astype(dg_ref.dtype)
        dln = dyv * _silu(g) * _dsilu(ln)
        dlg_ref[...] += jnp.sum(dln * xh, axis=0, keepdims=True)
        dlb_ref[...] += jnp.sum(dln, axis=0, keepdims=True)
        dy1 = _layernorm_rows_bwd(dln, xh, rstd, gamma)
        dy1_ref[...] = dy1
        dcb_ref[...] += jnp.sum(dy1, axis=0, keepdims=True)

        def chan_w(c, _):
            off = pl.multiple_of(c * LANES, LANES)
            d = dy1_ref[:, pl.ds(off, LANES)]
            for k in range(K):
                dcw_ref[k:k + 1, pl.ds(off, LANES)] += jnp.sum(
                    d * _rows_from(ext_ref, sh_ref.at[c], off, first + k), axis=0, keepdims=True)
            return 0

        lax.fori_loop(0, W // LANES, chan_w, 0)

    cur, halo, gate = _conv_specs(S, W)
    vec = pl.BlockSpec((1, W), lambda b, i: (0, 0))
    taps = pl.BlockSpec((K, W), lambda b, i: (0, 0))
    one = pl.BlockSpec((None, BLK, W), lambda b, i: (b, i, 0))
    return _call_hosting(
        body, exch, "conv_bwd1", (B, S // BLK), [cur, halo, gate, one, taps, vec, vec, vec],
        [one, one, taps, vec, vec, vec],
        [jax.ShapeDtypeStruct((B, S, W), F32), jax.ShapeDtypeStruct((B, S, W), BF16),
         jax.ShapeDtypeStruct((K, W), F32)] + [jax.ShapeDtypeStruct((1, W), F32)] * 3,
        [pltpu.VMEM((CONV_HALO + BLK, W), F32), pltpu.VMEM((BLK, W), F32),
         pltpu.VMEM((W // LANES, SUBLANES - 1, SHIFT_ROWS, LANES), F32)],
        (proj3, proj3, proj3, dy, cw, cb, ln_g, ln_b))


def _conv_bwd2(proj3, dy1, dgate, cw_rev, W):
    B, S, _ = proj3.shape
    K = cw_rev.shape[0]
    NQ = S // BLK
    per = BLK // CONV_HALO

    def body(cur_ref, d_ref, dnext_ref, dgate_ref, cw_ref, dp_ref, ext_ref, dy0_ref, sh_ref):
        i = pl.program_id(1)
        ext_ref[0:BLK, :] = d_ref[...]
        ext_ref[BLK:BLK + CONV_HALO, :] = jnp.where(i == NQ - 1, 0.0, dnext_ref[...])

        def chan(c, _):
            off = pl.multiple_of(c * LANES, LANES)
            _shift_rows(ext_ref, sh_ref, off)
            dy0_ref[:, pl.ds(off, LANES)] = _conv_taps(ext_ref, sh_ref, cw_ref, off, K, 0)
            return 0

        lax.fori_loop(0, W // LANES, chan, 0)
        a = cur_ref[:, :W]
        sg = _sigmoid(cur_ref[:, W:])
        dy0 = dy0_ref[...]
        dp_ref[:, 0:W] = (dy0 * sg).astype(dp_ref.dtype)
        dp_ref[:, W:2 * W] = (dy0 * a * sg * (1.0 - sg)).astype(dp_ref.dtype)
        dp_ref[:, 2 * W:3 * W] = dgate_ref[...]

    cur = pl.BlockSpec((None, BLK, 2 * W), lambda b, i: (b, i, 0))
    one = pl.BlockSpec((None, BLK, W), lambda b, i: (b, i, 0))
    nxt = pl.BlockSpec((None, CONV_HALO, W), lambda b, i: (b, jnp.minimum((i + 1) * per, S // CONV_HALO - 1), 0))
    return pl.pallas_call(
        body, name="conv_bwd2", grid=(B, NQ),
        in_specs=[cur, one, nxt, one, pl.BlockSpec((K, W), lambda b, i: (0, 0))],
        out_specs=pl.BlockSpec((None, BLK, 3 * W), lambda b, i: (b, i, 0)),
        out_shape=jax.ShapeDtypeStruct((B, S, 3 * W), BF16),
        scratch_shapes=[pltpu.VMEM((BLK + CONV_HALO, W), F32), pltpu.VMEM((BLK, W), F32),
                        pltpu.VMEM((SUBLANES - 1, SHIFT_ROWS, LANES), F32)],
        compiler_params=_params(("parallel", "parallel")),
    )(proj3, dy1, dy1, dgate, cw_rev)


def _pack(arrays):
    flat = jnp.concatenate([a.astype(F32).reshape(-1) for a in arrays])
    n = flat.shape[0]
    pad = (-n) % (8 * LANES)
    if pad:
        flat = jnp.concatenate([flat, jnp.zeros((pad,), F32)])
    return flat.reshape(-1, LANES)


def _unpack(packed, shapes, lead=()):
    flat = packed.reshape(lead + (-1,))
    out, off = [], 0
    for shp in shapes:
        n = math.prod(shp)
        out.append(flat[..., off:off + n].reshape(lead + tuple(shp)))
        off += n
    return out


def _cols_from_dev(g):
    g = jnp.moveaxis(g, 0, -2)
    return g.reshape(g.shape[:-2] + (g.shape[-2] * g.shape[-1],))


def _my_cols(full, me):
    n8 = full.shape[-1] // N_DEV
    return lax.dynamic_slice_in_dim(full, me * n8, n8, axis=full.ndim - 1)


def kernel(x, a_norm, a_w_in, a_w_out, b_norm, b_w_in, b_v_ln_g, b_v_ln_b, b_w_s, b_b_s, b_w_out, c_norm, c_w_in, c_conv_w, c_conv_b, c_ln_g, c_ln_b, c_w_out, d_norm, d_w_in, d_b_f, d_w_out, final_norm, loss_target, m_a_norm, m_a_w_in, m_a_w_out, m_b_norm, m_b_w_in, m_b_v_ln_g, m_b_v_ln_b, m_b_w_s, m_b_b_s, m_b_w_out, m_c_norm, m_c_w_in, m_c_conv_w, m_c_conv_b, m_c_ln_g, m_c_ln_b, m_c_w_out, m_d_norm, m_d_w_in, m_d_b_f, m_d_w_out, m_final_norm, v_a_norm, v_a_w_in, v_a_w_out, v_b_norm, v_b_w_in, v_b_v_ln_g, v_b_v_ln_b, v_b_w_s, v_b_b_s, v_b_w_out, v_c_norm, v_c_w_in, v_c_conv_w, v_c_conv_b, v_c_ln_g, v_c_ln_b, v_c_w_out, v_d_norm, v_d_w_in, v_d_b_f, v_d_w_out, v_final_norm):
    B, S, D = x.shape
    T = B * S
    xi, yi, ci = _me()
    me = 4 * xi + 2 * yi + ci

    G = b_w_s.shape[1]
    KC = c_conv_w.shape[1]
    H_D = d_b_f.shape[1]
    W_A = a_w_out.shape[1] * N_DEV
    W_B = b_w_out.shape[1] * N_DEV
    W_C = c_w_out.shape[1] * N_DEV
    W_D = d_w_out.shape[1] * N_DEV
    N_D = d_w_in.shape[2] * N_DEV
    N_D_PAD = -(-N_D // (3 * LANES)) * (3 * LANES)

    big_names = ["a_w_in", "a_w_out", "b_w_in", "b_w_out", "c_w_in", "c_w_out", "d_w_in", "d_w_out"]
    big_w = dict(a_w_in=a_w_in[0], a_w_out=a_w_out[0], b_w_in=b_w_in[0], b_w_out=b_w_out[0],
                 c_w_in=c_w_in[0], c_w_out=c_w_out[0], d_w_in=d_w_in[0], d_w_out=d_w_out[0])
    small_sharded = [b_norm, b_v_ln_g, b_v_ln_b, c_norm, c_conv_w, c_conv_b, c_ln_g, c_ln_b, d_norm]
    first_names, later_names = big_names[:1], big_names[1:]
    gathered = _Exchange([big_w[n].astype(BF16) for n in first_names] + [_pack(small_sharded)],
                         ["gather"] * (len(first_names) + 1)).run("gather_first")
    wg = dict(zip(first_names, gathered[:-1]))
    (b_norm_f, b_lg_f, b_lb_f, c_norm_f, c_cw_f, c_cb_f, c_lg_f, c_lb_f, d_norm_f) = [
        _cols_from_dev(t) for t in _unpack(gathered[-1], [s.shape for s in small_sharded], lead=(N_DEV,))]
    c_cw_f = c_cw_f[0]

    wm =jnp.tril(b_w_s[0]).astype(BF16)
    bs_t = jnp.pad(b_b_s[0].T, ((0, 0), (0, LANES - G)))

    x0 = x.reshape(T, D)
    h_a = _rmsnorm_fwd(x0, a_norm, "rms_a")
    proj_a = _mm_w_dev(h_a, wg["a_w_in"], "proj_a").reshape(B, S, 4 * W_A)
    (o_a, y_a), later = _sb_fwd(proj_a, W_A, SB_HEADS,
                                _Exchange([big_w[n].astype(BF16) for n in later_names], ["gather"] * len(later_names)))
    wg.update(zip(later_names, later))
    a_w_out_f = wg["a_w_out"].reshape(W_A, D)
    b_w_out_f = wg["b_w_out"].reshape(W_B, D)
    c_w_out_f = wg["c_w_out"].reshape(W_C, D)
    d_w_out_f = wg["d_w_out"].reshape(W_D, D)
    d_w_in_f = jnp.pad(_cols_from_dev(wg["d_w_in"]), ((0, 0), (0, N_D_PAD - N_D)))
    y_a = y_a.reshape(T, W_A)
    x1 = _mm(y_a, a_w_out_f, "nn", T, D, W_A, F32, "out_a", 512, D, W_A, res=x0)
    h_b = _rmsnorm_fwd(x1, b_norm_f, "rms_b")
    proj_b = _mm_w_dev(h_b, wg["b_w_in"], "proj_b")
    y_b = _gmlp_fwd(proj_b, wm, bs_t, b_lg_f, b_lb_f, W_B)
    x2 = _mm(y_b, b_w_out_f, "nn", T, D, W_B, F32, "out_b", 512, D, W_B, res=x1)
    h_c = _rmsnorm_fwd(x2, c_norm_f, "rms_c")
    proj_c = _mm_w_dev(h_c, wg["c_w_in"], "proj_c").reshape(B, S, 3 * W_C)
    y_c = _conv_fwd(proj_c, c_cw_f, c_cb_f, c_lg_f, c_lb_f, W_C).reshape(T, W_C)
    x3 = _mm(y_c, c_w_out_f, "nn", T, D, W_C, F32, "out_c", 512, D, W_C, res=x2)
    h_d = _rmsnorm_fwd(x3, d_norm_f, "rms_d")
    proj_d = _mm(h_d, d_w_in_f, "nn", T, N_D_PAD, D, F32, "proj_d", 1024, 384, D).reshape(B, S, N_D_PAD)
    f_t = jnp.swapaxes(proj_d[:, :, 4 * W_D:4 * W_D + H_D], 1, 2)
    b_f_col = d_b_f.reshape(H_D, 1)
    cum_t = _fox_gate_fwd(f_t, b_f_col)
    cum_c = jnp.swapaxes(cum_t, 1, 2)
    o_d, y_d, lse_d = _fox_fwd(proj_d, cum_c, cum_t, W_D, H_D)
    y_d = y_d.reshape(T, W_D)
    x4 = _mm(y_d, d_w_out_f, "nn", T, D, W_D, F32, "out_d", 512, D, W_D, res=x3)

    loss_part, dx, g_final = _loss_head(x4, final_norm.reshape(1, D), loss_target.reshape(T, D))
    loss = lax.psum(loss_part[0, 0], MESH_AXES)

    dy_d = _mm(dx, d_w_out_f, "nt", T, W_D, D, BF16, "dy_d", 512, W_D, D).reshape(B, S, W_D)
    gw_d_out = _mm(y_d, dx, "tn", W_D, D, T, BF16, "gw_d_out", W_D, D, 512).reshape(N_DEV, W_D // N_DEV, D)
    dproj_d, dcum = _fox_bwd(proj_d, cum_c, cum_t, o_d, lse_d, dy_d, W_D, H_D)
    df_t, g_b_f = _fox_gate_bwd(dcum.reshape(B, H_D, S), f_t, b_f_col)
    F_PAD = N_D_PAD - 4 * W_D
    df = jnp.pad(jnp.swapaxes(df_t, 1, 2), ((0, 0), (0, 0), (0, F_PAD - H_D))).reshape(T, F_PAD)
    tc = min(512, W_D)
    gw_main = _mm(h_d, dproj_d, "tn", D, 4 * W_D, T, BF16, "gw_d_in", D, tc, 512,
                  b_spec=_sectioned_spec(dproj_d, 512, tc, 2, 1))
    gw_f = _mm(h_d, df, "tn", D, F_PAD, T, BF16, "gw_d_in_f", D, F_PAD, 512)
    gw_d_in = jnp.moveaxis(
        jnp.concatenate([gw_main, gw_f], axis=1)[:, :N_D].reshape(D, N_DEV, N_D // N_DEV), 1, 0)
    dh = _mm(dproj_d, d_w_in_f, "nt", T, D, 4 * W_D, F32, "dh_d", 512, D, tc,
             a_spec=_sectioned_spec(dproj_d, 512, tc, 0, 2))
    dh = _mm(df, d_w_in_f[:, 4 * W_D:], "nt", T, D, F_PAD, F32, "dh_d_f", 512, D, F_PAD, res=dh)
    dx, g_d_norm = _rmsnorm_bwd(x3, d_norm_f, dh, dx, "rms_bwd_d")

    dy_c = _mm(dx, c_w_out_f, "nt", T, W_C, D, BF16, "dy_c", 512, W_C, D).reshape(B, S, W_C)
    gw_c_out = _mm(y_c, dx, "tn", W_C, D, T, BF16, "gw_c_out", 1024, D, 512).reshape(N_DEV, W_C // N_DEV, D)
    (dy1, dgate_c, g_c_cw, g_c_cb, g_c_lg, g_c_lb), parts_d = _conv_bwd1(
        proj_c, dy_c, c_cw_f, c_cb_f, c_lg_f, c_lb_f, W_C, _Exchange([gw_d_in, gw_d_out], ["scatter"] * 2))
    dproj_c = _conv_bwd2(proj_c, dy1, dgate_c, c_cw_f[::-1], W_C).reshape(T, 3 * W_C)
    gw_c_in = _mm_grad_dev(h_c, dproj_c, "gw_c_in")
    dh = _mm_wT_dev(dproj_c, wg["c_w_in"], "dh_c")
    dx, g_c_norm = _rmsnorm_bwd(x2, c_norm_f, dh, dx, "rms_bwd_c")

    dy_b = _mm(dx, b_w_out_f, "nt", T, W_B, D, BF16, "dy_b", 512, W_B, D)
    gw_b_out = _mm(y_b, dx, "tn", W_B, D, T, BF16, "gw_b_out", 1024, D, 512).reshape(N_DEV, W_B // N_DEV, D)
    dproj_b, g_wm, g_bs_t, g_b_lg, g_b_lb = _gmlp_bwd(proj_b, dy_b, wm, bs_t, b_lg_f, b_lb_f, W_B)
    g_b_w_s = jnp.tril(g_wm)
    g_b_b_s = g_bs_t[:, :G].T
    gw_b_in = _mm_grad_dev(h_b, dproj_b, "gw_b_in")
    dh = _mm_wT_dev(dproj_b, wg["b_w_in"], "dh_b")
    dx, g_b_norm = _rmsnorm_bwd(x1, b_norm_f, dh, dx, "rms_bwd_b")

    dy_a = _mm(dx, a_w_out_f, "nt", T, W_A, D, BF16, "dy_a", 512, W_A, D).reshape(B, S, W_A)
    gw_a_out = _mm(y_a, dx, "tn", W_A, D, T, BF16, "gw_a_out", W_A, D, 512).reshape(N_DEV, W_A // N_DEV, D)
    small_full = [g_b_norm, g_b_lg, g_b_lb, g_b_w_s, g_b_b_s, g_c_norm, g_c_cw, g_c_cb, g_c_lg, g_c_lb,
                  g_d_norm, g_b_f, g_final]
    dproj_a, parts_s = _sb_bwd(
        proj_a, o_a, dy_a, W_A, SB_HEADS,
        _Exchange([gw_c_in, gw_c_out, gw_b_in, gw_b_out, gw_a_out, _pack(small_full)], ["scatter"] * 5 + ["gather"]))
    gw_a_in = _mm_grad_dev(h_a, dproj_a, "gw_a_in")
    dh, parts_a = _mm_wT_dev(dproj_a, wg["a_w_in"], "dh_a", exch=_Exchange([gw_a_in], ["scatter"]))
    dx, g_a_norm = _rmsnorm_bwd(x0, a_norm, dh, dx, "rms_bwd_a")
    grad_x = dx.reshape(B, S, D)

    (parts_n,) = _Exchange([_pack([g_a_norm])], ["gather"]).run("exchange_last")
    big_parts = dict(a_w_in=parts_a[0], a_w_out=parts_s[4], b_w_in=parts_s[2], b_w_out=parts_s[3],
                     c_w_in=parts_s[0], c_w_out=parts_s[1], d_w_in=parts_d[0], d_w_out=parts_d[1])
    (s_b_norm, s_b_lg, s_b_lb, s_b_w_s, s_b_b_s, s_c_norm, s_c_cw, s_c_cb, s_c_lg, s_c_lb,
     s_d_norm, s_b_f, s_final) = _unpack(_sum_parts(parts_s[5], "sum_small"), [g.shape for g in small_full])
    (s_a_norm,) = _unpack(_sum_parts(parts_n, "sum_a_norm"), [g_a_norm.shape])

    weights = dict(a_norm=a_norm, a_w_in=a_w_in, a_w_out=a_w_out, b_norm=b_norm, b_w_in=b_w_in, b_v_ln_g=b_v_ln_g,
                   b_v_ln_b=b_v_ln_b, b_w_s=b_w_s, b_b_s=b_b_s, b_w_out=b_w_out, c_norm=c_norm, c_w_in=c_w_in,
                   c_conv_w=c_conv_w, c_conv_b=c_conv_b, c_ln_g=c_ln_g, c_ln_b=c_ln_b, c_w_out=c_w_out,
                   d_norm=d_norm, d_w_in=d_w_in, d_b_f=d_b_f, d_w_out=d_w_out, final_norm=final_norm)
    mom_m = dict(a_norm=m_a_norm, a_w_in=m_a_w_in, a_w_out=m_a_w_out, b_norm=m_b_norm, b_w_in=m_b_w_in,
                 b_v_ln_g=m_b_v_ln_g, b_v_ln_b=m_b_v_ln_b, b_w_s=m_b_w_s, b_b_s=m_b_b_s, b_w_out=m_b_w_out,
                 c_norm=m_c_norm, c_w_in=m_c_w_in, c_conv_w=m_c_conv_w, c_conv_b=m_c_conv_b, c_ln_g=m_c_ln_g,
                 c_ln_b=m_c_ln_b, c_w_out=m_c_w_out, d_norm=m_d_norm, d_w_in=m_d_w_in, d_b_f=m_d_b_f,
                 d_w_out=m_d_w_out, final_norm=m_final_norm)
    mom_v = dict(a_norm=v_a_norm, a_w_in=v_a_w_in, a_w_out=v_a_w_out, b_norm=v_b_norm, b_w_in=v_b_w_in,
                 b_v_ln_g=v_b_v_ln_g, b_v_ln_b=v_b_v_ln_b, b_w_s=v_b_w_s, b_b_s=v_b_b_s, b_w_out=v_b_w_out,
                 c_norm=v_c_norm, c_w_in=v_c_w_in, c_conv_w=v_c_conv_w, c_conv_b=v_c_conv_b, c_ln_g=v_c_ln_g,
                 c_ln_b=v_c_ln_b, c_w_out=v_c_w_out, d_norm=v_d_norm, d_w_in=v_d_w_in, d_b_f=v_d_b_f,
                 d_w_out=v_d_w_out, final_norm=v_final_norm)
    order = list(weights)
    grads, deltas, new_m, new_v = {}, {}, {}, {}

    for n in big_names:
        part = big_parts[n]
        shp = weights[n].shape
        R, C = shp[1], shp[2]
        res = _adamw(part, weights[n].reshape(R, C), mom_m[n].reshape(R, C), mom_v[n].reshape(R, C), "adamw_" + n)
        grads[n], deltas[n], new_m[n], new_v[n] = [r.reshape(shp) for r in res]

    small_g = dict(
        a_norm=s_a_norm, b_norm=_my_cols(s_b_norm, me), b_v_ln_g=_my_cols(s_b_lg, me),
        b_v_ln_b=_my_cols(s_b_lb, me), b_w_s=s_b_w_s[None], b_b_s=s_b_b_s[None], c_norm=_my_cols(s_c_norm, me),
        c_conv_w=_my_cols(s_c_cw, me)[None], c_conv_b=_my_cols(s_c_cb, me), c_ln_g=_my_cols(s_c_lg, me),
        c_ln_b=_my_cols(s_c_lb, me), d_norm=_my_cols(s_d_norm, me), d_b_f=s_b_f.reshape(1, H_D),
        final_norm=s_final.reshape(D))
    small_names = list(small_g)
    sg_p = _pack([small_g[n] for n in small_names])
    res = _adamw(sg_p[None], _pack([weights[n] for n in small_names]), _pack([mom_m[n] for n in small_names]),
                 _pack([mom_v[n] for n in small_names]), "adamw_small")
    shapes = [weights[n].shape for n in small_names]
    for dst, r in zip((grads, deltas, new_m, new_v), res):
        for n, val in zip(small_names, _unpack(r, shapes)):
            dst[n] = val

    return (loss, grad_x, *[grads[n] for n in order], *[deltas[n] for n in order],
            *[new_m[n] for n in order], *[new_v[n] for n in order])
```

```python
import functools
import math

import jax
import jax.numpy as jnp
from jax import lax
from jax.experimental import pallas as pl
from jax.experimental.pallas import tpu as pltpu

F32 = jnp.float32
BF16 = jnp.bfloat16

EPS = 1e-6
SB_HEADS = 16
CONV_HALO = 32
BLK = 128
ATT_TK = 256
ATT_TQ = 512
ATT_GP = 2
LANES = 128
N_DEV = 8
MESH_AXES = ("x", "y", "c")

ADAM_LR = 0.001
ADAM_B1 = 0.9
ADAM_B2 = 0.999
ADAM_EPS = 1e-08
ADAM_WD = 0.01
ADAM_STEP = 10

VMEM_LIMIT = 56 * 1024 * 1024
NEG_BIG = -1e30

_NN = (((1,), (0,)), ((), ()))
_NT = (((1,), (1,)), ((), ()))
_TN = (((0,), (0,)), ((), ()))


def _dot(a, b, dims=_NN):
    return lax.dot_general(a, b, dims, preferred_element_type=F32)


def _split_dot(x, m):
    hi = x.astype(BF16)
    lo = (x - hi.astype(F32)).astype(BF16)
    return _dot(hi, m) + _dot(lo, m)


def _split3_dot(x, m):
    hi = x.astype(BF16)
    r1 = x - hi.astype(F32)
    mid = r1.astype(BF16)
    lo = (r1 - mid.astype(F32)).astype(BF16)
    return _dot(hi, m) + _dot(mid, m) + _dot(lo, m)


def _params(sem=None):
    kw = dict(vmem_limit_bytes=VMEM_LIMIT)
    if sem is not None:
        kw["dimension_semantics"] = sem
    return pltpu.CompilerParams(**kw)


def _sigmoid(x):
    return jax.nn.sigmoid(x)


def _silu(x):
    return x * _sigmoid(x)


def _dsilu(x):
    s = _sigmoid(x)
    return s * (1.0 + x * (1.0 - s))


_GELU_C = math.sqrt(2.0 / math.pi)


def _gelu(x):
    return 0.5 * x * (1.0 + jnp.tanh(_GELU_C * (x + 0.044715 * x * x * x)))


def _dgelu(x):
    th = jnp.tanh(_GELU_C * (x + 0.044715 * x * x * x))
    return 0.5 * (1.0 + th) + 0.5 * x * (1.0 - th * th) * _GELU_C * (1.0 + 3.0 * 0.044715 * x * x)


def _mm(a, b, mode, M, N, K, out_dtype, name, tm, tn, tk, a_spec=None, b_spec=None, o_spec=None, out_shape=None,
        exch=None, res=None):
    tm, tn, tk = min(tm, M), min(tn, N), min(tk, K)
    assert M % tm == 0 and N % tn == 0 and K % tk == 0, (name, M, N, K, tm, tn, tk)
    nk = K // tk
    dims = {"nn": _NN, "nt": _NT, "tn": _TN}[mode]
    if a_spec is None:
        a_spec = (pl.BlockSpec((tk, tm), lambda i, j, k: (k, i)) if mode == "tn"
                  else pl.BlockSpec((tm, tk), lambda i, j, k: (i, k)))
    if b_spec is None:
        b_spec = (pl.BlockSpec((tn, tk), lambda i, j, k: (j, k)) if mode == "nt"
                  else pl.BlockSpec((tk, tn), lambda i, j, k: (k, j)))
    if o_spec is None:
        o_spec = pl.BlockSpec((tm, tn), lambda i, j, k: (i, j))
    if out_shape is None:
        out_shape = (M, N)

    def body(a_ref, b_ref, *rest):
        res_ref = rest[0] if res is not None else None
        if nk == 1:
            o_ref = rest[-1]
            d = _dot(a_ref[...].astype(BF16), b_ref[...].astype(BF16), dims)
            o_ref[...] = (d if res_ref is None else res_ref[...].astype(F32) + d).astype(o_ref.dtype)
            return
        o_ref, acc_ref = rest[-2:]
        k = pl.program_id(2)

        @pl.when(k == 0)
        def _():
            acc_ref[...] = jnp.zeros_like(acc_ref) if res_ref is None else res_ref[...].astype(F32)

        acc_ref[...] += _dot(a_ref[...].astype(BF16), b_ref[...].astype(BF16), dims)

        @pl.when(k == nk - 1)
        def _():
            o_ref[...] = acc_ref[...].astype(o_ref.dtype)

    in_specs, args = [a_spec, b_spec], (a, b)
    if res is not None:
        in_specs, args = in_specs + [o_spec], args + (res,)
    scratch = [pltpu.VMEM((tm, tn), F32)] if nk > 1 else []
    if exch is None:
        return pl.pallas_call(
            body, name=name, grid=(M // tm, N // tn, nk),
            in_specs=in_specs, out_specs=o_spec,
            out_shape=jax.ShapeDtypeStruct(out_shape, out_dtype),
            scratch_shapes=scratch,
            compiler_params=_params(("parallel", "parallel", "arbitrary")),
        )(*args)
    (out,), moved = _call_hosting(
        body, exch, name, (M // tm, N // tn, nk), in_specs, [o_spec],
        [jax.ShapeDtypeStruct(out_shape, out_dtype)], scratch, args)
    return out, moved


def _mm_w_dev(a, w3, name, out_dtype=F32, tm=1024):
    M, K = a.shape
    n8 = w3.shape[2]
    tn = n8 if n8 <= 768 else 512
    per = n8 // tn
    b_spec = pl.BlockSpec((None, K, tn), lambda i, j, k: (j // per, 0, j % per))
    return _mm(a, w3, "nn", M, N_DEV * n8, K, out_dtype, name, tm, tn, K, b_spec=b_spec)


def _sectioned_spec(d4, t_rows, t_cols, rows_axis, cols_axis):
    _, _, S, W = d4.shape
    assert S % t_rows == 0 and W % t_cols == 0
    rb, cb = S // t_rows, W // t_cols

    def index(*g):
        r, c = g[rows_axis], g[cols_axis]
        return (r // rb, c // cb, r % rb, c % cb)

    return pl.BlockSpec((None, None, t_rows, t_cols), index)


def _mm_wT_dev(a, w3, name, out_dtype=F32, tm=512, exch=None):
    K, n8 = w3.shape[1], w3.shape[2]
    tk = n8 if n8 <= 768 else 512
    per = n8 // tk
    b_spec = pl.BlockSpec((None, K, tk), lambda i, j, k: (k // per, 0, k % per))
    if a.ndim == 4:
        M, N = a.shape[0] * a.shape[2], a.shape[1] * a.shape[3]
        a_spec = _sectioned_spec(a, tm, tk, 0, 2)
    else:
        (M, N), a_spec = a.shape, None
    return _mm(a, w3, "nt", M, K, N, out_dtype, name, tm, K, tk, a_spec=a_spec, b_spec=b_spec, exch=exch)


def _mm_grad_dev(h, d, name, out_dtype=BF16):
    T, M = h.shape
    N = d.shape[1] * d.shape[3] if d.ndim == 4 else d.shape[1]
    n8 = N // N_DEV
    tn = n8 if n8 <= 768 else 512
    per = n8 // tn
    tm = min(M, 1024)
    o_spec = pl.BlockSpec((None, tm, tn), lambda i, j, k: (j // per, i, j % per))
    b_spec = _sectioned_spec(d, 512, tn, 2, 1) if d.ndim == 4 else None
    return _mm(h, d, "tn", M, N, T, out_dtype, name, tm, tn, 512, b_spec=b_spec, o_spec=o_spec,
               out_shape=(N_DEV, M, n8))


def _me():
    x, y, c = lax.axis_index("x"), lax.axis_index("y"), lax.axis_index("c")
    return x, y, c


def _peer(r):
    x, y, c = _me()
    px = 1 - x if (r >> 2) & 1 else x
    py = 1 - y if (r >> 1) & 1 else y
    pc = 1 - c if r & 1 else c
    return (px, py, pc), 4 * px + 2 * py + pc


class _Exchange:
    def __init__(self, arrays, kinds):
        self.arrays, self.kinds, self.n = list(arrays), list(kinds), len(arrays)
        self.out_shapes = [
            jax.ShapeDtypeStruct((N_DEV,) + a.shape if kind == "gather" else a.shape, a.dtype)
            for a, kind in zip(arrays, kinds)]
        self.specs = [pl.BlockSpec(memory_space=pl.ANY)] * self.n
        self.sems = [pltpu.SemaphoreType.DMA((self.n, N_DEV - 1)), pltpu.SemaphoreType.DMA((self.n, N_DEV - 1)),
                     pltpu.SemaphoreType.DMA((self.n,))]

    def _copies(self, ins, outs, sems, receiving):
        send_sems, recv_sems, local_sems = sems
        x, y, c = _me()
        me = 4 * x + 2 * y + c

        def src(k, pid):
            return ins[k] if self.kinds[k] == "gather" else ins[k].at[pid]

        local = [pltpu.make_async_copy(src(k, me), outs[k].at[me], local_sems.at[k]) for k in range(self.n)]
        remote = []
        for r in range(1, N_DEV):
            peer, pid = _peer(r)
            for k in range(self.n):
                remote.append(pltpu.make_async_remote_copy(
                    src_ref=src(k, pid), dst_ref=outs[k].at[pid if receiving else me],
                    send_sem=send_sems.at[k, r - 1], recv_sem=recv_sems.at[k, r - 1],
                    device_id=peer, device_id_type=pl.DeviceIdType.MESH))
        return local, remote

    def start(self, ins, outs, sems):
        local, remote = self._copies(ins, outs, sems, False)
        for cp in local + remote:
            cp.start()

    def wait(self, ins, outs, sems):
        local, remote = self._copies(ins, outs, sems, True)
        for cp in remote:
            cp.wait_recv()
        for cp in remote:
            cp.wait_send()
        for cp in local:
            cp.wait()

    def run(self, name):
        n = self.n

        def body(*refs):
            ins, outs, sems = refs[:n], refs[n:2 * n], refs[2 * n:]
            self.start(ins, outs, sems)
            self.wait(ins, outs, sems)

        return pl.pallas_call(
            body, name=name, in_specs=self.specs, out_specs=self.specs, out_shape=self.out_shapes,
            scratch_shapes=self.sems,
        )(*self.arrays)


def _call_hosting(body, exch, name, grid, in_specs, out_specs, out_shape, scratch_shapes, args):
    if exch is None:
        res = pl.pallas_call(
            body, name=name, grid=grid, in_specs=list(in_specs), out_specs=list(out_specs),
            out_shape=list(out_shape), scratch_shapes=list(scratch_shapes),
            compiler_params=_params(("arbitrary",) * len(grid)))(*args)
        return res, []
    n_in, n_out, n_scr, nc = len(in_specs), len(out_specs), len(scratch_shapes), exch.n

    def full_body(*refs):
        ins, refs = refs[:n_in], refs[n_in:]
        cins, refs = refs[:nc], refs[nc:]
        outs, refs = refs[:n_out], refs[n_out:]
        couts, refs = refs[:nc], refs[nc:]
        scr, sems = refs[:n_scr], refs[n_scr:]
        ids = [pl.program_id(a) for a in range(len(grid))]
        first = functools.reduce(jnp.logical_and, [i == 0 for i in ids])
        last = functools.reduce(jnp.logical_and, [i == g - 1 for i, g in zip(ids, grid)])

        @pl.when(first)
        def _():
            exch.start(cins, couts, sems)

        body(*ins, *outs, *scr)

        @pl.when(last)
        def _():
            exch.wait(cins, couts, sems)

    res = pl.pallas_call(
        full_body, name=name, grid=grid,
        in_specs=list(in_specs) + exch.specs, out_specs=list(out_specs) + exch.specs,
        out_shape=list(out_shape) + exch.out_shapes,
        scratch_shapes=list(scratch_shapes) + exch.sems,
        compiler_params=_params(("arbitrary",) * len(grid)),
    )(*args, *exch.arrays)
    return res[:n_out], res[n_out:]


def _rmsnorm_fwd(x, g, name):
    T, D = x.shape
    tr = min(256, T)

    def body(x_ref, g_ref, h_ref):
        xv = x_ref[...]
        r = lax.rsqrt(jnp.mean(xv * xv, axis=-1, keepdims=True) + EPS)
        h_ref[...] = (xv * r * g_ref[...]).astype(BF16)

    return pl.pallas_call(
        body, name=name, grid=(T // tr,),
        in_specs=[pl.BlockSpec((tr, D), lambda i: (i, 0)), pl.BlockSpec((1, D), lambda i: (0, 0))],
        out_specs=pl.BlockSpec((tr, D), lambda i: (i, 0)),
        out_shape=jax.ShapeDtypeStruct((T, D), BF16),
        compiler_params=_params(("parallel",)),
    )(x, g)


def _rmsnorm_bwd(x, g, dh, dres, name):
    T, D = x.shape
    tr = min(256, T)

    def body(x_ref, g_ref, dh_ref, dres_ref, dx_ref, dg_ref):
        i = pl.program_id(0)
        xv = x_ref[...]
        r = lax.rsqrt(jnp.mean(xv * xv, axis=-1, keepdims=True) + EPS)
        xh = xv * r
        dhv = dh_ref[...]
        dxh = dhv * g_ref[...]
        dx_ref[...] = dres_ref[...] + r * (dxh - xh * jnp.mean(dxh * xh, axis=-1, keepdims=True))

        @pl.when(i == 0)
        def _():
            dg_ref[...] = jnp.zeros_like(dg_ref)

        dg_ref[...] += jnp.sum(dhv * xh, axis=0, keepdims=True)

    row = pl.BlockSpec((tr, D), lambda i: (i, 0))
    vec = pl.BlockSpec((1, D), lambda i: (0, 0))
    return pl.pallas_call(
        body, name=name, grid=(T // tr,),
        in_specs=[row, vec, row, row], out_specs=[row, vec],
        out_shape=[jax.ShapeDtypeStruct((T, D), F32), jax.ShapeDtypeStruct((1, D), F32)],
        compiler_params=_params(("arbitrary",)),
    )(x, g, dh, dres)


def _loss_head(x, g, target):
    T, D = x.shape
    tr = min(256, T)

    def body(x_ref, g_ref, t_ref, loss_ref, dx_ref, dg_ref):
        i = pl.program_id(0)
        xv = x_ref[...]
        gv = g_ref[...]
        r = lax.rsqrt(jnp.mean(xv * xv, axis=-1, keepdims=True) + EPS)
        xh = xv * r
        diff = xh * gv - t_ref[...]
        dy = diff * (1.0 / D)
        dxh = dy * gv
        dx_ref[...] = r * (dxh - xh * jnp.mean(dxh * xh, axis=-1, keepdims=True))

        @pl.when(i == 0)
        def _():
            dg_ref[...] = jnp.zeros_like(dg_ref)
            loss_ref[...] = jnp.zeros_like(loss_ref)

        dg_ref[...] += jnp.sum(dy * xh, axis=0, keepdims=True)
        part = jnp.sum(jnp.sum(diff * diff, axis=1, keepdims=True), axis=0, keepdims=True)
        loss_ref[...] += (0.5 / D) * part

    row = pl.BlockSpec((tr, D), lambda i: (i, 0))
    vec = pl.BlockSpec((1, D), lambda i: (0, 0))
    return pl.pallas_call(
        body, name="loss_head", grid=(T // tr,),
        in_specs=[row, vec, row],
        out_specs=[pl.BlockSpec((1, 1), lambda i: (0, 0)), row, vec],
        out_shape=[jax.ShapeDtypeStruct((1, 1), F32), jax.ShapeDtypeStruct((T, D), F32),
                   jax.ShapeDtypeStruct((1, D), F32)],
        compiler_params=_params(("arbitrary",)),
    )(x, g, target)


ELEMS_PER_STEP = 1 << 20


def _row_tile(R, per_row):
    best = None
    for tr in range(8, R + 1, 8):
        if R % tr == 0 and tr * per_row <= ELEMS_PER_STEP:
            best = tr
    return best if best is not None else R


def _adamw(parts, w, m, v, name):
    P, R, C = parts.shape
    tr = _row_tile(R, P * C)

    def body(p_ref, w_ref, m_ref, v_ref, g_out, d_out, m_out, v_out):
        g = p_ref[0].astype(F32)
        for p in range(1, P):
            g = g + p_ref[p].astype(F32)
        wv = w_ref[...]
        mn = ADAM_B1 * m_ref[...] + (1.0 - ADAM_B1) * g
        vn = ADAM_B2 * v_ref[...] + (1.0 - ADAM_B2) * (g * g)
        m_hat = mn / (1.0 - ADAM_B1 ** ADAM_STEP)
        v_hat = vn / (1.0 - ADAM_B2 ** ADAM_STEP)
        g_out[...] = g
        d_out[...] = -ADAM_LR * (m_hat / (jnp.sqrt(v_hat) + ADAM_EPS) + ADAM_WD * wv)
        m_out[...] = mn
        v_out[...] = vn

    row = pl.BlockSpec((tr, C), lambda i: (i, 0))
    return pl.pallas_call(
        body, name=name, grid=(R // tr,),
        in_specs=[pl.BlockSpec((P, tr, C), lambda i: (0, i, 0)), row, row, row],
        out_specs=[row, row, row, row],
        out_shape=[jax.ShapeDtypeStruct((R, C), F32)] * 4,
        compiler_params=_params(("parallel",)),
    )(parts, w, m, v)


def _sum_parts(parts, name):
    P, R, C = parts.shape
    tr = _row_tile(R, P * C)

    def body(p_ref, o_ref):
        g = p_ref[0]
        for p in range(1, P):
            g = g + p_ref[p]
        o_ref[...] = g

    return pl.pallas_call(
        body, name=name, grid=(R // tr,),
        in_specs=[pl.BlockSpec((P, tr, C), lambda i: (0, i, 0))],
        out_specs=pl.BlockSpec((tr, C), lambda i: (i, 0)),
        out_shape=jax.ShapeDtypeStruct((R, C), F32),
        compiler_params=_params(("parallel",)),
    )(parts)


def _lane_head(Dh):
    assert Dh & (Dh - 1) == 0 and Dh <= LANES
    return lax.shift_right_logical(lax.broadcasted_iota(jnp.int32, (1, LANES), 1), Dh.bit_length() - 1)


def _stack_heads(x, lane_head, hpb):
    return jnp.concatenate([jnp.where(lane_head == h, x, 0.0) for h in range(hpb)], axis=0)


def _unstack_heads(acc, lane_head, hpb):
    TQ = acc.shape[0] // hpb
    out = acc[0:TQ]
    for h in range(1, hpb):
        out = jnp.where(lane_head == h, acc[h * TQ:(h + 1) * TQ], out)
    return out


def _key_tile(S):
    return ATT_TK if S % ATT_TK == 0 else BLK


def _query_tile(S):
    return ATT_TQ if S % ATT_TQ == 0 else BLK


def _lane_groups(P):
    return ATT_GP if P % ATT_GP == 0 else 1


def _lanes(u):
    return slice(u * LANES, (u + 1) * LANES)


def _causal_iotas(RS, TK, TQ):
    assert TQ & (TQ - 1) == 0 and (TK % TQ == 0 or TQ % TK == 0)
    trow = jnp.bitwise_and(lax.broadcasted_iota(jnp.int32, (RS, TK), 0), TQ - 1)
    col = lax.broadcasted_iota(jnp.int32, (RS, TK), 1)
    return trow, col


def _tri(TK, op):
    r = lax.broadcasted_iota(jnp.int32, (TK, TK), 0)
    c = lax.broadcasted_iota(jnp.int32, (TK, TK), 1)
    return op(r, c).astype(BF16)


def _logsig_parts(z):
    lb = jnp.minimum(z, 0.0) - jnp.log(1.0 + jnp.exp(-jnp.abs(z)))
    return lb, lb - z


def _sb_fwd(proj3, W, heads, exch):
    B, S, _ = proj3.shape
    Dh = W // heads
    hpb = LANES // Dh
    P, TQ = W // LANES, _query_tile(S)
    NQ = S // TQ
    scale = 1.0 / math.sqrt(Dh)

    TK = _key_tile(S)
    RS = hpb * TQ
    NM = max(1, TQ // TK)
    GP = _lane_groups(P)
    PG = P // GP

    def body(q_ref, k_ref, v_ref, g_ref, o_ref, y_ref):
        i = pl.program_id(2)
        lane_head = _lane_head(Dh)
        trow, col = _causal_iotas(RS, TK, TQ)
        msuf = _tri(TK, lambda r, c: r > c)
        qs = [(_stack_heads(q_ref[:, _lanes(u)], lane_head, hpb) * scale).astype(BF16) for u in range(GP)]
        nt = (i * TQ + TQ - 2) // TK + 1

        def tile(jt, carry, masked):
            off = pl.multiple_of(jt * TK, TK)
            if masked:
                msk = col + (jt * TK - i * TQ) < trow
            out = []
            for u, (rem, acc) in enumerate(carry):
                kj = k_ref[pl.ds(off, TK), _lanes(u)].astype(BF16)
                vj = v_ref[pl.ds(off, TK), _lanes(u)].astype(BF16)
                lb, lr = _logsig_parts(_dot(qs[u], kj, _NT))
                if masked:
                    lr = jnp.where(msk, lr, 0.0)
                w = jnp.exp(lb + _split_dot(lr, msuf) + rem)
                if masked:
                    w = jnp.where(msk, w, 0.0)
                out.append((rem + jnp.sum(lr, axis=1, keepdims=True), acc + _dot(w.astype(BF16), vj)))
            return tuple(out)

        zero = (jnp.zeros((RS, 1), F32), jnp.zeros((RS, LANES), F32))
        carry = (zero,) * GP
        for m in range(NM):
            carry = tile(nt - 1 - m, carry, True)
        carry = lax.fori_loop(NM, nt, lambda jj, c: tile(nt - 1 - jj, c, False), carry)
        for u in range(GP):
            o = _unstack_heads(carry[u][1], lane_head, hpb)
            o_ref[:, _lanes(u)] = o
            y_ref[:, _lanes(u)] = (o * _silu(g_ref[:, _lanes(u)])).astype(BF16)

    LW = GP * LANES
    blk = lambda sec: pl.BlockSpec((None, TQ, LW), lambda b, p, i: (b, i, sec * PG + p))
    full = lambda sec: pl.BlockSpec((None, S, LW), lambda b, p, i: (b, 0, sec * PG + p))
    out = pl.BlockSpec((None, TQ, LW), lambda b, p, i: (b, i, p))
    return _call_hosting(
        body, exch, "sb_fwd", (B, PG, NQ), [blk(0), full(1), full(2), blk(3)], [out, out],
        [jax.ShapeDtypeStruct((B, S, W), F32), jax.ShapeDtypeStruct((B, S, W), BF16)], [],
        (proj3, proj3, proj3, proj3))


def _sb_bwd(proj3, o, dy, W, heads, exch):
    B, S, _ = proj3.shape
    Dh = W // heads
    hpb = LANES // Dh
    P, TQ = W // LANES, _query_tile(S)
    NQ = S // TQ
    scale = 1.0 / math.sqrt(Dh)

    TK = _key_tile(S)
    RS = hpb * TQ
    NM = max(1, TQ // TK)

    def body(q_ref, k_ref, v_ref, g_ref, o_ref, dy_ref, dp_ref, dk_ref, dv_ref, u_ref, sig_ref, es_ref):
        i = pl.program_id(2)
        rows = pl.ds(pl.multiple_of(i * TQ, TQ), TQ)

        @pl.when(i == 0)
        def _():
            dk_ref[...] = jnp.zeros_like(dk_ref)
            dv_ref[...] = jnp.zeros_like(dv_ref)

        lane_head = _lane_head(Dh)
        trow, col = _causal_iotas(RS, TK, TQ)
        msuf = _tri(TK, lambda r, c: r > c)
        mpre = _tri(TK, lambda r, c: r < c)
        g = g_ref[...]
        dyv = dy_ref[...].astype(F32)
        dp_ref[3, rows, :] = (dyv * o_ref[...] * _dsilu(g)).astype(dp_ref.dtype)
        qs = (_stack_heads(q_ref[...], lane_head, hpb) * scale).astype(BF16)
        dos = _stack_heads(dyv * _silu(g), lane_head, hpb).astype(BF16)
        nt = (i * TQ + TQ - 2) // TK + 1

        def weights(jt, rem, masked):
            off = pl.multiple_of(jt * TK, TK)
            kj = k_ref[pl.ds(off, TK), :].astype(BF16)
            vj = v_ref[pl.ds(off, TK), :].astype(BF16)
            lb, lr = _logsig_parts(_dot(qs, kj, _NT))
            if masked:
                msk = col + (jt * TK - i * TQ) < trow
                lr = jnp.where(msk, lr, 0.0)
            w = jnp.exp(lb + _split_dot(lr, msuf) + rem)
            if masked:
                w = jnp.where(msk, w, 0.0)
            e = w * _dot(dos, vj, _NT)
            sig = jnp.exp(lb)
            u = e * (1.0 - sig) - _split_dot(e, mpre) * sig
            if masked:
                u = jnp.where(msk, u, 0.0)
                sig = jnp.where(msk, sig, 0.0)
            u_ref[jt] = u
            sig_ref[jt] = sig
            es_ref[jt] = jnp.sum(e, axis=1, keepdims=True)
            dv_ref[pl.ds(off, TK), :] += _dot(w.astype(BF16), dos, _TN)
            return rem + jnp.sum(lr, axis=1, keepdims=True)

        rem = jnp.zeros((RS, 1), F32)
        for m in range(NM):
            rem = weights(nt - 1 - m, rem, True)
        lax.fori_loop(NM, nt, lambda jj, r: weights(nt - 1 - jj, r, False), rem)

        def grads(jt, carry):
            pre, acc = carry
            off = pl.multiple_of(jt * TK, TK)
            kj = k_ref[pl.ds(off, TK), :].astype(BF16)
            dz = (u_ref[jt] - pre * sig_ref[jt]).astype(BF16)
            dk_ref[pl.ds(off, TK), :] += _dot(dz, qs, _TN)
            return pre + es_ref[jt], acc + _dot(dz, kj)

        _, acc = lax.fori_loop(0, nt, grads, (jnp.zeros((RS, 1), F32), jnp.zeros((RS, LANES), F32)))
        dp_ref[0, rows, :] = (_unstack_heads(acc, lane_head, hpb) * scale).astype(dp_ref.dtype)

        @pl.when(i == NQ - 1)
        def _():
            dp_ref[1] = dk_ref[...].astype(dp_ref.dtype)
            dp_ref[2] = dv_ref[...].astype(dp_ref.dtype)

    blk = lambda sec: pl.BlockSpec((None, TQ, LANES), lambda b, p, i: (b, i, sec * P + p))
    full = lambda sec: pl.BlockSpec((None, S, LANES), lambda b, p, i: (b, 0, sec * P + p))
    one = pl.BlockSpec((None, TQ, LANES), lambda b, p, i: (b, i, p))
    (dproj,), moved = _call_hosting(
        body, exch, "sb_bwd", (B, P, NQ), [blk(0), full(1), full(2), blk(3), one, one],
        [pl.BlockSpec((None, 4, S, LANES), lambda b, p, i: (b, 0, 0, p))],
        [jax.ShapeDtypeStruct((B, 4, S, W), BF16)],
        [pltpu.VMEM((S, LANES), F32), pltpu.VMEM((S, LANES), F32),
         pltpu.VMEM((S // TK, RS, TK), F32), pltpu.VMEM((S // TK, RS, TK), F32), pltpu.VMEM((S // TK, RS, 1), F32)],
        (proj3, proj3, proj3, proj3, o, dy))
    return dproj, moved


def _fox_gate_fwd(f_t, b_f):
    B, H, S = f_t.shape

    def body(f_ref, b_ref, c_ref):
        row = lax.broadcasted_iota(jnp.int32, (BLK, BLK), 0)
        col = lax.broadcasted_iota(jnp.int32, (BLK, BLK), 1)
        mpre = (row <= col).astype(BF16)
        carry = jnp.zeros((H, 1), F32)
        for n in range(S // BLK):
            sl = pl.ds(n * BLK, BLK)
            lf, _ = _logsig_parts(f_ref[:, sl] + b_ref[...])
            c_ref[:, sl] = _split3_dot(lf, mpre) + carry
            carry = carry + jnp.sum(lf, axis=1, keepdims=True)

    spec = pl.BlockSpec((None, H, S), lambda b: (b, 0, 0))
    return pl.pallas_call(
        body, name="fox_gate_fwd", grid=(B,),
        in_specs=[spec, pl.BlockSpec((H, 1), lambda b: (0, 0))], out_specs=spec,
        out_shape=jax.ShapeDtypeStruct((B, H, S), F32),
        compiler_params=_params(("parallel",)),
    )(f_t, b_f)


def _fox_gate_bwd(dcum_t, f_t, b_f):
    B, H, S = f_t.shape

    def body(d_ref, f_ref, b_ref, df_ref, db_ref):
        b = pl.program_id(0)

        @pl.when(b == 0)
        def _():
            db_ref[...] = jnp.zeros_like(db_ref)

        row = lax.broadcasted_iota(jnp.int32, (BLK, BLK), 0)
        col = lax.broadcasted_iota(jnp.int32, (BLK, BLK), 1)
        msuf = (row >= col).astype(BF16)
        carry = jnp.zeros((H, 1), F32)
        dbacc = jnp.zeros((H, 1), F32)
        for n in reversed(range(S // BLK)):
            sl = pl.ds(n * BLK, BLK)
            dv = d_ref[:, sl]
            dlf = _split3_dot(dv, msuf) + carry
            carry = carry + jnp.sum(dv, axis=1, keepdims=True)
            df = dlf * _sigmoid(-(f_ref[:, sl] + b_ref[...]))
            df_ref[:, sl] = df
            dbacc = dbacc + jnp.sum(df, axis=1, keepdims=True)
        db_ref[...] += dbacc

    spec = pl.BlockSpec((None, H, S), lambda b: (b, 0, 0))
    vec = pl.BlockSpec((H, 1), lambda b: (0, 0))
    return pl.pallas_call(
        body, name="fox_gate_bwd", grid=(B,),
        in_specs=[spec, spec, vec], out_specs=[spec, vec],
        out_shape=[jax.ShapeDtypeStruct((B, H, S), F32), jax.ShapeDtypeStruct((H, 1), F32)],
        compiler_params=_params(("arbitrary",)),
    )(dcum_t, f_t, b_f)


def _pick_col(block, idx, lane_iota):
    return jnp.sum(jnp.where(lane_iota == idx, block, 0.0), axis=1, keepdims=True)


def _pick_row(block, idx, sub_iota):
    return jnp.sum(jnp.where(sub_iota == idx, block, 0.0), axis=0, keepdims=True)


def _fox_fwd(proj3, cum_t, W, heads):
    B, S, _ = proj3.shape
    H = heads
    Dh = W // heads
    hpb = LANES // Dh
    P, TQ = W // LANES, _query_tile(S)
    NQ = S // TQ
    scale = 1.0 / math.sqrt(Dh)

    TK = _key_tile(S)
    RS = hpb * TQ
    NM = max(1, TQ // TK)

    def body(q_ref, k_ref, v_ref, g_ref, ct_ref, o_ref, y_ref, lse_ref):
        p = pl.program_id(1)
        i = pl.program_id(2)
        lane_head = _lane_head(Dh)
        trow, col = _causal_iotas(RS, TK, TQ)
        sub_h = lax.broadcasted_iota(jnp.int32, (H, 1), 0)
        qs = (_stack_heads(q_ref[...], lane_head, hpb) * scale).astype(BF16)
        nt = (i * TQ + TQ - 1) // TK + 1

        def tile(jt, carry, masked):
            mx, l, acc = carry
            off = pl.multiple_of(jt * TK, TK)
            kj = k_ref[pl.ds(off, TK), :].astype(BF16)
            vj = v_ref[pl.ds(off, TK), :].astype(BF16)
            ctb = ct_ref[:, pl.ds(off, TK)]
            z = _dot(qs, kj, _NT)
            s = jnp.concatenate([z[h * TQ:(h + 1) * TQ] - _pick_row(ctb, p * hpb + h, sub_h) for h in range(hpb)],
                                axis=0)
            if masked:
                s = jnp.where(col + (jt * TK - i * TQ) <= trow, s, NEG_BIG)
            mx2 = jnp.maximum(mx, jnp.max(s, axis=1, keepdims=True))
            pe = jnp.exp(s - mx2)
            alpha = jnp.exp(mx - mx2)
            return (mx2, alpha * l + jnp.sum(pe, axis=1, keepdims=True), alpha * acc + _dot(pe.astype(BF16), vj))

        carry = lax.fori_loop(
            0, nt - NM, lambda jt, c: tile(jt, c, False),
            (jnp.full((RS, 1), NEG_BIG, F32), jnp.zeros((RS, 1), F32), jnp.zeros((RS, LANES), F32)))
        for m in reversed(range(NM)):
            carry = tile(nt - 1 - m, carry, True)
        mx, l, acc = carry
        o = _unstack_heads(acc / l, lane_head, hpb)
        o_ref[...] = o
        lse_ref[...] = _unstack_heads(jnp.broadcast_to(mx + jnp.log(l), (RS, LANES)), lane_head, hpb)
        y_ref[...] = (o * _silu(g_ref[...])).astype(BF16)

    blk = lambda sec: pl.BlockSpec((None, TQ, LANES), lambda b, p, i: (b, i, sec * P + p))
    full = lambda sec: pl.BlockSpec((None, S, LANES), lambda b, p, i: (b, 0, sec * P + p))
    out = pl.BlockSpec((None, TQ, LANES), lambda b, p, i: (b, i, p))
    return pl.pallas_call(
        body, name="fox_fwd", grid=(B, P, NQ),
        in_specs=[blk(0), full(1), full(2), blk(3),
                  pl.BlockSpec((None, H, S), lambda b, p, i: (b, 0, 0))],
        out_specs=[out, out, out],
        out_shape=[jax.ShapeDtypeStruct((B, S, W), F32), jax.ShapeDtypeStruct((B, S, W), BF16),
                   jax.ShapeDtypeStruct((B, S, W), F32)],
        compiler_params=_params(("parallel", "parallel", "arbitrary")),
    )(proj3, proj3, proj3, proj3, cum_t)


def _fox_bwd(proj3, cum_t, o, lse, dy, W, heads):
    B, S, _ = proj3.shape
    H = heads
    Dh = W // heads
    hpb = LANES // Dh
    P, TQ = W // LANES, _query_tile(S)
    NQ = S // TQ
    scale = 1.0 / math.sqrt(Dh)

    TK = _key_tile(S)
    RS = hpb * TQ
    NM = max(1, TQ // TK)

    def body(q_ref, k_ref, v_ref, g_ref, ct_ref, o_ref, lse_ref, dy_ref,
             dpj_ref, dc_ref, dk_ref, dv_ref, p_scr, dp_scr):
        p = pl.program_id(1)
        i = pl.program_id(2)
        rows = pl.ds(pl.multiple_of(i * TQ, TQ), TQ)

        @pl.when(i == 0)
        def _():
            dk_ref[...] = jnp.zeros_like(dk_ref)
            dv_ref[...] = jnp.zeros_like(dv_ref)
            dc_ref[...] = jnp.zeros_like(dc_ref)

        lane_head = _lane_head(Dh)
        trow, col = _causal_iotas(RS, TK, TQ)
        sub_h = lax.broadcasted_iota(jnp.int32, (H, 1), 0)
        lane = lax.broadcasted_iota(jnp.int32, (1, LANES), 1)
        g = g_ref[...]
        lsev = lse_ref[...]
        dyv = dy_ref[...].astype(F32)
        dpj_ref[3, rows, :] = (dyv * o_ref[...] * _dsilu(g)).astype(dpj_ref.dtype)
        qs = (_stack_heads(q_ref[...], lane_head, hpb) * scale).astype(BF16)
        dos = _stack_heads(dyv * _silu(g), lane_head, hpb).astype(BF16)
        neg_lse = -jnp.concatenate([_pick_col(lsev, h * Dh, lane) for h in range(hpb)], axis=0)
        nt = (i * TQ + TQ - 1) // TK + 1

        def probs(jt, dsum, masked):
            off = pl.multiple_of(jt * TK, TK)
            kj = k_ref[pl.ds(off, TK), :].astype(BF16)
            vj = v_ref[pl.ds(off, TK), :].astype(BF16)
            ctb = ct_ref[:, pl.ds(off, TK)]
            z = _dot(qs, kj, _NT) + neg_lse
            s = jnp.concatenate([z[h * TQ:(h + 1) * TQ] - _pick_row(ctb, p * hpb + h, sub_h) for h in range(hpb)],
                                axis=0)
            pr = jnp.exp(s)
            if masked:
                pr = jnp.where(col + (jt * TK - i * TQ) <= trow, pr, 0.0)
            dp = _dot(dos, vj, _NT)
            p_scr[jt] = pr
            dp_scr[jt] = dp
            dv_ref[pl.ds(off, TK), :] += _dot(pr.astype(BF16), dos, _TN)
            return dsum + jnp.sum(pr * dp, axis=1, keepdims=True)

        dsum = lax.fori_loop(0, nt - NM, lambda jt, d: probs(jt, d, False), jnp.zeros((RS, 1), F32))
        for m in reversed(range(NM)):
            dsum = probs(nt - 1 - m, dsum, True)

        def grads(jt, acc):
            off = pl.multiple_of(jt * TK, TK)
            kj = k_ref[pl.ds(off, TK), :].astype(BF16)
            ds = p_scr[jt] * (dp_scr[jt] - dsum)
            for h in range(hpb):
                dc_ref[h:h + 1, pl.ds(off, TK)] -= jnp.sum(ds[h * TQ:(h + 1) * TQ], axis=0, keepdims=True)
            dsb = ds.astype(BF16)
            dk_ref[pl.ds(off, TK), :] += _dot(dsb, qs, _TN)
            return acc + _dot(dsb, kj)

        acc = lax.fori_loop(0, nt, grads, jnp.zeros((RS, LANES), F32))
        dpj_ref[0, rows, :] = (_unstack_heads(acc, lane_head, hpb) * scale).astype(dpj_ref.dtype)

        @pl.when(i == NQ - 1)
        def _():
            dpj_ref[1] = dk_ref[...].astype(dpj_ref.dtype)
            dpj_ref[2] = dv_ref[...].astype(dpj_ref.dtype)

    blk = lambda sec: pl.BlockSpec((None, TQ, LANES), lambda b, p, i: (b, i, sec * P + p))
    full = lambda sec: pl.BlockSpec((None, S, LANES), lambda b, p, i: (b, 0, sec * P + p))
    one = pl.BlockSpec((None, TQ, LANES), lambda b, p, i: (b, i, p))
    return pl.pallas_call(
        body, name="fox_bwd", grid=(B, P, NQ),
        in_specs=[blk(0), full(1), full(2), blk(3),
                  pl.BlockSpec((None, H, S), lambda b, p, i: (b, 0, 0)),
                  one, one, one],
        out_specs=[pl.BlockSpec((None, 4, S, LANES), lambda b, p, i: (b, 0, 0, p)),
                   pl.BlockSpec((None, None, hpb, S), lambda b, p, i: (b, p, 0, 0))],
        out_shape=[jax.ShapeDtypeStruct((B, 4, S, W), BF16), jax.ShapeDtypeStruct((B, P, hpb, S), F32)],
        scratch_shapes=[pltpu.VMEM((S, LANES), F32), pltpu.VMEM((S, LANES), F32),
                        pltpu.VMEM((S // TK, RS, TK), F32), pltpu.VMEM((S // TK, RS, TK), F32)],
        compiler_params=_params(("parallel", "parallel", "arbitrary")),
    )(proj3, proj3, proj3, proj3, cum_t, o, lse, dy)


def _layernorm_rows(v, gamma, beta):
    mu = jnp.mean(v, axis=-1, keepdims=True)
    xc = v - mu
    rstd = lax.rsqrt(jnp.mean(xc * xc, axis=-1, keepdims=True) + EPS)
    xh = xc * rstd
    return xh, rstd, xh * gamma + beta


def _layernorm_rows_bwd(dout, xh, rstd, gamma):
    dxh = dout * gamma
    return rstd * (dxh - jnp.mean(dxh, axis=-1, keepdims=True) - xh * jnp.mean(dxh * xh, axis=-1, keepdims=True))


def _gmlp_fwd(proj, wm, bs_t, ln_g, ln_b, W):
    T = proj.shape[0]
    G = wm.shape[0]
    cg = W // G
    assert cg == LANES

    def body(p_ref, wm_ref, bs_ref, lg_ref, lb_ref, y_ref, vn_ref):
        lane = lax.broadcasted_iota(jnp.int32, (1, LANES), 1)
        _, _, vn = _layernorm_rows(_gelu(p_ref[:, W:2 * W]), lg_ref[...], lb_ref[...])
        vn_ref[...] = vn.astype(BF16)
        bs = bs_ref[...]
        for g in range(G):
            sl = pl.ds(g * cg, cg)
            s = _dot(wm_ref[g], vn_ref[:, sl]) + _pick_col(bs, g, lane)
            gate = p_ref[:, pl.ds(2 * W + g * cg, cg)]
            y_ref[:, sl] = (_gelu(p_ref[:, sl]) * s * _silu(gate)).astype(BF16)

    vec = pl.BlockSpec((1, W), lambda r: (0, 0))
    return pl.pallas_call(
        body, name="gmlp_fwd", grid=(T // BLK,),
        in_specs=[pl.BlockSpec((BLK, 3 * W), lambda r: (r, 0)),
                  pl.BlockSpec((G, BLK, BLK), lambda r: (0, 0, 0)),
                  pl.BlockSpec((BLK, LANES), lambda r: (0, 0)), vec, vec],
        out_specs=pl.BlockSpec((BLK, W), lambda r: (r, 0)),
        out_shape=jax.ShapeDtypeStruct((T, W), BF16),
        scratch_shapes=[pltpu.VMEM((BLK, W), BF16)],
        compiler_params=_params(("parallel",)),
    )(proj, wm, bs_t, ln_g, ln_b)


def _gmlp_bwd(proj, dy, wm, bs_t, ln_g, ln_b, W):
    T = proj.shape[0]
    G = wm.shape[0]
    cg = W // G

    def body(p_ref, dy_ref, wm_ref, bs_ref, lg_ref, lb_ref,
             dp_ref, dwm_ref, dbs_ref, dlg_ref, dlb_ref, vn_ref, dvn_ref):
        r = pl.program_id(0)

        @pl.when(r == 0)
        def _():
            dwm_ref[...] = jnp.zeros_like(dwm_ref)
            dbs_ref[...] = jnp.zeros_like(dbs_ref)
            dlg_ref[...] = jnp.zeros_like(dlg_ref)
            dlb_ref[...] = jnp.zeros_like(dlb_ref)

        lane = lax.broadcasted_iota(jnp.int32, (1, LANES), 1)
        vpre = p_ref[:, W:2 * W]
        gamma = lg_ref[...]
        xh, rstd, vn = _layernorm_rows(_gelu(vpre), gamma, lb_ref[...])
        vn_ref[...] = vn.astype(BF16)
        bs = bs_ref[...]
        dbs = jnp.zeros((BLK, LANES), F32)
        for g in range(G):
            sl = pl.ds(g * cg, cg)
            gsl = pl.ds(2 * W + g * cg, cg)
            vng = vn_ref[:, sl]
            s = _dot(wm_ref[g], vng) + _pick_col(bs, g, lane)
            upre = p_ref[:, sl]
            u = _gelu(upre)
            gate = p_ref[:, gsl]
            dyv = dy_ref[:, sl].astype(F32)
            dp_ref[:, gsl] = (dyv * u * s * _dsilu(gate)).astype(dp_ref.dtype)
            do = dyv * _silu(gate)
            dp_ref[:, sl] = (do * s * _dgelu(upre)).astype(dp_ref.dtype)
            ds = do * u
            dbs = dbs + jnp.where(lane == g, jnp.sum(ds, axis=1, keepdims=True), 0.0)
            dsb = ds.astype(BF16)
            dwm_ref[g] += _dot(dsb, vng, _NT)
            dvn_ref[:, sl] = _dot(wm_ref[g], dsb, _TN)
        dbs_ref[...] += dbs
        dvn = dvn_ref[...]
        dlg_ref[...] += jnp.sum(dvn * xh, axis=0, keepdims=True)
        dlb_ref[...] += jnp.sum(dvn, axis=0, keepdims=True)
        dv = _layernorm_rows_bwd(dvn, xh, rstd, gamma)
        dp_ref[:, W:2 * W] = (dv * _dgelu(vpre)).astype(dp_ref.dtype)

    vec = pl.BlockSpec((1, W), lambda r: (0, 0))
    return pl.pallas_call(
        body, name="gmlp_bwd", grid=(T // BLK,),
        in_specs=[pl.BlockSpec((BLK, 3 * W), lambda r: (r, 0)),
                  pl.BlockSpec((BLK, W), lambda r: (r, 0)),
                  pl.BlockSpec((G, BLK, BLK), lambda r: (0, 0, 0)),
                  pl.BlockSpec((BLK, LANES), lambda r: (0, 0)), vec, vec],
        out_specs=[pl.BlockSpec((BLK, 3 * W), lambda r: (r, 0)),
                   pl.BlockSpec((G, BLK, BLK), lambda r: (0, 0, 0)),
                   pl.BlockSpec((BLK, LANES), lambda r: (0, 0)), vec, vec],
        out_shape=[jax.ShapeDtypeStruct((T, 3 * W), BF16), jax.ShapeDtypeStruct((G, BLK, BLK), F32),
                   jax.ShapeDtypeStruct((BLK, LANES), F32),
                   jax.ShapeDtypeStruct((1, W), F32), jax.ShapeDtypeStruct((1, W), F32)],
        scratch_shapes=[pltpu.VMEM((BLK, W), BF16), pltpu.VMEM((BLK, W), F32)],
        compiler_params=_params(("arbitrary",)),
    )(proj, dy, wm, bs_t, ln_g, ln_b)


SUBLANES = 8
SHIFT_ROWS = CONV_HALO + BLK - SUBLANES


def _shift_rows(ext_ref, sh_ref, off):
    for r in range(1, SUBLANES):
        sh_ref[r - 1] = ext_ref[pl.ds(r, SHIFT_ROWS), pl.ds(off, LANES)]


def _rows_from(ext_ref, sh_ref, off, start):
    r = start % SUBLANES
    if r == 0:
        return ext_ref[pl.ds(start, BLK), pl.ds(off, LANES)]
    return sh_ref[r - 1, pl.ds(start - r, BLK), :]


def _conv_taps(ext_ref, sh_ref, cw_ref, off, n_taps, first):
    acc = jnp.zeros((BLK, LANES), F32)
    for k in range(n_taps):
        acc = acc + cw_ref[k:k + 1, pl.ds(off, LANES)] * _rows_from(ext_ref, sh_ref, off, first + k)
    return acc


def _fill_glu_ext(ext_ref, halo_ref, cur_ref, W, first_block):
    y0h = halo_ref[:, :W] * _sigmoid(halo_ref[:, W:])
    ext_ref[0:CONV_HALO, :] = jnp.where(first_block, 0.0, y0h)
    ext_ref[CONV_HALO:CONV_HALO + BLK, :] = cur_ref[:, :W] * _sigmoid(cur_ref[:, W:])


def _conv_specs(S, W):
    per = BLK // CONV_HALO
    cur = pl.BlockSpec((None, BLK, 2 * W), lambda b, i: (b, i, 0))
    halo = pl.BlockSpec((None, CONV_HALO, 2 * W), lambda b, i: (b, jnp.maximum(i * per - 1, 0), 0))
    gate = pl.BlockSpec((None, BLK, W), lambda b, i: (b, i, 2))
    return cur, halo, gate


def _conv_fwd(proj3, cw, cb, ln_g, ln_b, W, exch):
    B, S, _ = proj3.shape
    K = cw.shape[0]
    first = CONV_HALO - (K - 1)
    assert first >= 0

    def body(cur_ref, halo_ref, g_ref, cw_ref, cb_ref, lg_ref, lb_ref, y_ref, ext_ref, y1_ref, sh_ref):
        i = pl.program_id(1)
        _fill_glu_ext(ext_ref, halo_ref, cur_ref, W, i == 0)

        def chan(c, _):
            off = pl.multiple_of(c * LANES, LANES)
            _shift_rows(ext_ref, sh_ref, off)
            y1_ref[:, pl.ds(off, LANES)] = (_conv_taps(ext_ref, sh_ref, cw_ref, off, K, first)
                                            + cb_ref[:, pl.ds(off, LANES)])
            return 0

        lax.fori_loop(0, W // LANES, chan, 0)
        _, _, ln = _layernorm_rows(y1_ref[...], lg_ref[...], lb_ref[...])
        y_ref[...] = (_silu(ln) * _silu(g_ref[...])).astype(BF16)

    cur, halo, gate = _conv_specs(S, W)
    vec = pl.BlockSpec((1, W), lambda b, i: (0, 0))
    (y,), moved = _call_hosting(
        body, exch, "conv_fwd", (B, S // BLK),
        [cur, halo, gate, pl.BlockSpec((K, W), lambda b, i: (0, 0)), vec, vec, vec],
        [pl.BlockSpec((None, BLK, W), lambda b, i: (b, i, 0))], [jax.ShapeDtypeStruct((B, S, W), BF16)],
        [pltpu.VMEM((CONV_HALO + BLK, W), F32), pltpu.VMEM((BLK, W), F32),
         pltpu.VMEM((SUBLANES - 1, SHIFT_ROWS, LANES), F32)],
        (proj3, proj3, proj3, cw, cb, ln_g, ln_b))
    return y, moved


def _conv_bwd1(proj3, dy, cw, cb, ln_g, ln_b, W, exch):
    B, S, _ = proj3.shape
    K = cw.shape[0]
    first = CONV_HALO - (K - 1)

    def body(cur_ref, halo_ref, g_ref, dy_ref, cw_ref, cb_ref, lg_ref, lb_ref,
             dy1_ref, dg_ref, dcw_ref, dcb_ref, dlg_ref, dlb_ref, ext_ref, y1_ref, sh_ref):
        b = pl.program_id(0)
        i = pl.program_id(1)

        @pl.when(jnp.logical_and(b == 0, i == 0))
        def _():
            dcw_ref[...] = jnp.zeros_like(dcw_ref)
            dcb_ref[...] = jnp.zeros_like(dcb_ref)
            dlg_ref[...] = jnp.zeros_like(dlg_ref)
            dlb_ref[...] = jnp.zeros_like(dlb_ref)

        _fill_glu_ext(ext_ref, halo_ref, cur_ref, W, i == 0)

        def chan(c, _):
            off = pl.multiple_of(c * LANES, LANES)
            _shift_rows(ext_ref, sh_ref.at[c], off)
            y1_ref[:, pl.ds(off, LANES)] = (_conv_taps(ext_ref, sh_ref.at[c], cw_ref, off, K, first)
                                            + cb_ref[:, pl.ds(off, LANES)])
            return 0

        lax.fori_loop(0, W // LANES, chan, 0)
        gamma = lg_ref[...]
        xh, rstd, ln = _layernorm_rows(y1_ref[...], gamma, lb_ref[...])
        g = g_ref[...]
        dyv = dy_ref[...].astype(F32)
        dg_ref[...] = (dyv * _silu(ln) * _dsilu(g)).astype(dg_ref.dtype)
        dln = dyv * _silu(g) * _dsilu(ln)
        dlg_ref[...] += jnp.sum(dln * xh, axis=0, keepdims=True)
        dlb_ref[...] += jnp.sum(dln, axis=0, keepdims=True)
        dy1 = _layernorm_rows_bwd(dln, xh, rstd, gamma)
        dy1_ref[...] = dy1
        dcb_ref[...] += jnp.sum(dy1, axis=0, keepdims=True)

        def chan_w(c, _):
            off = pl.multiple_of(c * LANES, LANES)
            d = dy1_ref[:, pl.ds(off, LANES)]
            for k in range(K):
                dcw_ref[k:k + 1, pl.ds(off, LANES)] += jnp.sum(
                    d * _rows_from(ext_ref, sh_ref.at[c], off, first + k), axis=0, keepdims=True)
            return 0

        lax.fori_loop(0, W // LANES, chan_w, 0)

    cur, halo, gate = _conv_specs(S, W)
    vec = pl.BlockSpec((1, W), lambda b, i: (0, 0))
    taps = pl.BlockSpec((K, W), lambda b, i: (0, 0))
    one = pl.BlockSpec((None, BLK, W), lambda b, i: (b, i, 0))
    return _call_hosting(
        body, exch, "conv_bwd1", (B, S // BLK), [cur, halo, gate, one, taps, vec, vec, vec],
        [one, one, taps, vec, vec, vec],
        [jax.ShapeDtypeStruct((B, S, W), F32), jax.ShapeDtypeStruct((B, S, W), BF16),
         jax.ShapeDtypeStruct((K, W), F32)] + [jax.ShapeDtypeStruct((1, W), F32)] * 3,
        [pltpu.VMEM((CONV_HALO + BLK, W), F32), pltpu.VMEM((BLK, W), F32),
         pltpu.VMEM((W // LANES, SUBLANES - 1, SHIFT_ROWS, LANES), F32)],
        (proj3, proj3, proj3, dy, cw, cb, ln_g, ln_b))


def _conv_bwd2(proj3, dy1, dgate, cw_rev, W):
    B, S, _ = proj3.shape
    K = cw_rev.shape[0]
    NQ = S // BLK
    per = BLK // CONV_HALO

    def body(cur_ref, d_ref, dnext_ref, dgate_ref, cw_ref, dp_ref, ext_ref, dy0_ref, sh_ref):
        i = pl.program_id(1)
        ext_ref[0:BLK, :] = d_ref[...]
        ext_ref[BLK:BLK + CONV_HALO, :] = jnp.where(i == NQ - 1, 0.0, dnext_ref[...])

        def chan(c, _):
            off = pl.multiple_of(c * LANES, LANES)
            _shift_rows(ext_ref, sh_ref, off)
            dy0_ref[:, pl.ds(off, LANES)] = _conv_taps(ext_ref, sh_ref, cw_ref, off, K, 0)
            return 0

        lax.fori_loop(0, W // LANES, chan, 0)
        a = cur_ref[:, :W]
        sg = _sigmoid(cur_ref[:, W:])
        dy0 = dy0_ref[...]
        dp_ref[:, 0:W] = (dy0 * sg).astype(dp_ref.dtype)
        dp_ref[:, W:2 * W] = (dy0 * a * sg * (1.0 - sg)).astype(dp_ref.dtype)
        dp_ref[:, 2 * W:3 * W] = dgate_ref[...]

    cur = pl.BlockSpec((None, BLK, 2 * W), lambda b, i: (b, i, 0))
    one = pl.BlockSpec((None, BLK, W), lambda b, i: (b, i, 0))
    nxt = pl.BlockSpec((None, CONV_HALO, W), lambda b, i: (b, jnp.minimum((i + 1) * per, S // CONV_HALO - 1), 0))
    return pl.pallas_call(
        body, name="conv_bwd2", grid=(B, NQ),
        in_specs=[cur, one, nxt, one, pl.BlockSpec((K, W), lambda b, i: (0, 0))],
        out_specs=pl.BlockSpec((None, BLK, 3 * W), lambda b, i: (b, i, 0)),
        out_shape=jax.ShapeDtypeStruct((B, S, 3 * W), BF16),
        scratch_shapes=[pltpu.VMEM((BLK + CONV_HALO, W), F32), pltpu.VMEM((BLK, W), F32),
                        pltpu.VMEM((SUBLANES - 1, SHIFT_ROWS, LANES), F32)],
        compiler_params=_params(("parallel", "parallel")),
    )(proj3, dy1, dy1, dgate, cw_rev)


def _pack(arrays):
    flat = jnp.concatenate([a.astype(F32).reshape(-1) for a in arrays])
    n = flat.shape[0]
    pad = (-n) % (8 * LANES)
    if pad:
        flat = jnp.concatenate([flat, jnp.zeros((pad,), F32)])
    return flat.reshape(-1, LANES)


def _unpack(packed, shapes, lead=()):
    flat = packed.reshape(lead + (-1,))
    out, off = [], 0
    for shp in shapes:
        n = math.prod(shp)
        out.append(flat[..., off:off + n].reshape(lead + tuple(shp)))
        off += n
    return out


def _cols_from_dev(g):
    g = jnp.moveaxis(g, 0, -2)
    return g.reshape(g.shape[:-2] + (g.shape[-2] * g.shape[-1],))


def _my_cols(full, me):
    n8 = full.shape[-1] // N_DEV
    return lax.dynamic_slice_in_dim(full, me * n8, n8, axis=full.ndim - 1)


def kernel(x, a_norm, a_w_in, a_w_out, b_norm, b_w_in, b_v_ln_g, b_v_ln_b, b_w_s, b_b_s, b_w_out, c_norm, c_w_in, c_conv_w, c_conv_b, c_ln_g, c_ln_b, c_w_out, d_norm, d_w_in, d_b_f, d_w_out, final_norm, loss_target, m_a_norm, m_a_w_in, m_a_w_out, m_b_norm, m_b_w_in, m_b_v_ln_g, m_b_v_ln_b, m_b_w_s, m_b_b_s, m_b_w_out, m_c_norm, m_c_w_in, m_c_conv_w, m_c_conv_b, m_c_ln_g, m_c_ln_b, m_c_w_out, m_d_norm, m_d_w_in, m_d_b_f, m_d_w_out, m_final_norm, v_a_norm, v_a_w_in, v_a_w_out, v_b_norm, v_b_w_in, v_b_v_ln_g, v_b_v_ln_b, v_b_w_s, v_b_b_s, v_b_w_out, v_c_norm, v_c_w_in, v_c_conv_w, v_c_conv_b, v_c_ln_g, v_c_ln_b, v_c_w_out, v_d_norm, v_d_w_in, v_d_b_f, v_d_w_out, v_final_norm):
    B, S, D = x.shape
    T = B * S
    xi, yi, ci = _me()
    me = 4 * xi + 2 * yi + ci

    G = b_w_s.shape[1]
    KC = c_conv_w.shape[1]
    H_D = d_b_f.shape[1]
    W_A = a_w_out.shape[1] * N_DEV
    W_B = b_w_out.shape[1] * N_DEV
    W_C = c_w_out.shape[1] * N_DEV
    W_D = d_w_out.shape[1] * N_DEV
    N_D = d_w_in.shape[2] * N_DEV
    N_D_PAD = -(-N_D // (3 * LANES)) * (3 * LANES)

    big_names = ["a_w_in", "a_w_out", "b_w_in", "b_w_out", "c_w_in", "c_w_out", "d_w_in", "d_w_out"]
    big_w = dict(a_w_in=a_w_in[0], a_w_out=a_w_out[0], b_w_in=b_w_in[0], b_w_out=b_w_out[0],
                 c_w_in=c_w_in[0], c_w_out=c_w_out[0], d_w_in=d_w_in[0], d_w_out=d_w_out[0])
    small_sharded = [b_norm, b_v_ln_g, b_v_ln_b, c_norm, c_conv_w, c_conv_b, c_ln_g, c_ln_b, d_norm]
    first_names, later_names, last_names = big_names[:1], big_names[1:6], big_names[6:]
    gathered = _Exchange([big_w[n].astype(BF16) for n in first_names] + [_pack(small_sharded)],
                         ["gather"] * (len(first_names) + 1)).run("gather_first")
    wg = dict(zip(first_names, gathered[:-1]))
    (b_norm_f, b_lg_f, b_lb_f, c_norm_f, c_cw_f, c_cb_f, c_lg_f, c_lb_f, d_norm_f) = [
        _cols_from_dev(t) for t in _unpack(gathered[-1], [s.shape for s in small_sharded], lead=(N_DEV,))]
    c_cw_f = c_cw_f[0]

    wm = jnp.tril(b_w_s[0]).astype(BF16)
    bs_t = jnp.pad(b_b_s[0].T, ((0, 0), (0, LANES - G)))

    x0 = x.reshape(T, D)
    h_a = _rmsnorm_fwd(x0, a_norm, "rms_a")
    proj_a = _mm_w_dev(h_a, wg["a_w_in"], "proj_a").reshape(B, S, 4 * W_A)
    (o_a, y_a), later = _sb_fwd(proj_a, W_A, SB_HEADS,
                                _Exchange([big_w[n].astype(BF16) for n in later_names], ["gather"] * len(later_names)))
    wg.update(zip(later_names, later))
    a_w_out_f = wg["a_w_out"].reshape(W_A, D)
    b_w_out_f = wg["b_w_out"].reshape(W_B, D)
    c_w_out_f = wg["c_w_out"].reshape(W_C, D)
    y_a = y_a.reshape(T, W_A)
    x1 = _mm(y_a, a_w_out_f, "nn", T, D, W_A, F32, "out_a", 512, D, W_A, res=x0)
    h_b = _rmsnorm_fwd(x1, b_norm_f, "rms_b")
    proj_b = _mm_w_dev(h_b, wg["b_w_in"], "proj_b")
    y_b = _gmlp_fwd(proj_b, wm, bs_t, b_lg_f, b_lb_f, W_B)
    x2 = _mm(y_b, b_w_out_f, "nn", T, D, W_B, F32, "out_b", 512, D, W_B, res=x1)
    h_c = _rmsnorm_fwd(x2, c_norm_f, "rms_c")
    proj_c = _mm_w_dev(h_c, wg["c_w_in"], "proj_c").reshape(B, S, 3 * W_C)
    y_c, last = _conv_fwd(proj_c, c_cw_f, c_cb_f, c_lg_f, c_lb_f, W_C,
                          _Exchange([big_w[n].astype(BF16) for n in last_names], ["gather"] * len(last_names)))
    wg.update(zip(last_names, last))
    d_w_out_f = wg["d_w_out"].reshape(W_D, D)
    d_w_in_f = jnp.pad(_cols_from_dev(wg["d_w_in"]), ((0, 0), (0, N_D_PAD - N_D)))
    y_c = y_c.reshape(T, W_C)
    x3 = _mm(y_c, c_w_out_f, "nn", T, D, W_C, F32, "out_c", 512, D, W_C, res=x2)
    h_d = _rmsnorm_fwd(x3, d_norm_f, "rms_d")
    proj_d = _mm(h_d, d_w_in_f, "nn", T, N_D_PAD, D, F32, "proj_d", 1024, 384, D).reshape(B, S, N_D_PAD)
    f_t = jnp.swapaxes(proj_d[:, :, 4 * W_D:4 * W_D + H_D], 1, 2)
    b_f_col = d_b_f.reshape(H_D, 1)
    cum_t = _fox_gate_fwd(f_t, b_f_col)
    o_d, y_d, lse_d = _fox_fwd(proj_d, cum_t, W_D, H_D)
    y_d = y_d.reshape(T, W_D)
    x4 = _mm(y_d, d_w_out_f, "nn", T, D, W_D, F32, "out_d", 512, D, W_D, res=x3)

    loss_part, dx, g_final = _loss_head(x4, final_norm.reshape(1, D), loss_target.reshape(T, D))
    loss = lax.psum(loss_part[0, 0], MESH_AXES)

    dy_d = _mm(dx, d_w_out_f, "nt", T, W_D, D, BF16, "dy_d", 512, W_D, D).reshape(B, S, W_D)
    gw_d_out = _mm(y_d, dx, "tn", W_D, D, T, BF16, "gw_d_out", W_D, D, 512).reshape(N_DEV, W_D // N_DEV, D)
    dproj_d, dcum = _fox_bwd(proj_d, cum_t, o_d, lse_d, dy_d, W_D, H_D)
    df_t, g_b_f = _fox_gate_bwd(dcum.reshape(B, H_D, S), f_t, b_f_col)
    F_PAD = N_D_PAD - 4 * W_D
    df = jnp.pad(jnp.swapaxes(df_t, 1, 2), ((0, 0), (0, 0), (0, F_PAD - H_D))).reshape(T, F_PAD)
    tc = min(512, W_D)
    gw_main = _mm(h_d, dproj_d, "tn", D, 4 * W_D, T, BF16, "gw_d_in", D, tc, 512,
                  b_spec=_sectioned_spec(dproj_d, 512, tc, 2, 1))
    gw_f = _mm(h_d, df, "tn", D, F_PAD, T, BF16, "gw_d_in_f", D, F_PAD, 512)
    gw_d_in = jnp.moveaxis(
        jnp.concatenate([gw_main, gw_f], axis=1)[:, :N_D].reshape(D, N_DEV, N_D // N_DEV), 1, 0)
    dh = _mm(dproj_d, d_w_in_f, "nt", T, D, 4 * W_D, F32, "dh_d", 512, D, tc,
             a_spec=_sectioned_spec(dproj_d, 512, tc, 0, 2))
    dh = _mm(df, d_w_in_f[:, 4 * W_D:], "nt", T, D, F_PAD, F32, "dh_d_f", 512, D, F_PAD, res=dh)
    dx, g_d_norm = _rmsnorm_bwd(x3, d_norm_f, dh, dx, "rms_bwd_d")

    dy_c = _mm(dx, c_w_out_f, "nt", T, W_C, D, BF16, "dy_c", 512, W_C, D).reshape(B, S, W_C)
    gw_c_out = _mm(y_c, dx, "tn", W_C, D, T, BF16, "gw_c_out", 1024, D, 512).reshape(N_DEV, W_C // N_DEV, D)
    (dy1, dgate_c, g_c_cw, g_c_cb, g_c_lg, g_c_lb), parts_d = _conv_bwd1(
        proj_c, dy_c, c_cw_f, c_cb_f, c_lg_f, c_lb_f, W_C, _Exchange([gw_d_in, gw_d_out], ["scatter"] * 2))
    dproj_c = _conv_bwd2(proj_c, dy1, dgate_c, c_cw_f[::-1], W_C).reshape(T, 3 * W_C)
    gw_c_in = _mm_grad_dev(h_c, dproj_c, "gw_c_in")
    dh = _mm_wT_dev(dproj_c, wg["c_w_in"], "dh_c")
    dx, g_c_norm = _rmsnorm_bwd(x2, c_norm_f, dh, dx, "rms_bwd_c")

    dy_b = _mm(dx, b_w_out_f, "nt", T, W_B, D, BF16, "dy_b", 512, W_B, D)
    gw_b_out = _mm(y_b, dx, "tn", W_B, D, T, BF16, "gw_b_out", 1024, D, 512).reshape(N_DEV, W_B // N_DEV, D)
    dproj_b, g_wm, g_bs_t, g_b_lg, g_b_lb = _gmlp_bwd(proj_b, dy_b, wm, bs_t, b_lg_f, b_lb_f, W_B)
    g_b_w_s = jnp.tril(g_wm)
    g_b_b_s = g_bs_t[:, :G].T
    gw_b_in = _mm_grad_dev(h_b, dproj_b, "gw_b_in")
    dh = _mm_wT_dev(dproj_b, wg["b_w_in"], "dh_b")
    dx, g_b_norm = _rmsnorm_bwd(x1, b_norm_f, dh, dx, "rms_bwd_b")

    dy_a = _mm(dx, a_w_out_f, "nt", T, W_A, D, BF16, "dy_a", 512, W_A, D).reshape(B, S, W_A)
    gw_a_out = _mm(y_a, dx, "tn", W_A, D, T, BF16, "gw_a_out", W_A, D, 512).reshape(N_DEV, W_A // N_DEV, D)
    small_full = [g_b_norm, g_b_lg, g_b_lb, g_b_w_s, g_b_b_s, g_c_norm, g_c_cw, g_c_cb, g_c_lg, g_c_lb,
                  g_d_norm, g_b_f, g_final]
    dproj_a, parts_s = _sb_bwd(
        proj_a, o_a, dy_a, W_A, SB_HEADS,
        _Exchange([gw_c_in, gw_c_out, gw_b_in, gw_b_out, gw_a_out, _pack(small_full)], ["scatter"] * 5 + ["gather"]))
    gw_a_in = _mm_grad_dev(h_a, dproj_a, "gw_a_in")
    dh, parts_a = _mm_wT_dev(dproj_a, wg["a_w_in"], "dh_a", exch=_Exchange([gw_a_in], ["scatter"]))
    dx, g_a_norm = _rmsnorm_bwd(x0, a_norm, dh, dx, "rms_bwd_a")
    grad_x = dx.reshape(B, S, D)

    (parts_n,) = _Exchange([_pack([g_a_norm])], ["gather"]).run("exchange_last")
    big_parts = dict(a_w_in=parts_a[0], a_w_out=parts_s[4], b_w_in=parts_s[2], b_w_out=parts_s[3],
                     c_w_in=parts_s[0], c_w_out=parts_s[1], d_w_in=parts_d[0], d_w_out=parts_d[1])
    (s_b_norm, s_b_lg, s_b_lb, s_b_w_s, s_b_b_s, s_c_norm, s_c_cw, s_c_cb, s_c_lg, s_c_lb,
     s_d_norm, s_b_f, s_final) = _unpack(_sum_parts(parts_s[5], "sum_small"), [g.shape for g in small_full])
    (s_a_norm,) = _unpack(_sum_parts(parts_n, "sum_a_norm"), [g_a_norm.shape])

    weights = dict(a_norm=a_norm, a_w_in=a_w_in, a_w_out=a_w_out, b_norm=b_norm, b_w_in=b_w_in, b_v_ln_g=b_v_ln_g,
                   b_v_ln_b=b_v_ln_b, b_w_s=b_w_s, b_b_s=b_b_s, b_w_out=b_w_out, c_norm=c_norm, c_w_in=c_w_in,
                   c_conv_w=c_conv_w, c_conv_b=c_conv_b, c_ln_g=c_ln_g, c_ln_b=c_ln_b, c_w_out=c_w_out,
                   d_norm=d_norm, d_w_in=d_w_in, d_b_f=d_b_f, d_w_out=d_w_out, final_norm=final_norm)
    mom_m = dict(a_norm=m_a_norm, a_w_in=m_a_w_in, a_w_out=m_a_w_out, b_norm=m_b_norm, b_w_in=m_b_w_in,
                 b_v_ln_g=m_b_v_ln_g, b_v_ln_b=m_b_v_ln_b, b_w_s=m_b_w_s, b_b_s=m_b_b_s, b_w_out=m_b_w_out,
                 c_norm=m_c_norm, c_w_in=m_c_w_in, c_conv_w=m_c_conv_w, c_conv_b=m_c_conv_b, c_ln_g=m_c_ln_g,
                 c_ln_b=m_c_ln_b, c_w_out=m_c_w_out, d_norm=m_d_norm, d_w_in=m_d_w_in, d_b_f=m_d_b_f,
                 d_w_out=m_d_w_out, final_norm=m_final_norm)
    mom_v = dict(a_norm=v_a_norm, a_w_in=v_a_w_in, a_w_out=v_a_w_out, b_norm=v_b_norm, b_w_in=v_b_w_in,
                 b_v_ln_g=v_b_v_ln_g, b_v_ln_b=v_b_v_ln_b, b_w_s=v_b_w_s, b_b_s=v_b_b_s, b_w_out=v_b_w_out,
                 c_norm=v_c_norm, c_w_in=v_c_w_in, c_conv_w=v_c_conv_w, c_conv_b=v_c_conv_b, c_ln_g=v_c_ln_g,
                 c_ln_b=v_c_ln_b, c_w_out=v_c_w_out, d_norm=v_d_norm, d_w_in=v_d_w_in, d_b_f=v_d_b_f,
                 d_w_out=v_d_w_out, final_norm=v_final_norm)
    order = list(weights)
    grads, deltas, new_m, new_v = {}, {}, {}, {}

    for n in big_names:
        part = big_parts[n]
        shp = weights[n].shape
        R, C = shp[1], shp[2]
        res = _adamw(part, weights[n].reshape(R, C), mom_m[n].reshape(R, C), mom_v[n].reshape(R, C), "adamw_" + n)
        grads[n], deltas[n], new_m[n], new_v[n] = [r.reshape(shp) for r in res]

    small_g = dict(
        a_norm=s_a_norm, b_norm=_my_cols(s_b_norm, me), b_v_ln_g=_my_cols(s_b_lg, me),
        b_v_ln_b=_my_cols(s_b_lb, me), b_w_s=s_b_w_s[None], b_b_s=s_b_b_s[None], c_norm=_my_cols(s_c_norm, me),
        c_conv_w=_my_cols(s_c_cw, me)[None], c_conv_b=_my_cols(s_c_cb, me), c_ln_g=_my_cols(s_c_lg, me),
        c_ln_b=_my_cols(s_c_lb, me), d_norm=_my_cols(s_d_norm, me), d_b_f=s_b_f.reshape(1, H_D),
        final_norm=s_final.reshape(D))
    small_names = list(small_g)
    sg_p = _pack([small_g[n] for n in small_names])
    res = _adamw(sg_p[None], _pack([weights[n] for n in small_names]), _pack([mom_m[n] for n in small_names]),
                 _pack([mom_v[n] for n in small_names]), "adamw_small")
    shapes = [weights[n].shape for n in small_names]
    for dst, r in zip((grads, deltas, new_m, new_v), res):
        for n, val in zip(small_names, _unpack(r, shapes)):
            dst[n] = val

    return (loss, grad_x, *[grads[n] for n in order], *[deltas[n] for n in order],
            *[new_m[n] for n in order], *[new_v[n] for n in order])
```

```python
import functools
import math

import jax
import jax.numpy as jnp
from jax import lax
from jax.experimental import pallas as pl
from jax.experimental.pallas import tpu as pltpu

F32 = jnp.float32
BF16 = jnp.bfloat16

EPS = 1e-6
SB_HEADS = 16
CONV_HALO = 32
BLK = 128
ATT_TK = 256
ATT_TQ = 512
ATT_GP = 2
LANES = 128
N_DEV = 8
MESH_AXES = ("x", "y", "c")

ADAM_LR = 0.001
ADAM_B1 = 0.9
ADAM_B2 = 0.999
ADAM_EPS = 1e-08
ADAM_WD = 0.01
ADAM_STEP = 10

VMEM_LIMIT = 56 * 1024 * 1024
NEG_BIG = -1e30

_NN = (((1,), (0,)), ((), ()))
_NT = (((1,), (1,)), ((), ()))
_TN = (((0,), (0,)), ((), ()))


def _dot(a, b, dims=_NN):
    return lax.dot_general(a, b, dims, preferred_element_type=F32)


def _split_dot(x, m):
    hi = x.astype(BF16)
    lo = (x - hi.astype(F32)).astype(BF16)
    return _dot(hi, m) + _dot(lo, m)


def _split3_dot(x, m):
    hi = x.astype(BF16)
    r1 = x - hi.astype(F32)
    mid = r1.astype(BF16)
    lo = (r1 - mid.astype(F32)).astype(BF16)
    return _dot(hi, m) + _dot(mid, m) + _dot(lo, m)


def _params(sem=None):
    kw = dict(vmem_limit_bytes=VMEM_LIMIT)
    if sem is not None:
        kw["dimension_semantics"] = sem
    return pltpu.CompilerParams(**kw)


def _sigmoid(x):
    return jax.nn.sigmoid(x)


def _silu(x):
    return x * _sigmoid(x)


def _dsilu(x):
    s = _sigmoid(x)
    return s * (1.0 + x * (1.0 - s))


_GELU_C = math.sqrt(2.0 / math.pi)


def _gelu(x):
    return 0.5 * x * (1.0 + jnp.tanh(_GELU_C * (x + 0.044715 * x * x * x)))


def _dgelu(x):
    th = jnp.tanh(_GELU_C * (x + 0.044715 * x * x * x))
    return 0.5 * (1.0 + th) + 0.5 * x * (1.0 - th * th) * _GELU_C * (1.0 + 3.0 * 0.044715 * x * x)


def _mm(a, b, mode, M, N, K, out_dtype, name, tm, tn, tk, a_spec=None, b_spec=None, o_spec=None, out_shape=None,
        exch=None, res=None):
    tm, tn, tk = min(tm, M), min(tn, N), min(tk, K)
    assert M % tm == 0 and N % tn == 0 and K % tk == 0, (name, M, N, K, tm, tn, tk)
    nk = K // tk
    dims = {"nn": _NN, "nt": _NT, "tn": _TN}[mode]
    if a_spec is None:
        a_spec = (pl.BlockSpec((tk, tm), lambda i, j, k: (k, i)) if mode == "tn"
                  else pl.BlockSpec((tm, tk), lambda i, j, k: (i, k)))
    if b_spec is None:
        b_spec = (pl.BlockSpec((tn, tk), lambda i, j, k: (j, k)) if mode == "nt"
                  else pl.BlockSpec((tk, tn), lambda i, j, k: (k, j)))
    if o_spec is None:
        o_spec = pl.BlockSpec((tm, tn), lambda i, j, k: (i, j))
    if out_shape is None:
        out_shape = (M, N)

    def body(a_ref, b_ref, *rest):
        res_ref = rest[0] if res is not None else None
        if nk == 1:
            o_ref = rest[-1]
            d = _dot(a_ref[...].astype(BF16), b_ref[...].astype(BF16), dims)
            o_ref[...] = (d if res_ref is None else res_ref[...].astype(F32) + d).astype(o_ref.dtype)
            return
        o_ref, acc_ref = rest[-2:]
        k = pl.program_id(2)

        @pl.when(k == 0)
        def _():
            acc_ref[...] = jnp.zeros_like(acc_ref) if res_ref is None else res_ref[...].astype(F32)

        acc_ref[...] += _dot(a_ref[...].astype(BF16), b_ref[...].astype(BF16), dims)

        @pl.when(k == nk - 1)
        def _():
            o_ref[...] = acc_ref[...].astype(o_ref.dtype)

    in_specs, args = [a_spec, b_spec], (a, b)
    if res is not None:
        in_specs, args = in_specs + [o_spec], args + (res,)
    scratch = [pltpu.VMEM((tm, tn), F32)] if nk > 1 else []
    if exch is None:
        return pl.pallas_call(
            body, name=name, grid=(M // tm, N // tn, nk),
            in_specs=in_specs, out_specs=o_spec,
            out_shape=jax.ShapeDtypeStruct(out_shape, out_dtype),
            scratch_shapes=scratch,
            compiler_params=_params(("parallel", "parallel", "arbitrary")),
        )(*args)
    (out,), moved = _call_hosting(
        body, exch, name, (M // tm, N // tn, nk), in_specs, [o_spec],
        [jax.ShapeDtypeStruct(out_shape, out_dtype)], scratch, args)
    return out, moved


def _mm_w_dev(a, w3, name, out_dtype=F32, tm=1024):
    M, K = a.shape
    n8 = w3.shape[2]
    tn = n8 if n8 <= 768 else 512
    per = n8 // tn
    b_spec = pl.BlockSpec((None, K, tn), lambda i, j, k: (j // per, 0, j % per))
    return _mm(a, w3, "nn", M, N_DEV * n8, K, out_dtype, name, tm, tn, K, b_spec=b_spec)


def _sectioned_spec(d4, t_rows, t_cols, rows_axis, cols_axis):
    _, _, S, W = d4.shape
    assert S % t_rows == 0 and W % t_cols == 0
    rb, cb = S // t_rows, W // t_cols

    def index(*g):
        r, c = g[rows_axis], g[cols_axis]
        return (r // rb, c // cb, r % rb, c % cb)

    return pl.BlockSpec((None, None, t_rows, t_cols), index)


def _mm_wT_dev(a, w3, name, out_dtype=F32, tm=1024, exch=None):
    K, n8 = w3.shape[1], w3.shape[2]
    tk = n8 if n8 <= 768 else 512
    per = n8 // tk
    b_spec = pl.BlockSpec((None, K, tk), lambda i, j, k: (k // per, 0, k % per))
    if a.ndim == 4:
        M, N = a.shape[0] * a.shape[2], a.shape[1] * a.shape[3]
        tm = min(tm, a.shape[2])
        a_spec = _sectioned_spec(a, tm, tk, 0, 2)
    else:
        (M, N), a_spec = a.shape, None
    return _mm(a, w3, "nt", M, K, N, out_dtype, name, tm, K, tk, a_spec=a_spec, b_spec=b_spec, exch=exch)


def _mm_grad_dev(h, d, name, out_dtype=BF16):
    T, M = h.shape
    N = d.shape[1] * d.shape[3] if d.ndim == 4 else d.shape[1]
    n8 = N // N_DEV
    tn = n8 if n8 <= 768 else 512
    per = n8 // tn
    tm = min(M, 1024)
    o_spec = pl.BlockSpec((None, tm, tn), lambda i, j, k: (j // per, i, j % per))
    tk = min(1024, d.shape[2] if d.ndim == 4 else T)
    b_spec = _sectioned_spec(d, tk, tn, 2, 1) if d.ndim == 4 else None
    return _mm(h, d, "tn", M, N, T, out_dtype, name, tm, tn, tk, b_spec=b_spec, o_spec=o_spec,
               out_shape=(N_DEV, M, n8))


def _me():
    x, y, c = lax.axis_index("x"), lax.axis_index("y"), lax.axis_index("c")
    return x, y, c


def _peer(r):
    x, y, c = _me()
    px = 1 - x if (r >> 2) & 1 else x
    py = 1 - y if (r >> 1) & 1 else y
    pc = 1 - c if r & 1 else c
    return (px, py, pc), 4 * px + 2 * py + pc


class _Exchange:
    def __init__(self, arrays, kinds):
        self.arrays, self.kinds, self.n = list(arrays), list(kinds), len(arrays)
        self.out_shapes = [
            jax.ShapeDtypeStruct((N_DEV,) + a.shape if kind == "gather" else a.shape, a.dtype)
            for a, kind in zip(arrays, kinds)]
        self.specs = [pl.BlockSpec(memory_space=pl.ANY)] * self.n
        self.sems = [pltpu.SemaphoreType.DMA((self.n, N_DEV - 1)), pltpu.SemaphoreType.DMA((self.n, N_DEV - 1)),
                     pltpu.SemaphoreType.DMA((self.n,))]

    def _copies(self, ins, outs, sems, receiving):
        send_sems, recv_sems, local_sems = sems
        x, y, c = _me()
        me = 4 * x + 2 * y + c

        def src(k, pid):
            return ins[k] if self.kinds[k] == "gather" else ins[k].at[pid]

        local = [pltpu.make_async_copy(src(k, me), outs[k].at[me], local_sems.at[k]) for k in range(self.n)]
        remote = []
        for r in range(1, N_DEV):
            peer, pid = _peer(r)
            for k in range(self.n):
                remote.append(pltpu.make_async_remote_copy(
                    src_ref=src(k, pid), dst_ref=outs[k].at[pid if receiving else me],
                    send_sem=send_sems.at[k, r - 1], recv_sem=recv_sems.at[k, r - 1],
                    device_id=peer, device_id_type=pl.DeviceIdType.MESH))
        return local, remote

    def start(self, ins, outs, sems):
        local, remote = self._copies(ins, outs, sems, False)
        for cp in local + remote:
            cp.start()

    def wait(self, ins, outs, sems):
        local, remote = self._copies(ins, outs, sems, True)
        for cp in remote:
            cp.wait_recv()
        for cp in remote:
            cp.wait_send()
        for cp in local:
            cp.wait()

    def run(self, name):
        n = self.n

        def body(*refs):
            ins, outs, sems = refs[:n], refs[n:2 * n], refs[2 * n:]
            self.start(ins, outs, sems)
            self.wait(ins, outs, sems)

        return pl.pallas_call(
            body, name=name, in_specs=self.specs, out_specs=self.specs, out_shape=self.out_shapes,
            scratch_shapes=self.sems,
        )(*self.arrays)


def _call_hosting(body, exch, name, grid, in_specs, out_specs, out_shape, scratch_shapes, args):
    if exch is None:
        res = pl.pallas_call(
            body, name=name, grid=grid, in_specs=list(in_specs), out_specs=list(out_specs),
            out_shape=list(out_shape), scratch_shapes=list(scratch_shapes),
            compiler_params=_params(("arbitrary",) * len(grid)))(*args)
        return res, []
    n_in, n_out, n_scr, nc = len(in_specs), len(out_specs), len(scratch_shapes), exch.n

    def full_body(*refs):
        ins, refs = refs[:n_in], refs[n_in:]
        cins, refs = refs[:nc], refs[nc:]
        outs, refs = refs[:n_out], refs[n_out:]
        couts, refs = refs[:nc], refs[nc:]
        scr, sems = refs[:n_scr], refs[n_scr:]
        ids = [pl.program_id(a) for a in range(len(grid))]
        first = functools.reduce(jnp.logical_and, [i == 0 for i in ids])
        last = functools.reduce(jnp.logical_and, [i == g - 1 for i, g in zip(ids, grid)])

        @pl.when(first)
        def _():
            exch.start(cins, couts, sems)

        body(*ins, *outs, *scr)

        @pl.when(last)
        def _():
            exch.wait(cins, couts, sems)

    res = pl.pallas_call(
        full_body, name=name, grid=grid,
        in_specs=list(in_specs) + exch.specs, out_specs=list(out_specs) + exch.specs,
        out_shape=list(out_shape) + exch.out_shapes,
        scratch_shapes=list(scratch_shapes) + exch.sems,
        compiler_params=_params(("arbitrary",) * len(grid)),
    )(*args, *exch.arrays)
    return res[:n_out], res[n_out:]


def _rmsnorm_fwd(x, g, name):
    T, D = x.shape
    tr = min(256, T)

    def body(x_ref, g_ref, h_ref):
        xv = x_ref[...]
        r = lax.rsqrt(jnp.mean(xv * xv, axis=-1, keepdims=True) + EPS)
        h_ref[...] = (xv * r * g_ref[...]).astype(BF16)

    return pl.pallas_call(
        body, name=name, grid=(T // tr,),
        in_specs=[pl.BlockSpec((tr, D), lambda i: (i, 0)), pl.BlockSpec((1, D), lambda i: (0, 0))],
        out_specs=pl.BlockSpec((tr, D), lambda i: (i, 0)),
        out_shape=jax.ShapeDtypeStruct((T, D), BF16),
        compiler_params=_params(("parallel",)),
    )(x, g)


def _rmsnorm_bwd(x, g, dh, dres, name):
    T, D = x.shape
    tr = min(256, T)

    def body(x_ref, g_ref, dh_ref, dres_ref, dx_ref, dg_ref):
        i = pl.program_id(0)
        xv = x_ref[...]
        r = lax.rsqrt(jnp.mean(xv * xv, axis=-1, keepdims=True) + EPS)
        xh = xv * r
        dhv = dh_ref[...]
        dxh = dhv * g_ref[...]
        dx_ref[...] = dres_ref[...] + r * (dxh - xh * jnp.mean(dxh * xh, axis=-1, keepdims=True))

        @pl.when(i == 0)
        def _():
            dg_ref[...] = jnp.zeros_like(dg_ref)

        dg_ref[...] += jnp.sum(dhv * xh, axis=0, keepdims=True)

    row = pl.BlockSpec((tr, D), lambda i: (i, 0))
    vec = pl.BlockSpec((1, D), lambda i: (0, 0))
    return pl.pallas_call(
        body, name=name, grid=(T // tr,),
        in_specs=[row, vec, row, row], out_specs=[row, vec],
        out_shape=[jax.ShapeDtypeStruct((T, D), F32), jax.ShapeDtypeStruct((1, D), F32)],
        compiler_params=_params(("arbitrary",)),
    )(x, g, dh, dres)


def _loss_head(x, g, target):
    T, D = x.shape
    tr = min(256, T)

    def body(x_ref, g_ref, t_ref, loss_ref, dx_ref, dg_ref):
        i = pl.program_id(0)
        xv = x_ref[...]
        gv = g_ref[...]
        r = lax.rsqrt(jnp.mean(xv * xv, axis=-1, keepdims=True) + EPS)
        xh = xv * r
        diff = xh * gv - t_ref[...]
        dy = diff * (1.0 / D)
        dxh = dy * gv
        dx_ref[...] = r * (dxh - xh * jnp.mean(dxh * xh, axis=-1, keepdims=True))

        @pl.when(i == 0)
        def _():
            dg_ref[...] = jnp.zeros_like(dg_ref)
            loss_ref[...] = jnp.zeros_like(loss_ref)

        dg_ref[...] += jnp.sum(dy * xh, axis=0, keepdims=True)
        part = jnp.sum(jnp.sum(diff * diff, axis=1, keepdims=True), axis=0, keepdims=True)
        loss_ref[...] += (0.5 / D) * part

    row = pl.BlockSpec((tr, D), lambda i: (i, 0))
    vec = pl.BlockSpec((1, D), lambda i: (0, 0))
    return pl.pallas_call(
        body, name="loss_head", grid=(T // tr,),
        in_specs=[row, vec, row],
        out_specs=[pl.BlockSpec((1, 1), lambda i: (0, 0)), row, vec],
        out_shape=[jax.ShapeDtypeStruct((1, 1), F32), jax.ShapeDtypeStruct((T, D), F32),
                   jax.ShapeDtypeStruct((1, D), F32)],
        compiler_params=_params(("arbitrary",)),
    )(x, g, target)


ELEMS_PER_STEP = 1 << 20


def _row_tile(R, per_row):
    best = None
    for tr in range(8, R + 1, 8):
        if R % tr == 0 and tr * per_row <= ELEMS_PER_STEP:
            best = tr
    return best if best is not None else R


def _adamw(parts, w, m, v, name):
    P, R, C = parts.shape
    tr = _row_tile(R, P * C)

    def body(p_ref, w_ref, m_ref, v_ref, g_out, d_out, m_out, v_out):
        g = p_ref[0].astype(F32)
        for p in range(1, P):
            g = g + p_ref[p].astype(F32)
        wv = w_ref[...]
        mn = ADAM_B1 * m_ref[...] + (1.0 - ADAM_B1) * g
        vn = ADAM_B2 * v_ref[...] + (1.0 - ADAM_B2) * (g * g)
        m_hat = mn / (1.0 - ADAM_B1 ** ADAM_STEP)
        v_hat = vn / (1.0 - ADAM_B2 ** ADAM_STEP)
        g_out[...] = g
        d_out[...] = -ADAM_LR * (m_hat / (jnp.sqrt(v_hat) + ADAM_EPS) + ADAM_WD * wv)
        m_out[...] = mn
        v_out[...] = vn

    row = pl.BlockSpec((tr, C), lambda i: (i, 0))
    return pl.pallas_call(
        body, name=name, grid=(R // tr,),
        in_specs=[pl.BlockSpec((P, tr, C), lambda i: (0, i, 0)), row, row, row],
        out_specs=[row, row, row, row],
        out_shape=[jax.ShapeDtypeStruct((R, C), F32)] * 4,
        compiler_params=_params(("parallel",)),
    )(parts, w, m, v)


def _sum_parts(parts, name):
    P, R, C = parts.shape
    tr = _row_tile(R, P * C)

    def body(p_ref, o_ref):
        g = p_ref[0]
        for p in range(1, P):
            g = g + p_ref[p]
        o_ref[...] = g

    return pl.pallas_call(
        body, name=name, grid=(R // tr,),
        in_specs=[pl.BlockSpec((P, tr, C), lambda i: (0, i, 0))],
        out_specs=pl.BlockSpec((tr, C), lambda i: (i, 0)),
        out_shape=jax.ShapeDtypeStruct((R, C), F32),
        compiler_params=_params(("parallel",)),
    )(parts)


def _lane_head(Dh):
    assert Dh & (Dh - 1) == 0 and Dh <= LANES
    return lax.shift_right_logical(lax.broadcasted_iota(jnp.int32, (1, LANES), 1), Dh.bit_length() - 1)


def _stack_heads(x, lane_head, hpb):
    return jnp.concatenate([jnp.where(lane_head == h, x, 0.0) for h in range(hpb)], axis=0)


def _unstack_heads(acc, lane_head, hpb):
    TQ = acc.shape[0] // hpb
    out = acc[0:TQ]
    for h in range(1, hpb):
        out = jnp.where(lane_head == h, acc[h * TQ:(h + 1) * TQ], out)
    return out


def _live_rows(x, r0, hpb):
    if r0 == 0:
        return x
    TQ = x.shape[0] // hpb
    return jnp.concatenate([x[h * TQ + r0:(h + 1) * TQ] for h in range(hpb)], axis=0)


def _put_rows(full, part, r0, hpb):
    if r0 == 0:
        return part
    TQ = full.shape[0] // hpb
    n = TQ - r0
    return jnp.concatenate(
        [blk for h in range(hpb) for blk in (full[h * TQ:h * TQ + r0], part[h * n:(h + 1) * n])], axis=0)


def _first_live_row(m, TQ, TK):
    return max(0, TQ - (m + 1) * TK)


def _key_tile(S):
    return ATT_TK if S % ATT_TK == 0 else BLK


def _query_tile(S):
    return ATT_TQ if S % ATT_TQ == 0 else BLK


def _lane_groups(P):
    return ATT_GP if P % ATT_GP == 0 else 1


def _lanes(u):
    return slice(u * LANES, (u + 1) * LANES)


def _causal_iotas(RS, TK, TQ, r0=0):
    n = TQ - r0
    assert n & (n - 1) == 0 and (TK % TQ == 0 or TQ % TK == 0)
    rows = RS // TQ * n
    trow = jnp.bitwise_and(lax.broadcasted_iota(jnp.int32, (rows, TK), 0), n - 1) + r0
    col = lax.broadcasted_iota(jnp.int32, (rows, TK), 1)
    return trow, col


def _tri(TK, op):
    r = lax.broadcasted_iota(jnp.int32, (TK, TK), 0)
    c = lax.broadcasted_iota(jnp.int32, (TK, TK), 1)
    return op(r, c).astype(BF16)


def _logsig_parts(z):
    lb = jnp.minimum(z, 0.0) - jnp.log(1.0 + jnp.exp(-jnp.abs(z)))
    return lb, lb - z


def _sb_fwd(proj3, W, heads, exch):
    B, S, _ = proj3.shape
    Dh = W // heads
    hpb = LANES // Dh
    P, TQ = W // LANES, _query_tile(S)
    NQ = S // TQ
    scale = 1.0 / math.sqrt(Dh)

    TK = _key_tile(S)
    RS = hpb * TQ
    NM = max(1, TQ // TK)
    GP = _lane_groups(P)
    PG = P // GP

    def body(q_ref, k_ref, v_ref, g_ref, o_ref, y_ref):
        i = pl.program_id(2)
        lane_head = _lane_head(Dh)
        msuf = _tri(TK, lambda r, c: r > c)
        qs = [(_stack_heads(q_ref[:, _lanes(u)], lane_head, hpb) * scale).astype(BF16) for u in range(GP)]
        nt = (i * TQ + TQ - 2) // TK + 1

        def tile(jt, carry, masked, r0=0):
            off = pl.multiple_of(jt * TK, TK)
            if masked:
                trow, col = _causal_iotas(RS, TK, TQ, r0)
                msk = col + (jt * TK - i * TQ) < trow
            out = []
            for u, (rem_all, acc_all) in enumerate(carry):
                rem, acc = _live_rows(rem_all, r0, hpb), _live_rows(acc_all, r0, hpb)
                kj = k_ref[pl.ds(off, TK), _lanes(u)].astype(BF16)
                vj = v_ref[pl.ds(off, TK), _lanes(u)].astype(BF16)
                lb, lr = _logsig_parts(_dot(_live_rows(qs[u], r0, hpb), kj, _NT))
                if masked:
                    lr = jnp.where(msk, lr, 0.0)
                w = jnp.exp(lb + _split_dot(lr, msuf) + rem)
                if masked:
                    w = jnp.where(msk, w, 0.0)
                out.append((_put_rows(rem_all, rem + jnp.sum(lr, axis=1, keepdims=True), r0, hpb),
                            _put_rows(acc_all, acc + _dot(w.astype(BF16), vj), r0, hpb)))
            return tuple(out)

        zero = (jnp.zeros((RS, 1), F32), jnp.zeros((RS, LANES), F32))
        carry = (zero,) * GP
        for m in range(NM):
            carry = tile(nt - 1 - m, carry, True, _first_live_row(m, TQ, TK))
        carry = lax.fori_loop(NM, nt, lambda jj, c: tile(nt - 1 - jj, c, False), carry)
        for u in range(GP):
            o = _unstack_heads(carry[u][1], lane_head, hpb)
            o_ref[:, _lanes(u)] = o
            y_ref[:, _lanes(u)] = (o * _silu(g_ref[:, _lanes(u)])).astype(BF16)

    LW = GP * LANES
    blk = lambda sec: pl.BlockSpec((None, TQ, LW), lambda b, p, i: (b, i, sec * PG + p))
    full = lambda sec: pl.BlockSpec((None, S, LW), lambda b, p, i: (b, 0, sec * PG + p))
    out = pl.BlockSpec((None, TQ, LW), lambda b, p, i: (b, i, p))
    return _call_hosting(
        body, exch, "sb_fwd", (B, PG, NQ), [blk(0), full(1), full(2), blk(3)], [out, out],
        [jax.ShapeDtypeStruct((B, S, W), F32), jax.ShapeDtypeStruct((B, S, W), BF16)], [],
        (proj3, proj3, proj3, proj3))


def _sb_bwd(proj3, o, dy, W, heads, exch):
    B, S, _ = proj3.shape
    Dh = W // heads
    hpb = LANES // Dh
    P, TQ = W // LANES, _query_tile(S)
    NQ = S // TQ
    scale = 1.0 / math.sqrt(Dh)

    TK = _key_tile(S)
    RS = hpb * TQ
    NM = max(1, TQ // TK)

    def body(q_ref, k_ref, v_ref, g_ref, o_ref, dy_ref, dp_ref, dk_ref, dv_ref, u_ref, sig_ref, es_ref):
        i = pl.program_id(2)
        rows = pl.ds(pl.multiple_of(i * TQ, TQ), TQ)

        @pl.when(i == 0)
        def _():
            dk_ref[...] = jnp.zeros_like(dk_ref)
            dv_ref[...] = jnp.zeros_like(dv_ref)

        lane_head = _lane_head(Dh)
        msuf = _tri(TK, lambda r, c: r > c)
        mpre = _tri(TK, lambda r, c: r < c)
        g = g_ref[...]
        dyv = dy_ref[...].astype(F32)
        dp_ref[3, rows, :] = (dyv * o_ref[...] * _dsilu(g)).astype(dp_ref.dtype)
        qs = (_stack_heads(q_ref[...], lane_head, hpb) * scale).astype(BF16)
        dos = _stack_heads(dyv * _silu(g), lane_head, hpb).astype(BF16)
        nt = (i * TQ + TQ - 2) // TK + 1

        def weights(jt, rem_all, masked, r0=0):
            off = pl.multiple_of(jt * TK, TK)
            kj = k_ref[pl.ds(off, TK), :].astype(BF16)
            vj = v_ref[pl.ds(off, TK), :].astype(BF16)
            dos_l = _live_rows(dos, r0, hpb)
            lb, lr = _logsig_parts(_dot(_live_rows(qs, r0, hpb), kj, _NT))
            if masked:
                trow, col = _causal_iotas(RS, TK, TQ, r0)
                msk = col + (jt * TK - i * TQ) < trow
                lr = jnp.where(msk, lr, 0.0)
            w = jnp.exp(lb + _split_dot(lr, msuf) + _live_rows(rem_all, r0, hpb))
            if masked:
                w = jnp.where(msk, w, 0.0)
            e = w * _dot(dos_l, vj, _NT)
            sig = jnp.exp(lb)
            u = e * (1.0 - sig) - _split_dot(e, mpre) * sig
            if masked:
                u = jnp.where(msk, u, 0.0)
                sig = jnp.where(msk, sig, 0.0)
            u_ref[jt] = _put_rows(jnp.zeros((RS, TK), F32), u, r0, hpb)
            sig_ref[jt] = _put_rows(jnp.zeros((RS, TK), F32), sig, r0, hpb)
            es_ref[jt] = _put_rows(jnp.zeros((RS, 1), F32), jnp.sum(e, axis=1, keepdims=True), r0, hpb)
            dv_ref[pl.ds(off, TK), :] += _dot(w.astype(BF16), dos_l, _TN)
            return _put_rows(rem_all, _live_rows(rem_all, r0, hpb) + jnp.sum(lr, axis=1, keepdims=True), r0, hpb)

        rem = jnp.zeros((RS, 1), F32)
        for m in range(NM):
            rem = weights(nt - 1 - m, rem, True, _first_live_row(m, TQ, TK))
        lax.fori_loop(NM, nt, lambda jj, r: weights(nt - 1 - jj, r, False), rem)

        def grads(jt, carry):
            pre, acc = carry
            off = pl.multiple_of(jt * TK, TK)
            kj = k_ref[pl.ds(off, TK), :].astype(BF16)
            dz = (u_ref[jt] - pre * sig_ref[jt]).astype(BF16)
            dk_ref[pl.ds(off, TK), :] += _dot(dz, qs, _TN)
            return pre + es_ref[jt], acc + _dot(dz, kj)

        _, acc = lax.fori_loop(0, nt, grads, (jnp.zeros((RS, 1), F32), jnp.zeros((RS, LANES), F32)))
        dp_ref[0, rows, :] = (_unstack_heads(acc, lane_head, hpb) * scale).astype(dp_ref.dtype)

        @pl.when(i == NQ - 1)
        def _():
            dp_ref[1] = dk_ref[...].astype(dp_ref.dtype)
            dp_ref[2] = dv_ref[...].astype(dp_ref.dtype)

    blk = lambda sec: pl.BlockSpec((None, TQ, LANES), lambda b, p, i: (b, i, sec * P + p))
    full = lambda sec: pl.BlockSpec((None, S, LANES), lambda b, p, i: (b, 0, sec * P + p))
    one = pl.BlockSpec((None, TQ, LANES), lambda b, p, i: (b, i, p))
    (dproj,), moved = _call_hosting(
        body, exch, "sb_bwd", (B, P, NQ), [blk(0), full(1), full(2), blk(3), one, one],
        [pl.BlockSpec((None, 4, S, LANES), lambda b, p, i: (b, 0, 0, p))],
        [jax.ShapeDtypeStruct((B, 4, S, W), BF16)],
        [pltpu.VMEM((S, LANES), F32), pltpu.VMEM((S, LANES), F32),
         pltpu.VMEM((S // TK, RS, TK), F32), pltpu.VMEM((S // TK, RS, TK), F32), pltpu.VMEM((S // TK, RS, 1), F32)],
        (proj3, proj3, proj3, proj3, o, dy))
    return dproj, moved


def _fox_gate_fwd(f_t, b_f):
    B, H, S = f_t.shape

    def body(f_ref, b_ref, c_ref):
        row = lax.broadcasted_iota(jnp.int32, (BLK, BLK), 0)
        col = lax.broadcasted_iota(jnp.int32, (BLK, BLK), 1)
        mpre = (row <= col).astype(BF16)
        carry = jnp.zeros((H, 1), F32)
        for n in range(S // BLK):
            sl = pl.ds(n * BLK, BLK)
            lf, _ = _logsig_parts(f_ref[:, sl] + b_ref[...])
            c_ref[:, sl] = _split3_dot(lf, mpre) + carry
            carry = carry + jnp.sum(lf, axis=1, keepdims=True)

    spec = pl.BlockSpec((None, H, S), lambda b: (b, 0, 0))
    return pl.pallas_call(
        body, name="fox_gate_fwd", grid=(B,),
        in_specs=[spec, pl.BlockSpec((H, 1), lambda b: (0, 0))], out_specs=spec,
        out_shape=jax.ShapeDtypeStruct((B, H, S), F32),
        compiler_params=_params(("parallel",)),
    )(f_t, b_f)


def _fox_gate_bwd(dcum_t, f_t, b_f):
    B, H, S = f_t.shape

    def body(d_ref, f_ref, b_ref, df_ref, db_ref):
        b = pl.program_id(0)

        @pl.when(b == 0)
        def _():
            db_ref[...] = jnp.zeros_like(db_ref)

        row = lax.broadcasted_iota(jnp.int32, (BLK, BLK), 0)
        col = lax.broadcasted_iota(jnp.int32, (BLK, BLK), 1)
        msuf = (row >= col).astype(BF16)
        carry = jnp.zeros((H, 1), F32)
        dbacc = jnp.zeros((H, 1), F32)
        for n in reversed(range(S // BLK)):
            sl = pl.ds(n * BLK, BLK)
            dv = d_ref[:, sl]
            dlf = _split3_dot(dv, msuf) + carry
            carry = carry + jnp.sum(dv, axis=1, keepdims=True)
            df = dlf * _sigmoid(-(f_ref[:, sl] + b_ref[...]))
            df_ref[:, sl] = df
            dbacc = dbacc + jnp.sum(df, axis=1, keepdims=True)
        db_ref[...] += dbacc

    spec = pl.BlockSpec((None, H, S), lambda b: (b, 0, 0))
    vec = pl.BlockSpec((H, 1), lambda b: (0, 0))
    return pl.pallas_call(
        body, name="fox_gate_bwd", grid=(B,),
        in_specs=[spec, spec, vec], out_specs=[spec, vec],
        out_shape=[jax.ShapeDtypeStruct((B, H, S), F32), jax.ShapeDtypeStruct((H, 1), F32)],
        compiler_params=_params(("arbitrary",)),
    )(dcum_t, f_t, b_f)


def _pick_col(block, idx, lane_iota):
    return jnp.sum(jnp.where(lane_iota == idx, block, 0.0), axis=1, keepdims=True)


def _pick_row(block, idx, sub_iota):
    return jnp.sum(jnp.where(sub_iota == idx, block, 0.0), axis=0, keepdims=True)


def _fox_fwd(proj3, cum_t, W, heads):
    B, S, _ = proj3.shape
    H = heads
    Dh = W // heads
    hpb = LANES // Dh
    P, TQ = W // LANES, _query_tile(S)
    NQ = S // TQ
    scale = 1.0 / math.sqrt(Dh)

    TK = _key_tile(S)
    RS = hpb * TQ
    NM = max(1, TQ // TK)

    def body(q_ref, k_ref, v_ref, g_ref, ct_ref, o_ref, y_ref, lse_ref):
        p = pl.program_id(1)
        i = pl.program_id(2)
        lane_head = _lane_head(Dh)
        sub_h = lax.broadcasted_iota(jnp.int32, (H, 1), 0)
        qs = (_stack_heads(q_ref[...], lane_head, hpb) * scale).astype(BF16)
        nt = (i * TQ + TQ - 1) // TK + 1

        def tile(jt, carry, masked, r0=0):
            mx, l, acc = [_live_rows(c, r0, hpb) for c in carry]
            n = TQ - r0
            off = pl.multiple_of(jt * TK, TK)
            kj = k_ref[pl.ds(off, TK), :].astype(BF16)
            vj = v_ref[pl.ds(off, TK), :].astype(BF16)
            ctb = ct_ref[:, pl.ds(off, TK)]
            z = _dot(_live_rows(qs, r0, hpb), kj, _NT)
            s = jnp.concatenate([z[h * n:(h + 1) * n] - _pick_row(ctb, p * hpb + h, sub_h) for h in range(hpb)],
                                axis=0)
            if masked:
                trow, col = _causal_iotas(RS, TK, TQ, r0)
                s = jnp.where(col + (jt * TK - i * TQ) <= trow, s, NEG_BIG)
            mx2 = jnp.maximum(mx, jnp.max(s, axis=1, keepdims=True))
            pe = jnp.exp(s - mx2)
            alpha = jnp.exp(mx - mx2)
            new = (mx2, alpha * l + jnp.sum(pe, axis=1, keepdims=True), alpha * acc + _dot(pe.astype(BF16), vj))
            return tuple(_put_rows(c, v, r0, hpb) for c, v in zip(carry, new))

        carry = lax.fori_loop(
            0, nt - NM, lambda jt, c: tile(jt, c, False),
            (jnp.full((RS, 1), NEG_BIG, F32), jnp.zeros((RS, 1), F32), jnp.zeros((RS, LANES), F32)))
        for m in reversed(range(NM)):
            carry = tile(nt - 1 - m, carry, True)
        mx, l, acc = carry
        o = _unstack_heads(acc / l, lane_head, hpb)
        o_ref[...] = o
        lse_ref[...] = _unstack_heads(jnp.broadcast_to(mx + jnp.log(l), (RS, LANES)), lane_head, hpb)
        y_ref[...] = (o * _silu(g_ref[...])).astype(BF16)

    blk = lambda sec: pl.BlockSpec((None, TQ, LANES), lambda b, p, i: (b, i, sec * P + p))
    full = lambda sec: pl.BlockSpec((None, S, LANES), lambda b, p, i: (b, 0, sec * P + p))
    out = pl.BlockSpec((None, TQ, LANES), lambda b, p, i: (b, i, p))
    return pl.pallas_call(
        body, name="fox_fwd", grid=(B, P, NQ),
        in_specs=[blk(0), full(1), full(2), blk(3),
                  pl.BlockSpec((None, H, S), lambda b, p, i: (b, 0, 0))],
        out_specs=[out, out, out],
        out_shape=[jax.ShapeDtypeStruct((B, S, W), F32), jax.ShapeDtypeStruct((B, S, W), BF16),
                   jax.ShapeDtypeStruct((B, S, W), F32)],
        compiler_params=_params(("parallel", "parallel", "arbitrary")),
    )(proj3, proj3, proj3, proj3, cum_t)


def _fox_bwd(proj3, cum_t, o, lse, dy, W, heads):
    B, S, _ = proj3.shape
    H = heads
    Dh = W // heads
    hpb = LANES // Dh
    P, TQ = W // LANES, _query_tile(S)
    NQ = S // TQ
    scale = 1.0 / math.sqrt(Dh)

    TK = _key_tile(S)
    RS = hpb * TQ
    NM = max(1, TQ // TK)

    def body(q_ref, k_ref, v_ref, g_ref, ct_ref, o_ref, lse_ref, dy_ref,
             dpj_ref, dc_ref, dk_ref, dv_ref, p_scr, dp_scr):
        p = pl.program_id(1)
        i = pl.program_id(2)
        rows = pl.ds(pl.multiple_of(i * TQ, TQ), TQ)

        @pl.when(i == 0)
        def _():
            dk_ref[...] = jnp.zeros_like(dk_ref)
            dv_ref[...] = jnp.zeros_like(dv_ref)
            dc_ref[...] = jnp.zeros_like(dc_ref)

        lane_head = _lane_head(Dh)
        sub_h = lax.broadcasted_iota(jnp.int32, (H, 1), 0)
        lane = lax.broadcasted_iota(jnp.int32, (1, LANES), 1)
        g = g_ref[...]
        lsev = lse_ref[...]
        dyv = dy_ref[...].astype(F32)
        dpj_ref[3, rows, :] = (dyv * o_ref[...] * _dsilu(g)).astype(dpj_ref.dtype)
        qs = (_stack_heads(q_ref[...], lane_head, hpb) * scale).astype(BF16)
        dos = _stack_heads(dyv * _silu(g), lane_head, hpb).astype(BF16)
        neg_lse = -jnp.concatenate([_pick_col(lsev, h * Dh, lane) for h in range(hpb)], axis=0)
        nt = (i * TQ + TQ - 1) // TK + 1

        def probs(jt, dsum, masked, r0=0):
            n = TQ - r0
            off = pl.multiple_of(jt * TK, TK)
            kj = k_ref[pl.ds(off, TK), :].astype(BF16)
            vj = v_ref[pl.ds(off, TK), :].astype(BF16)
            ctb = ct_ref[:, pl.ds(off, TK)]
            dos_l = _live_rows(dos, r0, hpb)
            z = _dot(_live_rows(qs, r0, hpb), kj, _NT) + _live_rows(neg_lse, r0, hpb)
            s = jnp.concatenate([z[h * n:(h + 1) * n] - _pick_row(ctb, p * hpb + h, sub_h) for h in range(hpb)],
                                axis=0)
            pr = jnp.exp(s)
            if masked:
                trow, col = _causal_iotas(RS, TK, TQ, r0)
                pr = jnp.where(col + (jt * TK - i * TQ) <= trow, pr, 0.0)
            dp = _dot(dos_l, vj, _NT)
            p_scr[jt] = _put_rows(jnp.zeros((RS, TK), F32), pr, r0, hpb)
            dp_scr[jt] = _put_rows(jnp.zeros((RS, TK), F32), dp, r0, hpb)
            dv_ref[pl.ds(off, TK), :] += _dot(pr.astype(BF16), dos_l, _TN)
            return _put_rows(dsum, _live_rows(dsum, r0, hpb) + jnp.sum(pr * dp, axis=1, keepdims=True), r0, hpb)

        dsum = lax.fori_loop(0, nt - NM, lambda jt, d: probs(jt, d, False), jnp.zeros((RS, 1), F32))
        for m in reversed(range(NM)):
            dsum = probs(nt - 1 - m, dsum, True)

        def grads(jt, acc):
            off = pl.multiple_of(jt * TK, TK)
            kj = k_ref[pl.ds(off, TK), :].astype(BF16)
            ds = p_scr[jt] * (dp_scr[jt] - dsum)
            for h in range(hpb):
                dc_ref[h:h + 1, pl.ds(off, TK)] -= jnp.sum(ds[h * TQ:(h + 1) * TQ], axis=0, keepdims=True)
            dsb = ds.astype(BF16)
            dk_ref[pl.ds(off, TK), :] += _dot(dsb, qs, _TN)
            return acc + _dot(dsb, kj)

        acc = lax.fori_loop(0, nt, grads, jnp.zeros((RS, LANES), F32))
        dpj_ref[0, rows, :] = (_unstack_heads(acc, lane_head, hpb) * scale).astype(dpj_ref.dtype)

        @pl.when(i == NQ - 1)
        def _():
            dpj_ref[1] = dk_ref[...].astype(dpj_ref.dtype)
            dpj_ref[2] = dv_ref[...].astype(dpj_ref.dtype)

    blk = lambda sec: pl.BlockSpec((None, TQ, LANES), lambda b, p, i: (b, i, sec * P + p))
    full = lambda sec: pl.BlockSpec((None, S, LANES), lambda b, p, i: (b, 0, sec * P + p))
    one = pl.BlockSpec((None, TQ, LANES), lambda b, p, i: (b, i, p))
    return pl.pallas_call(
        body, name="fox_bwd", grid=(B, P, NQ),
        in_specs=[blk(0), full(1), full(2), blk(3),
                  pl.BlockSpec((None, H, S), lambda b, p, i: (b, 0, 0)),
                  one, one, one],
        out_specs=[pl.BlockSpec((None, 4, S, LANES), lambda b, p, i: (b, 0, 0, p)),
                   pl.BlockSpec((None, None, hpb, S), lambda b, p, i: (b, p, 0, 0))],
        out_shape=[jax.ShapeDtypeStruct((B, 4, S, W), BF16), jax.ShapeDtypeStruct((B, P, hpb, S), F32)],
        scratch_shapes=[pltpu.VMEM((S, LANES), F32), pltpu.VMEM((S, LANES), F32),
                        pltpu.VMEM((S // TK, RS, TK), F32), pltpu.VMEM((S // TK, RS, TK), F32)],
        compiler_params=_params(("parallel", "parallel", "arbitrary")),
    )(proj3, proj3, proj3, proj3, cum_t, o, lse, dy)


def _layernorm_rows(v, gamma, beta):
    mu = jnp.mean(v, axis=-1, keepdims=True)
    xc = v - mu
    rstd = lax.rsqrt(jnp.mean(xc * xc, axis=-1, keepdims=True) + EPS)
    xh = xc * rstd
    return xh, rstd, xh * gamma + beta


def _layernorm_rows_bwd(dout, xh, rstd, gamma):
    dxh = dout * gamma
    return rstd * (dxh - jnp.mean(dxh, axis=-1, keepdims=True) - xh * jnp.mean(dxh * xh, axis=-1, keepdims=True))


def _gmlp_fwd(proj, wm, bs_t, ln_g, ln_b, W):
    T = proj.shape[0]
    G = wm.shape[0]
    cg = W // G
    assert cg == LANES

    def body(p_ref, wm_ref, bs_ref, lg_ref, lb_ref, y_ref, vn_ref):
        lane = lax.broadcasted_iota(jnp.int32, (1, LANES), 1)
        _, _, vn = _layernorm_rows(_gelu(p_ref[:, W:2 * W]), lg_ref[...], lb_ref[...])
        vn_ref[...] = vn.astype(BF16)
        bs = bs_ref[...]
        for g in range(G):
            sl = pl.ds(g * cg, cg)
            s = _dot(wm_ref[g], vn_ref[:, sl]) + _pick_col(bs, g, lane)
            gate = p_ref[:, pl.ds(2 * W + g * cg, cg)]
            y_ref[:, sl] = (_gelu(p_ref[:, sl]) * s * _silu(gate)).astype(BF16)

    vec = pl.BlockSpec((1, W), lambda r: (0, 0))
    return pl.pallas_call(
        body, name="gmlp_fwd", grid=(T // BLK,),
        in_specs=[pl.BlockSpec((BLK, 3 * W), lambda r: (r, 0)),
                  pl.BlockSpec((G, BLK, BLK), lambda r: (0, 0, 0)),
                  pl.BlockSpec((BLK, LANES), lambda r: (0, 0)), vec, vec],
        out_specs=pl.BlockSpec((BLK, W), lambda r: (r, 0)),
        out_shape=jax.ShapeDtypeStruct((T, W), BF16),
        scratch_shapes=[pltpu.VMEM((BLK, W), BF16)],
        compiler_params=_params(("parallel",)),
    )(proj, wm, bs_t, ln_g, ln_b)


def _gmlp_bwd(proj, dy, wm, bs_t, ln_g, ln_b, W):
    T = proj.shape[0]
    G = wm.shape[0]
    cg = W // G

    def body(p_ref, dy_ref, wm_ref, bs_ref, lg_ref, lb_ref,
             dp_ref, dwm_ref, dbs_ref, dlg_ref, dlb_ref, vn_ref, dvn_ref):
        r = pl.program_id(0)

        @pl.when(r == 0)
        def _():
            dwm_ref[...] = jnp.zeros_like(dwm_ref)
            dbs_ref[...] = jnp.zeros_like(dbs_ref)
            dlg_ref[...] = jnp.zeros_like(dlg_ref)
            dlb_ref[...] = jnp.zeros_like(dlb_ref)

        lane = lax.broadcasted_iota(jnp.int32, (1, LANES), 1)
        vpre = p_ref[:, W:2 * W]
        gamma = lg_ref[...]
        xh, rstd, vn = _layernorm_rows(_gelu(vpre), gamma, lb_ref[...])
        vn_ref[...] = vn.astype(BF16)
        bs = bs_ref[...]
        dbs = jnp.zeros((BLK, LANES), F32)
        for g in range(G):
            sl = pl.ds(g * cg, cg)
            gsl = pl.ds(2 * W + g * cg, cg)
            vng = vn_ref[:, sl]
            s = _dot(wm_ref[g], vng) + _pick_col(bs, g, lane)
            upre = p_ref[:, sl]
            u = _gelu(upre)
            gate = p_ref[:, gsl]
            dyv = dy_ref[:, sl].astype(F32)
            dp_ref[:, gsl] = (dyv * u * s * _dsilu(gate)).astype(dp_ref.dtype)
            do = dyv * _silu(gate)
            dp_ref[:, sl] = (do * s * _dgelu(upre)).astype(dp_ref.dtype)
            ds = do * u
            dbs = dbs + jnp.where(lane == g, jnp.sum(ds, axis=1, keepdims=True), 0.0)
            dsb = ds.astype(BF16)
            dwm_ref[g] += _dot(dsb, vng, _NT)
            dvn_ref[:, sl] = _dot(wm_ref[g], dsb, _TN)
        dbs_ref[...] += dbs
        dvn = dvn_ref[...]
        dlg_ref[...] += jnp.sum(dvn * xh, axis=0, keepdims=True)
        dlb_ref[...] += jnp.sum(dvn, axis=0, keepdims=True)
        dv = _layernorm_rows_bwd(dvn, xh, rstd, gamma)
        dp_ref[:, W:2 * W] = (dv * _dgelu(vpre)).astype(dp_ref.dtype)

    vec = pl.BlockSpec((1, W), lambda r: (0, 0))
    return pl.pallas_call(
        body, name="gmlp_bwd", grid=(T // BLK,),
        in_specs=[pl.BlockSpec((BLK, 3 * W), lambda r: (r, 0)),
                  pl.BlockSpec((BLK, W), lambda r: (r, 0)),
                  pl.BlockSpec((G, BLK, BLK), lambda r: (0, 0, 0)),
                  pl.BlockSpec((BLK, LANES), lambda r: (0, 0)), vec, vec],
        out_specs=[pl.BlockSpec((BLK, 3 * W), lambda r: (r, 0)),
                   pl.BlockSpec((G, BLK, BLK), lambda r: (0, 0, 0)),
                   pl.BlockSpec((BLK, LANES), lambda r: (0, 0)), vec, vec],
        out_shape=[jax.ShapeDtypeStruct((T, 3 * W), BF16), jax.ShapeDtypeStruct((G, BLK, BLK), F32),
                   jax.ShapeDtypeStruct((BLK, LANES), F32),
                   jax.ShapeDtypeStruct((1, W), F32), jax.ShapeDtypeStruct((1, W), F32)],
        scratch_shapes=[pltpu.VMEM((BLK, W), BF16), pltpu.VMEM((BLK, W), F32)],
        compiler_params=_params(("arbitrary",)),
    )(proj, dy, wm, bs_t, ln_g, ln_b)


SUBLANES = 8
SHIFT_ROWS = CONV_HALO + BLK - SUBLANES


def _shift_rows(ext_ref, sh_ref, off):
    for r in range(1, SUBLANES):
        sh_ref[r - 1] = ext_ref[pl.ds(r, SHIFT_ROWS), pl.ds(off, LANES)]


def _rows_from(ext_ref, sh_ref, off, start):
    r = start % SUBLANES
    if r == 0:
        return ext_ref[pl.ds(start, BLK), pl.ds(off, LANES)]
    return sh_ref[r - 1, pl.ds(start - r, BLK), :]


def _conv_taps(ext_ref, sh_ref, cw_ref, off, n_taps, first):
    acc = jnp.zeros((BLK, LANES), F32)
    for k in range(n_taps):
        acc = acc + cw_ref[k:k + 1, pl.ds(off, LANES)] * _rows_from(ext_ref, sh_ref, off, first + k)
    return acc


def _fill_glu_ext(ext_ref, halo_ref, cur_ref, W, first_block):
    y0h = halo_ref[:, :W] * _sigmoid(halo_ref[:, W:])
    ext_ref[0:CONV_HALO, :] = jnp.where(first_block, 0.0, y0h)
    ext_ref[CONV_HALO:CONV_HALO + BLK, :] = cur_ref[:, :W] * _sigmoid(cur_ref[:, W:])


def _conv_specs(S, W):
    per = BLK // CONV_HALO
    cur = pl.BlockSpec((None, BLK, 2 * W), lambda b, i: (b, i, 0))
    halo = pl.BlockSpec((None, CONV_HALO, 2 * W), lambda b, i: (b, jnp.maximum(i * per - 1, 0), 0))
    gate = pl.BlockSpec((None, BLK, W), lambda b, i: (b, i, 2))
    return cur, halo, gate


def _conv_fwd(proj3, cw, cb, ln_g, ln_b, W, exch):
    B, S, _ = proj3.shape
    K = cw.shape[0]
    first = CONV_HALO - (K - 1)
    assert first >= 0

    def body(cur_ref, halo_ref, g_ref, cw_ref, cb_ref, lg_ref, lb_ref, y_ref, ext_ref, y1_ref, sh_ref):
        i = pl.program_id(1)
        _fill_glu_ext(ext_ref, halo_ref, cur_ref, W, i == 0)

        def chan(c, _):
            off = pl.multiple_of(c * LANES, LANES)
            _shift_rows(ext_ref, sh_ref, off)
            y1_ref[:, pl.ds(off, LANES)] = (_conv_taps(ext_ref, sh_ref, cw_ref, off, K, first)
                                            + cb_ref[:, pl.ds(off, LANES)])
            return 0

        lax.fori_loop(0, W // LANES, chan, 0)
        _, _, ln = _layernorm_rows(y1_ref[...], lg_ref[...], lb_ref[...])
        y_ref[...] = (_silu(ln) * _silu(g_ref[...])).astype(BF16)

    cur, halo, gate = _conv_specs(S, W)
    vec = pl.BlockSpec((1, W), lambda b, i: (0, 0))
    (y,), moved = _call_hosting(
        body, exch, "conv_fwd", (B, S // BLK),
        [cur, halo, gate, pl.BlockSpec((K, W), lambda b, i: (0, 0)), vec, vec, vec],
        [pl.BlockSpec((None, BLK, W), lambda b, i: (b, i, 0))], [jax.ShapeDtypeStruct((B, S, W), BF16)],
        [pltpu.VMEM((CONV_HALO + BLK, W), F32), pltpu.VMEM((BLK, W), F32),
         pltpu.VMEM((SUBLANES - 1, SHIFT_ROWS, LANES), F32)],
        (proj3, proj3, proj3, cw, cb, ln_g, ln_b))
    return y, moved


def _conv_bwd1(proj3, dy, cw, cb, ln_g, ln_b, W, exch):
    B, S, _ = proj3.shape
    K = cw.shape[0]
    first = CONV_HALO - (K - 1)

    def body(cur_ref, halo_ref, g_ref, dy_ref, cw_ref, cb_ref, lg_ref, lb_ref,
             dy1_ref, dg_ref, dcw_ref, dcb_ref, dlg_ref, dlb_ref, ext_ref, y1_ref, sh_ref):
        b = pl.program_id(0)
        i = pl.program_id(1)

        @pl.when(jnp.logical_and(b == 0, i == 0))
        def _():
            dcw_ref[...] = jnp.zeros_like(dcw_ref)
            dcb_ref[...] = jnp.zeros_like(dcb_ref)
            dlg_ref[...] = jnp.zeros_like(dlg_ref)
            dlb_ref[...] = jnp.zeros_like(dlb_ref)

        _fill_glu_ext(ext_ref, halo_ref, cur_ref, W, i == 0)

        def chan(c, _):
            off = pl.multiple_of(c * LANES, LANES)
            _shift_rows(ext_ref, sh_ref.at[c], off)
            y1_ref[:, pl.ds(off, LANES)] = (_conv_taps(ext_ref, sh_ref.at[c], cw_ref, off, K, first)
                                            + cb_ref[:, pl.ds(off, LANES)])
            return 0

        lax.fori_loop(0, W // LANES, chan, 0)
        gamma = lg_ref[...]
        xh, rstd, ln = _layernorm_rows(y1_ref[...], gamma, lb_ref[...])
        g = g_ref[...]
        dyv = dy_ref[...].astype(F32)
        dg_ref[...] = (dyv * _silu(ln) * _dsilu(g)).astype(dg_ref.dtype)
        dln = dyv * _silu(g) * _dsilu(ln)
        dlg_ref[...] += jnp.sum(dln * xh, axis=0, keepdims=True)
        dlb_ref[...] += jnp.sum(dln, axis=0, keepdims=True)
        dy1 = _layernorm_rows_bwd(dln, xh, rstd, gamma)
        dy1_ref[...] = dy1
        dcb_ref[...] += jnp.sum(dy1, axis=0, keepdims=True)

        def chan_w(c, _):
            off = pl.multiple_of(c * LANES, LANES)
            d = dy1_ref[:, pl.ds(off, LANES)]
            for k in range(K):
                dcw_ref[k:k + 1, pl.ds(off, LANES)] += jnp.sum(
                    d * _rows_from(ext_ref, sh_ref.at[c], off, first + k), axis=0, keepdims=True)
            return 0

        lax.fori_loop(0, W // LANES, chan_w, 0)

    cur, halo, gate = _conv_specs(S, W)
    vec = pl.BlockSpec((1, W), lambda b, i: (0, 0))
    taps = pl.BlockSpec((K, W), lambda b, i: (0, 0))
    one = pl.BlockSpec((None, BLK, W), lambda b, i: (b, i, 0))
    return _call_hosting(
        body, exch, "conv_bwd1", (B, S // BLK), [cur, halo, gate, one, taps, vec, vec, vec],
        [one, one, taps, vec, vec, vec],
        [jax.ShapeDtypeStruct((B, S, W), F32), jax.ShapeDtypeStruct((B, S, W), BF16),
         jax.ShapeDtypeStruct((K, W), F32)] + [jax.ShapeDtypeStruct((1, W), F32)] * 3,
        [pltpu.VMEM((CONV_HALO + BLK, W), F32), pltpu.VMEM((BLK, W), F32),
         pltpu.VMEM((W // LANES, SUBLANES - 1, SHIFT_ROWS, LANES), F32)],
        (proj3, proj3, proj3, dy, cw, cb, ln_g, ln_b))


def _conv_bwd2(proj3, dy1, dgate, cw_rev, W):
    B, S, _ = proj3.shape
    K = cw_rev.shape[0]
    NQ = S // BLK
    per = BLK // CONV_HALO

    def body(cur_ref, d_ref, dnext_ref, dgate_ref, cw_ref, dp_ref, ext_ref, dy0_ref, sh_ref):
        i = pl.program_id(1)
        ext_ref[0:BLK, :] = d_ref[...]
        ext_ref[BLK:BLK + CONV_HALO, :] = jnp.where(i == NQ - 1, 0.0, dnext_ref[...])

        def chan(c, _):
            off = pl.multiple_of(c * LANES, LANES)
            _shift_rows(ext_ref, sh_ref, off)
            dy0_ref[:, pl.ds(off, LANES)] = _conv_taps(ext_ref, sh_ref, cw_ref, off, K, 0)
            return 0

        lax.fori_loop(0, W // LANES, chan, 0)
        a = cur_ref[:, :W]
        sg = _sigmoid(cur_ref[:, W:])
        dy0 = dy0_ref[...]
        dp_ref[:, 0:W] = (dy0 * sg).astype(dp_ref.dtype)
        dp_ref[:, W:2 * W] = (dy0 * a * sg * (1.0 - sg)).astype(dp_ref.dtype)
        dp_ref[:, 2 * W:3 * W] = dgate_ref[...]

    cur = pl.BlockSpec((None, BLK, 2 * W), lambda b, i: (b, i, 0))
    one = pl.BlockSpec((None, BLK, W), lambda b, i: (b, i, 0))
    nxt = pl.BlockSpec((None, CONV_HALO, W), lambda b, i: (b, jnp.minimum((i + 1) * per, S // CONV_HALO - 1), 0))
    return pl.pallas_call(
        body, name="conv_bwd2", grid=(B, NQ),
        in_specs=[cur, one, nxt, one, pl.BlockSpec((K, W), lambda b, i: (0, 0))],
        out_specs=pl.BlockSpec((None, BLK, 3 * W), lambda b, i: (b, i, 0)),
        out_shape=jax.ShapeDtypeStruct((B, S, 3 * W), BF16),
        scratch_shapes=[pltpu.VMEM((BLK + CONV_HALO, W), F32), pltpu.VMEM((BLK, W), F32),
                        pltpu.VMEM((SUBLANES - 1, SHIFT_ROWS, LANES), F32)],
        compiler_params=_params(("parallel", "parallel")),
    )(proj3, dy1, dy1, dgate, cw_rev)


def _pack(arrays):
    flat = jnp.concatenate([a.astype(F32).reshape(-1) for a in arrays])
    n = flat.shape[0]
    pad = (-n) % (8 * LANES)
    if pad:
        flat = jnp.concatenate([flat, jnp.zeros((pad,), F32)])
    return flat.reshape(-1, LANES)


def _unpack(packed, shapes, lead=()):
    flat = packed.reshape(lead + (-1,))
    out, off = [], 0
    for shp in shapes:
        n = math.prod(shp)
        out.append(flat[..., off:off + n].reshape(lead + tuple(shp)))
        off += n
    return out


def _cols_from_dev(g):
    g = jnp.moveaxis(g, 0, -2)
    return g.reshape(g.shape[:-2] + (g.shape[-2] * g.shape[-1],))


def _my_cols(full, me):
    n8 = full.shape[-1] // N_DEV
    return lax.dynamic_slice_in_dim(full, me * n8, n8, axis=full.ndim - 1)


def kernel(x, a_norm, a_w_in, a_w_out, b_norm, b_w_in, b_v_ln_g, b_v_ln_b, b_w_s, b_b_s, b_w_out, c_norm, c_w_in, c_conv_w, c_conv_b, c_ln_g, c_ln_b, c_w_out, d_norm, d_w_in, d_b_f, d_w_out, final_norm, loss_target, m_a_norm, m_a_w_in, m_a_w_out, m_b_norm, m_b_w_in, m_b_v_ln_g, m_b_v_ln_b, m_b_w_s, m_b_b_s, m_b_w_out, m_c_norm, m_c_w_in, m_c_conv_w, m_c_conv_b, m_c_ln_g, m_c_ln_b, m_c_w_out, m_d_norm, m_d_w_in, m_d_b_f, m_d_w_out, m_final_norm, v_a_norm, v_a_w_in, v_a_w_out, v_b_norm, v_b_w_in, v_b_v_ln_g, v_b_v_ln_b, v_b_w_s, v_b_b_s, v_b_w_out, v_c_norm, v_c_w_in, v_c_conv_w, v_c_conv_b, v_c_ln_g, v_c_ln_b, v_c_w_out, v_d_norm, v_d_w_in, v_d_b_f, v_d_w_out, v_final_norm):
    B, S, D = x.shape
    T = B * S
    xi, yi, ci = _me()
    me = 4 * xi + 2 * yi + ci

    G = b_w_s.shape[1]
    KC = c_conv_w.shape[1]
    H_D = d_b_f.shape[1]
    W_A = a_w_out.shape[1] * N_DEV
    W_B = b_w_out.shape[1] * N_DEV
    W_C = c_w_out.shape[1] * N_DEV
    W_D = d_w_out.shape[1] * N_DEV
    N_D = d_w_in.shape[2] * N_DEV
    N_D_PAD = -(-N_D // (3 * LANES)) * (3 * LANES)

    big_names = ["a_w_in", "a_w_out", "b_w_in", "b_w_out", "c_w_in", "c_w_out", "d_w_in", "d_w_out"]
    big_w = dict(a_w_in=a_w_in[0], a_w_out=a_w_out[0], b_w_in=b_w_in[0], b_w_out=b_w_out[0],
                 c_w_in=c_w_in[0], c_w_out=c_w_out[0], d_w_in=d_w_in[0], d_w_out=d_w_out[0])
    small_sharded = [b_norm, b_v_ln_g, b_v_ln_b, c_norm, c_conv_w, c_conv_b, c_ln_g, c_ln_b, d_norm]
    first_names, later_names, last_names = big_names[:1], big_names[1:6], big_names[6:]
    gathered = _Exchange([big_w[n].astype(BF16) for n in first_names] + [_pack(small_sharded)],
                         ["gather"] * (len(first_names) + 1)).run("gather_first")
    wg = dict(zip(first_names, gathered[:-1]))
    (b_norm_f, b_lg_f, b_lb_f, c_norm_f, c_cw_f, c_cb_f, c_lg_f, c_lb_f, d_norm_f) = [
        _cols_from_dev(t) for t in _unpack(gathered[-1], [s.shape for s in small_sharded], lead=(N_DEV,))]
    c_cw_f = c_cw_f[0]

    wm = jnp.tril(b_w_s[0]).astype(BF16)
    bs_t = jnp.pad(b_b_s[0].T, ((0, 0), (0, LANES - G)))

    x0 = x.reshape(T, D)
    h_a = _rmsnorm_fwd(x0, a_norm, "rms_a")
    proj_a = _mm_w_dev(h_a, wg["a_w_in"], "proj_a").reshape(B, S, 4 * W_A)
    (o_a, y_a), later = _sb_fwd(proj_a, W_A, SB_HEADS,
                                _Exchange([big_w[n].astype(BF16) for n in later_names], ["gather"] * len(later_names)))
    wg.update(zip(later_names, later))
    a_w_out_f = wg["a_w_out"].reshape(W_A, D)
    b_w_out_f = wg["b_w_out"].reshape(W_B, D)
    c_w_out_f = wg["c_w_out"].reshape(W_C, D)
    y_a = y_a.reshape(T, W_A)
    x1 = _mm(y_a, a_w_out_f, "nn", T, D, W_A, F32, "out_a", 512, D, W_A, res=x0)
    h_b = _rmsnorm_fwd(x1, b_norm_f, "rms_b")
    proj_b = _mm_w_dev(h_b, wg["b_w_in"], "proj_b")
    y_b = _gmlp_fwd(proj_b, wm, bs_t, b_lg_f, b_lb_f, W_B)
    x2 = _mm(y_b, b_w_out_f, "nn", T, D, W_B, F32, "out_b", 512, D, W_B, res=x1)
    h_c = _rmsnorm_fwd(x2, c_norm_f, "rms_c")
    proj_c = _mm_w_dev(h_c, wg["c_w_in"], "proj_c").reshape(B, S, 3 * W_C)
    y_c, last = _conv_fwd(proj_c, c_cw_f, c_cb_f, c_lg_f, c_lb_f, W_C,
                          _Exchange([big_w[n].astype(BF16) for n in last_names], ["gather"] * len(last_names)))
    wg.update(zip(last_names, last))
    d_w_out_f = wg["d_w_out"].reshape(W_D, D)
    d_w_in_f = jnp.pad(_cols_from_dev(wg["d_w_in"]), ((0, 0), (0, N_D_PAD - N_D)))
    y_c = y_c.reshape(T, W_C)
    x3 = _mm(y_c, c_w_out_f, "nn", T, D, W_C, F32, "out_c", 512, D, W_C, res=x2)
    h_d = _rmsnorm_fwd(x3, d_norm_f, "rms_d")
    proj_d = _mm(h_d, d_w_in_f, "nn", T, N_D_PAD, D, F32, "proj_d", 1024, 384, D).reshape(B, S, N_D_PAD)
    f_t = jnp.swapaxes(proj_d[:, :, 4 * W_D:4 * W_D + H_D], 1, 2)
    b_f_col = d_b_f.reshape(H_D, 1)
    cum_t = _fox_gate_fwd(f_t, b_f_col)
    o_d, y_d, lse_d = _fox_fwd(proj_d, cum_t, W_D, H_D)
    y_d = y_d.reshape(T, W_D)
    x4 = _mm(y_d, d_w_out_f, "nn", T, D, W_D, F32, "out_d", 512, D, W_D, res=x3)

    loss_part, dx, g_final = _loss_head(x4, final_norm.reshape(1, D), loss_target.reshape(T, D))
    loss = lax.psum(loss_part[0, 0], MESH_AXES)

    dy_d = _mm(dx, d_w_out_f, "nt", T, W_D, D, BF16, "dy_d", 512, W_D, D).reshape(B, S, W_D)
    gw_d_out = _mm(y_d, dx, "tn", W_D, D, T, BF16, "gw_d_out", W_D, D, 512).reshape(N_DEV, W_D // N_DEV, D)
    dproj_d, dcum = _fox_bwd(proj_d, cum_t, o_d, lse_d, dy_d, W_D, H_D)
    df_t, g_b_f = _fox_gate_bwd(dcum.reshape(B, H_D, S), f_t, b_f_col)
    F_PAD = N_D_PAD - 4 * W_D
    df = jnp.pad(jnp.swapaxes(df_t, 1, 2), ((0, 0), (0, 0), (0, F_PAD - H_D))).reshape(T, F_PAD)
    tc, tr = min(512, W_D), min(1024, S)
    gw_main = _mm(h_d, dproj_d, "tn", D, 4 * W_D, T, BF16, "gw_d_in", D, tc, tr,
                  b_spec=_sectioned_spec(dproj_d, tr, tc, 2, 1))
    gw_f = _mm(h_d, df, "tn", D, F_PAD, T, BF16, "gw_d_in_f", D, F_PAD, 512)
    gw_d_in = jnp.moveaxis(
        jnp.concatenate([gw_main, gw_f], axis=1)[:, :N_D].reshape(D, N_DEV, N_D // N_DEV), 1, 0)
    dh = _mm(dproj_d, d_w_in_f, "nt", T, D, 4 * W_D, F32, "dh_d", tr, D, tc,
             a_spec=_sectioned_spec(dproj_d, tr, tc, 0, 2))
    dh = _mm(df, d_w_in_f[:, 4 * W_D:], "nt", T, D, F_PAD, F32, "dh_d_f", 512, D, F_PAD, res=dh)
    dx, g_d_norm = _rmsnorm_bwd(x3, d_norm_f, dh, dx, "rms_bwd_d")

    dy_c = _mm(dx, c_w_out_f, "nt", T, W_C, D, BF16, "dy_c", 512, W_C, D).reshape(B, S, W_C)
    gw_c_out = _mm(y_c, dx, "tn", W_C, D, T, BF16, "gw_c_out", 1024, D, 512).reshape(N_DEV, W_C // N_DEV, D)
    (dy1, dgate_c, g_c_cw, g_c_cb, g_c_lg, g_c_lb), parts_d = _conv_bwd1(
        proj_c, dy_c, c_cw_f, c_cb_f, c_lg_f, c_lb_f, W_C, _Exchange([gw_d_in, gw_d_out], ["scatter"] * 2))
    dproj_c = _conv_bwd2(proj_c, dy1, dgate_c, c_cw_f[::-1], W_C).reshape(T, 3 * W_C)
    gw_c_in = _mm_grad_dev(h_c, dproj_c, "gw_c_in")
    dh = _mm_wT_dev(dproj_c, wg["c_w_in"], "dh_c")
    dx, g_c_norm = _rmsnorm_bwd(x2, c_norm_f, dh, dx, "rms_bwd_c")

    dy_b = _mm(dx, b_w_out_f, "nt", T, W_B, D, BF16, "dy_b", 512, W_B, D)
    gw_b_out = _mm(y_b, dx, "tn", W_B, D, T, BF16, "gw_b_out", 1024, D, 512).reshape(N_DEV, W_B // N_DEV, D)
    dproj_b, g_wm, g_bs_t, g_b_lg, g_b_lb = _gmlp_bwd(proj_b, dy_b, wm, bs_t, b_lg_f, b_lb_f, W_B)
    g_b_w_s = jnp.tril(g_wm)
    g_b_b_s = g_bs_t[:, :G].T
    gw_b_in = _mm_grad_dev(h_b, dproj_b, "gw_b_in")
    dh = _mm_wT_dev(dproj_b, wg["b_w_in"], "dh_b")
    dx, g_b_norm = _rmsnorm_bwd(x1, b_norm_f, dh, dx, "rms_bwd_b")

    dy_a = _mm(dx, a_w_out_f, "nt", T, W_A, D, BF16, "dy_a", 512, W_A, D).reshape(B, S, W_A)
    gw_a_out = _mm(y_a, dx, "tn", W_A, D, T, BF16, "gw_a_out", W_A, D, 512).reshape(N_DEV, W_A // N_DEV, D)
    small_full = [g_b_norm, g_b_lg, g_b_lb, g_b_w_s, g_b_b_s, g_c_norm, g_c_cw, g_c_cb, g_c_lg, g_c_lb,
                  g_d_norm, g_b_f, g_final]
    dproj_a, parts_s = _sb_bwd(
        proj_a, o_a, dy_a, W_A, SB_HEADS,
        _Exchange([gw_c_in, gw_c_out, gw_b_in, gw_b_out, gw_a_out, _pack(small_full)], ["scatter"] * 5 + ["gather"]))
    gw_a_in = _mm_grad_dev(h_a, dproj_a, "gw_a_in")
    dh, parts_a = _mm_wT_dev(dproj_a, wg["a_w_in"], "dh_a", exch=_Exchange([gw_a_in], ["scatter"]))
    dx, g_a_norm = _rmsnorm_bwd(x0, a_norm, dh, dx, "rms_bwd_a")
    grad_x = dx.reshape(B, S, D)

    (parts_n,) = _Exchange([_pack([g_a_norm])], ["gather"]).run("exchange_last")
    big_parts = dict(a_w_in=parts_a[0], a_w_out=parts_s[4], b_w_in=parts_s[2], b_w_out=parts_s[3],
                     c_w_in=parts_s[0], c_w_out=parts_s[1], d_w_in=parts_d[0], d_w_out=parts_d[1])
    (s_b_norm, s_b_lg, s_b_lb, s_b_w_s, s_b_b_s, s_c_norm, s_c_cw, s_c_cb, s_c_lg, s_c_lb,
     s_d_norm, s_b_f, s_final) = _unpack(_sum_parts(parts_s[5], "sum_small"), [g.shape for g in small_full])
    (s_a_norm,) = _unpack(_sum_parts(parts_n, "sum_a_norm"), [g_a_norm.shape])

    weights = dict(a_norm=a_norm, a_w_in=a_w_in, a_w_out=a_w_out, b_norm=b_norm, b_w_in=b_w_in, b_v_ln_g=b_v_ln_g,
                   b_v_ln_b=b_v_ln_b, b_w_s=b_w_s, b_b_s=b_b_s, b_w_out=b_w_out, c_norm=c_norm, c_w_in=c_w_in,
                   c_conv_w=c_conv_w, c_conv_b=c_conv_b, c_ln_g=c_ln_g, c_ln_b=c_ln_b, c_w_out=c_w_out,
                   d_norm=d_norm, d_w_in=d_w_in, d_b_f=d_b_f, d_w_out=d_w_out, final_norm=final_norm)
    mom_m = dict(a_norm=m_a_norm, a_w_in=m_a_w_in, a_w_out=m_a_w_out, b_norm=m_b_norm, b_w_in=m_b_w_in,
                 b_v_ln_g=m_b_v_ln_g, b_v_ln_b=m_b_v_ln_b, b_w_s=m_b_w_s, b_b_s=m_b_b_s, b_w_out=m_b_w_out,
                 c_norm=m_c_norm, c_w_in=m_c_w_in, c_conv_w=m_c_conv_w, c_conv_b=m_c_conv_b, c_ln_g=m_c_ln_g,
                 c_ln_b=m_c_ln_b, c_w_out=m_c_w_out, d_norm=m_d_norm, d_w_in=m_d_w_in, d_b_f=m_d_b_f,
                 d_w_out=m_d_w_out, final_norm=m_final_norm)
    mom_v = dict(a_norm=v_a_norm, a_w_in=v_a_w_in, a_w_out=v_a_w_out, b_norm=v_b_norm, b_w_in=v_b_w_in,
                 b_v_ln_g=v_b_v_ln_g, b_v_ln_b=v_b_v_ln_b, b_w_s=v_b_w_s, b_b_s=v_b_b_s, b_w_out=v_b_w_out,
                 c_norm=v_c_norm, c_w_in=v_c_w_in, c_conv_w=v_c_conv_w, c_conv_b=v_c_conv_b, c_ln_g=v_c_ln_g,
                 c_ln_b=v_c_ln_b, c_w_out=v_c_w_out, d_norm=v_d_norm, d_w_in=v_d_w_in, d_b_f=v_d_b_f,
                 d_w_out=v_d_w_out, final_norm=v_final_norm)
    order = list(weights)
    grads, deltas, new_m, new_v = {}, {}, {}, {}

    for n in big_names:
        part = big_parts[n]
        shp = weights[n].shape
        R, C = shp[1], shp[2]
        res = _adamw(part, weights[n].reshape(R, C), mom_m[n].reshape(R, C), mom_v[n].reshape(R, C), "adamw_" + n)
        grads[n], deltas[n], new_m[n], new_v[n] = [r.reshape(shp) for r in res]

    small_g = dict(
        a_norm=s_a_norm, b_norm=_my_cols(s_b_norm, me), b_v_ln_g=_my_cols(s_b_lg, me),
        b_v_ln_b=_my_cols(s_b_lb, me), b_w_s=s_b_w_s[None], b_b_s=s_b_b_s[None], c_norm=_my_cols(s_c_norm, me),
        c_conv_w=_my_cols(s_c_cw, me)[None], c_conv_b=_my_cols(s_c_cb, me), c_ln_g=_my_cols(s_c_lg, me),
        c_ln_b=_my_cols(s_c_lb, me), d_norm=_my_cols(s_d_norm, me), d_b_f=s_b_f.reshape(1, H_D),
        final_norm=s_final.reshape(D))
    small_names = list(small_g)
    sg_p = _pack([small_g[n] for n in small_names])
    res = _adamw(sg_p[None], _pack([weights[n] for n in small_names]), _pack([mom_m[n] for n in small_names]),
                 _pack([mom_v[n] for n in small_names]), "adamw_small")
    shapes = [weights[n].shape for n in small_names]
    for dst, r in zip((grads, deltas, new_m, new_v), res):
        for n, val in zip(small_names, _unpack(r, shapes)):
            dst[n] = val

    return (loss, grad_x, *[grads[n] for n in order], *[deltas[n] for n in order],
            *[new_m[n] for n in order], *[new_v[n] for n in order])
```

```python
import functools
import math

import jax
import jax.numpy as jnp
from jax import lax
from jax.experimental import pallas as pl
from jax.experimental.pallas import tpu as pltpu

F32 = jnp.float32
BF16 = jnp.bfloat16

EPS = 1e-6
SB_HEADS = 16
CONV_HALO = 32
BLK = 128
ATT_TK = 256
ATT_TQ = 512
ATT_GP = 2
LANES = 128
N_DEV = 8
MESH_AXES = ("x", "y", "c")

ADAM_LR = 0.001
ADAM_B1 = 0.9
ADAM_B2 = 0.999
ADAM_EPS = 1e-08
ADAM_WD = 0.01
ADAM_STEP = 10

VMEM_LIMIT = 56 * 1024 * 1024
NEG_BIG = -1e30

_NN = (((1,), (0,)), ((), ()))
_NT = (((1,), (1,)), ((), ()))
_TN = (((0,), (0,)), ((), ()))


def _dot(a, b, dims=_NN):
    return lax.dot_general(a, b, dims, preferred_element_type=F32)


def _split_dot(x, m):
    hi = x.astype(BF16)
    lo = (x - hi.astype(F32)).astype(BF16)
    return _dot(hi, m) + _dot(lo, m)


def _split3_dot(x, m):
    hi = x.astype(BF16)
    r1 = x - hi.astype(F32)
    mid = r1.astype(BF16)
    lo = (r1 - mid.astype(F32)).astype(BF16)
    return _dot(hi, m) + _dot(mid, m) + _dot(lo, m)


def _params(sem=None):
    kw = dict(vmem_limit_bytes=VMEM_LIMIT)
    if sem is not None:
        kw["dimension_semantics"] = sem
    return pltpu.CompilerParams(**kw)


def _sigmoid(x):
    return jax.nn.sigmoid(x)


def _silu(x):
    return x * _sigmoid(x)


def _dsilu(x):
    s = _sigmoid(x)
    return s * (1.0 + x * (1.0 - s))


_GELU_C = math.sqrt(2.0 / math.pi)


def _gelu(x):
    return 0.5 * x * (1.0 + jnp.tanh(_GELU_C * (x + 0.044715 * x * x * x)))


def _dgelu(x):
    th = jnp.tanh(_GELU_C * (x + 0.044715 * x * x * x))
    return 0.5 * (1.0 + th) + 0.5 * x * (1.0 - th * th) * _GELU_C * (1.0 + 3.0 * 0.044715 * x * x)


def _mm(a, b, mode, M, N, K, out_dtype, name, tm, tn, tk, a_spec=None, b_spec=None, o_spec=None, out_shape=None,
        exch=None, res=None):
    tm, tn, tk = min(tm, M), min(tn, N), min(tk, K)
    assert M % tm == 0 and N % tn == 0 and K % tk == 0, (name, M, N, K, tm, tn, tk)
    nk = K // tk
    dims = {"nn": _NN, "nt": _NT, "tn": _TN}[mode]
    if a_spec is None:
        a_spec = (pl.BlockSpec((tk, tm), lambda i, j, k: (k, i)) if mode == "tn"
                  else pl.BlockSpec((tm, tk), lambda i, j, k: (i, k)))
    if b_spec is None:
        b_spec = (pl.BlockSpec((tn, tk), lambda i, j, k: (j, k)) if mode == "nt"
                  else pl.BlockSpec((tk, tn), lambda i, j, k: (k, j)))
    if o_spec is None:
        o_spec = pl.BlockSpec((tm, tn), lambda i, j, k: (i, j))
    if out_shape is None:
        out_shape = (M, N)

    def body(a_ref, b_ref, *rest):
        res_ref = rest[0] if res is not None else None
        if nk == 1:
            o_ref = rest[-1]
            d = _dot(a_ref[...].astype(BF16), b_ref[...].astype(BF16), dims)
            o_ref[...] = (d if res_ref is None else res_ref[...].astype(F32) + d).astype(o_ref.dtype)
            return
        o_ref, acc_ref = rest[-2:]
        k = pl.program_id(2)

        @pl.when(k == 0)
        def _():
            acc_ref[...] = jnp.zeros_like(acc_ref) if res_ref is None else res_ref[...].astype(F32)

        acc_ref[...] += _dot(a_ref[...].astype(BF16), b_ref[...].astype(BF16), dims)

        @pl.when(k == nk - 1)
        def _():
            o_ref[...] = acc_ref[...].astype(o_ref.dtype)

    in_specs, args = [a_spec, b_spec], (a, b)
    if res is not None:
        in_specs, args = in_specs + [o_spec], args + (res,)
    scratch = [pltpu.VMEM((tm, tn), F32)] if nk > 1 else []
    if exch is None:
        return pl.pallas_call(
            body, name=name, grid=(M // tm, N // tn, nk),
            in_specs=in_specs, out_specs=o_spec,
            out_shape=jax.ShapeDtypeStruct(out_shape, out_dtype),
            scratch_shapes=scratch,
            compiler_params=_params(("parallel", "parallel", "arbitrary")),
        )(*args)
    (out,), moved = _call_hosting(
        body, exch, name, (M // tm, N // tn, nk), in_specs, [o_spec],
        [jax.ShapeDtypeStruct(out_shape, out_dtype)], scratch, args)
    return out, moved


def _mm_w_dev(a, w3, name, out_dtype=F32, tm=1024):
    M, K = a.shape
    n8 = w3.shape[2]
    tn = n8 if n8 <= 768 else 512
    per = n8 // tn
    b_spec = pl.BlockSpec((None, K, tn), lambda i, j, k: (j // per, 0, j % per))
    return _mm(a, w3, "nn", M, N_DEV * n8, K, out_dtype, name, tm, tn, K, b_spec=b_spec)


def _sectioned_spec(d4, t_rows, t_cols, rows_axis, cols_axis):
    _, _, S, W = d4.shape
    assert S % t_rows == 0 and W % t_cols == 0
    rb, cb = S // t_rows, W // t_cols

    def index(*g):
        r, c = g[rows_axis], g[cols_axis]
        return (r // rb, c // cb, r % rb, c % cb)

    return pl.BlockSpec((None, None, t_rows, t_cols), index)


def _mm_wT_dev(a, w3, name, out_dtype=F32, tm=1024, exch=None):
    K, n8 = w3.shape[1], w3.shape[2]
    tk = n8 if n8 <= 768 else 512
    per = n8 // tk
    b_spec = pl.BlockSpec((None, K, tk), lambda i, j, k: (k // per, 0, k % per))
    if a.ndim == 4:
        M, N = a.shape[0] * a.shape[2], a.shape[1] * a.shape[3]
        tm = min(tm, a.shape[2])
        a_spec = _sectioned_spec(a, tm, tk, 0, 2)
    else:
        (M, N), a_spec = a.shape, None
    return _mm(a, w3, "nt", M, K, N, out_dtype, name, tm, K, tk, a_spec=a_spec, b_spec=b_spec, exch=exch)


def _mm_grad_dev(h, d, name, out_dtype=BF16):
    T, M = h.shape
    N = d.shape[1] * d.shape[3] if d.ndim == 4 else d.shape[1]
    n8 = N // N_DEV
    tn = n8 if n8 <= 768 else 512
    per = n8 // tn
    tm = min(M, 1024)
    o_spec = pl.BlockSpec((None, tm, tn), lambda i, j, k: (j // per, i, j % per))
    tk = min(1024, d.shape[2] if d.ndim == 4 else T)
    b_spec = _sectioned_spec(d, tk, tn, 2, 1) if d.ndim == 4 else None
    return _mm(h, d, "tn", M, N, T, out_dtype, name, tm, tn, tk, b_spec=b_spec, o_spec=o_spec,
               out_shape=(N_DEV, M, n8))


def _me():
    x, y, c = lax.axis_index("x"), lax.axis_index("y"), lax.axis_index("c")
    return x, y, c


def _peer(r):
    x, y, c = _me()
    px = 1 - x if (r >> 2) & 1 else x
    py = 1 - y if (r >> 1) & 1 else y
    pc = 1 - c if r & 1 else c
    return (px, py, pc), 4 * px + 2 * py + pc


class _Exchange:
    def __init__(self, arrays, kinds):
        self.arrays, self.kinds, self.n = list(arrays), list(kinds), len(arrays)
        self.out_shapes = [
            jax.ShapeDtypeStruct((N_DEV,) + a.shape if kind == "gather" else a.shape, a.dtype)
            for a, kind in zip(arrays, kinds)]
        self.specs = [pl.BlockSpec(memory_space=pl.ANY)] * self.n
        self.sems = [pltpu.SemaphoreType.DMA((self.n, N_DEV - 1)), pltpu.SemaphoreType.DMA((self.n, N_DEV - 1)),
                     pltpu.SemaphoreType.DMA((self.n,))]

    def _copies(self, ins, outs, sems, receiving):
        send_sems, recv_sems, local_sems = sems
        x, y, c = _me()
        me = 4 * x + 2 * y + c

        def src(k, pid):
            return ins[k] if self.kinds[k] == "gather" else ins[k].at[pid]

        local = [pltpu.make_async_copy(src(k, me), outs[k].at[me], local_sems.at[k]) for k in range(self.n)]
        remote = []
        for r in range(1, N_DEV):
            peer, pid = _peer(r)
            for k in range(self.n):
                remote.append(pltpu.make_async_remote_copy(
                    src_ref=src(k, pid), dst_ref=outs[k].at[pid if receiving else me],
                    send_sem=send_sems.at[k, r - 1], recv_sem=recv_sems.at[k, r - 1],
                    device_id=peer, device_id_type=pl.DeviceIdType.MESH))
        return local, remote

    def start(self, ins, outs, sems):
        local, remote = self._copies(ins, outs, sems, False)
        for cp in local + remote:
            cp.start()

    def wait(self, ins, outs, sems):
        local, remote = self._copies(ins, outs, sems, True)
        for cp in remote:
            cp.wait_recv()
        for cp in remote:
            cp.wait_send()
        for cp in local:
            cp.wait()

    def run(self, name):
        n = self.n

        def body(*refs):
            ins, outs, sems = refs[:n], refs[n:2 * n], refs[2 * n:]
            self.start(ins, outs, sems)
            self.wait(ins, outs, sems)

        return pl.pallas_call(
            body, name=name, in_specs=self.specs, out_specs=self.specs, out_shape=self.out_shapes,
            scratch_shapes=self.sems,
        )(*self.arrays)


class _GatherViaSibling(_Exchange):
    ICI = (2, 4, 6)

    def __init__(self, arrays):
        super().__init__(arrays, ["gather"] * len(arrays))

    def _copy(self, ins, outs, sems, k, column, block, to, from_input=False):
        return pltpu.make_async_remote_copy(
            src_ref=ins[k] if from_input else outs[k].at[block], dst_ref=outs[k].at[block],
            send_sem=sems[0].at[k, column], recv_sem=sems[1].at[k, column],
            device_id=to, device_id_type=pl.DeviceIdType.MESH)

    def start(self, ins, outs, sems):
        x, y, c = _me()
        me = 4 * x + 2 * y + c
        for k in range(self.n):
            pltpu.make_async_copy(ins[k], outs[k].at[me], sems[2].at[k]).start()
            self._copy(ins, outs, sems, k, 0, me, _peer(1)[0], True).start()
            for j, r in enumerate(self.ICI):
                self._copy(ins, outs, sems, k, 1 + j, me, _peer(r)[0], True).start()

    def wait(self, ins, outs, sems):
        x, y, c = _me()
        me = 4 * x + 2 * y + c
        sibling, sibling_id = _peer(1)
        for j, r in enumerate(self.ICI):
            peer, pid = _peer(r)
            for k in range(self.n):
                self._copy(ins, outs, sems, k, 1 + j, pid, peer).wait_recv()
                self._copy(ins, outs, sems, k, 4 + j, pid, sibling).start()
        for k in range(self.n):
            self._copy(ins, outs, sems, k, 0, sibling_id, sibling).wait_recv()
            for j, r in enumerate(self.ICI):
                self._copy(ins, outs, sems, k, 4 + j, _peer(r ^ 1)[1], sibling).wait_recv()
            for column in range(N_DEV - 1):
                self._copy(ins, outs, sems, k, column, me, sibling).wait_send()
            pltpu.make_async_copy(ins[k], outs[k].at[me], sems[2].at[k]).wait()


def _call_hosting(body, exch, name, grid, in_specs, out_specs, out_shape, scratch_shapes, args):
    if exch is None:
        res = pl.pallas_call(
            body, name=name, grid=grid, in_specs=list(in_specs), out_specs=list(out_specs),
            out_shape=list(out_shape), scratch_shapes=list(scratch_shapes),
            compiler_params=_params(("arbitrary",) * len(grid)))(*args)
        return res, []
    n_in, n_out, n_scr, nc = len(in_specs), len(out_specs), len(scratch_shapes), exch.n

    def full_body(*refs):
        ins, refs = refs[:n_in], refs[n_in:]
        cins, refs = refs[:nc], refs[nc:]
        outs, refs = refs[:n_out], refs[n_out:]
        couts, refs = refs[:nc], refs[nc:]
        scr, sems = refs[:n_scr], refs[n_scr:]
        ids = [pl.program_id(a) for a in range(len(grid))]
        first = functools.reduce(jnp.logical_and, [i == 0 for i in ids])
        last = functools.reduce(jnp.logical_and, [i == g - 1 for i, g in zip(ids, grid)])

        @pl.when(first)
        def _():
            exch.start(cins, couts, sems)

        body(*ins, *outs, *scr)

        @pl.when(last)
        def _():
            exch.wait(cins, couts, sems)

    res = pl.pallas_call(
        full_body, name=name, grid=grid,
        in_specs=list(in_specs) + exch.specs, out_specs=list(out_specs) + exch.specs,
        out_shape=list(out_shape) + exch.out_shapes,
        scratch_shapes=list(scratch_shapes) + exch.sems,
        compiler_params=_params(("arbitrary",) * len(grid)),
    )(*args, *exch.arrays)
    return res[:n_out], res[n_out:]


def _rmsnorm_fwd(x, g, name):
    T, D = x.shape
    tr = min(256, T)

    def body(x_ref, g_ref, h_ref):
        xv = x_ref[...]
        r = lax.rsqrt(jnp.mean(xv * xv, axis=-1, keepdims=True) + EPS)
        h_ref[...] = (xv * r * g_ref[...]).astype(BF16)

    return pl.pallas_call(
        body, name=name, grid=(T // tr,),
        in_specs=[pl.BlockSpec((tr, D), lambda i: (i, 0)), pl.BlockSpec((1, D), lambda i: (0, 0))],
        out_specs=pl.BlockSpec((tr, D), lambda i: (i, 0)),
        out_shape=jax.ShapeDtypeStruct((T, D), BF16),
        compiler_params=_params(("parallel",)),
    )(x, g)


def _rmsnorm_bwd(x, g, dh, dres, name):
    T, D = x.shape
    tr = min(256, T)

    def body(x_ref, g_ref, dh_ref, dres_ref, dx_ref, dg_ref):
        i = pl.program_id(0)
        xv = x_ref[...]
        r = lax.rsqrt(jnp.mean(xv * xv, axis=-1, keepdims=True) + EPS)
        xh = xv * r
        dhv = dh_ref[...]
        dxh = dhv * g_ref[...]
        dx_ref[...] = dres_ref[...] + r * (dxh - xh * jnp.mean(dxh * xh, axis=-1, keepdims=True))

        @pl.when(i == 0)
        def _():
            dg_ref[...] = jnp.zeros_like(dg_ref)

        dg_ref[...] += jnp.sum(dhv * xh, axis=0, keepdims=True)

    row = pl.BlockSpec((tr, D), lambda i: (i, 0))
    vec = pl.BlockSpec((1, D), lambda i: (0, 0))
    return pl.pallas_call(
        body, name=name, grid=(T // tr,),
        in_specs=[row, vec, row, row], out_specs=[row, vec],
        out_shape=[jax.ShapeDtypeStruct((T, D), F32), jax.ShapeDtypeStruct((1, D), F32)],
        compiler_params=_params(("arbitrary",)),
    )(x, g, dh, dres)


def _loss_head(x, g, target):
    T, D = x.shape
    tr = min(256, T)

    def body(x_ref, g_ref, t_ref, loss_ref, dx_ref, dg_ref):
        i = pl.program_id(0)
        xv = x_ref[...]
        gv = g_ref[...]
        r = lax.rsqrt(jnp.mean(xv * xv, axis=-1, keepdims=True) + EPS)
        xh = xv * r
        diff = xh * gv - t_ref[...]
        dy = diff * (1.0 / D)
        dxh = dy * gv
        dx_ref[...] = r * (dxh - xh * jnp.mean(dxh * xh, axis=-1, keepdims=True))

        @pl.when(i == 0)
        def _():
            dg_ref[...] = jnp.zeros_like(dg_ref)
            loss_ref[...] = jnp.zeros_like(loss_ref)

        dg_ref[...] += jnp.sum(dy * xh, axis=0, keepdims=True)
        part = jnp.sum(jnp.sum(diff * diff, axis=1, keepdims=True), axis=0, keepdims=True)
        loss_ref[...] += (0.5 / D) * part

    row = pl.BlockSpec((tr, D), lambda i: (i, 0))
    vec = pl.BlockSpec((1, D), lambda i: (0, 0))
    return pl.pallas_call(
        body, name="loss_head", grid=(T // tr,),
        in_specs=[row, vec, row],
        out_specs=[pl.BlockSpec((1, 1), lambda i: (0, 0)), row, vec],
        out_shape=[jax.ShapeDtypeStruct((1, 1), F32), jax.ShapeDtypeStruct((T, D), F32),
                   jax.ShapeDtypeStruct((1, D), F32)],
        compiler_params=_params(("arbitrary",)),
    )(x, g, target)


ELEMS_PER_STEP = 1 << 20


def _row_tile(R, per_row):
    best = None
    for tr in range(8, R + 1, 8):
        if R % tr == 0 and tr * per_row <= ELEMS_PER_STEP:
            best = tr
    return best if best is not None else R


def _adamw(parts, w, m, v, name):
    P, R, C = parts.shape
    tr = _row_tile(R, P * C)

    def body(p_ref, w_ref, m_ref, v_ref, g_out, d_out, m_out, v_out):
        g = p_ref[0].astype(F32)
        for p in range(1, P):
            g = g + p_ref[p].astype(F32)
        wv = w_ref[...]
        mn = ADAM_B1 * m_ref[...] + (1.0 - ADAM_B1) * g
        vn = ADAM_B2 * v_ref[...] + (1.0 - ADAM_B2) * (g * g)
        m_hat = mn / (1.0 - ADAM_B1 ** ADAM_STEP)
        v_hat = vn / (1.0 - ADAM_B2 ** ADAM_STEP)
        g_out[...] = g
        d_out[...] = -ADAM_LR * (m_hat / (jnp.sqrt(v_hat) + ADAM_EPS) + ADAM_WD * wv)
        m_out[...] = mn
        v_out[...] = vn

    row = pl.BlockSpec((tr, C), lambda i: (i, 0))
    return pl.pallas_call(
        body, name=name, grid=(R // tr,),
        in_specs=[pl.BlockSpec((P, tr, C), lambda i: (0, i, 0)), row, row, row],
        out_specs=[row, row, row, row],
        out_shape=[jax.ShapeDtypeStruct((R, C), F32)] * 4,
        compiler_params=_params(("parallel",)),
    )(parts, w, m, v)


def _sum_parts(parts, name):
    P, R, C = parts.shape
    tr = _row_tile(R, P * C)

    def body(p_ref, o_ref):
        g = p_ref[0]
        for p in range(1, P):
            g = g + p_ref[p]
        o_ref[...] = g

    return pl.pallas_call(
        body, name=name, grid=(R // tr,),
        in_specs=[pl.BlockSpec((P, tr, C), lambda i: (0, i, 0))],
        out_specs=pl.BlockSpec((tr, C), lambda i: (i, 0)),
        out_shape=jax.ShapeDtypeStruct((R, C), F32),
        compiler_params=_params(("parallel",)),
    )(parts)


def _lane_head(Dh):
    assert Dh & (Dh - 1) == 0 and Dh <= LANES
    return lax.shift_right_logical(lax.broadcasted_iota(jnp.int32, (1, LANES), 1), Dh.bit_length() - 1)


def _stack_heads(x, lane_head, hpb):
    return jnp.concatenate([jnp.where(lane_head == h, x, 0.0) for h in range(hpb)], axis=0)


def _unstack_heads(acc, lane_head, hpb):
    TQ = acc.shape[0] // hpb
    out = acc[0:TQ]
    for h in range(1, hpb):
        out = jnp.where(lane_head == h, acc[h * TQ:(h + 1) * TQ], out)
    return out


def _live_rows(x, r0, hpb):
    if r0 == 0:
        return x
    TQ = x.shape[0] // hpb
    return jnp.concatenate([x[h * TQ + r0:(h + 1) * TQ] for h in range(hpb)], axis=0)


def _put_rows(full, part, r0, hpb):
    if r0 == 0:
        return part
    TQ = full.shape[0] // hpb
    n = TQ - r0
    return jnp.concatenate(
        [blk for h in range(hpb) for blk in (full[h * TQ:h * TQ + r0], part[h * n:(h + 1) * n])], axis=0)


def _first_live_row(m, TQ, TK):
    return max(0, TQ - (m + 1) * TK)


def _key_tile(S):
    return ATT_TK if S % ATT_TK == 0 else BLK


def _query_tile(S):
    return ATT_TQ if S % ATT_TQ == 0 else BLK


def _lane_groups(P):
    return ATT_GP if P % ATT_GP == 0 else 1


def _lanes(u):
    return slice(u * LANES, (u + 1) * LANES)


def _causal_iotas(RS, TK, TQ, r0=0):
    n = TQ - r0
    assert n & (n - 1) == 0 and (TK % TQ == 0 or TQ % TK == 0)
    rows = RS // TQ * n
    trow = jnp.bitwise_and(lax.broadcasted_iota(jnp.int32, (rows, TK), 0), n - 1) + r0
    col = lax.broadcasted_iota(jnp.int32, (rows, TK), 1)
    return trow, col


def _tri(TK, op):
    r = lax.broadcasted_iota(jnp.int32, (TK, TK), 0)
    c = lax.broadcasted_iota(jnp.int32, (TK, TK), 1)
    return op(r, c).astype(BF16)


def _logsig_parts(z):
    lb = jnp.minimum(z, 0.0) - jnp.log(1.0 + jnp.exp(-jnp.abs(z)))
    return lb, lb - z


def _sb_fwd(proj3, W, heads, exch):
    B, S, _ = proj3.shape
    Dh = W // heads
    hpb = LANES // Dh
    P, TQ = W // LANES, _query_tile(S)
    NQ = S // TQ
    scale = 1.0 / math.sqrt(Dh)

    TK = _key_tile(S)
    RS = hpb * TQ
    NM = max(1, TQ // TK)
    GP = _lane_groups(P)
    PG = P // GP

    def body(q_ref, k_ref, v_ref, g_ref, o_ref, y_ref):
        i = pl.program_id(2)
        lane_head = _lane_head(Dh)
        msuf = _tri(TK, lambda r, c: r > c)
        qs = [(_stack_heads(q_ref[:, _lanes(u)], lane_head, hpb) * scale).astype(BF16) for u in range(GP)]
        nt = (i * TQ + TQ - 2) // TK + 1

        def tile(jt, carry, masked, r0=0):
            off = pl.multiple_of(jt * TK, TK)
            if masked:
                trow, col = _causal_iotas(RS, TK, TQ, r0)
                msk = col + (jt * TK - i * TQ) < trow
            out = []
            for u, (rem_all, acc_all) in enumerate(carry):
                rem, acc = _live_rows(rem_all, r0, hpb), _live_rows(acc_all, r0, hpb)
                kj = k_ref[pl.ds(off, TK), _lanes(u)].astype(BF16)
                vj = v_ref[pl.ds(off, TK), _lanes(u)].astype(BF16)
                lb, lr = _logsig_parts(_dot(_live_rows(qs[u], r0, hpb), kj, _NT))
                if masked:
                    lr = jnp.where(msk, lr, 0.0)
                w = jnp.exp(lb + _split_dot(lr, msuf) + rem)
                if masked:
                    w = jnp.where(msk, w, 0.0)
                out.append((_put_rows(rem_all, rem + jnp.sum(lr, axis=1, keepdims=True), r0, hpb),
                            _put_rows(acc_all, acc + _dot(w.astype(BF16), vj), r0, hpb)))
            return tuple(out)

        zero = (jnp.zeros((RS, 1), F32), jnp.zeros((RS, LANES), F32))
        carry = (zero,) * GP
        for m in range(NM):
            carry = tile(nt - 1 - m, carry, True, _first_live_row(m, TQ, TK))
        carry = lax.fori_loop(NM, nt, lambda jj, c: tile(nt - 1 - jj, c, False), carry)
        for u in range(GP):
            o = _unstack_heads(carry[u][1], lane_head, hpb)
            o_ref[:, _lanes(u)] = o
            y_ref[:, _lanes(u)] = (o * _silu(g_ref[:, _lanes(u)])).astype(BF16)

    LW = GP * LANES
    blk = lambda sec: pl.BlockSpec((None, TQ, LW), lambda b, p, i: (b, i, sec * PG + p))
    full = lambda sec: pl.BlockSpec((None, S, LW), lambda b, p, i: (b, 0, sec * PG + p))
    out = pl.BlockSpec((None, TQ, LW), lambda b, p, i: (b, i, p))
    return _call_hosting(
        body, exch, "sb_fwd", (B, PG, NQ), [blk(0), full(1), full(2), blk(3)], [out, out],
        [jax.ShapeDtypeStruct((B, S, W), F32), jax.ShapeDtypeStruct((B, S, W), BF16)], [],
        (proj3, proj3, proj3, proj3))


def _sb_bwd(proj3, o, dy, W, heads, exch):
    B, S, _ = proj3.shape
    Dh = W // heads
    hpb = LANES // Dh
    P, TQ = W // LANES, _query_tile(S)
    NQ = S // TQ
    scale = 1.0 / math.sqrt(Dh)

    TK = _key_tile(S)
    RS = hpb * TQ
    NM = max(1, TQ // TK)

    def body(q_ref, k_ref, v_ref, g_ref, o_ref, dy_ref, dp_ref, dk_ref, dv_ref, u_ref, sig_ref, es_ref):
        i = pl.program_id(2)
        rows = pl.ds(pl.multiple_of(i * TQ, TQ), TQ)

        @pl.when(i == 0)
        def _():
            dk_ref[...] = jnp.zeros_like(dk_ref)
            dv_ref[...] = jnp.zeros_like(dv_ref)

        lane_head = _lane_head(Dh)
        msuf = _tri(TK, lambda r, c: r > c)
        mpre = _tri(TK, lambda r, c: r < c)
        g = g_ref[...]
        dyv = dy_ref[...].astype(F32)
        dp_ref[3, rows, :] = (dyv * o_ref[...] * _dsilu(g)).astype(dp_ref.dtype)
        qs = (_stack_heads(q_ref[...], lane_head, hpb) * scale).astype(BF16)
        dos = _stack_heads(dyv * _silu(g), lane_head, hpb).astype(BF16)
        nt = (i * TQ + TQ - 2) // TK + 1

        def weights(jt, rem_all, masked, r0=0):
            off = pl.multiple_of(jt * TK, TK)
            kj = k_ref[pl.ds(off, TK), :].astype(BF16)
            vj = v_ref[pl.ds(off, TK), :].astype(BF16)
            dos_l = _live_rows(dos, r0, hpb)
            lb, lr = _logsig_parts(_dot(_live_rows(qs, r0, hpb), kj, _NT))
            if masked:
                trow, col = _causal_iotas(RS, TK, TQ, r0)
                msk = col + (jt * TK - i * TQ) < trow
                lr = jnp.where(msk, lr, 0.0)
            w = jnp.exp(lb + _split_dot(lr, msuf) + _live_rows(rem_all, r0, hpb))
            if masked:
                w = jnp.where(msk, w, 0.0)
            e = w * _dot(dos_l, vj, _NT)
            sig = jnp.exp(lb)
            u = e * (1.0 - sig) - _split_dot(e, mpre) * sig
            if masked:
                u = jnp.where(msk, u, 0.0)
                sig = jnp.where(msk, sig, 0.0)
            u_ref[jt] = _put_rows(jnp.zeros((RS, TK), F32), u, r0, hpb)
            sig_ref[jt] = _put_rows(jnp.zeros((RS, TK), F32), sig, r0, hpb)
            es_ref[jt] = _put_rows(jnp.zeros((RS, 1), F32), jnp.sum(e, axis=1, keepdims=True), r0, hpb)
            dv_ref[pl.ds(off, TK), :] += _dot(w.astype(BF16), dos_l, _TN)
            return _put_rows(rem_all, _live_rows(rem_all, r0, hpb) + jnp.sum(lr, axis=1, keepdims=True), r0, hpb)

        rem = jnp.zeros((RS, 1), F32)
        for m in range(NM):
            rem = weights(nt - 1 - m, rem, True, _first_live_row(m, TQ, TK))
        lax.fori_loop(NM, nt, lambda jj, r: weights(nt - 1 - jj, r, False), rem)

        def grads(jt, carry):
            pre, acc = carry
            off = pl.multiple_of(jt * TK, TK)
            kj = k_ref[pl.ds(off, TK), :].astype(BF16)
            dz = (u_ref[jt] - pre * sig_ref[jt]).astype(BF16)
            dk_ref[pl.ds(off, TK), :] += _dot(dz, qs, _TN)
            return pre + es_ref[jt], acc + _dot(dz, kj)

        _, acc = lax.fori_loop(0, nt, grads, (jnp.zeros((RS, 1), F32), jnp.zeros((RS, LANES), F32)))
        dp_ref[0, rows, :] = (_unstack_heads(acc, lane_head, hpb) * scale).astype(dp_ref.dtype)

        @pl.when(i == NQ - 1)
        def _():
            dp_ref[1] = dk_ref[...].astype(dp_ref.dtype)
            dp_ref[2] = dv_ref[...].astype(dp_ref.dtype)

    blk = lambda sec: pl.BlockSpec((None, TQ, LANES), lambda b, p, i: (b, i, sec * P + p))
    full = lambda sec: pl.BlockSpec((None, S, LANES), lambda b, p, i: (b, 0, sec * P + p))
    one = pl.BlockSpec((None, TQ, LANES), lambda b, p, i: (b, i, p))
    (dproj,), moved = _call_hosting(
        body, exch, "sb_bwd", (B, P, NQ), [blk(0), full(1), full(2), blk(3), one, one],
        [pl.BlockSpec((None, 4, S, LANES), lambda b, p, i: (b, 0, 0, p))],
        [jax.ShapeDtypeStruct((B, 4, S, W), BF16)],
        [pltpu.VMEM((S, LANES), F32), pltpu.VMEM((S, LANES), F32),
         pltpu.VMEM((S // TK, RS, TK), F32), pltpu.VMEM((S // TK, RS, TK), F32), pltpu.VMEM((S // TK, RS, 1), F32)],
        (proj3, proj3, proj3, proj3, o, dy))
    return dproj, moved


def _fox_gate_fwd(f_t, b_f):
    B, H, S = f_t.shape

    def body(f_ref, b_ref, c_ref):
        row = lax.broadcasted_iota(jnp.int32, (BLK, BLK), 0)
        col = lax.broadcasted_iota(jnp.int32, (BLK, BLK), 1)
        mpre = (row <= col).astype(BF16)
        carry = jnp.zeros((H, 1), F32)
        for n in range(S // BLK):
            sl = pl.ds(n * BLK, BLK)
            lf, _ = _logsig_parts(f_ref[:, sl] + b_ref[...])
            c_ref[:, sl] = _split3_dot(lf, mpre) + carry
            carry = carry + jnp.sum(lf, axis=1, keepdims=True)

    spec = pl.BlockSpec((None, H, S), lambda b: (b, 0, 0))
    return pl.pallas_call(
        body, name="fox_gate_fwd", grid=(B,),
        in_specs=[spec, pl.BlockSpec((H, 1), lambda b: (0, 0))], out_specs=spec,
        out_shape=jax.ShapeDtypeStruct((B, H, S), F32),
        compiler_params=_params(("parallel",)),
    )(f_t, b_f)


def _fox_gate_bwd(dcum_t, f_t, b_f):
    B, H, S = f_t.shape

    def body(d_ref, f_ref, b_ref, df_ref, db_ref):
        b = pl.program_id(0)

        @pl.when(b == 0)
        def _():
            db_ref[...] = jnp.zeros_like(db_ref)

        row = lax.broadcasted_iota(jnp.int32, (BLK, BLK), 0)
        col = lax.broadcasted_iota(jnp.int32, (BLK, BLK), 1)
        msuf = (row >= col).astype(BF16)
        carry = jnp.zeros((H, 1), F32)
        dbacc = jnp.zeros((H, 1), F32)
        for n in reversed(range(S // BLK)):
            sl = pl.ds(n * BLK, BLK)
            dv = d_ref[:, sl]
            dlf = _split3_dot(dv, msuf) + carry
            carry = carry + jnp.sum(dv, axis=1, keepdims=True)
            df = dlf * _sigmoid(-(f_ref[:, sl] + b_ref[...]))
            df_ref[:, sl] = df
            dbacc = dbacc + jnp.sum(df, axis=1, keepdims=True)
        db_ref[...] += dbacc

    spec = pl.BlockSpec((None, H, S), lambda b: (b, 0, 0))
    vec = pl.BlockSpec((H, 1), lambda b: (0, 0))
    return pl.pallas_call(
        body, name="fox_gate_bwd", grid=(B,),
        in_specs=[spec, spec, vec], out_specs=[spec, vec],
        out_shape=[jax.ShapeDtypeStruct((B, H, S), F32), jax.ShapeDtypeStruct((H, 1), F32)],
        compiler_params=_params(("arbitrary",)),
    )(dcum_t, f_t, b_f)


def _pick_col(block, idx, lane_iota):
    return jnp.sum(jnp.where(lane_iota == idx, block, 0.0), axis=1, keepdims=True)


def _pick_row(block, idx, sub_iota):
    return jnp.sum(jnp.where(sub_iota == idx, block, 0.0), axis=0, keepdims=True)


def _fox_fwd(proj3, cum_t, W, heads):
    B, S, _ = proj3.shape
    H = heads
    Dh = W // heads
    hpb = LANES // Dh
    P, TQ = W // LANES, _query_tile(S)
    NQ = S // TQ
    scale = 1.0 / math.sqrt(Dh)

    TK = _key_tile(S)
    RS = hpb * TQ
    NM = max(1, TQ // TK)

    def body(q_ref, k_ref, v_ref, g_ref, ct_ref, o_ref, y_ref, lse_ref):
        p = pl.program_id(1)
        i = pl.program_id(2)
        lane_head = _lane_head(Dh)
        sub_h = lax.broadcasted_iota(jnp.int32, (H, 1), 0)
        qs = (_stack_heads(q_ref[...], lane_head, hpb) * scale).astype(BF16)
        nt = (i * TQ + TQ - 1) // TK + 1

        trow, col = _causal_iotas(RS, TK, TQ)

        def tile(jt, carry, masked):
            mx, l, acc = carry
            off = pl.multiple_of(jt * TK, TK)
            kj = k_ref[pl.ds(off, TK), :].astype(BF16)
            vj = v_ref[pl.ds(off, TK), :].astype(BF16)
            ctb = ct_ref[:, pl.ds(off, TK)]
            z = _dot(qs, kj, _NT)
            s = jnp.concatenate([z[h * TQ:(h + 1) * TQ] - _pick_row(ctb, p * hpb + h, sub_h) for h in range(hpb)],
                                axis=0)
            if masked:
                s = jnp.where(col + (jt * TK - i * TQ) <= trow, s, NEG_BIG)
            mx2 = jnp.maximum(mx, jnp.max(s, axis=1, keepdims=True))
            pe = jnp.exp(s - mx2)
            alpha = jnp.exp(mx - mx2)
            return (mx2, alpha * l + jnp.sum(pe, axis=1, keepdims=True), alpha * acc + _dot(pe.astype(BF16), vj))

        carry = lax.fori_loop(
            0, nt - NM, lambda jt, c: tile(jt, c, False),
            (jnp.full((RS, 1), NEG_BIG, F32), jnp.zeros((RS, 1), F32), jnp.zeros((RS, LANES), F32)))
        for m in reversed(range(NM)):
            carry = tile(nt - 1 - m, carry, True)
        mx, l, acc = carry
        o = _unstack_heads(acc / l, lane_head, hpb)
        o_ref[...] = o
        lse_ref[...] = _unstack_heads(jnp.broadcast_to(mx + jnp.log(l), (RS, LANES)), lane_head, hpb)
        y_ref[...] = (o * _silu(g_ref[...])).astype(BF16)

    blk = lambda sec: pl.BlockSpec((None, TQ, LANES), lambda b, p, i: (b, i, sec * P + p))
    full = lambda sec: pl.BlockSpec((None, S, LANES), lambda b, p, i: (b, 0, sec * P + p))
    out = pl.BlockSpec((None, TQ, LANES), lambda b, p, i: (b, i, p))
    return pl.pallas_call(
        body, name="fox_fwd", grid=(B, P, NQ),
        in_specs=[blk(0), full(1), full(2), blk(3),
                  pl.BlockSpec((None, H, S), lambda b, p, i: (b, 0, 0))],
        out_specs=[out, out, out],
        out_shape=[jax.ShapeDtypeStruct((B, S, W), F32), jax.ShapeDtypeStruct((B, S, W), BF16),
                   jax.ShapeDtypeStruct((B, S, W), F32)],
        compiler_params=_params(("parallel", "parallel", "arbitrary")),
    )(proj3, proj3, proj3, proj3, cum_t)


def _fox_bwd(proj3, cum_t, o, lse, dy, W, heads):
    B, S, _ = proj3.shape
    H = heads
    Dh = W // heads
    hpb = LANES // Dh
    P, TQ = W // LANES, _query_tile(S)
    NQ = S // TQ
    scale = 1.0 / math.sqrt(Dh)

    TK = _key_tile(S)
    RS = hpb * TQ
    NM = max(1, TQ // TK)

    def body(q_ref, k_ref, v_ref, g_ref, ct_ref, o_ref, lse_ref, dy_ref,
             dpj_ref, dc_ref, dk_ref, dv_ref, p_scr, dp_scr):
        p = pl.program_id(1)
        i = pl.program_id(2)
        rows = pl.ds(pl.multiple_of(i * TQ, TQ), TQ)

        @pl.when(i == 0)
        def _():
            dk_ref[...] = jnp.zeros_like(dk_ref)
            dv_ref[...] = jnp.zeros_like(dv_ref)
            dc_ref[...] = jnp.zeros_like(dc_ref)

        lane_head = _lane_head(Dh)
        sub_h = lax.broadcasted_iota(jnp.int32, (H, 1), 0)
        lane = lax.broadcasted_iota(jnp.int32, (1, LANES), 1)
        g = g_ref[...]
        lsev = lse_ref[...]
        dyv = dy_ref[...].astype(F32)
        dpj_ref[3, rows, :] = (dyv * o_ref[...] * _dsilu(g)).astype(dpj_ref.dtype)
        qs = (_stack_heads(q_ref[...], lane_head, hpb) * scale).astype(BF16)
        dos = _stack_heads(dyv * _silu(g), lane_head, hpb).astype(BF16)
        neg_lse = -jnp.concatenate([_pick_col(lsev, h * Dh, lane) for h in range(hpb)], axis=0)
        nt = (i * TQ + TQ - 1) // TK + 1

        def probs(jt, dsum, masked, r0=0):
            n = TQ - r0
            off = pl.multiple_of(jt * TK, TK)
            kj = k_ref[pl.ds(off, TK), :].astype(BF16)
            vj = v_ref[pl.ds(off, TK), :].astype(BF16)
            ctb = ct_ref[:, pl.ds(off, TK)]
            dos_l = _live_rows(dos, r0, hpb)
            z = _dot(_live_rows(qs, r0, hpb), kj, _NT) + _live_rows(neg_lse, r0, hpb)
            s = jnp.concatenate([z[h * n:(h + 1) * n] - _pick_row(ctb, p * hpb + h, sub_h) for h in range(hpb)],
                                axis=0)
            pr = jnp.exp(s)
            if masked:
                trow, col = _causal_iotas(RS, TK, TQ, r0)
                pr = jnp.where(col + (jt * TK - i * TQ) <= trow, pr, 0.0)
            dp = _dot(dos_l, vj, _NT)
            p_scr[jt] = _put_rows(jnp.zeros((RS, TK), F32), pr, r0, hpb)
            dp_scr[jt] = _put_rows(jnp.zeros((RS, TK), F32), dp, r0, hpb)
            dv_ref[pl.ds(off, TK), :] += _dot(pr.astype(BF16), dos_l, _TN)
            return _put_rows(dsum, _live_rows(dsum, r0, hpb) + jnp.sum(pr * dp, axis=1, keepdims=True), r0, hpb)

        dsum = lax.fori_loop(0, nt - NM, lambda jt, d: probs(jt, d, False), jnp.zeros((RS, 1), F32))
        for m in reversed(range(NM)):
            dsum = probs(nt - 1 - m, dsum, True)

        def grads(jt, acc):
            off = pl.multiple_of(jt * TK, TK)
            kj = k_ref[pl.ds(off, TK), :].astype(BF16)
            ds = p_scr[jt] * (dp_scr[jt] - dsum)
            for h in range(hpb):
                dc_ref[h:h + 1, pl.ds(off, TK)] -= jnp.sum(ds[h * TQ:(h + 1) * TQ], axis=0, keepdims=True)
            dsb = ds.astype(BF16)
            dk_ref[pl.ds(off, TK), :] += _dot(dsb, qs, _TN)
            return acc + _dot(dsb, kj)

        acc = lax.fori_loop(0, nt, grads, jnp.zeros((RS, LANES), F32))
        dpj_ref[0, rows, :] = (_unstack_heads(acc, lane_head, hpb) * scale).astype(dpj_ref.dtype)

        @pl.when(i == NQ - 1)
        def _():
            dpj_ref[1] = dk_ref[...].astype(dpj_ref.dtype)
            dpj_ref[2] = dv_ref[...].astype(dpj_ref.dtype)

    blk = lambda sec: pl.BlockSpec((None, TQ, LANES), lambda b, p, i: (b, i, sec * P + p))
    full = lambda sec: pl.BlockSpec((None, S, LANES), lambda b, p, i: (b, 0, sec * P + p))
    one = pl.BlockSpec((None, TQ, LANES), lambda b, p, i: (b, i, p))
    return pl.pallas_call(
        body, name="fox_bwd", grid=(B, P, NQ),
        in_specs=[blk(0), full(1), full(2), blk(3),
                  pl.BlockSpec((None, H, S), lambda b, p, i: (b, 0, 0)),
                  one, one, one],
        out_specs=[pl.BlockSpec((None, 4, S, LANES), lambda b, p, i: (b, 0, 0, p)),
                   pl.BlockSpec((None, None, hpb, S), lambda b, p, i: (b, p, 0, 0))],
        out_shape=[jax.ShapeDtypeStruct((B, 4, S, W), BF16), jax.ShapeDtypeStruct((B, P, hpb, S), F32)],
        scratch_shapes=[pltpu.VMEM((S, LANES), F32), pltpu.VMEM((S, LANES), F32),
                        pltpu.VMEM((S // TK, RS, TK), F32), pltpu.VMEM((S // TK, RS, TK), F32)],
        compiler_params=_params(("parallel", "parallel", "arbitrary")),
    )(proj3, proj3, proj3, proj3, cum_t, o, lse, dy)


def _layernorm_rows(v, gamma, beta):
    mu = jnp.mean(v, axis=-1, keepdims=True)
    xc = v - mu
    rstd = lax.rsqrt(jnp.mean(xc * xc, axis=-1, keepdims=True) + EPS)
    xh = xc * rstd
    return xh, rstd, xh * gamma + beta


def _layernorm_rows_bwd(dout, xh, rstd, gamma):
    dxh = dout * gamma
    return rstd * (dxh - jnp.mean(dxh, axis=-1, keepdims=True) - xh * jnp.mean(dxh * xh, axis=-1, keepdims=True))


def _gmlp_fwd(proj, wm, bs_t, ln_g, ln_b, W):
    T = proj.shape[0]
    G = wm.shape[0]
    cg = W // G
    assert cg == LANES

    def body(p_ref, wm_ref, bs_ref, lg_ref, lb_ref, y_ref, vn_ref):
        lane = lax.broadcasted_iota(jnp.int32, (1, LANES), 1)
        _, _, vn = _layernorm_rows(_gelu(p_ref[:, W:2 * W]), lg_ref[...], lb_ref[...])
        vn_ref[...] = vn.astype(BF16)
        bs = bs_ref[...]
        for g in range(G):
            sl = pl.ds(g * cg, cg)
            s = _dot(wm_ref[g], vn_ref[:, sl]) + _pick_col(bs, g, lane)
            gate = p_ref[:, pl.ds(2 * W + g * cg, cg)]
            y_ref[:, sl] = (_gelu(p_ref[:, sl]) * s * _silu(gate)).astype(BF16)

    vec = pl.BlockSpec((1, W), lambda r: (0, 0))
    return pl.pallas_call(
        body, name="gmlp_fwd", grid=(T // BLK,),
        in_specs=[pl.BlockSpec((BLK, 3 * W), lambda r: (r, 0)),
                  pl.BlockSpec((G, BLK, BLK), lambda r: (0, 0, 0)),
                  pl.BlockSpec((BLK, LANES), lambda r: (0, 0)), vec, vec],
        out_specs=pl.BlockSpec((BLK, W), lambda r: (r, 0)),
        out_shape=jax.ShapeDtypeStruct((T, W), BF16),
        scratch_shapes=[pltpu.VMEM((BLK, W), BF16)],
        compiler_params=_params(("parallel",)),
    )(proj, wm, bs_t, ln_g, ln_b)


def _gmlp_bwd(proj, dy, wm, bs_t, ln_g, ln_b, W):
    T = proj.shape[0]
    G = wm.shape[0]
    cg = W // G

    def body(p_ref, dy_ref, wm_ref, bs_ref, lg_ref, lb_ref,
             dp_ref, dwm_ref, dbs_ref, dlg_ref, dlb_ref, vn_ref, dvn_ref):
        r = pl.program_id(0)

        @pl.when(r == 0)
        def _():
            dwm_ref[...] = jnp.zeros_like(dwm_ref)
            dbs_ref[...] = jnp.zeros_like(dbs_ref)
            dlg_ref[...] = jnp.zeros_like(dlg_ref)
            dlb_ref[...] = jnp.zeros_like(dlb_ref)

        lane = lax.broadcasted_iota(jnp.int32, (1, LANES), 1)
        vpre = p_ref[:, W:2 * W]
        gamma = lg_ref[...]
        xh, rstd, vn = _layernorm_rows(_gelu(vpre), gamma, lb_ref[...])
        vn_ref[...] = vn.astype(BF16)
        bs = bs_ref[...]
        dbs = jnp.zeros((BLK, LANES), F32)
        for g in range(G):
            sl = pl.ds(g * cg, cg)
            gsl = pl.ds(2 * W + g * cg, cg)
            vng = vn_ref[:, sl]
            s = _dot(wm_ref[g], vng) + _pick_col(bs, g, lane)
            upre = p_ref[:, sl]
            u = _gelu(upre)
            gate = p_ref[:, gsl]
            dyv = dy_ref[:, sl].astype(F32)
            dp_ref[:, gsl] = (dyv * u * s * _dsilu(gate)).astype(dp_ref.dtype)
            do = dyv * _silu(gate)
            dp_ref[:, sl] = (do * s * _dgelu(upre)).astype(dp_ref.dtype)
            ds = do * u
            dbs = dbs + jnp.where(lane == g, jnp.sum(ds, axis=1, keepdims=True), 0.0)
            dsb = ds.astype(BF16)
            dwm_ref[g] += _dot(dsb, vng, _NT)
            dvn_ref[:, sl] = _dot(wm_ref[g], dsb, _TN)
        dbs_ref[...] += dbs
        dvn = dvn_ref[...]
        dlg_ref[...] += jnp.sum(dvn * xh, axis=0, keepdims=True)
        dlb_ref[...] += jnp.sum(dvn, axis=0, keepdims=True)
        dv = _layernorm_rows_bwd(dvn, xh, rstd, gamma)
        dp_ref[:, W:2 * W] = (dv * _dgelu(vpre)).astype(dp_ref.dtype)

    vec = pl.BlockSpec((1, W), lambda r: (0, 0))
    return pl.pallas_call(
        body, name="gmlp_bwd", grid=(T // BLK,),
        in_specs=[pl.BlockSpec((BLK, 3 * W), lambda r: (r, 0)),
                  pl.BlockSpec((BLK, W), lambda r: (r, 0)),
                  pl.BlockSpec((G, BLK, BLK), lambda r: (0, 0, 0)),
                  pl.BlockSpec((BLK, LANES), lambda r: (0, 0)), vec, vec],
        out_specs=[pl.BlockSpec((BLK, 3 * W), lambda r: (r, 0)),
                   pl.BlockSpec((G, BLK, BLK), lambda r: (0, 0, 0)),
                   pl.BlockSpec((BLK, LANES), lambda r: (0, 0)), vec, vec],
        out_shape=[jax.ShapeDtypeStruct((T, 3 * W), BF16), jax.ShapeDtypeStruct((G, BLK, BLK), F32),
                   jax.ShapeDtypeStruct((BLK, LANES), F32),
                   jax.ShapeDtypeStruct((1, W), F32), jax.ShapeDtypeStruct((1, W), F32)],
        scratch_shapes=[pltpu.VMEM((BLK, W), BF16), pltpu.VMEM((BLK, W), F32)],
        compiler_params=_params(("arbitrary",)),
    )(proj, dy, wm, bs_t, ln_g, ln_b)


SUBLANES = 8
SHIFT_ROWS = CONV_HALO + BLK - SUBLANES


def _shift_rows(ext_ref, sh_ref, off):
    for r in range(1, SUBLANES):
        sh_ref[r - 1] = ext_ref[pl.ds(r, SHIFT_ROWS), pl.ds(off, LANES)]


def _rows_from(ext_ref, sh_ref, off, start):
    r = start % SUBLANES
    if r == 0:
        return ext_ref[pl.ds(start, BLK), pl.ds(off, LANES)]
    return sh_ref[r - 1, pl.ds(start - r, BLK), :]


def _conv_taps(ext_ref, sh_ref, cw_ref, off, n_taps, first):
    acc = jnp.zeros((BLK, LANES), F32)
    for k in range(n_taps):
        acc = acc + cw_ref[k:k + 1, pl.ds(off, LANES)] * _rows_from(ext_ref, sh_ref, off, first + k)
    return acc


def _fill_glu_ext(ext_ref, halo_ref, cur_ref, W, first_block):
    y0h = halo_ref[:, :W] * _sigmoid(halo_ref[:, W:])
    ext_ref[0:CONV_HALO, :] = jnp.where(first_block, 0.0, y0h)
    ext_ref[CONV_HALO:CONV_HALO + BLK, :] = cur_ref[:, :W] * _sigmoid(cur_ref[:, W:])


def _conv_specs(S, W):
    per = BLK // CONV_HALO
    cur = pl.BlockSpec((None, BLK, 2 * W), lambda b, i: (b, i, 0))
    halo = pl.BlockSpec((None, CONV_HALO, 2 * W), lambda b, i: (b, jnp.maximum(i * per - 1, 0), 0))
    gate = pl.BlockSpec((None, BLK, W), lambda b, i: (b, i, 2))
    return cur, halo, gate


def _conv_fwd(proj3, cw, cb, ln_g, ln_b, W, exch):
    B, S, _ = proj3.shape
    K = cw.shape[0]
    first = CONV_HALO - (K - 1)
    assert first >= 0

    def body(cur_ref, halo_ref, g_ref, cw_ref, cb_ref, lg_ref, lb_ref, y_ref, ext_ref, y1_ref, sh_ref):
        i = pl.program_id(1)
        _fill_glu_ext(ext_ref, halo_ref, cur_ref, W, i == 0)

        def chan(c, _):
            off = pl.multiple_of(c * LANES, LANES)
            _shift_rows(ext_ref, sh_ref, off)
            y1_ref[:, pl.ds(off, LANES)] = (_conv_taps(ext_ref, sh_ref, cw_ref, off, K, first)
                                            + cb_ref[:, pl.ds(off, LANES)])
            return 0

        lax.fori_loop(0, W // LANES, chan, 0)
        _, _, ln = _layernorm_rows(y1_ref[...], lg_ref[...], lb_ref[...])
        y_ref[...] = (_silu(ln) * _silu(g_ref[...])).astype(BF16)

    cur, halo, gate = _conv_specs(S, W)
    vec = pl.BlockSpec((1, W), lambda b, i: (0, 0))
    (y,), moved = _call_hosting(
        body, exch, "conv_fwd", (B, S // BLK),
        [cur, halo, gate, pl.BlockSpec((K, W), lambda b, i: (0, 0)), vec, vec, vec],
        [pl.BlockSpec((None, BLK, W), lambda b, i: (b, i, 0))], [jax.ShapeDtypeStruct((B, S, W), BF16)],
        [pltpu.VMEM((CONV_HALO + BLK, W), F32), pltpu.VMEM((BLK, W), F32),
         pltpu.VMEM((SUBLANES - 1, SHIFT_ROWS, LANES), F32)],
        (proj3, proj3, proj3, cw, cb, ln_g, ln_b))
    return y, moved


def _conv_bwd1(proj3, dy, cw, cb, ln_g, ln_b, W, exch):
    B, S, _ = proj3.shape
    K = cw.shape[0]
    first = CONV_HALO - (K - 1)

    def body(cur_ref, halo_ref, g_ref, dy_ref, cw_ref, cb_ref, lg_ref, lb_ref,
             dy1_ref, dg_ref, dcw_ref, dcb_ref, dlg_ref, dlb_ref, ext_ref, y1_ref, sh_ref):
        b = pl.program_id(0)
        i = pl.program_id(1)

        @pl.when(jnp.logical_and(b == 0, i == 0))
        def _():
            dcw_ref[...] = jnp.zeros_like(dcw_ref)
            dcb_ref[...] = jnp.zeros_like(dcb_ref)
            dlg_ref[...] = jnp.zeros_like(dlg_ref)
            dlb_ref[...] = jnp.zeros_like(dlb_ref)

        _fill_glu_ext(ext_ref, halo_ref, cur_ref, W, i == 0)

        def chan(c, _):
            off = pl.multiple_of(c * LANES, LANES)
            _shift_rows(ext_ref, sh_ref.at[c], off)
            y1_ref[:, pl.ds(off, LANES)] = (_conv_taps(ext_ref, sh_ref.at[c], cw_ref, off, K, first)
                                            + cb_ref[:, pl.ds(off, LANES)])
            return 0

        lax.fori_loop(0, W // LANES, chan, 0)
        gamma = lg_ref[...]
        xh, rstd, ln = _layernorm_rows(y1_ref[...], gamma, lb_ref[...])
        g = g_ref[...]
        dyv = dy_ref[...].astype(F32)
        dg_ref[...] = (dyv * _silu(ln) * _dsilu(g)).astype(dg_ref.dtype)
        dln = dyv * _silu(g) * _dsilu(ln)
        dlg_ref[...] += jnp.sum(dln * xh, axis=0, keepdims=True)
        dlb_ref[...] += jnp.sum(dln, axis=0, keepdims=True)
        dy1 = _layernorm_rows_bwd(dln, xh, rstd, gamma)
        dy1_ref[...] = dy1
        dcb_ref[...] += jnp.sum(dy1, axis=0, keepdims=True)

        def chan_w(c, _):
            off = pl.multiple_of(c * LANES, LANES)
            d = dy1_ref[:, pl.ds(off, LANES)]
            for k in range(K):
                dcw_ref[k:k + 1, pl.ds(off, LANES)] += jnp.sum(
                    d * _rows_from(ext_ref, sh_ref.at[c], off, first + k), axis=0, keepdims=True)
            return 0

        lax.fori_loop(0, W // LANES, chan_w, 0)

    cur, halo, gate = _conv_specs(S, W)
    vec = pl.BlockSpec((1, W), lambda b, i: (0, 0))
    taps = pl.BlockSpec((K, W), lambda b, i: (0, 0))
    one = pl.BlockSpec((None, BLK, W), lambda b, i: (b, i, 0))
    return _call_hosting(
        body, exch, "conv_bwd1", (B, S // BLK), [cur, halo, gate, one, taps, vec, vec, vec],
        [one, one, taps, vec, vec, vec],
        [jax.ShapeDtypeStruct((B, S, W), F32), jax.ShapeDtypeStruct((B, S, W), BF16),
         jax.ShapeDtypeStruct((K, W), F32)] + [jax.ShapeDtypeStruct((1, W), F32)] * 3,
        [pltpu.VMEM((CONV_HALO + BLK, W), F32), pltpu.VMEM((BLK, W), F32),
         pltpu.VMEM((W // LANES, SUBLANES - 1, SHIFT_ROWS, LANES), F32)],
        (proj3, proj3, proj3, dy, cw, cb, ln_g, ln_b))


def _conv_bwd2(proj3, dy1, dgate, cw_rev, W):
    B, S, _ = proj3.shape
    K = cw_rev.shape[0]
    NQ = S // BLK
    per = BLK // CONV_HALO

    def body(cur_ref, d_ref, dnext_ref, dgate_ref, cw_ref, dp_ref, ext_ref, dy0_ref, sh_ref):
        i = pl.program_id(1)
        ext_ref[0:BLK, :] = d_ref[...]
        ext_ref[BLK:BLK + CONV_HALO, :] = jnp.where(i == NQ - 1, 0.0, dnext_ref[...])

        def chan(c, _):
            off = pl.multiple_of(c * LANES, LANES)
            _shift_rows(ext_ref, sh_ref, off)
            dy0_ref[:, pl.ds(off, LANES)] = _conv_taps(ext_ref, sh_ref, cw_ref, off, K, 0)
            return 0

        lax.fori_loop(0, W // LANES, chan, 0)
        a = cur_ref[:, :W]
        sg = _sigmoid(cur_ref[:, W:])
        dy0 = dy0_ref[...]
        dp_ref[:, 0:W] = (dy0 * sg).astype(dp_ref.dtype)
        dp_ref[:, W:2 * W] = (dy0 * a * sg * (1.0 - sg)).astype(dp_ref.dtype)
        dp_ref[:, 2 * W:3 * W] = dgate_ref[...]

    cur = pl.BlockSpec((None, BLK, 2 * W), lambda b, i: (b, i, 0))
    one = pl.BlockSpec((None, BLK, W), lambda b, i: (b, i, 0))
    nxt = pl.BlockSpec((None, CONV_HALO, W), lambda b, i: (b, jnp.minimum((i + 1) * per, S // CONV_HALO - 1), 0))
    return pl.pallas_call(
        body, name="conv_bwd2", grid=(B, NQ),
        in_specs=[cur, one, nxt, one, pl.BlockSpec((K, W), lambda b, i: (0, 0))],
        out_specs=pl.BlockSpec((None, BLK, 3 * W), lambda b, i: (b, i, 0)),
        out_shape=jax.ShapeDtypeStruct((B, S, 3 * W), BF16),
        scratch_shapes=[pltpu.VMEM((BLK + CONV_HALO, W), F32), pltpu.VMEM((BLK, W), F32),
                        pltpu.VMEM((SUBLANES - 1, SHIFT_ROWS, LANES), F32)],
        compiler_params=_params(("parallel", "parallel")),
    )(proj3, dy1, dy1, dgate, cw_rev)


def _pack(arrays):
    flat = jnp.concatenate([a.astype(F32).reshape(-1) for a in arrays])
    n = flat.shape[0]
    pad = (-n) % (8 * LANES)
    if pad:
        flat = jnp.concatenate([flat, jnp.zeros((pad,), F32)])
    return flat.reshape(-1, LANES)


def _unpack(packed, shapes, lead=()):
    flat = packed.reshape(lead + (-1,))
    out, off = [], 0
    for shp in shapes:
        n = math.prod(shp)
        out.append(flat[..., off:off + n].reshape(lead + tuple(shp)))
        off += n
    return out


def _cols_from_dev(g):
    g = jnp.moveaxis(g, 0, -2)
    return g.reshape(g.shape[:-2] + (g.shape[-2] * g.shape[-1],))


def _my_cols(full, me):
    n8 = full.shape[-1] // N_DEV
    return lax.dynamic_slice_in_dim(full, me * n8, n8, axis=full.ndim - 1)


def kernel(x, a_norm, a_w_in, a_w_out, b_norm, b_w_in, b_v_ln_g, b_v_ln_b, b_w_s, b_b_s, b_w_out, c_norm, c_w_in, c_conv_w, c_conv_b, c_ln_g, c_ln_b, c_w_out, d_norm, d_w_in, d_b_f, d_w_out, final_norm, loss_target, m_a_norm, m_a_w_in, m_a_w_out, m_b_norm, m_b_w_in, m_b_v_ln_g, m_b_v_ln_b, m_b_w_s, m_b_b_s, m_b_w_out, m_c_norm, m_c_w_in, m_c_conv_w, m_c_conv_b, m_c_ln_g, m_c_ln_b, m_c_w_out, m_d_norm, m_d_w_in, m_d_b_f, m_d_w_out, m_final_norm, v_a_norm, v_a_w_in, v_a_w_out, v_b_norm, v_b_w_in, v_b_v_ln_g, v_b_v_ln_b, v_b_w_s, v_b_b_s, v_b_w_out, v_c_norm, v_c_w_in, v_c_conv_w, v_c_conv_b, v_c_ln_g, v_c_ln_b, v_c_w_out, v_d_norm, v_d_w_in, v_d_b_f, v_d_w_out, v_final_norm):
    B, S, D = x.shape
    T = B * S
    xi, yi, ci = _me()
    me = 4 * xi + 2 * yi + ci

    G = b_w_s.shape[1]
    KC = c_conv_w.shape[1]
    H_D = d_b_f.shape[1]
    W_A = a_w_out.shape[1] * N_DEV
    W_B = b_w_out.shape[1] * N_DEV
    W_C = c_w_out.shape[1] * N_DEV
    W_D = d_w_out.shape[1] * N_DEV
    N_D = d_w_in.shape[2] * N_DEV
    N_D_PAD = -(-N_D // (3 * LANES)) * (3 * LANES)

    big_names = ["a_w_in", "a_w_out", "b_w_in", "b_w_out", "c_w_in", "c_w_out", "d_w_in", "d_w_out"]
    big_w = dict(a_w_in=a_w_in[0], a_w_out=a_w_out[0], b_w_in=b_w_in[0], b_w_out=b_w_out[0],
                 c_w_in=c_w_in[0], c_w_out=c_w_out[0], d_w_in=d_w_in[0], d_w_out=d_w_out[0])
    small_sharded = [b_norm, b_v_ln_g, b_v_ln_b, c_norm, c_conv_w, c_conv_b, c_ln_g, c_ln_b, d_norm]
    first_names, later_names, last_names = big_names[:1], big_names[1:6], big_names[6:]
    gathered = _GatherViaSibling(
        [big_w[n].astype(BF16) for n in first_names] + [_pack(small_sharded)]).run("gather_first")
    wg = dict(zip(first_names, gathered[:-1]))
    (b_norm_f, b_lg_f, b_lb_f, c_norm_f, c_cw_f, c_cb_f, c_lg_f, c_lb_f, d_norm_f) = [
        _cols_from_dev(t) for t in _unpack(gathered[-1], [s.shape for s in small_sharded], lead=(N_DEV,))]
    c_cw_f = c_cw_f[0]

    wm = jnp.tril(b_w_s[0]).astype(BF16)
    bs_t = jnp.pad(b_b_s[0].T, ((0, 0), (0, LANES - G)))

    x0 = x.reshape(T, D)
    h_a = _rmsnorm_fwd(x0, a_norm, "rms_a")
    proj_a = _mm_w_dev(h_a, wg["a_w_in"], "proj_a").reshape(B, S, 4 * W_A)
    (o_a, y_a), later = _sb_fwd(proj_a, W_A, SB_HEADS,
                                _GatherViaSibling([big_w[n].astype(BF16) for n in later_names]))
    wg.update(zip(later_names, later))
    a_w_out_f = wg["a_w_out"].reshape(W_A, D)
    b_w_out_f = wg["b_w_out"].reshape(W_B, D)
    c_w_out_f = wg["c_w_out"].reshape(W_C, D)
    y_a = y_a.reshape(T, W_A)
    x1 = _mm(y_a, a_w_out_f, "nn", T, D, W_A, F32, "out_a", 512, D, W_A, res=x0)
    h_b = _rmsnorm_fwd(x1, b_norm_f, "rms_b")
    proj_b = _mm_w_dev(h_b, wg["b_w_in"], "proj_b")
    y_b = _gmlp_fwd(proj_b, wm, bs_t, b_lg_f, b_lb_f, W_B)
    x2 = _mm(y_b, b_w_out_f, "nn", T, D, W_B, F32, "out_b", 512, D, W_B, res=x1)
    h_c = _rmsnorm_fwd(x2, c_norm_f, "rms_c")
    proj_c = _mm_w_dev(h_c, wg["c_w_in"], "proj_c").reshape(B, S, 3 * W_C)
    y_c, last = _conv_fwd(proj_c, c_cw_f, c_cb_f, c_lg_f, c_lb_f, W_C,
                          _GatherViaSibling([big_w[n].astype(BF16) for n in last_names]))
    wg.update(zip(last_names, last))
    d_w_out_f = wg["d_w_out"].reshape(W_D, D)
    d_w_in_f = jnp.pad(_cols_from_dev(wg["d_w_in"]), ((0, 0), (0, N_D_PAD - N_D)))
    y_c = y_c.reshape(T, W_C)
    x3 = _mm(y_c, c_w_out_f, "nn", T, D, W_C, F32, "out_c", 512, D, W_C, res=x2)
    h_d = _rmsnorm_fwd(x3, d_norm_f, "rms_d")
    proj_d = _mm(h_d, d_w_in_f, "nn", T, N_D_PAD, D, F32, "proj_d", 1024, 384, D).reshape(B, S, N_D_PAD)
    f_t = jnp.swapaxes(proj_d[:, :, 4 * W_D:4 * W_D + H_D], 1, 2)
    b_f_col = d_b_f.reshape(H_D, 1)
    cum_t = _fox_gate_fwd(f_t, b_f_col)
    o_d, y_d, lse_d = _fox_fwd(proj_d, cum_t, W_D, H_D)
    y_d = y_d.reshape(T, W_D)
    x4 = _mm(y_d, d_w_out_f, "nn", T, D, W_D, F32, "out_d", 512, D, W_D, res=x3)

    loss_part, dx, g_final = _loss_head(x4, final_norm.reshape(1, D), loss_target.reshape(T, D))
    loss = lax.psum(loss_part[0, 0], MESH_AXES)

    dy_d = _mm(dx, d_w_out_f, "nt", T, W_D, D, BF16, "dy_d", 512, W_D, D).reshape(B, S, W_D)
    gw_d_out = _mm(y_d, dx, "tn", W_D, D, T, BF16, "gw_d_out", W_D, D, 512).reshape(N_DEV, W_D // N_DEV, D)
    dproj_d, dcum = _fox_bwd(proj_d, cum_t, o_d, lse_d, dy_d, W_D, H_D)
    df_t, g_b_f = _fox_gate_bwd(dcum.reshape(B, H_D, S), f_t, b_f_col)
    F_PAD = N_D_PAD - 4 * W_D
    df = jnp.pad(jnp.swapaxes(df_t, 1, 2), ((0, 0), (0, 0), (0, F_PAD - H_D))).reshape(T, F_PAD)
    tc, tr = min(512, W_D), min(1024, S)
    gw_main = _mm(h_d, dproj_d, "tn", D, 4 * W_D, T, BF16, "gw_d_in", D, tc, tr,
                  b_spec=_sectioned_spec(dproj_d, tr, tc, 2, 1))
    gw_f = _mm(h_d, df, "tn", D, F_PAD, T, BF16, "gw_d_in_f", D, F_PAD, 512)
    gw_d_in = jnp.moveaxis(
        jnp.concatenate([gw_main, gw_f], axis=1)[:, :N_D].reshape(D, N_DEV, N_D // N_DEV), 1, 0)
    dh = _mm(dproj_d, d_w_in_f, "nt", T, D, 4 * W_D, F32, "dh_d", tr, D, tc,
             a_spec=_sectioned_spec(dproj_d, tr, tc, 0, 2))
    dh = _mm(df, d_w_in_f[:, 4 * W_D:], "nt", T, D, F_PAD, F32, "dh_d_f", 512, D, F_PAD, res=dh)
    dx, g_d_norm = _rmsnorm_bwd(x3, d_norm_f, dh, dx, "rms_bwd_d")

    dy_c = _mm(dx, c_w_out_f, "nt", T, W_C, D, BF16, "dy_c", 512, W_C, D).reshape(B, S, W_C)
    gw_c_out = _mm(y_c, dx, "tn", W_C, D, T, BF16, "gw_c_out", 1024, D, 512).reshape(N_DEV, W_C // N_DEV, D)
    (dy1, dgate_c, g_c_cw, g_c_cb, g_c_lg, g_c_lb), parts_d = _conv_bwd1(
        proj_c, dy_c, c_cw_f, c_cb_f, c_lg_f, c_lb_f, W_C, _Exchange([gw_d_in, gw_d_out], ["scatter"] * 2))
    dproj_c = _conv_bwd2(proj_c, dy1, dgate_c, c_cw_f[::-1], W_C).reshape(T, 3 * W_C)
    gw_c_in = _mm_grad_dev(h_c, dproj_c, "gw_c_in")
    dh = _mm_wT_dev(dproj_c, wg["c_w_in"], "dh_c")
    dx, g_c_norm = _rmsnorm_bwd(x2, c_norm_f, dh, dx, "rms_bwd_c")

    dy_b = _mm(dx, b_w_out_f, "nt", T, W_B, D, BF16, "dy_b", 512, W_B, D)
    gw_b_out = _mm(y_b, dx, "tn", W_B, D, T, BF16, "gw_b_out", 1024, D, 512).reshape(N_DEV, W_B // N_DEV, D)
    dproj_b, g_wm, g_bs_t, g_b_lg, g_b_lb = _gmlp_bwd(proj_b, dy_b, wm, bs_t, b_lg_f, b_lb_f, W_B)
    g_b_w_s = jnp.tril(g_wm)
    g_b_b_s = g_bs_t[:, :G].T
    gw_b_in = _mm_grad_dev(h_b, dproj_b, "gw_b_in")
    dh = _mm_wT_dev(dproj_b, wg["b_w_in"], "dh_b")
    dx, g_b_norm = _rmsnorm_bwd(x1, b_norm_f, dh, dx, "rms_bwd_b")

    dy_a = _mm(dx, a_w_out_f, "nt", T, W_A, D, BF16, "dy_a", 512, W_A, D).reshape(B, S, W_A)
    gw_a_out = _mm(y_a, dx, "tn", W_A, D, T, BF16, "gw_a_out", W_A, D, 512).reshape(N_DEV, W_A // N_DEV, D)
    small_full = [g_b_norm, g_b_lg, g_b_lb, g_b_b_s, g_c_norm, g_c_cw, g_c_cb, g_c_lg, g_c_lb,
                  g_d_norm, g_b_f, g_final]
    dproj_a, parts_s = _sb_bwd(
        proj_a, o_a, dy_a, W_A, SB_HEADS,
        _Exchange([gw_c_in, gw_c_out, gw_b_in, gw_b_out, gw_a_out, _pack(small_full), g_b_w_s.reshape(-1, LANES)],
                  ["scatter"] * 5 + ["gather"] * 2))
    gw_a_in = _mm_grad_dev(h_a, dproj_a, "gw_a_in")
    dh, parts_a = _mm_wT_dev(dproj_a, wg["a_w_in"], "dh_a", exch=_Exchange([gw_a_in], ["scatter"]))
    dx, g_a_norm = _rmsnorm_bwd(x0, a_norm, dh, dx, "rms_bwd_a")
    grad_x = dx.reshape(B, S, D)

    (parts_n,) = _Exchange([_pack([g_a_norm])], ["gather"]).run("exchange_last")
    big_parts = dict(a_w_in=parts_a[0], a_w_out=parts_s[4], b_w_in=parts_s[2], b_w_out=parts_s[3],
                     c_w_in=parts_s[0], c_w_out=parts_s[1], d_w_in=parts_d[0], d_w_out=parts_d[1])
    (s_b_norm, s_b_lg, s_b_lb, s_b_b_s, s_c_norm, s_c_cw, s_c_cb, s_c_lg, s_c_lb,
     s_d_norm, s_b_f, s_final) = _unpack(_sum_parts(parts_s[5], "sum_small"), [g.shape for g in small_full])
    (s_a_norm,) = _unpack(_sum_parts(parts_n, "sum_a_norm"), [g_a_norm.shape])

    weights = dict(a_norm=a_norm, a_w_in=a_w_in, a_w_out=a_w_out, b_norm=b_norm, b_w_in=b_w_in, b_v_ln_g=b_v_ln_g,
                   b_v_ln_b=b_v_ln_b, b_w_s=b_w_s, b_b_s=b_b_s, b_w_out=b_w_out, c_norm=c_norm, c_w_in=c_w_in,
                   c_conv_w=c_conv_w, c_conv_b=c_conv_b, c_ln_g=c_ln_g, c_ln_b=c_ln_b, c_w_out=c_w_out,
                   d_norm=d_norm, d_w_in=d_w_in, d_b_f=d_b_f, d_w_out=d_w_out, final_norm=final_norm)
    mom_m = dict(a_norm=m_a_norm, a_w_in=m_a_w_in, a_w_out=m_a_w_out, b_norm=m_b_norm, b_w_in=m_b_w_in,
                 b_v_ln_g=m_b_v_ln_g, b_v_ln_b=m_b_v_ln_b, b_w_s=m_b_w_s, b_b_s=m_b_b_s, b_w_out=m_b_w_out,
                 c_norm=m_c_norm, c_w_in=m_c_w_in, c_conv_w=m_c_conv_w, c_conv_b=m_c_conv_b, c_ln_g=m_c_ln_g,
                 c_ln_b=m_c_ln_b, c_w_out=m_c_w_out, d_norm=m_d_norm, d_w_in=m_d_w_in, d_b_f=m_d_b_f,
                 d_w_out=m_d_w_out, final_norm=m_final_norm)
    mom_v = dict(a_norm=v_a_norm, a_w_in=v_a_w_in, a_w_out=v_a_w_out, b_norm=v_b_norm, b_w_in=v_b_w_in,
                 b_v_ln_g=v_b_v_ln_g, b_v_ln_b=v_b_v_ln_b, b_w_s=v_b_w_s, b_b_s=v_b_b_s, b_w_out=v_b_w_out,
                 c_norm=v_c_norm, c_w_in=v_c_w_in, c_conv_w=v_c_conv_w, c_conv_b=v_c_conv_b, c_ln_g=v_c_ln_g,
                 c_ln_b=v_c_ln_b, c_w_out=v_c_w_out, d_norm=v_d_norm, d_w_in=v_d_w_in, d_b_f=v_d_b_f,
                 d_w_out=v_d_w_out, final_norm=v_final_norm)
    order = list(weights)
    grads, deltas, new_m, new_v = {}, {}, {}, {}

    for n in big_names:
        part = big_parts[n]
        shp = weights[n].shape
        R, C = shp[1], shp[2]
        res = _adamw(part, weights[n].reshape(R, C), mom_m[n].reshape(R, C), mom_v[n].reshape(R, C), "adamw_" + n)
        grads[n], deltas[n], new_m[n], new_v[n] = [r.reshape(shp) for r in res]

    res = _adamw(parts_s[6], b_w_s.reshape(-1, LANES), m_b_w_s.reshape(-1, LANES), v_b_w_s.reshape(-1, LANES),
                 "adamw_b_w_s")
    grads["b_w_s"], deltas["b_w_s"], new_m["b_w_s"], new_v["b_w_s"] = [r.reshape(b_w_s.shape) for r in res]

    small_g = dict(
        a_norm=s_a_norm, b_norm=_my_cols(s_b_norm, me), b_v_ln_g=_my_cols(s_b_lg, me),
        b_v_ln_b=_my_cols(s_b_lb, me), b_b_s=s_b_b_s[None], c_norm=_my_cols(s_c_norm, me),
        c_conv_w=_my_cols(s_c_cw, me)[None], c_conv_b=_my_cols(s_c_cb, me), c_ln_g=_my_cols(s_c_lg, me),
        c_ln_b=_my_cols(s_c_lb, me), d_norm=_my_cols(s_d_norm, me), d_b_f=s_b_f.reshape(1, H_D),
        final_norm=s_final.reshape(D))
    small_names = list(small_g)
    sg_p = _pack([small_g[n] for n in small_names])
    res = _adamw(sg_p[None], _pack([weights[n] for n in small_names]), _pack([mom_m[n] for n in small_names]),
                 _pack([mom_v[n] for n in small_names]), "adamw_small")
    shapes = [weights[n].shape for n in small_names]
    for dst, r in zip((grads, deltas, new_m, new_v), res):
        for n, val in zip(small_names, _unpack(r, shapes)):
            dst[n] = val

    return (loss, grad_x, *[grads[n] for n in order], *[deltas[n] for n in order],
            *[new_m[n] for n in order], *[new_v[n] for n in order])
```

```python
import functools
import math

import jax
import jax.numpy as jnp
from jax import lax
from jax.experimental import pallas as pl
from jax.experimental.pallas import tpu as pltpu

F32 = jnp.float32
BF16 = jnp.bfloat16

EPS = 1e-6
SB_HEADS = 16
CONV_HALO = 32
BLK = 128
ATT_TK = 256
ATT_TQ = 512
ATT_GP = 2
LANES = 128
N_DEV = 8
MESH_AXES = ("x", "y", "c")

ADAM_LR = 0.001
ADAM_B1 = 0.9
ADAM_B2 = 0.999
ADAM_EPS = 1e-08
ADAM_WD = 0.01
ADAM_STEP = 10

VMEM_LIMIT = 56 * 1024 * 1024
NEG_BIG = -1e30

_NN = (((1,), (0,)), ((), ()))
_NT = (((1,), (1,)), ((), ()))
_TN = (((0,), (0,)), ((), ()))


def _dot(a, b, dims=_NN):
    return lax.dot_general(a, b, dims, preferred_element_type=F32)


def _split_dot(x, m):
    hi = x.astype(BF16)
    lo = (x - hi.astype(F32)).astype(BF16)
    return _dot(hi, m) + _dot(lo, m)


def _split3_dot(x, m):
    hi = x.astype(BF16)
    r1 = x - hi.astype(F32)
    mid = r1.astype(BF16)
    lo = (r1 - mid.astype(F32)).astype(BF16)
    return _dot(hi, m) + _dot(mid, m) + _dot(lo, m)


def _params(sem=None):
    kw = dict(vmem_limit_bytes=VMEM_LIMIT)
    if sem is not None:
        kw["dimension_semantics"] = sem
    return pltpu.CompilerParams(**kw)


def _sigmoid(x):
    return jax.nn.sigmoid(x)


def _silu(x):
    return x * _sigmoid(x)


def _dsilu(x):
    s = _sigmoid(x)
    return s * (1.0 + x * (1.0 - s))


_GELU_C = math.sqrt(2.0 / math.pi)


def _gelu(x):
    return 0.5 * x * (1.0 + jnp.tanh(_GELU_C * (x + 0.044715 * x * x * x)))


def _dgelu(x):
    th = jnp.tanh(_GELU_C * (x + 0.044715 * x * x * x))
    return 0.5 * (1.0 + th) + 0.5 * x * (1.0 - th * th) * _GELU_C * (1.0 + 3.0 * 0.044715 * x * x)


def _mm(a, b, mode, M, N, K, out_dtype, name, tm, tn, tk, a_spec=None, b_spec=None, o_spec=None, out_shape=None,
        exch=None, res=None, norm_gain=None, norm_bwd=None):
    tm, tn, tk = min(tm, M), min(tn, N), min(tk, K)
    assert M % tm == 0 and N % tn == 0 and K % tk == 0, (name, M, N, K, tm, tn, tk)
    nk = K // tk
    assert norm_gain is None or (nk == 1 and tn == N and exch is None)
    dims = {"nn": _NN, "nt": _NT, "tn": _TN}[mode]
    if a_spec is None:
        a_spec = (pl.BlockSpec((tk, tm), lambda i, j, k: (k, i)) if mode == "tn"
                  else pl.BlockSpec((tm, tk), lambda i, j, k: (i, k)))
    if b_spec is None:
        b_spec = (pl.BlockSpec((tn, tk), lambda i, j, k: (j, k)) if mode == "nt"
                  else pl.BlockSpec((tk, tn), lambda i, j, k: (k, j)))
    if o_spec is None:
        o_spec = pl.BlockSpec((tm, tn), lambda i, j, k: (i, j))
    if out_shape is None:
        out_shape = (M, N)

    def body(a_ref, b_ref, *rest):
        res_ref = rest[0] if res is not None else None
        if nk == 1:
            d = _dot(a_ref[...].astype(BF16), b_ref[...].astype(BF16), dims)
            r = d if res_ref is None else res_ref[...].astype(F32) + d
            if norm_gain is None:
                rest[-1][...] = r.astype(rest[-1].dtype)
            else:
                g_ref, o_ref, h_ref = rest[-3:]
                o_ref[...] = r.astype(o_ref.dtype)
                scale = lax.rsqrt(jnp.mean(r * r, axis=-1, keepdims=True) + EPS)
                h_ref[...] = (r * scale * g_ref[...]).astype(BF16)
            return
        i = pl.program_id(0)
        k = pl.program_id(2)
        if norm_bwd is not None:
            x_ref, g_ref, dres_ref, o_ref, dg_ref, acc_ref = rest

            @pl.when(jnp.logical_and(i == 0, k == 0))
            def _():
                dg_ref[...] = jnp.zeros_like(dg_ref)
        else:
            o_ref, acc_ref = rest[-2:]

        @pl.when(k == 0)
        def _():
            acc_ref[...] = jnp.zeros_like(acc_ref) if res_ref is None else res_ref[...].astype(F32)

        acc_ref[...] += _dot(a_ref[...].astype(BF16), b_ref[...].astype(BF16), dims)

        @pl.when(k == nk - 1)
        def _():
            if norm_bwd is None:
                o_ref[...] = acc_ref[...].astype(o_ref.dtype)
            else:
                dh = acc_ref[...]
                xv = x_ref[...]
                r = lax.rsqrt(jnp.mean(xv * xv, axis=-1, keepdims=True) + EPS)
                xh = xv * r
                dxh = dh * g_ref[...]
                o_ref[...] = dres_ref[...] + r * (dxh - xh * jnp.mean(dxh * xh, axis=-1, keepdims=True))
                dg_ref[...] += jnp.sum(dh * xh, axis=0, keepdims=True)

    in_specs, args = [a_spec, b_spec], (a, b)
    if res is not None:
        in_specs, args = in_specs + [o_spec], args + (res,)
    scratch = [pltpu.VMEM((tm, tn), F32)] if nk > 1 else []
    if norm_bwd is not None:
        assert nk > 1 and tn == N and res is None and norm_gain is None
        vec = pl.BlockSpec((1, N), lambda i, j, k: (0, 0))
        x_in, g_in, dres_in = norm_bwd
        (dx, dg), moved = _call_hosting(
            body, exch, name, (M // tm, 1, nk), in_specs + [o_spec, vec, o_spec], [o_spec, vec],
            [jax.ShapeDtypeStruct((M, N), F32), jax.ShapeDtypeStruct((1, N), F32)], scratch,
            args + (x_in, g_in, dres_in))
        return dx, dg, moved
    if norm_gain is not None:
        return pl.pallas_call(
            body, name=name, grid=(M // tm, N // tn, nk),
            in_specs=in_specs + [pl.BlockSpec((1, N), lambda i, j, k: (0, 0))], out_specs=[o_spec, o_spec],
            out_shape=[jax.ShapeDtypeStruct(out_shape, out_dtype), jax.ShapeDtypeStruct(out_shape, BF16)],
            compiler_params=_params(("parallel", "parallel", "arbitrary")),
        )(*args, norm_gain)
    if exch is None:
        return pl.pallas_call(
            body, name=name, grid=(M // tm, N // tn, nk),
            in_specs=in_specs, out_specs=o_spec,
            out_shape=jax.ShapeDtypeStruct(out_shape, out_dtype),
            scratch_shapes=scratch,
            compiler_params=_params(("parallel", "parallel", "arbitrary")),
        )(*args)
    (out,), moved = _call_hosting(
        body, exch, name, (M // tm, N // tn, nk), in_specs, [o_spec],
        [jax.ShapeDtypeStruct(out_shape, out_dtype)], scratch, args)
    return out, moved


def _mm_w_dev(a, w3, name, out_dtype=F32, tm=1024):
    M, K = a.shape
    n8 = w3.shape[2]
    tn = n8 if n8 <= 768 else 512
    per = n8 // tn
    b_spec = pl.BlockSpec((None, K, tn), lambda i, j, k: (j // per, 0, j % per))
    return _mm(a, w3, "nn", M, N_DEV * n8, K, out_dtype, name, tm, tn, K, b_spec=b_spec)


def _sectioned_spec(d4, t_rows, t_cols, rows_axis, cols_axis):
    _, _, S, W = d4.shape
    assert S % t_rows == 0 and W % t_cols == 0
    rb, cb = S // t_rows, W // t_cols

    def index(*g):
        r, c = g[rows_axis], g[cols_axis]
        return (r // rb, c // cb, r % rb, c % cb)

    return pl.BlockSpec((None, None, t_rows, t_cols), index)


def _mm_wT_dev(a, w3, name, out_dtype=F32, tm=1024, exch=None, norm_bwd=None):
    K, n8 = w3.shape[1], w3.shape[2]
    tk = n8 if n8 <= 768 else 512
    per = n8 // tk
    b_spec = pl.BlockSpec((None, K, tk), lambda i, j, k: (k // per, 0, k % per))
    if a.ndim == 4:
        M, N = a.shape[0] * a.shape[2], a.shape[1] * a.shape[3]
        tm = min(tm, a.shape[2])
        a_spec = _sectioned_spec(a, tm, tk, 0, 2)
    else:
        (M, N), a_spec = a.shape, None
    return _mm(a, w3, "nt", M, K, N, out_dtype, name, tm, K, tk, a_spec=a_spec, b_spec=b_spec, exch=exch,
               norm_bwd=norm_bwd)


def _mm_grad_dev(h, d, name, out_dtype=BF16):
    T, M = h.shape
    N = d.shape[1] * d.shape[3] if d.ndim == 4 else d.shape[1]
    n8 = N // N_DEV
    tn = n8 if n8 <= 768 else 512
    per = n8 // tn
    tm = min(M, 1024)
    o_spec = pl.BlockSpec((None, tm, tn), lambda i, j, k: (j // per, i, j % per))
    tk = min(1024, d.shape[2] if d.ndim == 4 else T)
    b_spec = _sectioned_spec(d, tk, tn, 2, 1) if d.ndim == 4 else None
    return _mm(h, d, "tn", M, N, T, out_dtype, name, tm, tn, tk, b_spec=b_spec, o_spec=o_spec,
               out_shape=(N_DEV, M, n8))


def _me():
    x, y, c = lax.axis_index("x"), lax.axis_index("y"), lax.axis_index("c")
    return x, y, c


def _peer(r):
    x, y, c = _me()
    px = 1 - x if (r >> 2) & 1 else x
    py = 1 - y if (r >> 1) & 1 else y
    pc = 1 - c if r & 1 else c
    return (px, py, pc), 4 * px + 2 * py + pc


class _Exchange:
    def __init__(self, arrays, kinds):
        self.arrays, self.kinds, self.n = list(arrays), list(kinds), len(arrays)
        self.out_shapes = [
            jax.ShapeDtypeStruct((N_DEV,) + a.shape if kind == "gather" else a.shape, a.dtype)
            for a, kind in zip(arrays, kinds)]
        self.specs = [pl.BlockSpec(memory_space=pl.ANY)] * self.n
        self.sems = [pltpu.SemaphoreType.DMA((self.n, N_DEV - 1)), pltpu.SemaphoreType.DMA((self.n, N_DEV - 1)),
                     pltpu.SemaphoreType.DMA((self.n,))]

    def _copies(self, ins, outs, sems, receiving):
        send_sems, recv_sems, local_sems = sems
        x, y, c = _me()
        me = 4 * x + 2 * y + c

        def src(k, pid):
            return ins[k] if self.kinds[k] == "gather" else ins[k].at[pid]

        local = [pltpu.make_async_copy(src(k, me), outs[k].at[me], local_sems.at[k]) for k in range(self.n)]
        remote = []
        for r in range(1, N_DEV):
            peer, pid = _peer(r)
            for k in range(self.n):
                remote.append(pltpu.make_async_remote_copy(
                    src_ref=src(k, pid), dst_ref=outs[k].at[pid if receiving else me],
                    send_sem=send_sems.at[k, r - 1], recv_sem=recv_sems.at[k, r - 1],
                    device_id=peer, device_id_type=pl.DeviceIdType.MESH))
        return local, remote

    def start(self, ins, outs, sems):
        local, remote = self._copies(ins, outs, sems, False)
        for cp in local + remote:
            cp.start()

    def wait(self, ins, outs, sems):
        local, remote = self._copies(ins, outs, sems, True)
        for cp in remote:
            cp.wait_recv()
        for cp in remote:
            cp.wait_send()
        for cp in local:
            cp.wait()

    def run(self, name):
        n = self.n

        def body(*refs):
            ins, outs, sems = refs[:n], refs[n:2 * n], refs[2 * n:]
            self.start(ins, outs, sems)
            self.wait(ins, outs, sems)

        return pl.pallas_call(
            body, name=name, in_specs=self.specs, out_specs=self.specs, out_shape=self.out_shapes,
            scratch_shapes=self.sems,
        )(*self.arrays)


class _GatherViaSibling(_Exchange):
    ICI = (2, 4, 6)

    def __init__(self, arrays):
        super().__init__(arrays, ["gather"] * len(arrays))

    def _copy(self, ins, outs, sems, k, column, block, to, from_input=False):
        return pltpu.make_async_remote_copy(
            src_ref=ins[k] if from_input else outs[k].at[block], dst_ref=outs[k].at[block],
            send_sem=sems[0].at[k, column], recv_sem=sems[1].at[k, column],
            device_id=to, device_id_type=pl.DeviceIdType.MESH)

    def start(self, ins, outs, sems):
        x, y, c = _me()
        me = 4 * x + 2 * y + c
        for k in range(self.n):
            pltpu.make_async_copy(ins[k], outs[k].at[me], sems[2].at[k]).start()
            self._copy(ins, outs, sems, k, 0, me, _peer(1)[0], True).start()
            for j, r in enumerate(self.ICI):
                self._copy(ins, outs, sems, k, 1 + j, me, _peer(r)[0], True).start()

    def wait(self, ins, outs, sems):
        x, y, c = _me()
        me = 4 * x + 2 * y + c
        sibling, sibling_id = _peer(1)
        for j, r in enumerate(self.ICI):
            peer, pid = _peer(r)
            for k in range(self.n):
                self._copy(ins, outs, sems, k, 1 + j, pid, peer).wait_recv()
                self._copy(ins, outs, sems, k, 4 + j, pid, sibling).start()
        for k in range(self.n):
            self._copy(ins, outs, sems, k, 0, sibling_id, sibling).wait_recv()
            for j, r in enumerate(self.ICI):
                self._copy(ins, outs, sems, k, 4 + j, _peer(r ^ 1)[1], sibling).wait_recv()
            for column in range(N_DEV - 1):
                self._copy(ins, outs, sems, k, column, me, sibling).wait_send()
            pltpu.make_async_copy(ins[k], outs[k].at[me], sems[2].at[k]).wait()


def _call_hosting(body, exch, name, grid, in_specs, out_specs, out_shape, scratch_shapes, args):
    if exch is None:
        res = pl.pallas_call(
            body, name=name, grid=grid, in_specs=list(in_specs), out_specs=list(out_specs),
            out_shape=list(out_shape), scratch_shapes=list(scratch_shapes),
            compiler_params=_params(("arbitrary",) * len(grid)))(*args)
        return res, []
    n_in, n_out, n_scr, nc = len(in_specs), len(out_specs), len(scratch_shapes), exch.n

    def full_body(*refs):
        ins, refs = refs[:n_in], refs[n_in:]
        cins, refs = refs[:nc], refs[nc:]
        outs, refs = refs[:n_out], refs[n_out:]
        couts, refs = refs[:nc], refs[nc:]
        scr, sems = refs[:n_scr], refs[n_scr:]
        ids = [pl.program_id(a) for a in range(len(grid))]
        first = functools.reduce(jnp.logical_and, [i == 0 for i in ids])
        last = functools.reduce(jnp.logical_and, [i == g - 1 for i, g in zip(ids, grid)])

        @pl.when(first)
        def _():
            exch.start(cins, couts, sems)

        body(*ins, *outs, *scr)

        @pl.when(last)
        def _():
            exch.wait(cins, couts, sems)

    res = pl.pallas_call(
        full_body, name=name, grid=grid,
        in_specs=list(in_specs) + exch.specs, out_specs=list(out_specs) + exch.specs,
        out_shape=list(out_shape) + exch.out_shapes,
        scratch_shapes=list(scratch_shapes) + exch.sems,
        compiler_params=_params(("arbitrary",) * len(grid)),
    )(*args, *exch.arrays)
    return res[:n_out], res[n_out:]


def _rmsnorm_fwd(x, g, name):
    T, D = x.shape
    tr = min(256, T)

    def body(x_ref, g_ref, h_ref):
        xv = x_ref[...]
        r = lax.rsqrt(jnp.mean(xv * xv, axis=-1, keepdims=True) + EPS)
        h_ref[...] = (xv * r * g_ref[...]).astype(BF16)

    return pl.pallas_call(
        body, name=name, grid=(T // tr,),
        in_specs=[pl.BlockSpec((tr, D), lambda i: (i, 0)), pl.BlockSpec((1, D), lambda i: (0, 0))],
        out_specs=pl.BlockSpec((tr, D), lambda i: (i, 0)),
        out_shape=jax.ShapeDtypeStruct((T, D), BF16),
        compiler_params=_params(("parallel",)),
    )(x, g)


def _rmsnorm_bwd(x, g, dh, dres, name):
    T, D = x.shape
    tr = min(256, T)

    def body(x_ref, g_ref, dh_ref, dres_ref, dx_ref, dg_ref):
        i = pl.program_id(0)
        xv = x_ref[...]
        r = lax.rsqrt(jnp.mean(xv * xv, axis=-1, keepdims=True) + EPS)
        xh = xv * r
        dhv = dh_ref[...]
        dxh = dhv * g_ref[...]
        dx_ref[...] = dres_ref[...] + r * (dxh - xh * jnp.mean(dxh * xh, axis=-1, keepdims=True))

        @pl.when(i == 0)
        def _():
            dg_ref[...] = jnp.zeros_like(dg_ref)

        dg_ref[...] += jnp.sum(dhv * xh, axis=0, keepdims=True)

    row = pl.BlockSpec((tr, D), lambda i: (i, 0))
    vec = pl.BlockSpec((1, D), lambda i: (0, 0))
    return pl.pallas_call(
        body, name=name, grid=(T // tr,),
        in_specs=[row, vec, row, row], out_specs=[row, vec],
        out_shape=[jax.ShapeDtypeStruct((T, D), F32), jax.ShapeDtypeStruct((1, D), F32)],
        compiler_params=_params(("arbitrary",)),
    )(x, g, dh, dres)


def _loss_head(x, g, target):
    T, D = x.shape
    tr = min(256, T)

    def body(x_ref, g_ref, t_ref, loss_ref, dx_ref, dg_ref):
        i = pl.program_id(0)
        xv = x_ref[...]
        gv = g_ref[...]
        r = lax.rsqrt(jnp.mean(xv * xv, axis=-1, keepdims=True) + EPS)
        xh = xv * r
        diff = xh * gv - t_ref[...]
        dy = diff * (1.0 / D)
        dxh = dy * gv
        dx_ref[...] = r * (dxh - xh * jnp.mean(dxh * xh, axis=-1, keepdims=True))

        @pl.when(i == 0)
        def _():
            dg_ref[...] = jnp.zeros_like(dg_ref)
            loss_ref[...] = jnp.zeros_like(loss_ref)

        dg_ref[...] += jnp.sum(dy * xh, axis=0, keepdims=True)
        part = jnp.sum(jnp.sum(diff * diff, axis=1, keepdims=True), axis=0, keepdims=True)
        loss_ref[...] += (0.5 / D) * part

    row = pl.BlockSpec((tr, D), lambda i: (i, 0))
    vec = pl.BlockSpec((1, D), lambda i: (0, 0))
    return pl.pallas_call(
        body, name="loss_head", grid=(T // tr,),
        in_specs=[row, vec, row],
        out_specs=[pl.BlockSpec((1, 1), lambda i: (0, 0)), row, vec],
        out_shape=[jax.ShapeDtypeStruct((1, 1), F32), jax.ShapeDtypeStruct((T, D), F32),
                   jax.ShapeDtypeStruct((1, D), F32)],
        compiler_params=_params(("arbitrary",)),
    )(x, g, target)


ELEMS_PER_STEP = 1 << 20


def _row_tile(R, per_row):
    best = None
    for tr in range(8, R + 1, 8):
        if R % tr == 0 and tr * per_row <= ELEMS_PER_STEP:
            best = tr
    return best if best is not None else R


def _adamw(parts, w, m, v, name):
    P, R, C = parts.shape
    tr = _row_tile(R, P * C)

    def body(p_ref, w_ref, m_ref, v_ref, g_out, d_out, m_out, v_out):
        g = p_ref[0].astype(F32)
        for p in range(1, P):
            g = g + p_ref[p].astype(F32)
        wv = w_ref[...]
        mn = ADAM_B1 * m_ref[...] + (1.0 - ADAM_B1) * g
        vn = ADAM_B2 * v_ref[...] + (1.0 - ADAM_B2) * (g * g)
        m_hat = mn / (1.0 - ADAM_B1 ** ADAM_STEP)
        v_hat = vn / (1.0 - ADAM_B2 ** ADAM_STEP)
        g_out[...] = g
        d_out[...] = -ADAM_LR * (m_hat / (jnp.sqrt(v_hat) + ADAM_EPS) + ADAM_WD * wv)
        m_out[...] = mn
        v_out[...] = vn

    row = pl.BlockSpec((tr, C), lambda i: (i, 0))
    return pl.pallas_call(
        body, name=name, grid=(R // tr,),
        in_specs=[pl.BlockSpec((P, tr, C), lambda i: (0, i, 0)), row, row, row],
        out_specs=[row, row, row, row],
        out_shape=[jax.ShapeDtypeStruct((R, C), F32)] * 4,
        compiler_params=_params(("parallel",)),
    )(parts, w, m, v)


def _sum_parts(parts, name):
    P, R, C = parts.shape
    tr = _row_tile(R, P * C)

    def body(p_ref, o_ref):
        g = p_ref[0]
        for p in range(1, P):
            g = g + p_ref[p]
        o_ref[...] = g

    return pl.pallas_call(
        body, name=name, grid=(R // tr,),
        in_specs=[pl.BlockSpec((P, tr, C), lambda i: (0, i, 0))],
        out_specs=pl.BlockSpec((tr, C), lambda i: (i, 0)),
        out_shape=jax.ShapeDtypeStruct((R, C), F32),
        compiler_params=_params(("parallel",)),
    )(parts)


def _lane_head(Dh):
    assert Dh & (Dh - 1) == 0 and Dh <= LANES
    return lax.shift_right_logical(lax.broadcasted_iota(jnp.int32, (1, LANES), 1), Dh.bit_length() - 1)


def _stack_heads(x, lane_head, hpb):
    return jnp.concatenate([jnp.where(lane_head == h, x, 0.0) for h in range(hpb)], axis=0)


def _unstack_heads(acc, lane_head, hpb):
    TQ = acc.shape[0] // hpb
    out = acc[0:TQ]
    for h in range(1, hpb):
        out = jnp.where(lane_head == h, acc[h * TQ:(h + 1) * TQ], out)
    return out


def _live_rows(x, r0, hpb):
    if r0 == 0:
        return x
    TQ = x.shape[0] // hpb
    return jnp.concatenate([x[h * TQ + r0:(h + 1) * TQ] for h in range(hpb)], axis=0)


def _put_rows(full, part, r0, hpb):
    if r0 == 0:
        return part
    TQ = full.shape[0] // hpb
    n = TQ - r0
    return jnp.concatenate(
        [blk for h in range(hpb) for blk in (full[h * TQ:h * TQ + r0], part[h * n:(h + 1) * n])], axis=0)


def _first_live_row(m, TQ, TK):
    return max(0, TQ - (m + 1) * TK)


def _key_tile(S):
    return ATT_TK if S % ATT_TK == 0 else BLK


def _query_tile(S):
    return ATT_TQ if S % ATT_TQ == 0 else BLK


def _lane_groups(P):
    return ATT_GP if P % ATT_GP == 0 else 1


def _lanes(u):
    return slice(u * LANES, (u + 1) * LANES)


def _causal_iotas(RS, TK, TQ, r0=0):
    n = TQ - r0
    assert n & (n - 1) == 0 and (TK % TQ == 0 or TQ % TK == 0)
    rows = RS // TQ * n
    trow = jnp.bitwise_and(lax.broadcasted_iota(jnp.int32, (rows, TK), 0), n - 1) + r0
    col = lax.broadcasted_iota(jnp.int32, (rows, TK), 1)
    return trow, col


def _tri(TK, op):
    r = lax.broadcasted_iota(jnp.int32, (TK, TK), 0)
    c = lax.broadcasted_iota(jnp.int32, (TK, TK), 1)
    return op(r, c).astype(BF16)


def _logsig_parts(z):
    lb = jnp.minimum(z, 0.0) - jnp.log(1.0 + jnp.exp(-jnp.abs(z)))
    return lb, lb - z


def _sb_fwd(proj3, W, heads, exch):
    B, S, _ = proj3.shape
    Dh = W // heads
    hpb = LANES // Dh
    P, TQ = W // LANES, _query_tile(S)
    NQ = S // TQ
    scale = 1.0 / math.sqrt(Dh)

    TK = _key_tile(S)
    RS = hpb * TQ
    NM = max(1, TQ // TK)
    GP = _lane_groups(P)
    PG = P // GP

    def body(q_ref, k_ref, v_ref, g_ref, o_ref, y_ref):
        i = pl.program_id(2)
        lane_head = _lane_head(Dh)
        msuf = _tri(TK, lambda r, c: r > c)
        qs = [(_stack_heads(q_ref[:, _lanes(u)], lane_head, hpb) * scale).astype(BF16) for u in range(GP)]
        nt = (i * TQ + TQ - 2) // TK + 1

        def tile(jt, carry, masked, r0=0):
            off = pl.multiple_of(jt * TK, TK)
            if masked:
                trow, col = _causal_iotas(RS, TK, TQ, r0)
                msk = col + (jt * TK - i * TQ) < trow
            out = []
            for u, (rem_all, acc_all) in enumerate(carry):
                rem, acc = _live_rows(rem_all, r0, hpb), _live_rows(acc_all, r0, hpb)
                kj = k_ref[pl.ds(off, TK), _lanes(u)].astype(BF16)
                vj = v_ref[pl.ds(off, TK), _lanes(u)].astype(BF16)
                lb, lr = _logsig_parts(_dot(_live_rows(qs[u], r0, hpb), kj, _NT))
                if masked:
                    lr = jnp.where(msk, lr, 0.0)
                w = jnp.exp(lb + _split_dot(lr, msuf) + rem)
                if masked:
                    w = jnp.where(msk, w, 0.0)
                out.append((_put_rows(rem_all, rem + jnp.sum(lr, axis=1, keepdims=True), r0, hpb),
                            _put_rows(acc_all, acc + _dot(w.astype(BF16), vj), r0, hpb)))
            return tuple(out)

        zero = (jnp.zeros((RS, 1), F32), jnp.zeros((RS, LANES), F32))
        carry = (zero,) * GP
        for m in range(NM):
            carry = tile(nt - 1 - m, carry, True, _first_live_row(m, TQ, TK))
        carry = lax.fori_loop(NM, nt, lambda jj, c: tile(nt - 1 - jj, c, False), carry)
        for u in range(GP):
            o = _unstack_heads(carry[u][1], lane_head, hpb)
            o_ref[:, _lanes(u)] = o
            y_ref[:, _lanes(u)] = (o * _silu(g_ref[:, _lanes(u)])).astype(BF16)

    LW = GP * LANES
    blk = lambda sec: pl.BlockSpec((None, TQ, LW), lambda b, p, i: (b, i, sec * PG + p))
    full = lambda sec: pl.BlockSpec((None, S, LW), lambda b, p, i: (b, 0, sec * PG + p))
    out = pl.BlockSpec((None, TQ, LW), lambda b, p, i: (b, i, p))
    return _call_hosting(
        body, exch, "sb_fwd", (B, PG, NQ), [blk(0), full(1), full(2), blk(3)], [out, out],
        [jax.ShapeDtypeStruct((B, S, W), F32), jax.ShapeDtypeStruct((B, S, W), BF16)], [],
        (proj3, proj3, proj3, proj3))


def _sb_bwd(proj3, o, dy, W, heads, exch):
    B, S, _ = proj3.shape
    Dh = W // heads
    hpb = LANES // Dh
    P, TQ = W // LANES, _query_tile(S)
    NQ = S // TQ
    scale = 1.0 / math.sqrt(Dh)

    TK = _key_tile(S)
    RS = hpb * TQ
    NM = max(1, TQ // TK)

    def body(q_ref, k_ref, v_ref, g_ref, o_ref, dy_ref, dp_ref, dk_ref, dv_ref, u_ref, sig_ref, es_ref):
        i = pl.program_id(2)
        rows = pl.ds(pl.multiple_of(i * TQ, TQ), TQ)

        @pl.when(i == 0)
        def _():
            dk_ref[...] = jnp.zeros_like(dk_ref)
            dv_ref[...] = jnp.zeros_like(dv_ref)

        lane_head = _lane_head(Dh)
        msuf = _tri(TK, lambda r, c: r > c)
        mpre = _tri(TK, lambda r, c: r < c)
        g = g_ref[...]
        dyv = dy_ref[...].astype(F32)
        dp_ref[3, rows, :] = (dyv * o_ref[...] * _dsilu(g)).astype(dp_ref.dtype)
        qs = (_stack_heads(q_ref[...], lane_head, hpb) * scale).astype(BF16)
        dos = _stack_heads(dyv * _silu(g), lane_head, hpb).astype(BF16)
        nt = (i * TQ + TQ - 2) // TK + 1

        def weights(jt, rem_all, masked, r0=0):
            off = pl.multiple_of(jt * TK, TK)
            kj = k_ref[pl.ds(off, TK), :].astype(BF16)
            vj = v_ref[pl.ds(off, TK), :].astype(BF16)
            dos_l = _live_rows(dos, r0, hpb)
            lb, lr = _logsig_parts(_dot(_live_rows(qs, r0, hpb), kj, _NT))
            if masked:
                trow, col = _causal_iotas(RS, TK, TQ, r0)
                msk = col + (jt * TK - i * TQ) < trow
                lr = jnp.where(msk, lr, 0.0)
            w = jnp.exp(lb + _split_dot(lr, msuf) + _live_rows(rem_all, r0, hpb))
            if masked:
                w = jnp.where(msk, w, 0.0)
            e = w * _dot(dos_l, vj, _NT)
            sig = jnp.exp(lb)
            u = e * (1.0 - sig) - _split_dot(e, mpre) * sig
            if masked:
                u = jnp.where(msk, u, 0.0)
                sig = jnp.where(msk, sig, 0.0)
            u_ref[jt] = _put_rows(jnp.zeros((RS, TK), F32), u, r0, hpb)
            sig_ref[jt] = _put_rows(jnp.zeros((RS, TK), F32), sig, r0, hpb)
            es_ref[jt] = _put_rows(jnp.zeros((RS, 1), F32), jnp.sum(e, axis=1, keepdims=True), r0, hpb)
            dv_ref[pl.ds(off, TK), :] += _dot(w.astype(BF16), dos_l, _TN)
            return _put_rows(rem_all, _live_rows(rem_all, r0, hpb) + jnp.sum(lr, axis=1, keepdims=True), r0, hpb)

        rem = jnp.zeros((RS, 1), F32)
        for m in range(NM):
            rem = weights(nt - 1 - m, rem, True, _first_live_row(m, TQ, TK))
        lax.fori_loop(NM, nt, lambda jj, r: weights(nt - 1 - jj, r, False), rem)

        def grads(jt, carry):
            pre, acc = carry
            off = pl.multiple_of(jt * TK, TK)
            kj = k_ref[pl.ds(off, TK), :].astype(BF16)
            dz = (u_ref[jt] - pre * sig_ref[jt]).astype(BF16)
            dk_ref[pl.ds(off, TK), :] += _dot(dz, qs, _TN)
            return pre + es_ref[jt], acc + _dot(dz, kj)

        _, acc = lax.fori_loop(0, nt, grads, (jnp.zeros((RS, 1), F32), jnp.zeros((RS, LANES), F32)))
        dp_ref[0, rows, :] = (_unstack_heads(acc, lane_head, hpb) * scale).astype(dp_ref.dtype)

        @pl.when(i == NQ - 1)
        def _():
            dp_ref[1] = dk_ref[...].astype(dp_ref.dtype)
            dp_ref[2] = dv_ref[...].astype(dp_ref.dtype)

    blk = lambda sec: pl.BlockSpec((None, TQ, LANES), lambda b, p, i: (b, i, sec * P + p))
    full = lambda sec: pl.BlockSpec((None, S, LANES), lambda b, p, i: (b, 0, sec * P + p))
    one = pl.BlockSpec((None, TQ, LANES), lambda b, p, i: (b, i, p))
    (dproj,), moved = _call_hosting(
        body, exch, "sb_bwd", (B, P, NQ), [blk(0), full(1), full(2), blk(3), one, one],
        [pl.BlockSpec((None, 4, S, LANES), lambda b, p, i: (b, 0, 0, p))],
        [jax.ShapeDtypeStruct((B, 4, S, W), BF16)],
        [pltpu.VMEM((S, LANES), F32), pltpu.VMEM((S, LANES), F32),
         pltpu.VMEM((S // TK, RS, TK), F32), pltpu.VMEM((S // TK, RS, TK), F32), pltpu.VMEM((S // TK, RS, 1), F32)],
        (proj3, proj3, proj3, proj3, o, dy))
    return dproj, moved


def _fox_gate_fwd(f_t, b_f):
    B, H, S = f_t.shape

    def body(f_ref, b_ref, c_ref):
        row = lax.broadcasted_iota(jnp.int32, (BLK, BLK), 0)
        col = lax.broadcasted_iota(jnp.int32, (BLK, BLK), 1)
        mpre = (row <= col).astype(BF16)
        carry = jnp.zeros((H, 1), F32)
        for n in range(S // BLK):
            sl = pl.ds(n * BLK, BLK)
            lf, _ = _logsig_parts(f_ref[:, sl] + b_ref[...])
            c_ref[:, sl] = _split3_dot(lf, mpre) + carry
            carry = carry + jnp.sum(lf, axis=1, keepdims=True)

    spec = pl.BlockSpec((None, H, S), lambda b: (b, 0, 0))
    return pl.pallas_call(
        body, name="fox_gate_fwd", grid=(B,),
        in_specs=[spec, pl.BlockSpec((H, 1), lambda b: (0, 0))], out_specs=spec,
        out_shape=jax.ShapeDtypeStruct((B, H, S), F32),
        compiler_params=_params(("parallel",)),
    )(f_t, b_f)


def _fox_gate_bwd(dcum_t, f_t, b_f):
    B, H, S = f_t.shape

    def body(d_ref, f_ref, b_ref, df_ref, db_ref):
        b = pl.program_id(0)

        @pl.when(b == 0)
        def _():
            db_ref[...] = jnp.zeros_like(db_ref)

        row = lax.broadcasted_iota(jnp.int32, (BLK, BLK), 0)
        col = lax.broadcasted_iota(jnp.int32, (BLK, BLK), 1)
        msuf = (row >= col).astype(BF16)
        carry = jnp.zeros((H, 1), F32)
        dbacc = jnp.zeros((H, 1), F32)
        for n in reversed(range(S // BLK)):
            sl = pl.ds(n * BLK, BLK)
            dv = d_ref[:, sl]
            dlf = _split3_dot(dv, msuf) + carry
            carry = carry + jnp.sum(dv, axis=1, keepdims=True)
            df = dlf * _sigmoid(-(f_ref[:, sl] + b_ref[...]))
            df_ref[:, sl] = df
            dbacc = dbacc + jnp.sum(df, axis=1, keepdims=True)
        db_ref[...] += dbacc

    spec = pl.BlockSpec((None, H, S), lambda b: (b, 0, 0))
    vec = pl.BlockSpec((H, 1), lambda b: (0, 0))
    return pl.pallas_call(
        body, name="fox_gate_bwd", grid=(B,),
        in_specs=[spec, spec, vec], out_specs=[spec, vec],
        out_shape=[jax.ShapeDtypeStruct((B, H, S), F32), jax.ShapeDtypeStruct((H, 1), F32)],
        compiler_params=_params(("arbitrary",)),
    )(dcum_t, f_t, b_f)


def _pick_col(block, idx, lane_iota):
    return jnp.sum(jnp.where(lane_iota == idx, block, 0.0), axis=1, keepdims=True)


def _pick_row(block, idx, sub_iota):
    return jnp.sum(jnp.where(sub_iota == idx, block, 0.0), axis=0, keepdims=True)


def _fox_fwd(proj3, cum_t, W, heads):
    B, S, _ = proj3.shape
    H = heads
    Dh = W // heads
    hpb = LANES // Dh
    P, TQ = W // LANES, _query_tile(S)
    NQ = S // TQ
    scale = 1.0 / math.sqrt(Dh)

    TK = _key_tile(S)
    RS = hpb * TQ
    NM = max(1, TQ // TK)

    def body(q_ref, k_ref, v_ref, g_ref, ct_ref, o_ref, y_ref, lse_ref):
        p = pl.program_id(1)
        i = pl.program_id(2)
        lane_head = _lane_head(Dh)
        sub_h = lax.broadcasted_iota(jnp.int32, (H, 1), 0)
        qs = (_stack_heads(q_ref[...], lane_head, hpb) * scale).astype(BF16)
        nt = (i * TQ + TQ - 1) // TK + 1

        trow, col = _causal_iotas(RS, TK, TQ)

        def tile(jt, carry, masked):
            mx, l, acc = carry
            off = pl.multiple_of(jt * TK, TK)
            kj = k_ref[pl.ds(off, TK), :].astype(BF16)
            vj = v_ref[pl.ds(off, TK), :].astype(BF16)
            ctb = ct_ref[:, pl.ds(off, TK)]
            z = _dot(qs, kj, _NT)
            s = jnp.concatenate([z[h * TQ:(h + 1) * TQ] - _pick_row(ctb, p * hpb + h, sub_h) for h in range(hpb)],
                                axis=0)
            if masked:
                s = jnp.where(col + (jt * TK - i * TQ) <= trow, s, NEG_BIG)
            mx2 = jnp.maximum(mx, jnp.max(s, axis=1, keepdims=True))
            pe = jnp.exp(s - mx2)
            alpha = jnp.exp(mx - mx2)
            return (mx2, alpha * l + jnp.sum(pe, axis=1, keepdims=True), alpha * acc + _dot(pe.astype(BF16), vj))

        carry = lax.fori_loop(
            0, nt - NM, lambda jt, c: tile(jt, c, False),
            (jnp.full((RS, 1), NEG_BIG, F32), jnp.zeros((RS, 1), F32), jnp.zeros((RS, LANES), F32)))
        for m in reversed(range(NM)):
            carry = tile(nt - 1 - m, carry, True)
        mx, l, acc = carry
        o = _unstack_heads(acc / l, lane_head, hpb)
        o_ref[...] = o
        lse_ref[...] = _unstack_heads(jnp.broadcast_to(mx + jnp.log(l), (RS, LANES)), lane_head, hpb)
        y_ref[...] = (o * _silu(g_ref[...])).astype(BF16)

    blk = lambda sec: pl.BlockSpec((None, TQ, LANES), lambda b, p, i: (b, i, sec * P + p))
    full = lambda sec: pl.BlockSpec((None, S, LANES), lambda b, p, i: (b, 0, sec * P + p))
    out = pl.BlockSpec((None, TQ, LANES), lambda b, p, i: (b, i, p))
    return pl.pallas_call(
        body, name="fox_fwd", grid=(B, P, NQ),
        in_specs=[blk(0), full(1), full(2), blk(3),
                  pl.BlockSpec((None, H, S), lambda b, p, i: (b, 0, 0))],
        out_specs=[out, out, out],
        out_shape=[jax.ShapeDtypeStruct((B, S, W), F32), jax.ShapeDtypeStruct((B, S, W), BF16),
                   jax.ShapeDtypeStruct((B, S, W), F32)],
        compiler_params=_params(("parallel", "parallel", "arbitrary")),
    )(proj3, proj3, proj3, proj3, cum_t)


def _fox_bwd(proj3, cum_t, o, lse, dy, W, heads):
    B, S, _ = proj3.shape
    H = heads
    Dh = W // heads
    hpb = LANES // Dh
    P, TQ = W // LANES, _query_tile(S)
    NQ = S // TQ
    scale = 1.0 / math.sqrt(Dh)

    TK = _key_tile(S)
    RS = hpb * TQ
    NM = max(1, TQ // TK)

    def body(q_ref, k_ref, v_ref, g_ref, ct_ref, o_ref, lse_ref, dy_ref,
             dpj_ref, dc_ref, dk_ref, dv_ref, p_scr, dp_scr):
        p = pl.program_id(1)
        i = pl.program_id(2)
        rows = pl.ds(pl.multiple_of(i * TQ, TQ), TQ)

        @pl.when(i == 0)
        def _():
            dk_ref[...] = jnp.zeros_like(dk_ref)
            dv_ref[...] = jnp.zeros_like(dv_ref)
            dc_ref[...] = jnp.zeros_like(dc_ref)

        lane_head = _lane_head(Dh)
        sub_h = lax.broadcasted_iota(jnp.int32, (H, 1), 0)
        lane = lax.broadcasted_iota(jnp.int32, (1, LANES), 1)
        g = g_ref[...]
        lsev = lse_ref[...]
        dyv = dy_ref[...].astype(F32)
        dpj_ref[3, rows, :] = (dyv * o_ref[...] * _dsilu(g)).astype(dpj_ref.dtype)
        qs = (_stack_heads(q_ref[...], lane_head, hpb) * scale).astype(BF16)
        dos = _stack_heads(dyv * _silu(g), lane_head, hpb).astype(BF16)
        neg_lse = -jnp.concatenate([_pick_col(lsev, h * Dh, lane) for h in range(hpb)], axis=0)
        nt = (i * TQ + TQ - 1) // TK + 1

        def probs(jt, dsum, masked, r0=0):
            n = TQ - r0
            off = pl.multiple_of(jt * TK, TK)
            kj = k_ref[pl.ds(off, TK), :].astype(BF16)
            vj = v_ref[pl.ds(off, TK), :].astype(BF16)
            ctb = ct_ref[:, pl.ds(off, TK)]
            dos_l = _live_rows(dos, r0, hpb)
            z = _dot(_live_rows(qs, r0, hpb), kj, _NT) + _live_rows(neg_lse, r0, hpb)
            s = jnp.concatenate([z[h * n:(h + 1) * n] - _pick_row(ctb, p * hpb + h, sub_h) for h in range(hpb)],
                                axis=0)
            pr = jnp.exp(s)
            if masked:
                trow, col = _causal_iotas(RS, TK, TQ, r0)
                pr = jnp.where(col + (jt * TK - i * TQ) <= trow, pr, 0.0)
            dp = _dot(dos_l, vj, _NT)
            p_scr[jt] = _put_rows(jnp.zeros((RS, TK), F32), pr, r0, hpb)
            dp_scr[jt] = _put_rows(jnp.zeros((RS, TK), F32), dp, r0, hpb)
            dv_ref[pl.ds(off, TK), :] += _dot(pr.astype(BF16), dos_l, _TN)
            return _put_rows(dsum, _live_rows(dsum, r0, hpb) + jnp.sum(pr * dp, axis=1, keepdims=True), r0, hpb)

        dsum = lax.fori_loop(0, nt - NM, lambda jt, d: probs(jt, d, False), jnp.zeros((RS, 1), F32))
        for m in reversed(range(NM)):
            dsum = probs(nt - 1 - m, dsum, True)

        def grads(jt, acc):
            off = pl.multiple_of(jt * TK, TK)
            kj = k_ref[pl.ds(off, TK), :].astype(BF16)
            ds = p_scr[jt] * (dp_scr[jt] - dsum)
            for h in range(hpb):
                dc_ref[h:h + 1, pl.ds(off, TK)] -= jnp.sum(ds[h * TQ:(h + 1) * TQ], axis=0, keepdims=True)
            dsb = ds.astype(BF16)
            dk_ref[pl.ds(off, TK), :] += _dot(dsb, qs, _TN)
            return acc + _dot(dsb, kj)

        acc = lax.fori_loop(0, nt, grads, jnp.zeros((RS, LANES), F32))
        dpj_ref[0, rows, :] = (_unstack_heads(acc, lane_head, hpb) * scale).astype(dpj_ref.dtype)

        @pl.when(i == NQ - 1)
        def _():
            dpj_ref[1] = dk_ref[...].astype(dpj_ref.dtype)
            dpj_ref[2] = dv_ref[...].astype(dpj_ref.dtype)

    blk = lambda sec: pl.BlockSpec((None, TQ, LANES), lambda b, p, i: (b, i, sec * P + p))
    full = lambda sec: pl.BlockSpec((None, S, LANES), lambda b, p, i: (b, 0, sec * P + p))
    one = pl.BlockSpec((None, TQ, LANES), lambda b, p, i: (b, i, p))
    return pl.pallas_call(
        body, name="fox_bwd", grid=(B, P, NQ),
        in_specs=[blk(0), full(1), full(2), blk(3),
                  pl.BlockSpec((None, H, S), lambda b, p, i: (b, 0, 0)),
                  one, one, one],
        out_specs=[pl.BlockSpec((None, 4, S, LANES), lambda b, p, i: (b, 0, 0, p)),
                   pl.BlockSpec((None, None, hpb, S), lambda b, p, i: (b, p, 0, 0))],
        out_shape=[jax.ShapeDtypeStruct((B, 4, S, W), BF16), jax.ShapeDtypeStruct((B, P, hpb, S), F32)],
        scratch_shapes=[pltpu.VMEM((S, LANES), F32), pltpu.VMEM((S, LANES), F32),
                        pltpu.VMEM((S // TK, RS, TK), F32), pltpu.VMEM((S // TK, RS, TK), F32)],
        compiler_params=_params(("parallel", "parallel", "arbitrary")),
    )(proj3, proj3, proj3, proj3, cum_t, o, lse, dy)


def _layernorm_rows(v, gamma, beta):
    mu = jnp.mean(v, axis=-1, keepdims=True)
    xc = v - mu
    rstd = lax.rsqrt(jnp.mean(xc * xc, axis=-1, keepdims=True) + EPS)
    xh = xc * rstd
    return xh, rstd, xh * gamma + beta


def _layernorm_rows_bwd(dout, xh, rstd, gamma):
    dxh = dout * gamma
    return rstd * (dxh - jnp.mean(dxh, axis=-1, keepdims=True) - xh * jnp.mean(dxh * xh, axis=-1, keepdims=True))


def _gmlp_fwd(proj, wm, bs_t, ln_g, ln_b, W):
    T = proj.shape[0]
    G = wm.shape[0]
    cg = W // G
    assert cg == LANES

    def body(p_ref, wm_ref, bs_ref, lg_ref, lb_ref, y_ref, vn_ref):
        lane = lax.broadcasted_iota(jnp.int32, (1, LANES), 1)
        _, _, vn = _layernorm_rows(_gelu(p_ref[:, W:2 * W]), lg_ref[...], lb_ref[...])
        vn_ref[...] = vn.astype(BF16)
        bs = bs_ref[...]
        for g in range(G):
            sl = pl.ds(g * cg, cg)
            s = _dot(wm_ref[g], vn_ref[:, sl]) + _pick_col(bs, g, lane)
            gate = p_ref[:, pl.ds(2 * W + g * cg, cg)]
            y_ref[:, sl] = (_gelu(p_ref[:, sl]) * s * _silu(gate)).astype(BF16)

    vec = pl.BlockSpec((1, W), lambda r: (0, 0))
    return pl.pallas_call(
        body, name="gmlp_fwd", grid=(T // BLK,),
        in_specs=[pl.BlockSpec((BLK, 3 * W), lambda r: (r, 0)),
                  pl.BlockSpec((G, BLK, BLK), lambda r: (0, 0, 0)),
                  pl.BlockSpec((BLK, LANES), lambda r: (0, 0)), vec, vec],
        out_specs=pl.BlockSpec((BLK, W), lambda r: (r, 0)),
        out_shape=jax.ShapeDtypeStruct((T, W), BF16),
        scratch_shapes=[pltpu.VMEM((BLK, W), BF16)],
        compiler_params=_params(("parallel",)),
    )(proj, wm, bs_t, ln_g, ln_b)


def _gmlp_bwd(proj, dy, wm, bs_t, ln_g, ln_b, W):
    T = proj.shape[0]
    G = wm.shape[0]
    cg = W // G

    def body(p_ref, dy_ref, wm_ref, bs_ref, lg_ref, lb_ref,
             dp_ref, dwm_ref, dbs_ref, dlg_ref, dlb_ref, vn_ref, dvn_ref):
        r = pl.program_id(0)

        @pl.when(r == 0)
        def _():
            dwm_ref[...] = jnp.zeros_like(dwm_ref)
            dbs_ref[...] = jnp.zeros_like(dbs_ref)
            dlg_ref[...] = jnp.zeros_like(dlg_ref)
            dlb_ref[...] = jnp.zeros_like(dlb_ref)

        lane = lax.broadcasted_iota(jnp.int32, (1, LANES), 1)
        vpre = p_ref[:, W:2 * W]
        gamma = lg_ref[...]
        xh, rstd, vn = _layernorm_rows(_gelu(vpre), gamma, lb_ref[...])
        vn_ref[...] = vn.astype(BF16)
        bs = bs_ref[...]
        dbs = jnp.zeros((BLK, LANES), F32)
        for g in range(G):
            sl = pl.ds(g * cg, cg)
            gsl = pl.ds(2 * W + g * cg, cg)
            vng = vn_ref[:, sl]
            s = _dot(wm_ref[g], vng) + _pick_col(bs, g, lane)
            upre = p_ref[:, sl]
            u = _gelu(upre)
            gate = p_ref[:, gsl]
            dyv = dy_ref[:, sl].astype(F32)
            dp_ref[:, gsl] = (dyv * u * s * _dsilu(gate)).astype(dp_ref.dtype)
            do = dyv * _silu(gate)
            dp_ref[:, sl] = (do * s * _dgelu(upre)).astype(dp_ref.dtype)
            ds = do * u
            dbs = dbs + jnp.where(lane == g, jnp.sum(ds, axis=1, keepdims=True), 0.0)
            dsb = ds.astype(BF16)
            dwm_ref[g] += _dot(dsb, vng, _NT)
            dvn_ref[:, sl] = _dot(wm_ref[g], dsb, _TN)
        dbs_ref[...] += dbs
        dvn = dvn_ref[...]
        dlg_ref[...] += jnp.sum(dvn * xh, axis=0, keepdims=True)
        dlb_ref[...] += jnp.sum(dvn, axis=0, keepdims=True)
        dv = _layernorm_rows_bwd(dvn, xh, rstd, gamma)
        dp_ref[:, W:2 * W] = (dv * _dgelu(vpre)).astype(dp_ref.dtype)

    vec = pl.BlockSpec((1, W), lambda r: (0, 0))
    return pl.pallas_call(
        body, name="gmlp_bwd", grid=(T // BLK,),
        in_specs=[pl.BlockSpec((BLK, 3 * W), lambda r: (r, 0)),
                  pl.BlockSpec((BLK, W), lambda r: (r, 0)),
                  pl.BlockSpec((G, BLK, BLK), lambda r: (0, 0, 0)),
                  pl.BlockSpec((BLK, LANES), lambda r: (0, 0)), vec, vec],
        out_specs=[pl.BlockSpec((BLK, 3 * W), lambda r: (r, 0)),
                   pl.BlockSpec((G, BLK, BLK), lambda r: (0, 0, 0)),
                   pl.BlockSpec((BLK, LANES), lambda r: (0, 0)), vec, vec],
        out_shape=[jax.ShapeDtypeStruct((T, 3 * W), BF16), jax.ShapeDtypeStruct((G, BLK, BLK), F32),
                   jax.ShapeDtypeStruct((BLK, LANES), F32),
                   jax.ShapeDtypeStruct((1, W), F32), jax.ShapeDtypeStruct((1, W), F32)],
        scratch_shapes=[pltpu.VMEM((BLK, W), BF16), pltpu.VMEM((BLK, W), F32)],
        compiler_params=_params(("arbitrary",)),
    )(proj, dy, wm, bs_t, ln_g, ln_b)


SUBLANES = 8
SHIFT_ROWS = CONV_HALO + BLK - SUBLANES


def _shift_rows(ext_ref, sh_ref, off):
    for r in range(1, SUBLANES):
        sh_ref[r - 1] = ext_ref[pl.ds(r, SHIFT_ROWS), pl.ds(off, LANES)]


def _rows_from(ext_ref, sh_ref, off, start):
    r = start % SUBLANES
    if r == 0:
        return ext_ref[pl.ds(start, BLK), pl.ds(off, LANES)]
    return sh_ref[r - 1, pl.ds(start - r, BLK), :]


def _conv_taps(ext_ref, sh_ref, cw_ref, off, n_taps, first):
    acc = jnp.zeros((BLK, LANES), F32)
    for k in range(n_taps):
        acc = acc + cw_ref[k:k + 1, pl.ds(off, LANES)] * _rows_from(ext_ref, sh_ref, off, first + k)
    return acc


def _fill_glu_ext(ext_ref, halo_ref, cur_ref, W, first_block):
    y0h = halo_ref[:, :W] * _sigmoid(halo_ref[:, W:])
    ext_ref[0:CONV_HALO, :] = jnp.where(first_block, 0.0, y0h)
    ext_ref[CONV_HALO:CONV_HALO + BLK, :] = cur_ref[:, :W] * _sigmoid(cur_ref[:, W:])


def _conv_specs(S, W):
    per = BLK // CONV_HALO
    cur = pl.BlockSpec((None, BLK, 2 * W), lambda b, i: (b, i, 0))
    halo = pl.BlockSpec((None, CONV_HALO, 2 * W), lambda b, i: (b, jnp.maximum(i * per - 1, 0), 0))
    gate = pl.BlockSpec((None, BLK, W), lambda b, i: (b, i, 2))
    return cur, halo, gate


def _conv_fwd(proj3, cw, cb, ln_g, ln_b, W, exch):
    B, S, _ = proj3.shape
    K = cw.shape[0]
    first = CONV_HALO - (K - 1)
    assert first >= 0

    def body(cur_ref, halo_ref, g_ref, cw_ref, cb_ref, lg_ref, lb_ref, y_ref, ext_ref, y1_ref, sh_ref):
        i = pl.program_id(1)
        _fill_glu_ext(ext_ref, halo_ref, cur_ref, W, i == 0)

        def chan(c, _):
            off = pl.multiple_of(c * LANES, LANES)
            _shift_rows(ext_ref, sh_ref, off)
            y1_ref[:, pl.ds(off, LANES)] = (_conv_taps(ext_ref, sh_ref, cw_ref, off, K, first)
                                            + cb_ref[:, pl.ds(off, LANES)])
            return 0

        lax.fori_loop(0, W // LANES, chan, 0)
        _, _, ln = _layernorm_rows(y1_ref[...], lg_ref[...], lb_ref[...])
        y_ref[...] = (_silu(ln) * _silu(g_ref[...])).astype(BF16)

    cur, halo, gate = _conv_specs(S, W)
    vec = pl.BlockSpec((1, W), lambda b, i: (0, 0))
    (y,), moved = _call_hosting(
        body, exch, "conv_fwd", (B, S // BLK),
        [cur, halo, gate, pl.BlockSpec((K, W), lambda b, i: (0, 0)), vec, vec, vec],
        [pl.BlockSpec((None, BLK, W), lambda b, i: (b, i, 0))], [jax.ShapeDtypeStruct((B, S, W), BF16)],
        [pltpu.VMEM((CONV_HALO + BLK, W), F32), pltpu.VMEM((BLK, W), F32),
         pltpu.VMEM((SUBLANES - 1, SHIFT_ROWS, LANES), F32)],
        (proj3, proj3, proj3, cw, cb, ln_g, ln_b))
    return y, moved


def _conv_bwd1(proj3, dy, cw, cb, ln_g, ln_b, W, exch):
    B, S, _ = proj3.shape
    K = cw.shape[0]
    first = CONV_HALO - (K - 1)

    def body(cur_ref, halo_ref, g_ref, dy_ref, cw_ref, cb_ref, lg_ref, lb_ref,
             dy1_ref, dg_ref, dcw_ref, dcb_ref, dlg_ref, dlb_ref, ext_ref, y1_ref, sh_ref):
        b = pl.program_id(0)
        i = pl.program_id(1)

        @pl.when(jnp.logical_and(b == 0, i == 0))
        def _():
            dcw_ref[...] = jnp.zeros_like(dcw_ref)
            dcb_ref[...] = jnp.zeros_like(dcb_ref)
            dlg_ref[...] = jnp.zeros_like(dlg_ref)
            dlb_ref[...] = jnp.zeros_like(dlb_ref)

        _fill_glu_ext(ext_ref, halo_ref, cur_ref, W, i == 0)

        def chan(c, _):
            off = pl.multiple_of(c * LANES, LANES)
            _shift_rows(ext_ref, sh_ref.at[c], off)
            y1_ref[:, pl.ds(off, LANES)] = (_conv_taps(ext_ref, sh_ref.at[c], cw_ref, off, K, first)
                                            + cb_ref[:, pl.ds(off, LANES)])
            return 0

        lax.fori_loop(0, W // LANES, chan, 0)
        gamma = lg_ref[...]
        xh, rstd, ln = _layernorm_rows(y1_ref[...], gamma, lb_ref[...])
        g = g_ref[...]
        dyv = dy_ref[...].astype(F32)
        dg_ref[...] = (dyv * _silu(ln) * _dsilu(g)).astype(dg_ref.dtype)
        dln = dyv * _silu(g) * _dsilu(ln)
        dlg_ref[...] += jnp.sum(dln * xh, axis=0, keepdims=True)
        dlb_ref[...] += jnp.sum(dln, axis=0, keepdims=True)
        dy1 = _layernorm_rows_bwd(dln, xh, rstd, gamma)
        dy1_ref[...] = dy1
        dcb_ref[...] += jnp.sum(dy1, axis=0, keepdims=True)

        def chan_w(c, _):
            off = pl.multiple_of(c * LANES, LANES)
            d = dy1_ref[:, pl.ds(off, LANES)]
            for k in range(K):
                dcw_ref[k:k + 1, pl.ds(off, LANES)] += jnp.sum(
                    d * _rows_from(ext_ref, sh_ref.at[c], off, first + k), axis=0, keepdims=True)
            return 0

        lax.fori_loop(0, W // LANES, chan_w, 0)

    cur, halo, gate = _conv_specs(S, W)
    vec = pl.BlockSpec((1, W), lambda b, i: (0, 0))
    taps = pl.BlockSpec((K, W), lambda b, i: (0, 0))
    one = pl.BlockSpec((None, BLK, W), lambda b, i: (b, i, 0))
    return _call_hosting(
        body, exch, "conv_bwd1", (B, S // BLK), [cur, halo, gate, one, taps, vec, vec, vec],
        [one, one, taps, vec, vec, vec],
        [jax.ShapeDtypeStruct((B, S, W), F32), jax.ShapeDtypeStruct((B, S, W), BF16),
         jax.ShapeDtypeStruct((K, W), F32)] + [jax.ShapeDtypeStruct((1, W), F32)] * 3,
        [pltpu.VMEM((CONV_HALO + BLK, W), F32), pltpu.VMEM((BLK, W), F32),
         pltpu.VMEM((W // LANES, SUBLANES - 1, SHIFT_ROWS, LANES), F32)],
        (proj3, proj3, proj3, dy, cw, cb, ln_g, ln_b))


def _conv_bwd2(proj3, dy1, dgate, cw_rev, W):
    B, S, _ = proj3.shape
    K = cw_rev.shape[0]
    NQ = S // BLK
    per = BLK // CONV_HALO

    def body(cur_ref, d_ref, dnext_ref, dgate_ref, cw_ref, dp_ref, ext_ref, dy0_ref, sh_ref):
        i = pl.program_id(1)
        ext_ref[0:BLK, :] = d_ref[...]
        ext_ref[BLK:BLK + CONV_HALO, :] = jnp.where(i == NQ - 1, 0.0, dnext_ref[...])

        def chan(c, _):
            off = pl.multiple_of(c * LANES, LANES)
            _shift_rows(ext_ref, sh_ref, off)
            dy0_ref[:, pl.ds(off, LANES)] = _conv_taps(ext_ref, sh_ref, cw_ref, off, K, 0)
            return 0

        lax.fori_loop(0, W // LANES, chan, 0)
        a = cur_ref[:, :W]
        sg = _sigmoid(cur_ref[:, W:])
        dy0 = dy0_ref[...]
        dp_ref[:, 0:W] = (dy0 * sg).astype(dp_ref.dtype)
        dp_ref[:, W:2 * W] = (dy0 * a * sg * (1.0 - sg)).astype(dp_ref.dtype)
        dp_ref[:, 2 * W:3 * W] = dgate_ref[...]

    cur = pl.BlockSpec((None, BLK, 2 * W), lambda b, i: (b, i, 0))
    one = pl.BlockSpec((None, BLK, W), lambda b, i: (b, i, 0))
    nxt = pl.BlockSpec((None, CONV_HALO, W), lambda b, i: (b, jnp.minimum((i + 1) * per, S // CONV_HALO - 1), 0))
    return pl.pallas_call(
        body, name="conv_bwd2", grid=(B, NQ),
        in_specs=[cur, one, nxt, one, pl.BlockSpec((K, W), lambda b, i: (0, 0))],
        out_specs=pl.BlockSpec((None, BLK, 3 * W), lambda b, i: (b, i, 0)),
        out_shape=jax.ShapeDtypeStruct((B, S, 3 * W), BF16),
        scratch_shapes=[pltpu.VMEM((BLK + CONV_HALO, W), F32), pltpu.VMEM((BLK, W), F32),
                        pltpu.VMEM((SUBLANES - 1, SHIFT_ROWS, LANES), F32)],
        compiler_params=_params(("parallel", "parallel")),
    )(proj3, dy1, dy1, dgate, cw_rev)


def _pack(arrays):
    flat = jnp.concatenate([a.astype(F32).reshape(-1) for a in arrays])
    n = flat.shape[0]
    pad = (-n) % (8 * LANES)
    if pad:
        flat = jnp.concatenate([flat, jnp.zeros((pad,), F32)])
    return flat.reshape(-1, LANES)


def _unpack(packed, shapes, lead=()):
    flat = packed.reshape(lead + (-1,))
    out, off = [], 0
    for shp in shapes:
        n = math.prod(shp)
        out.append(flat[..., off:off + n].reshape(lead + tuple(shp)))
        off += n
    return out


def _cols_from_dev(g):
    g = jnp.moveaxis(g, 0, -2)
    return g.reshape(g.shape[:-2] + (g.shape[-2] * g.shape[-1],))


def _my_cols(full, me):
    n8 = full.shape[-1] // N_DEV
    return lax.dynamic_slice_in_dim(full, me * n8, n8, axis=full.ndim - 1)


def kernel(x, a_norm, a_w_in, a_w_out, b_norm, b_w_in, b_v_ln_g, b_v_ln_b, b_w_s, b_b_s, b_w_out, c_norm, c_w_in, c_conv_w, c_conv_b, c_ln_g, c_ln_b, c_w_out, d_norm, d_w_in, d_b_f, d_w_out, final_norm, loss_target, m_a_norm, m_a_w_in, m_a_w_out, m_b_norm, m_b_w_in, m_b_v_ln_g, m_b_v_ln_b, m_b_w_s, m_b_b_s, m_b_w_out, m_c_norm, m_c_w_in, m_c_conv_w, m_c_conv_b, m_c_ln_g, m_c_ln_b, m_c_w_out, m_d_norm, m_d_w_in, m_d_b_f, m_d_w_out, m_final_norm, v_a_norm, v_a_w_in, v_a_w_out, v_b_norm, v_b_w_in, v_b_v_ln_g, v_b_v_ln_b, v_b_w_s, v_b_b_s, v_b_w_out, v_c_norm, v_c_w_in, v_c_conv_w, v_c_conv_b, v_c_ln_g, v_c_ln_b, v_c_w_out, v_d_norm, v_d_w_in, v_d_b_f, v_d_w_out, v_final_norm):
    B, S, D = x.shape
    T = B * S
    xi, yi, ci = _me()
    me = 4 * xi + 2 * yi + ci

    G = b_w_s.shape[1]
    KC = c_conv_w.shape[1]
    H_D = d_b_f.shape[1]
    W_A = a_w_out.shape[1] * N_DEV
    W_B = b_w_out.shape[1] * N_DEV
    W_C = c_w_out.shape[1] * N_DEV
    W_D = d_w_out.shape[1] * N_DEV
    N_D = d_w_in.shape[2] * N_DEV
    N_D_PAD = -(-N_D // (3 * LANES)) * (3 * LANES)

    big_names = ["a_w_in", "a_w_out", "b_w_in", "b_w_out", "c_w_in", "c_w_out", "d_w_in", "d_w_out"]
    big_w = dict(a_w_in=a_w_in[0], a_w_out=a_w_out[0], b_w_in=b_w_in[0], b_w_out=b_w_out[0],
                 c_w_in=c_w_in[0], c_w_out=c_w_out[0], d_w_in=d_w_in[0], d_w_out=d_w_out[0])
    small_sharded = [b_norm, b_v_ln_g, b_v_ln_b, c_norm, c_conv_w, c_conv_b, c_ln_g, c_ln_b, d_norm]
    first_names, later_names, last_names = big_names[:1], big_names[1:6], big_names[6:]
    gathered = _GatherViaSibling(
        [big_w[n].astype(BF16) for n in first_names] + [_pack(small_sharded)]).run("gather_first")
    wg = dict(zip(first_names, gathered[:-1]))
    (b_norm_f, b_lg_f, b_lb_f, c_norm_f, c_cw_f, c_cb_f, c_lg_f, c_lb_f, d_norm_f) = [
        _cols_from_dev(t) for t in _unpack(gathered[-1], [s.shape for s in small_sharded], lead=(N_DEV,))]
    c_cw_f = c_cw_f[0]

    wm = jnp.tril(b_w_s[0]).astype(BF16)
    bs_t = jnp.pad(b_b_s[0].T, ((0, 0), (0, LANES - G)))

    x0 = x.reshape(T, D)
    h_a = _rmsnorm_fwd(x0, a_norm, "rms_a")
    proj_a = _mm_w_dev(h_a, wg["a_w_in"], "proj_a").reshape(B, S, 4 * W_A)
    (o_a, y_a), later = _sb_fwd(proj_a, W_A, SB_HEADS,
                                _Exchange([big_w[n].astype(BF16) for n in later_names], ["gather"] * len(later_names)))
    wg.update(zip(later_names, later))
    a_w_out_f = wg["a_w_out"].reshape(W_A, D)
    b_w_out_f = wg["b_w_out"].reshape(W_B, D)
    c_w_out_f = wg["c_w_out"].reshape(W_C, D)
    y_a = y_a.reshape(T, W_A)
    x1, h_b = _mm(y_a, a_w_out_f, "nn", T, D, W_A, F32, "out_a", 512, D, W_A, res=x0, norm_gain=b_norm_f)
    proj_b = _mm_w_dev(h_b, wg["b_w_in"], "proj_b")
    y_b = _gmlp_fwd(proj_b, wm, bs_t, b_lg_f, b_lb_f, W_B)
    x2, h_c = _mm(y_b, b_w_out_f, "nn", T, D, W_B, F32, "out_b", 512, D, W_B, res=x1, norm_gain=c_norm_f)
    proj_c = _mm_w_dev(h_c, wg["c_w_in"], "proj_c").reshape(B, S, 3 * W_C)
    y_c, last = _conv_fwd(proj_c, c_cw_f, c_cb_f, c_lg_f, c_lb_f, W_C,
                          _Exchange([big_w[n].astype(BF16) for n in last_names], ["gather"] * len(last_names)))
    wg.update(zip(last_names, last))
    d_w_out_f = wg["d_w_out"].reshape(W_D, D)
    d_w_in_f = jnp.pad(_cols_from_dev(wg["d_w_in"]), ((0, 0), (0, N_D_PAD - N_D)))
    y_c = y_c.reshape(T, W_C)
    x3, h_d = _mm(y_c, c_w_out_f, "nn", T, D, W_C, F32, "out_c", 512, D, W_C, res=x2, norm_gain=d_norm_f)
    proj_d = _mm(h_d, d_w_in_f, "nn", T, N_D_PAD, D, F32, "proj_d", 1024, 384, D).reshape(B, S, N_D_PAD)
    f_t = jnp.swapaxes(proj_d[:, :, 4 * W_D:4 * W_D + H_D], 1, 2)
    b_f_col = d_b_f.reshape(H_D, 1)
    cum_t = _fox_gate_fwd(f_t, b_f_col)
    o_d, y_d, lse_d = _fox_fwd(proj_d, cum_t, W_D, H_D)
    y_d = y_d.reshape(T, W_D)
    x4 = _mm(y_d, d_w_out_f, "nn", T, D, W_D, F32, "out_d", 512, D, W_D, res=x3)

    loss_part, dx, g_final = _loss_head(x4, final_norm.reshape(1, D), loss_target.reshape(T, D))
    loss = lax.psum(loss_part[0, 0], MESH_AXES)

    dy_d = _mm(dx, d_w_out_f, "nt", T, W_D, D, BF16, "dy_d", 512, W_D, D).reshape(B, S, W_D)
    gw_d_out = _mm(y_d, dx, "tn", W_D, D, T, BF16, "gw_d_out", W_D, D, 512).reshape(N_DEV, W_D // N_DEV, D)
    dproj_d, dcum = _fox_bwd(proj_d, cum_t, o_d, lse_d, dy_d, W_D, H_D)
    df_t, g_b_f = _fox_gate_bwd(dcum.reshape(B, H_D, S), f_t, b_f_col)
    F_PAD = N_D_PAD - 4 * W_D
    df = jnp.pad(jnp.swapaxes(df_t, 1, 2), ((0, 0), (0, 0), (0, F_PAD - H_D))).reshape(T, F_PAD)
    tc, tr = min(512, W_D), min(1024, S)
    gw_main = _mm(h_d, dproj_d, "tn", D, 4 * W_D, T, BF16, "gw_d_in", D, tc, tr,
                  b_spec=_sectioned_spec(dproj_d, tr, tc, 2, 1))
    gw_f = _mm(h_d, df, "tn", D, F_PAD, T, BF16, "gw_d_in_f", D, F_PAD, 512)
    gw_d_in = jnp.moveaxis(
        jnp.concatenate([gw_main, gw_f], axis=1)[:, :N_D].reshape(D, N_DEV, N_D // N_DEV), 1, 0)
    dh = _mm(dproj_d, d_w_in_f, "nt", T, D, 4 * W_D, F32, "dh_d", tr, D, tc,
             a_spec=_sectioned_spec(dproj_d, tr, tc, 0, 2))
    dh = _mm(df, d_w_in_f[:, 4 * W_D:], "nt", T, D, F_PAD, F32, "dh_d_f", 512, D, F_PAD, res=dh)
    dx, g_d_norm = _rmsnorm_bwd(x3, d_norm_f, dh, dx, "rms_bwd_d")

    dy_c = _mm(dx, c_w_out_f, "nt", T, W_C, D, BF16, "dy_c", 512, W_C, D).reshape(B, S, W_C)
    gw_c_out = _mm(y_c, dx, "tn", W_C, D, T, BF16, "gw_c_out", 1024, D, 512).reshape(N_DEV, W_C // N_DEV, D)
    (dy1, dgate_c, g_c_cw, g_c_cb, g_c_lg, g_c_lb), parts_d = _conv_bwd1(
        proj_c, dy_c, c_cw_f, c_cb_f, c_lg_f, c_lb_f, W_C, _Exchange([gw_d_in, gw_d_out], ["scatter"] * 2))
    dproj_c = _conv_bwd2(proj_c, dy1, dgate_c, c_cw_f[::-1], W_C).reshape(T, 3 * W_C)
    gw_c_in = _mm_grad_dev(h_c, dproj_c, "gw_c_in")
    dx, g_c_norm, _ = _mm_wT_dev(dproj_c, wg["c_w_in"], "dh_c", norm_bwd=(x2, c_norm_f, dx))

    dy_b = _mm(dx, b_w_out_f, "nt", T, W_B, D, BF16, "dy_b", 512, W_B, D)
    gw_b_out = _mm(y_b, dx, "tn", W_B, D, T, BF16, "gw_b_out", 1024, D, 512).reshape(N_DEV, W_B // N_DEV, D)
    dproj_b, g_wm, g_bs_t, g_b_lg, g_b_lb = _gmlp_bwd(proj_b, dy_b, wm, bs_t, b_lg_f, b_lb_f, W_B)
    g_b_w_s = jnp.tril(g_wm)
    g_b_b_s = g_bs_t[:, :G].T
    gw_b_in = _mm_grad_dev(h_b, dproj_b, "gw_b_in")
    dx, g_b_norm, _ = _mm_wT_dev(dproj_b, wg["b_w_in"], "dh_b", norm_bwd=(x1, b_norm_f, dx))

    dy_a = _mm(dx, a_w_out_f, "nt", T, W_A, D, BF16, "dy_a", 512, W_A, D).reshape(B, S, W_A)
    gw_a_out = _mm(y_a, dx, "tn", W_A, D, T, BF16, "gw_a_out", W_A, D, 512).reshape(N_DEV, W_A // N_DEV, D)
    small_full = [g_b_norm, g_b_lg, g_b_lb, g_b_b_s, g_c_norm, g_c_cw, g_c_cb, g_c_lg, g_c_lb,
                  g_d_norm, g_b_f, g_final]
    dproj_a, parts_s = _sb_bwd(
        proj_a, o_a, dy_a, W_A, SB_HEADS,
        _Exchange([gw_c_in, gw_c_out, gw_b_in, gw_b_out, gw_a_out, _pack(small_full), g_b_w_s.reshape(-1, LANES)],
                  ["scatter"] * 5 + ["gather"] * 2))
    gw_a_in = _mm_grad_dev(h_a, dproj_a, "gw_a_in")
    dx, g_a_norm, parts_a = _mm_wT_dev(dproj_a, wg["a_w_in"], "dh_a", exch=_Exchange([gw_a_in], ["scatter"]),
                                       norm_bwd=(x0, a_norm, dx))
    grad_x = dx.reshape(B, S, D)

    (parts_n,) = _Exchange([_pack([g_a_norm])], ["gather"]).run("exchange_last")
    big_parts = dict(a_w_in=parts_a[0], a_w_out=parts_s[4], b_w_in=parts_s[2], b_w_out=parts_s[3],
                     c_w_in=parts_s[0], c_w_out=parts_s[1], d_w_in=parts_d[0], d_w_out=parts_d[1])
    (s_b_norm, s_b_lg, s_b_lb, s_b_b_s, s_c_norm, s_c_cw, s_c_cb, s_c_lg, s_c_lb,
     s_d_norm, s_b_f, s_final) = _unpack(_sum_parts(parts_s[5], "sum_small"), [g.shape for g in small_full])
    (s_a_norm,) = _unpack(_sum_parts(parts_n, "sum_a_norm"), [g_a_norm.shape])

    weights = dict(a_norm=a_norm, a_w_in=a_w_in, a_w_out=a_w_out, b_norm=b_norm, b_w_in=b_w_in, b_v_ln_g=b_v_ln_g,
                   b_v_ln_b=b_v_ln_b, b_w_s=b_w_s, b_b_s=b_b_s, b_w_out=b_w_out, c_norm=c_norm, c_w_in=c_w_in,
                   c_conv_w=c_conv_w, c_conv_b=c_conv_b, c_ln_g=c_ln_g, c_ln_b=c_ln_b, c_w_out=c_w_out,
                   d_norm=d_norm, d_w_in=d_w_in, d_b_f=d_b_f, d_w_out=d_w_out, final_norm=final_norm)
    mom_m = dict(a_norm=m_a_norm, a_w_in=m_a_w_in, a_w_out=m_a_w_out, b_norm=m_b_norm, b_w_in=m_b_w_in,
                 b_v_ln_g=m_b_v_ln_g, b_v_ln_b=m_b_v_ln_b, b_w_s=m_b_w_s, b_b_s=m_b_b_s, b_w_out=m_b_w_out,
                 c_norm=m_c_norm, c_w_in=m_c_w_in, c_conv_w=m_c_conv_w, c_conv_b=m_c_conv_b, c_ln_g=m_c_ln_g,
                 c_ln_b=m_c_ln_b, c_w_out=m_c_w_out, d_norm=m_d_norm, d_w_in=m_d_w_in, d_b_f=m_d_b_f,
                 d_w_out=m_d_w_out, final_norm=m_final_norm)
    mom_v = dict(a_norm=v_a_norm, a_w_in=v_a_w_in, a_w_out=v_a_w_out, b_norm=v_b_norm, b_w_in=v_b_w_in,
                 b_v_ln_g=v_b_v_ln_g, b_v_ln_b=v_b_v_ln_b, b_w_s=v_b_w_s, b_b_s=v_b_b_s, b_w_out=v_b_w_out,
                 c_norm=v_c_norm, c_w_in=v_c_w_in, c_conv_w=v_c_conv_w, c_conv_b=v_c_conv_b, c_ln_g=v_c_ln_g,
                 c_ln_b=v_c_ln_b, c_w_out=v_c_w_out, d_norm=v_d_norm, d_w_in=v_d_w_in, d_b_f=v_d_b_f,
                 d_w_out=v_d_w_out, final_norm=v_final_norm)
    order = list(weights)
    grads, deltas, new_m, new_v = {}, {}, {}, {}

    for n in big_names:
        part = big_parts[n]
        shp = weights[n].shape
        R, C = shp[1], shp[2]
        res = _adamw(part, weights[n].reshape(R, C), mom_m[n].reshape(R, C), mom_v[n].reshape(R, C), "adamw_" + n)
        grads[n], deltas[n], new_m[n], new_v[n] = [r.reshape(shp) for r in res]

    res = _adamw(parts_s[6], b_w_s.reshape(-1, LANES), m_b_w_s.reshape(-1, LANES), v_b_w_s.reshape(-1, LANES),
                 "adamw_b_w_s")
    grads["b_w_s"], deltas["b_w_s"], new_m["b_w_s"], new_v["b_w_s"] = [r.reshape(b_w_s.shape) for r in res]

    small_g = dict(
        a_norm=s_a_norm, b_norm=_my_cols(s_b_norm, me), b_v_ln_g=_my_cols(s_b_lg, me),
        b_v_ln_b=_my_cols(s_b_lb, me), b_b_s=s_b_b_s[None], c_norm=_my_cols(s_c_norm, me),
        c_conv_w=_my_cols(s_c_cw, me)[None], c_conv_b=_my_cols(s_c_cb, me), c_ln_g=_my_cols(s_c_lg, me),
        c_ln_b=_my_cols(s_c_lb, me), d_norm=_my_cols(s_d_norm, me), d_b_f=s_b_f.reshape(1, H_D),
        final_norm=s_final.reshape(D))
    small_names = list(small_g)
    sg_p = _pack([small_g[n] for n in small_names])
    res = _adamw(sg_p[None], _pack([weights[n] for n in small_names]), _pack([mom_m[n] for n in small_names]),
                 _pack([mom_v[n] for n in small_names]), "adamw_small")
    shapes = [weights[n].shape for n in small_names]
    for dst, r in zip((grads, deltas, new_m, new_v), res):
        for n, val in zip(small_names, _unpack(r, shapes)):
            dst[n] = val

    return (loss, grad_x, *[grads[n] for n in order], *[deltas[n] for n in order],
            *[new_m[n] for n in order], *[new_v[n] for n in order])
```

```python
import functools
import math

import jax
import jax.numpy as jnp
from jax import lax
from jax.experimental import pallas as pl
from jax.experimental.pallas import tpu as pltpu

F32 = jnp.float32
BF16 = jnp.bfloat16

EPS = 1e-6
SB_HEADS = 16
CONV_HALO = 32
BLK = 128
ATT_TK = 256
ATT_TQ = 512
ATT_GP = 2
LANES = 128
N_DEV = 8
MESH_AXES = ("x", "y", "c")

ADAM_LR = 0.001
ADAM_B1 = 0.9
ADAM_B2 = 0.999
ADAM_EPS = 1e-08
ADAM_WD = 0.01
ADAM_STEP = 10

VMEM_LIMIT = 56 * 1024 * 1024
NEG_BIG = -1e30

_NN = (((1,), (0,)), ((), ()))
_NT = (((1,), (1,)), ((), ()))
_TN = (((0,), (0,)), ((), ()))


def _dot(a, b, dims=_NN):
    return lax.dot_general(a, b, dims, preferred_element_type=F32)


def _split_dot(x, m):
    hi = x.astype(BF16)
    lo = (x - hi.astype(F32)).astype(BF16)
    return _dot(hi, m) + _dot(lo, m)


def _split3_dot(x, m):
    hi = x.astype(BF16)
    r1 = x - hi.astype(F32)
    mid = r1.astype(BF16)
    lo = (r1 - mid.astype(F32)).astype(BF16)
    return _dot(hi, m) + _dot(mid, m) + _dot(lo, m)


def _params(sem=None):
    kw = dict(vmem_limit_bytes=VMEM_LIMIT)
    if sem is not None:
        kw["dimension_semantics"] = sem
    return pltpu.CompilerParams(**kw)


def _sigmoid(x):
    return jax.nn.sigmoid(x)


def _silu(x):
    return x * _sigmoid(x)


def _dsilu(x):
    s = _sigmoid(x)
    return s * (1.0 + x * (1.0 - s))


_GELU_C = math.sqrt(2.0 / math.pi)


def _gelu(x):
    return 0.5 * x * (1.0 + jnp.tanh(_GELU_C * (x + 0.044715 * x * x * x)))


def _dgelu(x):
    th = jnp.tanh(_GELU_C * (x + 0.044715 * x * x * x))
    return 0.5 * (1.0 + th) + 0.5 * x * (1.0 - th * th) * _GELU_C * (1.0 + 3.0 * 0.044715 * x * x)


def _mm(a, b, mode, M, N, K, out_dtype, name, tm, tn, tk, a_spec=None, b_spec=None, o_spec=None, out_shape=None,
        exch=None, res=None, norm_gain=None, norm_bwd=None):
    tm, tn, tk = min(tm, M), min(tn, N), min(tk, K)
    assert M % tm == 0 and N % tn == 0 and K % tk == 0, (name, M, N, K, tm, tn, tk)
    nk = K // tk
    assert norm_gain is None or (nk == 1 and tn == N and exch is None)
    dims = {"nn": _NN, "nt": _NT, "tn": _TN}[mode]
    if a_spec is None:
        a_spec = (pl.BlockSpec((tk, tm), lambda i, j, k: (k, i)) if mode == "tn"
                  else pl.BlockSpec((tm, tk), lambda i, j, k: (i, k)))
    if b_spec is None:
        b_spec = (pl.BlockSpec((tn, tk), lambda i, j, k: (j, k)) if mode == "nt"
                  else pl.BlockSpec((tk, tn), lambda i, j, k: (k, j)))
    if o_spec is None:
        o_spec = pl.BlockSpec((tm, tn), lambda i, j, k: (i, j))
    if out_shape is None:
        out_shape = (M, N)

    def body(a_ref, b_ref, *rest):
        res_ref = rest[0] if res is not None else None
        if nk == 1:
            d = _dot(a_ref[...].astype(BF16), b_ref[...].astype(BF16), dims)
            r = d if res_ref is None else res_ref[...].astype(F32) + d
            if norm_gain is None:
                rest[-1][...] = r.astype(rest[-1].dtype)
            else:
                g_ref, o_ref, h_ref = rest[-3:]
                o_ref[...] = r.astype(o_ref.dtype)
                scale = lax.rsqrt(jnp.mean(r * r, axis=-1, keepdims=True) + EPS)
                h_ref[...] = (r * scale * g_ref[...]).astype(BF16)
            return
        i = pl.program_id(0)
        k = pl.program_id(2)
        if norm_bwd is not None:
            x_ref, g_ref, dres_ref, o_ref, dg_ref, acc_ref = rest[-6:]

            @pl.when(jnp.logical_and(i == 0, k == 0))
            def _():
                dg_ref[...] = jnp.zeros_like(dg_ref)
        else:
            o_ref, acc_ref = rest[-2:]

        @pl.when(k == 0)
        def _():
            acc_ref[...] = jnp.zeros_like(acc_ref) if res_ref is None else res_ref[...].astype(F32)

        acc_ref[...] += _dot(a_ref[...].astype(BF16), b_ref[...].astype(BF16), dims)

        @pl.when(k == nk - 1)
        def _():
            if norm_bwd is None:
                o_ref[...] = acc_ref[...].astype(o_ref.dtype)
            else:
                dh = acc_ref[...]
                xv = x_ref[...]
                r = lax.rsqrt(jnp.mean(xv * xv, axis=-1, keepdims=True) + EPS)
                xh = xv * r
                dxh = dh * g_ref[...]
                o_ref[...] = dres_ref[...] + r * (dxh - xh * jnp.mean(dxh * xh, axis=-1, keepdims=True))
                dg_ref[...] += jnp.sum(dh * xh, axis=0, keepdims=True)

    in_specs, args = [a_spec, b_spec], (a, b)
    if res is not None:
        in_specs, args = in_specs + [o_spec], args + (res,)
    scratch = [pltpu.VMEM((tm, tn), F32)] if nk > 1 else []
    if norm_bwd is not None:
        assert nk > 1 and tn == N and norm_gain is None
        vec = pl.BlockSpec((1, N), lambda i, j, k: (0, 0))
        x_in, g_in, dres_in = norm_bwd
        (dx, dg), moved = _call_hosting(
            body, exch, name, (M // tm, 1, nk), in_specs + [o_spec, vec, o_spec], [o_spec, vec],
            [jax.ShapeDtypeStruct((M, N), F32), jax.ShapeDtypeStruct((1, N), F32)], scratch,
            args + (x_in, g_in, dres_in))
        return dx, dg, moved
    if norm_gain is not None:
        return pl.pallas_call(
            body, name=name, grid=(M // tm, N // tn, nk),
            in_specs=in_specs + [pl.BlockSpec((1, N), lambda i, j, k: (0, 0))], out_specs=[o_spec, o_spec],
            out_shape=[jax.ShapeDtypeStruct(out_shape, out_dtype), jax.ShapeDtypeStruct(out_shape, BF16)],
            compiler_params=_params(("parallel", "parallel", "arbitrary")),
        )(*args, norm_gain)
    if exch is None:
        return pl.pallas_call(
            body, name=name, grid=(M // tm, N // tn, nk),
            in_specs=in_specs, out_specs=o_spec,
            out_shape=jax.ShapeDtypeStruct(out_shape, out_dtype),
            scratch_shapes=scratch,
            compiler_params=_params(("parallel", "parallel", "arbitrary")),
        )(*args)
    (out,), moved = _call_hosting(
        body, exch, name, (M // tm, N // tn, nk), in_specs, [o_spec],
        [jax.ShapeDtypeStruct(out_shape, out_dtype)], scratch, args)
    return out, moved


def _mm_w_dev(a, w3, name, out_dtype=F32, tm=1024):
    M, K = a.shape
    n8 = w3.shape[2]
    tn = n8 if n8 <= 768 else 512
    per = n8 // tn
    b_spec = pl.BlockSpec((None, K, tn), lambda i, j, k: (j // per, 0, j % per))
    return _mm(a, w3, "nn", M, N_DEV * n8, K, out_dtype, name, tm, tn, K, b_spec=b_spec)


def _sectioned_spec(d4, t_rows, t_cols, rows_axis, cols_axis):
    _, _, S, W = d4.shape
    assert S % t_rows == 0 and W % t_cols == 0
    rb, cb = S // t_rows, W // t_cols

    def index(*g):
        r, c = g[rows_axis], g[cols_axis]
        return (r // rb, c // cb, r % rb, c % cb)

    return pl.BlockSpec((None, None, t_rows, t_cols), index)


def _mm_wT_dev(a, w3, name, out_dtype=F32, tm=1024, exch=None, norm_bwd=None):
    K, n8 = w3.shape[1], w3.shape[2]
    tk = n8 if n8 <= 768 else 512
    per = n8 // tk
    b_spec = pl.BlockSpec((None, K, tk), lambda i, j, k: (k // per, 0, k % per))
    if a.ndim == 4:
        M, N = a.shape[0] * a.shape[2], a.shape[1] * a.shape[3]
        tm = min(tm, a.shape[2])
        a_spec = _sectioned_spec(a, tm, tk, 0, 2)
    else:
        (M, N), a_spec = a.shape, None
    return _mm(a, w3, "nt", M, K, N, out_dtype, name, tm, K, tk, a_spec=a_spec, b_spec=b_spec, exch=exch,
               norm_bwd=norm_bwd)


def _mm_grad_dev(h, d, name, out_dtype=BF16):
    T, M = h.shape
    N = d.shape[1] * d.shape[3] if d.ndim == 4 else d.shape[1]
    n8 = N // N_DEV
    tn = n8 if n8 <= 768 else 512
    per = n8 // tn
    tm = min(M, 1024)
    o_spec = pl.BlockSpec((None, tm, tn), lambda i, j, k: (j // per, i, j % per))
    tk = min(1024, d.shape[2] if d.ndim == 4 else T)
    b_spec = _sectioned_spec(d, tk, tn, 2, 1) if d.ndim == 4 else None
    return _mm(h, d, "tn", M, N, T, out_dtype, name, tm, tn, tk, b_spec=b_spec, o_spec=o_spec,
               out_shape=(N_DEV, M, n8))


def _me():
    x, y, c = lax.axis_index("x"), lax.axis_index("y"), lax.axis_index("c")
    return x, y, c


def _peer(r):
    x, y, c = _me()
    px = 1 - x if (r >> 2) & 1 else x
    py = 1 - y if (r >> 1) & 1 else y
    pc = 1 - c if r & 1 else c
    return (px, py, pc), 4 * px + 2 * py + pc


class _Exchange:
    def __init__(self, arrays, kinds):
        self.arrays, self.kinds, self.n = list(arrays), list(kinds), len(arrays)
        self.out_shapes = [
            jax.ShapeDtypeStruct((N_DEV,) + a.shape if kind == "gather" else a.shape, a.dtype)
            for a, kind in zip(arrays, kinds)]
        self.specs = [pl.BlockSpec(memory_space=pl.ANY)] * self.n
        self.sems = [pltpu.SemaphoreType.DMA((self.n, N_DEV - 1)), pltpu.SemaphoreType.DMA((self.n, N_DEV - 1)),
                     pltpu.SemaphoreType.DMA((self.n,))]

    def _copies(self, ins, outs, sems, receiving):
        send_sems, recv_sems, local_sems = sems
        x, y, c = _me()
        me = 4 * x + 2 * y + c

        def src(k, pid):
            return ins[k] if self.kinds[k] == "gather" else ins[k].at[pid]

        local = [pltpu.make_async_copy(src(k, me), outs[k].at[me], local_sems.at[k]) for k in range(self.n)]
        remote = []
        for r in range(1, N_DEV):
            peer, pid = _peer(r)
            for k in range(self.n):
                remote.append(pltpu.make_async_remote_copy(
                    src_ref=src(k, pid), dst_ref=outs[k].at[pid if receiving else me],
                    send_sem=send_sems.at[k, r - 1], recv_sem=recv_sems.at[k, r - 1],
                    device_id=peer, device_id_type=pl.DeviceIdType.MESH))
        return local, remote

    def start(self, ins, outs, sems):
        local, remote = self._copies(ins, outs, sems, False)
        for cp in local + remote:
            cp.start()

    def wait(self, ins, outs, sems):
        local, remote = self._copies(ins, outs, sems, True)
        for cp in remote:
            cp.wait_recv()
        for cp in remote:
            cp.wait_send()
        for cp in local:
            cp.wait()

    def run(self, name):
        n = self.n

        def body(*refs):
            ins, outs, sems = refs[:n], refs[n:2 * n], refs[2 * n:]
            self.start(ins, outs, sems)
            self.wait(ins, outs, sems)

        return pl.pallas_call(
            body, name=name, in_specs=self.specs, out_specs=self.specs, out_shape=self.out_shapes,
            scratch_shapes=self.sems,
        )(*self.arrays)


class _GatherViaSibling(_Exchange):
    ICI = (2, 4, 6)

    def __init__(self, arrays):
        super().__init__(arrays, ["gather"] * len(arrays))

    def _copy(self, ins, outs, sems, k, column, block, to, from_input=False):
        return pltpu.make_async_remote_copy(
            src_ref=ins[k] if from_input else outs[k].at[block], dst_ref=outs[k].at[block],
            send_sem=sems[0].at[k, column], recv_sem=sems[1].at[k, column],
            device_id=to, device_id_type=pl.DeviceIdType.MESH)

    def start(self, ins, outs, sems):
        x, y, c = _me()
        me = 4 * x + 2 * y + c
        for k in range(self.n):
            pltpu.make_async_copy(ins[k], outs[k].at[me], sems[2].at[k]).start()
            self._copy(ins, outs, sems, k, 0, me, _peer(1)[0], True).start()
            for j, r in enumerate(self.ICI):
                self._copy(ins, outs, sems, k, 1 + j, me, _peer(r)[0], True).start()

    def wait(self, ins, outs, sems):
        x, y, c = _me()
        me = 4 * x + 2 * y + c
        sibling, sibling_id = _peer(1)
        for j, r in enumerate(self.ICI):
            peer, pid = _peer(r)
            for k in range(self.n):
                self._copy(ins, outs, sems, k, 1 + j, pid, peer).wait_recv()
                self._copy(ins, outs, sems, k, 4 + j, pid, sibling).start()
        for k in range(self.n):
            self._copy(ins, outs, sems, k, 0, sibling_id, sibling).wait_recv()
            for j, r in enumerate(self.ICI):
                self._copy(ins, outs, sems, k, 4 + j, _peer(r ^ 1)[1], sibling).wait_recv()
            for column in range(N_DEV - 1):
                self._copy(ins, outs, sems, k, column, me, sibling).wait_send()
            pltpu.make_async_copy(ins[k], outs[k].at[me], sems[2].at[k]).wait()


def _call_hosting(body, exch, name, grid, in_specs, out_specs, out_shape, scratch_shapes, args):
    if exch is None:
        res = pl.pallas_call(
            body, name=name, grid=grid, in_specs=list(in_specs), out_specs=list(out_specs),
            out_shape=list(out_shape), scratch_shapes=list(scratch_shapes),
            compiler_params=_params(("arbitrary",) * len(grid)))(*args)
        return res, []
    n_in, n_out, n_scr, nc = len(in_specs), len(out_specs), len(scratch_shapes), exch.n

    def full_body(*refs):
        ins, refs = refs[:n_in], refs[n_in:]
        cins, refs = refs[:nc], refs[nc:]
        outs, refs = refs[:n_out], refs[n_out:]
        couts, refs = refs[:nc], refs[nc:]
        scr, sems = refs[:n_scr], refs[n_scr:]
        ids = [pl.program_id(a) for a in range(len(grid))]
        first = functools.reduce(jnp.logical_and, [i == 0 for i in ids])
        last = functools.reduce(jnp.logical_and, [i == g - 1 for i, g in zip(ids, grid)])

        @pl.when(first)
        def _():
            exch.start(cins, couts, sems)

        body(*ins, *outs, *scr)

        @pl.when(last)
        def _():
            exch.wait(cins, couts, sems)

    res = pl.pallas_call(
        full_body, name=name, grid=grid,
        in_specs=list(in_specs) + exch.specs, out_specs=list(out_specs) + exch.specs,
        out_shape=list(out_shape) + exch.out_shapes,
        scratch_shapes=list(scratch_shapes) + exch.sems,
        compiler_params=_params(("arbitrary",) * len(grid)),
    )(*args, *exch.arrays)
    return res[:n_out], res[n_out:]


def _rmsnorm_fwd(x, g, name):
    T, D = x.shape
    tr = min(256, T)

    def body(x_ref, g_ref, h_ref):
        xv = x_ref[...]
        r = lax.rsqrt(jnp.mean(xv * xv, axis=-1, keepdims=True) + EPS)
        h_ref[...] = (xv * r * g_ref[...]).astype(BF16)

    return pl.pallas_call(
        body, name=name, grid=(T // tr,),
        in_specs=[pl.BlockSpec((tr, D), lambda i: (i, 0)), pl.BlockSpec((1, D), lambda i: (0, 0))],
        out_specs=pl.BlockSpec((tr, D), lambda i: (i, 0)),
        out_shape=jax.ShapeDtypeStruct((T, D), BF16),
        compiler_params=_params(("parallel",)),
    )(x, g)


def _loss_head(x, g, target):
    T, D = x.shape
    tr = min(256, T)

    def body(x_ref, g_ref, t_ref, loss_ref, dx_ref, dg_ref):
        i = pl.program_id(0)
        xv = x_ref[...]
        gv = g_ref[...]
        r = lax.rsqrt(jnp.mean(xv * xv, axis=-1, keepdims=True) + EPS)
        xh = xv * r
        diff = xh * gv - t_ref[...]
        dy = diff * (1.0 / D)
        dxh = dy * gv
        dx_ref[...] = r * (dxh - xh * jnp.mean(dxh * xh, axis=-1, keepdims=True))

        @pl.when(i == 0)
        def _():
            dg_ref[...] = jnp.zeros_like(dg_ref)
            loss_ref[...] = jnp.zeros_like(loss_ref)

        dg_ref[...] += jnp.sum(dy * xh, axis=0, keepdims=True)
        part = jnp.sum(jnp.sum(diff * diff, axis=1, keepdims=True), axis=0, keepdims=True)
        loss_ref[...] += (0.5 / D) * part

    row = pl.BlockSpec((tr, D), lambda i: (i, 0))
    vec = pl.BlockSpec((1, D), lambda i: (0, 0))
    return pl.pallas_call(
        body, name="loss_head", grid=(T // tr,),
        in_specs=[row, vec, row],
        out_specs=[pl.BlockSpec((1, 1), lambda i: (0, 0)), row, vec],
        out_shape=[jax.ShapeDtypeStruct((1, 1), F32), jax.ShapeDtypeStruct((T, D), F32),
                   jax.ShapeDtypeStruct((1, D), F32)],
        compiler_params=_params(("arbitrary",)),
    )(x, g, target)


ELEMS_PER_STEP = 1 << 20


def _row_tile(R, per_row):
    best = None
    for tr in range(8, R + 1, 8):
        if R % tr == 0 and tr * per_row <= ELEMS_PER_STEP:
            best = tr
    return best if best is not None else R


def _adamw(parts, w, m, v, name):
    P, R, C = parts.shape
    tr = _row_tile(R, P * C)

    def body(p_ref, w_ref, m_ref, v_ref, g_out, d_out, m_out, v_out):
        g = p_ref[0].astype(F32)
        for p in range(1, P):
            g = g + p_ref[p].astype(F32)
        wv = w_ref[...]
        mn = ADAM_B1 * m_ref[...] + (1.0 - ADAM_B1) * g
        vn = ADAM_B2 * v_ref[...] + (1.0 - ADAM_B2) * (g * g)
        m_hat = mn / (1.0 - ADAM_B1 ** ADAM_STEP)
        v_hat = vn / (1.0 - ADAM_B2 ** ADAM_STEP)
        g_out[...] = g
        d_out[...] = -ADAM_LR * (m_hat / (jnp.sqrt(v_hat) + ADAM_EPS) + ADAM_WD * wv)
        m_out[...] = mn
        v_out[...] = vn

    row = pl.BlockSpec((tr, C), lambda i: (i, 0))
    return pl.pallas_call(
        body, name=name, grid=(R // tr,),
        in_specs=[pl.BlockSpec((P, tr, C), lambda i: (0, i, 0)), row, row, row],
        out_specs=[row, row, row, row],
        out_shape=[jax.ShapeDtypeStruct((R, C), F32)] * 4,
        compiler_params=_params(("parallel",)),
    )(parts, w, m, v)


def _sum_parts(parts, name):
    P, R, C = parts.shape
    tr = _row_tile(R, P * C)

    def body(p_ref, o_ref):
        g = p_ref[0]
        for p in range(1, P):
            g = g + p_ref[p]
        o_ref[...] = g

    return pl.pallas_call(
        body, name=name, grid=(R // tr,),
        in_specs=[pl.BlockSpec((P, tr, C), lambda i: (0, i, 0))],
        out_specs=pl.BlockSpec((tr, C), lambda i: (i, 0)),
        out_shape=jax.ShapeDtypeStruct((R, C), F32),
        compiler_params=_params(("parallel",)),
    )(parts)


def _lane_head(Dh):
    assert Dh & (Dh - 1) == 0 and Dh <= LANES
    return lax.shift_right_logical(lax.broadcasted_iota(jnp.int32, (1, LANES), 1), Dh.bit_length() - 1)


def _stack_heads(x, lane_head, hpb):
    return jnp.concatenate([jnp.where(lane_head == h, x, 0.0) for h in range(hpb)], axis=0)


def _unstack_heads(acc, lane_head, hpb):
    TQ = acc.shape[0] // hpb
    out = acc[0:TQ]
    for h in range(1, hpb):
        out = jnp.where(lane_head == h, acc[h * TQ:(h + 1) * TQ], out)
    return out


def _live_rows(x, r0, hpb):
    if r0 == 0:
        return x
    TQ = x.shape[0] // hpb
    return jnp.concatenate([x[h * TQ + r0:(h + 1) * TQ] for h in range(hpb)], axis=0)


def _put_rows(full, part, r0, hpb):
    if r0 == 0:
        return part
    TQ = full.shape[0] // hpb
    n = TQ - r0
    return jnp.concatenate(
        [blk for h in range(hpb) for blk in (full[h * TQ:h * TQ + r0], part[h * n:(h + 1) * n])], axis=0)


def _first_live_row(m, TQ, TK):
    return max(0, TQ - (m + 1) * TK)


def _key_tile(S):
    return ATT_TK if S % ATT_TK == 0 else BLK


def _query_tile(S):
    return ATT_TQ if S % ATT_TQ == 0 else BLK


def _lane_groups(P):
    return ATT_GP if P % ATT_GP == 0 else 1


def _lanes(u):
    return slice(u * LANES, (u + 1) * LANES)


def _causal_iotas(RS, TK, TQ, r0=0):
    n = TQ - r0
    assert n & (n - 1) == 0 and (TK % TQ == 0 or TQ % TK == 0)
    rows = RS // TQ * n
    trow = jnp.bitwise_and(lax.broadcasted_iota(jnp.int32, (rows, TK), 0), n - 1) + r0
    col = lax.broadcasted_iota(jnp.int32, (rows, TK), 1)
    return trow, col


def _tri(TK, op):
    r = lax.broadcasted_iota(jnp.int32, (TK, TK), 0)
    c = lax.broadcasted_iota(jnp.int32, (TK, TK), 1)
    return op(r, c).astype(BF16)


def _logsig_parts(z):
    lb = jnp.minimum(z, 0.0) - jnp.log(1.0 + jnp.exp(-jnp.abs(z)))
    return lb, lb - z


def _sb_fwd(proj3, W, heads, exch):
    B, S, _ = proj3.shape
    Dh = W // heads
    hpb = LANES // Dh
    P, TQ = W // LANES, _query_tile(S)
    NQ = S // TQ
    scale = 1.0 / math.sqrt(Dh)

    TK = _key_tile(S)
    RS = hpb * TQ
    NM = max(1, TQ // TK)
    GP = _lane_groups(P)
    PG = P // GP

    def body(q_ref, k_ref, v_ref, g_ref, o_ref, y_ref):
        i = pl.program_id(2)
        lane_head = _lane_head(Dh)
        msuf = _tri(TK, lambda r, c: r > c)
        qs = [(_stack_heads(q_ref[:, _lanes(u)], lane_head, hpb) * scale).astype(BF16) for u in range(GP)]
        nt = (i * TQ + TQ - 2) // TK + 1

        def tile(jt, carry, masked, r0=0):
            off = pl.multiple_of(jt * TK, TK)
            if masked:
                trow, col = _causal_iotas(RS, TK, TQ, r0)
                msk = col + (jt * TK - i * TQ) < trow
            out = []
            for u, (rem_all, acc_all) in enumerate(carry):
                rem, acc = _live_rows(rem_all, r0, hpb), _live_rows(acc_all, r0, hpb)
                kj = k_ref[pl.ds(off, TK), _lanes(u)].astype(BF16)
                vj = v_ref[pl.ds(off, TK), _lanes(u)].astype(BF16)
                lb, lr = _logsig_parts(_dot(_live_rows(qs[u], r0, hpb), kj, _NT))
                if masked:
                    lr = jnp.where(msk, lr, 0.0)
                w = jnp.exp(lb + _split_dot(lr, msuf) + rem)
                if masked:
                    w = jnp.where(msk, w, 0.0)
                out.append((_put_rows(rem_all, rem + jnp.sum(lr, axis=1, keepdims=True), r0, hpb),
                            _put_rows(acc_all, acc + _dot(w.astype(BF16), vj), r0, hpb)))
            return tuple(out)

        zero = (jnp.zeros((RS, 1), F32), jnp.zeros((RS, LANES), F32))
        carry = (zero,) * GP
        for m in range(NM):
            carry = tile(nt - 1 - m, carry, True, _first_live_row(m, TQ, TK))
        carry = lax.fori_loop(NM, nt, lambda jj, c: tile(nt - 1 - jj, c, False), carry)
        for u in range(GP):
            o = _unstack_heads(carry[u][1], lane_head, hpb)
            o_ref[:, _lanes(u)] = o
            y_ref[:, _lanes(u)] = (o * _silu(g_ref[:, _lanes(u)])).astype(BF16)

    LW = GP * LANES
    blk = lambda sec: pl.BlockSpec((None, TQ, LW), lambda b, p, i: (b, i, sec * PG + p))
    full = lambda sec: pl.BlockSpec((None, S, LW), lambda b, p, i: (b, 0, sec * PG + p))
    out = pl.BlockSpec((None, TQ, LW), lambda b, p, i: (b, i, p))
    return _call_hosting(
        body, exch, "sb_fwd", (B, PG, NQ), [blk(0), full(1), full(2), blk(3)], [out, out],
        [jax.ShapeDtypeStruct((B, S, W), F32), jax.ShapeDtypeStruct((B, S, W), BF16)], [],
        (proj3, proj3, proj3, proj3))


def _sb_bwd(proj3, o, dy, W, heads, exch):
    B, S, _ = proj3.shape
    Dh = W // heads
    hpb = LANES // Dh
    P, TQ = W // LANES, _query_tile(S)
    NQ = S // TQ
    scale = 1.0 / math.sqrt(Dh)

    TK = _key_tile(S)
    RS = hpb * TQ
    NM = max(1, TQ // TK)

    def body(q_ref, k_ref, v_ref, g_ref, o_ref, dy_ref, dp_ref, dk_ref, dv_ref, u_ref, sig_ref, es_ref):
        i = pl.program_id(2)
        rows = pl.ds(pl.multiple_of(i * TQ, TQ), TQ)

        @pl.when(i == 0)
        def _():
            dk_ref[...] = jnp.zeros_like(dk_ref)
            dv_ref[...] = jnp.zeros_like(dv_ref)

        lane_head = _lane_head(Dh)
        msuf = _tri(TK, lambda r, c: r > c)
        mpre = _tri(TK, lambda r, c: r < c)
        g = g_ref[...]
        dyv = dy_ref[...].astype(F32)
        dp_ref[3, rows, :] = (dyv * o_ref[...] * _dsilu(g)).astype(dp_ref.dtype)
        qs = (_stack_heads(q_ref[...], lane_head, hpb) * scale).astype(BF16)
        dos = _stack_heads(dyv * _silu(g), lane_head, hpb).astype(BF16)
        nt = (i * TQ + TQ - 2) // TK + 1

        def weights(jt, rem_all, masked, r0=0):
            off = pl.multiple_of(jt * TK, TK)
            kj = k_ref[pl.ds(off, TK), :].astype(BF16)
            vj = v_ref[pl.ds(off, TK), :].astype(BF16)
            dos_l = _live_rows(dos, r0, hpb)
            lb, lr = _logsig_parts(_dot(_live_rows(qs, r0, hpb), kj, _NT))
            if masked:
                trow, col = _causal_iotas(RS, TK, TQ, r0)
                msk = col + (jt * TK - i * TQ) < trow
                lr = jnp.where(msk, lr, 0.0)
            w = jnp.exp(lb + _split_dot(lr, msuf) + _live_rows(rem_all, r0, hpb))
            if masked:
                w = jnp.where(msk, w, 0.0)
            e = w * _dot(dos_l, vj, _NT)
            sig = jnp.exp(lb)
            u = e * (1.0 - sig) - _split_dot(e, mpre) * sig
            if masked:
                u = jnp.where(msk, u, 0.0)
                sig = jnp.where(msk, sig, 0.0)
            u_ref[jt] = _put_rows(jnp.zeros((RS, TK), F32), u, r0, hpb)
            sig_ref[jt] = _put_rows(jnp.zeros((RS, TK), F32), sig, r0, hpb)
            es_ref[jt] = _put_rows(jnp.zeros((RS, 1), F32), jnp.sum(e, axis=1, keepdims=True), r0, hpb)
            dv_ref[pl.ds(off, TK), :] += _dot(w.astype(BF16), dos_l, _TN)
            return _put_rows(rem_all, _live_rows(rem_all, r0, hpb) + jnp.sum(lr, axis=1, keepdims=True), r0, hpb)

        rem = jnp.zeros((RS, 1), F32)
        for m in range(NM):
            rem = weights(nt - 1 - m, rem, True, _first_live_row(m, TQ, TK))
        lax.fori_loop(NM, nt, lambda jj, r: weights(nt - 1 - jj, r, False), rem)

        def grads(jt, carry):
            pre, acc = carry
            off = pl.multiple_of(jt * TK, TK)
            kj = k_ref[pl.ds(off, TK), :].astype(BF16)
            dz = (u_ref[jt] - pre * sig_ref[jt]).astype(BF16)
            dk_ref[pl.ds(off, TK), :] += _dot(dz, qs, _TN)
            return pre + es_ref[jt], acc + _dot(dz, kj)

        _, acc = lax.fori_loop(0, nt, grads, (jnp.zeros((RS, 1), F32), jnp.zeros((RS, LANES), F32)))
        dp_ref[0, rows, :] = (_unstack_heads(acc, lane_head, hpb) * scale).astype(dp_ref.dtype)

        @pl.when(i == NQ - 1)
        def _():
            dp_ref[1] = dk_ref[...].astype(dp_ref.dtype)
            dp_ref[2] = dv_ref[...].astype(dp_ref.dtype)

    blk = lambda sec: pl.BlockSpec((None, TQ, LANES), lambda b, p, i: (b, i, sec * P + p))
    full = lambda sec: pl.BlockSpec((None, S, LANES), lambda b, p, i: (b, 0, sec * P + p))
    one = pl.BlockSpec((None, TQ, LANES), lambda b, p, i: (b, i, p))
    (dproj,), moved = _call_hosting(
        body, exch, "sb_bwd", (B, P, NQ), [blk(0), full(1), full(2), blk(3), one, one],
        [pl.BlockSpec((None, 4, S, LANES), lambda b, p, i: (b, 0, 0, p))],
        [jax.ShapeDtypeStruct((B, 4, S, W), BF16)],
        [pltpu.VMEM((S, LANES), F32), pltpu.VMEM((S, LANES), F32),
         pltpu.VMEM((S // TK, RS, TK), F32), pltpu.VMEM((S // TK, RS, TK), F32), pltpu.VMEM((S // TK, RS, 1), F32)],
        (proj3, proj3, proj3, proj3, o, dy))
    return dproj, moved


def _fox_gate_fwd(f_t, b_f):
    B, H, S = f_t.shape

    def body(f_ref, b_ref, c_ref):
        row = lax.broadcasted_iota(jnp.int32, (BLK, BLK), 0)
        col = lax.broadcasted_iota(jnp.int32, (BLK, BLK), 1)
        mpre = (row <= col).astype(BF16)
        carry = jnp.zeros((H, 1), F32)
        for n in range(S // BLK):
            sl = pl.ds(n * BLK, BLK)
            lf, _ = _logsig_parts(f_ref[:, sl] + b_ref[...])
            c_ref[:, sl] = _split3_dot(lf, mpre) + carry
            carry = carry + jnp.sum(lf, axis=1, keepdims=True)

    spec = pl.BlockSpec((None, H, S), lambda b: (b, 0, 0))
    return pl.pallas_call(
        body, name="fox_gate_fwd", grid=(B,),
        in_specs=[spec, pl.BlockSpec((H, 1), lambda b: (0, 0))], out_specs=spec,
        out_shape=jax.ShapeDtypeStruct((B, H, S), F32),
        compiler_params=_params(("parallel",)),
    )(f_t, b_f)


def _fox_gate_bwd(dcum_t, f_t, b_f):
    B, H, S = f_t.shape

    def body(d_ref, f_ref, b_ref, df_ref, db_ref):
        b = pl.program_id(0)

        @pl.when(b == 0)
        def _():
            db_ref[...] = jnp.zeros_like(db_ref)

        row = lax.broadcasted_iota(jnp.int32, (BLK, BLK), 0)
        col = lax.broadcasted_iota(jnp.int32, (BLK, BLK), 1)
        msuf = (row >= col).astype(BF16)
        carry = jnp.zeros((H, 1), F32)
        dbacc = jnp.zeros((H, 1), F32)
        for n in reversed(range(S // BLK)):
            sl = pl.ds(n * BLK, BLK)
            dv = d_ref[:, sl]
            dlf = _split3_dot(dv, msuf) + carry
            carry = carry + jnp.sum(dv, axis=1, keepdims=True)
            df = dlf * _sigmoid(-(f_ref[:, sl] + b_ref[...]))
            df_ref[:, sl] = df
            dbacc = dbacc + jnp.sum(df, axis=1, keepdims=True)
        db_ref[...] += dbacc

    spec = pl.BlockSpec((None, H, S), lambda b: (b, 0, 0))
    vec = pl.BlockSpec((H, 1), lambda b: (0, 0))
    return pl.pallas_call(
        body, name="fox_gate_bwd", grid=(B,),
        in_specs=[spec, spec, vec], out_specs=[spec, vec],
        out_shape=[jax.ShapeDtypeStruct((B, H, S), F32), jax.ShapeDtypeStruct((H, 1), F32)],
        compiler_params=_params(("arbitrary",)),
    )(dcum_t, f_t, b_f)


def _pick_col(block, idx, lane_iota):
    return jnp.sum(jnp.where(lane_iota == idx, block, 0.0), axis=1, keepdims=True)


def _pick_row(block, idx, sub_iota):
    return jnp.sum(jnp.where(sub_iota == idx, block, 0.0), axis=0, keepdims=True)


def _fox_fwd(proj3, cum_t, W, heads):
    B, S, _ = proj3.shape
    H = heads
    Dh = W // heads
    hpb = LANES // Dh
    P, TQ = W // LANES, _query_tile(S)
    NQ = S // TQ
    scale = 1.0 / math.sqrt(Dh)

    TK = _key_tile(S)
    RS = hpb * TQ
    NM = max(1, TQ // TK)

    def body(q_ref, k_ref, v_ref, g_ref, ct_ref, o_ref, y_ref, lse_ref):
        p = pl.program_id(1)
        i = pl.program_id(2)
        lane_head = _lane_head(Dh)
        sub_h = lax.broadcasted_iota(jnp.int32, (H, 1), 0)
        qs = (_stack_heads(q_ref[...], lane_head, hpb) * scale).astype(BF16)
        nt = (i * TQ + TQ - 1) // TK + 1

        trow, col = _causal_iotas(RS, TK, TQ)

        def tile(jt, carry, masked):
            mx, l, acc = carry
            off = pl.multiple_of(jt * TK, TK)
            kj = k_ref[pl.ds(off, TK), :].astype(BF16)
            vj = v_ref[pl.ds(off, TK), :].astype(BF16)
            ctb = ct_ref[:, pl.ds(off, TK)]
            z = _dot(qs, kj, _NT)
            s = jnp.concatenate([z[h * TQ:(h + 1) * TQ] - _pick_row(ctb, p * hpb + h, sub_h) for h in range(hpb)],
                                axis=0)
            if masked:
                s = jnp.where(col + (jt * TK - i * TQ) <= trow, s, NEG_BIG)
            mx2 = jnp.maximum(mx, jnp.max(s, axis=1, keepdims=True))
            pe = jnp.exp(s - mx2)
            alpha = jnp.exp(mx - mx2)
            return (mx2, alpha * l + jnp.sum(pe, axis=1, keepdims=True), alpha * acc + _dot(pe.astype(BF16), vj))

        carry = lax.fori_loop(
            0, nt - NM, lambda jt, c: tile(jt, c, False),
            (jnp.full((RS, 1), NEG_BIG, F32), jnp.zeros((RS, 1), F32), jnp.zeros((RS, LANES), F32)))
        for m in reversed(range(NM)):
            carry = tile(nt - 1 - m, carry, True)
        mx, l, acc = carry
        o = _unstack_heads(acc / l, lane_head, hpb)
        o_ref[...] = o
        lse_ref[...] = _unstack_heads(jnp.broadcast_to(mx + jnp.log(l), (RS, LANES)), lane_head, hpb)
        y_ref[...] = (o * _silu(g_ref[...])).astype(BF16)

    blk = lambda sec: pl.BlockSpec((None, TQ, LANES), lambda b, p, i: (b, i, sec * P + p))
    full = lambda sec: pl.BlockSpec((None, S, LANES), lambda b, p, i: (b, 0, sec * P + p))
    out = pl.BlockSpec((None, TQ, LANES), lambda b, p, i: (b, i, p))
    return pl.pallas_call(
        body, name="fox_fwd", grid=(B, P, NQ),
        in_specs=[blk(0), full(1), full(2), blk(3),
                  pl.BlockSpec((None, H, S), lambda b, p, i: (b, 0, 0))],
        out_specs=[out, out, out],
        out_shape=[jax.ShapeDtypeStruct((B, S, W), F32), jax.ShapeDtypeStruct((B, S, W), BF16),
                   jax.ShapeDtypeStruct((B, S, W), F32)],
        compiler_params=_params(("parallel", "parallel", "arbitrary")),
    )(proj3, proj3, proj3, proj3, cum_t)


def _fox_bwd(proj3, cum_t, o, lse, dy, W, heads):
    B, S, _ = proj3.shape
    H = heads
    Dh = W // heads
    hpb = LANES // Dh
    P, TQ = W // LANES, _query_tile(S)
    NQ = S // TQ
    scale = 1.0 / math.sqrt(Dh)

    TK = _key_tile(S)
    RS = hpb * TQ
    NM = max(1, TQ // TK)

    def body(q_ref, k_ref, v_ref, g_ref, ct_ref, o_ref, lse_ref, dy_ref,
             dpj_ref, dc_ref, dk_ref, dv_ref, p_scr, dp_scr):
        p = pl.program_id(1)
        i = pl.program_id(2)
        rows = pl.ds(pl.multiple_of(i * TQ, TQ), TQ)

        @pl.when(i == 0)
        def _():
            dk_ref[...] = jnp.zeros_like(dk_ref)
            dv_ref[...] = jnp.zeros_like(dv_ref)
            dc_ref[...] = jnp.zeros_like(dc_ref)

        lane_head = _lane_head(Dh)
        sub_h = lax.broadcasted_iota(jnp.int32, (H, 1), 0)
        lane = lax.broadcasted_iota(jnp.int32, (1, LANES), 1)
        g = g_ref[...]
        lsev = lse_ref[...]
        dyv = dy_ref[...].astype(F32)
        dpj_ref[3, rows, :] = (dyv * o_ref[...] * _dsilu(g)).astype(dpj_ref.dtype)
        qs = (_stack_heads(q_ref[...], lane_head, hpb) * scale).astype(BF16)
        dos = _stack_heads(dyv * _silu(g), lane_head, hpb).astype(BF16)
        neg_lse = -jnp.concatenate([_pick_col(lsev, h * Dh, lane) for h in range(hpb)], axis=0)
        nt = (i * TQ + TQ - 1) // TK + 1

        def probs(jt, dsum, masked, r0=0):
            n = TQ - r0
            off = pl.multiple_of(jt * TK, TK)
            kj = k_ref[pl.ds(off, TK), :].astype(BF16)
            vj = v_ref[pl.ds(off, TK), :].astype(BF16)
            ctb = ct_ref[:, pl.ds(off, TK)]
            dos_l = _live_rows(dos, r0, hpb)
            z = _dot(_live_rows(qs, r0, hpb), kj, _NT) + _live_rows(neg_lse, r0, hpb)
            s = jnp.concatenate([z[h * n:(h + 1) * n] - _pick_row(ctb, p * hpb + h, sub_h) for h in range(hpb)],
                                axis=0)
            pr = jnp.exp(s)
            if masked:
                trow, col = _causal_iotas(RS, TK, TQ, r0)
                pr = jnp.where(col + (jt * TK - i * TQ) <= trow, pr, 0.0)
            dp = _dot(dos_l, vj, _NT)
            p_scr[jt] = _put_rows(jnp.zeros((RS, TK), F32), pr, r0, hpb)
            dp_scr[jt] = _put_rows(jnp.zeros((RS, TK), F32), dp, r0, hpb)
            dv_ref[pl.ds(off, TK), :] += _dot(pr.astype(BF16), dos_l, _TN)
            return _put_rows(dsum, _live_rows(dsum, r0, hpb) + jnp.sum(pr * dp, axis=1, keepdims=True), r0, hpb)

        dsum = lax.fori_loop(0, nt - NM, lambda jt, d: probs(jt, d, False), jnp.zeros((RS, 1), F32))
        for m in reversed(range(NM)):
            dsum = probs(nt - 1 - m, dsum, True)

        def grads(jt, acc):
            off = pl.multiple_of(jt * TK, TK)
            kj = k_ref[pl.ds(off, TK), :].astype(BF16)
            ds = p_scr[jt] * (dp_scr[jt] - dsum)
            for h in range(hpb):
                dc_ref[h:h + 1, pl.ds(off, TK)] -= jnp.sum(ds[h * TQ:(h + 1) * TQ], axis=0, keepdims=True)
            dsb = ds.astype(BF16)
            dk_ref[pl.ds(off, TK), :] += _dot(dsb, qs, _TN)
            return acc + _dot(dsb, kj)

        acc = lax.fori_loop(0, nt, grads, jnp.zeros((RS, LANES), F32))
        dpj_ref[0, rows, :] = (_unstack_heads(acc, lane_head, hpb) * scale).astype(dpj_ref.dtype)

        @pl.when(i == NQ - 1)
        def _():
            dpj_ref[1] = dk_ref[...].astype(dpj_ref.dtype)
            dpj_ref[2] = dv_ref[...].astype(dpj_ref.dtype)

    blk = lambda sec: pl.BlockSpec((None, TQ, LANES), lambda b, p, i: (b, i, sec * P + p))
    full = lambda sec: pl.BlockSpec((None, S, LANES), lambda b, p, i: (b, 0, sec * P + p))
    one = pl.BlockSpec((None, TQ, LANES), lambda b, p, i: (b, i, p))
    return pl.pallas_call(
        body, name="fox_bwd", grid=(B, P, NQ),
        in_specs=[blk(0), full(1), full(2), blk(3),
                  pl.BlockSpec((None, H, S), lambda b, p, i: (b, 0, 0)),
                  one, one, one],
        out_specs=[pl.BlockSpec((None, 4, S, LANES), lambda b, p, i: (b, 0, 0, p)),
                   pl.BlockSpec((None, None, hpb, S), lambda b, p, i: (b, p, 0, 0))],
        out_shape=[jax.ShapeDtypeStruct((B, 4, S, W), BF16), jax.ShapeDtypeStruct((B, P, hpb, S), F32)],
        scratch_shapes=[pltpu.VMEM((S, LANES), F32), pltpu.VMEM((S, LANES), F32),
                        pltpu.VMEM((S // TK, RS, TK), F32), pltpu.VMEM((S // TK, RS, TK), F32)],
        compiler_params=_params(("parallel", "parallel", "arbitrary")),
    )(proj3, proj3, proj3, proj3, cum_t, o, lse, dy)


def _layernorm_rows(v, gamma, beta):
    mu = jnp.mean(v, axis=-1, keepdims=True)
    xc = v - mu
    rstd = lax.rsqrt(jnp.mean(xc * xc, axis=-1, keepdims=True) + EPS)
    xh = xc * rstd
    return xh, rstd, xh * gamma + beta


def _layernorm_rows_bwd(dout, xh, rstd, gamma):
    dxh = dout * gamma
    return rstd * (dxh - jnp.mean(dxh, axis=-1, keepdims=True) - xh * jnp.mean(dxh * xh, axis=-1, keepdims=True))


def _gmlp_fwd(proj, wm, bs_t, ln_g, ln_b, W):
    T = proj.shape[0]
    G = wm.shape[0]
    cg = W // G
    assert cg == LANES

    def body(p_ref, wm_ref, bs_ref, lg_ref, lb_ref, y_ref, vn_ref):
        lane = lax.broadcasted_iota(jnp.int32, (1, LANES), 1)
        _, _, vn = _layernorm_rows(_gelu(p_ref[:, W:2 * W]), lg_ref[...], lb_ref[...])
        vn_ref[...] = vn.astype(BF16)
        bs = bs_ref[...]
        for g in range(G):
            sl = pl.ds(g * cg, cg)
            s = _dot(wm_ref[g], vn_ref[:, sl]) + _pick_col(bs, g, lane)
            gate = p_ref[:, pl.ds(2 * W + g * cg, cg)]
            y_ref[:, sl] = (_gelu(p_ref[:, sl]) * s * _silu(gate)).astype(BF16)

    vec = pl.BlockSpec((1, W), lambda r: (0, 0))
    return pl.pallas_call(
        body, name="gmlp_fwd", grid=(T // BLK,),
        in_specs=[pl.BlockSpec((BLK, 3 * W), lambda r: (r, 0)),
                  pl.BlockSpec((G, BLK, BLK), lambda r: (0, 0, 0)),
                  pl.BlockSpec((BLK, LANES), lambda r: (0, 0)), vec, vec],
        out_specs=pl.BlockSpec((BLK, W), lambda r: (r, 0)),
        out_shape=jax.ShapeDtypeStruct((T, W), BF16),
        scratch_shapes=[pltpu.VMEM((BLK, W), BF16)],
        compiler_params=_params(("parallel",)),
    )(proj, wm, bs_t, ln_g, ln_b)


def _gmlp_bwd(proj, dy, wm, bs_t, ln_g, ln_b, W):
    T = proj.shape[0]
    G = wm.shape[0]
    cg = W // G

    def body(p_ref, dy_ref, wm_ref, bs_ref, lg_ref, lb_ref,
             dp_ref, dwm_ref, dbs_ref, dlg_ref, dlb_ref, vn_ref, dvn_ref):
        r = pl.program_id(0)

        @pl.when(r == 0)
        def _():
            dwm_ref[...] = jnp.zeros_like(dwm_ref)
            dbs_ref[...] = jnp.zeros_like(dbs_ref)
            dlg_ref[...] = jnp.zeros_like(dlg_ref)
            dlb_ref[...] = jnp.zeros_like(dlb_ref)

        lane = lax.broadcasted_iota(jnp.int32, (1, LANES), 1)
        vpre = p_ref[:, W:2 * W]
        gamma = lg_ref[...]
        xh, rstd, vn = _layernorm_rows(_gelu(vpre), gamma, lb_ref[...])
        vn_ref[...] = vn.astype(BF16)
        bs = bs_ref[...]
        dbs = jnp.zeros((BLK, LANES), F32)
        for g in range(G):
            sl = pl.ds(g * cg, cg)
            gsl = pl.ds(2 * W + g * cg, cg)
            vng = vn_ref[:, sl]
            s = _dot(wm_ref[g], vng) + _pick_col(bs, g, lane)
            upre = p_ref[:, sl]
            u = _gelu(upre)
            gate = p_ref[:, gsl]
            dyv = dy_ref[:, sl].astype(F32)
            dp_ref[:, gsl] = (dyv * u * s * _dsilu(gate)).astype(dp_ref.dtype)
            do = dyv * _silu(gate)
            dp_ref[:, sl] = (do * s * _dgelu(upre)).astype(dp_ref.dtype)
            ds = do * u
            dbs = dbs + jnp.where(lane == g, jnp.sum(ds, axis=1, keepdims=True), 0.0)
            dsb = ds.astype(BF16)
            dwm_ref[g] += _dot(dsb, vng, _NT)
            dvn_ref[:, sl] = _dot(wm_ref[g], dsb, _TN)
        dbs_ref[...] += dbs
        dvn = dvn_ref[...]
        dlg_ref[...] += jnp.sum(dvn * xh, axis=0, keepdims=True)
        dlb_ref[...] += jnp.sum(dvn, axis=0, keepdims=True)
        dv = _layernorm_rows_bwd(dvn, xh, rstd, gamma)
        dp_ref[:, W:2 * W] = (dv * _dgelu(vpre)).astype(dp_ref.dtype)

    vec = pl.BlockSpec((1, W), lambda r: (0, 0))
    return pl.pallas_call(
        body, name="gmlp_bwd", grid=(T // BLK,),
        in_specs=[pl.BlockSpec((BLK, 3 * W), lambda r: (r, 0)),
                  pl.BlockSpec((BLK, W), lambda r: (r, 0)),
                  pl.BlockSpec((G, BLK, BLK), lambda r: (0, 0, 0)),
                  pl.BlockSpec((BLK, LANES), lambda r: (0, 0)), vec, vec],
        out_specs=[pl.BlockSpec((BLK, 3 * W), lambda r: (r, 0)),
                   pl.BlockSpec((G, BLK, BLK), lambda r: (0, 0, 0)),
                   pl.BlockSpec((BLK, LANES), lambda r: (0, 0)), vec, vec],
        out_shape=[jax.ShapeDtypeStruct((T, 3 * W), BF16), jax.ShapeDtypeStruct((G, BLK, BLK), F32),
                   jax.ShapeDtypeStruct((BLK, LANES), F32),
                   jax.ShapeDtypeStruct((1, W), F32), jax.ShapeDtypeStruct((1, W), F32)],
        scratch_shapes=[pltpu.VMEM((BLK, W), BF16), pltpu.VMEM((BLK, W), F32)],
        compiler_params=_params(("arbitrary",)),
    )(proj, dy, wm, bs_t, ln_g, ln_b)


SUBLANES = 8
SHIFT_ROWS = CONV_HALO + BLK - SUBLANES


def _shift_rows(ext_ref, sh_ref, off):
    for r in range(1, SUBLANES):
        sh_ref[r - 1] = ext_ref[pl.ds(r, SHIFT_ROWS), pl.ds(off, LANES)]


def _rows_from(ext_ref, sh_ref, off, start):
    r = start % SUBLANES
    if r == 0:
        return ext_ref[pl.ds(start, BLK), pl.ds(off, LANES)]
    return sh_ref[r - 1, pl.ds(start - r, BLK), :]


def _conv_taps(ext_ref, sh_ref, cw_ref, off, n_taps, first):
    acc = jnp.zeros((BLK, LANES), F32)
    for k in range(n_taps):
        acc = acc + cw_ref[k:k + 1, pl.ds(off, LANES)] * _rows_from(ext_ref, sh_ref, off, first + k)
    return acc


def _fill_glu_ext(ext_ref, halo_ref, cur_ref, W, first_block):
    y0h = halo_ref[:, :W] * _sigmoid(halo_ref[:, W:])
    ext_ref[0:CONV_HALO, :] = jnp.where(first_block, 0.0, y0h)
    ext_ref[CONV_HALO:CONV_HALO + BLK, :] = cur_ref[:, :W] * _sigmoid(cur_ref[:, W:])


def _conv_specs(S, W):
    per = BLK // CONV_HALO
    cur = pl.BlockSpec((None, BLK, 2 * W), lambda b, i: (b, i, 0))
    halo = pl.BlockSpec((None, CONV_HALO, 2 * W), lambda b, i: (b, jnp.maximum(i * per - 1, 0), 0))
    gate = pl.BlockSpec((None, BLK, W), lambda b, i: (b, i, 2))
    return cur, halo, gate


def _conv_fwd(proj3, cw, cb, ln_g, ln_b, W, exch):
    B, S, _ = proj3.shape
    K = cw.shape[0]
    first = CONV_HALO - (K - 1)
    assert first >= 0

    def body(cur_ref, halo_ref, g_ref, cw_ref, cb_ref, lg_ref, lb_ref, y_ref, ext_ref, y1_ref, sh_ref):
        i = pl.program_id(1)
        _fill_glu_ext(ext_ref, halo_ref, cur_ref, W, i == 0)

        def chan(c, _):
            off = pl.multiple_of(c * LANES, LANES)
            _shift_rows(ext_ref, sh_ref, off)
            y1_ref[:, pl.ds(off, LANES)] = (_conv_taps(ext_ref, sh_ref, cw_ref, off, K, first)
                                            + cb_ref[:, pl.ds(off, LANES)])
            return 0

        lax.fori_loop(0, W // LANES, chan, 0)
        _, _, ln = _layernorm_rows(y1_ref[...], lg_ref[...], lb_ref[...])
        y_ref[...] = (_silu(ln) * _silu(g_ref[...])).astype(BF16)

    cur, halo, gate = _conv_specs(S, W)
    vec = pl.BlockSpec((1, W), lambda b, i: (0, 0))
    (y,), moved = _call_hosting(
        body, exch, "conv_fwd", (B, S // BLK),
        [cur, halo, gate, pl.BlockSpec((K, W), lambda b, i: (0, 0)), vec, vec, vec],
        [pl.BlockSpec((None, BLK, W), lambda b, i: (b, i, 0))], [jax.ShapeDtypeStruct((B, S, W), BF16)],
        [pltpu.VMEM((CONV_HALO + BLK, W), F32), pltpu.VMEM((BLK, W), F32),
         pltpu.VMEM((SUBLANES - 1, SHIFT_ROWS, LANES), F32)],
        (proj3, proj3, proj3, cw, cb, ln_g, ln_b))
    return y, moved


def _conv_bwd1(proj3, dy, cw, cb, ln_g, ln_b, W, exch):
    B, S, _ = proj3.shape
    K = cw.shape[0]
    first = CONV_HALO - (K - 1)

    def body(cur_ref, halo_ref, g_ref, dy_ref, cw_ref, cb_ref, lg_ref, lb_ref,
             dy1_ref, dg_ref, dcw_ref, dcb_ref, dlg_ref, dlb_ref, ext_ref, y1_ref, sh_ref):
        b = pl.program_id(0)
        i = pl.program_id(1)

        @pl.when(jnp.logical_and(b == 0, i == 0))
        def _():
            dcw_ref[...] = jnp.zeros_like(dcw_ref)
            dcb_ref[...] = jnp.zeros_like(dcb_ref)
            dlg_ref[...] = jnp.zeros_like(dlg_ref)
            dlb_ref[...] = jnp.zeros_like(dlb_ref)

        _fill_glu_ext(ext_ref, halo_ref, cur_ref, W, i == 0)

        def chan(c, _):
            off = pl.multiple_of(c * LANES, LANES)
            _shift_rows(ext_ref, sh_ref.at[c], off)
            y1_ref[:, pl.ds(off, LANES)] = (_conv_taps(ext_ref, sh_ref.at[c], cw_ref, off, K, first)
                                            + cb_ref[:, pl.ds(off, LANES)])
            return 0

        lax.fori_loop(0, W // LANES, chan, 0)
        gamma = lg_ref[...]
        xh, rstd, ln = _layernorm_rows(y1_ref[...], gamma, lb_ref[...])
        g = g_ref[...]
        dyv = dy_ref[...].astype(F32)
        dg_ref[...] = (dyv * _silu(ln) * _dsilu(g)).astype(dg_ref.dtype)
        dln = dyv * _silu(g) * _dsilu(ln)
        dlg_ref[...] += jnp.sum(dln * xh, axis=0, keepdims=True)
        dlb_ref[...] += jnp.sum(dln, axis=0, keepdims=True)
        dy1 = _layernorm_rows_bwd(dln, xh, rstd, gamma)
        dy1_ref[...] = dy1
        dcb_ref[...] += jnp.sum(dy1, axis=0, keepdims=True)

        def chan_w(c, _):
            off = pl.multiple_of(c * LANES, LANES)
            d = dy1_ref[:, pl.ds(off, LANES)]
            for k in range(K):
                dcw_ref[k:k + 1, pl.ds(off, LANES)] += jnp.sum(
                    d * _rows_from(ext_ref, sh_ref.at[c], off, first + k), axis=0, keepdims=True)
            return 0

        lax.fori_loop(0, W // LANES, chan_w, 0)

    cur, halo, gate = _conv_specs(S, W)
    vec = pl.BlockSpec((1, W), lambda b, i: (0, 0))
    taps = pl.BlockSpec((K, W), lambda b, i: (0, 0))
    one = pl.BlockSpec((None, BLK, W), lambda b, i: (b, i, 0))
    return _call_hosting(
        body, exch, "conv_bwd1", (B, S // BLK), [cur, halo, gate, one, taps, vec, vec, vec],
        [one, one, taps, vec, vec, vec],
        [jax.ShapeDtypeStruct((B, S, W), F32), jax.ShapeDtypeStruct((B, S, W), BF16),
         jax.ShapeDtypeStruct((K, W), F32)] + [jax.ShapeDtypeStruct((1, W), F32)] * 3,
        [pltpu.VMEM((CONV_HALO + BLK, W), F32), pltpu.VMEM((BLK, W), F32),
         pltpu.VMEM((W // LANES, SUBLANES - 1, SHIFT_ROWS, LANES), F32)],
        (proj3, proj3, proj3, dy, cw, cb, ln_g, ln_b))


def _conv_bwd2(proj3, dy1, dgate, cw_rev, W):
    B, S, _ = proj3.shape
    K = cw_rev.shape[0]
    NQ = S // BLK
    per = BLK // CONV_HALO

    def body(cur_ref, d_ref, dnext_ref, dgate_ref, cw_ref, dp_ref, ext_ref, dy0_ref, sh_ref):
        i = pl.program_id(1)
        ext_ref[0:BLK, :] = d_ref[...]
        ext_ref[BLK:BLK + CONV_HALO, :] = jnp.where(i == NQ - 1, 0.0, dnext_ref[...])

        def chan(c, _):
            off = pl.multiple_of(c * LANES, LANES)
            _shift_rows(ext_ref, sh_ref, off)
            dy0_ref[:, pl.ds(off, LANES)] = _conv_taps(ext_ref, sh_ref, cw_ref, off, K, 0)
            return 0

        lax.fori_loop(0, W // LANES, chan, 0)
        a = cur_ref[:, :W]
        sg = _sigmoid(cur_ref[:, W:])
        dy0 = dy0_ref[...]
        dp_ref[:, 0:W] = (dy0 * sg).astype(dp_ref.dtype)
        dp_ref[:, W:2 * W] = (dy0 * a * sg * (1.0 - sg)).astype(dp_ref.dtype)
        dp_ref[:, 2 * W:3 * W] = dgate_ref[...]

    cur = pl.BlockSpec((None, BLK, 2 * W), lambda b, i: (b, i, 0))
    one = pl.BlockSpec((None, BLK, W), lambda b, i: (b, i, 0))
    nxt = pl.BlockSpec((None, CONV_HALO, W), lambda b, i: (b, jnp.minimum((i + 1) * per, S // CONV_HALO - 1), 0))
    return pl.pallas_call(
        body, name="conv_bwd2", grid=(B, NQ),
        in_specs=[cur, one, nxt, one, pl.BlockSpec((K, W), lambda b, i: (0, 0))],
        out_specs=pl.BlockSpec((None, BLK, 3 * W), lambda b, i: (b, i, 0)),
        out_shape=jax.ShapeDtypeStruct((B, S, 3 * W), BF16),
        scratch_shapes=[pltpu.VMEM((BLK + CONV_HALO, W), F32), pltpu.VMEM((BLK, W), F32),
                        pltpu.VMEM((SUBLANES - 1, SHIFT_ROWS, LANES), F32)],
        compiler_params=_params(("parallel", "parallel")),
    )(proj3, dy1, dy1, dgate, cw_rev)


def _pack(arrays):
    flat = jnp.concatenate([a.astype(F32).reshape(-1) for a in arrays])
    n = flat.shape[0]
    pad = (-n) % (8 * LANES)
    if pad:
        flat = jnp.concatenate([flat, jnp.zeros((pad,), F32)])
    return flat.reshape(-1, LANES)


def _unpack(packed, shapes, lead=()):
    flat = packed.reshape(lead + (-1,))
    out, off = [], 0
    for shp in shapes:
        n = math.prod(shp)
        out.append(flat[..., off:off + n].reshape(lead + tuple(shp)))
        off += n
    return out


def _cols_from_dev(g):
    g = jnp.moveaxis(g, 0, -2)
    return g.reshape(g.shape[:-2] + (g.shape[-2] * g.shape[-1],))


def _my_cols(full, me):
    n8 = full.shape[-1] // N_DEV
    return lax.dynamic_slice_in_dim(full, me * n8, n8, axis=full.ndim - 1)


def kernel(x, a_norm, a_w_in, a_w_out, b_norm, b_w_in, b_v_ln_g, b_v_ln_b, b_w_s, b_b_s, b_w_out, c_norm, c_w_in, c_conv_w, c_conv_b, c_ln_g, c_ln_b, c_w_out, d_norm, d_w_in, d_b_f, d_w_out, final_norm, loss_target, m_a_norm, m_a_w_in, m_a_w_out, m_b_norm, m_b_w_in, m_b_v_ln_g, m_b_v_ln_b, m_b_w_s, m_b_b_s, m_b_w_out, m_c_norm, m_c_w_in, m_c_conv_w, m_c_conv_b, m_c_ln_g, m_c_ln_b, m_c_w_out, m_d_norm, m_d_w_in, m_d_b_f, m_d_w_out, m_final_norm, v_a_norm, v_a_w_in, v_a_w_out, v_b_norm, v_b_w_in, v_b_v_ln_g, v_b_v_ln_b, v_b_w_s, v_b_b_s, v_b_w_out, v_c_norm, v_c_w_in, v_c_conv_w, v_c_conv_b, v_c_ln_g, v_c_ln_b, v_c_w_out, v_d_norm, v_d_w_in, v_d_b_f, v_d_w_out, v_final_norm):
    B, S, D = x.shape
    T = B * S
    xi, yi, ci = _me()
    me = 4 * xi + 2 * yi + ci

    G = b_w_s.shape[1]
    KC = c_conv_w.shape[1]
    H_D = d_b_f.shape[1]
    W_A = a_w_out.shape[1] * N_DEV
    W_B = b_w_out.shape[1] * N_DEV
    W_C = c_w_out.shape[1] * N_DEV
    W_D = d_w_out.shape[1] * N_DEV
    N_D = d_w_in.shape[2] * N_DEV
    N_D_PAD = -(-N_D // (3 * LANES)) * (3 * LANES)

    big_names = ["a_w_in", "a_w_out", "b_w_in", "b_w_out", "c_w_in", "c_w_out", "d_w_in", "d_w_out"]
    big_w = dict(a_w_in=a_w_in[0], a_w_out=a_w_out[0], b_w_in=b_w_in[0], b_w_out=b_w_out[0],
                 c_w_in=c_w_in[0], c_w_out=c_w_out[0], d_w_in=d_w_in[0], d_w_out=d_w_out[0])
    small_sharded = [b_norm, b_v_ln_g, b_v_ln_b, c_norm, c_conv_w, c_conv_b, c_ln_g, c_ln_b, d_norm]
    first_names, later_names, last_names = big_names[:1], big_names[1:6], big_names[6:]
    gathered = _GatherViaSibling(
        [big_w[n].astype(BF16) for n in first_names] + [_pack(small_sharded)]).run("gather_first")
    wg = dict(zip(first_names, gathered[:-1]))
    (b_norm_f, b_lg_f, b_lb_f, c_norm_f, c_cw_f, c_cb_f, c_lg_f, c_lb_f, d_norm_f) = [
        _cols_from_dev(t) for t in _unpack(gathered[-1], [s.shape for s in small_sharded], lead=(N_DEV,))]
    c_cw_f = c_cw_f[0]

    wm = jnp.tril(b_w_s[0]).astype(BF16)
    bs_t = jnp.pad(b_b_s[0].T, ((0, 0), (0, LANES - G)))

    x0 = x.reshape(T, D)
    h_a = _rmsnorm_fwd(x0, a_norm, "rms_a")
    proj_a = _mm_w_dev(h_a, wg["a_w_in"], "proj_a").reshape(B, S, 4 * W_A)
    (o_a, y_a), later = _sb_fwd(proj_a, W_A, SB_HEADS,
                                _Exchange([big_w[n].astype(BF16) for n in later_names], ["gather"] * len(later_names)))
    wg.update(zip(later_names, later))
    a_w_out_f = wg["a_w_out"].reshape(W_A, D)
    b_w_out_f = wg["b_w_out"].reshape(W_B, D)
    c_w_out_f = wg["c_w_out"].reshape(W_C, D)
    y_a = y_a.reshape(T, W_A)
    x1, h_b = _mm(y_a, a_w_out_f, "nn", T, D, W_A, F32, "out_a", 512, D, W_A, res=x0, norm_gain=b_norm_f)
    proj_b = _mm_w_dev(h_b, wg["b_w_in"], "proj_b")
    y_b = _gmlp_fwd(proj_b, wm, bs_t, b_lg_f, b_lb_f, W_B)
    x2, h_c = _mm(y_b, b_w_out_f, "nn", T, D, W_B, F32, "out_b", 512, D, W_B, res=x1, norm_gain=c_norm_f)
    proj_c = _mm_w_dev(h_c, wg["c_w_in"], "proj_c").reshape(B, S, 3 * W_C)
    y_c, last = _conv_fwd(proj_c, c_cw_f, c_cb_f, c_lg_f, c_lb_f, W_C,
                          _Exchange([big_w[n].astype(BF16) for n in last_names], ["gather"] * len(last_names)))
    wg.update(zip(last_names, last))
    d_w_out_f = wg["d_w_out"].reshape(W_D, D)
    d_w_in_f = jnp.pad(_cols_from_dev(wg["d_w_in"]), ((0, 0), (0, N_D_PAD - N_D)))
    y_c = y_c.reshape(T, W_C)
    x3, h_d = _mm(y_c, c_w_out_f, "nn", T, D, W_C, F32, "out_c", 512, D, W_C, res=x2, norm_gain=d_norm_f)
    proj_d = _mm(h_d, d_w_in_f, "nn", T, N_D_PAD, D, F32, "proj_d", 1024, 384, D).reshape(B, S, N_D_PAD)
    f_t = jnp.swapaxes(proj_d[:, :, 4 * W_D:4 * W_D + H_D], 1, 2)
    b_f_col = d_b_f.reshape(H_D, 1)
    cum_t = _fox_gate_fwd(f_t, b_f_col)
    o_d, y_d, lse_d = _fox_fwd(proj_d, cum_t, W_D, H_D)
    y_d = y_d.reshape(T, W_D)
    x4 = _mm(y_d, d_w_out_f, "nn", T, D, W_D, F32, "out_d", 512, D, W_D, res=x3)

    loss_part, dx, g_final = _loss_head(x4, final_norm.reshape(1, D), loss_target.reshape(T, D))
    loss = lax.psum(loss_part[0, 0], MESH_AXES)

    dy_d = _mm(dx, d_w_out_f, "nt", T, W_D, D, BF16, "dy_d", 512, W_D, D).reshape(B, S, W_D)
    gw_d_out = _mm(y_d, dx, "tn", W_D, D, T, BF16, "gw_d_out", W_D, D, 512).reshape(N_DEV, W_D // N_DEV, D)
    dproj_d, dcum = _fox_bwd(proj_d, cum_t, o_d, lse_d, dy_d, W_D, H_D)
    df_t, g_b_f = _fox_gate_bwd(dcum.reshape(B, H_D, S), f_t, b_f_col)
    F_PAD = N_D_PAD - 4 * W_D
    df = jnp.pad(jnp.swapaxes(df_t, 1, 2), ((0, 0), (0, 0), (0, F_PAD - H_D))).reshape(T, F_PAD)
    tc, tr = min(512, W_D), min(1024, S)
    gw_main = _mm(h_d, dproj_d, "tn", D, 4 * W_D, T, BF16, "gw_d_in", D, tc, tr,
                  b_spec=_sectioned_spec(dproj_d, tr, tc, 2, 1))
    gw_f = _mm(h_d, df, "tn", D, F_PAD, T, BF16, "gw_d_in_f", D, F_PAD, 512)
    gw_d_in = jnp.moveaxis(
        jnp.concatenate([gw_main, gw_f], axis=1)[:, :N_D].reshape(D, N_DEV, N_D // N_DEV), 1, 0)
    dh_f = _mm(df, d_w_in_f[:, 4 * W_D:], "nt", T, D, F_PAD, F32, "dh_d_f", 512, D, F_PAD)
    dx, g_d_norm, _ = _mm(dproj_d, d_w_in_f, "nt", T, D, 4 * W_D, F32, "dh_d", tr, D, tc,
                          a_spec=_sectioned_spec(dproj_d, tr, tc, 0, 2), res=dh_f, norm_bwd=(x3, d_norm_f, dx))

    dy_c = _mm(dx, c_w_out_f, "nt", T, W_C, D, BF16, "dy_c", 512, W_C, D).reshape(B, S, W_C)
    gw_c_out = _mm(y_c, dx, "tn", W_C, D, T, BF16, "gw_c_out", 1024, D, 512).reshape(N_DEV, W_C // N_DEV, D)
    (dy1, dgate_c, g_c_cw, g_c_cb, g_c_lg, g_c_lb), parts_d = _conv_bwd1(
        proj_c, dy_c, c_cw_f, c_cb_f, c_lg_f, c_lb_f, W_C, _Exchange([gw_d_in, gw_d_out], ["scatter"] * 2))
    dproj_c = _conv_bwd2(proj_c, dy1, dgate_c, c_cw_f[::-1], W_C).reshape(T, 3 * W_C)
    gw_c_in = _mm_grad_dev(h_c, dproj_c, "gw_c_in")
    dx, g_c_norm, _ = _mm_wT_dev(dproj_c, wg["c_w_in"], "dh_c", norm_bwd=(x2, c_norm_f, dx))

    dy_b = _mm(dx, b_w_out_f, "nt", T, W_B, D, BF16, "dy_b", 512, W_B, D)
    gw_b_out = _mm(y_b, dx, "tn", W_B, D, T, BF16, "gw_b_out", 1024, D, 512).reshape(N_DEV, W_B // N_DEV, D)
    dproj_b, g_wm, g_bs_t, g_b_lg, g_b_lb = _gmlp_bwd(proj_b, dy_b, wm, bs_t, b_lg_f, b_lb_f, W_B)
    g_b_w_s = jnp.tril(g_wm)
    g_b_b_s = g_bs_t[:, :G].T
    gw_b_in = _mm_grad_dev(h_b, dproj_b, "gw_b_in")
    dx, g_b_norm, _ = _mm_wT_dev(dproj_b, wg["b_w_in"], "dh_b", norm_bwd=(x1, b_norm_f, dx))

    dy_a = _mm(dx, a_w_out_f, "nt", T, W_A, D, BF16, "dy_a", 512, W_A, D).reshape(B, S, W_A)
    gw_a_out = _mm(y_a, dx, "tn", W_A, D, T, BF16, "gw_a_out", W_A, D, 512).reshape(N_DEV, W_A // N_DEV, D)
    small_full = [g_b_norm, g_b_lg, g_b_lb, g_b_b_s, g_c_norm, g_c_cw, g_c_cb, g_c_lg, g_c_lb,
                  g_d_norm, g_b_f, g_final]
    dproj_a, parts_s = _sb_bwd(
        proj_a, o_a, dy_a, W_A, SB_HEADS,
        _Exchange([gw_c_in, gw_c_out, gw_b_in, gw_b_out, gw_a_out, _pack(small_full), g_b_w_s.reshape(-1, LANES)],
                  ["scatter"] * 5 + ["gather"] * 2))
    gw_a_in = _mm_grad_dev(h_a, dproj_a, "gw_a_in")
    dx, g_a_norm, parts_a = _mm_wT_dev(dproj_a, wg["a_w_in"], "dh_a", exch=_Exchange([gw_a_in], ["scatter"]),
                                       norm_bwd=(x0, a_norm, dx))
    grad_x = dx.reshape(B, S, D)

    (parts_n,) = _Exchange([_pack([g_a_norm])], ["gather"]).run("exchange_last")
    big_parts = dict(a_w_in=parts_a[0], a_w_out=parts_s[4], b_w_in=parts_s[2], b_w_out=parts_s[3],
                     c_w_in=parts_s[0], c_w_out=parts_s[1], d_w_in=parts_d[0], d_w_out=parts_d[1])
    (s_b_norm, s_b_lg, s_b_lb, s_b_b_s, s_c_norm, s_c_cw, s_c_cb, s_c_lg, s_c_lb,
     s_d_norm, s_b_f, s_final) = _unpack(_sum_parts(parts_s[5], "sum_small"), [g.shape for g in small_full])
    (s_a_norm,) = _unpack(_sum_parts(parts_n, "sum_a_norm"), [g_a_norm.shape])

    weights = dict(a_norm=a_norm, a_w_in=a_w_in, a_w_out=a_w_out, b_norm=b_norm, b_w_in=b_w_in, b_v_ln_g=b_v_ln_g,
                   b_v_ln_b=b_v_ln_b, b_w_s=b_w_s, b_b_s=b_b_s, b_w_out=b_w_out, c_norm=c_norm, c_w_in=c_w_in,
                   c_conv_w=c_conv_w, c_conv_b=c_conv_b, c_ln_g=c_ln_g, c_ln_b=c_ln_b, c_w_out=c_w_out,
                   d_norm=d_norm, d_w_in=d_w_in, d_b_f=d_b_f, d_w_out=d_w_out, final_norm=final_norm)
    mom_m = dict(a_norm=m_a_norm, a_w_in=m_a_w_in, a_w_out=m_a_w_out, b_norm=m_b_norm, b_w_in=m_b_w_in,
                 b_v_ln_g=m_b_v_ln_g, b_v_ln_b=m_b_v_ln_b, b_w_s=m_b_w_s, b_b_s=m_b_b_s, b_w_out=m_b_w_out,
                 c_norm=m_c_norm, c_w_in=m_c_w_in, c_conv_w=m_c_conv_w, c_conv_b=m_c_conv_b, c_ln_g=m_c_ln_g,
                 c_ln_b=m_c_ln_b, c_w_out=m_c_w_out, d_norm=m_d_norm, d_w_in=m_d_w_in, d_b_f=m_d_b_f,
                 d_w_out=m_d_w_out, final_norm=m_final_norm)
    mom_v = dict(a_norm=v_a_norm, a_w_in=v_a_w_in, a_w_out=v_a_w_out, b_norm=v_b_norm, b_w_in=v_b_w_in,
                 b_v_ln_g=v_b_v_ln_g, b_v_ln_b=v_b_v_ln_b, b_w_s=v_b_w_s, b_b_s=v_b_b_s, b_w_out=v_b_w_out,
                 c_norm=v_c_norm, c_w_in=v_c_w_in, c_conv_w=v_c_conv_w, c_conv_b=v_c_conv_b, c_ln_g=v_c_ln_g,
                 c_ln_b=v_c_ln_b, c_w_out=v_c_w_out, d_norm=v_d_norm, d_w_in=v_d_w_in, d_b_f=v_d_b_f,
                 d_w_out=v_d_w_out, final_norm=v_final_norm)
    order = list(weights)
    grads, deltas, new_m, new_v = {}, {}, {}, {}

    for n in big_names:
        part = big_parts[n]
        shp = weights[n].shape
        R, C = shp[1], shp[2]
        res = _adamw(part, weights[n].reshape(R, C), mom_m[n].reshape(R, C), mom_v[n].reshape(R, C), "adamw_" + n)
        grads[n], deltas[n], new_m[n], new_v[n] = [r.reshape(shp) for r in res]

    res = _adamw(parts_s[6], b_w_s.reshape(-1, LANES), m_b_w_s.reshape(-1, LANES), v_b_w_s.reshape(-1, LANES),
                 "adamw_b_w_s")
    grads["b_w_s"], deltas["b_w_s"], new_m["b_w_s"], new_v["b_w_s"] = [r.reshape(b_w_s.shape) for r in res]

    small_g = dict(
        a_norm=s_a_norm, b_norm=_my_cols(s_b_norm, me), b_v_ln_g=_my_cols(s_b_lg, me),
        b_v_ln_b=_my_cols(s_b_lb, me), b_b_s=s_b_b_s[None], c_norm=_my_cols(s_c_norm, me),
        c_conv_w=_my_cols(s_c_cw, me)[None], c_conv_b=_my_cols(s_c_cb, me), c_ln_g=_my_cols(s_c_lg, me),
        c_ln_b=_my_cols(s_c_lb, me), d_norm=_my_cols(s_d_norm, me), d_b_f=s_b_f.reshape(1, H_D),
        final_norm=s_final.reshape(D))
    small_names = list(small_g)
    sg_p = _pack([small_g[n] for n in small_names])
    res = _adamw(sg_p[None], _pack([weights[n] for n in small_names]), _pack([mom_m[n] for n in small_names]),
                 _pack([mom_v[n] for n in small_names]), "adamw_small")
    shapes = [weights[n].shape for n in small_names]
    for dst, r in zip((grads, deltas, new_m, new_v), res):
        for n, val in zip(small_names, _unpack(r, shapes)):
            dst[n] = val

    return (loss, grad_x, *[grads[n] for n in order], *[deltas[n] for n in order],
            *[new_m[n] for n in order], *[new_v[n] for n in order])
```

```python
import functools
import math

import jax
import jax.numpy as jnp
from jax import lax
from jax.experimental import pallas as pl
from jax.experimental.pallas import tpu as pltpu

F32 = jnp.float32
BF16 = jnp.bfloat16

EPS = 1e-6
SB_HEADS = 16
CONV_HALO = 32
BLK = 128
ATT_TK = 256
ATT_TQ = 512
ATT_GP = 2
LANES = 128
N_DEV = 8
MESH_AXES = ("x", "y", "c")

ADAM_LR = 0.001
ADAM_B1 = 0.9
ADAM_B2 = 0.999
ADAM_EPS = 1e-08
ADAM_WD = 0.01
ADAM_STEP = 10

VMEM_LIMIT = 56 * 1024 * 1024
NEG_BIG = -1e30

_NN = (((1,), (0,)), ((), ()))
_NT = (((1,), (1,)), ((), ()))
_TN = (((0,), (0,)), ((), ()))


def _dot(a, b, dims=_NN):
    return lax.dot_general(a, b, dims, preferred_element_type=F32)


def _split_dot(x, m):
    hi = x.astype(BF16)
    lo = (x - hi.astype(F32)).astype(BF16)
    return _dot(hi, m) + _dot(lo, m)


def _split3_dot(x, m):
    hi = x.astype(BF16)
    r1 = x - hi.astype(F32)
    mid = r1.astype(BF16)
    lo = (r1 - mid.astype(F32)).astype(BF16)
    return _dot(hi, m) + _dot(mid, m) + _dot(lo, m)


def _params(sem=None):
    kw = dict(vmem_limit_bytes=VMEM_LIMIT)
    if sem is not None:
        kw["dimension_semantics"] = sem
    return pltpu.CompilerParams(**kw)


def _sigmoid(x):
    return jax.nn.sigmoid(x)


def _silu(x):
    return x * _sigmoid(x)


def _dsilu(x):
    s = _sigmoid(x)
    return s * (1.0 + x * (1.0 - s))


_GELU_C = math.sqrt(2.0 / math.pi)


def _gelu(x):
    return 0.5 * x * (1.0 + jnp.tanh(_GELU_C * (x + 0.044715 * x * x * x)))


def _dgelu(x):
    th = jnp.tanh(_GELU_C * (x + 0.044715 * x * x * x))
    return 0.5 * (1.0 + th) + 0.5 * x * (1.0 - th * th) * _GELU_C * (1.0 + 3.0 * 0.044715 * x * x)


def _mm(a, b, mode, M, N, K, out_dtype, name, tm, tn, tk, a_spec=None, b_spec=None, o_spec=None, out_shape=None,
        exch=None, res=None, norm_gain=None, norm_bwd=None):
    tm, tn, tk = min(tm, M), min(tn, N), min(tk, K)
    assert M % tm == 0 and N % tn == 0 and K % tk == 0, (name, M, N, K, tm, tn, tk)
    nk = K // tk
    assert norm_gain is None or (nk == 1 and tn == N and exch is None)
    dims = {"nn": _NN, "nt": _NT, "tn": _TN}[mode]
    if a_spec is None:
        a_spec = (pl.BlockSpec((tk, tm), lambda i, j, k: (k, i)) if mode == "tn"
                  else pl.BlockSpec((tm, tk), lambda i, j, k: (i, k)))
    if b_spec is None:
        b_spec = (pl.BlockSpec((tn, tk), lambda i, j, k: (j, k)) if mode == "nt"
                  else pl.BlockSpec((tk, tn), lambda i, j, k: (k, j)))
    if o_spec is None:
        o_spec = pl.BlockSpec((tm, tn), lambda i, j, k: (i, j))
    if out_shape is None:
        out_shape = (M, N)

    def body(a_ref, b_ref, *rest):
        res_ref = rest[0] if res is not None else None
        if nk == 1:
            d = _dot(a_ref[...].astype(BF16), b_ref[...].astype(BF16), dims)
            r = d if res_ref is None else res_ref[...].astype(F32) + d
            if norm_gain is None:
                rest[-1][...] = r.astype(rest[-1].dtype)
            else:
                g_ref, o_ref, h_ref = rest[-3:]
                o_ref[...] = r.astype(o_ref.dtype)
                scale = lax.rsqrt(jnp.mean(r * r, axis=-1, keepdims=True) + EPS)
                h_ref[...] = (r * scale * g_ref[...]).astype(BF16)
            return
        i = pl.program_id(0)
        k = pl.program_id(2)
        if norm_bwd is not None:
            x_ref, g_ref, dres_ref, o_ref, dg_ref, acc_ref = rest[-6:]

            @pl.when(jnp.logical_and(i == 0, k == 0))
            def _():
                dg_ref[...] = jnp.zeros_like(dg_ref)
        else:
            o_ref, acc_ref = rest[-2:]

        @pl.when(k == 0)
        def _():
            acc_ref[...] = jnp.zeros_like(acc_ref) if res_ref is None else res_ref[...].astype(F32)

        acc_ref[...] += _dot(a_ref[...].astype(BF16), b_ref[...].astype(BF16), dims)

        @pl.when(k == nk - 1)
        def _():
            if norm_bwd is None:
                o_ref[...] = acc_ref[...].astype(o_ref.dtype)
            else:
                dh = acc_ref[...]
                xv = x_ref[...]
                r = lax.rsqrt(jnp.mean(xv * xv, axis=-1, keepdims=True) + EPS)
                xh = xv * r
                dxh = dh * g_ref[...]
                o_ref[...] = dres_ref[...] + r * (dxh - xh * jnp.mean(dxh * xh, axis=-1, keepdims=True))
                dg_ref[...] += jnp.sum(dh * xh, axis=0, keepdims=True)

    in_specs, args = [a_spec, b_spec], (a, b)
    if res is not None:
        in_specs, args = in_specs + [o_spec], args + (res,)
    scratch = [pltpu.VMEM((tm, tn), F32)] if nk > 1 else []
    if norm_bwd is not None:
        assert nk > 1 and tn == N and norm_gain is None
        vec = pl.BlockSpec((1, N), lambda i, j, k: (0, 0))
        x_in, g_in, dres_in = norm_bwd
        (dx, dg), moved = _call_hosting(
            body, exch, name, (M // tm, 1, nk), in_specs + [o_spec, vec, o_spec], [o_spec, vec],
            [jax.ShapeDtypeStruct((M, N), F32), jax.ShapeDtypeStruct((1, N), F32)], scratch,
            args + (x_in, g_in, dres_in))
        return dx, dg, moved
    if norm_gain is not None:
        return pl.pallas_call(
            body, name=name, grid=(M // tm, N // tn, nk),
            in_specs=in_specs + [pl.BlockSpec((1, N), lambda i, j, k: (0, 0))], out_specs=[o_spec, o_spec],
            out_shape=[jax.ShapeDtypeStruct(out_shape, out_dtype), jax.ShapeDtypeStruct(out_shape, BF16)],
            compiler_params=_params(("parallel", "parallel", "arbitrary")),
        )(*args, norm_gain)
    if exch is None:
        return pl.pallas_call(
            body, name=name, grid=(M // tm, N // tn, nk),
            in_specs=in_specs, out_specs=o_spec,
            out_shape=jax.ShapeDtypeStruct(out_shape, out_dtype),
            scratch_shapes=scratch,
            compiler_params=_params(("parallel", "parallel", "arbitrary")),
        )(*args)
    (out,), moved = _call_hosting(
        body, exch, name, (M // tm, N // tn, nk), in_specs, [o_spec],
        [jax.ShapeDtypeStruct(out_shape, out_dtype)], scratch, args)
    return out, moved


def _mm_w_dev(a, w3, name, out_dtype=F32, tm=1024):
    M, K = a.shape
    n8 = w3.shape[2]
    tn = n8 if n8 <= 768 else 512
    per = n8 // tn
    b_spec = pl.BlockSpec((None, K, tn), lambda i, j, k: (j // per, 0, j % per))
    return _mm(a, w3, "nn", M, N_DEV * n8, K, out_dtype, name, tm, tn, K, b_spec=b_spec)


def _sectioned_spec(d4, t_rows, t_cols, rows_axis, cols_axis):
    _, _, S, W = d4.shape
    assert S % t_rows == 0 and W % t_cols == 0
    rb, cb = S // t_rows, W // t_cols

    def index(*g):
        r, c = g[rows_axis], g[cols_axis]
        return (r // rb, c // cb, r % rb, c % cb)

    return pl.BlockSpec((None, None, t_rows, t_cols), index)


def _mm_wT_dev(a, w3, name, out_dtype=F32, tm=1024, exch=None, norm_bwd=None):
    K, n8 = w3.shape[1], w3.shape[2]
    tk = n8 if n8 <= 768 else 512
    per = n8 // tk
    b_spec = pl.BlockSpec((None, K, tk), lambda i, j, k: (k // per, 0, k % per))
    if a.ndim == 4:
        M, N = a.shape[0] * a.shape[2], a.shape[1] * a.shape[3]
        tm = min(tm, a.shape[2])
        a_spec = _sectioned_spec(a, tm, tk, 0, 2)
    else:
        (M, N), a_spec = a.shape, None
    return _mm(a, w3, "nt", M, K, N, out_dtype, name, tm, K, tk, a_spec=a_spec, b_spec=b_spec, exch=exch,
               norm_bwd=norm_bwd)


def _mm_grad_dev(h, d, name, out_dtype=BF16):
    T, M = h.shape
    N = d.shape[1] * d.shape[3] if d.ndim == 4 else d.shape[1]
    n8 = N // N_DEV
    tn = n8 if n8 <= 768 else 512
    per = n8 // tn
    tm = min(M, 1024)
    o_spec = pl.BlockSpec((None, tm, tn), lambda i, j, k: (j // per, i, j % per))
    tk = min(1024, d.shape[2] if d.ndim == 4 else T)
    b_spec = _sectioned_spec(d, tk, tn, 2, 1) if d.ndim == 4 else None
    return _mm(h, d, "tn", M, N, T, out_dtype, name, tm, tn, tk, b_spec=b_spec, o_spec=o_spec,
               out_shape=(N_DEV, M, n8))


def _me():
    x, y, c = lax.axis_index("x"), lax.axis_index("y"), lax.axis_index("c")
    return x, y, c


def _peer(r):
    x, y, c = _me()
    px = 1 - x if (r >> 2) & 1 else x
    py = 1 - y if (r >> 1) & 1 else y
    pc = 1 - c if r & 1 else c
    return (px, py, pc), 4 * px + 2 * py + pc


class _Exchange:
    def __init__(self, arrays, kinds):
        self.arrays, self.kinds, self.n = list(arrays), list(kinds), len(arrays)
        self.out_shapes = [
            jax.ShapeDtypeStruct((N_DEV,) + a.shape if kind == "gather" else a.shape, a.dtype)
            for a, kind in zip(arrays, kinds)]
        self.specs = [pl.BlockSpec(memory_space=pl.ANY)] * self.n
        self.sems = [pltpu.SemaphoreType.DMA((self.n, N_DEV - 1)), pltpu.SemaphoreType.DMA((self.n, N_DEV - 1)),
                     pltpu.SemaphoreType.DMA((self.n,))]

    def _copies(self, ins, outs, sems, receiving):
        send_sems, recv_sems, local_sems = sems
        x, y, c = _me()
        me = 4 * x + 2 * y + c

        def src(k, pid):
            return ins[k] if self.kinds[k] == "gather" else ins[k].at[pid]

        local = [pltpu.make_async_copy(src(k, me), outs[k].at[me], local_sems.at[k]) for k in range(self.n)]
        remote = []
        for r in range(1, N_DEV):
            peer, pid = _peer(r)
            for k in range(self.n):
                remote.append(pltpu.make_async_remote_copy(
                    src_ref=src(k, pid), dst_ref=outs[k].at[pid if receiving else me],
                    send_sem=send_sems.at[k, r - 1], recv_sem=recv_sems.at[k, r - 1],
                    device_id=peer, device_id_type=pl.DeviceIdType.MESH))
        return local, remote

    def start(self, ins, outs, sems):
        local, remote = self._copies(ins, outs, sems, False)
        for cp in local + remote:
            cp.start()

    def wait(self, ins, outs, sems):
        local, remote = self._copies(ins, outs, sems, True)
        for cp in remote:
            cp.wait_recv()
        for cp in remote:
            cp.wait_send()
        for cp in local:
            cp.wait()

    def run(self, name):
        n = self.n

        def body(*refs):
            ins, outs, sems = refs[:n], refs[n:2 * n], refs[2 * n:]
            self.start(ins, outs, sems)
            self.wait(ins, outs, sems)

        return pl.pallas_call(
            body, name=name, in_specs=self.specs, out_specs=self.specs, out_shape=self.out_shapes,
            scratch_shapes=self.sems,
        )(*self.arrays)


class _GatherViaSibling(_Exchange):
    ICI = (2, 4, 6)

    def __init__(self, arrays):
        super().__init__(arrays, ["gather"] * len(arrays))

    def _copy(self, ins, outs, sems, k, column, block, to, from_input=False):
        return pltpu.make_async_remote_copy(
            src_ref=ins[k] if from_input else outs[k].at[block], dst_ref=outs[k].at[block],
            send_sem=sems[0].at[k, column], recv_sem=sems[1].at[k, column],
            device_id=to, device_id_type=pl.DeviceIdType.MESH)

    def start(self, ins, outs, sems):
        x, y, c = _me()
        me = 4 * x + 2 * y + c
        for k in range(self.n):
            pltpu.make_async_copy(ins[k], outs[k].at[me], sems[2].at[k]).start()
            self._copy(ins, outs, sems, k, 0, me, _peer(1)[0], True).start()
            for j, r in enumerate(self.ICI):
                self._copy(ins, outs, sems, k, 1 + j, me, _peer(r)[0], True).start()

    def wait(self, ins, outs, sems):
        x, y, c = _me()
        me = 4 * x + 2 * y + c
        sibling, sibling_id = _peer(1)
        for j, r in enumerate(self.ICI):
            peer, pid = _peer(r)
            for k in range(self.n):
                self._copy(ins, outs, sems, k, 1 + j, pid, peer).wait_recv()
                self._copy(ins, outs, sems, k, 4 + j, pid, sibling).start()
        for k in range(self.n):
            self._copy(ins, outs, sems, k, 0, sibling_id, sibling).wait_recv()
            for j, r in enumerate(self.ICI):
                self._copy(ins, outs, sems, k, 4 + j, _peer(r ^ 1)[1], sibling).wait_recv()
            for column in range(N_DEV - 1):
                self._copy(ins, outs, sems, k, column, me, sibling).wait_send()
            pltpu.make_async_copy(ins[k], outs[k].at[me], sems[2].at[k]).wait()


def _call_hosting(body, exch, name, grid, in_specs, out_specs, out_shape, scratch_shapes, args):
    if exch is None:
        res = pl.pallas_call(
            body, name=name, grid=grid, in_specs=list(in_specs), out_specs=list(out_specs),
            out_shape=list(out_shape), scratch_shapes=list(scratch_shapes),
            compiler_params=_params(("arbitrary",) * len(grid)))(*args)
        return res, []
    n_in, n_out, n_scr, nc = len(in_specs), len(out_specs), len(scratch_shapes), exch.n

    def full_body(*refs):
        ins, refs = refs[:n_in], refs[n_in:]
        cins, refs = refs[:nc], refs[nc:]
        outs, refs = refs[:n_out], refs[n_out:]
        couts, refs = refs[:nc], refs[nc:]
        scr, sems = refs[:n_scr], refs[n_scr:]
        ids = [pl.program_id(a) for a in range(len(grid))]
        first = functools.reduce(jnp.logical_and, [i == 0 for i in ids])
        last = functools.reduce(jnp.logical_and, [i == g - 1 for i, g in zip(ids, grid)])

        @pl.when(first)
        def _():
            exch.start(cins, couts, sems)

        body(*ins, *outs, *scr)

        @pl.when(last)
        def _():
            exch.wait(cins, couts, sems)

    res = pl.pallas_call(
        full_body, name=name, grid=grid,
        in_specs=list(in_specs) + exch.specs, out_specs=list(out_specs) + exch.specs,
        out_shape=list(out_shape) + exch.out_shapes,
        scratch_shapes=list(scratch_shapes) + exch.sems,
        compiler_params=_params(("arbitrary",) * len(grid)),
    )(*args, *exch.arrays)
    return res[:n_out], res[n_out:]


def _rmsnorm_fwd(x, g, name):
    T, D = x.shape
    tr = min(256, T)

    def body(x_ref, g_ref, h_ref):
        xv = x_ref[...]
        r = lax.rsqrt(jnp.mean(xv * xv, axis=-1, keepdims=True) + EPS)
        h_ref[...] = (xv * r * g_ref[...]).astype(BF16)

    return pl.pallas_call(
        body, name=name, grid=(T // tr,),
        in_specs=[pl.BlockSpec((tr, D), lambda i: (i, 0)), pl.BlockSpec((1, D), lambda i: (0, 0))],
        out_specs=pl.BlockSpec((tr, D), lambda i: (i, 0)),
        out_shape=jax.ShapeDtypeStruct((T, D), BF16),
        compiler_params=_params(("parallel",)),
    )(x, g)


def _loss_head(x, g, target):
    T, D = x.shape
    tr = min(256, T)

    def body(x_ref, g_ref, t_ref, loss_ref, dx_ref, dg_ref):
        i = pl.program_id(0)
        xv = x_ref[...]
        gv = g_ref[...]
        r = lax.rsqrt(jnp.mean(xv * xv, axis=-1, keepdims=True) + EPS)
        xh = xv * r
        diff = xh * gv - t_ref[...]
        dy = diff * (1.0 / D)
        dxh = dy * gv
        dx_ref[...] = r * (dxh - xh * jnp.mean(dxh * xh, axis=-1, keepdims=True))

        @pl.when(i == 0)
        def _():
            dg_ref[...] = jnp.zeros_like(dg_ref)
            loss_ref[...] = jnp.zeros_like(loss_ref)

        dg_ref[...] += jnp.sum(dy * xh, axis=0, keepdims=True)
        part = jnp.sum(jnp.sum(diff * diff, axis=1, keepdims=True), axis=0, keepdims=True)
        loss_ref[...] += (0.5 / D) * part

    row = pl.BlockSpec((tr, D), lambda i: (i, 0))
    vec = pl.BlockSpec((1, D), lambda i: (0, 0))
    return pl.pallas_call(
        body, name="loss_head", grid=(T // tr,),
        in_specs=[row, vec, row],
        out_specs=[pl.BlockSpec((1, 1), lambda i: (0, 0)), row, vec],
        out_shape=[jax.ShapeDtypeStruct((1, 1), F32), jax.ShapeDtypeStruct((T, D), F32),
                   jax.ShapeDtypeStruct((1, D), F32)],
        compiler_params=_params(("arbitrary",)),
    )(x, g, target)


ELEMS_PER_STEP = 1 << 20


def _row_tile(R, per_row):
    best = None
    for tr in range(8, R + 1, 8):
        if R % tr == 0 and tr * per_row <= ELEMS_PER_STEP:
            best = tr
    return best if best is not None else R


def _adamw(parts, w, m, v, name):
    P, R, C = parts.shape
    tr = _row_tile(R, P * C)

    def body(p_ref, w_ref, m_ref, v_ref, g_out, d_out, m_out, v_out):
        g = p_ref[0].astype(F32)
        for p in range(1, P):
            g = g + p_ref[p].astype(F32)
        wv = w_ref[...]
        mn = ADAM_B1 * m_ref[...] + (1.0 - ADAM_B1) * g
        vn = ADAM_B2 * v_ref[...] + (1.0 - ADAM_B2) * (g * g)
        m_hat = mn / (1.0 - ADAM_B1 ** ADAM_STEP)
        v_hat = vn / (1.0 - ADAM_B2 ** ADAM_STEP)
        g_out[...] = g
        d_out[...] = -ADAM_LR * (m_hat / (jnp.sqrt(v_hat) + ADAM_EPS) + ADAM_WD * wv)
        m_out[...] = mn
        v_out[...] = vn

    row = pl.BlockSpec((tr, C), lambda i: (i, 0))
    return pl.pallas_call(
        body, name=name, grid=(R // tr,),
        in_specs=[pl.BlockSpec((P, tr, C), lambda i: (0, i, 0)), row, row, row],
        out_specs=[row, row, row, row],
        out_shape=[jax.ShapeDtypeStruct((R, C), F32)] * 4,
        compiler_params=_params(("parallel",)),
    )(parts, w, m, v)


def _sum_parts(parts, name):
    P, R, C = parts.shape
    tr = _row_tile(R, P * C)

    def body(p_ref, o_ref):
        g = p_ref[0]
        for p in range(1, P):
            g = g + p_ref[p]
        o_ref[...] = g

    return pl.pallas_call(
        body, name=name, grid=(R // tr,),
        in_specs=[pl.BlockSpec((P, tr, C), lambda i: (0, i, 0))],
        out_specs=pl.BlockSpec((tr, C), lambda i: (i, 0)),
        out_shape=jax.ShapeDtypeStruct((R, C), F32),
        compiler_params=_params(("parallel",)),
    )(parts)


def _lane_head(Dh):
    assert Dh & (Dh - 1) == 0 and Dh <= LANES
    return lax.shift_right_logical(lax.broadcasted_iota(jnp.int32, (1, LANES), 1), Dh.bit_length() - 1)


def _stack_heads(x, lane_head, hpb):
    return jnp.concatenate([jnp.where(lane_head == h, x, 0.0) for h in range(hpb)], axis=0)


def _unstack_heads(acc, lane_head, hpb):
    TQ = acc.shape[0] // hpb
    out = acc[0:TQ]
    for h in range(1, hpb):
        out = jnp.where(lane_head == h, acc[h * TQ:(h + 1) * TQ], out)
    return out


def _live_rows(x, r0, hpb):
    if r0 == 0:
        return x
    TQ = x.shape[0] // hpb
    return jnp.concatenate([x[h * TQ + r0:(h + 1) * TQ] for h in range(hpb)], axis=0)


def _put_rows(full, part, r0, hpb):
    if r0 == 0:
        return part
    TQ = full.shape[0] // hpb
    n = TQ - r0
    return jnp.concatenate(
        [blk for h in range(hpb) for blk in (full[h * TQ:h * TQ + r0], part[h * n:(h + 1) * n])], axis=0)


def _first_live_row(m, TQ, TK):
    return max(0, TQ - (m + 1) * TK)


def _key_tile(S):
    return ATT_TK if S % ATT_TK == 0 else BLK


def _query_tile(S):
    return ATT_TQ if S % ATT_TQ == 0 else BLK


def _lane_groups(P):
    return ATT_GP if P % ATT_GP == 0 else 1


def _lanes(u):
    return slice(u * LANES, (u + 1) * LANES)


def _causal_iotas(RS, TK, TQ, r0=0):
    n = TQ - r0
    assert n & (n - 1) == 0 and (TK % TQ == 0 or TQ % TK == 0)
    rows = RS // TQ * n
    trow = jnp.bitwise_and(lax.broadcasted_iota(jnp.int32, (rows, TK), 0), n - 1) + r0
    col = lax.broadcasted_iota(jnp.int32, (rows, TK), 1)
    return trow, col


def _tri(TK, op):
    r = lax.broadcasted_iota(jnp.int32, (TK, TK), 0)
    c = lax.broadcasted_iota(jnp.int32, (TK, TK), 1)
    return op(r, c).astype(BF16)


def _logsig_parts(z):
    lb = jnp.minimum(z, 0.0) - jnp.log(1.0 + jnp.exp(-jnp.abs(z)))
    return lb, lb - z


def _sb_fwd(proj3, W, heads, exch):
    B, S, _ = proj3.shape
    Dh = W // heads
    hpb = LANES // Dh
    P, TQ = W // LANES, _query_tile(S)
    NQ = S // TQ
    scale = 1.0 / math.sqrt(Dh)

    TK = _key_tile(S)
    RS = hpb * TQ
    NM = max(1, TQ // TK)
    GP = _lane_groups(P)
    PG = P // GP

    def body(q_ref, k_ref, v_ref, g_ref, o_ref, y_ref):
        i = pl.program_id(2)
        lane_head = _lane_head(Dh)
        msuf = _tri(TK, lambda r, c: r > c)
        qs = [(_stack_heads(q_ref[:, _lanes(u)], lane_head, hpb) * scale).astype(BF16) for u in range(GP)]
        nt = (i * TQ + TQ - 2) // TK + 1

        def tile(jt, carry, masked, r0=0):
            off = pl.multiple_of(jt * TK, TK)
            if masked:
                trow, col = _causal_iotas(RS, TK, TQ, r0)
                msk = col + (jt * TK - i * TQ) < trow
            out = []
            for u, (rem_all, acc_all) in enumerate(carry):
                rem, acc = _live_rows(rem_all, r0, hpb), _live_rows(acc_all, r0, hpb)
                kj = k_ref[pl.ds(off, TK), _lanes(u)].astype(BF16)
                vj = v_ref[pl.ds(off, TK), _lanes(u)].astype(BF16)
                lb, lr = _logsig_parts(_dot(_live_rows(qs[u], r0, hpb), kj, _NT))
                if masked:
                    lr = jnp.where(msk, lr, 0.0)
                w = jnp.exp(lb + _split_dot(lr, msuf) + rem)
                if masked:
                    w = jnp.where(msk, w, 0.0)
                out.append((_put_rows(rem_all, rem + jnp.sum(lr, axis=1, keepdims=True), r0, hpb),
                            _put_rows(acc_all, acc + _dot(w.astype(BF16), vj), r0, hpb)))
            return tuple(out)

        zero = (jnp.zeros((RS, 1), F32), jnp.zeros((RS, LANES), F32))
        carry = (zero,) * GP
        for m in range(NM):
            carry = tile(nt - 1 - m, carry, True, _first_live_row(m, TQ, TK))
        carry = lax.fori_loop(NM, nt, lambda jj, c: tile(nt - 1 - jj, c, False), carry)
        for u in range(GP):
            o = _unstack_heads(carry[u][1], lane_head, hpb)
            o_ref[:, _lanes(u)] = o
            y_ref[:, _lanes(u)] = (o * _silu(g_ref[:, _lanes(u)])).astype(BF16)

    LW = GP * LANES
    blk = lambda sec: pl.BlockSpec((None, TQ, LW), lambda b, p, i: (b, i, sec * PG + p))
    full = lambda sec: pl.BlockSpec((None, S, LW), lambda b, p, i: (b, 0, sec * PG + p))
    out = pl.BlockSpec((None, TQ, LW), lambda b, p, i: (b, i, p))
    return _call_hosting(
        body, exch, "sb_fwd", (B, PG, NQ), [blk(0), full(1), full(2), blk(3)], [out, out],
        [jax.ShapeDtypeStruct((B, S, W), F32), jax.ShapeDtypeStruct((B, S, W), BF16)], [],
        (proj3, proj3, proj3, proj3))


def _sb_bwd(proj3, o, dy, W, heads, exch):
    B, S, _ = proj3.shape
    Dh = W // heads
    hpb = LANES // Dh
    P, TQ = W // LANES, _query_tile(S)
    NQ = S // TQ
    scale = 1.0 / math.sqrt(Dh)

    TK = _key_tile(S)
    RS = hpb * TQ
    NM = max(1, TQ // TK)

    def body(q_ref, k_ref, v_ref, g_ref, o_ref, dy_ref, dp_ref, dk_ref, dv_ref, u_ref, sig_ref, es_ref):
        i = pl.program_id(2)
        rows = pl.ds(pl.multiple_of(i * TQ, TQ), TQ)

        @pl.when(i == 0)
        def _():
            dk_ref[...] = jnp.zeros_like(dk_ref)
            dv_ref[...] = jnp.zeros_like(dv_ref)

        lane_head = _lane_head(Dh)
        msuf = _tri(TK, lambda r, c: r > c)
        mpre = _tri(TK, lambda r, c: r < c)
        g = g_ref[...]
        dyv = dy_ref[...].astype(F32)
        dp_ref[3, rows, :] = (dyv * o_ref[...] * _dsilu(g)).astype(dp_ref.dtype)
        qs = (_stack_heads(q_ref[...], lane_head, hpb) * scale).astype(BF16)
        dos = _stack_heads(dyv * _silu(g), lane_head, hpb).astype(BF16)
        nt = (i * TQ + TQ - 2) // TK + 1

        def weights(jt, rem_all, masked, r0=0):
            off = pl.multiple_of(jt * TK, TK)
            kj = k_ref[pl.ds(off, TK), :].astype(BF16)
            vj = v_ref[pl.ds(off, TK), :].astype(BF16)
            dos_l = _live_rows(dos, r0, hpb)
            lb, lr = _logsig_parts(_dot(_live_rows(qs, r0, hpb), kj, _NT))
            if masked:
                trow, col = _causal_iotas(RS, TK, TQ, r0)
                msk = col + (jt * TK - i * TQ) < trow
                lr = jnp.where(msk, lr, 0.0)
            w = jnp.exp(lb + _split_dot(lr, msuf) + _live_rows(rem_all, r0, hpb))
            if masked:
                w = jnp.where(msk, w, 0.0)
            e = w * _dot(dos_l, vj, _NT)
            sig = jnp.exp(lb)
            u = e * (1.0 - sig) - _split_dot(e, mpre) * sig
            if masked:
                u = jnp.where(msk, u, 0.0)
                sig = jnp.where(msk, sig, 0.0)
            n = TQ - r0
            for h in range(hpb):
                u_ref[jt, h * TQ + r0:(h + 1) * TQ, :] = u[h * n:(h + 1) * n]
                sig_ref[jt, h * TQ + r0:(h + 1) * TQ, :] = sig[h * n:(h + 1) * n]
            es_ref[jt] = _put_rows(jnp.zeros((RS, 1), F32), jnp.sum(e, axis=1, keepdims=True), r0, hpb)
            dv_ref[pl.ds(off, TK), :] += _dot(w.astype(BF16), dos_l, _TN)
            return _put_rows(rem_all, _live_rows(rem_all, r0, hpb) + jnp.sum(lr, axis=1, keepdims=True), r0, hpb)

        rem = jnp.zeros((RS, 1), F32)
        for m in range(NM):
            rem = weights(nt - 1 - m, rem, True, _first_live_row(m, TQ, TK))
        lax.fori_loop(NM, nt, lambda jj, r: weights(nt - 1 - jj, r, False), rem)

        def grads(jt, carry, r0=0):
            pre, acc = carry
            off = pl.multiple_of(jt * TK, TK)
            kj = k_ref[pl.ds(off, TK), :].astype(BF16)
            if r0 == 0:
                u, sig = u_ref[jt], sig_ref[jt]
            else:
                u = jnp.concatenate([u_ref[jt, h * TQ + r0:(h + 1) * TQ, :] for h in range(hpb)], axis=0)
                sig = jnp.concatenate([sig_ref[jt, h * TQ + r0:(h + 1) * TQ, :] for h in range(hpb)], axis=0)
            dz = (u - _live_rows(pre, r0, hpb) * sig).astype(BF16)
            dk_ref[pl.ds(off, TK), :] += _dot(dz, _live_rows(qs, r0, hpb), _TN)
            return pre + es_ref[jt], _put_rows(acc, _live_rows(acc, r0, hpb) + _dot(dz, kj), r0, hpb)

        carry = lax.fori_loop(0, nt - NM, grads, (jnp.zeros((RS, 1), F32), jnp.zeros((RS, LANES), F32)))
        for m in reversed(range(NM)):
            carry = grads(nt - 1 - m, carry, _first_live_row(m, TQ, TK))
        _, acc = carry
        dp_ref[0, rows, :] = (_unstack_heads(acc, lane_head, hpb) * scale).astype(dp_ref.dtype)

        @pl.when(i == NQ - 1)
        def _():
            dp_ref[1] = dk_ref[...].astype(dp_ref.dtype)
            dp_ref[2] = dv_ref[...].astype(dp_ref.dtype)

    blk = lambda sec: pl.BlockSpec((None, TQ, LANES), lambda b, p, i: (b, i, sec * P + p))
    full = lambda sec: pl.BlockSpec((None, S, LANES), lambda b, p, i: (b, 0, sec * P + p))
    one = pl.BlockSpec((None, TQ, LANES), lambda b, p, i: (b, i, p))
    (dproj,), moved = _call_hosting(
        body, exch, "sb_bwd", (B, P, NQ), [blk(0), full(1), full(2), blk(3), one, one],
        [pl.BlockSpec((None, 4, S, LANES), lambda b, p, i: (b, 0, 0, p))],
        [jax.ShapeDtypeStruct((B, 4, S, W), BF16)],
        [pltpu.VMEM((S, LANES), F32), pltpu.VMEM((S, LANES), F32),
         pltpu.VMEM((S // TK, RS, TK), F32), pltpu.VMEM((S // TK, RS, TK), F32), pltpu.VMEM((S // TK, RS, 1), F32)],
        (proj3, proj3, proj3, proj3, o, dy))
    return dproj, moved


def _fox_gate_fwd(f_t, b_f):
    B, H, S = f_t.shape

    def body(f_ref, b_ref, c_ref):
        row = lax.broadcasted_iota(jnp.int32, (BLK, BLK), 0)
        col = lax.broadcasted_iota(jnp.int32, (BLK, BLK), 1)
        mpre = (row <= col).astype(BF16)
        carry = jnp.zeros((H, 1), F32)
        for n in range(S // BLK):
            sl = pl.ds(n * BLK, BLK)
            lf, _ = _logsig_parts(f_ref[:, sl] + b_ref[...])
            c_ref[:, sl] = _split3_dot(lf, mpre) + carry
            carry = carry + jnp.sum(lf, axis=1, keepdims=True)

    spec = pl.BlockSpec((None, H, S), lambda b: (b, 0, 0))
    return pl.pallas_call(
        body, name="fox_gate_fwd", grid=(B,),
        in_specs=[spec, pl.BlockSpec((H, 1), lambda b: (0, 0))], out_specs=spec,
        out_shape=jax.ShapeDtypeStruct((B, H, S), F32),
        compiler_params=_params(("parallel",)),
    )(f_t, b_f)


def _fox_gate_bwd(dcum_t, f_t, b_f):
    B, H, S = f_t.shape

    def body(d_ref, f_ref, b_ref, df_ref, db_ref):
        b = pl.program_id(0)

        @pl.when(b == 0)
        def _():
            db_ref[...] = jnp.zeros_like(db_ref)

        row = lax.broadcasted_iota(jnp.int32, (BLK, BLK), 0)
        col = lax.broadcasted_iota(jnp.int32, (BLK, BLK), 1)
        msuf = (row >= col).astype(BF16)
        carry = jnp.zeros((H, 1), F32)
        dbacc = jnp.zeros((H, 1), F32)
        for n in reversed(range(S // BLK)):
            sl = pl.ds(n * BLK, BLK)
            dv = d_ref[:, sl]
            dlf = _split3_dot(dv, msuf) + carry
            carry = carry + jnp.sum(dv, axis=1, keepdims=True)
            df = dlf * _sigmoid(-(f_ref[:, sl] + b_ref[...]))
            df_ref[:, sl] = df
            dbacc = dbacc + jnp.sum(df, axis=1, keepdims=True)
        db_ref[...] += dbacc

    spec = pl.BlockSpec((None, H, S), lambda b: (b, 0, 0))
    vec = pl.BlockSpec((H, 1), lambda b: (0, 0))
    return pl.pallas_call(
        body, name="fox_gate_bwd", grid=(B,),
        in_specs=[spec, spec, vec], out_specs=[spec, vec],
        out_shape=[jax.ShapeDtypeStruct((B, H, S), F32), jax.ShapeDtypeStruct((H, 1), F32)],
        compiler_params=_params(("arbitrary",)),
    )(dcum_t, f_t, b_f)


def _pick_col(block, idx, lane_iota):
    return jnp.sum(jnp.where(lane_iota == idx, block, 0.0), axis=1, keepdims=True)


def _pick_row(block, idx, sub_iota):
    return jnp.sum(jnp.where(sub_iota == idx, block, 0.0), axis=0, keepdims=True)


def _fox_fwd(proj3, cum_t, W, heads):
    B, S, _ = proj3.shape
    H = heads
    Dh = W // heads
    hpb = LANES // Dh
    P, TQ = W // LANES, _query_tile(S)
    NQ = S // TQ
    scale = 1.0 / math.sqrt(Dh)

    TK = _key_tile(S)
    RS = hpb * TQ
    NM = max(1, TQ // TK)

    def body(q_ref, k_ref, v_ref, g_ref, ct_ref, o_ref, y_ref, lse_ref):
        p = pl.program_id(1)
        i = pl.program_id(2)
        lane_head = _lane_head(Dh)
        sub_h = lax.broadcasted_iota(jnp.int32, (H, 1), 0)
        qs = (_stack_heads(q_ref[...], lane_head, hpb) * scale).astype(BF16)
        nt = (i * TQ + TQ - 1) // TK + 1

        trow, col = _causal_iotas(RS, TK, TQ)

        def tile(jt, carry, masked):
            mx, l, acc = carry
            off = pl.multiple_of(jt * TK, TK)
            kj = k_ref[pl.ds(off, TK), :].astype(BF16)
            vj = v_ref[pl.ds(off, TK), :].astype(BF16)
            ctb = ct_ref[:, pl.ds(off, TK)]
            z = _dot(qs, kj, _NT)
            s = jnp.concatenate([z[h * TQ:(h + 1) * TQ] - _pick_row(ctb, p * hpb + h, sub_h) for h in range(hpb)],
                                axis=0)
            if masked:
                s = jnp.where(col + (jt * TK - i * TQ) <= trow, s, NEG_BIG)
            mx2 = jnp.maximum(mx, jnp.max(s, axis=1, keepdims=True))
            pe = jnp.exp(s - mx2)
            alpha = jnp.exp(mx - mx2)
            return (mx2, alpha * l + jnp.sum(pe, axis=1, keepdims=True), alpha * acc + _dot(pe.astype(BF16), vj))

        carry = lax.fori_loop(
            0, nt - NM, lambda jt, c: tile(jt, c, False),
            (jnp.full((RS, 1), NEG_BIG, F32), jnp.zeros((RS, 1), F32), jnp.zeros((RS, LANES), F32)))
        for m in reversed(range(NM)):
            carry = tile(nt - 1 - m, carry, True)
        mx, l, acc = carry
        o = _unstack_heads(acc / l, lane_head, hpb)
        o_ref[...] = o
        lse_ref[...] = _unstack_heads(jnp.broadcast_to(mx + jnp.log(l), (RS, LANES)), lane_head, hpb)
        y_ref[...] = (o * _silu(g_ref[...])).astype(BF16)

    blk = lambda sec: pl.BlockSpec((None, TQ, LANES), lambda b, p, i: (b, i, sec * P + p))
    full = lambda sec: pl.BlockSpec((None, S, LANES), lambda b, p, i: (b, 0, sec * P + p))
    out = pl.BlockSpec((None, TQ, LANES), lambda b, p, i: (b, i, p))
    return pl.pallas_call(
        body, name="fox_fwd", grid=(B, P, NQ),
        in_specs=[blk(0), full(1), full(2), blk(3),
                  pl.BlockSpec((None, H, S), lambda b, p, i: (b, 0, 0))],
        out_specs=[out, out, out],
        out_shape=[jax.ShapeDtypeStruct((B, S, W), F32), jax.ShapeDtypeStruct((B, S, W), BF16),
                   jax.ShapeDtypeStruct((B, S, W), F32)],
        compiler_params=_params(("parallel", "parallel", "arbitrary")),
    )(proj3, proj3, proj3, proj3, cum_t)


def _fox_bwd(proj3, cum_t, o, lse, dy, W, heads):
    B, S, _ = proj3.shape
    H = heads
    Dh = W // heads
    hpb = LANES // Dh
    P, TQ = W // LANES, _query_tile(S)
    NQ = S // TQ
    scale = 1.0 / math.sqrt(Dh)

    TK = _key_tile(S)
    RS = hpb * TQ
    NM = max(1, TQ // TK)

    def body(q_ref, k_ref, v_ref, g_ref, ct_ref, o_ref, lse_ref, dy_ref,
             dpj_ref, dc_ref, dk_ref, dv_ref, p_scr, dp_scr):
        p = pl.program_id(1)
        i = pl.program_id(2)
        rows = pl.ds(pl.multiple_of(i * TQ, TQ), TQ)

        @pl.when(i == 0)
        def _():
            dk_ref[...] = jnp.zeros_like(dk_ref)
            dv_ref[...] = jnp.zeros_like(dv_ref)
            dc_ref[...] = jnp.zeros_like(dc_ref)

        lane_head = _lane_head(Dh)
        sub_h = lax.broadcasted_iota(jnp.int32, (H, 1), 0)
        lane = lax.broadcasted_iota(jnp.int32, (1, LANES), 1)
        g = g_ref[...]
        lsev = lse_ref[...]
        dyv = dy_ref[...].astype(F32)
        dpj_ref[3, rows, :] = (dyv * o_ref[...] * _dsilu(g)).astype(dpj_ref.dtype)
        qs = (_stack_heads(q_ref[...], lane_head, hpb) * scale).astype(BF16)
        dos = _stack_heads(dyv * _silu(g), lane_head, hpb).astype(BF16)
        neg_lse = -jnp.concatenate([_pick_col(lsev, h * Dh, lane) for h in range(hpb)], axis=0)
        nt = (i * TQ + TQ - 1) // TK + 1

        def probs(jt, dsum, masked, r0=0):
            n = TQ - r0
            off = pl.multiple_of(jt * TK, TK)
            kj = k_ref[pl.ds(off, TK), :].astype(BF16)
            vj = v_ref[pl.ds(off, TK), :].astype(BF16)
            ctb = ct_ref[:, pl.ds(off, TK)]
            dos_l = _live_rows(dos, r0, hpb)
            z = _dot(_live_rows(qs, r0, hpb), kj, _NT) + _live_rows(neg_lse, r0, hpb)
            s = jnp.concatenate([z[h * n:(h + 1) * n] - _pick_row(ctb, p * hpb + h, sub_h) for h in range(hpb)],
                                axis=0)
            pr = jnp.exp(s)
            if masked:
                trow, col = _causal_iotas(RS, TK, TQ, r0)
                pr = jnp.where(col + (jt * TK - i * TQ) <= trow, pr, 0.0)
            dp = _dot(dos_l, vj, _NT)
            p_scr[jt] = _put_rows(jnp.zeros((RS, TK), F32), pr, r0, hpb)
            dp_scr[jt] = _put_rows(jnp.zeros((RS, TK), F32), dp, r0, hpb)
            dv_ref[pl.ds(off, TK), :] += _dot(pr.astype(BF16), dos_l, _TN)
            return _put_rows(dsum, _live_rows(dsum, r0, hpb) + jnp.sum(pr * dp, axis=1, keepdims=True), r0, hpb)

        dsum = lax.fori_loop(0, nt - NM, lambda jt, d: probs(jt, d, False), jnp.zeros((RS, 1), F32))
        for m in reversed(range(NM)):
            dsum = probs(nt - 1 - m, dsum, True)

        def grads(jt, acc, r0=0):
            n = TQ - r0
            off = pl.multiple_of(jt * TK, TK)
            kj = k_ref[pl.ds(off, TK), :].astype(BF16)
            if r0 == 0:
                pr, dp = p_scr[jt], dp_scr[jt]
            else:
                pr = jnp.concatenate([p_scr[jt, h * TQ + r0:(h + 1) * TQ, :] for h in range(hpb)], axis=0)
                dp = jnp.concatenate([dp_scr[jt, h * TQ + r0:(h + 1) * TQ, :] for h in range(hpb)], axis=0)
            ds = pr * (dp - _live_rows(dsum, r0, hpb))
            for h in range(hpb):
                dc_ref[h:h + 1, pl.ds(off, TK)] -= jnp.sum(ds[h * n:(h + 1) * n], axis=0, keepdims=True)
            dsb = ds.astype(BF16)
            dk_ref[pl.ds(off, TK), :] += _dot(dsb, _live_rows(qs, r0, hpb), _TN)
            return _put_rows(acc, _live_rows(acc, r0, hpb) + _dot(dsb, kj), r0, hpb)

        acc = lax.fori_loop(0, nt - NM, grads, jnp.zeros((RS, LANES), F32))
        for m in reversed(range(NM)):
            acc = grads(nt - 1 - m, acc, _first_live_row(m, TQ, TK))
        dpj_ref[0, rows, :] = (_unstack_heads(acc, lane_head, hpb) * scale).astype(dpj_ref.dtype)

        @pl.when(i == NQ - 1)
        def _():
            dpj_ref[1] = dk_ref[...].astype(dpj_ref.dtype)
            dpj_ref[2] = dv_ref[...].astype(dpj_ref.dtype)

    blk = lambda sec: pl.BlockSpec((None, TQ, LANES), lambda b, p, i: (b, i, sec * P + p))
    full = lambda sec: pl.BlockSpec((None, S, LANES), lambda b, p, i: (b, 0, sec * P + p))
    one = pl.BlockSpec((None, TQ, LANES), lambda b, p, i: (b, i, p))
    return pl.pallas_call(
        body, name="fox_bwd", grid=(B, P, NQ),
        in_specs=[blk(0), full(1), full(2), blk(3),
                  pl.BlockSpec((None, H, S), lambda b, p, i: (b, 0, 0)),
                  one, one, one],
        out_specs=[pl.BlockSpec((None, 4, S, LANES), lambda b, p, i: (b, 0, 0, p)),
                   pl.BlockSpec((None, None, hpb, S), lambda b, p, i: (b, p, 0, 0))],
        out_shape=[jax.ShapeDtypeStruct((B, 4, S, W), BF16), jax.ShapeDtypeStruct((B, P, hpb, S), F32)],
        scratch_shapes=[pltpu.VMEM((S, LANES), F32), pltpu.VMEM((S, LANES), F32),
                        pltpu.VMEM((S // TK, RS, TK), F32), pltpu.VMEM((S // TK, RS, TK), F32)],
        compiler_params=_params(("parallel", "parallel", "arbitrary")),
    )(proj3, proj3, proj3, proj3, cum_t, o, lse, dy)


def _layernorm_rows(v, gamma, beta):
    mu = jnp.mean(v, axis=-1, keepdims=True)
    xc = v - mu
    rstd = lax.rsqrt(jnp.mean(xc * xc, axis=-1, keepdims=True) + EPS)
    xh = xc * rstd
    return xh, rstd, xh * gamma + beta


def _layernorm_rows_bwd(dout, xh, rstd, gamma):
    dxh = dout * gamma
    return rstd * (dxh - jnp.mean(dxh, axis=-1, keepdims=True) - xh * jnp.mean(dxh * xh, axis=-1, keepdims=True))


def _gmlp_fwd(proj, wm, bs_t, ln_g, ln_b, W):
    T = proj.shape[0]
    G = wm.shape[0]
    cg = W // G
    assert cg == LANES

    def body(p_ref, wm_ref, bs_ref, lg_ref, lb_ref, y_ref, vn_ref):
        lane = lax.broadcasted_iota(jnp.int32, (1, LANES), 1)
        _, _, vn = _layernorm_rows(_gelu(p_ref[:, W:2 * W]), lg_ref[...], lb_ref[...])
        vn_ref[...] = vn.astype(BF16)
        bs = bs_ref[...]
        for g in range(G):
            sl = pl.ds(g * cg, cg)
            s = _dot(wm_ref[g], vn_ref[:, sl]) + _pick_col(bs, g, lane)
            gate = p_ref[:, pl.ds(2 * W + g * cg, cg)]
            y_ref[:, sl] = (_gelu(p_ref[:, sl]) * s * _silu(gate)).astype(BF16)

    vec = pl.BlockSpec((1, W), lambda r: (0, 0))
    return pl.pallas_call(
        body, name="gmlp_fwd", grid=(T // BLK,),
        in_specs=[pl.BlockSpec((BLK, 3 * W), lambda r: (r, 0)),
                  pl.BlockSpec((G, BLK, BLK), lambda r: (0, 0, 0)),
                  pl.BlockSpec((BLK, LANES), lambda r: (0, 0)), vec, vec],
        out_specs=pl.BlockSpec((BLK, W), lambda r: (r, 0)),
        out_shape=jax.ShapeDtypeStruct((T, W), BF16),
        scratch_shapes=[pltpu.VMEM((BLK, W), BF16)],
        compiler_params=_params(("parallel",)),
    )(proj, wm, bs_t, ln_g, ln_b)


def _gmlp_bwd(proj, dy, wm, bs_t, ln_g, ln_b, W):
    T = proj.shape[0]
    G = wm.shape[0]
    cg = W // G

    def body(p_ref, dy_ref, wm_ref, bs_ref, lg_ref, lb_ref,
             dp_ref, dwm_ref, dbs_ref, dlg_ref, dlb_ref, vn_ref, dvn_ref):
        r = pl.program_id(0)

        @pl.when(r == 0)
        def _():
            dwm_ref[...] = jnp.zeros_like(dwm_ref)
            dbs_ref[...] = jnp.zeros_like(dbs_ref)
            dlg_ref[...] = jnp.zeros_like(dlg_ref)
            dlb_ref[...] = jnp.zeros_like(dlb_ref)

        lane = lax.broadcasted_iota(jnp.int32, (1, LANES), 1)
        vpre = p_ref[:, W:2 * W]
        gamma = lg_ref[...]
        xh, rstd, vn = _layernorm_rows(_gelu(vpre), gamma, lb_ref[...])
        vn_ref[...] = vn.astype(BF16)
        bs = bs_ref[...]
        dbs = jnp.zeros((BLK, LANES), F32)
        for g in range(G):
            sl = pl.ds(g * cg, cg)
            gsl = pl.ds(2 * W + g * cg, cg)
            vng = vn_ref[:, sl]
            s = _dot(wm_ref[g], vng) + _pick_col(bs, g, lane)
            upre = p_ref[:, sl]
            u = _gelu(upre)
            gate = p_ref[:, gsl]
            dyv = dy_ref[:, sl].astype(F32)
            dp_ref[:, gsl] = (dyv * u * s * _dsilu(gate)).astype(dp_ref.dtype)
            do = dyv * _silu(gate)
            dp_ref[:, sl] = (do * s * _dgelu(upre)).astype(dp_ref.dtype)
            ds = do * u
            dbs = dbs + jnp.where(lane == g, jnp.sum(ds, axis=1, keepdims=True), 0.0)
            dsb = ds.astype(BF16)
            dwm_ref[g] += _dot(dsb, vng, _NT)
            dvn_ref[:, sl] = _dot(wm_ref[g], dsb, _TN)
        dbs_ref[...] += dbs
        dvn = dvn_ref[...]
        dlg_ref[...] += jnp.sum(dvn * xh, axis=0, keepdims=True)
        dlb_ref[...] += jnp.sum(dvn, axis=0, keepdims=True)
        dv = _layernorm_rows_bwd(dvn, xh, rstd, gamma)
        dp_ref[:, W:2 * W] = (dv * _dgelu(vpre)).astype(dp_ref.dtype)

    vec = pl.BlockSpec((1, W), lambda r: (0, 0))
    return pl.pallas_call(
        body, name="gmlp_bwd", grid=(T // BLK,),
        in_specs=[pl.BlockSpec((BLK, 3 * W), lambda r: (r, 0)),
                  pl.BlockSpec((BLK, W), lambda r: (r, 0)),
                  pl.BlockSpec((G, BLK, BLK), lambda r: (0, 0, 0)),
                  pl.BlockSpec((BLK, LANES), lambda r: (0, 0)), vec, vec],
        out_specs=[pl.BlockSpec((BLK, 3 * W), lambda r: (r, 0)),
                   pl.BlockSpec((G, BLK, BLK), lambda r: (0, 0, 0)),
                   pl.BlockSpec((BLK, LANES), lambda r: (0, 0)), vec, vec],
        out_shape=[jax.ShapeDtypeStruct((T, 3 * W), BF16), jax.ShapeDtypeStruct((G, BLK, BLK), F32),
                   jax.ShapeDtypeStruct((BLK, LANES), F32),
                   jax.ShapeDtypeStruct((1, W), F32), jax.ShapeDtypeStruct((1, W), F32)],
        scratch_shapes=[pltpu.VMEM((BLK, W), BF16), pltpu.VMEM((BLK, W), F32)],
        compiler_params=_params(("arbitrary",)),
    )(proj, dy, wm, bs_t, ln_g, ln_b)


SUBLANES = 8
SHIFT_ROWS = CONV_HALO + BLK - SUBLANES


def _shift_rows(ext_ref, sh_ref, off):
    for r in range(1, SUBLANES):
        sh_ref[r - 1] = ext_ref[pl.ds(r, SHIFT_ROWS), pl.ds(off, LANES)]


def _rows_from(ext_ref, sh_ref, off, start):
    r = start % SUBLANES
    if r == 0:
        return ext_ref[pl.ds(start, BLK), pl.ds(off, LANES)]
    return sh_ref[r - 1, pl.ds(start - r, BLK), :]


def _conv_taps(ext_ref, sh_ref, cw_ref, off, n_taps, first):
    acc = jnp.zeros((BLK, LANES), F32)
    for k in range(n_taps):
        acc = acc + cw_ref[k:k + 1, pl.ds(off, LANES)] * _rows_from(ext_ref, sh_ref, off, first + k)
    return acc


def _fill_glu_ext(ext_ref, halo_ref, cur_ref, W, first_block):
    y0h = halo_ref[:, :W] * _sigmoid(halo_ref[:, W:])
    ext_ref[0:CONV_HALO, :] = jnp.where(first_block, 0.0, y0h)
    ext_ref[CONV_HALO:CONV_HALO + BLK, :] = cur_ref[:, :W] * _sigmoid(cur_ref[:, W:])


def _conv_specs(S, W):
    per = BLK // CONV_HALO
    cur = pl.BlockSpec((None, BLK, 2 * W), lambda b, i: (b, i, 0))
    halo = pl.BlockSpec((None, CONV_HALO, 2 * W), lambda b, i: (b, jnp.maximum(i * per - 1, 0), 0))
    gate = pl.BlockSpec((None, BLK, W), lambda b, i: (b, i, 2))
    return cur, halo, gate


def _conv_fwd(proj3, cw, cb, ln_g, ln_b, W, exch):
    B, S, _ = proj3.shape
    K = cw.shape[0]
    first = CONV_HALO - (K - 1)
    assert first >= 0

    def body(cur_ref, halo_ref, g_ref, cw_ref, cb_ref, lg_ref, lb_ref, y_ref, ext_ref, y1_ref, sh_ref):
        i = pl.program_id(1)
        _fill_glu_ext(ext_ref, halo_ref, cur_ref, W, i == 0)

        def chan(c, _):
            off = pl.multiple_of(c * LANES, LANES)
            _shift_rows(ext_ref, sh_ref, off)
            y1_ref[:, pl.ds(off, LANES)] = (_conv_taps(ext_ref, sh_ref, cw_ref, off, K, first)
                                            + cb_ref[:, pl.ds(off, LANES)])
            return 0

        lax.fori_loop(0, W // LANES, chan, 0)
        _, _, ln = _layernorm_rows(y1_ref[...], lg_ref[...], lb_ref[...])
        y_ref[...] = (_silu(ln) * _silu(g_ref[...])).astype(BF16)

    cur, halo, gate = _conv_specs(S, W)
    vec = pl.BlockSpec((1, W), lambda b, i: (0, 0))
    (y,), moved = _call_hosting(
        body, exch, "conv_fwd", (B, S // BLK),
        [cur, halo, gate, pl.BlockSpec((K, W), lambda b, i: (0, 0)), vec, vec, vec],
        [pl.BlockSpec((None, BLK, W), lambda b, i: (b, i, 0))], [jax.ShapeDtypeStruct((B, S, W), BF16)],
        [pltpu.VMEM((CONV_HALO + BLK, W), F32), pltpu.VMEM((BLK, W), F32),
         pltpu.VMEM((SUBLANES - 1, SHIFT_ROWS, LANES), F32)],
        (proj3, proj3, proj3, cw, cb, ln_g, ln_b))
    return y, moved


def _conv_bwd1(proj3, dy, cw, cb, ln_g, ln_b, W, exch):
    B, S, _ = proj3.shape
    K = cw.shape[0]
    first = CONV_HALO - (K - 1)

    def body(cur_ref, halo_ref, g_ref, dy_ref, cw_ref, cb_ref, lg_ref, lb_ref,
             dy1_ref, dg_ref, dcw_ref, dcb_ref, dlg_ref, dlb_ref, ext_ref, y1_ref, sh_ref):
        b = pl.program_id(0)
        i = pl.program_id(1)

        @pl.when(jnp.logical_and(b == 0, i == 0))
        def _():
            dcw_ref[...] = jnp.zeros_like(dcw_ref)
            dcb_ref[...] = jnp.zeros_like(dcb_ref)
            dlg_ref[...] = jnp.zeros_like(dlg_ref)
            dlb_ref[...] = jnp.zeros_like(dlb_ref)

        _fill_glu_ext(ext_ref, halo_ref, cur_ref, W, i == 0)

        def chan(c, _):
            off = pl.multiple_of(c * LANES, LANES)
            _shift_rows(ext_ref, sh_ref.at[c], off)
            y1_ref[:, pl.ds(off, LANES)] = (_conv_taps(ext_ref, sh_ref.at[c], cw_ref, off, K, first)
                                            + cb_ref[:, pl.ds(off, LANES)])
            return 0

        lax.fori_loop(0, W // LANES, chan, 0)
        gamma = lg_ref[...]
        xh, rstd, ln = _layernorm_rows(y1_ref[...], gamma, lb_ref[...])
        g = g_ref[...]
        dyv = dy_ref[...].astype(F32)
        dg_ref[...] = (dyv * _silu(ln) * _dsilu(g)).astype(dg_ref.dtype)
        dln = dyv * _silu(g) * _dsilu(ln)
        dlg_ref[...] += jnp.sum(dln * xh, axis=0, keepdims=True)
        dlb_ref[...] += jnp.sum(dln, axis=0, keepdims=True)
        dy1 = _layernorm_rows_bwd(dln, xh, rstd, gamma)
        dy1_ref[...] = dy1
        dcb_ref[...] += jnp.sum(dy1, axis=0, keepdims=True)

        def chan_w(c, _):
            off = pl.multiple_of(c * LANES, LANES)
            d = dy1_ref[:, pl.ds(off, LANES)]
            for k in range(K):
                dcw_ref[k:k + 1, pl.ds(off, LANES)] += jnp.sum(
                    d * _rows_from(ext_ref, sh_ref.at[c], off, first + k), axis=0, keepdims=True)
            return 0

        lax.fori_loop(0, W // LANES, chan_w, 0)

    cur, halo, gate = _conv_specs(S, W)
    vec = pl.BlockSpec((1, W), lambda b, i: (0, 0))
    taps = pl.BlockSpec((K, W), lambda b, i: (0, 0))
    one = pl.BlockSpec((None, BLK, W), lambda b, i: (b, i, 0))
    return _call_hosting(
        body, exch, "conv_bwd1", (B, S // BLK), [cur, halo, gate, one, taps, vec, vec, vec],
        [one, one, taps, vec, vec, vec],
        [jax.ShapeDtypeStruct((B, S, W), F32), jax.ShapeDtypeStruct((B, S, W), BF16),
         jax.ShapeDtypeStruct((K, W), F32)] + [jax.ShapeDtypeStruct((1, W), F32)] * 3,
        [pltpu.VMEM((CONV_HALO + BLK, W), F32), pltpu.VMEM((BLK, W), F32),
         pltpu.VMEM((W // LANES, SUBLANES - 1, SHIFT_ROWS, LANES), F32)],
        (proj3, proj3, proj3, dy, cw, cb, ln_g, ln_b))


def _conv_bwd2(proj3, dy1, dgate, cw_rev, W):
    B, S, _ = proj3.shape
    K = cw_rev.shape[0]
    NQ = S // BLK
    per = BLK // CONV_HALO

    def body(cur_ref, d_ref, dnext_ref, dgate_ref, cw_ref, dp_ref, ext_ref, dy0_ref, sh_ref):
        i = pl.program_id(1)
        ext_ref[0:BLK, :] = d_ref[...]
        ext_ref[BLK:BLK + CONV_HALO, :] = jnp.where(i == NQ - 1, 0.0, dnext_ref[...])

        def chan(c, _):
            off = pl.multiple_of(c * LANES, LANES)
            _shift_rows(ext_ref, sh_ref, off)
            dy0_ref[:, pl.ds(off, LANES)] = _conv_taps(ext_ref, sh_ref, cw_ref, off, K, 0)
            return 0

        lax.fori_loop(0, W // LANES, chan, 0)
        a = cur_ref[:, :W]
        sg = _sigmoid(cur_ref[:, W:])
        dy0 = dy0_ref[...]
        dp_ref[:, 0:W] = (dy0 * sg).astype(dp_ref.dtype)
        dp_ref[:, W:2 * W] = (dy0 * a * sg * (1.0 - sg)).astype(dp_ref.dtype)
        dp_ref[:, 2 * W:3 * W] = dgate_ref[...]

    cur = pl.BlockSpec((None, BLK, 2 * W), lambda b, i: (b, i, 0))
    one = pl.BlockSpec((None, BLK, W), lambda b, i: (b, i, 0))
    nxt = pl.BlockSpec((None, CONV_HALO, W), lambda b, i: (b, jnp.minimum((i + 1) * per, S // CONV_HALO - 1), 0))
    return pl.pallas_call(
        body, name="conv_bwd2", grid=(B, NQ),
        in_specs=[cur, one, nxt, one, pl.BlockSpec((K, W), lambda b, i: (0, 0))],
        out_specs=pl.BlockSpec((None, BLK, 3 * W), lambda b, i: (b, i, 0)),
        out_shape=jax.ShapeDtypeStruct((B, S, 3 * W), BF16),
        scratch_shapes=[pltpu.VMEM((BLK + CONV_HALO, W), F32), pltpu.VMEM((BLK, W), F32),
                        pltpu.VMEM((SUBLANES - 1, SHIFT_ROWS, LANES), F32)],
        compiler_params=_params(("parallel", "parallel")),
    )(proj3, dy1, dy1, dgate, cw_rev)


def _pack(arrays):
    flat = jnp.concatenate([a.astype(F32).reshape(-1) for a in arrays])
    n = flat.shape[0]
    pad = (-n) % (8 * LANES)
    if pad:
        flat = jnp.concatenate([flat, jnp.zeros((pad,), F32)])
    return flat.reshape(-1, LANES)


def _unpack(packed, shapes, lead=()):
    flat = packed.reshape(lead + (-1,))
    out, off = [], 0
    for shp in shapes:
        n = math.prod(shp)
        out.append(flat[..., off:off + n].reshape(lead + tuple(shp)))
        off += n
    return out


def _cols_from_dev(g):
    g = jnp.moveaxis(g, 0, -2)
    return g.reshape(g.shape[:-2] + (g.shape[-2] * g.shape[-1],))


def _my_cols(full, me):
    n8 = full.shape[-1] // N_DEV
    return lax.dynamic_slice_in_dim(full, me * n8, n8, axis=full.ndim - 1)


def kernel(x, a_norm, a_w_in, a_w_out, b_norm, b_w_in, b_v_ln_g, b_v_ln_b, b_w_s, b_b_s, b_w_out, c_norm, c_w_in, c_conv_w, c_conv_b, c_ln_g, c_ln_b, c_w_out, d_norm, d_w_in, d_b_f, d_w_out, final_norm, loss_target, m_a_norm, m_a_w_in, m_a_w_out, m_b_norm, m_b_w_in, m_b_v_ln_g, m_b_v_ln_b, m_b_w_s, m_b_b_s, m_b_w_out, m_c_norm, m_c_w_in, m_c_conv_w, m_c_conv_b, m_c_ln_g, m_c_ln_b, m_c_w_out, m_d_norm, m_d_w_in, m_d_b_f, m_d_w_out, m_final_norm, v_a_norm, v_a_w_in, v_a_w_out, v_b_norm, v_b_w_in, v_b_v_ln_g, v_b_v_ln_b, v_b_w_s, v_b_b_s, v_b_w_out, v_c_norm, v_c_w_in, v_c_conv_w, v_c_conv_b, v_c_ln_g, v_c_ln_b, v_c_w_out, v_d_norm, v_d_w_in, v_d_b_f, v_d_w_out, v_final_norm):
    B, S, D = x.shape
    T = B * S
    xi, yi, ci = _me()
    me = 4 * xi + 2 * yi + ci

    G = b_w_s.shape[1]
    KC = c_conv_w.shape[1]
    H_D = d_b_f.shape[1]
    W_A = a_w_out.shape[1] * N_DEV
    W_B = b_w_out.shape[1] * N_DEV
    W_C = c_w_out.shape[1] * N_DEV
    W_D = d_w_out.shape[1] * N_DEV
    N_D = d_w_in.shape[2] * N_DEV
    N_D_PAD = -(-N_D // (3 * LANES)) * (3 * LANES)

    big_names = ["a_w_in", "a_w_out", "b_w_in", "b_w_out", "c_w_in", "c_w_out", "d_w_in", "d_w_out"]
    big_w = dict(a_w_in=a_w_in[0], a_w_out=a_w_out[0], b_w_in=b_w_in[0], b_w_out=b_w_out[0],
                 c_w_in=c_w_in[0], c_w_out=c_w_out[0], d_w_in=d_w_in[0], d_w_out=d_w_out[0])
    small_sharded = [b_norm, b_v_ln_g, b_v_ln_b, c_norm, c_conv_w, c_conv_b, c_ln_g, c_ln_b, d_norm]
    first_names, later_names, last_names = big_names[:1], big_names[1:6], big_names[6:]
    gathered = _GatherViaSibling(
        [big_w[n].astype(BF16) for n in first_names] + [_pack(small_sharded)]).run("gather_first")
    wg = dict(zip(first_names, gathered[:-1]))
    (b_norm_f, b_lg_f, b_lb_f, c_norm_f, c_cw_f, c_cb_f, c_lg_f, c_lb_f, d_norm_f) = [
        _cols_from_dev(t) for t in _unpack(gathered[-1], [s.shape for s in small_sharded], lead=(N_DEV,))]
    c_cw_f = c_cw_f[0]

    wm = jnp.tril(b_w_s[0]).astype(BF16)
    bs_t = jnp.pad(b_b_s[0].T, ((0, 0), (0, LANES - G)))

    x0 = x.reshape(T, D)
    h_a = _rmsnorm_fwd(x0, a_norm, "rms_a")
    proj_a = _mm_w_dev(h_a, wg["a_w_in"], "proj_a").reshape(B, S, 4 * W_A)
    (o_a, y_a), later = _sb_fwd(proj_a, W_A, SB_HEADS,
                                _Exchange([big_w[n].astype(BF16) for n in later_names], ["gather"] * len(later_names)))
    wg.update(zip(later_names, later))
    a_w_out_f = wg["a_w_out"].reshape(W_A, D)
    b_w_out_f = wg["b_w_out"].reshape(W_B, D)
    c_w_out_f = wg["c_w_out"].reshape(W_C, D)
    y_a = y_a.reshape(T, W_A)
    x1, h_b = _mm(y_a, a_w_out_f, "nn", T, D, W_A, F32, "out_a", 512, D, W_A, res=x0, norm_gain=b_norm_f)
    proj_b = _mm_w_dev(h_b, wg["b_w_in"], "proj_b")
    y_b = _gmlp_fwd(proj_b, wm, bs_t, b_lg_f, b_lb_f, W_B)
    x2, h_c = _mm(y_b, b_w_out_f, "nn", T, D, W_B, F32, "out_b", 512, D, W_B, res=x1, norm_gain=c_norm_f)
    proj_c = _mm_w_dev(h_c, wg["c_w_in"], "proj_c").reshape(B, S, 3 * W_C)
    y_c, last = _conv_fwd(proj_c, c_cw_f, c_cb_f, c_lg_f, c_lb_f, W_C,
                          _Exchange([big_w[n].astype(BF16) for n in last_names], ["gather"] * len(last_names)))
    wg.update(zip(last_names, last))
    d_w_out_f = wg["d_w_out"].reshape(W_D, D)
    d_w_in_f = jnp.pad(_cols_from_dev(wg["d_w_in"]), ((0, 0), (0, N_D_PAD - N_D)))
    y_c = y_c.reshape(T, W_C)
    x3, h_d = _mm(y_c, c_w_out_f, "nn", T, D, W_C, F32, "out_c", 512, D, W_C, res=x2, norm_gain=d_norm_f)
    proj_d = _mm(h_d, d_w_in_f, "nn", T, N_D_PAD, D, F32, "proj_d", 1024, 384, D).reshape(B, S, N_D_PAD)
    f_t = jnp.swapaxes(proj_d[:, :, 4 * W_D:4 * W_D + H_D], 1, 2)
    b_f_col = d_b_f.reshape(H_D, 1)
    cum_t = _fox_gate_fwd(f_t, b_f_col)
    o_d, y_d, lse_d = _fox_fwd(proj_d, cum_t, W_D, H_D)
    y_d = y_d.reshape(T, W_D)
    x4 = _mm(y_d, d_w_out_f, "nn", T, D, W_D, F32, "out_d", 512, D, W_D, res=x3)

    loss_part, dx, g_final = _loss_head(x4, final_norm.reshape(1, D), loss_target.reshape(T, D))
    loss = lax.psum(loss_part[0, 0], MESH_AXES)

    dy_d = _mm(dx, d_w_out_f, "nt", T, W_D, D, BF16, "dy_d", 512, W_D, D).reshape(B, S, W_D)
    gw_d_out = _mm(y_d, dx, "tn", W_D, D, T, BF16, "gw_d_out", W_D, D, 512).reshape(N_DEV, W_D // N_DEV, D)
    dproj_d, dcum = _fox_bwd(proj_d, cum_t, o_d, lse_d, dy_d, W_D, H_D)
    df_t, g_b_f = _fox_gate_bwd(dcum.reshape(B, H_D, S), f_t, b_f_col)
    F_PAD = N_D_PAD - 4 * W_D
    df = jnp.pad(jnp.swapaxes(df_t, 1, 2), ((0, 0), (0, 0), (0, F_PAD - H_D))).reshape(T, F_PAD)
    tc, tr = min(512, W_D), min(1024, S)
    gw_main = _mm(h_d, dproj_d, "tn", D, 4 * W_D, T, BF16, "gw_d_in", D, tc, tr,
                  b_spec=_sectioned_spec(dproj_d, tr, tc, 2, 1))
    gw_f = _mm(h_d, df, "tn", D, F_PAD, T, BF16, "gw_d_in_f", D, F_PAD, 512)
    gw_d_in = jnp.moveaxis(
        jnp.concatenate([gw_main, gw_f], axis=1)[:, :N_D].reshape(D, N_DEV, N_D // N_DEV), 1, 0)
    dh_f = _mm(df, d_w_in_f[:, 4 * W_D:], "nt", T, D, F_PAD, F32, "dh_d_f", 512, D, F_PAD)
    dx, g_d_norm, _ = _mm(dproj_d, d_w_in_f, "nt", T, D, 4 * W_D, F32, "dh_d", tr, D, tc,
                          a_spec=_sectioned_spec(dproj_d, tr, tc, 0, 2), res=dh_f, norm_bwd=(x3, d_norm_f, dx))

    dy_c = _mm(dx, c_w_out_f, "nt", T, W_C, D, BF16, "dy_c", 512, W_C, D).reshape(B, S, W_C)
    gw_c_out = _mm(y_c, dx, "tn", W_C, D, T, BF16, "gw_c_out", 1024, D, 512).reshape(N_DEV, W_C // N_DEV, D)
    (dy1, dgate_c, g_c_cw, g_c_cb, g_c_lg, g_c_lb), parts_d = _conv_bwd1(
        proj_c, dy_c, c_cw_f, c_cb_f, c_lg_f, c_lb_f, W_C, _Exchange([gw_d_in, gw_d_out], ["scatter"] * 2))
    dproj_c = _conv_bwd2(proj_c, dy1, dgate_c, c_cw_f[::-1], W_C).reshape(T, 3 * W_C)
    gw_c_in = _mm_grad_dev(h_c, dproj_c, "gw_c_in")
    dx, g_c_norm, _ = _mm_wT_dev(dproj_c, wg["c_w_in"], "dh_c", norm_bwd=(x2, c_norm_f, dx))

    dy_b = _mm(dx, b_w_out_f, "nt", T, W_B, D, BF16, "dy_b", 512, W_B, D)
    gw_b_out = _mm(y_b, dx, "tn", W_B, D, T, BF16, "gw_b_out", 1024, D, 512).reshape(N_DEV, W_B // N_DEV, D)
    dproj_b, g_wm, g_bs_t, g_b_lg, g_b_lb = _gmlp_bwd(proj_b, dy_b, wm, bs_t, b_lg_f, b_lb_f, W_B)
    g_b_w_s = jnp.tril(g_wm)
    g_b_b_s = g_bs_t[:, :G].T
    gw_b_in = _mm_grad_dev(h_b, dproj_b, "gw_b_in")
    dx, g_b_norm, _ = _mm_wT_dev(dproj_b, wg["b_w_in"], "dh_b", norm_bwd=(x1, b_norm_f, dx))

    dy_a = _mm(dx, a_w_out_f, "nt", T, W_A, D, BF16, "dy_a", 512, W_A, D).reshape(B, S, W_A)
    gw_a_out = _mm(y_a, dx, "tn", W_A, D, T, BF16, "gw_a_out", W_A, D, 512).reshape(N_DEV, W_A // N_DEV, D)
    small_full = [g_b_norm, g_b_lg, g_b_lb, g_b_b_s, g_c_norm, g_c_cw, g_c_cb, g_c_lg, g_c_lb,
                  g_d_norm, g_b_f, g_final]
    dproj_a, parts_s = _sb_bwd(
        proj_a, o_a, dy_a, W_A, SB_HEADS,
        _Exchange([gw_c_in, gw_c_out, gw_b_in, gw_b_out, gw_a_out, _pack(small_full), g_b_w_s.reshape(-1, LANES)],
                  ["scatter"] * 5 + ["gather"] * 2))
    gw_a_in = _mm_grad_dev(h_a, dproj_a, "gw_a_in")
    dx, g_a_norm, parts_a = _mm_wT_dev(dproj_a, wg["a_w_in"], "dh_a", exch=_Exchange([gw_a_in], ["scatter"]),
                                       norm_bwd=(x0, a_norm, dx))
    grad_x = dx.reshape(B, S, D)

    (parts_n,) = _Exchange([_pack([g_a_norm])], ["gather"]).run("exchange_last")
    big_parts = dict(a_w_in=parts_a[0], a_w_out=parts_s[4], b_w_in=parts_s[2], b_w_out=parts_s[3],
                     c_w_in=parts_s[0], c_w_out=parts_s[1], d_w_in=parts_d[0], d_w_out=parts_d[1])
    (s_b_norm, s_b_lg, s_b_lb, s_b_b_s, s_c_norm, s_c_cw, s_c_cb, s_c_lg, s_c_lb,
     s_d_norm, s_b_f, s_final) = _unpack(_sum_parts(parts_s[5], "sum_small"), [g.shape for g in small_full])
    (s_a_norm,) = _unpack(_sum_parts(parts_n, "sum_a_norm"), [g_a_norm.shape])

    weights = dict(a_norm=a_norm, a_w_in=a_w_in, a_w_out=a_w_out, b_norm=b_norm, b_w_in=b_w_in, b_v_ln_g=b_v_ln_g,
                   b_v_ln_b=b_v_ln_b, b_w_s=b_w_s, b_b_s=b_b_s, b_w_out=b_w_out, c_norm=c_norm, c_w_in=c_w_in,
                   c_conv_w=c_conv_w, c_conv_b=c_conv_b, c_ln_g=c_ln_g, c_ln_b=c_ln_b, c_w_out=c_w_out,
                   d_norm=d_norm, d_w_in=d_w_in, d_b_f=d_b_f, d_w_out=d_w_out, final_norm=final_norm)
    mom_m = dict(a_norm=m_a_norm, a_w_in=m_a_w_in, a_w_out=m_a_w_out, b_norm=m_b_norm, b_w_in=m_b_w_in,
                 b_v_ln_g=m_b_v_ln_g, b_v_ln_b=m_b_v_ln_b, b_w_s=m_b_w_s, b_b_s=m_b_b_s, b_w_out=m_b_w_out,
                 c_norm=m_c_norm, c_w_in=m_c_w_in, c_conv_w=m_c_conv_w, c_conv_b=m_c_conv_b, c_ln_g=m_c_ln_g,
                 c_ln_b=m_c_ln_b, c_w_out=m_c_w_out, d_norm=m_d_norm, d_w_in=m_d_w_in, d_b_f=m_d_b_f,
                 d_w_out=m_d_w_out, final_norm=m_final_norm)
    mom_v = dict(a_norm=v_a_norm, a_w_in=v_a_w_in, a_w_out=v_a_w_out, b_norm=v_b_norm, b_w_in=v_b_w_in,
                 b_v_ln_g=v_b_v_ln_g, b_v_ln_b=v_b_v_ln_b, b_w_s=v_b_w_s, b_b_s=v_b_b_s, b_w_out=v_b_w_out,
                 c_norm=v_c_norm, c_w_in=v_c_w_in, c_conv_w=v_c_conv_w, c_conv_b=v_c_conv_b, c_ln_g=v_c_ln_g,
                 c_ln_b=v_c_ln_b, c_w_out=v_c_w_out, d_norm=v_d_norm, d_w_in=v_d_w_in, d_b_f=v_d_b_f,
                 d_w_out=v_d_w_out, final_norm=v_final_norm)
    order = list(weights)
    grads, deltas, new_m, new_v = {}, {}, {}, {}

    for n in big_names:
        part = big_parts[n]
        shp = weights[n].shape
        R, C = shp[1], shp[2]
        res = _adamw(part, weights[n].reshape(R, C), mom_m[n].reshape(R, C), mom_v[n].reshape(R, C), "adamw_" + n)
        grads[n], deltas[n], new_m[n], new_v[n] = [r.reshape(shp) for r in res]

    res = _adamw(parts_s[6], b_w_s.reshape(-1, LANES), m_b_w_s.reshape(-1, LANES), v_b_w_s.reshape(-1, LANES),
                 "adamw_b_w_s")
    grads["b_w_s"], deltas["b_w_s"], new_m["b_w_s"], new_v["b_w_s"] = [r.reshape(b_w_s.shape) for r in res]

    small_g = dict(
        a_norm=s_a_norm, b_norm=_my_cols(s_b_norm, me), b_v_ln_g=_my_cols(s_b_lg, me),
        b_v_ln_b=_my_cols(s_b_lb, me), b_b_s=s_b_b_s[None], c_norm=_my_cols(s_c_norm, me),
        c_conv_w=_my_cols(s_c_cw, me)[None], c_conv_b=_my_cols(s_c_cb, me), c_ln_g=_my_cols(s_c_lg, me),
        c_ln_b=_my_cols(s_c_lb, me), d_norm=_my_cols(s_d_norm, me), d_b_f=s_b_f.reshape(1, H_D),
        final_norm=s_final.reshape(D))
    small_names = list(small_g)
    sg_p = _pack([small_g[n] for n in small_names])
    res = _adamw(sg_p[None], _pack([weights[n] for n in small_names]), _pack([mom_m[n] for n in small_names]),
                 _pack([mom_v[n] for n in small_names]), "adamw_small")
    shapes = [weights[n].shape for n in small_names]
    for dst, r in zip((grads, deltas, new_m, new_v), res):
        for n, val in zip(small_names, _unpack(r, shapes)):
            dst[n] = val

    return (loss, grad_x, *[grads[n] for n in order], *[deltas[n] for n in order],
            *[new_m[n] for n in order], *[new_v[n] for n in order])
```

```python
import functools
import math

import jax
import jax.numpy as jnp
from jax import lax
from jax.experimental import pallas as pl
from jax.experimental.pallas import tpu as pltpu

F32 = jnp.float32
BF16 = jnp.bfloat16

EPS = 1e-6
SB_HEADS = 16
CONV_HALO = 32
BLK = 128
ATT_TK = 256
ATT_TQ = 512
ATT_GP = 2
LANES = 128
N_DEV = 8
MESH_AXES = ("x", "y", "c")

ADAM_LR = 0.001
ADAM_B1 = 0.9
ADAM_B2 = 0.999
ADAM_EPS = 1e-08
ADAM_WD = 0.01
ADAM_STEP = 10

VMEM_LIMIT = 56 * 1024 * 1024
NEG_BIG = -1e30

_NN = (((1,), (0,)), ((), ()))
_NT = (((1,), (1,)), ((), ()))
_TN = (((0,), (0,)), ((), ()))


def _dot(a, b, dims=_NN):
    return lax.dot_general(a, b, dims, preferred_element_type=F32)


def _split_dot(x, m):
    hi = x.astype(BF16)
    lo = (x - hi.astype(F32)).astype(BF16)
    return _dot(hi, m) + _dot(lo, m)


def _split3_dot(x, m):
    hi = x.astype(BF16)
    r1 = x - hi.astype(F32)
    mid = r1.astype(BF16)
    lo = (r1 - mid.astype(F32)).astype(BF16)
    return _dot(hi, m) + _dot(mid, m) + _dot(lo, m)


def _params(sem=None):
    kw = dict(vmem_limit_bytes=VMEM_LIMIT)
    if sem is not None:
        kw["dimension_semantics"] = sem
    return pltpu.CompilerParams(**kw)


def _sigmoid(x):
    return jax.nn.sigmoid(x)


def _silu(x):
    return x * _sigmoid(x)


def _dsilu(x):
    s = _sigmoid(x)
    return s * (1.0 + x * (1.0 - s))


_GELU_C = math.sqrt(2.0 / math.pi)


def _gelu(x):
    return 0.5 * x * (1.0 + jnp.tanh(_GELU_C * (x + 0.044715 * x * x * x)))


def _dgelu(x):
    th = jnp.tanh(_GELU_C * (x + 0.044715 * x * x * x))
    return 0.5 * (1.0 + th) + 0.5 * x * (1.0 - th * th) * _GELU_C * (1.0 + 3.0 * 0.044715 * x * x)


def _mm(a, b, mode, M, N, K, out_dtype, name, tm, tn, tk, a_spec=None, b_spec=None, o_spec=None, out_shape=None,
        exch=None, res=None, norm_gain=None, norm_bwd=None):
    tm, tn, tk = min(tm, M), min(tn, N), min(tk, K)
    assert M % tm == 0 and N % tn == 0 and K % tk == 0, (name, M, N, K, tm, tn, tk)
    nk = K // tk
    assert norm_gain is None or (nk == 1 and tn == N and exch is None)
    dims = {"nn": _NN, "nt": _NT, "tn": _TN}[mode]
    if a_spec is None:
        a_spec = (pl.BlockSpec((tk, tm), lambda i, j, k: (k, i)) if mode == "tn"
                  else pl.BlockSpec((tm, tk), lambda i, j, k: (i, k)))
    if b_spec is None:
        b_spec = (pl.BlockSpec((tn, tk), lambda i, j, k: (j, k)) if mode == "nt"
                  else pl.BlockSpec((tk, tn), lambda i, j, k: (k, j)))
    if o_spec is None:
        o_spec = pl.BlockSpec((tm, tn), lambda i, j, k: (i, j))
    if out_shape is None:
        out_shape = (M, N)

    def body(a_ref, b_ref, *rest):
        res_ref = rest[0] if res is not None else None
        if nk == 1:
            d = _dot(a_ref[...].astype(BF16), b_ref[...].astype(BF16), dims)
            r = d if res_ref is None else res_ref[...].astype(F32) + d
            if norm_gain is None:
                rest[-1][...] = r.astype(rest[-1].dtype)
            else:
                g_ref, o_ref, h_ref = rest[-3:]
                o_ref[...] = r.astype(o_ref.dtype)
                scale = lax.rsqrt(jnp.mean(r * r, axis=-1, keepdims=True) + EPS)
                h_ref[...] = (r * scale * g_ref[...]).astype(BF16)
            return
        i = pl.program_id(0)
        k = pl.program_id(2)
        if norm_bwd is not None:
            x_ref, g_ref, dres_ref, o_ref, dg_ref, acc_ref = rest[-6:]

            @pl.when(jnp.logical_and(i == 0, k == 0))
            def _():
                dg_ref[...] = jnp.zeros_like(dg_ref)
        else:
            o_ref, acc_ref = rest[-2:]

        @pl.when(k == 0)
        def _():
            acc_ref[...] = jnp.zeros_like(acc_ref) if res_ref is None else res_ref[...].astype(F32)

        acc_ref[...] += _dot(a_ref[...].astype(BF16), b_ref[...].astype(BF16), dims)

        @pl.when(k == nk - 1)
        def _():
            if norm_bwd is None:
                o_ref[...] = acc_ref[...].astype(o_ref.dtype)
            else:
                dh = acc_ref[...]
                xv = x_ref[...]
                r = lax.rsqrt(jnp.mean(xv * xv, axis=-1, keepdims=True) + EPS)
                xh = xv * r
                dxh = dh * g_ref[...]
                o_ref[...] = dres_ref[...] + r * (dxh - xh * jnp.mean(dxh * xh, axis=-1, keepdims=True))
                dg_ref[...] += jnp.sum(dh * xh, axis=0, keepdims=True)

    in_specs, args = [a_spec, b_spec], (a, b)
    if res is not None:
        in_specs, args = in_specs + [o_spec], args + (res,)
    scratch = [pltpu.VMEM((tm, tn), F32)] if nk > 1 else []
    if norm_bwd is not None:
        assert nk > 1 and tn == N and norm_gain is None
        vec = pl.BlockSpec((1, N), lambda i, j, k: (0, 0))
        x_in, g_in, dres_in = norm_bwd
        (dx, dg), moved = _call_hosting(
            body, exch, name, (M // tm, 1, nk), in_specs + [o_spec, vec, o_spec], [o_spec, vec],
            [jax.ShapeDtypeStruct((M, N), F32), jax.ShapeDtypeStruct((1, N), F32)], scratch,
            args + (x_in, g_in, dres_in))
        return dx, dg, moved
    if norm_gain is not None:
        return pl.pallas_call(
            body, name=name, grid=(M // tm, N // tn, nk),
            in_specs=in_specs + [pl.BlockSpec((1, N), lambda i, j, k: (0, 0))], out_specs=[o_spec, o_spec],
            out_shape=[jax.ShapeDtypeStruct(out_shape, out_dtype), jax.ShapeDtypeStruct(out_shape, BF16)],
            compiler_params=_params(("parallel", "parallel", "arbitrary")),
        )(*args, norm_gain)
    if exch is None:
        return pl.pallas_call(
            body, name=name, grid=(M // tm, N // tn, nk),
            in_specs=in_specs, out_specs=o_spec,
            out_shape=jax.ShapeDtypeStruct(out_shape, out_dtype),
            scratch_shapes=scratch,
            compiler_params=_params(("parallel", "parallel", "arbitrary")),
        )(*args)
    (out,), moved = _call_hosting(
        body, exch, name, (M // tm, N // tn, nk), in_specs, [o_spec],
        [jax.ShapeDtypeStruct(out_shape, out_dtype)], scratch, args)
    return out, moved


def _mm_w_dev(a, w3, name, out_dtype=F32, tm=1024):
    M, K = a.shape
    n8 = w3.shape[2]
    tn = n8 if n8 <= 768 else 512
    per = n8 // tn
    b_spec = pl.BlockSpec((None, K, tn), lambda i, j, k: (j // per, 0, j % per))
    return _mm(a, w3, "nn", M, N_DEV * n8, K, out_dtype, name, tm, tn, K, b_spec=b_spec)


def _sectioned_spec(d4, t_rows, t_cols, rows_axis, cols_axis):
    _, _, S, W = d4.shape
    assert S % t_rows == 0 and W % t_cols == 0
    rb, cb = S // t_rows, W // t_cols

    def index(*g):
        r, c = g[rows_axis], g[cols_axis]
        return (r // rb, c // cb, r % rb, c % cb)

    return pl.BlockSpec((None, None, t_rows, t_cols), index)


def _mm_wT_dev(a, w3, name, out_dtype=F32, tm=1024, exch=None, norm_bwd=None):
    K, n8 = w3.shape[1], w3.shape[2]
    tk = n8 if n8 <= 768 else 512
    per = n8 // tk
    b_spec = pl.BlockSpec((None, K, tk), lambda i, j, k: (k // per, 0, k % per))
    if a.ndim == 4:
        M, N = a.shape[0] * a.shape[2], a.shape[1] * a.shape[3]
        tm = min(tm, a.shape[2])
        a_spec = _sectioned_spec(a, tm, tk, 0, 2)
    else:
        (M, N), a_spec = a.shape, None
    return _mm(a, w3, "nt", M, K, N, out_dtype, name, tm, K, tk, a_spec=a_spec, b_spec=b_spec, exch=exch,
               norm_bwd=norm_bwd)


def _mm_grad_dev(h, d, name, out_dtype=BF16):
    T, M = h.shape
    N = d.shape[1] * d.shape[3] if d.ndim == 4 else d.shape[1]
    n8 = N // N_DEV
    tn = n8 if n8 <= 768 else 512
    per = n8 // tn
    tm = min(M, 1024)
    o_spec = pl.BlockSpec((None, tm, tn), lambda i, j, k: (j // per, i, j % per))
    tk = min(1024, d.shape[2] if d.ndim == 4 else T)
    b_spec = _sectioned_spec(d, tk, tn, 2, 1) if d.ndim == 4 else None
    return _mm(h, d, "tn", M, N, T, out_dtype, name, tm, tn, tk, b_spec=b_spec, o_spec=o_spec,
               out_shape=(N_DEV, M, n8))


def _me():
    x, y, c = lax.axis_index("x"), lax.axis_index("y"), lax.axis_index("c")
    return x, y, c


def _peer(r):
    x, y, c = _me()
    px = 1 - x if (r >> 2) & 1 else x
    py = 1 - y if (r >> 1) & 1 else y
    pc = 1 - c if r & 1 else c
    return (px, py, pc), 4 * px + 2 * py + pc


class _Exchange:
    def __init__(self, arrays, kinds):
        self.arrays, self.kinds, self.n = list(arrays), list(kinds), len(arrays)
        self.out_shapes = [
            jax.ShapeDtypeStruct((N_DEV,) + a.shape if kind == "gather" else a.shape, a.dtype)
            for a, kind in zip(arrays, kinds)]
        self.specs = [pl.BlockSpec(memory_space=pl.ANY)] * self.n
        self.sems = [pltpu.SemaphoreType.DMA((self.n, N_DEV - 1)), pltpu.SemaphoreType.DMA((self.n, N_DEV - 1)),
                     pltpu.SemaphoreType.DMA((self.n,))]

    def _copies(self, ins, outs, sems, receiving):
        send_sems, recv_sems, local_sems = sems
        x, y, c = _me()
        me = 4 * x + 2 * y + c

        def src(k, pid):
            return ins[k] if self.kinds[k] == "gather" else ins[k].at[pid]

        local = [pltpu.make_async_copy(src(k, me), outs[k].at[me], local_sems.at[k]) for k in range(self.n)]
        remote = []
        for r in range(1, N_DEV):
            peer, pid = _peer(r)
            for k in range(self.n):
                remote.append(pltpu.make_async_remote_copy(
                    src_ref=src(k, pid), dst_ref=outs[k].at[pid if receiving else me],
                    send_sem=send_sems.at[k, r - 1], recv_sem=recv_sems.at[k, r - 1],
                    device_id=peer, device_id_type=pl.DeviceIdType.MESH))
        return local, remote

    def start(self, ins, outs, sems):
        local, remote = self._copies(ins, outs, sems, False)
        for cp in local + remote:
            cp.start()

    def wait(self, ins, outs, sems):
        local, remote = self._copies(ins, outs, sems, True)
        for cp in remote:
            cp.wait_recv()
        for cp in remote:
            cp.wait_send()
        for cp in local:
            cp.wait()

    def run(self, name):
        n = self.n

        def body(*refs):
            ins, outs, sems = refs[:n], refs[n:2 * n], refs[2 * n:]
            self.start(ins, outs, sems)
            self.wait(ins, outs, sems)

        return pl.pallas_call(
            body, name=name, in_specs=self.specs, out_specs=self.specs, out_shape=self.out_shapes,
            scratch_shapes=self.sems,
        )(*self.arrays)


class _GatherViaSibling(_Exchange):
    ICI = (2, 4, 6)

    def __init__(self, arrays):
        super().__init__(arrays, ["gather"] * len(arrays))

    def _copy(self, ins, outs, sems, k, column, block, to, from_input=False):
        return pltpu.make_async_remote_copy(
            src_ref=ins[k] if from_input else outs[k].at[block], dst_ref=outs[k].at[block],
            send_sem=sems[0].at[k, column], recv_sem=sems[1].at[k, column],
            device_id=to, device_id_type=pl.DeviceIdType.MESH)

    def start(self, ins, outs, sems):
        x, y, c = _me()
        me = 4 * x + 2 * y + c
        for k in range(self.n):
            pltpu.make_async_copy(ins[k], outs[k].at[me], sems[2].at[k]).start()
            self._copy(ins, outs, sems, k, 0, me, _peer(1)[0], True).start()
            for j, r in enumerate(self.ICI):
                self._copy(ins, outs, sems, k, 1 + j, me, _peer(r)[0], True).start()

    def wait(self, ins, outs, sems):
        x, y, c = _me()
        me = 4 * x + 2 * y + c
        sibling, sibling_id = _peer(1)
        for j, r in enumerate(self.ICI):
            peer, pid = _peer(r)
            for k in range(self.n):
                self._copy(ins, outs, sems, k, 1 + j, pid, peer).wait_recv()
                self._copy(ins, outs, sems, k, 4 + j, pid, sibling).start()
        for k in range(self.n):
            self._copy(ins, outs, sems, k, 0, sibling_id, sibling).wait_recv()
            for j, r in enumerate(self.ICI):
                self._copy(ins, outs, sems, k, 4 + j, _peer(r ^ 1)[1], sibling).wait_recv()
            for column in range(N_DEV - 1):
                self._copy(ins, outs, sems, k, column, me, sibling).wait_send()
            pltpu.make_async_copy(ins[k], outs[k].at[me], sems[2].at[k]).wait()


def _call_hosting(body, exch, name, grid, in_specs, out_specs, out_shape, scratch_shapes, args):
    if exch is None:
        res = pl.pallas_call(
            body, name=name, grid=grid, in_specs=list(in_specs), out_specs=list(out_specs),
            out_shape=list(out_shape), scratch_shapes=list(scratch_shapes),
            compiler_params=_params(("arbitrary",) * len(grid)))(*args)
        return res, []
    n_in, n_out, n_scr, nc = len(in_specs), len(out_specs), len(scratch_shapes), exch.n

    def full_body(*refs):
        ins, refs = refs[:n_in], refs[n_in:]
        cins, refs = refs[:nc], refs[nc:]
        outs, refs = refs[:n_out], refs[n_out:]
        couts, refs = refs[:nc], refs[nc:]
        scr, sems = refs[:n_scr], refs[n_scr:]
        ids = [pl.program_id(a) for a in range(len(grid))]
        first = functools.reduce(jnp.logical_and, [i == 0 for i in ids])
        last = functools.reduce(jnp.logical_and, [i == g - 1 for i, g in zip(ids, grid)])

        @pl.when(first)
        def _():
            exch.start(cins, couts, sems)

        body(*ins, *outs, *scr)

        @pl.when(last)
        def _():
            exch.wait(cins, couts, sems)

    res = pl.pallas_call(
        full_body, name=name, grid=grid,
        in_specs=list(in_specs) + exch.specs, out_specs=list(out_specs) + exch.specs,
        out_shape=list(out_shape) + exch.out_shapes,
        scratch_shapes=list(scratch_shapes) + exch.sems,
        compiler_params=_params(("arbitrary",) * len(grid)),
    )(*args, *exch.arrays)
    return res[:n_out], res[n_out:]


def _rmsnorm_fwd(x, g, name):
    T, D = x.shape
    tr = min(256, T)

    def body(x_ref, g_ref, h_ref):
        xv = x_ref[...]
        r = lax.rsqrt(jnp.mean(xv * xv, axis=-1, keepdims=True) + EPS)
        h_ref[...] = (xv * r * g_ref[...]).astype(BF16)

    return pl.pallas_call(
        body, name=name, grid=(T // tr,),
        in_specs=[pl.BlockSpec((tr, D), lambda i: (i, 0)), pl.BlockSpec((1, D), lambda i: (0, 0))],
        out_specs=pl.BlockSpec((tr, D), lambda i: (i, 0)),
        out_shape=jax.ShapeDtypeStruct((T, D), BF16),
        compiler_params=_params(("parallel",)),
    )(x, g)


def _loss_head(x, g, target):
    T, D = x.shape
    tr = min(256, T)

    def body(x_ref, g_ref, t_ref, loss_ref, dx_ref, dg_ref):
        i = pl.program_id(0)
        xv = x_ref[...]
        gv = g_ref[...]
        r = lax.rsqrt(jnp.mean(xv * xv, axis=-1, keepdims=True) + EPS)
        xh = xv * r
        diff = xh * gv - t_ref[...]
        dy = diff * (1.0 / D)
        dxh = dy * gv
        dx_ref[...] = r * (dxh - xh * jnp.mean(dxh * xh, axis=-1, keepdims=True))

        @pl.when(i == 0)
        def _():
            dg_ref[...] = jnp.zeros_like(dg_ref)
            loss_ref[...] = jnp.zeros_like(loss_ref)

        dg_ref[...] += jnp.sum(dy * xh, axis=0, keepdims=True)
        part = jnp.sum(jnp.sum(diff * diff, axis=1, keepdims=True), axis=0, keepdims=True)
        loss_ref[...] += (0.5 / D) * part

    row = pl.BlockSpec((tr, D), lambda i: (i, 0))
    vec = pl.BlockSpec((1, D), lambda i: (0, 0))
    return pl.pallas_call(
        body, name="loss_head", grid=(T // tr,),
        in_specs=[row, vec, row],
        out_specs=[pl.BlockSpec((1, 1), lambda i: (0, 0)), row, vec],
        out_shape=[jax.ShapeDtypeStruct((1, 1), F32), jax.ShapeDtypeStruct((T, D), F32),
                   jax.ShapeDtypeStruct((1, D), F32)],
        compiler_params=_params(("arbitrary",)),
    )(x, g, target)


ELEMS_PER_STEP = 1 << 20


def _row_tile(R, per_row):
    best = None
    for tr in range(8, R + 1, 8):
        if R % tr == 0 and tr * per_row <= ELEMS_PER_STEP:
            best = tr
    return best if best is not None else R


def _adamw(parts, w, m, v, name):
    P, R, C = parts.shape
    tr = _row_tile(R, P * C)

    def body(p_ref, w_ref, m_ref, v_ref, g_out, d_out, m_out, v_out):
        g = p_ref[0].astype(F32)
        for p in range(1, P):
            g = g + p_ref[p].astype(F32)
        wv = w_ref[...]
        mn = ADAM_B1 * m_ref[...] + (1.0 - ADAM_B1) * g
        vn = ADAM_B2 * v_ref[...] + (1.0 - ADAM_B2) * (g * g)
        m_hat = mn / (1.0 - ADAM_B1 ** ADAM_STEP)
        v_hat = vn / (1.0 - ADAM_B2 ** ADAM_STEP)
        g_out[...] = g
        d_out[...] = -ADAM_LR * (m_hat / (jnp.sqrt(v_hat) + ADAM_EPS) + ADAM_WD * wv)
        m_out[...] = mn
        v_out[...] = vn

    row = pl.BlockSpec((tr, C), lambda i: (i, 0))
    return pl.pallas_call(
        body, name=name, grid=(R // tr,),
        in_specs=[pl.BlockSpec((P, tr, C), lambda i: (0, i, 0)), row, row, row],
        out_specs=[row, row, row, row],
        out_shape=[jax.ShapeDtypeStruct((R, C), F32)] * 4,
        compiler_params=_params(("parallel",)),
    )(parts, w, m, v)


def _sum_parts(parts, name):
    P, R, C = parts.shape
    tr = _row_tile(R, P * C)

    def body(p_ref, o_ref):
        g = p_ref[0]
        for p in range(1, P):
            g = g + p_ref[p]
        o_ref[...] = g

    return pl.pallas_call(
        body, name=name, grid=(R // tr,),
        in_specs=[pl.BlockSpec((P, tr, C), lambda i: (0, i, 0))],
        out_specs=pl.BlockSpec((tr, C), lambda i: (i, 0)),
        out_shape=jax.ShapeDtypeStruct((R, C), F32),
        compiler_params=_params(("parallel",)),
    )(parts)


def _lane_head(Dh):
    assert Dh & (Dh - 1) == 0 and Dh <= LANES
    return lax.shift_right_logical(lax.broadcasted_iota(jnp.int32, (1, LANES), 1), Dh.bit_length() - 1)


def _stack_heads(x, lane_head, hpb):
    return jnp.concatenate([jnp.where(lane_head == h, x, 0.0) for h in range(hpb)], axis=0)


def _unstack_heads(acc, lane_head, hpb):
    TQ = acc.shape[0] // hpb
    out = acc[0:TQ]
    for h in range(1, hpb):
        out = jnp.where(lane_head == h, acc[h * TQ:(h + 1) * TQ], out)
    return out


def _live_rows(x, r0, hpb):
    if r0 == 0:
        return x
    TQ = x.shape[0] // hpb
    return jnp.concatenate([x[h * TQ + r0:(h + 1) * TQ] for h in range(hpb)], axis=0)


def _put_rows(full, part, r0, hpb):
    if r0 == 0:
        return part
    TQ = full.shape[0] // hpb
    n = TQ - r0
    return jnp.concatenate(
        [blk for h in range(hpb) for blk in (full[h * TQ:h * TQ + r0], part[h * n:(h + 1) * n])], axis=0)


def _first_live_row(m, TQ, TK):
    return max(0, TQ - (m + 1) * TK)


def _key_tile(S):
    return ATT_TK if S % ATT_TK == 0 else BLK


def _query_tile(S):
    return ATT_TQ if S % ATT_TQ == 0 else BLK


def _lane_groups(P):
    return ATT_GP if P % ATT_GP == 0 else 1


def _lanes(u):
    return slice(u * LANES, (u + 1) * LANES)


def _causal_iotas(RS, TK, TQ, r0=0):
    n = TQ - r0
    assert n & (n - 1) == 0 and (TK % TQ == 0 or TQ % TK == 0)
    rows = RS // TQ * n
    trow = jnp.bitwise_and(lax.broadcasted_iota(jnp.int32, (rows, TK), 0), n - 1) + r0
    col = lax.broadcasted_iota(jnp.int32, (rows, TK), 1)
    return trow, col


def _tri(TK, op):
    r = lax.broadcasted_iota(jnp.int32, (TK, TK), 0)
    c = lax.broadcasted_iota(jnp.int32, (TK, TK), 1)
    return op(r, c).astype(BF16)


def _logsig_parts(z):
    lb = jnp.minimum(z, 0.0) - jnp.log(1.0 + jnp.exp(-jnp.abs(z)))
    return lb, lb - z


def _sb_fwd(proj3, W, heads, exch):
    B, S, _ = proj3.shape
    Dh = W // heads
    hpb = LANES // Dh
    P, TQ = W // LANES, _query_tile(S)
    NQ = S // TQ
    scale = 1.0 / math.sqrt(Dh)

    TK = _key_tile(S)
    RS = hpb * TQ
    NM = max(1, TQ // TK)
    GP = _lane_groups(P)
    PG = P // GP

    def body(q_ref, k_ref, v_ref, g_ref, o_ref, y_ref):
        i = pl.program_id(2)
        lane_head = _lane_head(Dh)
        msuf = _tri(TK, lambda r, c: r > c)
        qs = [(_stack_heads(q_ref[:, _lanes(u)], lane_head, hpb) * scale).astype(BF16) for u in range(GP)]
        nt = (i * TQ + TQ - 2) // TK + 1

        def tile(jt, carry, masked, r0=0):
            off = pl.multiple_of(jt * TK, TK)
            if masked:
                trow, col = _causal_iotas(RS, TK, TQ, r0)
                msk = col + (jt * TK - i * TQ) < trow
            out = []
            for u, (rem_all, acc_all) in enumerate(carry):
                rem, acc = _live_rows(rem_all, r0, hpb), _live_rows(acc_all, r0, hpb)
                kj = k_ref[pl.ds(off, TK), _lanes(u)].astype(BF16)
                vj = v_ref[pl.ds(off, TK), _lanes(u)].astype(BF16)
                lb, lr = _logsig_parts(_dot(_live_rows(qs[u], r0, hpb), kj, _NT))
                if masked:
                    lr = jnp.where(msk, lr, 0.0)
                w = jnp.exp(lb + _split_dot(lr, msuf) + rem)
                if masked:
                    w = jnp.where(msk, w, 0.0)
                out.append((_put_rows(rem_all, rem + jnp.sum(lr, axis=1, keepdims=True), r0, hpb),
                            _put_rows(acc_all, acc + _dot(w.astype(BF16), vj), r0, hpb)))
            return tuple(out)

        zero = (jnp.zeros((RS, 1), F32), jnp.zeros((RS, LANES), F32))
        carry = (zero,) * GP
        for m in range(NM):
            carry = tile(nt - 1 - m, carry, True, _first_live_row(m, TQ, TK))
        carry = lax.fori_loop(NM, nt, lambda jj, c: tile(nt - 1 - jj, c, False), carry)
        for u in range(GP):
            o = _unstack_heads(carry[u][1], lane_head, hpb)
            o_ref[:, _lanes(u)] = o
            y_ref[:, _lanes(u)] = (o * _silu(g_ref[:, _lanes(u)])).astype(BF16)

    LW = GP * LANES
    blk = lambda sec: pl.BlockSpec((None, TQ, LW), lambda b, p, i: (b, i, sec * PG + p))
    full = lambda sec: pl.BlockSpec((None, S, LW), lambda b, p, i: (b, 0, sec * PG + p))
    out = pl.BlockSpec((None, TQ, LW), lambda b, p, i: (b, i, p))
    return _call_hosting(
        body, exch, "sb_fwd", (B, PG, NQ), [blk(0), full(1), full(2), blk(3)], [out, out],
        [jax.ShapeDtypeStruct((B, S, W), F32), jax.ShapeDtypeStruct((B, S, W), BF16)], [],
        (proj3, proj3, proj3, proj3))


def _sb_bwd(proj3, o, dy, W, heads, exch):
    B, S, _ = proj3.shape
    Dh = W // heads
    hpb = LANES // Dh
    P, TQ = W // LANES, _query_tile(S)
    NQ = S // TQ
    scale = 1.0 / math.sqrt(Dh)

    TK = _key_tile(S)
    RS = hpb * TQ
    NM = max(1, TQ // TK)

    def body(q_ref, k_ref, v_ref, g_ref, o_ref, dy_ref, dp_ref, dk_ref, dv_ref, u_ref, sig_ref, es_ref):
        i = pl.program_id(2)
        rows = pl.ds(pl.multiple_of(i * TQ, TQ), TQ)

        @pl.when(i == 0)
        def _():
            dk_ref[...] = jnp.zeros_like(dk_ref)
            dv_ref[...] = jnp.zeros_like(dv_ref)

        lane_head = _lane_head(Dh)
        msuf = _tri(TK, lambda r, c: r > c)
        mpre = _tri(TK, lambda r, c: r < c)
        g = g_ref[...]
        dyv = dy_ref[...].astype(F32)
        dp_ref[3, rows, :] = (dyv * o_ref[...] * _dsilu(g)).astype(dp_ref.dtype)
        qs = (_stack_heads(q_ref[...], lane_head, hpb) * scale).astype(BF16)
        dos = _stack_heads(dyv * _silu(g), lane_head, hpb).astype(BF16)
        nt = (i * TQ + TQ - 2) // TK + 1

        def weights(jt, rem_all, masked, r0=0):
            off = pl.multiple_of(jt * TK, TK)
            kj = k_ref[pl.ds(off, TK), :].astype(BF16)
            vj = v_ref[pl.ds(off, TK), :].astype(BF16)
            dos_l = _live_rows(dos, r0, hpb)
            lb, lr = _logsig_parts(_dot(_live_rows(qs, r0, hpb), kj, _NT))
            if masked:
                trow, col = _causal_iotas(RS, TK, TQ, r0)
                msk = col + (jt * TK - i * TQ) < trow
                lr = jnp.where(msk, lr, 0.0)
            w = jnp.exp(lb + _split_dot(lr, msuf) + _live_rows(rem_all, r0, hpb))
            if masked:
                w = jnp.where(msk, w, 0.0)
            e = w * _dot(dos_l, vj, _NT)
            sig = jnp.exp(lb)
            u = e * (1.0 - sig) - _split_dot(e, mpre) * sig
            if masked:
                u = jnp.where(msk, u, 0.0)
                sig = jnp.where(msk, sig, 0.0)
            n = TQ - r0
            for h in range(hpb):
                u_ref[jt, h * TQ + r0:(h + 1) * TQ, :] = u[h * n:(h + 1) * n]
                sig_ref[jt, h * TQ + r0:(h + 1) * TQ, :] = sig[h * n:(h + 1) * n]
            es_ref[jt] = _put_rows(jnp.zeros((RS, 1), F32), jnp.sum(e, axis=1, keepdims=True), r0, hpb)
            dv_ref[pl.ds(off, TK), :] += _dot(w.astype(BF16), dos_l, _TN)
            return _put_rows(rem_all, _live_rows(rem_all, r0, hpb) + jnp.sum(lr, axis=1, keepdims=True), r0, hpb)

        rem = jnp.zeros((RS, 1), F32)
        for m in range(NM):
            rem = weights(nt - 1 - m, rem, True, _first_live_row(m, TQ, TK))
        lax.fori_loop(NM, nt, lambda jj, r: weights(nt - 1 - jj, r, False), rem)

        def grads(jt, carry, r0=0):
            pre, acc = carry
            off = pl.multiple_of(jt * TK, TK)
            kj = k_ref[pl.ds(off, TK), :].astype(BF16)
            if r0 == 0:
                u, sig = u_ref[jt], sig_ref[jt]
            else:
                u = jnp.concatenate([u_ref[jt, h * TQ + r0:(h + 1) * TQ, :] for h in range(hpb)], axis=0)
                sig = jnp.concatenate([sig_ref[jt, h * TQ + r0:(h + 1) * TQ, :] for h in range(hpb)], axis=0)
            dz = (u - _live_rows(pre, r0, hpb) * sig).astype(BF16)
            dk_ref[pl.ds(off, TK), :] += _dot(dz, _live_rows(qs, r0, hpb), _TN)
            return pre + es_ref[jt], _put_rows(acc, _live_rows(acc, r0, hpb) + _dot(dz, kj), r0, hpb)

        carry = lax.fori_loop(0, nt - NM, grads, (jnp.zeros((RS, 1), F32), jnp.zeros((RS, LANES), F32)))
        for m in reversed(range(NM)):
            carry = grads(nt - 1 - m, carry, _first_live_row(m, TQ, TK))
        _, acc = carry
        dp_ref[0, rows, :] = (_unstack_heads(acc, lane_head, hpb) * scale).astype(dp_ref.dtype)

        @pl.when(i == NQ - 1)
        def _():
            dp_ref[1] = dk_ref[...].astype(dp_ref.dtype)
            dp_ref[2] = dv_ref[...].astype(dp_ref.dtype)

    blk = lambda sec: pl.BlockSpec((None, TQ, LANES), lambda b, p, i: (b, i, sec * P + p))
    full = lambda sec: pl.BlockSpec((None, S, LANES), lambda b, p, i: (b, 0, sec * P + p))
    one = pl.BlockSpec((None, TQ, LANES), lambda b, p, i: (b, i, p))
    (dproj,), moved = _call_hosting(
        body, exch, "sb_bwd", (B, P, NQ), [blk(0), full(1), full(2), blk(3), one, one],
        [pl.BlockSpec((None, 4, S, LANES), lambda b, p, i: (b, 0, 0, p))],
        [jax.ShapeDtypeStruct((B, 4, S, W), BF16)],
        [pltpu.VMEM((S, LANES), F32), pltpu.VMEM((S, LANES), F32),
         pltpu.VMEM((S // TK, RS, TK), F32), pltpu.VMEM((S // TK, RS, TK), F32), pltpu.VMEM((S // TK, RS, 1), F32)],
        (proj3, proj3, proj3, proj3, o, dy))
    return dproj, moved


def _fox_gate_fwd(f_t, b_f):
    B, H, S = f_t.shape

    def body(f_ref, b_ref, c_ref):
        row = lax.broadcasted_iota(jnp.int32, (BLK, BLK), 0)
        col = lax.broadcasted_iota(jnp.int32, (BLK, BLK), 1)
        mpre = (row <= col).astype(BF16)
        carry = jnp.zeros((H, 1), F32)
        for n in range(S // BLK):
            sl = pl.ds(n * BLK, BLK)
            lf, _ = _logsig_parts(f_ref[:, sl] + b_ref[...])
            c_ref[:, sl] = _split3_dot(lf, mpre) + carry
            carry = carry + jnp.sum(lf, axis=1, keepdims=True)

    spec = pl.BlockSpec((None, H, S), lambda b: (b, 0, 0))
    return pl.pallas_call(
        body, name="fox_gate_fwd", grid=(B,),
        in_specs=[spec, pl.BlockSpec((H, 1), lambda b: (0, 0))], out_specs=spec,
        out_shape=jax.ShapeDtypeStruct((B, H, S), F32),
        compiler_params=_params(("parallel",)),
    )(f_t, b_f)


def _fox_gate_bwd(dcum_t, f_t, b_f):
    B, H, S = f_t.shape

    def body(d_ref, f_ref, b_ref, df_ref, db_ref):
        b = pl.program_id(0)

        @pl.when(b == 0)
        def _():
            db_ref[...] = jnp.zeros_like(db_ref)

        row = lax.broadcasted_iota(jnp.int32, (BLK, BLK), 0)
        col = lax.broadcasted_iota(jnp.int32, (BLK, BLK), 1)
        msuf = (row >= col).astype(BF16)
        carry = jnp.zeros((H, 1), F32)
        dbacc = jnp.zeros((H, 1), F32)
        for n in reversed(range(S // BLK)):
            sl = pl.ds(n * BLK, BLK)
            dv = d_ref[:, sl]
            dlf = _split3_dot(dv, msuf) + carry
            carry = carry + jnp.sum(dv, axis=1, keepdims=True)
            df = dlf * _sigmoid(-(f_ref[:, sl] + b_ref[...]))
            df_ref[:, sl] = df
            dbacc = dbacc + jnp.sum(df, axis=1, keepdims=True)
        db_ref[...] += dbacc

    spec = pl.BlockSpec((None, H, S), lambda b: (b, 0, 0))
    vec = pl.BlockSpec((H, 1), lambda b: (0, 0))
    return pl.pallas_call(
        body, name="fox_gate_bwd", grid=(B,),
        in_specs=[spec, spec, vec], out_specs=[spec, vec],
        out_shape=[jax.ShapeDtypeStruct((B, H, S), F32), jax.ShapeDtypeStruct((H, 1), F32)],
        compiler_params=_params(("arbitrary",)),
    )(dcum_t, f_t, b_f)


def _pick_col(block, idx, lane_iota):
    return jnp.sum(jnp.where(lane_iota == idx, block, 0.0), axis=1, keepdims=True)


def _pick_row(block, idx, sub_iota):
    return jnp.sum(jnp.where(sub_iota == idx, block, 0.0), axis=0, keepdims=True)


def _fox_fwd(proj3, cum_t, W, heads):
    B, S, _ = proj3.shape
    H = heads
    Dh = W // heads
    hpb = LANES // Dh
    P, TQ = W // LANES, _query_tile(S)
    NQ = S // TQ
    scale = 1.0 / math.sqrt(Dh)

    TK = _key_tile(S)
    RS = hpb * TQ
    NM = max(1, TQ // TK)

    def body(q_ref, k_ref, v_ref, g_ref, ct_ref, o_ref, y_ref, lse_ref):
        p = pl.program_id(1)
        i = pl.program_id(2)
        lane_head = _lane_head(Dh)
        sub_h = lax.broadcasted_iota(jnp.int32, (H, 1), 0)
        qs = (_stack_heads(q_ref[...], lane_head, hpb) * scale).astype(BF16)
        nt = (i * TQ + TQ - 1) // TK + 1

        def tile(jt, carry, masked, r0=0):
            mx, l, acc = carry
            n = TQ - r0
            off = pl.multiple_of(jt * TK, TK)
            kj = k_ref[pl.ds(off, TK), :].astype(BF16)
            vj = v_ref[pl.ds(off, TK), :].astype(BF16)
            ctb = ct_ref[:, pl.ds(off, TK)]
            z = _dot(_live_rows(qs, r0, hpb), kj, _NT)
            s = jnp.concatenate([z[h * n:(h + 1) * n] - _pick_row(ctb, p * hpb + h, sub_h) for h in range(hpb)],
                                axis=0)
            if masked:
                trow, col = _causal_iotas(RS, TK, TQ, r0)
                s = jnp.where(col + (jt * TK - i * TQ) <= trow, s, NEG_BIG)
            mx2 = jnp.maximum(mx, jnp.max(s, axis=1, keepdims=True))
            pe = jnp.exp(s - mx2)
            alpha = jnp.exp(mx - mx2)
            return (mx2, alpha * l + jnp.sum(pe, axis=1, keepdims=True), alpha * acc + _dot(pe.astype(BF16), vj))

        carry = lax.fori_loop(
            0, nt - NM, lambda jt, c: tile(jt, c, False),
            (jnp.full((RS, 1), NEG_BIG, F32), jnp.zeros((RS, 1), F32), jnp.zeros((RS, LANES), F32)))
        for m in reversed(range(NM)):
            carry = tile(nt - 1 - m, carry, True)
        mx, l, acc = carry
        o = _unstack_heads(acc / l, lane_head, hpb)
        o_ref[...] = o
        lse_ref[...] = _unstack_heads(jnp.broadcast_to(mx + jnp.log(l), (RS, LANES)), lane_head, hpb)
        y_ref[...] = (o * _silu(g_ref[...])).astype(BF16)

    blk = lambda sec: pl.BlockSpec((None, TQ, LANES), lambda b, p, i: (b, i, sec * P + p))
    full = lambda sec: pl.BlockSpec((None, S, LANES), lambda b, p, i: (b, 0, sec * P + p))
    out = pl.BlockSpec((None, TQ, LANES), lambda b, p, i: (b, i, p))
    return pl.pallas_call(
        body, name="fox_fwd", grid=(B, P, NQ),
        in_specs=[blk(0), full(1), full(2), blk(3),
                  pl.BlockSpec((None, H, S), lambda b, p, i: (b, 0, 0))],
        out_specs=[out, out, out],
        out_shape=[jax.ShapeDtypeStruct((B, S, W), F32), jax.ShapeDtypeStruct((B, S, W), BF16),
                   jax.ShapeDtypeStruct((B, S, W), F32)],
        compiler_params=_params(("parallel", "parallel", "arbitrary")),
    )(proj3, proj3, proj3, proj3, cum_t)


def _fox_bwd(proj3, cum_t, o, lse, dy, W, heads):
    B, S, _ = proj3.shape
    H = heads
    Dh = W // heads
    hpb = LANES // Dh
    P, TQ = W // LANES, _query_tile(S)
    NQ = S // TQ
    scale = 1.0 / math.sqrt(Dh)

    TK = _key_tile(S)
    RS = hpb * TQ
    NM = max(1, TQ // TK)

    def body(q_ref, k_ref, v_ref, g_ref, ct_ref, o_ref, lse_ref, dy_ref,
             dpj_ref, dc_ref, dk_ref, dv_ref, p_scr, dp_scr):
        p = pl.program_id(1)
        i = pl.program_id(2)
        rows = pl.ds(pl.multiple_of(i * TQ, TQ), TQ)

        @pl.when(i == 0)
        def _():
            dk_ref[...] = jnp.zeros_like(dk_ref)
            dv_ref[...] = jnp.zeros_like(dv_ref)
            dc_ref[...] = jnp.zeros_like(dc_ref)

        lane_head = _lane_head(Dh)
        sub_h = lax.broadcasted_iota(jnp.int32, (H, 1), 0)
        lane = lax.broadcasted_iota(jnp.int32, (1, LANES), 1)
        g = g_ref[...]
        lsev = lse_ref[...]
        dyv = dy_ref[...].astype(F32)
        dpj_ref[3, rows, :] = (dyv * o_ref[...] * _dsilu(g)).astype(dpj_ref.dtype)
        qs = (_stack_heads(q_ref[...], lane_head, hpb) * scale).astype(BF16)
        dos = _stack_heads(dyv * _silu(g), lane_head, hpb).astype(BF16)
        neg_lse = -jnp.concatenate([_pick_col(lsev, h * Dh, lane) for h in range(hpb)], axis=0)
        nt = (i * TQ + TQ - 1) // TK + 1

        def probs(jt, dsum, masked, r0=0):
            n = TQ - r0
            off = pl.multiple_of(jt * TK, TK)
            kj = k_ref[pl.ds(off, TK), :].astype(BF16)
            vj = v_ref[pl.ds(off, TK), :].astype(BF16)
            ctb = ct_ref[:, pl.ds(off, TK)]
            dos_l = _live_rows(dos, r0, hpb)
            z = _dot(_live_rows(qs, r0, hpb), kj, _NT) + _live_rows(neg_lse, r0, hpb)
            s = jnp.concatenate([z[h * n:(h + 1) * n] - _pick_row(ctb, p * hpb + h, sub_h) for h in range(hpb)],
                                axis=0)
            pr = jnp.exp(s)
            if masked:
                trow, col = _causal_iotas(RS, TK, TQ, r0)
                pr = jnp.where(col + (jt * TK - i * TQ) <= trow, pr, 0.0)
            dp = _dot(dos_l, vj, _NT)
            p_scr[jt] = _put_rows(jnp.zeros((RS, TK), F32), pr, r0, hpb)
            dp_scr[jt] = _put_rows(jnp.zeros((RS, TK), F32), dp, r0, hpb)
            dv_ref[pl.ds(off, TK), :] += _dot(pr.astype(BF16), dos_l, _TN)
            return _put_rows(dsum, _live_rows(dsum, r0, hpb) + jnp.sum(pr * dp, axis=1, keepdims=True), r0, hpb)

        dsum = lax.fori_loop(0, nt - NM, lambda jt, d: probs(jt, d, False), jnp.zeros((RS, 1), F32))
        for m in reversed(range(NM)):
            dsum = probs(nt - 1 - m, dsum, True)

        def grads(jt, acc, r0=0):
            n = TQ - r0
            off = pl.multiple_of(jt * TK, TK)
            kj = k_ref[pl.ds(off, TK), :].astype(BF16)
            if r0 == 0:
                pr, dp = p_scr[jt], dp_scr[jt]
            else:
                pr = jnp.concatenate([p_scr[jt, h * TQ + r0:(h + 1) * TQ, :] for h in range(hpb)], axis=0)
                dp = jnp.concatenate([dp_scr[jt, h * TQ + r0:(h + 1) * TQ, :] for h in range(hpb)], axis=0)
            ds = pr * (dp - _live_rows(dsum, r0, hpb))
            for h in range(hpb):
                dc_ref[h:h + 1, pl.ds(off, TK)] -= jnp.sum(ds[h * n:(h + 1) * n], axis=0, keepdims=True)
            dsb = ds.astype(BF16)
            dk_ref[pl.ds(off, TK), :] += _dot(dsb, _live_rows(qs, r0, hpb), _TN)
            return _put_rows(acc, _live_rows(acc, r0, hpb) + _dot(dsb, kj), r0, hpb)

        acc = lax.fori_loop(0, nt - NM, grads, jnp.zeros((RS, LANES), F32))
        for m in reversed(range(NM)):
            acc = grads(nt - 1 - m, acc, _first_live_row(m, TQ, TK))
        dpj_ref[0, rows, :] = (_unstack_heads(acc, lane_head, hpb) * scale).astype(dpj_ref.dtype)

        @pl.when(i == NQ - 1)
        def _():
            dpj_ref[1] = dk_ref[...].astype(dpj_ref.dtype)
            dpj_ref[2] = dv_ref[...].astype(dpj_ref.dtype)

    blk = lambda sec: pl.BlockSpec((None, TQ, LANES), lambda b, p, i: (b, i, sec * P + p))
    full = lambda sec: pl.BlockSpec((None, S, LANES), lambda b, p, i: (b, 0, sec * P + p))
    one = pl.BlockSpec((None, TQ, LANES), lambda b, p, i: (b, i, p))
    return pl.pallas_call(
        body, name="fox_bwd", grid=(B, P, NQ),
        in_specs=[blk(0), full(1), full(2), blk(3),
                  pl.BlockSpec((None, H, S), lambda b, p, i: (b, 0, 0)),
                  one, one, one],
        out_specs=[pl.BlockSpec((None, 4, S, LANES), lambda b, p, i: (b, 0, 0, p)),
                   pl.BlockSpec((None, None, hpb, S), lambda b, p, i: (b, p, 0, 0))],
        out_shape=[jax.ShapeDtypeStruct((B, 4, S, W), BF16), jax.ShapeDtypeStruct((B, P, hpb, S), F32)],
        scratch_shapes=[pltpu.VMEM((S, LANES), F32), pltpu.VMEM((S, LANES), F32),
                        pltpu.VMEM((S // TK, RS, TK), F32), pltpu.VMEM((S // TK, RS, TK), F32)],
        compiler_params=_params(("parallel", "parallel", "arbitrary")),
    )(proj3, proj3, proj3, proj3, cum_t, o, lse, dy)


def _layernorm_rows(v, gamma, beta):
    mu = jnp.mean(v, axis=-1, keepdims=True)
    xc = v - mu
    rstd = lax.rsqrt(jnp.mean(xc * xc, axis=-1, keepdims=True) + EPS)
    xh = xc * rstd
    return xh, rstd, xh * gamma + beta


def _layernorm_rows_bwd(dout, xh, rstd, gamma):
    dxh = dout * gamma
    return rstd * (dxh - jnp.mean(dxh, axis=-1, keepdims=True) - xh * jnp.mean(dxh * xh, axis=-1, keepdims=True))


def _gmlp_fwd(proj, wm, bs_t, ln_g, ln_b, W):
    T = proj.shape[0]
    G = wm.shape[0]
    cg = W // G
    assert cg == LANES

    def body(p_ref, wm_ref, bs_ref, lg_ref, lb_ref, y_ref, vn_ref):
        lane = lax.broadcasted_iota(jnp.int32, (1, LANES), 1)
        _, _, vn = _layernorm_rows(_gelu(p_ref[:, W:2 * W]), lg_ref[...], lb_ref[...])
        vn_ref[...] = vn.astype(BF16)
        bs = bs_ref[...]
        for g in range(G):
            sl = pl.ds(g * cg, cg)
            s = _dot(wm_ref[g], vn_ref[:, sl]) + _pick_col(bs, g, lane)
            gate = p_ref[:, pl.ds(2 * W + g * cg, cg)]
            y_ref[:, sl] = (_gelu(p_ref[:, sl]) * s * _silu(gate)).astype(BF16)

    vec = pl.BlockSpec((1, W), lambda r: (0, 0))
    return pl.pallas_call(
        body, name="gmlp_fwd", grid=(T // BLK,),
        in_specs=[pl.BlockSpec((BLK, 3 * W), lambda r: (r, 0)),
                  pl.BlockSpec((G, BLK, BLK), lambda r: (0, 0, 0)),
                  pl.BlockSpec((BLK, LANES), lambda r: (0, 0)), vec, vec],
        out_specs=pl.BlockSpec((BLK, W), lambda r: (r, 0)),
        out_shape=jax.ShapeDtypeStruct((T, W), BF16),
        scratch_shapes=[pltpu.VMEM((BLK, W), BF16)],
        compiler_params=_params(("parallel",)),
    )(proj, wm, bs_t, ln_g, ln_b)


def _gmlp_bwd(proj, dy, wm, bs_t, ln_g, ln_b, W):
    T = proj.shape[0]
    G = wm.shape[0]
    cg = W // G

    def body(p_ref, dy_ref, wm_ref, bs_ref, lg_ref, lb_ref,
             dp_ref, dwm_ref, dbs_ref, dlg_ref, dlb_ref, vn_ref, dvn_ref):
        r = pl.program_id(0)

        @pl.when(r == 0)
        def _():
            dwm_ref[...] = jnp.zeros_like(dwm_ref)
            dbs_ref[...] = jnp.zeros_like(dbs_ref)
            dlg_ref[...] = jnp.zeros_like(dlg_ref)
            dlb_ref[...] = jnp.zeros_like(dlb_ref)

        lane = lax.broadcasted_iota(jnp.int32, (1, LANES), 1)
        vpre = p_ref[:, W:2 * W]
        gamma = lg_ref[...]
        xh, rstd, vn = _layernorm_rows(_gelu(vpre), gamma, lb_ref[...])
        vn_ref[...] = vn.astype(BF16)
        bs = bs_ref[...]
        dbs = jnp.zeros((BLK, LANES), F32)
        for g in range(G):
            sl = pl.ds(g * cg, cg)
            gsl = pl.ds(2 * W + g * cg, cg)
            vng = vn_ref[:, sl]
            s = _dot(wm_ref[g], vng) + _pick_col(bs, g, lane)
            upre = p_ref[:, sl]
            u = _gelu(upre)
            gate = p_ref[:, gsl]
            dyv = dy_ref[:, sl].astype(F32)
            dp_ref[:, gsl] = (dyv * u * s * _dsilu(gate)).astype(dp_ref.dtype)
            do = dyv * _silu(gate)
            dp_ref[:, sl] = (do * s * _dgelu(upre)).astype(dp_ref.dtype)
            ds = do * u
            dbs = dbs + jnp.where(lane == g, jnp.sum(ds, axis=1, keepdims=True), 0.0)
            dsb = ds.astype(BF16)
            dwm_ref[g] += _dot(dsb, vng, _NT)
            dvn_ref[:, sl] = _dot(wm_ref[g], dsb, _TN)
        dbs_ref[...] += dbs
        dvn = dvn_ref[...]
        dlg_ref[...] += jnp.sum(dvn * xh, axis=0, keepdims=True)
        dlb_ref[...] += jnp.sum(dvn, axis=0, keepdims=True)
        dv = _layernorm_rows_bwd(dvn, xh, rstd, gamma)
        dp_ref[:, W:2 * W] = (dv * _dgelu(vpre)).astype(dp_ref.dtype)

    vec = pl.BlockSpec((1, W), lambda r: (0, 0))
    return pl.pallas_call(
        body, name="gmlp_bwd", grid=(T // BLK,),
        in_specs=[pl.BlockSpec((BLK, 3 * W), lambda r: (r, 0)),
                  pl.BlockSpec((BLK, W), lambda r: (r, 0)),
                  pl.BlockSpec((G, BLK, BLK), lambda r: (0, 0, 0)),
                  pl.BlockSpec((BLK, LANES), lambda r: (0, 0)), vec, vec],
        out_specs=[pl.BlockSpec((BLK, 3 * W), lambda r: (r, 0)),
                   pl.BlockSpec((G, BLK, BLK), lambda r: (0, 0, 0)),
                   pl.BlockSpec((BLK, LANES), lambda r: (0, 0)), vec, vec],
        out_shape=[jax.ShapeDtypeStruct((T, 3 * W), BF16), jax.ShapeDtypeStruct((G, BLK, BLK), F32),
                   jax.ShapeDtypeStruct((BLK, LANES), F32),
                   jax.ShapeDtypeStruct((1, W), F32), jax.ShapeDtypeStruct((1, W), F32)],
        scratch_shapes=[pltpu.VMEM((BLK, W), BF16), pltpu.VMEM((BLK, W), F32)],
        compiler_params=_params(("arbitrary",)),
    )(proj, dy, wm, bs_t, ln_g, ln_b)


SUBLANES = 8
SHIFT_ROWS = CONV_HALO + BLK - SUBLANES


def _shift_rows(ext_ref, sh_ref, off):
    for r in range(1, SUBLANES):
        sh_ref[r - 1] = ext_ref[pl.ds(r, SHIFT_ROWS), pl.ds(off, LANES)]


def _rows_from(ext_ref, sh_ref, off, start):
    r = start % SUBLANES
    if r == 0:
        return ext_ref[pl.ds(start, BLK), pl.ds(off, LANES)]
    return sh_ref[r - 1, pl.ds(start - r, BLK), :]


def _conv_taps(ext_ref, sh_ref, cw_ref, off, n_taps, first):
    acc = jnp.zeros((BLK, LANES), F32)
    for k in range(n_taps):
        acc = acc + cw_ref[k:k + 1, pl.ds(off, LANES)] * _rows_from(ext_ref, sh_ref, off, first + k)
    return acc


def _fill_glu_ext(ext_ref, halo_ref, cur_ref, W, first_block):
    y0h = halo_ref[:, :W] * _sigmoid(halo_ref[:, W:])
    ext_ref[0:CONV_HALO, :] = jnp.where(first_block, 0.0, y0h)
    ext_ref[CONV_HALO:CONV_HALO + BLK, :] = cur_ref[:, :W] * _sigmoid(cur_ref[:, W:])


def _conv_specs(S, W):
    per = BLK // CONV_HALO
    cur = pl.BlockSpec((None, BLK, 2 * W), lambda b, i: (b, i, 0))
    halo = pl.BlockSpec((None, CONV_HALO, 2 * W), lambda b, i: (b, jnp.maximum(i * per - 1, 0), 0))
    gate = pl.BlockSpec((None, BLK, W), lambda b, i: (b, i, 2))
    return cur, halo, gate


def _conv_fwd(proj3, cw, cb, ln_g, ln_b, W, exch):
    B, S, _ = proj3.shape
    K = cw.shape[0]
    first = CONV_HALO - (K - 1)
    assert first >= 0

    def body(cur_ref, halo_ref, g_ref, cw_ref, cb_ref, lg_ref, lb_ref, y_ref, y1_ref, ext_ref, sh_ref):
        i = pl.program_id(1)
        _fill_glu_ext(ext_ref, halo_ref, cur_ref, W, i == 0)

        def chan(c, _):
            off = pl.multiple_of(c * LANES, LANES)
            _shift_rows(ext_ref, sh_ref, off)
            y1_ref[:, pl.ds(off, LANES)] = (_conv_taps(ext_ref, sh_ref, cw_ref, off, K, first)
                                            + cb_ref[:, pl.ds(off, LANES)])
            return 0

        lax.fori_loop(0, W // LANES, chan, 0)
        _, _, ln = _layernorm_rows(y1_ref[...], lg_ref[...], lb_ref[...])
        y_ref[...] = (_silu(ln) * _silu(g_ref[...])).astype(BF16)

    cur, halo, gate = _conv_specs(S, W)
    vec = pl.BlockSpec((1, W), lambda b, i: (0, 0))
    one = pl.BlockSpec((None, BLK, W), lambda b, i: (b, i, 0))
    (y, y1), moved = _call_hosting(
        body, exch, "conv_fwd", (B, S // BLK),
        [cur, halo, gate, pl.BlockSpec((K, W), lambda b, i: (0, 0)), vec, vec, vec],
        [one, one], [jax.ShapeDtypeStruct((B, S, W), BF16), jax.ShapeDtypeStruct((B, S, W), F32)],
        [pltpu.VMEM((CONV_HALO + BLK, W), F32), pltpu.VMEM((SUBLANES - 1, SHIFT_ROWS, LANES), F32)],
        (proj3, proj3, proj3, cw, cb, ln_g, ln_b))
    return y, y1, moved


def _conv_bwd1(proj3, y1, dy, K, ln_g, ln_b, W, exch):
    B, S, _ = proj3.shape
    first = CONV_HALO - (K - 1)

    def body(cur_ref, halo_ref, g_ref, y1_ref, dy_ref, lg_ref, lb_ref,
             dy1_ref, dg_ref, dcw_ref, dcb_ref, dlg_ref, dlb_ref, ext_ref, sh_ref):
        b = pl.program_id(0)
        i = pl.program_id(1)

        @pl.when(jnp.logical_and(b == 0, i == 0))
        def _():
            dcw_ref[...] = jnp.zeros_like(dcw_ref)
            dcb_ref[...] = jnp.zeros_like(dcb_ref)
            dlg_ref[...] = jnp.zeros_like(dlg_ref)
            dlb_ref[...] = jnp.zeros_like(dlb_ref)

        _fill_glu_ext(ext_ref, halo_ref, cur_ref, W, i == 0)
        gamma = lg_ref[...]
        xh, rstd, ln = _layernorm_rows(y1_ref[...], gamma, lb_ref[...])
        g = g_ref[...]
        dyv = dy_ref[...].astype(F32)
        dg_ref[...] = (dyv * _silu(ln) * _dsilu(g)).astype(dg_ref.dtype)
        dln = dyv * _silu(g) * _dsilu(ln)
        dlg_ref[...] += jnp.sum(dln * xh, axis=0, keepdims=True)
        dlb_ref[...] += jnp.sum(dln, axis=0, keepdims=True)
        dy1 = _layernorm_rows_bwd(dln, xh, rstd, gamma)
        dy1_ref[...] = dy1
        dcb_ref[...] += jnp.sum(dy1, axis=0, keepdims=True)

        def chan_w(c, _):
            off = pl.multiple_of(c * LANES, LANES)
            _shift_rows(ext_ref, sh_ref, off)
            d = dy1_ref[:, pl.ds(off, LANES)]
            for k in range(K):
                dcw_ref[k:k + 1, pl.ds(off, LANES)] += jnp.sum(
                    d * _rows_from(ext_ref, sh_ref, off, first + k), axis=0, keepdims=True)
            return 0

        lax.fori_loop(0, W // LANES, chan_w, 0)

    cur, halo, gate = _conv_specs(S, W)
    vec = pl.BlockSpec((1, W), lambda b, i: (0, 0))
    taps = pl.BlockSpec((K, W), lambda b, i: (0, 0))
    one = pl.BlockSpec((None, BLK, W), lambda b, i: (b, i, 0))
    return _call_hosting(
        body, exch, "conv_bwd1", (B, S // BLK), [cur, halo, gate, one, one, vec, vec],
        [one, one, taps, vec, vec, vec],
        [jax.ShapeDtypeStruct((B, S, W), F32), jax.ShapeDtypeStruct((B, S, W), BF16),
         jax.ShapeDtypeStruct((K, W), F32)] + [jax.ShapeDtypeStruct((1, W), F32)] * 3,
        [pltpu.VMEM((CONV_HALO + BLK, W), F32), pltpu.VMEM((SUBLANES - 1, SHIFT_ROWS, LANES), F32)],
        (proj3, proj3, proj3, y1, dy, ln_g, ln_b))


def _conv_bwd2(proj3, dy1, dgate, cw_rev, W):
    B, S, _ = proj3.shape
    K = cw_rev.shape[0]
    NQ = S // BLK
    per = BLK // CONV_HALO

    def body(cur_ref, d_ref, dnext_ref, dgate_ref, cw_ref, dp_ref, ext_ref, dy0_ref, sh_ref):
        i = pl.program_id(1)
        ext_ref[0:BLK, :] = d_ref[...]
        ext_ref[BLK:BLK + CONV_HALO, :] = jnp.where(i == NQ - 1, 0.0, dnext_ref[...])

        def chan(c, _):
            off = pl.multiple_of(c * LANES, LANES)
            _shift_rows(ext_ref, sh_ref, off)
            dy0_ref[:, pl.ds(off, LANES)] = _conv_taps(ext_ref, sh_ref, cw_ref, off, K, 0)
            return 0

        lax.fori_loop(0, W // LANES, chan, 0)
        a = cur_ref[:, :W]
        sg = _sigmoid(cur_ref[:, W:])
        dy0 = dy0_ref[...]
        dp_ref[:, 0:W] = (dy0 * sg).astype(dp_ref.dtype)
        dp_ref[:, W:2 * W] = (dy0 * a * sg * (1.0 - sg)).astype(dp_ref.dtype)
        dp_ref[:, 2 * W:3 * W] = dgate_ref[...]

    cur = pl.BlockSpec((None, BLK, 2 * W), lambda b, i: (b, i, 0))
    one = pl.BlockSpec((None, BLK, W), lambda b, i: (b, i, 0))
    nxt = pl.BlockSpec((None, CONV_HALO, W), lambda b, i: (b, jnp.minimum((i + 1) * per, S // CONV_HALO - 1), 0))
    return pl.pallas_call(
        body, name="conv_bwd2", grid=(B, NQ),
        in_specs=[cur, one, nxt, one, pl.BlockSpec((K, W), lambda b, i: (0, 0))],
        out_specs=pl.BlockSpec((None, BLK, 3 * W), lambda b, i: (b, i, 0)),
        out_shape=jax.ShapeDtypeStruct((B, S, 3 * W), BF16),
        scratch_shapes=[pltpu.VMEM((BLK + CONV_HALO, W), F32), pltpu.VMEM((BLK, W), F32),
                        pltpu.VMEM((SUBLANES - 1, SHIFT_ROWS, LANES), F32)],
        compiler_params=_params(("parallel", "parallel")),
    )(proj3, dy1, dy1, dgate, cw_rev)


def _pack(arrays):
    flat = jnp.concatenate([a.astype(F32).reshape(-1) for a in arrays])
    n = flat.shape[0]
    pad = (-n) % (8 * LANES)
    if pad:
        flat = jnp.concatenate([flat, jnp.zeros((pad,), F32)])
    return flat.reshape(-1, LANES)


def _unpack(packed, shapes, lead=()):
    flat = packed.reshape(lead + (-1,))
    out, off = [], 0
    for shp in shapes:
        n = math.prod(shp)
        out.append(flat[..., off:off + n].reshape(lead + tuple(shp)))
        off += n
    return out


def _cols_from_dev(g):
    g = jnp.moveaxis(g, 0, -2)
    return g.reshape(g.shape[:-2] + (g.shape[-2] * g.shape[-1],))


def _my_cols(full, me):
    n8 = full.shape[-1] // N_DEV
    return lax.dynamic_slice_in_dim(full, me * n8, n8, axis=full.ndim - 1)


def kernel(x, a_norm, a_w_in, a_w_out, b_norm, b_w_in, b_v_ln_g, b_v_ln_b, b_w_s, b_b_s, b_w_out, c_norm, c_w_in, c_conv_w, c_conv_b, c_ln_g, c_ln_b, c_w_out, d_norm, d_w_in, d_b_f, d_w_out, final_norm, loss_target, m_a_norm, m_a_w_in, m_a_w_out, m_b_norm, m_b_w_in, m_b_v_ln_g, m_b_v_ln_b, m_b_w_s, m_b_b_s, m_b_w_out, m_c_norm, m_c_w_in, m_c_conv_w, m_c_conv_b, m_c_ln_g, m_c_ln_b, m_c_w_out, m_d_norm, m_d_w_in, m_d_b_f, m_d_w_out, m_final_norm, v_a_norm, v_a_w_in, v_a_w_out, v_b_norm, v_b_w_in, v_b_v_ln_g, v_b_v_ln_b, v_b_w_s, v_b_b_s, v_b_w_out, v_c_norm, v_c_w_in, v_c_conv_w, v_c_conv_b, v_c_ln_g, v_c_ln_b, v_c_w_out, v_d_norm, v_d_w_in, v_d_b_f, v_d_w_out, v_final_norm):
    B, S, D = x.shape
    T = B * S
    xi, yi, ci = _me()
    me = 4 * xi + 2 * yi + ci

    G = b_w_s.shape[1]
    KC = c_conv_w.shape[1]
    H_D = d_b_f.shape[1]
    W_A = a_w_out.shape[1] * N_DEV
    W_B = b_w_out.shape[1] * N_DEV
    W_C = c_w_out.shape[1] * N_DEV
    W_D = d_w_out.shape[1] * N_DEV
    N_D = d_w_in.shape[2] * N_DEV
    N_D_PAD = -(-N_D // (3 * LANES)) * (3 * LANES)

    big_names = ["a_w_in", "a_w_out", "b_w_in", "b_w_out", "c_w_in", "c_w_out", "d_w_in", "d_w_out"]
    big_w = dict(a_w_in=a_w_in[0], a_w_out=a_w_out[0], b_w_in=b_w_in[0], b_w_out=b_w_out[0],
                 c_w_in=c_w_in[0], c_w_out=c_w_out[0], d_w_in=d_w_in[0], d_w_out=d_w_out[0])
    small_sharded = [b_norm, b_v_ln_g, b_v_ln_b, c_norm, c_conv_w, c_conv_b, c_ln_g, c_ln_b, d_norm]
    first_names, later_names, last_names = big_names[:1], big_names[1:6], big_names[6:]
    gathered = _GatherViaSibling(
        [big_w[n].astype(BF16) for n in first_names] + [_pack(small_sharded)]).run("gather_first")
    wg = dict(zip(first_names, gathered[:-1]))
    (b_norm_f, b_lg_f, b_lb_f, c_norm_f, c_cw_f, c_cb_f, c_lg_f, c_lb_f, d_norm_f) = [
        _cols_from_dev(t) for t in _unpack(gathered[-1], [s.shape for s in small_sharded], lead=(N_DEV,))]
    c_cw_f = c_cw_f[0]

    wm = jnp.tril(b_w_s[0]).astype(BF16)
    bs_t = jnp.pad(b_b_s[0].T, ((0, 0), (0, LANES - G)))

    x0 = x.reshape(T, D)
    h_a = _rmsnorm_fwd(x0, a_norm, "rms_a")
    proj_a = _mm_w_dev(h_a, wg["a_w_in"], "proj_a").reshape(B, S, 4 * W_A)
    (o_a, y_a), later = _sb_fwd(proj_a, W_A, SB_HEADS,
                                _Exchange([big_w[n].astype(BF16) for n in later_names], ["gather"] * len(later_names)))
    wg.update(zip(later_names, later))
    a_w_out_f = wg["a_w_out"].reshape(W_A, D)
    b_w_out_f = wg["b_w_out"].reshape(W_B, D)
    c_w_out_f = wg["c_w_out"].reshape(W_C, D)
    y_a = y_a.reshape(T, W_A)
    x1, h_b = _mm(y_a, a_w_out_f, "nn", T, D, W_A, F32, "out_a", 512, D, W_A, res=x0, norm_gain=b_norm_f)
    proj_b = _mm_w_dev(h_b, wg["b_w_in"], "proj_b")
    y_b = _gmlp_fwd(proj_b, wm, bs_t, b_lg_f, b_lb_f, W_B)
    x2, h_c = _mm(y_b, b_w_out_f, "nn", T, D, W_B, F32, "out_b", 512, D, W_B, res=x1, norm_gain=c_norm_f)
    proj_c = _mm_w_dev(h_c, wg["c_w_in"], "proj_c").reshape(B, S, 3 * W_C)
    y_c, y1_c, last = _conv_fwd(
        proj_c, c_cw_f, c_cb_f, c_lg_f, c_lb_f, W_C,
        _Exchange([big_w[n].astype(BF16) for n in last_names], ["gather"] * len(last_names)))
    wg.update(zip(last_names, last))
    d_w_out_f = wg["d_w_out"].reshape(W_D, D)
    d_w_in_f = jnp.pad(_cols_from_dev(wg["d_w_in"]), ((0, 0), (0, N_D_PAD - N_D)))
    y_c = y_c.reshape(T, W_C)
    x3, h_d = _mm(y_c, c_w_out_f, "nn", T, D, W_C, F32, "out_c", 512, D, W_C, res=x2, norm_gain=d_norm_f)
    proj_d = _mm(h_d, d_w_in_f, "nn", T, N_D_PAD, D, F32, "proj_d", 1024, 384, D).reshape(B, S, N_D_PAD)
    f_t = jnp.swapaxes(proj_d[:, :, 4 * W_D:4 * W_D + H_D], 1, 2)
    b_f_col = d_b_f.reshape(H_D, 1)
    cum_t = _fox_gate_fwd(f_t, b_f_col)
    o_d, y_d, lse_d = _fox_fwd(proj_d, cum_t, W_D, H_D)
    y_d = y_d.reshape(T, W_D)
    x4 = _mm(y_d, d_w_out_f, "nn", T, D, W_D, F32, "out_d", 512, D, W_D, res=x3)

    loss_part, dx, g_final = _loss_head(x4, final_norm.reshape(1, D), loss_target.reshape(T, D))
    loss = lax.psum(loss_part[0, 0], MESH_AXES)

    dy_d = _mm(dx, d_w_out_f, "nt", T, W_D, D, BF16, "dy_d", 512, W_D, D).reshape(B, S, W_D)
    gw_d_out = _mm(y_d, dx, "tn", W_D, D, T, BF16, "gw_d_out", W_D, D, 512).reshape(N_DEV, W_D // N_DEV, D)
    dproj_d, dcum = _fox_bwd(proj_d, cum_t, o_d, lse_d, dy_d, W_D, H_D)
    df_t, g_b_f = _fox_gate_bwd(dcum.reshape(B, H_D, S), f_t, b_f_col)
    F_PAD = N_D_PAD - 4 * W_D
    df = jnp.pad(jnp.swapaxes(df_t, 1, 2), ((0, 0), (0, 0), (0, F_PAD - H_D))).reshape(T, F_PAD)
    tc, tr = min(512, W_D), min(1024, S)
    gw_main = _mm(h_d, dproj_d, "tn", D, 4 * W_D, T, BF16, "gw_d_in", D, tc, tr,
                  b_spec=_sectioned_spec(dproj_d, tr, tc, 2, 1))
    gw_f = _mm(h_d, df, "tn", D, F_PAD, T, BF16, "gw_d_in_f", D, F_PAD, 512)
    gw_d_in = jnp.moveaxis(
        jnp.concatenate([gw_main, gw_f], axis=1)[:, :N_D].reshape(D, N_DEV, N_D // N_DEV), 1, 0)
    dh_f = _mm(df, d_w_in_f[:, 4 * W_D:], "nt", T, D, F_PAD, F32, "dh_d_f", 512, D, F_PAD)
    dx, g_d_norm, _ = _mm(dproj_d, d_w_in_f, "nt", T, D, 4 * W_D, F32, "dh_d", tr, D, tc,
                          a_spec=_sectioned_spec(dproj_d, tr, tc, 0, 2), res=dh_f, norm_bwd=(x3, d_norm_f, dx))

    dy_c = _mm(dx, c_w_out_f, "nt", T, W_C, D, BF16, "dy_c", 512, W_C, D).reshape(B, S, W_C)
    gw_c_out = _mm(y_c, dx, "tn", W_C, D, T, BF16, "gw_c_out", 1024, D, 512).reshape(N_DEV, W_C // N_DEV, D)
    (dy1, dgate_c, g_c_cw, g_c_cb, g_c_lg, g_c_lb), parts_d = _conv_bwd1(
        proj_c, y1_c, dy_c, KC, c_lg_f, c_lb_f, W_C, _Exchange([gw_d_in, gw_d_out], ["scatter"] * 2))
    dproj_c = _conv_bwd2(proj_c, dy1, dgate_c, c_cw_f[::-1], W_C).reshape(T, 3 * W_C)
    gw_c_in = _mm_grad_dev(h_c, dproj_c, "gw_c_in")
    dx, g_c_norm, _ = _mm_wT_dev(dproj_c, wg["c_w_in"], "dh_c", norm_bwd=(x2, c_norm_f, dx))

    dy_b = _mm(dx, b_w_out_f, "nt", T, W_B, D, BF16, "dy_b", 512, W_B, D)
    gw_b_out = _mm(y_b, dx, "tn", W_B, D, T, BF16, "gw_b_out", 1024, D, 512).reshape(N_DEV, W_B // N_DEV, D)
    dproj_b, g_wm, g_bs_t, g_b_lg, g_b_lb = _gmlp_bwd(proj_b, dy_b, wm, bs_t, b_lg_f, b_lb_f, W_B)
    g_b_w_s = jnp.tril(g_wm)
    g_b_b_s = g_bs_t[:, :G].T
    gw_b_in = _mm_grad_dev(h_b, dproj_b, "gw_b_in")
    dx, g_b_norm, _ = _mm_wT_dev(dproj_b, wg["b_w_in"], "dh_b", norm_bwd=(x1, b_norm_f, dx))

    dy_a = _mm(dx, a_w_out_f, "nt", T, W_A, D, BF16, "dy_a", 512, W_A, D).reshape(B, S, W_A)
    gw_a_out = _mm(y_a, dx, "tn", W_A, D, T, BF16, "gw_a_out", W_A, D, 512).reshape(N_DEV, W_A // N_DEV, D)
    small_full = [g_b_norm, g_b_lg, g_b_lb, g_b_b_s, g_c_norm, g_c_cw, g_c_cb, g_c_lg, g_c_lb,
                  g_d_norm, g_b_f, g_final]
    dproj_a, parts_s = _sb_bwd(
        proj_a, o_a, dy_a, W_A, SB_HEADS,
        _Exchange([gw_c_in, gw_c_out, gw_b_in, gw_b_out, gw_a_out, _pack(small_full), g_b_w_s.reshape(-1, LANES)],
                  ["scatter"] * 5 + ["gather"] * 2))
    gw_a_in = _mm_grad_dev(h_a, dproj_a, "gw_a_in")
    dx, g_a_norm, parts_a = _mm_wT_dev(dproj_a, wg["a_w_in"], "dh_a", exch=_Exchange([gw_a_in], ["scatter"]),
                                       norm_bwd=(x0, a_norm, dx))
    grad_x = dx.reshape(B, S, D)

    (parts_n,) = _Exchange([_pack([g_a_norm])], ["gather"]).run("exchange_last")
    big_parts = dict(a_w_in=parts_a[0], a_w_out=parts_s[4], b_w_in=parts_s[2], b_w_out=parts_s[3],
                     c_w_in=parts_s[0], c_w_out=parts_s[1], d_w_in=parts_d[0], d_w_out=parts_d[1])
    (s_b_norm, s_b_lg, s_b_lb, s_b_b_s, s_c_norm, s_c_cw, s_c_cb, s_c_lg, s_c_lb,
     s_d_norm, s_b_f, s_final) = _unpack(_sum_parts(parts_s[5], "sum_small"), [g.shape for g in small_full])
    (s_a_norm,) = _unpack(_sum_parts(parts_n, "sum_a_norm"), [g_a_norm.shape])

    weights = dict(a_norm=a_norm, a_w_in=a_w_in, a_w_out=a_w_out, b_norm=b_norm, b_w_in=b_w_in, b_v_ln_g=b_v_ln_g,
                   b_v_ln_b=b_v_ln_b, b_w_s=b_w_s, b_b_s=b_b_s, b_w_out=b_w_out, c_norm=c_norm, c_w_in=c_w_in,
                   c_conv_w=c_conv_w, c_conv_b=c_conv_b, c_ln_g=c_ln_g, c_ln_b=c_ln_b, c_w_out=c_w_out,
                   d_norm=d_norm, d_w_in=d_w_in, d_b_f=d_b_f, d_w_out=d_w_out, final_norm=final_norm)
    mom_m = dict(a_norm=m_a_norm, a_w_in=m_a_w_in, a_w_out=m_a_w_out, b_norm=m_b_norm, b_w_in=m_b_w_in,
                 b_v_ln_g=m_b_v_ln_g, b_v_ln_b=m_b_v_ln_b, b_w_s=m_b_w_s, b_b_s=m_b_b_s, b_w_out=m_b_w_out,
                 c_norm=m_c_norm, c_w_in=m_c_w_in, c_conv_w=m_c_conv_w, c_conv_b=m_c_conv_b, c_ln_g=m_c_ln_g,
                 c_ln_b=m_c_ln_b, c_w_out=m_c_w_out, d_norm=m_d_norm, d_w_in=m_d_w_in, d_b_f=m_d_b_f,
                 d_w_out=m_d_w_out, final_norm=m_final_norm)
    mom_v = dict(a_norm=v_a_norm, a_w_in=v_a_w_in, a_w_out=v_a_w_out, b_norm=v_b_norm, b_w_in=v_b_w_in,
                 b_v_ln_g=v_b_v_ln_g, b_v_ln_b=v_b_v_ln_b, b_w_s=v_b_w_s, b_b_s=v_b_b_s, b_w_out=v_b_w_out,
                 c_norm=v_c_norm, c_w_in=v_c_w_in, c_conv_w=v_c_conv_w, c_conv_b=v_c_conv_b, c_ln_g=v_c_ln_g,
                 c_ln_b=v_c_ln_b, c_w_out=v_c_w_out, d_norm=v_d_norm, d_w_in=v_d_w_in, d_b_f=v_d_b_f,
                 d_w_out=v_d_w_out, final_norm=v_final_norm)
    order = list(weights)
    grads, deltas, new_m, new_v = {}, {}, {}, {}

    for n in big_names:
        part = big_parts[n]
        shp = weights[n].shape
        R, C = shp[1], shp[2]
        res = _adamw(part, weights[n].reshape(R, C), mom_m[n].reshape(R, C), mom_v[n].reshape(R, C), "adamw_" + n)
        grads[n], deltas[n], new_m[n], new_v[n] = [r.reshape(shp) for r in res]

    res = _adamw(parts_s[6], b_w_s.reshape(-1, LANES), m_b_w_s.reshape(-1, LANES), v_b_w_s.reshape(-1, LANES),
                 "adamw_b_w_s")
    grads["b_w_s"], deltas["b_w_s"], new_m["b_w_s"], new_v["b_w_s"] = [r.reshape(b_w_s.shape) for r in res]

    small_g = dict(
        a_norm=s_a_norm, b_norm=_my_cols(s_b_norm, me), b_v_ln_g=_my_cols(s_b_lg, me),
        b_v_ln_b=_my_cols(s_b_lb, me), b_b_s=s_b_b_s[None], c_norm=_my_cols(s_c_norm, me),
        c_conv_w=_my_cols(s_c_cw, me)[None], c_conv_b=_my_cols(s_c_cb, me), c_ln_g=_my_cols(s_c_lg, me),
        c_ln_b=_my_cols(s_c_lb, me), d_norm=_my_cols(s_d_norm, me), d_b_f=s_b_f.reshape(1, H_D),
        final_norm=s_final.reshape(D))
    small_names = list(small_g)
    sg_p = _pack([small_g[n] for n in small_names])
    res = _adamw(sg_p[None], _pack([weights[n] for n in small_names]), _pack([mom_m[n] for n in small_names]),
                 _pack([mom_v[n] for n in small_names]), "adamw_small")
    shapes = [weights[n].shape for n in small_names]
    for dst, r in zip((grads, deltas, new_m, new_v), res):
        for n, val in zip(small_names, _unpack(r, shapes)):
            dst[n] = val

    return (loss, grad_x, *[grads[n] for n in order], *[deltas[n] for n in order],
            *[new_m[n] for n in order], *[new_v[n] for n in order])
```

```python
import functools
import math

import jax
import jax.numpy as jnp
from jax import lax
from jax.experimental import pallas as pl
from jax.experimental.pallas import tpu as pltpu

F32 = jnp.float32
BF16 = jnp.bfloat16

EPS = 1e-6
SB_HEADS = 16
CONV_HALO = 32
BLK = 128
ATT_TK = 256
FOX_TK = 512
ATT_TQ = 512
ATT_GP = 2
LANES = 128
N_DEV = 8
MESH_AXES = ("x", "y", "c")

ADAM_LR = 0.001
ADAM_B1 = 0.9
ADAM_B2 = 0.999
ADAM_EPS = 1e-08
ADAM_WD = 0.01
ADAM_STEP = 10

VMEM_LIMIT = 56 * 1024 * 1024
NEG_BIG = -1e30

_NN = (((1,), (0,)), ((), ()))
_NT = (((1,), (1,)), ((), ()))
_TN = (((0,), (0,)), ((), ()))


def _dot(a, b, dims=_NN):
    return lax.dot_general(a, b, dims, preferred_element_type=F32)


def _split_dot(x, m):
    hi = lax.bitcast_convert_type(lax.bitcast_convert_type(x, jnp.int32) & jnp.int32(-65536), F32)
    return _dot(hi.astype(BF16), m) + _dot((x - hi).astype(BF16), m)


def _split3_dot(x, m):
    hi = x.astype(BF16)
    r1 = x - hi.astype(F32)
    mid = r1.astype(BF16)
    lo = (r1 - mid.astype(F32)).astype(BF16)
    return _dot(hi, m) + _dot(mid, m) + _dot(lo, m)


def _params(sem=None):
    kw = dict(vmem_limit_bytes=VMEM_LIMIT)
    if sem is not None:
        kw["dimension_semantics"] = sem
    return pltpu.CompilerParams(**kw)


def _sigmoid(x):
    return jax.nn.sigmoid(x)


def _silu(x):
    return x * _sigmoid(x)


def _dsilu(x):
    s = _sigmoid(x)
    return s * (1.0 + x * (1.0 - s))


_GELU_C = math.sqrt(2.0 / math.pi)


def _gelu(x):
    return 0.5 * x * (1.0 + jnp.tanh(_GELU_C * (x + 0.044715 * x * x * x)))


def _dgelu(x):
    th = jnp.tanh(_GELU_C * (x + 0.044715 * x * x * x))
    return 0.5 * (1.0 + th) + 0.5 * x * (1.0 - th * th) * _GELU_C * (1.0 + 3.0 * 0.044715 * x * x)


def _mm(a, b, mode, M, N, K, out_dtype, name, tm, tn, tk, a_spec=None, b_spec=None, o_spec=None, out_shape=None,
        exch=None, res=None, norm_gain=None, norm_bwd=None):
    tm, tn, tk = min(tm, M), min(tn, N), min(tk, K)
    assert M % tm == 0 and N % tn == 0 and K % tk == 0, (name, M, N, K, tm, tn, tk)
    nk = K // tk
    assert norm_gain is None or (nk == 1 and tn == N and exch is None)
    dims = {"nn": _NN, "nt": _NT, "tn": _TN}[mode]
    if a_spec is None:
        a_spec = (pl.BlockSpec((tk, tm), lambda i, j, k: (k, i)) if mode == "tn"
                  else pl.BlockSpec((tm, tk), lambda i, j, k: (i, k)))
    if b_spec is None:
        b_spec = (pl.BlockSpec((tn, tk), lambda i, j, k: (j, k)) if mode == "nt"
                  else pl.BlockSpec((tk, tn), lambda i, j, k: (k, j)))
    if o_spec is None:
        o_spec = pl.BlockSpec((tm, tn), lambda i, j, k: (i, j))
    if out_shape is None:
        out_shape = (M, N)

    def body(a_ref, b_ref, *rest):
        res_ref = rest[0] if res is not None else None
        if nk == 1:
            d = _dot(a_ref[...].astype(BF16), b_ref[...].astype(BF16), dims)
            r = d if res_ref is None else res_ref[...].astype(F32) + d
            if norm_gain is None:
                rest[-1][...] = r.astype(rest[-1].dtype)
            else:
                g_ref, o_ref, h_ref = rest[-3:]
                o_ref[...] = r.astype(o_ref.dtype)
                scale = lax.rsqrt(jnp.mean(r * r, axis=-1, keepdims=True) + EPS)
                h_ref[...] = (r * scale * g_ref[...]).astype(BF16)
            return
        i = pl.program_id(0)
        k = pl.program_id(2)
        if norm_bwd is not None:
            x_ref, g_ref, dres_ref, o_ref, dg_ref, acc_ref = rest[-6:]

            @pl.when(jnp.logical_and(i == 0, k == 0))
            def _():
                dg_ref[...] = jnp.zeros_like(dg_ref)
        else:
            o_ref, acc_ref = rest[-2:]

        @pl.when(k == 0)
        def _():
            acc_ref[...] = jnp.zeros_like(acc_ref) if res_ref is None else res_ref[...].astype(F32)

        acc_ref[...] += _dot(a_ref[...].astype(BF16), b_ref[...].astype(BF16), dims)

        @pl.when(k == nk - 1)
        def _():
            if norm_bwd is None:
                o_ref[...] = acc_ref[...].astype(o_ref.dtype)
            else:
                dh = acc_ref[...]
                xv = x_ref[...]
                r = lax.rsqrt(jnp.mean(xv * xv, axis=-1, keepdims=True) + EPS)
                xh = xv * r
                dxh = dh * g_ref[...]
                o_ref[...] = dres_ref[...] + r * (dxh - xh * jnp.mean(dxh * xh, axis=-1, keepdims=True))
                dg_ref[...] += jnp.sum(dh * xh, axis=0, keepdims=True)

    in_specs, args = [a_spec, b_spec], (a, b)
    if res is not None:
        in_specs, args = in_specs + [o_spec], args + (res,)
    scratch = [pltpu.VMEM((tm, tn), F32)] if nk > 1 else []
    if norm_bwd is not None:
        assert nk > 1 and tn == N and norm_gain is None
        vec = pl.BlockSpec((1, N), lambda i, j, k: (0, 0))
        x_in, g_in, dres_in = norm_bwd
        (dx, dg), moved = _call_hosting(
            body, exch, name, (M // tm, 1, nk), in_specs + [o_spec, vec, o_spec], [o_spec, vec],
            [jax.ShapeDtypeStruct((M, N), F32), jax.ShapeDtypeStruct((1, N), F32)], scratch,
            args + (x_in, g_in, dres_in))
        return dx, dg, moved
    if norm_gain is not None:
        return pl.pallas_call(
            body, name=name, grid=(M // tm, N // tn, nk),
            in_specs=in_specs + [pl.BlockSpec((1, N), lambda i, j, k: (0, 0))], out_specs=[o_spec, o_spec],
            out_shape=[jax.ShapeDtypeStruct(out_shape, out_dtype), jax.ShapeDtypeStruct(out_shape, BF16)],
            compiler_params=_params(("parallel", "parallel", "arbitrary")),
        )(*args, norm_gain)
    if exch is None:
        return pl.pallas_call(
            body, name=name, grid=(M // tm, N // tn, nk),
            in_specs=in_specs, out_specs=o_spec,
            out_shape=jax.ShapeDtypeStruct(out_shape, out_dtype),
            scratch_shapes=scratch,
            compiler_params=_params(("parallel", "parallel", "arbitrary")),
        )(*args)
    (out,), moved = _call_hosting(
        body, exch, name, (M // tm, N // tn, nk), in_specs, [o_spec],
        [jax.ShapeDtypeStruct(out_shape, out_dtype)], scratch, args)
    return out, moved


def _mm_w_dev(a, w3, name, out_dtype=F32, tm=1024):
    M, K = a.shape
    n8 = w3.shape[2]
    tn = n8 if n8 <= 768 else 512
    per = n8 // tn
    b_spec = pl.BlockSpec((None, K, tn), lambda i, j, k: (j // per, 0, j % per))
    return _mm(a, w3, "nn", M, N_DEV * n8, K, out_dtype, name, tm, tn, K, b_spec=b_spec)


def _sectioned_spec(d4, t_rows, t_cols, rows_axis, cols_axis):
    _, _, S, W = d4.shape
    assert S % t_rows == 0 and W % t_cols == 0
    rb, cb = S // t_rows, W // t_cols

    def index(*g):
        r, c = g[rows_axis], g[cols_axis]
        return (r // rb, c // cb, r % rb, c % cb)

    return pl.BlockSpec((None, None, t_rows, t_cols), index)


def _mm_wT_dev(a, w3, name, out_dtype=F32, tm=1024, exch=None, norm_bwd=None):
    K, n8 = w3.shape[1], w3.shape[2]
    tk = n8 if n8 <= 768 else 512
    per = n8 // tk
    b_spec = pl.BlockSpec((None, K, tk), lambda i, j, k: (k // per, 0, k % per))
    if a.ndim == 4:
        M, N = a.shape[0] * a.shape[2], a.shape[1] * a.shape[3]
        tm = min(tm, a.shape[2])
        a_spec = _sectioned_spec(a, tm, tk, 0, 2)
    else:
        (M, N), a_spec = a.shape, None
    return _mm(a, w3, "nt", M, K, N, out_dtype, name, tm, K, tk, a_spec=a_spec, b_spec=b_spec, exch=exch,
               norm_bwd=norm_bwd)


def _mm_grad_dev(h, d, name, out_dtype=BF16):
    T, M = h.shape
    N = d.shape[1] * d.shape[3] if d.ndim == 4 else d.shape[1]
    n8 = N // N_DEV
    tn = n8 if n8 <= 768 else 512
    per = n8 // tn
    tm = min(M, 1024)
    o_spec = pl.BlockSpec((None, tm, tn), lambda i, j, k: (j // per, i, j % per))
    tk = min(1024, d.shape[2] if d.ndim == 4 else T)
    b_spec = _sectioned_spec(d, tk, tn, 2, 1) if d.ndim == 4 else None
    return _mm(h, d, "tn", M, N, T, out_dtype, name, tm, tn, tk, b_spec=b_spec, o_spec=o_spec,
               out_shape=(N_DEV, M, n8))


def _me():
    x, y, c = lax.axis_index("x"), lax.axis_index("y"), lax.axis_index("c")
    return x, y, c


def _peer(r):
    x, y, c = _me()
    px = 1 - x if (r >> 2) & 1 else x
    py = 1 - y if (r >> 1) & 1 else y
    pc = 1 - c if r & 1 else c
    return (px, py, pc), 4 * px + 2 * py + pc


class _Exchange:
    def __init__(self, arrays, kinds):
        self.arrays, self.kinds, self.n = list(arrays), list(kinds), len(arrays)
        self.out_shapes = [
            jax.ShapeDtypeStruct((N_DEV,) + a.shape if kind == "gather" else a.shape, a.dtype)
            for a, kind in zip(arrays, kinds)]
        self.specs = [pl.BlockSpec(memory_space=pl.ANY)] * self.n
        self.sems = [pltpu.SemaphoreType.DMA((self.n, N_DEV - 1)), pltpu.SemaphoreType.DMA((self.n, N_DEV - 1)),
                     pltpu.SemaphoreType.DMA((self.n,))]

    def _copies(self, ins, outs, sems, receiving):
        send_sems, recv_sems, local_sems = sems
        x, y, c = _me()
        me = 4 * x + 2 * y + c

        def src(k, pid):
            return ins[k] if self.kinds[k] == "gather" else ins[k].at[pid]

        local = [pltpu.make_async_copy(src(k, me), outs[k].at[me], local_sems.at[k]) for k in range(self.n)]
        remote = []
        for r in range(1, N_DEV):
            peer, pid = _peer(r)
            for k in range(self.n):
                remote.append(pltpu.make_async_remote_copy(
                    src_ref=src(k, pid), dst_ref=outs[k].at[pid if receiving else me],
                    send_sem=send_sems.at[k, r - 1], recv_sem=recv_sems.at[k, r - 1],
                    device_id=peer, device_id_type=pl.DeviceIdType.MESH))
        return local, remote

    def start(self, ins, outs, sems):
        local, remote = self._copies(ins, outs, sems, False)
        for cp in local + remote:
            cp.start()

    def wait(self, ins, outs, sems):
        local, remote = self._copies(ins, outs, sems, True)
        for cp in remote:
            cp.wait_recv()
        for cp in remote:
            cp.wait_send()
        for cp in local:
            cp.wait()

    def run(self, name):
        n = self.n

        def body(*refs):
            ins, outs, sems = refs[:n], refs[n:2 * n], refs[2 * n:]
            self.start(ins, outs, sems)
            self.wait(ins, outs, sems)

        return pl.pallas_call(
            body, name=name, in_specs=self.specs, out_specs=self.specs, out_shape=self.out_shapes,
            scratch_shapes=self.sems,
        )(*self.arrays)


class _GatherViaSibling(_Exchange):
    ICI = (2, 4, 6)

    def __init__(self, arrays):
        super().__init__(arrays, ["gather"] * len(arrays))

    def _copy(self, ins, outs, sems, k, column, block, to, from_input=False):
        return pltpu.make_async_remote_copy(
            src_ref=ins[k] if from_input else outs[k].at[block], dst_ref=outs[k].at[block],
            send_sem=sems[0].at[k, column], recv_sem=sems[1].at[k, column],
            device_id=to, device_id_type=pl.DeviceIdType.MESH)

    def start(self, ins, outs, sems):
        x, y, c = _me()
        me = 4 * x + 2 * y + c
        for k in range(self.n):
            pltpu.make_async_copy(ins[k], outs[k].at[me], sems[2].at[k]).start()
            self._copy(ins, outs, sems, k, 0, me, _peer(1)[0], True).start()
            for j, r in enumerate(self.ICI):
                self._copy(ins, outs, sems, k, 1 + j, me, _peer(r)[0], True).start()

    def wait(self, ins, outs, sems):
        x, y, c = _me()
        me = 4 * x + 2 * y + c
        sibling, sibling_id = _peer(1)
        for j, r in enumerate(self.ICI):
            peer, pid = _peer(r)
            for k in range(self.n):
                self._copy(ins, outs, sems, k, 1 + j, pid, peer).wait_recv()
                self._copy(ins, outs, sems, k, 4 + j, pid, sibling).start()
        for k in range(self.n):
            self._copy(ins, outs, sems, k, 0, sibling_id, sibling).wait_recv()
            for j, r in enumerate(self.ICI):
                self._copy(ins, outs, sems, k, 4 + j, _peer(r ^ 1)[1], sibling).wait_recv()
            for column in range(N_DEV - 1):
                self._copy(ins, outs, sems, k, column, me, sibling).wait_send()
            pltpu.make_async_copy(ins[k], outs[k].at[me], sems[2].at[k]).wait()


def _call_hosting(body, exch, name, grid, in_specs, out_specs, out_shape, scratch_shapes, args):
    if exch is None:
        res = pl.pallas_call(
            body, name=name, grid=grid, in_specs=list(in_specs), out_specs=list(out_specs),
            out_shape=list(out_shape), scratch_shapes=list(scratch_shapes),
            compiler_params=_params(("arbitrary",) * len(grid)))(*args)
        return res, []
    n_in, n_out, n_scr, nc = len(in_specs), len(out_specs), len(scratch_shapes), exch.n

    def full_body(*refs):
        ins, refs = refs[:n_in], refs[n_in:]
        cins, refs = refs[:nc], refs[nc:]
        outs, refs = refs[:n_out], refs[n_out:]
        couts, refs = refs[:nc], refs[nc:]
        scr, sems = refs[:n_scr], refs[n_scr:]
        ids = [pl.program_id(a) for a in range(len(grid))]
        first = functools.reduce(jnp.logical_and, [i == 0 for i in ids])
        last = functools.reduce(jnp.logical_and, [i == g - 1 for i, g in zip(ids, grid)])

        @pl.when(first)
        def _():
            exch.start(cins, couts, sems)

        body(*ins, *outs, *scr)

        @pl.when(last)
        def _():
            exch.wait(cins, couts, sems)

    res = pl.pallas_call(
        full_body, name=name, grid=grid,
        in_specs=list(in_specs) + exch.specs, out_specs=list(out_specs) + exch.specs,
        out_shape=list(out_shape) + exch.out_shapes,
        scratch_shapes=list(scratch_shapes) + exch.sems,
        compiler_params=_params(("arbitrary",) * len(grid)),
    )(*args, *exch.arrays)
    return res[:n_out], res[n_out:]


def _rmsnorm_fwd(x, g, name):
    T, D = x.shape
    tr = min(256, T)

    def body(x_ref, g_ref, h_ref):
        xv = x_ref[...]
        r = lax.rsqrt(jnp.mean(xv * xv, axis=-1, keepdims=True) + EPS)
        h_ref[...] = (xv * r * g_ref[...]).astype(BF16)

    return pl.pallas_call(
        body, name=name, grid=(T // tr,),
        in_specs=[pl.BlockSpec((tr, D), lambda i: (i, 0)), pl.BlockSpec((1, D), lambda i: (0, 0))],
        out_specs=pl.BlockSpec((tr, D), lambda i: (i, 0)),
        out_shape=jax.ShapeDtypeStruct((T, D), BF16),
        compiler_params=_params(("parallel",)),
    )(x, g)


def _loss_head(x, g, target):
    T, D = x.shape
    tr = min(256, T)

    def body(x_ref, g_ref, t_ref, loss_ref, dx_ref, dg_ref):
        i = pl.program_id(0)
        xv = x_ref[...]
        gv = g_ref[...]
        r = lax.rsqrt(jnp.mean(xv * xv, axis=-1, keepdims=True) + EPS)
        xh = xv * r
        diff = xh * gv - t_ref[...]
        dy = diff * (1.0 / D)
        dxh = dy * gv
        dx_ref[...] = r * (dxh - xh * jnp.mean(dxh * xh, axis=-1, keepdims=True))

        @pl.when(i == 0)
        def _():
            dg_ref[...] = jnp.zeros_like(dg_ref)
            loss_ref[...] = jnp.zeros_like(loss_ref)

        dg_ref[...] += jnp.sum(dy * xh, axis=0, keepdims=True)
        part = jnp.sum(jnp.sum(diff * diff, axis=1, keepdims=True), axis=0, keepdims=True)
        loss_ref[...] += (0.5 / D) * part

    row = pl.BlockSpec((tr, D), lambda i: (i, 0))
    vec = pl.BlockSpec((1, D), lambda i: (0, 0))
    return pl.pallas_call(
        body, name="loss_head", grid=(T // tr,),
        in_specs=[row, vec, row],
        out_specs=[pl.BlockSpec((1, 1), lambda i: (0, 0)), row, vec],
        out_shape=[jax.ShapeDtypeStruct((1, 1), F32), jax.ShapeDtypeStruct((T, D), F32),
                   jax.ShapeDtypeStruct((1, D), F32)],
        compiler_params=_params(("arbitrary",)),
    )(x, g, target)


ELEMS_PER_STEP = 1 << 20


def _row_tile(R, per_row):
    best = None
    for tr in range(8, R + 1, 8):
        if R % tr == 0 and tr * per_row <= ELEMS_PER_STEP:
            best = tr
    return best if best is not None else R


def _adamw(parts, w, m, v, name):
    P, R, C = parts.shape
    tr = _row_tile(R, P * C)

    def body(p_ref, w_ref, m_ref, v_ref, g_out, d_out, m_out, v_out):
        g = p_ref[0].astype(F32)
        for p in range(1, P):
            g = g + p_ref[p].astype(F32)
        wv = w_ref[...]
        mn = ADAM_B1 * m_ref[...] + (1.0 - ADAM_B1) * g
        vn = ADAM_B2 * v_ref[...] + (1.0 - ADAM_B2) * (g * g)
        m_hat = mn / (1.0 - ADAM_B1 ** ADAM_STEP)
        v_hat = vn / (1.0 - ADAM_B2 ** ADAM_STEP)
        g_out[...] = g
        d_out[...] = -ADAM_LR * (m_hat / (jnp.sqrt(v_hat) + ADAM_EPS) + ADAM_WD * wv)
        m_out[...] = mn
        v_out[...] = vn

    row = pl.BlockSpec((tr, C), lambda i: (i, 0))
    return pl.pallas_call(
        body, name=name, grid=(R // tr,),
        in_specs=[pl.BlockSpec((P, tr, C), lambda i: (0, i, 0)), row, row, row],
        out_specs=[row, row, row, row],
        out_shape=[jax.ShapeDtypeStruct((R, C), F32)] * 4,
        compiler_params=_params(("parallel",)),
    )(parts, w, m, v)


def _sum_parts(parts, name):
    P, R, C = parts.shape
    tr = _row_tile(R, P * C)

    def body(p_ref, o_ref):
        g = p_ref[0]
        for p in range(1, P):
            g = g + p_ref[p]
        o_ref[...] = g

    return pl.pallas_call(
        body, name=name, grid=(R // tr,),
        in_specs=[pl.BlockSpec((P, tr, C), lambda i: (0, i, 0))],
        out_specs=pl.BlockSpec((tr, C), lambda i: (i, 0)),
        out_shape=jax.ShapeDtypeStruct((R, C), F32),
        compiler_params=_params(("parallel",)),
    )(parts)


def _lane_head(Dh):
    assert Dh & (Dh - 1) == 0 and Dh <= LANES
    return lax.shift_right_logical(lax.broadcasted_iota(jnp.int32, (1, LANES), 1), Dh.bit_length() - 1)


def _stack_heads(x, lane_head, hpb):
    return jnp.concatenate([jnp.where(lane_head == h, x, 0.0) for h in range(hpb)], axis=0)


def _unstack_heads(acc, lane_head, hpb):
    TQ = acc.shape[0] // hpb
    out = acc[0:TQ]
    for h in range(1, hpb):
        out = jnp.where(lane_head == h, acc[h * TQ:(h + 1) * TQ], out)
    return out


def _live_rows(x, r0, hpb):
    if r0 == 0:
        return x
    TQ = x.shape[0] // hpb
    return jnp.concatenate([x[h * TQ + r0:(h + 1) * TQ] for h in range(hpb)], axis=0)


def _put_rows(full, part, r0, hpb):
    if r0 == 0:
        return part
    TQ = full.shape[0] // hpb
    n = TQ - r0
    return jnp.concatenate(
        [blk for h in range(hpb) for blk in (full[h * TQ:h * TQ + r0], part[h * n:(h + 1) * n])], axis=0)


def _first_live_row(m, TQ, TK):
    return max(0, TQ - (m + 1) * TK)


def _key_tile(S, tk=None):
    tk = ATT_TK if tk is None else tk
    return tk if S % tk == 0 else BLK


def _query_tile(S):
    return ATT_TQ if S % ATT_TQ == 0 else BLK


def _lane_groups(P):
    return ATT_GP if P % ATT_GP == 0 else 1


def _lanes(u):
    return slice(u * LANES, (u + 1) * LANES)


def _causal_iotas(RS, TK, TQ, r0=0):
    n = TQ - r0
    assert n & (n - 1) == 0 and (TK % TQ == 0 or TQ % TK == 0)
    rows = RS // TQ * n
    trow = jnp.bitwise_and(lax.broadcasted_iota(jnp.int32, (rows, TK), 0), n - 1) + r0
    col = lax.broadcasted_iota(jnp.int32, (rows, TK), 1)
    return trow, col


def _tri(TK, op):
    r = lax.broadcasted_iota(jnp.int32, (TK, TK), 0)
    c = lax.broadcasted_iota(jnp.int32, (TK, TK), 1)
    return op(r, c).astype(BF16)


def _logsig_parts(z):
    lb = jnp.minimum(z, 0.0) - jnp.log(1.0 + jnp.exp(-jnp.abs(z)))
    return lb, lb - z


def _sb_fwd(proj3, W, heads, exch):
    B, S, _ = proj3.shape
    Dh = W // heads
    hpb = LANES // Dh
    P, TQ = W // LANES, _query_tile(S)
    NQ = S // TQ
    scale = 1.0 / math.sqrt(Dh)

    TK = _key_tile(S)
    RS = hpb * TQ
    NM = max(1, TQ // TK)
    GP = _lane_groups(P)
    PG = P // GP

    def body(q_ref, k_ref, v_ref, g_ref, o_ref, y_ref):
        i = pl.program_id(2)
        lane_head = _lane_head(Dh)
        msuf = _tri(TK, lambda r, c: r > c)
        qs = [(_stack_heads(q_ref[:, _lanes(u)], lane_head, hpb) * scale).astype(BF16) for u in range(GP)]
        nt = (i * TQ + TQ - 2) // TK + 1

        def tile(jt, carry, masked, r0=0):
            off = pl.multiple_of(jt * TK, TK)
            if masked:
                trow, col = _causal_iotas(RS, TK, TQ, r0)
                msk = col + (jt * TK - i * TQ) < trow
            out = []
            for u, (rem_all, acc_all) in enumerate(carry):
                rem, acc = _live_rows(rem_all, r0, hpb), _live_rows(acc_all, r0, hpb)
                kj = k_ref[pl.ds(off, TK), _lanes(u)].astype(BF16)
                vj = v_ref[pl.ds(off, TK), _lanes(u)].astype(BF16)
                lb, lr = _logsig_parts(_dot(_live_rows(qs[u], r0, hpb), kj, _NT))
                if masked:
                    lr = jnp.where(msk, lr, 0.0)
                w = jnp.exp(lb + _split_dot(lr, msuf) + rem)
                if masked:
                    w = jnp.where(msk, w, 0.0)
                out.append((_put_rows(rem_all, rem + jnp.sum(lr, axis=1, keepdims=True), r0, hpb),
                            _put_rows(acc_all, acc + _dot(w.astype(BF16), vj), r0, hpb)))
            return tuple(out)

        zero = (jnp.zeros((RS, 1), F32), jnp.zeros((RS, LANES), F32))
        carry = (zero,) * GP
        for m in range(NM):
            carry = tile(nt - 1 - m, carry, True, _first_live_row(m, TQ, TK))
        carry = lax.fori_loop(NM, nt, lambda jj, c: tile(nt - 1 - jj, c, False), carry)
        for u in range(GP):
            o = _unstack_heads(carry[u][1], lane_head, hpb)
            o_ref[:, _lanes(u)] = o
            y_ref[:, _lanes(u)] = (o * _silu(g_ref[:, _lanes(u)])).astype(BF16)

    LW = GP * LANES
    blk = lambda sec: pl.BlockSpec((None, TQ, LW), lambda b, p, i: (b, i, sec * PG + p))
    full = lambda sec: pl.BlockSpec((None, S, LW), lambda b, p, i: (b, 0, sec * PG + p))
    out = pl.BlockSpec((None, TQ, LW), lambda b, p, i: (b, i, p))
    return _call_hosting(
        body, exch, "sb_fwd", (B, PG, NQ), [blk(0), full(1), full(2), blk(3)], [out, out],
        [jax.ShapeDtypeStruct((B, S, W), F32), jax.ShapeDtypeStruct((B, S, W), BF16)], [],
        (proj3, proj3, proj3, proj3))


def _sb_bwd(proj3, o, dy, W, heads, exch):
    B, S, _ = proj3.shape
    Dh = W // heads
    hpb = LANES // Dh
    P, TQ = W // LANES, _query_tile(S)
    NQ = S // TQ
    scale = 1.0 / math.sqrt(Dh)

    TK = _key_tile(S)
    RS = hpb * TQ
    NM = max(1, TQ // TK)

    def body(q_ref, k_ref, v_ref, g_ref, o_ref, dy_ref, dp_ref, dk_ref, dv_ref, u_ref, sig_ref, es_ref):
        i = pl.program_id(2)
        rows = pl.ds(pl.multiple_of(i * TQ, TQ), TQ)

        @pl.when(i == 0)
        def _():
            dk_ref[...] = jnp.zeros_like(dk_ref)
            dv_ref[...] = jnp.zeros_like(dv_ref)

        lane_head = _lane_head(Dh)
        msuf = _tri(TK, lambda r, c: r > c)
        mpre = _tri(TK, lambda r, c: r < c)
        g = g_ref[...]
        dyv = dy_ref[...].astype(F32)
        dp_ref[3, rows, :] = (dyv * o_ref[...] * _dsilu(g)).astype(dp_ref.dtype)
        qs = (_stack_heads(q_ref[...], lane_head, hpb) * scale).astype(BF16)
        dos = _stack_heads(dyv * _silu(g), lane_head, hpb).astype(BF16)
        nt = (i * TQ + TQ - 2) // TK + 1

        def weights(jt, rem_all, masked, r0=0):
            off = pl.multiple_of(jt * TK, TK)
            kj = k_ref[pl.ds(off, TK), :].astype(BF16)
            vj = v_ref[pl.ds(off, TK), :].astype(BF16)
            dos_l = _live_rows(dos, r0, hpb)
            lb, lr = _logsig_parts(_dot(_live_rows(qs, r0, hpb), kj, _NT))
            if masked:
                trow, col = _causal_iotas(RS, TK, TQ, r0)
                msk = col + (jt * TK - i * TQ) < trow
                lr = jnp.where(msk, lr, 0.0)
            w = jnp.exp(lb + _split_dot(lr, msuf) + _live_rows(rem_all, r0, hpb))
            if masked:
                w = jnp.where(msk, w, 0.0)
            e = w * _dot(dos_l, vj, _NT)
            sig = jnp.exp(lb)
            u = e - sig * (e + _split_dot(e, mpre))
            if masked:
                u = jnp.where(msk, u, 0.0)
                sig = jnp.where(msk, sig, 0.0)
            n = TQ - r0
            for h in range(hpb):
                u_ref[jt, h * TQ + r0:(h + 1) * TQ, :] = u[h * n:(h + 1) * n]
                sig_ref[jt, h * TQ + r0:(h + 1) * TQ, :] = sig[h * n:(h + 1) * n]
            es_ref[jt] = _put_rows(jnp.zeros((RS, 1), F32), jnp.sum(e, axis=1, keepdims=True), r0, hpb)
            dv_ref[pl.ds(off, TK), :] += _dot(w.astype(BF16), dos_l, _TN)
            return _put_rows(rem_all, _live_rows(rem_all, r0, hpb) + jnp.sum(lr, axis=1, keepdims=True), r0, hpb)

        rem = jnp.zeros((RS, 1), F32)
        for m in range(NM):
            rem = weights(nt - 1 - m, rem, True, _first_live_row(m, TQ, TK))
        lax.fori_loop(NM, nt, lambda jj, r: weights(nt - 1 - jj, r, False), rem)

        def grads(jt, carry, r0=0):
            pre, acc = carry
            off = pl.multiple_of(jt * TK, TK)
            kj = k_ref[pl.ds(off, TK), :].astype(BF16)
            if r0 == 0:
                u, sig = u_ref[jt], sig_ref[jt]
            else:
                u = jnp.concatenate([u_ref[jt, h * TQ + r0:(h + 1) * TQ, :] for h in range(hpb)], axis=0)
                sig = jnp.concatenate([sig_ref[jt, h * TQ + r0:(h + 1) * TQ, :] for h in range(hpb)], axis=0)
            dz = (u - _live_rows(pre, r0, hpb) * sig).astype(BF16)
            dk_ref[pl.ds(off, TK), :] += _dot(dz, _live_rows(qs, r0, hpb), _TN)
            return pre + es_ref[jt], _put_rows(acc, _live_rows(acc, r0, hpb) + _dot(dz, kj), r0, hpb)

        carry = lax.fori_loop(0, nt - NM, grads, (jnp.zeros((RS, 1), F32), jnp.zeros((RS, LANES), F32)))
        for m in reversed(range(NM)):
            carry = grads(nt - 1 - m, carry, _first_live_row(m, TQ, TK))
        _, acc = carry
        dp_ref[0, rows, :] = (_unstack_heads(acc, lane_head, hpb) * scale).astype(dp_ref.dtype)

        @pl.when(i == NQ - 1)
        def _():
            dp_ref[1] = dk_ref[...].astype(dp_ref.dtype)
            dp_ref[2] = dv_ref[...].astype(dp_ref.dtype)

    blk = lambda sec: pl.BlockSpec((None, TQ, LANES), lambda b, p, i: (b, i, sec * P + p))
    full = lambda sec: pl.BlockSpec((None, S, LANES), lambda b, p, i: (b, 0, sec * P + p))
    one = pl.BlockSpec((None, TQ, LANES), lambda b, p, i: (b, i, p))
    (dproj,), moved = _call_hosting(
        body, exch, "sb_bwd", (B, P, NQ), [blk(0), full(1), full(2), blk(3), one, one],
        [pl.BlockSpec((None, 4, S, LANES), lambda b, p, i: (b, 0, 0, p))],
        [jax.ShapeDtypeStruct((B, 4, S, W), BF16)],
        [pltpu.VMEM((S, LANES), F32), pltpu.VMEM((S, LANES), F32),
         pltpu.VMEM((S // TK, RS, TK), F32), pltpu.VMEM((S // TK, RS, TK), F32), pltpu.VMEM((S // TK, RS, 1), F32)],
        (proj3, proj3, proj3, proj3, o, dy))
    return dproj, moved


def _fox_gate_fwd(f_t, b_f):
    B, H, S = f_t.shape

    def body(f_ref, b_ref, c_ref):
        row = lax.broadcasted_iota(jnp.int32, (BLK, BLK), 0)
        col = lax.broadcasted_iota(jnp.int32, (BLK, BLK), 1)
        mpre = (row <= col).astype(BF16)
        carry = jnp.zeros((H, 1), F32)
        for n in range(S // BLK):
            sl = pl.ds(n * BLK, BLK)
            lf, _ = _logsig_parts(f_ref[:, sl] + b_ref[...])
            c_ref[:, sl] = _split3_dot(lf, mpre) + carry
            carry = carry + jnp.sum(lf, axis=1, keepdims=True)

    spec = pl.BlockSpec((None, H, S), lambda b: (b, 0, 0))
    return pl.pallas_call(
        body, name="fox_gate_fwd", grid=(B,),
        in_specs=[spec, pl.BlockSpec((H, 1), lambda b: (0, 0))], out_specs=spec,
        out_shape=jax.ShapeDtypeStruct((B, H, S), F32),
        compiler_params=_params(("parallel",)),
    )(f_t, b_f)


def _fox_gate_bwd(dcum_t, f_t, b_f):
    B, H, S = f_t.shape

    def body(d_ref, f_ref, b_ref, df_ref, db_ref):
        b = pl.program_id(0)

        @pl.when(b == 0)
        def _():
            db_ref[...] = jnp.zeros_like(db_ref)

        row = lax.broadcasted_iota(jnp.int32, (BLK, BLK), 0)
        col = lax.broadcasted_iota(jnp.int32, (BLK, BLK), 1)
        msuf = (row >= col).astype(BF16)
        carry = jnp.zeros((H, 1), F32)
        dbacc = jnp.zeros((H, 1), F32)
        for n in reversed(range(S // BLK)):
            sl = pl.ds(n * BLK, BLK)
            dv = d_ref[:, sl]
            dlf = _split3_dot(dv, msuf) + carry
            carry = carry + jnp.sum(dv, axis=1, keepdims=True)
            df = dlf * _sigmoid(-(f_ref[:, sl] + b_ref[...]))
            df_ref[:, sl] = df
            dbacc = dbacc + jnp.sum(df, axis=1, keepdims=True)
        db_ref[...] += dbacc

    spec = pl.BlockSpec((None, H, S), lambda b: (b, 0, 0))
    vec = pl.BlockSpec((H, 1), lambda b: (0, 0))
    return pl.pallas_call(
        body, name="fox_gate_bwd", grid=(B,),
        in_specs=[spec, spec, vec], out_specs=[spec, vec],
        out_shape=[jax.ShapeDtypeStruct((B, H, S), F32), jax.ShapeDtypeStruct((H, 1), F32)],
        compiler_params=_params(("arbitrary",)),
    )(dcum_t, f_t, b_f)


def _pick_col(block, idx, lane_iota):
    return jnp.sum(jnp.where(lane_iota == idx, block, 0.0), axis=1, keepdims=True)


def _pick_row(block, idx, sub_iota):
    return jnp.sum(jnp.where(sub_iota == idx, block, 0.0), axis=0, keepdims=True)


def _fox_fwd(proj3, cum_t, W, heads):
    B, S, _ = proj3.shape
    H = heads
    Dh = W // heads
    hpb = LANES // Dh
    P, TQ = W // LANES, _query_tile(S)
    NQ = S // TQ
    scale = 1.0 / math.sqrt(Dh)

    TK = _key_tile(S, FOX_TK)
    RS = hpb * TQ
    NM = max(1, TQ // TK)

    def body(q_ref, k_ref, v_ref, g_ref, ct_ref, o_ref, y_ref, lse_ref):
        p = pl.program_id(1)
        i = pl.program_id(2)
        lane_head = _lane_head(Dh)
        sub_h = lax.broadcasted_iota(jnp.int32, (H, 1), 0)
        qs = (_stack_heads(q_ref[...], lane_head, hpb) * scale).astype(BF16)
        nt = (i * TQ + TQ - 1) // TK + 1

        def tile(jt, carry, masked, r0=0):
            mx, l, acc = carry
            n = TQ - r0
            off = pl.multiple_of(jt * TK, TK)
            kj = k_ref[pl.ds(off, TK), :].astype(BF16)
            vj = v_ref[pl.ds(off, TK), :].astype(BF16)
            ctb = ct_ref[:, pl.ds(off, TK)]
            z = _dot(_live_rows(qs, r0, hpb), kj, _NT)
            s = jnp.concatenate([z[h * n:(h + 1) * n] - _pick_row(ctb, p * hpb + h, sub_h) for h in range(hpb)],
                                axis=0)
            if masked:
                trow, col = _causal_iotas(RS, TK, TQ, r0)
                s = jnp.where(col + (jt * TK - i * TQ) <= trow, s, NEG_BIG)
            mx2 = jnp.maximum(mx, jnp.max(s, axis=1, keepdims=True))
            pe = jnp.exp(s - mx2)
            alpha = jnp.exp(mx - mx2)
            return (mx2, alpha * l + jnp.sum(pe, axis=1, keepdims=True), alpha * acc + _dot(pe.astype(BF16), vj))

        carry = lax.fori_loop(
            0, nt - NM, lambda jt, c: tile(jt, c, False),
            (jnp.full((RS, 1), NEG_BIG, F32), jnp.zeros((RS, 1), F32), jnp.zeros((RS, LANES), F32)))
        for m in reversed(range(NM)):
            carry = tile(nt - 1 - m, carry, True)
        mx, l, acc = carry
        o = _unstack_heads(acc / l, lane_head, hpb)
        o_ref[...] = o
        lse_ref[...] = _unstack_heads(jnp.broadcast_to(mx + jnp.log(l), (RS, LANES)), lane_head, hpb)
        y_ref[...] = (o * _silu(g_ref[...])).astype(BF16)

    blk = lambda sec: pl.BlockSpec((None, TQ, LANES), lambda b, p, i: (b, i, sec * P + p))
    full = lambda sec: pl.BlockSpec((None, S, LANES), lambda b, p, i: (b, 0, sec * P + p))
    out = pl.BlockSpec((None, TQ, LANES), lambda b, p, i: (b, i, p))
    return pl.pallas_call(
        body, name="fox_fwd", grid=(B, P, NQ),
        in_specs=[blk(0), full(1), full(2), blk(3),
                  pl.BlockSpec((None, H, S), lambda b, p, i: (b, 0, 0))],
        out_specs=[out, out, out],
        out_shape=[jax.ShapeDtypeStruct((B, S, W), F32), jax.ShapeDtypeStruct((B, S, W), BF16),
                   jax.ShapeDtypeStruct((B, S, W), F32)],
        compiler_params=_params(("parallel", "parallel", "arbitrary")),
    )(proj3, proj3, proj3, proj3, cum_t)


def _fox_bwd(proj3, cum_t, o, lse, dy, W, heads):
    B, S, _ = proj3.shape
    H = heads
    Dh = W // heads
    hpb = LANES // Dh
    P, TQ = W // LANES, _query_tile(S)
    NQ = S // TQ
    scale = 1.0 / math.sqrt(Dh)

    TK = _key_tile(S)
    RS = hpb * TQ
    NM = max(1, TQ // TK)

    def body(q_ref, k_ref, v_ref, g_ref, ct_ref, o_ref, lse_ref, dy_ref,
             dpj_ref, dc_ref, dk_ref, dv_ref, p_scr, dp_scr):
        p = pl.program_id(1)
        i = pl.program_id(2)
        rows = pl.ds(pl.multiple_of(i * TQ, TQ), TQ)

        @pl.when(i == 0)
        def _():
            dk_ref[...] = jnp.zeros_like(dk_ref)
            dv_ref[...] = jnp.zeros_like(dv_ref)
            dc_ref[...] = jnp.zeros_like(dc_ref)

        lane_head = _lane_head(Dh)
        sub_h = lax.broadcasted_iota(jnp.int32, (H, 1), 0)
        lane = lax.broadcasted_iota(jnp.int32, (1, LANES), 1)
        g = g_ref[...]
        lsev = lse_ref[...]
        dyv = dy_ref[...].astype(F32)
        dpj_ref[3, rows, :] = (dyv * o_ref[...] * _dsilu(g)).astype(dpj_ref.dtype)
        qs = (_stack_heads(q_ref[...], lane_head, hpb) * scale).astype(BF16)
        dos = _stack_heads(dyv * _silu(g), lane_head, hpb).astype(BF16)
        neg_lse = -jnp.concatenate([_pick_col(lsev, h * Dh, lane) for h in range(hpb)], axis=0)
        nt = (i * TQ + TQ - 1) // TK + 1

        def probs(jt, dsum, masked, r0=0):
            n = TQ - r0
            off = pl.multiple_of(jt * TK, TK)
            kj = k_ref[pl.ds(off, TK), :].astype(BF16)
            vj = v_ref[pl.ds(off, TK), :].astype(BF16)
            ctb = ct_ref[:, pl.ds(off, TK)]
            dos_l = _live_rows(dos, r0, hpb)
            z = _dot(_live_rows(qs, r0, hpb), kj, _NT) + _live_rows(neg_lse, r0, hpb)
            s = jnp.concatenate([z[h * n:(h + 1) * n] - _pick_row(ctb, p * hpb + h, sub_h) for h in range(hpb)],
                                axis=0)
            pr = jnp.exp(s)
            if masked:
                trow, col = _causal_iotas(RS, TK, TQ, r0)
                pr = jnp.where(col + (jt * TK - i * TQ) <= trow, pr, 0.0)
            dp = _dot(dos_l, vj, _NT)
            p_scr[jt] = _put_rows(jnp.zeros((RS, TK), F32), pr, r0, hpb)
            dp_scr[jt] = _put_rows(jnp.zeros((RS, TK), F32), dp, r0, hpb)
            dv_ref[pl.ds(off, TK), :] += _dot(pr.astype(BF16), dos_l, _TN)
            return _put_rows(dsum, _live_rows(dsum, r0, hpb) + jnp.sum(pr * dp, axis=1, keepdims=True), r0, hpb)

        dsum = lax.fori_loop(0, nt - NM, lambda jt, d: probs(jt, d, False), jnp.zeros((RS, 1), F32))
        for m in reversed(range(NM)):
            dsum = probs(nt - 1 - m, dsum, True)

        def grads(jt, acc, r0=0):
            n = TQ - r0
            off = pl.multiple_of(jt * TK, TK)
            kj = k_ref[pl.ds(off, TK), :].astype(BF16)
            if r0 == 0:
                pr, dp = p_scr[jt], dp_scr[jt]
            else:
                pr = jnp.concatenate([p_scr[jt, h * TQ + r0:(h + 1) * TQ, :] for h in range(hpb)], axis=0)
                dp = jnp.concatenate([dp_scr[jt, h * TQ + r0:(h + 1) * TQ, :] for h in range(hpb)], axis=0)
            ds = pr * (dp - _live_rows(dsum, r0, hpb))
            for h in range(hpb):
                dc_ref[h:h + 1, pl.ds(off, TK)] -= jnp.sum(ds[h * n:(h + 1) * n], axis=0, keepdims=True)
            dsb = ds.astype(BF16)
            dk_ref[pl.ds(off, TK), :] += _dot(dsb, _live_rows(qs, r0, hpb), _TN)
            return _put_rows(acc, _live_rows(acc, r0, hpb) + _dot(dsb, kj), r0, hpb)

        acc = lax.fori_loop(0, nt - NM, grads, jnp.zeros((RS, LANES), F32))
        for m in reversed(range(NM)):
            acc = grads(nt - 1 - m, acc, _first_live_row(m, TQ, TK))
        dpj_ref[0, rows, :] = (_unstack_heads(acc, lane_head, hpb) * scale).astype(dpj_ref.dtype)

        @pl.when(i == NQ - 1)
        def _():
            dpj_ref[1] = dk_ref[...].astype(dpj_ref.dtype)
            dpj_ref[2] = dv_ref[...].astype(dpj_ref.dtype)

    blk = lambda sec: pl.BlockSpec((None, TQ, LANES), lambda b, p, i: (b, i, sec * P + p))
    full = lambda sec: pl.BlockSpec((None, S, LANES), lambda b, p, i: (b, 0, sec * P + p))
    one = pl.BlockSpec((None, TQ, LANES), lambda b, p, i: (b, i, p))
    return pl.pallas_call(
        body, name="fox_bwd", grid=(B, P, NQ),
        in_specs=[blk(0), full(1), full(2), blk(3),
                  pl.BlockSpec((None, H, S), lambda b, p, i: (b, 0, 0)),
                  one, one, one],
        out_specs=[pl.BlockSpec((None, 4, S, LANES), lambda b, p, i: (b, 0, 0, p)),
                   pl.BlockSpec((None, None, hpb, S), lambda b, p, i: (b, p, 0, 0))],
        out_shape=[jax.ShapeDtypeStruct((B, 4, S, W), BF16), jax.ShapeDtypeStruct((B, P, hpb, S), F32)],
        scratch_shapes=[pltpu.VMEM((S, LANES), F32), pltpu.VMEM((S, LANES), F32),
                        pltpu.VMEM((S // TK, RS, TK), F32), pltpu.VMEM((S // TK, RS, TK), F32)],
        compiler_params=_params(("parallel", "parallel", "arbitrary")),
    )(proj3, proj3, proj3, proj3, cum_t, o, lse, dy)


def _layernorm_rows(v, gamma, beta):
    mu = jnp.mean(v, axis=-1, keepdims=True)
    xc = v - mu
    rstd = lax.rsqrt(jnp.mean(xc * xc, axis=-1, keepdims=True) + EPS)
    xh = xc * rstd
    return xh, rstd, xh * gamma + beta


def _layernorm_rows_bwd(dout, xh, rstd, gamma):
    dxh = dout * gamma
    return rstd * (dxh - jnp.mean(dxh, axis=-1, keepdims=True) - xh * jnp.mean(dxh * xh, axis=-1, keepdims=True))


def _gmlp_fwd(proj, wm, bs_t, ln_g, ln_b, W):
    T = proj.shape[0]
    G = wm.shape[0]
    cg = W // G
    assert cg == LANES

    def body(p_ref, wm_ref, bs_ref, lg_ref, lb_ref, y_ref, vn_ref):
        lane = lax.broadcasted_iota(jnp.int32, (1, LANES), 1)
        _, _, vn = _layernorm_rows(_gelu(p_ref[:, W:2 * W]), lg_ref[...], lb_ref[...])
        vn_ref[...] = vn.astype(BF16)
        bs = bs_ref[...]
        for g in range(G):
            sl = pl.ds(g * cg, cg)
            s = _dot(wm_ref[g], vn_ref[:, sl]) + _pick_col(bs, g, lane)
            gate = p_ref[:, pl.ds(2 * W + g * cg, cg)]
            y_ref[:, sl] = (_gelu(p_ref[:, sl]) * s * _silu(gate)).astype(BF16)

    vec = pl.BlockSpec((1, W), lambda r: (0, 0))
    return pl.pallas_call(
        body, name="gmlp_fwd", grid=(T // BLK,),
        in_specs=[pl.BlockSpec((BLK, 3 * W), lambda r: (r, 0)),
                  pl.BlockSpec((G, BLK, BLK), lambda r: (0, 0, 0)),
                  pl.BlockSpec((BLK, LANES), lambda r: (0, 0)), vec, vec],
        out_specs=pl.BlockSpec((BLK, W), lambda r: (r, 0)),
        out_shape=jax.ShapeDtypeStruct((T, W), BF16),
        scratch_shapes=[pltpu.VMEM((BLK, W), BF16)],
        compiler_params=_params(("parallel",)),
    )(proj, wm, bs_t, ln_g, ln_b)


def _gmlp_bwd(proj, dy, wm, bs_t, ln_g, ln_b, W):
    T = proj.shape[0]
    G = wm.shape[0]
    cg = W // G

    def body(p_ref, dy_ref, wm_ref, bs_ref, lg_ref, lb_ref,
             dp_ref, dwm_ref, dbs_ref, dlg_ref, dlb_ref, vn_ref, dvn_ref):
        r = pl.program_id(0)

        @pl.when(r == 0)
        def _():
            dwm_ref[...] = jnp.zeros_like(dwm_ref)
            dbs_ref[...] = jnp.zeros_like(dbs_ref)
            dlg_ref[...] = jnp.zeros_like(dlg_ref)
            dlb_ref[...] = jnp.zeros_like(dlb_ref)

        lane = lax.broadcasted_iota(jnp.int32, (1, LANES), 1)
        vpre = p_ref[:, W:2 * W]
        gamma = lg_ref[...]
        xh, rstd, vn = _layernorm_rows(_gelu(vpre), gamma, lb_ref[...])
        vn_ref[...] = vn.astype(BF16)
        bs = bs_ref[...]
        dbs = jnp.zeros((BLK, LANES), F32)
        for g in range(G):
            sl = pl.ds(g * cg, cg)
            gsl = pl.ds(2 * W + g * cg, cg)
            vng = vn_ref[:, sl]
            s = _dot(wm_ref[g], vng) + _pick_col(bs, g, lane)
            upre = p_ref[:, sl]
            u = _gelu(upre)
            gate = p_ref[:, gsl]
            dyv = dy_ref[:, sl].astype(F32)
            dp_ref[:, gsl] = (dyv * u * s * _dsilu(gate)).astype(dp_ref.dtype)
            do = dyv * _silu(gate)
            dp_ref[:, sl] = (do * s * _dgelu(upre)).astype(dp_ref.dtype)
            ds = do * u
            dbs = dbs + jnp.where(lane == g, jnp.sum(ds, axis=1, keepdims=True), 0.0)
            dsb = ds.astype(BF16)
            dwm_ref[g] += _dot(dsb, vng, _NT)
            dvn_ref[:, sl] = _dot(wm_ref[g], dsb, _TN)
        dbs_ref[...] += dbs
        dvn = dvn_ref[...]
        dlg_ref[...] += jnp.sum(dvn * xh, axis=0, keepdims=True)
        dlb_ref[...] += jnp.sum(dvn, axis=0, keepdims=True)
        dv = _layernorm_rows_bwd(dvn, xh, rstd, gamma)
        dp_ref[:, W:2 * W] = (dv * _dgelu(vpre)).astype(dp_ref.dtype)

    vec = pl.BlockSpec((1, W), lambda r: (0, 0))
    return pl.pallas_call(
        body, name="gmlp_bwd", grid=(T // BLK,),
        in_specs=[pl.BlockSpec((BLK, 3 * W), lambda r: (r, 0)),
                  pl.BlockSpec((BLK, W), lambda r: (r, 0)),
                  pl.BlockSpec((G, BLK, BLK), lambda r: (0, 0, 0)),
                  pl.BlockSpec((BLK, LANES), lambda r: (0, 0)), vec, vec],
        out_specs=[pl.BlockSpec((BLK, 3 * W), lambda r: (r, 0)),
                   pl.BlockSpec((G, BLK, BLK), lambda r: (0, 0, 0)),
                   pl.BlockSpec((BLK, LANES), lambda r: (0, 0)), vec, vec],
        out_shape=[jax.ShapeDtypeStruct((T, 3 * W), BF16), jax.ShapeDtypeStruct((G, BLK, BLK), F32),
                   jax.ShapeDtypeStruct((BLK, LANES), F32),
                   jax.ShapeDtypeStruct((1, W), F32), jax.ShapeDtypeStruct((1, W), F32)],
        scratch_shapes=[pltpu.VMEM((BLK, W), BF16), pltpu.VMEM((BLK, W), F32)],
        compiler_params=_params(("arbitrary",)),
    )(proj, dy, wm, bs_t, ln_g, ln_b)


SUBLANES = 8
SHIFT_ROWS = CONV_HALO + BLK - SUBLANES


def _shift_rows(ext_ref, sh_ref, off):
    for r in range(1, SUBLANES):
        sh_ref[r - 1] = ext_ref[pl.ds(r, SHIFT_ROWS), pl.ds(off, LANES)]


def _rows_from(ext_ref, sh_ref, off, start):
    r = start % SUBLANES
    if r == 0:
        return ext_ref[pl.ds(start, BLK), pl.ds(off, LANES)]
    return sh_ref[r - 1, pl.ds(start - r, BLK), :]


def _conv_taps(ext_ref, sh_ref, cw_ref, off, n_taps, first):
    acc = jnp.zeros((BLK, LANES), F32)
    for k in range(n_taps):
        acc = acc + cw_ref[k:k + 1, pl.ds(off, LANES)] * _rows_from(ext_ref, sh_ref, off, first + k)
    return acc


def _fill_glu_ext(ext_ref, halo_ref, cur_ref, W, first_block):
    y0h = halo_ref[:, :W] * _sigmoid(halo_ref[:, W:])
    ext_ref[0:CONV_HALO, :] = jnp.where(first_block, 0.0, y0h)
    ext_ref[CONV_HALO:CONV_HALO + BLK, :] = cur_ref[:, :W] * _sigmoid(cur_ref[:, W:])


def _conv_specs(S, W):
    per = BLK // CONV_HALO
    cur = pl.BlockSpec((None, BLK, 2 * W), lambda b, i: (b, i, 0))
    halo = pl.BlockSpec((None, CONV_HALO, 2 * W), lambda b, i: (b, jnp.maximum(i * per - 1, 0), 0))
    gate = pl.BlockSpec((None, BLK, W), lambda b, i: (b, i, 2))
    return cur, halo, gate


def _conv_fwd(proj3, cw, cb, ln_g, ln_b, W, exch):
    B, S, _ = proj3.shape
    K = cw.shape[0]
    first = CONV_HALO - (K - 1)
    assert first >= 0

    def body(cur_ref, halo_ref, g_ref, cw_ref, cb_ref, lg_ref, lb_ref, y_ref, y1_ref, ext_ref, sh_ref):
        i = pl.program_id(1)
        _fill_glu_ext(ext_ref, halo_ref, cur_ref, W, i == 0)

        def chan(c, _):
            off = pl.multiple_of(c * LANES, LANES)
            _shift_rows(ext_ref, sh_ref, off)
            y1_ref[:, pl.ds(off, LANES)] = (_conv_taps(ext_ref, sh_ref, cw_ref, off, K, first)
                                            + cb_ref[:, pl.ds(off, LANES)])
            return 0

        lax.fori_loop(0, W // LANES, chan, 0)
        _, _, ln = _layernorm_rows(y1_ref[...], lg_ref[...], lb_ref[...])
        y_ref[...] = (_silu(ln) * _silu(g_ref[...])).astype(BF16)

    cur, halo, gate = _conv_specs(S, W)
    vec = pl.BlockSpec((1, W), lambda b, i: (0, 0))
    one = pl.BlockSpec((None, BLK, W), lambda b, i: (b, i, 0))
    (y, y1), moved = _call_hosting(
        body, exch, "conv_fwd", (B, S // BLK),
        [cur, halo, gate, pl.BlockSpec((K, W), lambda b, i: (0, 0)), vec, vec, vec],
        [one, one], [jax.ShapeDtypeStruct((B, S, W), BF16), jax.ShapeDtypeStruct((B, S, W), F32)],
        [pltpu.VMEM((CONV_HALO + BLK, W), F32), pltpu.VMEM((SUBLANES - 1, SHIFT_ROWS, LANES), F32)],
        (proj3, proj3, proj3, cw, cb, ln_g, ln_b))
    return y, y1, moved


def _conv_bwd1(proj3, y1, dy, K, ln_g, ln_b, W, exch):
    B, S, _ = proj3.shape
    first = CONV_HALO - (K - 1)

    def body(cur_ref, halo_ref, g_ref, y1_ref, dy_ref, lg_ref, lb_ref,
             dy1_ref, dg_ref, dcw_ref, dcb_ref, dlg_ref, dlb_ref, ext_ref, sh_ref):
        b = pl.program_id(0)
        i = pl.program_id(1)

        @pl.when(jnp.logical_and(b == 0, i == 0))
        def _():
            dcw_ref[...] = jnp.zeros_like(dcw_ref)
            dcb_ref[...] = jnp.zeros_like(dcb_ref)
            dlg_ref[...] = jnp.zeros_like(dlg_ref)
            dlb_ref[...] = jnp.zeros_like(dlb_ref)

        _fill_glu_ext(ext_ref, halo_ref, cur_ref, W, i == 0)
        gamma = lg_ref[...]
        xh, rstd, ln = _layernorm_rows(y1_ref[...], gamma, lb_ref[...])
        g = g_ref[...]
        dyv = dy_ref[...].astype(F32)
        dg_ref[...] = (dyv * _silu(ln) * _dsilu(g)).astype(dg_ref.dtype)
        dln = dyv * _silu(g) * _dsilu(ln)
        dlg_ref[...] += jnp.sum(dln * xh, axis=0, keepdims=True)
        dlb_ref[...] += jnp.sum(dln, axis=0, keepdims=True)
        dy1 = _layernorm_rows_bwd(dln, xh, rstd, gamma)
        dy1_ref[...] = dy1
        dcb_ref[...] += jnp.sum(dy1, axis=0, keepdims=True)

        def chan_w(c, _):
            off = pl.multiple_of(c * LANES, LANES)
            _shift_rows(ext_ref, sh_ref, off)
            d = dy1_ref[:, pl.ds(off, LANES)]
            for k in range(K):
                dcw_ref[k:k + 1, pl.ds(off, LANES)] += jnp.sum(
                    d * _rows_from(ext_ref, sh_ref, off, first + k), axis=0, keepdims=True)
            return 0

        lax.fori_loop(0, W // LANES, chan_w, 0)

    cur, halo, gate = _conv_specs(S, W)
    vec = pl.BlockSpec((1, W), lambda b, i: (0, 0))
    taps = pl.BlockSpec((K, W), lambda b, i: (0, 0))
    one = pl.BlockSpec((None, BLK, W), lambda b, i: (b, i, 0))
    return _call_hosting(
        body, exch, "conv_bwd1", (B, S // BLK), [cur, halo, gate, one, one, vec, vec],
        [one, one, taps, vec, vec, vec],
        [jax.ShapeDtypeStruct((B, S, W), F32), jax.ShapeDtypeStruct((B, S, W), BF16),
         jax.ShapeDtypeStruct((K, W), F32)] + [jax.ShapeDtypeStruct((1, W), F32)] * 3,
        [pltpu.VMEM((CONV_HALO + BLK, W), F32), pltpu.VMEM((SUBLANES - 1, SHIFT_ROWS, LANES), F32)],
        (proj3, proj3, proj3, y1, dy, ln_g, ln_b))


def _conv_bwd2(proj3, dy1, dgate, cw_rev, W):
    B, S, _ = proj3.shape
    K = cw_rev.shape[0]
    NQ = S // BLK
    per = BLK // CONV_HALO

    def body(cur_ref, d_ref, dnext_ref, dgate_ref, cw_ref, dp_ref, ext_ref, dy0_ref, sh_ref):
        i = pl.program_id(1)
        ext_ref[0:BLK, :] = d_ref[...]
        ext_ref[BLK:BLK + CONV_HALO, :] = jnp.where(i == NQ - 1, 0.0, dnext_ref[...])

        def chan(c, _):
            off = pl.multiple_of(c * LANES, LANES)
            _shift_rows(ext_ref, sh_ref, off)
            dy0_ref[:, pl.ds(off, LANES)] = _conv_taps(ext_ref, sh_ref, cw_ref, off, K, 0)
            return 0

        lax.fori_loop(0, W // LANES, chan, 0)
        a = cur_ref[:, :W]
        sg = _sigmoid(cur_ref[:, W:])
        dy0 = dy0_ref[...]
        dp_ref[:, 0:W] = (dy0 * sg).astype(dp_ref.dtype)
        dp_ref[:, W:2 * W] = (dy0 * a * sg * (1.0 - sg)).astype(dp_ref.dtype)
        dp_ref[:, 2 * W:3 * W] = dgate_ref[...]

    cur = pl.BlockSpec((None, BLK, 2 * W), lambda b, i: (b, i, 0))
    one = pl.BlockSpec((None, BLK, W), lambda b, i: (b, i, 0))
    nxt = pl.BlockSpec((None, CONV_HALO, W), lambda b, i: (b, jnp.minimum((i + 1) * per, S // CONV_HALO - 1), 0))
    return pl.pallas_call(
        body, name="conv_bwd2", grid=(B, NQ),
        in_specs=[cur, one, nxt, one, pl.BlockSpec((K, W), lambda b, i: (0, 0))],
        out_specs=pl.BlockSpec((None, BLK, 3 * W), lambda b, i: (b, i, 0)),
        out_shape=jax.ShapeDtypeStruct((B, S, 3 * W), BF16),
        scratch_shapes=[pltpu.VMEM((BLK + CONV_HALO, W), F32), pltpu.VMEM((BLK, W), F32),
                        pltpu.VMEM((SUBLANES - 1, SHIFT_ROWS, LANES), F32)],
        compiler_params=_params(("parallel", "parallel")),
    )(proj3, dy1, dy1, dgate, cw_rev)


def _pack(arrays):
    flat = jnp.concatenate([a.astype(F32).reshape(-1) for a in arrays])
    n = flat.shape[0]
    pad = (-n) % (8 * LANES)
    if pad:
        flat = jnp.concatenate([flat, jnp.zeros((pad,), F32)])
    return flat.reshape(-1, LANES)


def _unpack(packed, shapes, lead=()):
    flat = packed.reshape(lead + (-1,))
    out, off = [], 0
    for shp in shapes:
        n = math.prod(shp)
        out.append(flat[..., off:off + n].reshape(lead + tuple(shp)))
        off += n
    return out


def _cols_from_dev(g):
    g = jnp.moveaxis(g, 0, -2)
    return g.reshape(g.shape[:-2] + (g.shape[-2] * g.shape[-1],))


def _my_cols(full, me):
    n8 = full.shape[-1] // N_DEV
    return lax.dynamic_slice_in_dim(full, me * n8, n8, axis=full.ndim - 1)


def kernel(x, a_norm, a_w_in, a_w_out, b_norm, b_w_in, b_v_ln_g, b_v_ln_b, b_w_s, b_b_s, b_w_out, c_norm, c_w_in, c_conv_w, c_conv_b, c_ln_g, c_ln_b, c_w_out, d_norm, d_w_in, d_b_f, d_w_out, final_norm, loss_target, m_a_norm, m_a_w_in, m_a_w_out, m_b_norm, m_b_w_in, m_b_v_ln_g, m_b_v_ln_b, m_b_w_s, m_b_b_s, m_b_w_out, m_c_norm, m_c_w_in, m_c_conv_w, m_c_conv_b, m_c_ln_g, m_c_ln_b, m_c_w_out, m_d_norm, m_d_w_in, m_d_b_f, m_d_w_out, m_final_norm, v_a_norm, v_a_w_in, v_a_w_out, v_b_norm, v_b_w_in, v_b_v_ln_g, v_b_v_ln_b, v_b_w_s, v_b_b_s, v_b_w_out, v_c_norm, v_c_w_in, v_c_conv_w, v_c_conv_b, v_c_ln_g, v_c_ln_b, v_c_w_out, v_d_norm, v_d_w_in, v_d_b_f, v_d_w_out, v_final_norm):
    B, S, D = x.shape
    T = B * S
    xi, yi, ci = _me()
    me = 4 * xi + 2 * yi + ci

    G = b_w_s.shape[1]
    KC = c_conv_w.shape[1]
    H_D = d_b_f.shape[1]
    W_A = a_w_out.shape[1] * N_DEV
    W_B = b_w_out.shape[1] * N_DEV
    W_C = c_w_out.shape[1] * N_DEV
    W_D = d_w_out.shape[1] * N_DEV
    N_D = d_w_in.shape[2] * N_DEV
    N_D_PAD = -(-N_D // (3 * LANES)) * (3 * LANES)

    big_names = ["a_w_in", "a_w_out", "b_w_in", "b_w_out", "c_w_in", "c_w_out", "d_w_in", "d_w_out"]
    big_w = dict(a_w_in=a_w_in[0], a_w_out=a_w_out[0], b_w_in=b_w_in[0], b_w_out=b_w_out[0],
                 c_w_in=c_w_in[0], c_w_out=c_w_out[0], d_w_in=d_w_in[0], d_w_out=d_w_out[0])
    small_sharded = [b_norm, b_v_ln_g, b_v_ln_b, c_norm, c_conv_w, c_conv_b, c_ln_g, c_ln_b, d_norm]
    first_names, later_names, last_names = big_names[:1], big_names[1:6], big_names[6:]
    gathered = _GatherViaSibling(
        [big_w[n].astype(BF16) for n in first_names] + [_pack(small_sharded)]).run("gather_first")
    wg = dict(zip(first_names, gathered[:-1]))
    (b_norm_f, b_lg_f, b_lb_f, c_norm_f, c_cw_f, c_cb_f, c_lg_f, c_lb_f, d_norm_f) = [
        _cols_from_dev(t) for t in _unpack(gathered[-1], [s.shape for s in small_sharded], lead=(N_DEV,))]
    c_cw_f = c_cw_f[0]

    wm = jnp.tril(b_w_s[0]).astype(BF16)
    bs_t = jnp.pad(b_b_s[0].T, ((0, 0), (0, LANES - G)))

    x0 = x.reshape(T, D)
    h_a = _rmsnorm_fwd(x0, a_norm, "rms_a")
    proj_a = _mm_w_dev(h_a, wg["a_w_in"], "proj_a").reshape(B, S, 4 * W_A)
    (o_a, y_a), later = _sb_fwd(proj_a, W_A, SB_HEADS,
                                _Exchange([big_w[n].astype(BF16) for n in later_names], ["gather"] * len(later_names)))
    wg.update(zip(later_names, later))
    a_w_out_f = wg["a_w_out"].reshape(W_A, D)
    b_w_out_f = wg["b_w_out"].reshape(W_B, D)
    c_w_out_f = wg["c_w_out"].reshape(W_C, D)
    y_a = y_a.reshape(T, W_A)
    x1, h_b = _mm(y_a, a_w_out_f, "nn", T, D, W_A, F32, "out_a", 512, D, W_A, res=x0, norm_gain=b_norm_f)
    proj_b = _mm_w_dev(h_b, wg["b_w_in"], "proj_b")
    y_b = _gmlp_fwd(proj_b, wm, bs_t, b_lg_f, b_lb_f, W_B)
    x2, h_c = _mm(y_b, b_w_out_f, "nn", T, D, W_B, F32, "out_b", 512, D, W_B, res=x1, norm_gain=c_norm_f)
    proj_c = _mm_w_dev(h_c, wg["c_w_in"], "proj_c").reshape(B, S, 3 * W_C)
    y_c, y1_c, last = _conv_fwd(
        proj_c, c_cw_f, c_cb_f, c_lg_f, c_lb_f, W_C,
        _Exchange([big_w[n].astype(BF16) for n in last_names], ["gather"] * len(last_names)))
    wg.update(zip(last_names, last))
    d_w_out_f = wg["d_w_out"].reshape(W_D, D)
    d_w_in_f = jnp.pad(_cols_from_dev(wg["d_w_in"]), ((0, 0), (0, N_D_PAD - N_D)))
    y_c = y_c.reshape(T, W_C)
    x3, h_d = _mm(y_c, c_w_out_f, "nn", T, D, W_C, F32, "out_c", 512, D, W_C, res=x2, norm_gain=d_norm_f)
    proj_d = _mm(h_d, d_w_in_f, "nn", T, N_D_PAD, D, F32, "proj_d", 1024, 384, D).reshape(B, S, N_D_PAD)
    f_t = jnp.swapaxes(proj_d[:, :, 4 * W_D:4 * W_D + H_D], 1, 2)
    b_f_col = d_b_f.reshape(H_D, 1)
    cum_t = _fox_gate_fwd(f_t, b_f_col)
    o_d, y_d, lse_d = _fox_fwd(proj_d, cum_t, W_D, H_D)
    y_d = y_d.reshape(T, W_D)
    x4 = _mm(y_d, d_w_out_f, "nn", T, D, W_D, F32, "out_d", 512, D, W_D, res=x3)

    loss_part, dx, g_final = _loss_head(x4, final_norm.reshape(1, D), loss_target.reshape(T, D))
    loss = lax.psum(loss_part[0, 0], MESH_AXES)

    dy_d = _mm(dx, d_w_out_f, "nt", T, W_D, D, BF16, "dy_d", 512, W_D, D).reshape(B, S, W_D)
    gw_d_out = _mm(y_d, dx, "tn", W_D, D, T, BF16, "gw_d_out", W_D, D, 512).reshape(N_DEV, W_D // N_DEV, D)
    dproj_d, dcum = _fox_bwd(proj_d, cum_t, o_d, lse_d, dy_d, W_D, H_D)
    df_t, g_b_f = _fox_gate_bwd(dcum.reshape(B, H_D, S), f_t, b_f_col)
    F_PAD = N_D_PAD - 4 * W_D
    df = jnp.pad(jnp.swapaxes(df_t, 1, 2), ((0, 0), (0, 0), (0, F_PAD - H_D))).reshape(T, F_PAD)
    tc, tr = min(512, W_D), min(1024, S)
    gw_main = _mm(h_d, dproj_d, "tn", D, 4 * W_D, T, BF16, "gw_d_in", D, tc, tr,
                  b_spec=_sectioned_spec(dproj_d, tr, tc, 2, 1))
    gw_f = _mm(h_d, df, "tn", D, F_PAD, T, BF16, "gw_d_in_f", D, F_PAD, 512)
    gw_d_in = jnp.moveaxis(
        jnp.concatenate([gw_main, gw_f], axis=1)[:, :N_D].reshape(D, N_DEV, N_D // N_DEV), 1, 0)
    dh_f = _mm(df, d_w_in_f[:, 4 * W_D:], "nt", T, D, F_PAD, F32, "dh_d_f", 512, D, F_PAD)
    dx, g_d_norm, _ = _mm(dproj_d, d_w_in_f, "nt", T, D, 4 * W_D, F32, "dh_d", tr, D, tc,
                          a_spec=_sectioned_spec(dproj_d, tr, tc, 0, 2), res=dh_f, norm_bwd=(x3, d_norm_f, dx))

    dy_c = _mm(dx, c_w_out_f, "nt", T, W_C, D, BF16, "dy_c", 512, W_C, D).reshape(B, S, W_C)
    gw_c_out = _mm(y_c, dx, "tn", W_C, D, T, BF16, "gw_c_out", 1024, D, 512).reshape(N_DEV, W_C // N_DEV, D)
    (dy1, dgate_c, g_c_cw, g_c_cb, g_c_lg, g_c_lb), parts_d = _conv_bwd1(
        proj_c, y1_c, dy_c, KC, c_lg_f, c_lb_f, W_C, _Exchange([gw_d_in, gw_d_out], ["scatter"] * 2))
    dproj_c = _conv_bwd2(proj_c, dy1, dgate_c, c_cw_f[::-1], W_C).reshape(T, 3 * W_C)
    gw_c_in = _mm_grad_dev(h_c, dproj_c, "gw_c_in")
    dx, g_c_norm, _ = _mm_wT_dev(dproj_c, wg["c_w_in"], "dh_c", norm_bwd=(x2, c_norm_f, dx))

    dy_b = _mm(dx, b_w_out_f, "nt", T, W_B, D, BF16, "dy_b", 512, W_B, D)
    gw_b_out = _mm(y_b, dx, "tn", W_B, D, T, BF16, "gw_b_out", 1024, D, 512).reshape(N_DEV, W_B // N_DEV, D)
    dproj_b, g_wm, g_bs_t, g_b_lg, g_b_lb = _gmlp_bwd(proj_b, dy_b, wm, bs_t, b_lg_f, b_lb_f, W_B)
    g_b_w_s = jnp.tril(g_wm)
    g_b_b_s = g_bs_t[:, :G].T
    gw_b_in = _mm_grad_dev(h_b, dproj_b, "gw_b_in")
    dx, g_b_norm, _ = _mm_wT_dev(dproj_b, wg["b_w_in"], "dh_b", norm_bwd=(x1, b_norm_f, dx))

    dy_a = _mm(dx, a_w_out_f, "nt", T, W_A, D, BF16, "dy_a", 512, W_A, D).reshape(B, S, W_A)
    gw_a_out = _mm(y_a, dx, "tn", W_A, D, T, BF16, "gw_a_out", W_A, D, 512).reshape(N_DEV, W_A // N_DEV, D)
    small_full = [g_b_norm, g_b_lg, g_b_lb, g_b_b_s, g_c_norm, g_c_cw, g_c_cb, g_c_lg, g_c_lb,
                  g_d_norm, g_b_f, g_final]
    dproj_a, parts_s = _sb_bwd(
        proj_a, o_a, dy_a, W_A, SB_HEADS,
        _Exchange([gw_c_in, gw_c_out, gw_b_in, gw_b_out, gw_a_out, _pack(small_full), g_b_w_s.reshape(-1, LANES)],
                  ["scatter"] * 5 + ["gather"] * 2))
    gw_a_in = _mm_grad_dev(h_a, dproj_a, "gw_a_in")
    dx, g_a_norm, parts_a = _mm_wT_dev(dproj_a, wg["a_w_in"], "dh_a", exch=_Exchange([gw_a_in], ["scatter"]),
                                       norm_bwd=(x0, a_norm, dx))
    grad_x = dx.reshape(B, S, D)

    (parts_n,) = _Exchange([_pack([g_a_norm])], ["gather"]).run("exchange_last")
    big_parts = dict(a_w_in=parts_a[0], a_w_out=parts_s[4], b_w_in=parts_s[2], b_w_out=parts_s[3],
                     c_w_in=parts_s[0], c_w_out=parts_s[1], d_w_in=parts_d[0], d_w_out=parts_d[1])
    (s_b_norm, s_b_lg, s_b_lb, s_b_b_s, s_c_norm, s_c_cw, s_c_cb, s_c_lg, s_c_lb,
     s_d_norm, s_b_f, s_final) = _unpack(_sum_parts(parts_s[5], "sum_small"), [g.shape for g in small_full])
    (s_a_norm,) = _unpack(_sum_parts(parts_n, "sum_a_norm"), [g_a_norm.shape])

    weights = dict(a_norm=a_norm, a_w_in=a_w_in, a_w_out=a_w_out, b_norm=b_norm, b_w_in=b_w_in, b_v_ln_g=b_v_ln_g,
                   b_v_ln_b=b_v_ln_b, b_w_s=b_w_s, b_b_s=b_b_s, b_w_out=b_w_out, c_norm=c_norm, c_w_in=c_w_in,
                   c_conv_w=c_conv_w, c_conv_b=c_conv_b, c_ln_g=c_ln_g, c_ln_b=c_ln_b, c_w_out=c_w_out,
                   d_norm=d_norm, d_w_in=d_w_in, d_b_f=d_b_f, d_w_out=d_w_out, final_norm=final_norm)
    mom_m = dict(a_norm=m_a_norm, a_w_in=m_a_w_in, a_w_out=m_a_w_out, b_norm=m_b_norm, b_w_in=m_b_w_in,
                 b_v_ln_g=m_b_v_ln_g, b_v_ln_b=m_b_v_ln_b, b_w_s=m_b_w_s, b_b_s=m_b_b_s, b_w_out=m_b_w_out,
                 c_norm=m_c_norm, c_w_in=m_c_w_in, c_conv_w=m_c_conv_w, c_conv_b=m_c_conv_b, c_ln_g=m_c_ln_g,
                 c_ln_b=m_c_ln_b, c_w_out=m_c_w_out, d_norm=m_d_norm, d_w_in=m_d_w_in, d_b_f=m_d_b_f,
                 d_w_out=m_d_w_out, final_norm=m_final_norm)
    mom_v = dict(a_norm=v_a_norm, a_w_in=v_a_w_in, a_w_out=v_a_w_out, b_norm=v_b_norm, b_w_in=v_b_w_in,
                 b_v_ln_g=v_b_v_ln_g, b_v_ln_b=v_b_v_ln_b, b_w_s=v_b_w_s, b_b_s=v_b_b_s, b_w_out=v_b_w_out,
                 c_norm=v_c_norm, c_w_in=v_c_w_in, c_conv_w=v_c_conv_w, c_conv_b=v_c_conv_b, c_ln_g=v_c_ln_g,
                 c_ln_b=v_c_ln_b, c_w_out=v_c_w_out, d_norm=v_d_norm, d_w_in=v_d_w_in, d_b_f=v_d_b_f,
                 d_w_out=v_d_w_out, final_norm=v_final_norm)
    order = list(weights)
    grads, deltas, new_m, new_v = {}, {}, {}, {}

    for n in big_names:
        part = big_parts[n]
        shp = weights[n].shape
        R, C = shp[1], shp[2]
        res = _adamw(part, weights[n].reshape(R, C), mom_m[n].reshape(R, C), mom_v[n].reshape(R, C), "adamw_" + n)
        grads[n], deltas[n], new_m[n], new_v[n] = [r.reshape(shp) for r in res]

    res = _adamw(parts_s[6], b_w_s.reshape(-1, LANES), m_b_w_s.reshape(-1, LANES), v_b_w_s.reshape(-1, LANES),
                 "adamw_b_w_s")
    grads["b_w_s"], deltas["b_w_s"], new_m["b_w_s"], new_v["b_w_s"] = [r.reshape(b_w_s.shape) for r in res]

    small_g = dict(
        a_norm=s_a_norm, b_norm=_my_cols(s_b_norm, me), b_v_ln_g=_my_cols(s_b_lg, me),
        b_v_ln_b=_my_cols(s_b_lb, me), b_b_s=s_b_b_s[None], c_norm=_my_cols(s_c_norm, me),
        c_conv_w=_my_cols(s_c_cw, me)[None], c_conv_b=_my_cols(s_c_cb, me), c_ln_g=_my_cols(s_c_lg, me),
        c_ln_b=_my_cols(s_c_lb, me), d_norm=_my_cols(s_d_norm, me), d_b_f=s_b_f.reshape(1, H_D),
        final_norm=s_final.reshape(D))
    small_names = list(small_g)
    sg_p = _pack([small_g[n] for n in small_names])
    res = _adamw(sg_p[None], _pack([weights[n] for n in small_names]), _pack([mom_m[n] for n in small_names]),
                 _pack([mom_v[n] for n in small_names]), "adamw_small")
    shapes = [weights[n].shape for n in small_names]
    for dst, r in zip((grads, deltas, new_m, new_v), res):
        for n, val in zip(small_names, _unpack(r, shapes)):
            dst[n] = val

    return (loss, grad_x, *[grads[n] for n in order], *[deltas[n] for n in order],
            *[new_m[n] for n in order], *[new_v[n] for n in order])
```

```python
import functools
import math

import jax
import jax.numpy as jnp
from jax import lax
from jax.experimental import pallas as pl
from jax.experimental.pallas import tpu as pltpu

F32 = jnp.float32
BF16 = jnp.bfloat16

EPS = 1e-6
SB_HEADS = 16
CONV_HALO = 32
BLK = 128
ATT_TK = 256
FOX_TK = 512
ATT_TQ = 512
ATT_GP = 2
LANES = 128
N_DEV = 8
MESH_AXES = ("x", "y", "c")

ADAM_LR = 0.001
ADAM_B1 = 0.9
ADAM_B2 = 0.999
ADAM_EPS = 1e-08
ADAM_WD = 0.01
ADAM_STEP = 10

VMEM_LIMIT = 56 * 1024 * 1024
NEG_BIG = -1e30

_NN = (((1,), (0,)), ((), ()))
_NT = (((1,), (1,)), ((), ()))
_TN = (((0,), (0,)), ((), ()))


def _dot(a, b, dims=_NN):
    return lax.dot_general(a, b, dims, preferred_element_type=F32)


def _split_dot(x, m):
    hi = lax.bitcast_convert_type(lax.bitcast_convert_type(x, jnp.int32) & jnp.int32(-65536), F32)
    return _dot(hi.astype(BF16), m) + _dot((x - hi).astype(BF16), m)


def _split3_dot(x, m):
    hi = x.astype(BF16)
    r1 = x - hi.astype(F32)
    mid = r1.astype(BF16)
    lo = (r1 - mid.astype(F32)).astype(BF16)
    return _dot(hi, m) + _dot(mid, m) + _dot(lo, m)


def _params(sem=None):
    kw = dict(vmem_limit_bytes=VMEM_LIMIT)
    if sem is not None:
        kw["dimension_semantics"] = sem
    return pltpu.CompilerParams(**kw)


def _sigmoid(x):
    return jax.nn.sigmoid(x)


def _silu(x):
    return x * _sigmoid(x)


def _dsilu(x):
    s = _sigmoid(x)
    return s * (1.0 + x * (1.0 - s))


_GELU_C = math.sqrt(2.0 / math.pi)


def _gelu(x):
    return 0.5 * x * (1.0 + jnp.tanh(_GELU_C * (x + 0.044715 * x * x * x)))


def _dgelu(x):
    th = jnp.tanh(_GELU_C * (x + 0.044715 * x * x * x))
    return 0.5 * (1.0 + th) + 0.5 * x * (1.0 - th * th) * _GELU_C * (1.0 + 3.0 * 0.044715 * x * x)


def _mm(a, b, mode, M, N, K, out_dtype, name, tm, tn, tk, a_spec=None, b_spec=None, o_spec=None, out_shape=None,
        exch=None, res=None, norm_gain=None, norm_bwd=None):
    tm, tn, tk = min(tm, M), min(tn, N), min(tk, K)
    assert M % tm == 0 and N % tn == 0 and K % tk == 0, (name, M, N, K, tm, tn, tk)
    nk = K // tk
    assert norm_gain is None or (nk == 1 and tn == N and exch is None)
    dims = {"nn": _NN, "nt": _NT, "tn": _TN}[mode]
    if a_spec is None:
        a_spec = (pl.BlockSpec((tk, tm), lambda i, j, k: (k, i)) if mode == "tn"
                  else pl.BlockSpec((tm, tk), lambda i, j, k: (i, k)))
    if b_spec is None:
        b_spec = (pl.BlockSpec((tn, tk), lambda i, j, k: (j, k)) if mode == "nt"
                  else pl.BlockSpec((tk, tn), lambda i, j, k: (k, j)))
    if o_spec is None:
        o_spec = pl.BlockSpec((tm, tn), lambda i, j, k: (i, j))
    if out_shape is None:
        out_shape = (M, N)

    def body(a_ref, b_ref, *rest):
        res_ref = rest[0] if res is not None else None
        if nk == 1:
            d = _dot(a_ref[...].astype(BF16), b_ref[...].astype(BF16), dims)
            r = d if res_ref is None else res_ref[...].astype(F32) + d
            if norm_gain is None:
                rest[-1][...] = r.astype(rest[-1].dtype)
            else:
                g_ref, o_ref, h_ref = rest[-3:]
                o_ref[...] = r.astype(o_ref.dtype)
                scale = lax.rsqrt(jnp.mean(r * r, axis=-1, keepdims=True) + EPS)
                h_ref[...] = (r * scale * g_ref[...]).astype(BF16)
            return
        i = pl.program_id(0)
        k = pl.program_id(2)
        if norm_bwd is not None:
            x_ref, g_ref, dres_ref, o_ref, dg_ref, acc_ref = rest[-6:]

            @pl.when(jnp.logical_and(i == 0, k == 0))
            def _():
                dg_ref[...] = jnp.zeros_like(dg_ref)
        else:
            o_ref, acc_ref = rest[-2:]

        @pl.when(k == 0)
        def _():
            acc_ref[...] = jnp.zeros_like(acc_ref) if res_ref is None else res_ref[...].astype(F32)

        acc_ref[...] += _dot(a_ref[...].astype(BF16), b_ref[...].astype(BF16), dims)

        @pl.when(k == nk - 1)
        def _():
            if norm_bwd is None:
                o_ref[...] = acc_ref[...].astype(o_ref.dtype)
            else:
                dh = acc_ref[...]
                xv = x_ref[...]
                r = lax.rsqrt(jnp.mean(xv * xv, axis=-1, keepdims=True) + EPS)
                xh = xv * r
                dxh = dh * g_ref[...]
                o_ref[...] = dres_ref[...] + r * (dxh - xh * jnp.mean(dxh * xh, axis=-1, keepdims=True))
                dg_ref[...] += jnp.sum(dh * xh, axis=0, keepdims=True)

    in_specs, args = [a_spec, b_spec], (a, b)
    if res is not None:
        in_specs, args = in_specs + [o_spec], args + (res,)
    scratch = [pltpu.VMEM((tm, tn), F32)] if nk > 1 else []
    if norm_bwd is not None:
        assert nk > 1 and tn == N and norm_gain is None
        vec = pl.BlockSpec((1, N), lambda i, j, k: (0, 0))
        x_in, g_in, dres_in = norm_bwd
        (dx, dg), moved = _call_hosting(
            body, exch, name, (M // tm, 1, nk), in_specs + [o_spec, vec, o_spec], [o_spec, vec],
            [jax.ShapeDtypeStruct((M, N), F32), jax.ShapeDtypeStruct((1, N), F32)], scratch,
            args + (x_in, g_in, dres_in))
        return dx, dg, moved
    if norm_gain is not None:
        return pl.pallas_call(
            body, name=name, grid=(M // tm, N // tn, nk),
            in_specs=in_specs + [pl.BlockSpec((1, N), lambda i, j, k: (0, 0))], out_specs=[o_spec, o_spec],
            out_shape=[jax.ShapeDtypeStruct(out_shape, out_dtype), jax.ShapeDtypeStruct(out_shape, BF16)],
            compiler_params=_params(("parallel", "parallel", "arbitrary")),
        )(*args, norm_gain)
    if exch is None:
        return pl.pallas_call(
            body, name=name, grid=(M // tm, N // tn, nk),
            in_specs=in_specs, out_specs=o_spec,
            out_shape=jax.ShapeDtypeStruct(out_shape, out_dtype),
            scratch_shapes=scratch,
            compiler_params=_params(("parallel", "parallel", "arbitrary")),
        )(*args)
    (out,), moved = _call_hosting(
        body, exch, name, (M // tm, N // tn, nk), in_specs, [o_spec],
        [jax.ShapeDtypeStruct(out_shape, out_dtype)], scratch, args)
    return out, moved


def _mm_w_dev(a, w3, name, out_dtype=F32, tm=1024):
    M, K = a.shape
    n8 = w3.shape[2]
    tn = n8 if n8 <= 768 else 512
    per = n8 // tn
    b_spec = pl.BlockSpec((None, K, tn), lambda i, j, k: (j // per, 0, j % per))
    return _mm(a, w3, "nn", M, N_DEV * n8, K, out_dtype, name, tm, tn, K, b_spec=b_spec)


def _sectioned_spec(d4, t_rows, t_cols, rows_axis, cols_axis):
    _, _, S, W = d4.shape
    assert S % t_rows == 0 and W % t_cols == 0
    rb, cb = S // t_rows, W // t_cols

    def index(*g):
        r, c = g[rows_axis], g[cols_axis]
        return (r // rb, c // cb, r % rb, c % cb)

    return pl.BlockSpec((None, None, t_rows, t_cols), index)


def _mm_wT_dev(a, w3, name, out_dtype=F32, tm=1024, exch=None, norm_bwd=None):
    K, n8 = w3.shape[1], w3.shape[2]
    tk = n8 if n8 <= 768 else 512
    per = n8 // tk
    b_spec = pl.BlockSpec((None, K, tk), lambda i, j, k: (k // per, 0, k % per))
    if a.ndim == 4:
        M, N = a.shape[0] * a.shape[2], a.shape[1] * a.shape[3]
        tm = min(tm, a.shape[2])
        a_spec = _sectioned_spec(a, tm, tk, 0, 2)
    else:
        (M, N), a_spec = a.shape, None
    return _mm(a, w3, "nt", M, K, N, out_dtype, name, tm, K, tk, a_spec=a_spec, b_spec=b_spec, exch=exch,
               norm_bwd=norm_bwd)


def _mm_grad_dev(h, d, name, out_dtype=BF16):
    T, M = h.shape
    N = d.shape[1] * d.shape[3] if d.ndim == 4 else d.shape[1]
    n8 = N // N_DEV
    tn = n8 if n8 <= 768 else 512
    per = n8 // tn
    tm = min(M, 1024)
    o_spec = pl.BlockSpec((None, tm, tn), lambda i, j, k: (j // per, i, j % per))
    tk = min(1024, d.shape[2] if d.ndim == 4 else T)
    b_spec = _sectioned_spec(d, tk, tn, 2, 1) if d.ndim == 4 else None
    return _mm(h, d, "tn", M, N, T, out_dtype, name, tm, tn, tk, b_spec=b_spec, o_spec=o_spec,
               out_shape=(N_DEV, M, n8))


def _me():
    x, y, c = lax.axis_index("x"), lax.axis_index("y"), lax.axis_index("c")
    return x, y, c


def _peer(r):
    x, y, c = _me()
    px = 1 - x if (r >> 2) & 1 else x
    py = 1 - y if (r >> 1) & 1 else y
    pc = 1 - c if r & 1 else c
    return (px, py, pc), 4 * px + 2 * py + pc


class _Exchange:
    def __init__(self, arrays, kinds):
        self.arrays, self.kinds, self.n = list(arrays), list(kinds), len(arrays)
        self.out_shapes = [
            jax.ShapeDtypeStruct((N_DEV,) + a.shape if kind == "gather" else a.shape, a.dtype)
            for a, kind in zip(arrays, kinds)]
        self.specs = [pl.BlockSpec(memory_space=pl.ANY)] * self.n
        self.sems = [pltpu.SemaphoreType.DMA((self.n, N_DEV - 1)), pltpu.SemaphoreType.DMA((self.n, N_DEV - 1)),
                     pltpu.SemaphoreType.DMA((self.n,))]

    def _copies(self, ins, outs, sems, receiving):
        send_sems, recv_sems, local_sems = sems
        x, y, c = _me()
        me = 4 * x + 2 * y + c

        def src(k, pid):
            return ins[k] if self.kinds[k] == "gather" else ins[k].at[pid]

        local = [pltpu.make_async_copy(src(k, me), outs[k].at[me], local_sems.at[k]) for k in range(self.n)]
        remote = []
        for r in range(1, N_DEV):
            peer, pid = _peer(r)
            for k in range(self.n):
                remote.append(pltpu.make_async_remote_copy(
                    src_ref=src(k, pid), dst_ref=outs[k].at[pid if receiving else me],
                    send_sem=send_sems.at[k, r - 1], recv_sem=recv_sems.at[k, r - 1],
                    device_id=peer, device_id_type=pl.DeviceIdType.MESH))
        return local, remote

    def start(self, ins, outs, sems):
        local, remote = self._copies(ins, outs, sems, False)
        for cp in local + remote:
            cp.start()

    def wait(self, ins, outs, sems):
        local, remote = self._copies(ins, outs, sems, True)
        for cp in remote:
            cp.wait_recv()
        for cp in remote:
            cp.wait_send()
        for cp in local:
            cp.wait()

    def run(self, name):
        n = self.n

        def body(*refs):
            ins, outs, sems = refs[:n], refs[n:2 * n], refs[2 * n:]
            self.start(ins, outs, sems)
            self.wait(ins, outs, sems)

        return pl.pallas_call(
            body, name=name, in_specs=self.specs, out_specs=self.specs, out_shape=self.out_shapes,
            scratch_shapes=self.sems,
        )(*self.arrays)


class _GatherViaSibling(_Exchange):
    ICI = (2, 4, 6)

    def __init__(self, arrays):
        super().__init__(arrays, ["gather"] * len(arrays))

    def _copy(self, ins, outs, sems, k, column, block, to, from_input=False):
        return pltpu.make_async_remote_copy(
            src_ref=ins[k] if from_input else outs[k].at[block], dst_ref=outs[k].at[block],
            send_sem=sems[0].at[k, column], recv_sem=sems[1].at[k, column],
            device_id=to, device_id_type=pl.DeviceIdType.MESH)

    def start(self, ins, outs, sems):
        x, y, c = _me()
        me = 4 * x + 2 * y + c
        for k in range(self.n):
            pltpu.make_async_copy(ins[k], outs[k].at[me], sems[2].at[k]).start()
            self._copy(ins, outs, sems, k, 0, me, _peer(1)[0], True).start()
            for j, r in enumerate(self.ICI):
                self._copy(ins, outs, sems, k, 1 + j, me, _peer(r)[0], True).start()

    def wait(self, ins, outs, sems):
        x, y, c = _me()
        me = 4 * x + 2 * y + c
        sibling, sibling_id = _peer(1)
        for j, r in enumerate(self.ICI):
            peer, pid = _peer(r)
            for k in range(self.n):
                self._copy(ins, outs, sems, k, 1 + j, pid, peer).wait_recv()
                self._copy(ins, outs, sems, k, 4 + j, pid, sibling).start()
        for k in range(self.n):
            self._copy(ins, outs, sems, k, 0, sibling_id, sibling).wait_recv()
            for j, r in enumerate(self.ICI):
                self._copy(ins, outs, sems, k, 4 + j, _peer(r ^ 1)[1], sibling).wait_recv()
            for column in range(N_DEV - 1):
                self._copy(ins, outs, sems, k, column, me, sibling).wait_send()
            pltpu.make_async_copy(ins[k], outs[k].at[me], sems[2].at[k]).wait()


def _call_hosting(body, exch, name, grid, in_specs, out_specs, out_shape, scratch_shapes, args):
    if exch is None:
        res = pl.pallas_call(
            body, name=name, grid=grid, in_specs=list(in_specs), out_specs=list(out_specs),
            out_shape=list(out_shape), scratch_shapes=list(scratch_shapes),
            compiler_params=_params(("arbitrary",) * len(grid)))(*args)
        return res, []
    n_in, n_out, n_scr, nc = len(in_specs), len(out_specs), len(scratch_shapes), exch.n

    def full_body(*refs):
        ins, refs = refs[:n_in], refs[n_in:]
        cins, refs = refs[:nc], refs[nc:]
        outs, refs = refs[:n_out], refs[n_out:]
        couts, refs = refs[:nc], refs[nc:]
        scr, sems = refs[:n_scr], refs[n_scr:]
        ids = [pl.program_id(a) for a in range(len(grid))]
        first = functools.reduce(jnp.logical_and, [i == 0 for i in ids])
        last = functools.reduce(jnp.logical_and, [i == g - 1 for i, g in zip(ids, grid)])

        @pl.when(first)
        def _():
            exch.start(cins, couts, sems)

        body(*ins, *outs, *scr)

        @pl.when(last)
        def _():
            exch.wait(cins, couts, sems)

    res = pl.pallas_call(
        full_body, name=name, grid=grid,
        in_specs=list(in_specs) + exch.specs, out_specs=list(out_specs) + exch.specs,
        out_shape=list(out_shape) + exch.out_shapes,
        scratch_shapes=list(scratch_shapes) + exch.sems,
        compiler_params=_params(("arbitrary",) * len(grid)),
    )(*args, *exch.arrays)
    return res[:n_out], res[n_out:]


def _rmsnorm_fwd(x, g, name, exch=None):
    T, D = x.shape
    tr = min(256, T)

    def body(x_ref, g_ref, h_ref):
        xv = x_ref[...]
        r = lax.rsqrt(jnp.mean(xv * xv, axis=-1, keepdims=True) + EPS)
        h_ref[...] = (xv * r * g_ref[...]).astype(BF16)

    row = pl.BlockSpec((tr, D), lambda i: (i, 0))
    (h,), moved = _call_hosting(
        body, exch, name, (T // tr,), [row, pl.BlockSpec((1, D), lambda i: (0, 0))], [row],
        [jax.ShapeDtypeStruct((T, D), BF16)], [], (x, g))
    return h, moved


def _loss_head(x, g, target):
    T, D = x.shape
    tr = min(256, T)

    def body(x_ref, g_ref, t_ref, loss_ref, dx_ref, dg_ref):
        i = pl.program_id(0)
        xv = x_ref[...]
        gv = g_ref[...]
        r = lax.rsqrt(jnp.mean(xv * xv, axis=-1, keepdims=True) + EPS)
        xh = xv * r
        diff = xh * gv - t_ref[...]
        dy = diff * (1.0 / D)
        dxh = dy * gv
        dx_ref[...] = r * (dxh - xh * jnp.mean(dxh * xh, axis=-1, keepdims=True))

        @pl.when(i == 0)
        def _():
            dg_ref[...] = jnp.zeros_like(dg_ref)
            loss_ref[...] = jnp.zeros_like(loss_ref)

        dg_ref[...] += jnp.sum(dy * xh, axis=0, keepdims=True)
        part = jnp.sum(jnp.sum(diff * diff, axis=1, keepdims=True), axis=0, keepdims=True)
        loss_ref[...] += (0.5 / D) * part

    row = pl.BlockSpec((tr, D), lambda i: (i, 0))
    vec = pl.BlockSpec((1, D), lambda i: (0, 0))
    return pl.pallas_call(
        body, name="loss_head", grid=(T // tr,),
        in_specs=[row, vec, row],
        out_specs=[pl.BlockSpec((1, 1), lambda i: (0, 0)), row, vec],
        out_shape=[jax.ShapeDtypeStruct((1, 1), F32), jax.ShapeDtypeStruct((T, D), F32),
                   jax.ShapeDtypeStruct((1, D), F32)],
        compiler_params=_params(("arbitrary",)),
    )(x, g, target)


ELEMS_PER_STEP = 1 << 20


def _row_tile(R, per_row):
    best = None
    for tr in range(8, R + 1, 8):
        if R % tr == 0 and tr * per_row <= ELEMS_PER_STEP:
            best = tr
    return best if best is not None else R


def _adamw(parts, w, m, v, name):
    P, R, C = parts.shape
    tr = _row_tile(R, P * C)

    def body(p_ref, w_ref, m_ref, v_ref, g_out, d_out, m_out, v_out):
        g = p_ref[0].astype(F32)
        for p in range(1, P):
            g = g + p_ref[p].astype(F32)
        wv = w_ref[...]
        mn = ADAM_B1 * m_ref[...] + (1.0 - ADAM_B1) * g
        vn = ADAM_B2 * v_ref[...] + (1.0 - ADAM_B2) * (g * g)
        m_hat = mn / (1.0 - ADAM_B1 ** ADAM_STEP)
        v_hat = vn / (1.0 - ADAM_B2 ** ADAM_STEP)
        g_out[...] = g
        d_out[...] = -ADAM_LR * (m_hat / (jnp.sqrt(v_hat) + ADAM_EPS) + ADAM_WD * wv)
        m_out[...] = mn
        v_out[...] = vn

    row = pl.BlockSpec((tr, C), lambda i: (i, 0))
    return pl.pallas_call(
        body, name=name, grid=(R // tr,),
        in_specs=[pl.BlockSpec((P, tr, C), lambda i: (0, i, 0)), row, row, row],
        out_specs=[row, row, row, row],
        out_shape=[jax.ShapeDtypeStruct((R, C), F32)] * 4,
        compiler_params=_params(("parallel",)),
    )(parts, w, m, v)


def _sum_parts(parts, name):
    P, R, C = parts.shape
    tr = _row_tile(R, P * C)

    def body(p_ref, o_ref):
        g = p_ref[0]
        for p in range(1, P):
            g = g + p_ref[p]
        o_ref[...] = g

    return pl.pallas_call(
        body, name=name, grid=(R // tr,),
        in_specs=[pl.BlockSpec((P, tr, C), lambda i: (0, i, 0))],
        out_specs=pl.BlockSpec((tr, C), lambda i: (i, 0)),
        out_shape=jax.ShapeDtypeStruct((R, C), F32),
        compiler_params=_params(("parallel",)),
    )(parts)


def _lane_head(Dh):
    assert Dh & (Dh - 1) == 0 and Dh <= LANES
    return lax.shift_right_logical(lax.broadcasted_iota(jnp.int32, (1, LANES), 1), Dh.bit_length() - 1)


def _stack_heads(x, lane_head, hpb):
    return jnp.concatenate([jnp.where(lane_head == h, x, 0.0) for h in range(hpb)], axis=0)


def _unstack_heads(acc, lane_head, hpb):
    TQ = acc.shape[0] // hpb
    out = acc[0:TQ]
    for h in range(1, hpb):
        out = jnp.where(lane_head == h, acc[h * TQ:(h + 1) * TQ], out)
    return out


def _live_rows(x, r0, hpb):
    if r0 == 0:
        return x
    TQ = x.shape[0] // hpb
    return jnp.concatenate([x[h * TQ + r0:(h + 1) * TQ] for h in range(hpb)], axis=0)


def _put_rows(full, part, r0, hpb):
    if r0 == 0:
        return part
    TQ = full.shape[0] // hpb
    n = TQ - r0
    return jnp.concatenate(
        [blk for h in range(hpb) for blk in (full[h * TQ:h * TQ + r0], part[h * n:(h + 1) * n])], axis=0)


def _first_live_row(m, TQ, TK):
    return max(0, TQ - (m + 1) * TK)


def _key_tile(S, tk=None):
    tk = ATT_TK if tk is None else tk
    return tk if S % tk == 0 else BLK


def _query_tile(S):
    return ATT_TQ if S % ATT_TQ == 0 else BLK


def _lane_groups(P):
    return ATT_GP if P % ATT_GP == 0 else 1


def _lanes(u):
    return slice(u * LANES, (u + 1) * LANES)


def _causal_iotas(RS, TK, TQ, r0=0):
    n = TQ - r0
    assert n & (n - 1) == 0 and (TK % TQ == 0 or TQ % TK == 0)
    rows = RS // TQ * n
    trow = jnp.bitwise_and(lax.broadcasted_iota(jnp.int32, (rows, TK), 0), n - 1) + r0
    col = lax.broadcasted_iota(jnp.int32, (rows, TK), 1)
    return trow, col


def _tri(TK, op):
    r = lax.broadcasted_iota(jnp.int32, (TK, TK), 0)
    c = lax.broadcasted_iota(jnp.int32, (TK, TK), 1)
    return op(r, c).astype(BF16)


def _logsig_parts(z):
    lb = jnp.minimum(z, 0.0) - jnp.log(1.0 + jnp.exp(-jnp.abs(z)))
    return lb, lb - z


def _sb_fwd(proj3, W, heads, exch):
    B, S, _ = proj3.shape
    Dh = W // heads
    hpb = LANES // Dh
    P, TQ = W // LANES, _query_tile(S)
    NQ = S // TQ
    scale = 1.0 / math.sqrt(Dh)

    TK = _key_tile(S)
    RS = hpb * TQ
    NM = max(1, TQ // TK)
    GP = _lane_groups(P)
    PG = P // GP

    def body(q_ref, k_ref, v_ref, g_ref, o_ref, y_ref):
        i = pl.program_id(2)
        lane_head = _lane_head(Dh)
        msuf = _tri(TK, lambda r, c: r > c)
        qs = [(_stack_heads(q_ref[:, _lanes(u)], lane_head, hpb) * scale).astype(BF16) for u in range(GP)]
        nt = (i * TQ + TQ - 2) // TK + 1

        def tile(jt, carry, masked, r0=0):
            off = pl.multiple_of(jt * TK, TK)
            if masked:
                trow, col = _causal_iotas(RS, TK, TQ, r0)
                msk = col + (jt * TK - i * TQ) < trow
            out = []
            for u, (rem_all, acc_all) in enumerate(carry):
                rem, acc = _live_rows(rem_all, r0, hpb), _live_rows(acc_all, r0, hpb)
                kj = k_ref[pl.ds(off, TK), _lanes(u)].astype(BF16)
                vj = v_ref[pl.ds(off, TK), _lanes(u)].astype(BF16)
                lb, lr = _logsig_parts(_dot(_live_rows(qs[u], r0, hpb), kj, _NT))
                if masked:
                    lr = jnp.where(msk, lr, 0.0)
                w = jnp.exp(lb + _split_dot(lr, msuf) + rem)
                if masked:
                    w = jnp.where(msk, w, 0.0)
                out.append((_put_rows(rem_all, rem + jnp.sum(lr, axis=1, keepdims=True), r0, hpb),
                            _put_rows(acc_all, acc + _dot(w.astype(BF16), vj), r0, hpb)))
            return tuple(out)

        zero = (jnp.zeros((RS, 1), F32), jnp.zeros((RS, LANES), F32))
        carry = (zero,) * GP
        for m in range(NM):
            carry = tile(nt - 1 - m, carry, True, _first_live_row(m, TQ, TK))
        carry = lax.fori_loop(NM, nt, lambda jj, c: tile(nt - 1 - jj, c, False), carry)
        for u in range(GP):
            o = _unstack_heads(carry[u][1], lane_head, hpb)
            o_ref[:, _lanes(u)] = o
            y_ref[:, _lanes(u)] = (o * _silu(g_ref[:, _lanes(u)])).astype(BF16)

    LW = GP * LANES
    blk = lambda sec: pl.BlockSpec((None, TQ, LW), lambda b, p, i: (b, i, sec * PG + p))
    full = lambda sec: pl.BlockSpec((None, S, LW), lambda b, p, i: (b, 0, sec * PG + p))
    out = pl.BlockSpec((None, TQ, LW), lambda b, p, i: (b, i, p))
    return _call_hosting(
        body, exch, "sb_fwd", (B, PG, NQ), [blk(0), full(1), full(2), blk(3)], [out, out],
        [jax.ShapeDtypeStruct((B, S, W), F32), jax.ShapeDtypeStruct((B, S, W), BF16)], [],
        (proj3, proj3, proj3, proj3))


def _sb_bwd(proj3, o, dy, W, heads, exch):
    B, S, _ = proj3.shape
    Dh = W // heads
    hpb = LANES // Dh
    P, TQ = W // LANES, _query_tile(S)
    NQ = S // TQ
    scale = 1.0 / math.sqrt(Dh)

    TK = _key_tile(S)
    RS = hpb * TQ
    NM = max(1, TQ // TK)

    def body(q_ref, k_ref, v_ref, g_ref, o_ref, dy_ref, dp_ref, dk_ref, dv_ref, u_ref, sig_ref, es_ref):
        i = pl.program_id(2)
        rows = pl.ds(pl.multiple_of(i * TQ, TQ), TQ)

        @pl.when(i == 0)
        def _():
            dk_ref[...] = jnp.zeros_like(dk_ref)
            dv_ref[...] = jnp.zeros_like(dv_ref)

        lane_head = _lane_head(Dh)
        msuf = _tri(TK, lambda r, c: r > c)
        mpre = _tri(TK, lambda r, c: r < c)
        g = g_ref[...]
        dyv = dy_ref[...].astype(F32)
        dp_ref[3, rows, :] = (dyv * o_ref[...] * _dsilu(g)).astype(dp_ref.dtype)
        qs = (_stack_heads(q_ref[...], lane_head, hpb) * scale).astype(BF16)
        dos = _stack_heads(dyv * _silu(g), lane_head, hpb).astype(BF16)
        nt = (i * TQ + TQ - 2) // TK + 1

        def weights(jt, rem_all, masked, r0=0):
            off = pl.multiple_of(jt * TK, TK)
            kj = k_ref[pl.ds(off, TK), :].astype(BF16)
            vj = v_ref[pl.ds(off, TK), :].astype(BF16)
            dos_l = _live_rows(dos, r0, hpb)
            lb, lr = _logsig_parts(_dot(_live_rows(qs, r0, hpb), kj, _NT))
            if masked:
                trow, col = _causal_iotas(RS, TK, TQ, r0)
                msk = col + (jt * TK - i * TQ) < trow
                lr = jnp.where(msk, lr, 0.0)
            w = jnp.exp(lb + _split_dot(lr, msuf) + _live_rows(rem_all, r0, hpb))
            if masked:
                w = jnp.where(msk, w, 0.0)
            e = w * _dot(dos_l, vj, _NT)
            sig = jnp.exp(lb)
            u = e - sig * (e + _split_dot(e, mpre))
            if masked:
                u = jnp.where(msk, u, 0.0)
                sig = jnp.where(msk, sig, 0.0)
            n = TQ - r0
            for h in range(hpb):
                u_ref[jt, h * TQ + r0:(h + 1) * TQ, :] = u[h * n:(h + 1) * n]
                sig_ref[jt, h * TQ + r0:(h + 1) * TQ, :] = sig[h * n:(h + 1) * n]
            es_ref[jt] = _put_rows(jnp.zeros((RS, 1), F32), jnp.sum(e, axis=1, keepdims=True), r0, hpb)
            dv_ref[pl.ds(off, TK), :] += _dot(w.astype(BF16), dos_l, _TN)
            return _put_rows(rem_all, _live_rows(rem_all, r0, hpb) + jnp.sum(lr, axis=1, keepdims=True), r0, hpb)

        rem = jnp.zeros((RS, 1), F32)
        for m in range(NM):
            rem = weights(nt - 1 - m, rem, True, _first_live_row(m, TQ, TK))
        lax.fori_loop(NM, nt, lambda jj, r: weights(nt - 1 - jj, r, False), rem)

        def grads(jt, carry, r0=0):
            pre, acc = carry
            off = pl.multiple_of(jt * TK, TK)
            kj = k_ref[pl.ds(off, TK), :].astype(BF16)
            if r0 == 0:
                u, sig = u_ref[jt], sig_ref[jt]
            else:
                u = jnp.concatenate([u_ref[jt, h * TQ + r0:(h + 1) * TQ, :] for h in range(hpb)], axis=0)
                sig = jnp.concatenate([sig_ref[jt, h * TQ + r0:(h + 1) * TQ, :] for h in range(hpb)], axis=0)
            dz = (u - _live_rows(pre, r0, hpb) * sig).astype(BF16)
            dk_ref[pl.ds(off, TK), :] += _dot(dz, _live_rows(qs, r0, hpb), _TN)
            return pre + es_ref[jt], _put_rows(acc, _live_rows(acc, r0, hpb) + _dot(dz, kj), r0, hpb)

        carry = lax.fori_loop(0, nt - NM, grads, (jnp.zeros((RS, 1), F32), jnp.zeros((RS, LANES), F32)))
        for m in reversed(range(NM)):
            carry = grads(nt - 1 - m, carry, _first_live_row(m, TQ, TK))
        _, acc = carry
        dp_ref[0, rows, :] = (_unstack_heads(acc, lane_head, hpb) * scale).astype(dp_ref.dtype)

        @pl.when(i == NQ - 1)
        def _():
            dp_ref[1] = dk_ref[...].astype(dp_ref.dtype)
            dp_ref[2] = dv_ref[...].astype(dp_ref.dtype)

    blk = lambda sec: pl.BlockSpec((None, TQ, LANES), lambda b, p, i: (b, i, sec * P + p))
    full = lambda sec: pl.BlockSpec((None, S, LANES), lambda b, p, i: (b, 0, sec * P + p))
    one = pl.BlockSpec((None, TQ, LANES), lambda b, p, i: (b, i, p))
    (dproj,), moved = _call_hosting(
        body, exch, "sb_bwd", (B, P, NQ), [blk(0), full(1), full(2), blk(3), one, one],
        [pl.BlockSpec((None, 4, S, LANES), lambda b, p, i: (b, 0, 0, p))],
        [jax.ShapeDtypeStruct((B, 4, S, W), BF16)],
        [pltpu.VMEM((S, LANES), F32), pltpu.VMEM((S, LANES), F32),
         pltpu.VMEM((S // TK, RS, TK), F32), pltpu.VMEM((S // TK, RS, TK), F32), pltpu.VMEM((S // TK, RS, 1), F32)],
        (proj3, proj3, proj3, proj3, o, dy))
    return dproj, moved


def _fox_gate_fwd(f_t, b_f):
    B, H, S = f_t.shape

    def body(f_ref, b_ref, c_ref):
        row = lax.broadcasted_iota(jnp.int32, (BLK, BLK), 0)
        col = lax.broadcasted_iota(jnp.int32, (BLK, BLK), 1)
        mpre = (row <= col).astype(BF16)
        carry = jnp.zeros((H, 1), F32)
        for n in range(S // BLK):
            sl = pl.ds(n * BLK, BLK)
            lf, _ = _logsig_parts(f_ref[:, sl] + b_ref[...])
            c_ref[:, sl] = _split3_dot(lf, mpre) + carry
            carry = carry + jnp.sum(lf, axis=1, keepdims=True)

    spec = pl.BlockSpec((None, H, S), lambda b: (b, 0, 0))
    return pl.pallas_call(
        body, name="fox_gate_fwd", grid=(B,),
        in_specs=[spec, pl.BlockSpec((H, 1), lambda b: (0, 0))], out_specs=spec,
        out_shape=jax.ShapeDtypeStruct((B, H, S), F32),
        compiler_params=_params(("parallel",)),
    )(f_t, b_f)


def _fox_gate_bwd(dcum_t, f_t, b_f):
    B, H, S = f_t.shape

    def body(d_ref, f_ref, b_ref, df_ref, db_ref):
        b = pl.program_id(0)

        @pl.when(b == 0)
        def _():
            db_ref[...] = jnp.zeros_like(db_ref)

        row = lax.broadcasted_iota(jnp.int32, (BLK, BLK), 0)
        col = lax.broadcasted_iota(jnp.int32, (BLK, BLK), 1)
        msuf = (row >= col).astype(BF16)
        carry = jnp.zeros((H, 1), F32)
        dbacc = jnp.zeros((H, 1), F32)
        for n in reversed(range(S // BLK)):
            sl = pl.ds(n * BLK, BLK)
            dv = d_ref[:, sl]
            dlf = _split3_dot(dv, msuf) + carry
            carry = carry + jnp.sum(dv, axis=1, keepdims=True)
            df = dlf * _sigmoid(-(f_ref[:, sl] + b_ref[...]))
            df_ref[:, sl] = df
            dbacc = dbacc + jnp.sum(df, axis=1, keepdims=True)
        db_ref[...] += dbacc

    spec = pl.BlockSpec((None, H, S), lambda b: (b, 0, 0))
    vec = pl.BlockSpec((H, 1), lambda b: (0, 0))
    return pl.pallas_call(
        body, name="fox_gate_bwd", grid=(B,),
        in_specs=[spec, spec, vec], out_specs=[spec, vec],
        out_shape=[jax.ShapeDtypeStruct((B, H, S), F32), jax.ShapeDtypeStruct((H, 1), F32)],
        compiler_params=_params(("arbitrary",)),
    )(dcum_t, f_t, b_f)


def _pick_col(block, idx, lane_iota):
    return jnp.sum(jnp.where(lane_iota == idx, block, 0.0), axis=1, keepdims=True)


def _pick_row(block, idx, sub_iota):
    return jnp.sum(jnp.where(sub_iota == idx, block, 0.0), axis=0, keepdims=True)


def _fox_fwd(proj3, cum_t, W, heads):
    B, S, _ = proj3.shape
    H = heads
    Dh = W // heads
    hpb = LANES // Dh
    P, TQ = W // LANES, _query_tile(S)
    NQ = S // TQ
    scale = 1.0 / math.sqrt(Dh)

    TK = _key_tile(S, FOX_TK)
    RS = hpb * TQ
    NM = max(1, TQ // TK)

    def body(q_ref, k_ref, v_ref, g_ref, ct_ref, o_ref, y_ref, lse_ref):
        p = pl.program_id(1)
        i = pl.program_id(2)
        lane_head = _lane_head(Dh)
        sub_h = lax.broadcasted_iota(jnp.int32, (H, 1), 0)
        qs = (_stack_heads(q_ref[...], lane_head, hpb) * scale).astype(BF16)
        nt = (i * TQ + TQ - 1) // TK + 1

        def tile(jt, carry, masked, r0=0):
            mx, l, acc = carry
            n = TQ - r0
            off = pl.multiple_of(jt * TK, TK)
            kj = k_ref[pl.ds(off, TK), :].astype(BF16)
            vj = v_ref[pl.ds(off, TK), :].astype(BF16)
            ctb = ct_ref[:, pl.ds(off, TK)]
            z = _dot(_live_rows(qs, r0, hpb), kj, _NT)
            s = jnp.concatenate([z[h * n:(h + 1) * n] - _pick_row(ctb, p * hpb + h, sub_h) for h in range(hpb)],
                                axis=0)
            if masked:
                trow, col = _causal_iotas(RS, TK, TQ, r0)
                s = jnp.where(col + (jt * TK - i * TQ) <= trow, s, NEG_BIG)
            mx2 = jnp.maximum(mx, jnp.max(s, axis=1, keepdims=True))
            pe = jnp.exp(s - mx2)
            alpha = jnp.exp(mx - mx2)
            return (mx2, alpha * l + jnp.sum(pe, axis=1, keepdims=True), alpha * acc + _dot(pe.astype(BF16), vj))

        carry = lax.fori_loop(
            0, nt - NM, lambda jt, c: tile(jt, c, False),
            (jnp.full((RS, 1), NEG_BIG, F32), jnp.zeros((RS, 1), F32), jnp.zeros((RS, LANES), F32)))
        for m in reversed(range(NM)):
            carry = tile(nt - 1 - m, carry, True)
        mx, l, acc = carry
        o = _unstack_heads(acc / l, lane_head, hpb)
        o_ref[...] = o
        lse_ref[...] = _unstack_heads(jnp.broadcast_to(mx + jnp.log(l), (RS, LANES)), lane_head, hpb)
        y_ref[...] = (o * _silu(g_ref[...])).astype(BF16)

    blk = lambda sec: pl.BlockSpec((None, TQ, LANES), lambda b, p, i: (b, i, sec * P + p))
    full = lambda sec: pl.BlockSpec((None, S, LANES), lambda b, p, i: (b, 0, sec * P + p))
    out = pl.BlockSpec((None, TQ, LANES), lambda b, p, i: (b, i, p))
    return pl.pallas_call(
        body, name="fox_fwd", grid=(B, P, NQ),
        in_specs=[blk(0), full(1), full(2), blk(3),
                  pl.BlockSpec((None, H, S), lambda b, p, i: (b, 0, 0))],
        out_specs=[out, out, out],
        out_shape=[jax.ShapeDtypeStruct((B, S, W), F32), jax.ShapeDtypeStruct((B, S, W), BF16),
                   jax.ShapeDtypeStruct((B, S, W), F32)],
        compiler_params=_params(("parallel", "parallel", "arbitrary")),
    )(proj3, proj3, proj3, proj3, cum_t)


def _fox_bwd(proj3, cum_t, o, lse, dy, W, heads):
    B, S, _ = proj3.shape
    H = heads
    Dh = W // heads
    hpb = LANES // Dh
    P, TQ = W // LANES, _query_tile(S)
    NQ = S // TQ
    scale = 1.0 / math.sqrt(Dh)

    TK = _key_tile(S)
    RS = hpb * TQ
    NM = max(1, TQ // TK)

    def body(q_ref, k_ref, v_ref, g_ref, ct_ref, o_ref, lse_ref, dy_ref,
             dpj_ref, dc_ref, dk_ref, dv_ref, p_scr, dp_scr):
        p = pl.program_id(1)
        i = pl.program_id(2)
        rows = pl.ds(pl.multiple_of(i * TQ, TQ), TQ)

        @pl.when(i == 0)
        def _():
            dk_ref[...] = jnp.zeros_like(dk_ref)
            dv_ref[...] = jnp.zeros_like(dv_ref)
            dc_ref[...] = jnp.zeros_like(dc_ref)

        lane_head = _lane_head(Dh)
        sub_h = lax.broadcasted_iota(jnp.int32, (H, 1), 0)
        lane = lax.broadcasted_iota(jnp.int32, (1, LANES), 1)
        g = g_ref[...]
        lsev = lse_ref[...]
        dyv = dy_ref[...].astype(F32)
        dpj_ref[3, rows, :] = (dyv * o_ref[...] * _dsilu(g)).astype(dpj_ref.dtype)
        qs = (_stack_heads(q_ref[...], lane_head, hpb) * scale).astype(BF16)
        dos = _stack_heads(dyv * _silu(g), lane_head, hpb).astype(BF16)
        neg_lse = -jnp.concatenate([_pick_col(lsev, h * Dh, lane) for h in range(hpb)], axis=0)
        nt = (i * TQ + TQ - 1) // TK + 1

        def probs(jt, dsum, masked, r0=0):
            n = TQ - r0
            off = pl.multiple_of(jt * TK, TK)
            kj = k_ref[pl.ds(off, TK), :].astype(BF16)
            vj = v_ref[pl.ds(off, TK), :].astype(BF16)
            ctb = ct_ref[:, pl.ds(off, TK)]
            dos_l = _live_rows(dos, r0, hpb)
            z = _dot(_live_rows(qs, r0, hpb), kj, _NT) + _live_rows(neg_lse, r0, hpb)
            s = jnp.concatenate([z[h * n:(h + 1) * n] - _pick_row(ctb, p * hpb + h, sub_h) for h in range(hpb)],
                                axis=0)
            pr = jnp.exp(s)
            if masked:
                trow, col = _causal_iotas(RS, TK, TQ, r0)
                pr = jnp.where(col + (jt * TK - i * TQ) <= trow, pr, 0.0)
            dp = _dot(dos_l, vj, _NT)
            p_scr[jt] = _put_rows(jnp.zeros((RS, TK), F32), pr, r0, hpb)
            dp_scr[jt] = _put_rows(jnp.zeros((RS, TK), F32), dp, r0, hpb)
            dv_ref[pl.ds(off, TK), :] += _dot(pr.astype(BF16), dos_l, _TN)
            return _put_rows(dsum, _live_rows(dsum, r0, hpb) + jnp.sum(pr * dp, axis=1, keepdims=True), r0, hpb)

        dsum = lax.fori_loop(0, nt - NM, lambda jt, d: probs(jt, d, False), jnp.zeros((RS, 1), F32))
        for m in reversed(range(NM)):
            dsum = probs(nt - 1 - m, dsum, True)

        def grads(jt, acc, r0=0):
            n = TQ - r0
            off = pl.multiple_of(jt * TK, TK)
            kj = k_ref[pl.ds(off, TK), :].astype(BF16)
            if r0 == 0:
                pr, dp = p_scr[jt], dp_scr[jt]
            else:
                pr = jnp.concatenate([p_scr[jt, h * TQ + r0:(h + 1) * TQ, :] for h in range(hpb)], axis=0)
                dp = jnp.concatenate([dp_scr[jt, h * TQ + r0:(h + 1) * TQ, :] for h in range(hpb)], axis=0)
            ds = pr * (dp - _live_rows(dsum, r0, hpb))
            for h in range(hpb):
                dc_ref[h:h + 1, pl.ds(off, TK)] -= jnp.sum(ds[h * n:(h + 1) * n], axis=0, keepdims=True)
            dsb = ds.astype(BF16)
            dk_ref[pl.ds(off, TK), :] += _dot(dsb, _live_rows(qs, r0, hpb), _TN)
            return _put_rows(acc, _live_rows(acc, r0, hpb) + _dot(dsb, kj), r0, hpb)

        acc = lax.fori_loop(0, nt - NM, grads, jnp.zeros((RS, LANES), F32))
        for m in reversed(range(NM)):
            acc = grads(nt - 1 - m, acc, _first_live_row(m, TQ, TK))
        dpj_ref[0, rows, :] = (_unstack_heads(acc, lane_head, hpb) * scale).astype(dpj_ref.dtype)

        @pl.when(i == NQ - 1)
        def _():
            dpj_ref[1] = dk_ref[...].astype(dpj_ref.dtype)
            dpj_ref[2] = dv_ref[...].astype(dpj_ref.dtype)

    blk = lambda sec: pl.BlockSpec((None, TQ, LANES), lambda b, p, i: (b, i, sec * P + p))
    full = lambda sec: pl.BlockSpec((None, S, LANES), lambda b, p, i: (b, 0, sec * P + p))
    one = pl.BlockSpec((None, TQ, LANES), lambda b, p, i: (b, i, p))
    return pl.pallas_call(
        body, name="fox_bwd", grid=(B, P, NQ),
        in_specs=[blk(0), full(1), full(2), blk(3),
                  pl.BlockSpec((None, H, S), lambda b, p, i: (b, 0, 0)),
                  one, one, one],
        out_specs=[pl.BlockSpec((None, 4, S, LANES), lambda b, p, i: (b, 0, 0, p)),
                   pl.BlockSpec((None, None, hpb, S), lambda b, p, i: (b, p, 0, 0))],
        out_shape=[jax.ShapeDtypeStruct((B, 4, S, W), BF16), jax.ShapeDtypeStruct((B, P, hpb, S), F32)],
        scratch_shapes=[pltpu.VMEM((S, LANES), F32), pltpu.VMEM((S, LANES), F32),
                        pltpu.VMEM((S // TK, RS, TK), F32), pltpu.VMEM((S // TK, RS, TK), F32)],
        compiler_params=_params(("parallel", "parallel", "arbitrary")),
    )(proj3, proj3, proj3, proj3, cum_t, o, lse, dy)


def _layernorm_rows(v, gamma, beta):
    mu = jnp.mean(v, axis=-1, keepdims=True)
    xc = v - mu
    rstd = lax.rsqrt(jnp.mean(xc * xc, axis=-1, keepdims=True) + EPS)
    xh = xc * rstd
    return xh, rstd, xh * gamma + beta


def _layernorm_rows_bwd(dout, xh, rstd, gamma):
    dxh = dout * gamma
    return rstd * (dxh - jnp.mean(dxh, axis=-1, keepdims=True) - xh * jnp.mean(dxh * xh, axis=-1, keepdims=True))


def _gmlp_fwd(proj, wm, bs_t, ln_g, ln_b, W):
    T = proj.shape[0]
    G = wm.shape[0]
    cg = W // G
    assert cg == LANES

    def body(p_ref, wm_ref, bs_ref, lg_ref, lb_ref, y_ref, vn_ref):
        lane = lax.broadcasted_iota(jnp.int32, (1, LANES), 1)
        _, _, vn = _layernorm_rows(_gelu(p_ref[:, W:2 * W]), lg_ref[...], lb_ref[...])
        vn_ref[...] = vn.astype(BF16)
        bs = bs_ref[...]
        for g in range(G):
            sl = pl.ds(g * cg, cg)
            s = _dot(wm_ref[g], vn_ref[:, sl]) + _pick_col(bs, g, lane)
            gate = p_ref[:, pl.ds(2 * W + g * cg, cg)]
            y_ref[:, sl] = (_gelu(p_ref[:, sl]) * s * _silu(gate)).astype(BF16)

    vec = pl.BlockSpec((1, W), lambda r: (0, 0))
    return pl.pallas_call(
        body, name="gmlp_fwd", grid=(T // BLK,),
        in_specs=[pl.BlockSpec((BLK, 3 * W), lambda r: (r, 0)),
                  pl.BlockSpec((G, BLK, BLK), lambda r: (0, 0, 0)),
                  pl.BlockSpec((BLK, LANES), lambda r: (0, 0)), vec, vec],
        out_specs=pl.BlockSpec((BLK, W), lambda r: (r, 0)),
        out_shape=jax.ShapeDtypeStruct((T, W), BF16),
        scratch_shapes=[pltpu.VMEM((BLK, W), BF16)],
        compiler_params=_params(("parallel",)),
    )(proj, wm, bs_t, ln_g, ln_b)


def _gmlp_bwd(proj, dy, wm, bs_t, ln_g, ln_b, W):
    T = proj.shape[0]
    G = wm.shape[0]
    cg = W // G

    def body(p_ref, dy_ref, wm_ref, bs_ref, lg_ref, lb_ref,
             dp_ref, dwm_ref, dbs_ref, dlg_ref, dlb_ref, vn_ref, dvn_ref):
        r = pl.program_id(0)

        @pl.when(r == 0)
        def _():
            dwm_ref[...] = jnp.zeros_like(dwm_ref)
            dbs_ref[...] = jnp.zeros_like(dbs_ref)
            dlg_ref[...] = jnp.zeros_like(dlg_ref)
            dlb_ref[...] = jnp.zeros_like(dlb_ref)

        lane = lax.broadcasted_iota(jnp.int32, (1, LANES), 1)
        vpre = p_ref[:, W:2 * W]
        gamma = lg_ref[...]
        xh, rstd, vn = _layernorm_rows(_gelu(vpre), gamma, lb_ref[...])
        vn_ref[...] = vn.astype(BF16)
        bs = bs_ref[...]
        dbs = jnp.zeros((BLK, LANES), F32)
        for g in range(G):
            sl = pl.ds(g * cg, cg)
            gsl = pl.ds(2 * W + g * cg, cg)
            vng = vn_ref[:, sl]
            s = _dot(wm_ref[g], vng) + _pick_col(bs, g, lane)
            upre = p_ref[:, sl]
            u = _gelu(upre)
            gate = p_ref[:, gsl]
            dyv = dy_ref[:, sl].astype(F32)
            dp_ref[:, gsl] = (dyv * u * s * _dsilu(gate)).astype(dp_ref.dtype)
            do = dyv * _silu(gate)
            dp_ref[:, sl] = (do * s * _dgelu(upre)).astype(dp_ref.dtype)
            ds = do * u
            dbs = dbs + jnp.where(lane == g, jnp.sum(ds, axis=1, keepdims=True), 0.0)
            dsb = ds.astype(BF16)
            dwm_ref[g] += _dot(dsb, vng, _NT)
            dvn_ref[:, sl] = _dot(wm_ref[g], dsb, _TN)
        dbs_ref[...] += dbs
        dvn = dvn_ref[...]
        dlg_ref[...] += jnp.sum(dvn * xh, axis=0, keepdims=True)
        dlb_ref[...] += jnp.sum(dvn, axis=0, keepdims=True)
        dv = _layernorm_rows_bwd(dvn, xh, rstd, gamma)
        dp_ref[:, W:2 * W] = (dv * _dgelu(vpre)).astype(dp_ref.dtype)

    vec = pl.BlockSpec((1, W), lambda r: (0, 0))
    return pl.pallas_call(
        body, name="gmlp_bwd", grid=(T // BLK,),
        in_specs=[pl.BlockSpec((BLK, 3 * W), lambda r: (r, 0)),
                  pl.BlockSpec((BLK, W), lambda r: (r, 0)),
                  pl.BlockSpec((G, BLK, BLK), lambda r: (0, 0, 0)),
                  pl.BlockSpec((BLK, LANES), lambda r: (0, 0)), vec, vec],
        out_specs=[pl.BlockSpec((BLK, 3 * W), lambda r: (r, 0)),
                   pl.BlockSpec((G, BLK, BLK), lambda r: (0, 0, 0)),
                   pl.BlockSpec((BLK, LANES), lambda r: (0, 0)), vec, vec],
        out_shape=[jax.ShapeDtypeStruct((T, 3 * W), BF16), jax.ShapeDtypeStruct((G, BLK, BLK), F32),
                   jax.ShapeDtypeStruct((BLK, LANES), F32),
                   jax.ShapeDtypeStruct((1, W), F32), jax.ShapeDtypeStruct((1, W), F32)],
        scratch_shapes=[pltpu.VMEM((BLK, W), BF16), pltpu.VMEM((BLK, W), F32)],
        compiler_params=_params(("arbitrary",)),
    )(proj, dy, wm, bs_t, ln_g, ln_b)


SUBLANES = 8
SHIFT_ROWS = CONV_HALO + BLK - SUBLANES


def _shift_rows(ext_ref, sh_ref, off):
    for r in range(1, SUBLANES):
        sh_ref[r - 1] = ext_ref[pl.ds(r, SHIFT_ROWS), pl.ds(off, LANES)]


def _rows_from(ext_ref, sh_ref, off, start):
    r = start % SUBLANES
    if r == 0:
        return ext_ref[pl.ds(start, BLK), pl.ds(off, LANES)]
    return sh_ref[r - 1, pl.ds(start - r, BLK), :]


def _conv_taps(ext_ref, sh_ref, cw_ref, off, n_taps, first):
    acc = jnp.zeros((BLK, LANES), F32)
    for k in range(n_taps):
        acc = acc + cw_ref[k:k + 1, pl.ds(off, LANES)] * _rows_from(ext_ref, sh_ref, off, first + k)
    return acc


def _fill_glu_ext(ext_ref, halo_ref, cur_ref, W, first_block):
    y0h = halo_ref[:, :W] * _sigmoid(halo_ref[:, W:])
    ext_ref[0:CONV_HALO, :] = jnp.where(first_block, 0.0, y0h)
    ext_ref[CONV_HALO:CONV_HALO + BLK, :] = cur_ref[:, :W] * _sigmoid(cur_ref[:, W:])


def _conv_specs(S, W):
    per = BLK // CONV_HALO
    cur = pl.BlockSpec((None, BLK, 2 * W), lambda b, i: (b, i, 0))
    halo = pl.BlockSpec((None, CONV_HALO, 2 * W), lambda b, i: (b, jnp.maximum(i * per - 1, 0), 0))
    gate = pl.BlockSpec((None, BLK, W), lambda b, i: (b, i, 2))
    return cur, halo, gate


def _conv_fwd(proj3, cw, cb, ln_g, ln_b, W, exch):
    B, S, _ = proj3.shape
    K = cw.shape[0]
    first = CONV_HALO - (K - 1)
    assert first >= 0

    def body(cur_ref, halo_ref, g_ref, cw_ref, cb_ref, lg_ref, lb_ref, y_ref, y1_ref, ext_ref, sh_ref):
        i = pl.program_id(1)
        _fill_glu_ext(ext_ref, halo_ref, cur_ref, W, i == 0)

        def chan(c, _):
            off = pl.multiple_of(c * LANES, LANES)
            _shift_rows(ext_ref, sh_ref, off)
            y1_ref[:, pl.ds(off, LANES)] = (_conv_taps(ext_ref, sh_ref, cw_ref, off, K, first)
                                            + cb_ref[:, pl.ds(off, LANES)])
            return 0

        lax.fori_loop(0, W // LANES, chan, 0)
        _, _, ln = _layernorm_rows(y1_ref[...], lg_ref[...], lb_ref[...])
        y_ref[...] = (_silu(ln) * _silu(g_ref[...])).astype(BF16)

    cur, halo, gate = _conv_specs(S, W)
    vec = pl.BlockSpec((1, W), lambda b, i: (0, 0))
    one = pl.BlockSpec((None, BLK, W), lambda b, i: (b, i, 0))
    (y, y1), moved = _call_hosting(
        body, exch, "conv_fwd", (B, S // BLK),
        [cur, halo, gate, pl.BlockSpec((K, W), lambda b, i: (0, 0)), vec, vec, vec],
        [one, one], [jax.ShapeDtypeStruct((B, S, W), BF16), jax.ShapeDtypeStruct((B, S, W), F32)],
        [pltpu.VMEM((CONV_HALO + BLK, W), F32), pltpu.VMEM((SUBLANES - 1, SHIFT_ROWS, LANES), F32)],
        (proj3, proj3, proj3, cw, cb, ln_g, ln_b))
    return y, y1, moved


def _conv_bwd1(proj3, y1, dy, K, ln_g, ln_b, W, exch):
    B, S, _ = proj3.shape
    first = CONV_HALO - (K - 1)

    def body(cur_ref, halo_ref, g_ref, y1_ref, dy_ref, lg_ref, lb_ref,
             dy1_ref, dg_ref, dcw_ref, dcb_ref, dlg_ref, dlb_ref, ext_ref, sh_ref):
        b = pl.program_id(0)
        i = pl.program_id(1)

        @pl.when(jnp.logical_and(b == 0, i == 0))
        def _():
            dcw_ref[...] = jnp.zeros_like(dcw_ref)
            dcb_ref[...] = jnp.zeros_like(dcb_ref)
            dlg_ref[...] = jnp.zeros_like(dlg_ref)
            dlb_ref[...] = jnp.zeros_like(dlb_ref)

        _fill_glu_ext(ext_ref, halo_ref, cur_ref, W, i == 0)
        gamma = lg_ref[...]
        xh, rstd, ln = _layernorm_rows(y1_ref[...], gamma, lb_ref[...])
        g = g_ref[...]
        dyv = dy_ref[...].astype(F32)
        dg_ref[...] = (dyv * _silu(ln) * _dsilu(g)).astype(dg_ref.dtype)
        dln = dyv * _silu(g) * _dsilu(ln)
        dlg_ref[...] += jnp.sum(dln * xh, axis=0, keepdims=True)
        dlb_ref[...] += jnp.sum(dln, axis=0, keepdims=True)
        dy1 = _layernorm_rows_bwd(dln, xh, rstd, gamma)
        dy1_ref[...] = dy1
        dcb_ref[...] += jnp.sum(dy1, axis=0, keepdims=True)

        def chan_w(c, _):
            off = pl.multiple_of(c * LANES, LANES)
            _shift_rows(ext_ref, sh_ref, off)
            d = dy1_ref[:, pl.ds(off, LANES)]
            for k in range(K):
                dcw_ref[k:k + 1, pl.ds(off, LANES)] += jnp.sum(
                    d * _rows_from(ext_ref, sh_ref, off, first + k), axis=0, keepdims=True)
            return 0

        lax.fori_loop(0, W // LANES, chan_w, 0)

    cur, halo, gate = _conv_specs(S, W)
    vec = pl.BlockSpec((1, W), lambda b, i: (0, 0))
    taps = pl.BlockSpec((K, W), lambda b, i: (0, 0))
    one = pl.BlockSpec((None, BLK, W), lambda b, i: (b, i, 0))
    return _call_hosting(
        body, exch, "conv_bwd1", (B, S // BLK), [cur, halo, gate, one, one, vec, vec],
        [one, one, taps, vec, vec, vec],
        [jax.ShapeDtypeStruct((B, S, W), F32), jax.ShapeDtypeStruct((B, S, W), BF16),
         jax.ShapeDtypeStruct((K, W), F32)] + [jax.ShapeDtypeStruct((1, W), F32)] * 3,
        [pltpu.VMEM((CONV_HALO + BLK, W), F32), pltpu.VMEM((SUBLANES - 1, SHIFT_ROWS, LANES), F32)],
        (proj3, proj3, proj3, y1, dy, ln_g, ln_b))


def _conv_bwd2(proj3, dy1, dgate, cw_rev, W):
    B, S, _ = proj3.shape
    K = cw_rev.shape[0]
    NQ = S // BLK
    per = BLK // CONV_HALO

    def body(cur_ref, d_ref, dnext_ref, dgate_ref, cw_ref, dp_ref, ext_ref, dy0_ref, sh_ref):
        i = pl.program_id(1)
        ext_ref[0:BLK, :] = d_ref[...]
        ext_ref[BLK:BLK + CONV_HALO, :] = jnp.where(i == NQ - 1, 0.0, dnext_ref[...])

        def chan(c, _):
            off = pl.multiple_of(c * LANES, LANES)
            _shift_rows(ext_ref, sh_ref, off)
            dy0_ref[:, pl.ds(off, LANES)] = _conv_taps(ext_ref, sh_ref, cw_ref, off, K, 0)
            return 0

        lax.fori_loop(0, W // LANES, chan, 0)
        a = cur_ref[:, :W]
        sg = _sigmoid(cur_ref[:, W:])
        dy0 = dy0_ref[...]
        dp_ref[:, 0:W] = (dy0 * sg).astype(dp_ref.dtype)
        dp_ref[:, W:2 * W] = (dy0 * a * sg * (1.0 - sg)).astype(dp_ref.dtype)
        dp_ref[:, 2 * W:3 * W] = dgate_ref[...]

    cur = pl.BlockSpec((None, BLK, 2 * W), lambda b, i: (b, i, 0))
    one = pl.BlockSpec((None, BLK, W), lambda b, i: (b, i, 0))
    nxt = pl.BlockSpec((None, CONV_HALO, W), lambda b, i: (b, jnp.minimum((i + 1) * per, S // CONV_HALO - 1), 0))
    return pl.pallas_call(
        body, name="conv_bwd2", grid=(B, NQ),
        in_specs=[cur, one, nxt, one, pl.BlockSpec((K, W), lambda b, i: (0, 0))],
        out_specs=pl.BlockSpec((None, BLK, 3 * W), lambda b, i: (b, i, 0)),
        out_shape=jax.ShapeDtypeStruct((B, S, 3 * W), BF16),
        scratch_shapes=[pltpu.VMEM((BLK + CONV_HALO, W), F32), pltpu.VMEM((BLK, W), F32),
                        pltpu.VMEM((SUBLANES - 1, SHIFT_ROWS, LANES), F32)],
        compiler_params=_params(("parallel", "parallel")),
    )(proj3, dy1, dy1, dgate, cw_rev)


def _pack(arrays):
    flat = jnp.concatenate([a.astype(F32).reshape(-1) for a in arrays])
    n = flat.shape[0]
    pad = (-n) % (8 * LANES)
    if pad:
        flat = jnp.concatenate([flat, jnp.zeros((pad,), F32)])
    return flat.reshape(-1, LANES)


def _unpack(packed, shapes, lead=()):
    flat = packed.reshape(lead + (-1,))
    out, off = [], 0
    for shp in shapes:
        n = math.prod(shp)
        out.append(flat[..., off:off + n].reshape(lead + tuple(shp)))
        off += n
    return out


def _cols_from_dev(g):
    g = jnp.moveaxis(g, 0, -2)
    return g.reshape(g.shape[:-2] + (g.shape[-2] * g.shape[-1],))


def _my_cols(full, me):
    n8 = full.shape[-1] // N_DEV
    return lax.dynamic_slice_in_dim(full, me * n8, n8, axis=full.ndim - 1)


def kernel(x, a_norm, a_w_in, a_w_out, b_norm, b_w_in, b_v_ln_g, b_v_ln_b, b_w_s, b_b_s, b_w_out, c_norm, c_w_in, c_conv_w, c_conv_b, c_ln_g, c_ln_b, c_w_out, d_norm, d_w_in, d_b_f, d_w_out, final_norm, loss_target, m_a_norm, m_a_w_in, m_a_w_out, m_b_norm, m_b_w_in, m_b_v_ln_g, m_b_v_ln_b, m_b_w_s, m_b_b_s, m_b_w_out, m_c_norm, m_c_w_in, m_c_conv_w, m_c_conv_b, m_c_ln_g, m_c_ln_b, m_c_w_out, m_d_norm, m_d_w_in, m_d_b_f, m_d_w_out, m_final_norm, v_a_norm, v_a_w_in, v_a_w_out, v_b_norm, v_b_w_in, v_b_v_ln_g, v_b_v_ln_b, v_b_w_s, v_b_b_s, v_b_w_out, v_c_norm, v_c_w_in, v_c_conv_w, v_c_conv_b, v_c_ln_g, v_c_ln_b, v_c_w_out, v_d_norm, v_d_w_in, v_d_b_f, v_d_w_out, v_final_norm):
    B, S, D = x.shape
    T = B * S
    xi, yi, ci = _me()
    me = 4 * xi + 2 * yi + ci

    G = b_w_s.shape[1]
    KC = c_conv_w.shape[1]
    H_D = d_b_f.shape[1]
    W_A = a_w_out.shape[1] * N_DEV
    W_B = b_w_out.shape[1] * N_DEV
    W_C = c_w_out.shape[1] * N_DEV
    W_D = d_w_out.shape[1] * N_DEV
    N_D = d_w_in.shape[2] * N_DEV
    N_D_PAD = -(-N_D // (3 * LANES)) * (3 * LANES)

    big_names = ["a_w_in", "a_w_out", "b_w_in", "b_w_out", "c_w_in", "c_w_out", "d_w_in", "d_w_out"]
    big_w = dict(a_w_in=a_w_in[0], a_w_out=a_w_out[0], b_w_in=b_w_in[0], b_w_out=b_w_out[0],
                 c_w_in=c_w_in[0], c_w_out=c_w_out[0], d_w_in=d_w_in[0], d_w_out=d_w_out[0])
    small_sharded = [b_norm, b_v_ln_g, b_v_ln_b, c_norm, c_conv_w, c_conv_b, c_ln_g, c_ln_b, d_norm]
    first_names, later_names, last_names = big_names[:1], big_names[1:6], big_names[6:]
    x0 = x.reshape(T, D)
    h_a, gathered = _rmsnorm_fwd(
        x0, a_norm, "rms_a", _GatherViaSibling([big_w[n].astype(BF16) for n in first_names] + [_pack(small_sharded)]))
    wg = dict(zip(first_names, gathered[:-1]))
    (b_norm_f, b_lg_f, b_lb_f, c_norm_f, c_cw_f, c_cb_f, c_lg_f, c_lb_f, d_norm_f) = [
        _cols_from_dev(t) for t in _unpack(gathered[-1], [s.shape for s in small_sharded], lead=(N_DEV,))]
    c_cw_f = c_cw_f[0]

    wm = jnp.tril(b_w_s[0]).astype(BF16)
    bs_t = jnp.pad(b_b_s[0].T, ((0, 0), (0, LANES - G)))

    proj_a =_mm_w_dev(h_a, wg["a_w_in"], "proj_a").reshape(B, S, 4 * W_A)
    (o_a, y_a), later = _sb_fwd(proj_a, W_A, SB_HEADS,
                                _Exchange([big_w[n].astype(BF16) for n in later_names], ["gather"] * len(later_names)))
    wg.update(zip(later_names, later))
    a_w_out_f = wg["a_w_out"].reshape(W_A, D)
    b_w_out_f = wg["b_w_out"].reshape(W_B, D)
    c_w_out_f = wg["c_w_out"].reshape(W_C, D)
    y_a = y_a.reshape(T, W_A)
    x1, h_b = _mm(y_a, a_w_out_f, "nn", T, D, W_A, F32, "out_a", 512, D, W_A, res=x0, norm_gain=b_norm_f)
    proj_b = _mm_w_dev(h_b, wg["b_w_in"], "proj_b")
    y_b = _gmlp_fwd(proj_b, wm, bs_t, b_lg_f, b_lb_f, W_B)
    x2, h_c = _mm(y_b, b_w_out_f, "nn", T, D, W_B, F32, "out_b", 512, D, W_B, res=x1, norm_gain=c_norm_f)
    proj_c = _mm_w_dev(h_c, wg["c_w_in"], "proj_c").reshape(B, S, 3 * W_C)
    y_c, y1_c, last = _conv_fwd(
        proj_c, c_cw_f, c_cb_f, c_lg_f, c_lb_f, W_C,
        _Exchange([big_w[n].astype(BF16) for n in last_names], ["gather"] * len(last_names)))
    wg.update(zip(last_names, last))
    d_w_out_f = wg["d_w_out"].reshape(W_D, D)
    d_w_in_f = jnp.pad(_cols_from_dev(wg["d_w_in"]), ((0, 0), (0, N_D_PAD - N_D)))
    y_c = y_c.reshape(T, W_C)
    x3, h_d = _mm(y_c, c_w_out_f, "nn", T, D, W_C, F32, "out_c", 512, D, W_C, res=x2, norm_gain=d_norm_f)
    proj_d = _mm(h_d, d_w_in_f, "nn", T, N_D_PAD, D, F32, "proj_d", 1024, 384, D).reshape(B, S, N_D_PAD)
    f_t = jnp.swapaxes(proj_d[:, :, 4 * W_D:4 * W_D + H_D], 1, 2)
    b_f_col = d_b_f.reshape(H_D, 1)
    cum_t = _fox_gate_fwd(f_t, b_f_col)
    o_d, y_d, lse_d = _fox_fwd(proj_d, cum_t, W_D, H_D)
    y_d = y_d.reshape(T, W_D)
    x4 = _mm(y_d, d_w_out_f, "nn", T, D, W_D, F32, "out_d", 512, D, W_D, res=x3)

    loss_part, dx, g_final = _loss_head(x4, final_norm.reshape(1, D), loss_target.reshape(T, D))
    loss = lax.psum(loss_part[0, 0], MESH_AXES)

    dy_d = _mm(dx, d_w_out_f, "nt", T, W_D, D, BF16, "dy_d", 512, W_D, D).reshape(B, S, W_D)
    gw_d_out = _mm(y_d, dx, "tn", W_D, D, T, BF16, "gw_d_out", W_D, D, 512).reshape(N_DEV, W_D // N_DEV, D)
    dproj_d, dcum = _fox_bwd(proj_d, cum_t, o_d, lse_d, dy_d, W_D, H_D)
    df_t, g_b_f = _fox_gate_bwd(dcum.reshape(B, H_D, S), f_t, b_f_col)
    F_PAD = N_D_PAD - 4 * W_D
    df = jnp.pad(jnp.swapaxes(df_t, 1, 2), ((0, 0), (0, 0), (0, F_PAD - H_D))).reshape(T, F_PAD)
    tc, tr = min(512, W_D), min(1024, S)
    gw_main = _mm(h_d, dproj_d, "tn", D, 4 * W_D, T, BF16, "gw_d_in", D, tc, tr,
                  b_spec=_sectioned_spec(dproj_d, tr, tc, 2, 1))
    gw_f = _mm(h_d, df, "tn", D, F_PAD, T, BF16, "gw_d_in_f", D, F_PAD, 512)
    gw_d_in = jnp.moveaxis(
        jnp.concatenate([gw_main, gw_f], axis=1)[:, :N_D].reshape(D, N_DEV, N_D // N_DEV), 1, 0)
    dh_f = _mm(df, d_w_in_f[:, 4 * W_D:], "nt", T, D, F_PAD, F32, "dh_d_f", 512, D, F_PAD)
    dx, g_d_norm, _ = _mm(dproj_d, d_w_in_f, "nt", T, D, 4 * W_D, F32, "dh_d", tr, D, tc,
                          a_spec=_sectioned_spec(dproj_d, tr, tc, 0, 2), res=dh_f, norm_bwd=(x3, d_norm_f, dx))

    dy_c = _mm(dx, c_w_out_f, "nt", T, W_C, D, BF16, "dy_c", 512, W_C, D).reshape(B, S, W_C)
    gw_c_out = _mm(y_c, dx, "tn", W_C, D, T, BF16, "gw_c_out", 1024, D, 512).reshape(N_DEV, W_C // N_DEV, D)
    (dy1, dgate_c, g_c_cw, g_c_cb, g_c_lg, g_c_lb), parts_d = _conv_bwd1(
        proj_c, y1_c, dy_c, KC, c_lg_f, c_lb_f, W_C, _Exchange([gw_d_in, gw_d_out], ["scatter"] * 2))
    dproj_c = _conv_bwd2(proj_c, dy1, dgate_c, c_cw_f[::-1], W_C).reshape(T, 3 * W_C)
    gw_c_in = _mm_grad_dev(h_c, dproj_c, "gw_c_in")
    dx, g_c_norm, _ = _mm_wT_dev(dproj_c, wg["c_w_in"], "dh_c", norm_bwd=(x2, c_norm_f, dx))

    dy_b = _mm(dx, b_w_out_f, "nt", T, W_B, D, BF16, "dy_b", 512, W_B, D)
    gw_b_out = _mm(y_b, dx, "tn", W_B, D, T, BF16, "gw_b_out", 1024, D, 512).reshape(N_DEV, W_B // N_DEV, D)
    dproj_b, g_wm, g_bs_t, g_b_lg, g_b_lb = _gmlp_bwd(proj_b, dy_b, wm, bs_t, b_lg_f, b_lb_f, W_B)
    g_b_w_s = jnp.tril(g_wm)
    g_b_b_s = g_bs_t[:, :G].T
    gw_b_in = _mm_grad_dev(h_b, dproj_b, "gw_b_in")
    dx, g_b_norm, _ = _mm_wT_dev(dproj_b, wg["b_w_in"], "dh_b", norm_bwd=(x1, b_norm_f, dx))

    dy_a = _mm(dx, a_w_out_f, "nt", T, W_A, D, BF16, "dy_a", 512, W_A, D).reshape(B, S, W_A)
    gw_a_out = _mm(y_a, dx, "tn", W_A, D, T, BF16, "gw_a_out", W_A, D, 512).reshape(N_DEV, W_A // N_DEV, D)
    small_full = [g_b_norm, g_b_lg, g_b_lb, g_b_b_s, g_c_norm, g_c_cw, g_c_cb, g_c_lg, g_c_lb,
                  g_d_norm, g_b_f, g_final]
    dproj_a, parts_s = _sb_bwd(
        proj_a, o_a, dy_a, W_A, SB_HEADS,
        _Exchange([gw_c_in, gw_c_out, gw_b_in, gw_b_out, gw_a_out, _pack(small_full), g_b_w_s.reshape(-1, LANES)],
                  ["scatter"] * 5 + ["gather"] * 2))
    gw_a_in = _mm_grad_dev(h_a, dproj_a, "gw_a_in")
    dx, g_a_norm, parts_a = _mm_wT_dev(dproj_a, wg["a_w_in"], "dh_a", exch=_Exchange([gw_a_in], ["scatter"]),
                                       norm_bwd=(x0, a_norm, dx))
    grad_x = dx.reshape(B, S, D)

    (parts_n,) = _Exchange([_pack([g_a_norm])], ["gather"]).run("exchange_last")
    big_parts = dict(a_w_in=parts_a[0], a_w_out=parts_s[4], b_w_in=parts_s[2], b_w_out=parts_s[3],
                     c_w_in=parts_s[0], c_w_out=parts_s[1], d_w_in=parts_d[0], d_w_out=parts_d[1])
    (s_b_norm, s_b_lg, s_b_lb, s_b_b_s, s_c_norm, s_c_cw, s_c_cb, s_c_lg, s_c_lb,
     s_d_norm, s_b_f, s_final) = _unpack(_sum_parts(parts_s[5], "sum_small"), [g.shape for g in small_full])
    (s_a_norm,) = _unpack(_sum_parts(parts_n, "sum_a_norm"), [g_a_norm.shape])

    weights = dict(a_norm=a_norm, a_w_in=a_w_in, a_w_out=a_w_out, b_norm=b_norm, b_w_in=b_w_in, b_v_ln_g=b_v_ln_g,
                   b_v_ln_b=b_v_ln_b, b_w_s=b_w_s, b_b_s=b_b_s, b_w_out=b_w_out, c_norm=c_norm, c_w_in=c_w_in,
                   c_conv_w=c_conv_w, c_conv_b=c_conv_b, c_ln_g=c_ln_g, c_ln_b=c_ln_b, c_w_out=c_w_out,
                   d_norm=d_norm, d_w_in=d_w_in, d_b_f=d_b_f, d_w_out=d_w_out, final_norm=final_norm)
    mom_m = dict(a_norm=m_a_norm, a_w_in=m_a_w_in, a_w_out=m_a_w_out, b_norm=m_b_norm, b_w_in=m_b_w_in,
                 b_v_ln_g=m_b_v_ln_g, b_v_ln_b=m_b_v_ln_b, b_w_s=m_b_w_s, b_b_s=m_b_b_s, b_w_out=m_b_w_out,
                 c_norm=m_c_norm, c_w_in=m_c_w_in, c_conv_w=m_c_conv_w, c_conv_b=m_c_conv_b, c_ln_g=m_c_ln_g,
                 c_ln_b=m_c_ln_b, c_w_out=m_c_w_out, d_norm=m_d_norm, d_w_in=m_d_w_in, d_b_f=m_d_b_f,
                 d_w_out=m_d_w_out, final_norm=m_final_norm)
    mom_v = dict(a_norm=v_a_norm, a_w_in=v_a_w_in, a_w_out=v_a_w_out, b_norm=v_b_norm, b_w_in=v_b_w_in,
                 b_v_ln_g=v_b_v_ln_g, b_v_ln_b=v_b_v_ln_b, b_w_s=v_b_w_s, b_b_s=v_b_b_s, b_w_out=v_b_w_out,
                 c_norm=v_c_norm, c_w_in=v_c_w_in, c_conv_w=v_c_conv_w, c_conv_b=v_c_conv_b, c_ln_g=v_c_ln_g,
                 c_ln_b=v_c_ln_b, c_w_out=v_c_w_out, d_norm=v_d_norm, d_w_in=v_d_w_in, d_b_f=v_d_b_f,
                 d_w_out=v_d_w_out, final_norm=v_final_norm)
    order = list(weights)
    grads, deltas, new_m, new_v = {}, {}, {}, {}

    for n in big_names:
        part = big_parts[n]
        shp = weights[n].shape
        R, C = shp[1], shp[2]
        res = _adamw(part, weights[n].reshape(R, C), mom_m[n].reshape(R, C), mom_v[n].reshape(R, C), "adamw_" + n)
        grads[n], deltas[n], new_m[n], new_v[n] = [r.reshape(shp) for r in res]

    res = _adamw(parts_s[6], b_w_s.reshape(-1, LANES), m_b_w_s.reshape(-1, LANES), v_b_w_s.reshape(-1, LANES),
                 "adamw_b_w_s")
    grads["b_w_s"], deltas["b_w_s"], new_m["b_w_s"], new_v["b_w_s"] = [r.reshape(b_w_s.shape) for r in res]

    small_g = dict(
        a_norm=s_a_norm, b_norm=_my_cols(s_b_norm, me), b_v_ln_g=_my_cols(s_b_lg, me),
        b_v_ln_b=_my_cols(s_b_lb, me), b_b_s=s_b_b_s[None], c_norm=_my_cols(s_c_norm, me),
        c_conv_w=_my_cols(s_c_cw, me)[None], c_conv_b=_my_cols(s_c_cb, me), c_ln_g=_my_cols(s_c_lg, me),
        c_ln_b=_my_cols(s_c_lb, me), d_norm=_my_cols(s_d_norm, me), d_b_f=s_b_f.reshape(1, H_D),
        final_norm=s_final.reshape(D))
    small_names = list(small_g)
    sg_p = _pack([small_g[n] for n in small_names])
    res = _adamw(sg_p[None], _pack([weights[n] for n in small_names]), _pack([mom_m[n] for n in small_names]),
                 _pack([mom_v[n] for n in small_names]), "adamw_small")
    shapes = [weights[n].shape for n in small_names]
    for dst, r in zip((grads, deltas, new_m, new_v), res):
        for n, val in zip(small_names, _unpack(r, shapes)):
            dst[n] = val

    return (loss, grad_x, *[grads[n] for n in order], *[deltas[n] for n in order],
            *[new_m[n] for n in order], *[new_v[n] for n in order])
```

```python
import functools
import math

import jax
import jax.numpy as jnp
from jax import lax
from jax.experimental import pallas as pl
from jax.experimental.pallas import tpu as pltpu

F32 = jnp.float32
BF16 = jnp.bfloat16

EPS = 1e-6
SB_HEADS = 16
CONV_HALO = 32
BLK = 128
ATT_TK = 256
FOX_TK = 512
ATT_TQ = 512
ATT_GP = 2
LANES = 128
N_DEV = 8
MESH_AXES = ("x", "y", "c")

ADAM_LR = 0.001
ADAM_B1 = 0.9
ADAM_B2 = 0.999
ADAM_EPS = 1e-08
ADAM_WD = 0.01
ADAM_STEP = 10

VMEM_LIMIT = 56 * 1024 * 1024
NEG_BIG = -1e30

_NN = (((1,), (0,)), ((), ()))
_NT = (((1,), (1,)), ((), ()))
_TN = (((0,), (0,)), ((), ()))


def _dot(a, b, dims=_NN):
    return lax.dot_general(a, b, dims, preferred_element_type=F32)


def _split_dot(x, m):
    hi = lax.bitcast_convert_type(lax.bitcast_convert_type(x, jnp.int32) & jnp.int32(-65536), F32)
    return _dot(hi.astype(BF16), m) + _dot((x - hi).astype(BF16), m)


def _split3_dot(x, m):
    hi = x.astype(BF16)
    r1 = x - hi.astype(F32)
    mid = r1.astype(BF16)
    lo = (r1 - mid.astype(F32)).astype(BF16)
    return _dot(hi, m) + _dot(mid, m) + _dot(lo, m)


def _params(sem=None):
    kw = dict(vmem_limit_bytes=VMEM_LIMIT)
    if sem is not None:
        kw["dimension_semantics"] = sem
    return pltpu.CompilerParams(**kw)


def _sigmoid(x):
    return jax.nn.sigmoid(x)


def _silu(x):
    return x * _sigmoid(x)


def _dsilu(x):
    s = _sigmoid(x)
    return s * (1.0 + x * (1.0 - s))


_GELU_C = math.sqrt(2.0 / math.pi)


def _gelu(x):
    return 0.5 * x * (1.0 + jnp.tanh(_GELU_C * (x + 0.044715 * x * x * x)))


def _dgelu(x):
    th = jnp.tanh(_GELU_C * (x + 0.044715 * x * x * x))
    return 0.5 * (1.0 + th) + 0.5 * x * (1.0 - th * th) * _GELU_C * (1.0 + 3.0 * 0.044715 * x * x)


def _mm(a, b, mode, M, N, K, out_dtype, name, tm, tn, tk, a_spec=None, b_spec=None, o_spec=None, out_shape=None,
        exch=None, res=None, norm_gain=None, norm_bwd=None):
    tm, tn, tk = min(tm, M), min(tn, N), min(tk, K)
    assert M % tm == 0 and N % tn == 0 and K % tk == 0, (name, M, N, K, tm, tn, tk)
    nk = K // tk
    assert norm_gain is None or (nk == 1 and tn == N and exch is None)
    dims = {"nn": _NN, "nt": _NT, "tn": _TN}[mode]
    if a_spec is None:
        a_spec = (pl.BlockSpec((tk, tm), lambda i, j, k: (k, i)) if mode == "tn"
                  else pl.BlockSpec((tm, tk), lambda i, j, k: (i, k)))
    if b_spec is None:
        b_spec = (pl.BlockSpec((tn, tk), lambda i, j, k: (j, k)) if mode == "nt"
                  else pl.BlockSpec((tk, tn), lambda i, j, k: (k, j)))
    if o_spec is None:
        o_spec = pl.BlockSpec((tm, tn), lambda i, j, k: (i, j))
    if out_shape is None:
        out_shape = (M, N)

    def body(a_ref, b_ref, *rest):
        res_ref = rest[0] if res is not None else None
        if nk == 1:
            d = _dot(a_ref[...].astype(BF16), b_ref[...].astype(BF16), dims)
            r = d if res_ref is None else res_ref[...].astype(F32) + d
            if norm_gain is None:
                rest[-1][...] = r.astype(rest[-1].dtype)
            else:
                g_ref, o_ref, h_ref = rest[-3:]
                o_ref[...] = r.astype(o_ref.dtype)
                scale = lax.rsqrt(jnp.mean(r * r, axis=-1, keepdims=True) + EPS)
                h_ref[...] = (r * scale * g_ref[...]).astype(BF16)
            return
        i = pl.program_id(0)
        k = pl.program_id(2)
        if norm_bwd is not None:
            x_ref, g_ref, dres_ref, o_ref, dg_ref, acc_ref = rest[-6:]

            @pl.when(jnp.logical_and(i == 0, k == 0))
            def _():
                dg_ref[...] = jnp.zeros_like(dg_ref)
        else:
            o_ref, acc_ref = rest[-2:]

        @pl.when(k == 0)
        def _():
            acc_ref[...] = jnp.zeros_like(acc_ref) if res_ref is None else res_ref[...].astype(F32)

        acc_ref[...] += _dot(a_ref[...].astype(BF16), b_ref[...].astype(BF16), dims)

        @pl.when(k == nk - 1)
        def _():
            if norm_bwd is None:
                o_ref[...] = acc_ref[...].astype(o_ref.dtype)
            else:
                dh = acc_ref[...]
                xv = x_ref[...]
                r = lax.rsqrt(jnp.mean(xv * xv, axis=-1, keepdims=True) + EPS)
                xh = xv * r
                dxh = dh * g_ref[...]
                o_ref[...] = dres_ref[...] + r * (dxh - xh * jnp.mean(dxh * xh, axis=-1, keepdims=True))
                dg_ref[...] += jnp.sum(dh * xh, axis=0, keepdims=True)

    in_specs, args = [a_spec, b_spec], (a, b)
    if res is not None:
        in_specs, args = in_specs + [o_spec], args + (res,)
    scratch = [pltpu.VMEM((tm, tn), F32)] if nk > 1 else []
    if norm_bwd is not None:
        assert nk > 1 and tn == N and norm_gain is None
        vec = pl.BlockSpec((1, N), lambda i, j, k: (0, 0))
        x_in, g_in, dres_in = norm_bwd
        (dx, dg), moved = _call_hosting(
            body, exch, name, (M // tm, 1, nk), in_specs + [o_spec, vec, o_spec], [o_spec, vec],
            [jax.ShapeDtypeStruct((M, N), F32), jax.ShapeDtypeStruct((1, N), F32)], scratch,
            args + (x_in, g_in, dres_in))
        return dx, dg, moved
    if norm_gain is not None:
        return pl.pallas_call(
            body, name=name, grid=(M // tm, N // tn, nk),
            in_specs=in_specs + [pl.BlockSpec((1, N), lambda i, j, k: (0, 0))], out_specs=[o_spec, o_spec],
            out_shape=[jax.ShapeDtypeStruct(out_shape, out_dtype), jax.ShapeDtypeStruct(out_shape, BF16)],
            compiler_params=_params(("parallel", "parallel", "arbitrary")),
        )(*args, norm_gain)
    if exch is None:
        return pl.pallas_call(
            body, name=name, grid=(M // tm, N // tn, nk),
            in_specs=in_specs, out_specs=o_spec,
            out_shape=jax.ShapeDtypeStruct(out_shape, out_dtype),
            scratch_shapes=scratch,
            compiler_params=_params(("parallel", "parallel", "arbitrary")),
        )(*args)
    (out,), moved = _call_hosting(
        body, exch, name, (M // tm, N // tn, nk), in_specs, [o_spec],
        [jax.ShapeDtypeStruct(out_shape, out_dtype)], scratch, args)
    return out, moved


def _mm_w_dev(a, w3, name, out_dtype=F32, tm=1024):
    M, K = a.shape
    n8 = w3.shape[2]
    tn = n8 if n8 <= 768 else 512
    per = n8 // tn
    b_spec = pl.BlockSpec((None, K, tn), lambda i, j, k: (j // per, 0, j % per))
    return _mm(a, w3, "nn", M, N_DEV * n8, K, out_dtype, name, tm, tn, K, b_spec=b_spec)


def _sectioned_spec(d4, t_rows, t_cols, rows_axis, cols_axis):
    _, _, S, W = d4.shape
    assert S % t_rows == 0 and W % t_cols == 0
    rb, cb = S // t_rows, W // t_cols

    def index(*g):
        r, c = g[rows_axis], g[cols_axis]
        return (r // rb, c // cb, r % rb, c % cb)

    return pl.BlockSpec((None, None, t_rows, t_cols), index)


def _mm_wT_dev(a, w3, name, out_dtype=F32, tm=1024, exch=None, norm_bwd=None):
    K, n8 = w3.shape[1], w3.shape[2]
    tk = n8 if n8 <= 768 else 512
    per = n8 // tk
    b_spec = pl.BlockSpec((None, K, tk), lambda i, j, k: (k // per, 0, k % per))
    if a.ndim == 4:
        M, N = a.shape[0] * a.shape[2], a.shape[1] * a.shape[3]
        tm = min(tm, a.shape[2])
        a_spec = _sectioned_spec(a, tm, tk, 0, 2)
    else:
        (M, N), a_spec = a.shape, None
    return _mm(a, w3, "nt", M, K, N, out_dtype, name, tm, K, tk, a_spec=a_spec, b_spec=b_spec, exch=exch,
               norm_bwd=norm_bwd)


def _mm_grad_dev(h, d, name, out_dtype=BF16):
    T, M = h.shape
    N = d.shape[1] * d.shape[3] if d.ndim == 4 else d.shape[1]
    n8 = N // N_DEV
    tn = n8 if n8 <= 768 else 512
    per = n8 // tn
    tm = min(M, 1024)
    o_spec = pl.BlockSpec((None, tm, tn), lambda i, j, k: (j // per, i, j % per))
    tk = min(1024, d.shape[2] if d.ndim == 4 else T)
    b_spec = _sectioned_spec(d, tk, tn, 2, 1) if d.ndim == 4 else None
    return _mm(h, d, "tn", M, N, T, out_dtype, name, tm, tn, tk, b_spec=b_spec, o_spec=o_spec,
               out_shape=(N_DEV, M, n8))


def _me():
    x, y, c = lax.axis_index("x"), lax.axis_index("y"), lax.axis_index("c")
    return x, y, c


def _peer(r):
    x, y, c = _me()
    px = 1 - x if (r >> 2) & 1 else x
    py = 1 - y if (r >> 1) & 1 else y
    pc = 1 - c if r & 1 else c
    return (px, py, pc), 4 * px + 2 * py + pc


class _Exchange:
    def __init__(self, arrays, kinds):
        self.arrays, self.kinds, self.n = list(arrays), list(kinds), len(arrays)
        self.out_shapes = [
            jax.ShapeDtypeStruct((N_DEV,) + a.shape if kind == "gather" else a.shape, a.dtype)
            for a, kind in zip(arrays, kinds)]
        self.specs = [pl.BlockSpec(memory_space=pl.ANY)] * self.n
        self.sems = [pltpu.SemaphoreType.DMA((self.n, N_DEV - 1)), pltpu.SemaphoreType.DMA((self.n, N_DEV - 1)),
                     pltpu.SemaphoreType.DMA((self.n,))]

    def _copies(self, ins, outs, sems, receiving):
        send_sems, recv_sems, local_sems = sems
        x, y, c = _me()
        me = 4 * x + 2 * y + c

        def src(k, pid):
            return ins[k] if self.kinds[k] == "gather" else ins[k].at[pid]

        local = [pltpu.make_async_copy(src(k, me), outs[k].at[me], local_sems.at[k]) for k in range(self.n)]
        remote = []
        for r in range(1, N_DEV):
            peer, pid = _peer(r)
            for k in range(self.n):
                remote.append(pltpu.make_async_remote_copy(
                    src_ref=src(k, pid), dst_ref=outs[k].at[pid if receiving else me],
                    send_sem=send_sems.at[k, r - 1], recv_sem=recv_sems.at[k, r - 1],
                    device_id=peer, device_id_type=pl.DeviceIdType.MESH))
        return local, remote

    def start(self, ins, outs, sems):
        local, remote = self._copies(ins, outs, sems, False)
        for cp in local + remote:
            cp.start()

    def wait(self, ins, outs, sems):
        local, remote = self._copies(ins, outs, sems, True)
        for cp in remote:
            cp.wait_recv()
        for cp in remote:
            cp.wait_send()
        for cp in local:
            cp.wait()

    def run(self, name):
        n = self.n

        def body(*refs):
            ins, outs, sems = refs[:n], refs[n:2 * n], refs[2 * n:]
            self.start(ins, outs, sems)
            self.wait(ins, outs, sems)

        return pl.pallas_call(
            body, name=name, in_specs=self.specs, out_specs=self.specs, out_shape=self.out_shapes,
            scratch_shapes=self.sems,
        )(*self.arrays)


class _GatherViaSibling(_Exchange):
    ICI = (2, 4, 6)

    def __init__(self, arrays):
        super().__init__(arrays, ["gather"] * len(arrays))

    def _copy(self, ins, outs, sems, k, column, block, to, from_input=False):
        return pltpu.make_async_remote_copy(
            src_ref=ins[k] if from_input else outs[k].at[block], dst_ref=outs[k].at[block],
            send_sem=sems[0].at[k, column], recv_sem=sems[1].at[k, column],
            device_id=to, device_id_type=pl.DeviceIdType.MESH)

    def start(self, ins, outs, sems):
        x, y, c = _me()
        me = 4 * x + 2 * y + c
        for k in range(self.n):
            pltpu.make_async_copy(ins[k], outs[k].at[me], sems[2].at[k]).start()
            self._copy(ins, outs, sems, k, 0, me, _peer(1)[0], True).start()
            for j, r in enumerate(self.ICI):
                self._copy(ins, outs, sems, k, 1 + j, me, _peer(r)[0], True).start()

    def wait(self, ins, outs, sems):
        x, y, c = _me()
        me = 4 * x + 2 * y + c
        sibling, sibling_id = _peer(1)
        for j, r in enumerate(self.ICI):
            peer, pid = _peer(r)
            for k in range(self.n):
                self._copy(ins, outs, sems, k, 1 + j, pid, peer).wait_recv()
                self._copy(ins, outs, sems, k, 4 + j, pid, sibling).start()
        for k in range(self.n):
            self._copy(ins, outs, sems, k, 0, sibling_id, sibling).wait_recv()
            for j, r in enumerate(self.ICI):
                self._copy(ins, outs, sems, k, 4 + j, _peer(r ^ 1)[1], sibling).wait_recv()
            for column in range(N_DEV - 1):
                self._copy(ins, outs, sems, k, column, me, sibling).wait_send()
            pltpu.make_async_copy(ins[k], outs[k].at[me], sems[2].at[k]).wait()


def _call_hosting(body, exch, name, grid, in_specs, out_specs, out_shape, scratch_shapes, args):
    if exch is None:
        res = pl.pallas_call(
            body, name=name, grid=grid, in_specs=list(in_specs), out_specs=list(out_specs),
            out_shape=list(out_shape), scratch_shapes=list(scratch_shapes),
            compiler_params=_params(("arbitrary",) * len(grid)))(*args)
        return res, []
    n_in, n_out, n_scr, nc = len(in_specs), len(out_specs), len(scratch_shapes), exch.n

    def full_body(*refs):
        ins, refs = refs[:n_in], refs[n_in:]
        cins, refs = refs[:nc], refs[nc:]
        outs, refs = refs[:n_out], refs[n_out:]
        couts, refs = refs[:nc], refs[nc:]
        scr, sems = refs[:n_scr], refs[n_scr:]
        ids = [pl.program_id(a) for a in range(len(grid))]
        first = functools.reduce(jnp.logical_and, [i == 0 for i in ids])
        last = functools.reduce(jnp.logical_and, [i == g - 1 for i, g in zip(ids, grid)])

        @pl.when(first)
        def _():
            exch.start(cins, couts, sems)

        body(*ins, *outs, *scr)

        @pl.when(last)
        def _():
            exch.wait(cins, couts, sems)

    res = pl.pallas_call(
        full_body, name=name, grid=grid,
        in_specs=list(in_specs) + exch.specs, out_specs=list(out_specs) + exch.specs,
        out_shape=list(out_shape) + exch.out_shapes,
        scratch_shapes=list(scratch_shapes) + exch.sems,
        compiler_params=_params(("arbitrary",) * len(grid)),
    )(*args, *exch.arrays)
    return res[:n_out], res[n_out:]


def _rmsnorm_fwd(x, g, name, exch=None):
    T, D = x.shape
    tr = min(256, T)

    def body(x_ref, g_ref, h_ref):
        xv = x_ref[...]
        r = lax.rsqrt(jnp.mean(xv * xv, axis=-1, keepdims=True) + EPS)
        h_ref[...] = (xv * r * g_ref[...]).astype(BF16)

    row = pl.BlockSpec((tr, D), lambda i: (i, 0))
    (h,), moved = _call_hosting(
        body, exch, name, (T // tr,), [row, pl.BlockSpec((1, D), lambda i: (0, 0))], [row],
        [jax.ShapeDtypeStruct((T, D), BF16)], [], (x, g))
    return h, moved


def _loss_head(x, g, target, y=None, w=None):
    T, D = x.shape
    tr = min(512 if y is not None else 256, T)

    def body(x_ref, g_ref, t_ref, *rest):
        loss_ref, dx_ref, dg_ref = rest[-3:]
        i = pl.program_id(0)
        xv = x_ref[...]
        if y is not None:
            xv = xv + _dot(rest[0][...].astype(BF16), rest[1][...].astype(BF16))
        gv = g_ref[...]
        r = lax.rsqrt(jnp.mean(xv * xv, axis=-1, keepdims=True) + EPS)
        xh = xv * r
        diff = xh * gv - t_ref[...]
        dy = diff * (1.0 / D)
        dxh = dy * gv
        dx_ref[...] = r * (dxh - xh * jnp.mean(dxh * xh, axis=-1, keepdims=True))

        @pl.when(i == 0)
        def _():
            dg_ref[...] = jnp.zeros_like(dg_ref)
            loss_ref[...] = jnp.zeros_like(loss_ref)

        dg_ref[...] += jnp.sum(dy * xh, axis=0, keepdims=True)
        part = jnp.sum(jnp.sum(diff * diff, axis=1, keepdims=True), axis=0, keepdims=True)
        loss_ref[...] += (0.5 / D) * part

    row = pl.BlockSpec((tr, D), lambda i: (i, 0))
    vec = pl.BlockSpec((1, D), lambda i: (0, 0))
    in_specs, args = [row, vec, row], (x, g, target)
    if y is not None:
        K = y.shape[1]
        in_specs += [pl.BlockSpec((tr, K), lambda i: (i, 0)), pl.BlockSpec((K, D), lambda i: (0, 0))]
        args += (y, w)
    return pl.pallas_call(
        body, name="loss_head", grid=(T // tr,),
        in_specs=in_specs,
        out_specs=[pl.BlockSpec((1, 1), lambda i: (0, 0)), row, vec],
        out_shape=[jax.ShapeDtypeStruct((1, 1), F32), jax.ShapeDtypeStruct((T, D), F32),
                   jax.ShapeDtypeStruct((1, D), F32)],
        compiler_params=_params(("arbitrary",)),
    )(*args)


ELEMS_PER_STEP = 1 << 20


def _row_tile(R, per_row):
    best = None
    for tr in range(8, R + 1, 8):
        if R % tr == 0 and tr * per_row <= ELEMS_PER_STEP:
            best = tr
    return best if best is not None else R


def _adamw(parts, w, m, v, name):
    P, R, C = parts.shape
    tr = _row_tile(R, P * C)

    def body(p_ref, w_ref, m_ref, v_ref, g_out, d_out, m_out, v_out):
        g = p_ref[0].astype(F32)
        for p in range(1, P):
            g = g + p_ref[p].astype(F32)
        wv = w_ref[...]
        mn = ADAM_B1 * m_ref[...] + (1.0 - ADAM_B1) * g
        vn = ADAM_B2 * v_ref[...] + (1.0 - ADAM_B2) * (g * g)
        m_hat = mn / (1.0 - ADAM_B1 ** ADAM_STEP)
        v_hat = vn / (1.0 - ADAM_B2 ** ADAM_STEP)
        g_out[...] = g
        d_out[...] = -ADAM_LR * (m_hat / (jnp.sqrt(v_hat) + ADAM_EPS) + ADAM_WD * wv)
        m_out[...] = mn
        v_out[...] = vn

    row = pl.BlockSpec((tr, C), lambda i: (i, 0))
    return pl.pallas_call(
        body, name=name, grid=(R // tr,),
        in_specs=[pl.BlockSpec((P, tr, C), lambda i: (0, i, 0)), row, row, row],
        out_specs=[row, row, row, row],
        out_shape=[jax.ShapeDtypeStruct((R, C), F32)] * 4,
        compiler_params=_params(("parallel",)),
    )(parts, w, m, v)


def _sum_parts(parts, name):
    P, R, C = parts.shape
    tr = _row_tile(R, P * C)

    def body(p_ref, o_ref):
        g = p_ref[0]
        for p in range(1, P):
            g = g + p_ref[p]
        o_ref[...] = g

    return pl.pallas_call(
        body, name=name, grid=(R // tr,),
        in_specs=[pl.BlockSpec((P, tr, C), lambda i: (0, i, 0))],
        out_specs=pl.BlockSpec((tr, C), lambda i: (i, 0)),
        out_shape=jax.ShapeDtypeStruct((R, C), F32),
        compiler_params=_params(("parallel",)),
    )(parts)


def _lane_head(Dh):
    assert Dh & (Dh - 1) == 0 and Dh <= LANES
    return lax.shift_right_logical(lax.broadcasted_iota(jnp.int32, (1, LANES), 1), Dh.bit_length() - 1)


def _stack_heads(x, lane_head, hpb):
    return jnp.concatenate([jnp.where(lane_head == h, x, 0.0) for h in range(hpb)], axis=0)


def _unstack_heads(acc, lane_head, hpb):
    TQ = acc.shape[0] // hpb
    out = acc[0:TQ]
    for h in range(1, hpb):
        out = jnp.where(lane_head == h, acc[h * TQ:(h + 1) * TQ], out)
    return out


def _live_rows(x, r0, hpb):
    if r0 == 0:
        return x
    TQ = x.shape[0] // hpb
    return jnp.concatenate([x[h * TQ + r0:(h + 1) * TQ] for h in range(hpb)], axis=0)


def _put_rows(full, part, r0, hpb):
    if r0 == 0:
        return part
    TQ = full.shape[0] // hpb
    n = TQ - r0
    return jnp.concatenate(
        [blk for h in range(hpb) for blk in (full[h * TQ:h * TQ + r0], part[h * n:(h + 1) * n])], axis=0)


def _first_live_row(m, TQ, TK):
    return max(0, TQ - (m + 1) * TK)


def _key_tile(S, tk=None):
    tk = ATT_TK if tk is None else tk
    return tk if S % tk == 0 else BLK


def _query_tile(S):
    return ATT_TQ if S % ATT_TQ == 0 else BLK


def _lane_groups(P):
    return ATT_GP if P % ATT_GP == 0 else 1


def _lanes(u):
    return slice(u * LANES, (u + 1) * LANES)


def _causal_iotas(RS, TK, TQ, r0=0):
    n = TQ - r0
    assert n & (n - 1) == 0 and (TK % TQ == 0 or TQ % TK == 0)
    rows = RS // TQ * n
    trow = jnp.bitwise_and(lax.broadcasted_iota(jnp.int32, (rows, TK), 0), n - 1) + r0
    col = lax.broadcasted_iota(jnp.int32, (rows, TK), 1)
    return trow, col


def _tri(TK, op):
    r = lax.broadcasted_iota(jnp.int32, (TK, TK), 0)
    c = lax.broadcasted_iota(jnp.int32, (TK, TK), 1)
    return op(r, c).astype(BF16)


def _logsig_parts(z):
    lb = jnp.minimum(z, 0.0) - jnp.log(1.0 + jnp.exp(-jnp.abs(z)))
    return lb, lb - z


def _sb_fwd(proj3, W, heads, exch):
    B, S, _ = proj3.shape
    Dh = W // heads
    hpb = LANES // Dh
    P, TQ = W // LANES, _query_tile(S)
    NQ = S // TQ
    scale = 1.0 / math.sqrt(Dh)

    TK = _key_tile(S)
    RS = hpb * TQ
    NM = max(1, TQ // TK)
    GP = _lane_groups(P)
    PG = P // GP

    def body(q_ref, k_ref, v_ref, g_ref, o_ref, y_ref):
        i = pl.program_id(2)
        lane_head = _lane_head(Dh)
        msuf = _tri(TK, lambda r, c: r > c)
        qs = [(_stack_heads(q_ref[:, _lanes(u)], lane_head, hpb) * scale).astype(BF16) for u in range(GP)]
        nt = (i * TQ + TQ - 2) // TK + 1

        def tile(jt, carry, masked, r0=0):
            off = pl.multiple_of(jt * TK, TK)
            if masked:
                trow, col = _causal_iotas(RS, TK, TQ, r0)
                msk = col + (jt * TK - i * TQ) < trow
            out = []
            for u, (rem_all, acc_all) in enumerate(carry):
                rem, acc = _live_rows(rem_all, r0, hpb), _live_rows(acc_all, r0, hpb)
                kj = k_ref[pl.ds(off, TK), _lanes(u)].astype(BF16)
                vj = v_ref[pl.ds(off, TK), _lanes(u)].astype(BF16)
                lb, lr = _logsig_parts(_dot(_live_rows(qs[u], r0, hpb), kj, _NT))
                if masked:
                    lr = jnp.where(msk, lr, 0.0)
                w = jnp.exp(lb + _split_dot(lr, msuf) + rem)
                if masked:
                    w = jnp.where(msk, w, 0.0)
                out.append((_put_rows(rem_all, rem + jnp.sum(lr, axis=1, keepdims=True), r0, hpb),
                            _put_rows(acc_all, acc + _dot(w.astype(BF16), vj), r0, hpb)))
            return tuple(out)

        zero = (jnp.zeros((RS, 1), F32), jnp.zeros((RS, LANES), F32))
        carry = (zero,) * GP
        for m in range(NM):
            carry = tile(nt - 1 - m, carry, True, _first_live_row(m, TQ, TK))
        carry = lax.fori_loop(NM, nt, lambda jj, c: tile(nt - 1 - jj, c, False), carry)
        for u in range(GP):
            o = _unstack_heads(carry[u][1], lane_head, hpb)
            o_ref[:, _lanes(u)] = o
            y_ref[:, _lanes(u)] = (o * _silu(g_ref[:, _lanes(u)])).astype(BF16)

    LW = GP * LANES
    blk = lambda sec: pl.BlockSpec((None, TQ, LW), lambda b, p, i: (b, i, sec * PG + p))
    full = lambda sec: pl.BlockSpec((None, S, LW), lambda b, p, i: (b, 0, sec * PG + p))
    out = pl.BlockSpec((None, TQ, LW), lambda b, p, i: (b, i, p))
    return _call_hosting(
        body, exch, "sb_fwd", (B, PG, NQ), [blk(0), full(1), full(2), blk(3)], [out, out],
        [jax.ShapeDtypeStruct((B, S, W), F32), jax.ShapeDtypeStruct((B, S, W), BF16)], [],
        (proj3, proj3, proj3, proj3))


def _sb_bwd(proj3, o, dy, W, heads, exch):
    B, S, _ = proj3.shape
    Dh = W // heads
    hpb = LANES // Dh
    P, TQ = W // LANES, _query_tile(S)
    NQ = S // TQ
    scale = 1.0 / math.sqrt(Dh)

    TK = _key_tile(S)
    RS = hpb * TQ
    NM = max(1, TQ // TK)

    def body(q_ref, k_ref, v_ref, g_ref, o_ref, dy_ref, dp_ref, dk_ref, dv_ref, u_ref, sig_ref, es_ref):
        i = pl.program_id(2)
        rows = pl.ds(pl.multiple_of(i * TQ, TQ), TQ)

        @pl.when(i == 0)
        def _():
            dk_ref[...] = jnp.zeros_like(dk_ref)
            dv_ref[...] = jnp.zeros_like(dv_ref)

        lane_head = _lane_head(Dh)
        msuf = _tri(TK, lambda r, c: r > c)
        mpre = _tri(TK, lambda r, c: r < c)
        g = g_ref[...]
        dyv = dy_ref[...].astype(F32)
        dp_ref[3, rows, :] = (dyv * o_ref[...] * _dsilu(g)).astype(dp_ref.dtype)
        qs = (_stack_heads(q_ref[...], lane_head, hpb) * scale).astype(BF16)
        dos = _stack_heads(dyv * _silu(g), lane_head, hpb).astype(BF16)
        nt = (i * TQ + TQ - 2) // TK + 1

        def weights(jt, rem_all, masked, r0=0):
            off = pl.multiple_of(jt * TK, TK)
            kj = k_ref[pl.ds(off, TK), :].astype(BF16)
            vj = v_ref[pl.ds(off, TK), :].astype(BF16)
            dos_l = _live_rows(dos, r0, hpb)
            lb, lr = _logsig_parts(_dot(_live_rows(qs, r0, hpb), kj, _NT))
            if masked:
                trow, col = _causal_iotas(RS, TK, TQ, r0)
                msk = col + (jt * TK - i * TQ) < trow
                lr = jnp.where(msk, lr, 0.0)
            w = jnp.exp(lb + _split_dot(lr, msuf) + _live_rows(rem_all, r0, hpb))
            if masked:
                w = jnp.where(msk, w, 0.0)
            e = w * _dot(dos_l, vj, _NT)
            sig = jnp.exp(lb)
            u = e - sig * (e + _split_dot(e, mpre))
            if masked:
                u = jnp.where(msk, u, 0.0)
                sig = jnp.where(msk, sig, 0.0)
            n = TQ - r0
            for h in range(hpb):
                u_ref[jt, h * TQ + r0:(h + 1) * TQ, :] = u[h * n:(h + 1) * n]
                sig_ref[jt, h * TQ + r0:(h + 1) * TQ, :] = sig[h * n:(h + 1) * n]
            es_ref[jt] = _put_rows(jnp.zeros((RS, 1), F32), jnp.sum(e, axis=1, keepdims=True), r0, hpb)
            dv_ref[pl.ds(off, TK), :] += _dot(w.astype(BF16), dos_l, _TN)
            return _put_rows(rem_all, _live_rows(rem_all, r0, hpb) + jnp.sum(lr, axis=1, keepdims=True), r0, hpb)

        rem = jnp.zeros((RS, 1), F32)
        for m in range(NM):
            rem = weights(nt - 1 - m, rem, True, _first_live_row(m, TQ, TK))
        lax.fori_loop(NM, nt, lambda jj, r: weights(nt - 1 - jj, r, False), rem)

        def grads(jt, carry, r0=0):
            pre, acc = carry
            off = pl.multiple_of(jt * TK, TK)
            kj = k_ref[pl.ds(off, TK), :].astype(BF16)
            if r0 == 0:
                u, sig = u_ref[jt], sig_ref[jt]
            else:
                u = jnp.concatenate([u_ref[jt, h * TQ + r0:(h + 1) * TQ, :] for h in range(hpb)], axis=0)
                sig = jnp.concatenate([sig_ref[jt, h * TQ + r0:(h + 1) * TQ, :] for h in range(hpb)], axis=0)
            dz = (u - _live_rows(pre, r0, hpb) * sig).astype(BF16)
            dk_ref[pl.ds(off, TK), :] += _dot(dz, _live_rows(qs, r0, hpb), _TN)
            return pre + es_ref[jt], _put_rows(acc, _live_rows(acc, r0, hpb) + _dot(dz, kj), r0, hpb)

        carry = lax.fori_loop(0, nt - NM, grads, (jnp.zeros((RS, 1), F32), jnp.zeros((RS, LANES), F32)))
        for m in reversed(range(NM)):
            carry = grads(nt - 1 - m, carry, _first_live_row(m, TQ, TK))
        _, acc = carry
        dp_ref[0, rows, :] = (_unstack_heads(acc, lane_head, hpb) * scale).astype(dp_ref.dtype)

        @pl.when(i == NQ - 1)
        def _():
            dp_ref[1] = dk_ref[...].astype(dp_ref.dtype)
            dp_ref[2] = dv_ref[...].astype(dp_ref.dtype)

    blk = lambda sec: pl.BlockSpec((None, TQ, LANES), lambda b, p, i: (b, i, sec * P + p))
    full = lambda sec: pl.BlockSpec((None, S, LANES), lambda b, p, i: (b, 0, sec * P + p))
    one = pl.BlockSpec((None, TQ, LANES), lambda b, p, i: (b, i, p))
    (dproj,), moved = _call_hosting(
        body, exch, "sb_bwd", (B, P, NQ), [blk(0), full(1), full(2), blk(3), one, one],
        [pl.BlockSpec((None, 4, S, LANES), lambda b, p, i: (b, 0, 0, p))],
        [jax.ShapeDtypeStruct((B, 4, S, W), BF16)],
        [pltpu.VMEM((S, LANES), F32), pltpu.VMEM((S, LANES), F32),
         pltpu.VMEM((S // TK, RS, TK), F32), pltpu.VMEM((S // TK, RS, TK), F32), pltpu.VMEM((S // TK, RS, 1), F32)],
        (proj3, proj3, proj3, proj3, o, dy))
    return dproj, moved


def _fox_gate_fwd(f_t, b_f):
    B, H, S = f_t.shape

    def body(f_ref, b_ref, c_ref):
        row = lax.broadcasted_iota(jnp.int32, (BLK, BLK), 0)
        col = lax.broadcasted_iota(jnp.int32, (BLK, BLK), 1)
        mpre = (row <= col).astype(BF16)
        carry = jnp.zeros((H, 1), F32)
        for n in range(S // BLK):
            sl = pl.ds(n * BLK, BLK)
            lf, _ = _logsig_parts(f_ref[:, sl] + b_ref[...])
            c_ref[:, sl] = _split3_dot(lf, mpre) + carry
            carry = carry + jnp.sum(lf, axis=1, keepdims=True)

    spec = pl.BlockSpec((None, H, S), lambda b: (b, 0, 0))
    return pl.pallas_call(
        body, name="fox_gate_fwd", grid=(B,),
        in_specs=[spec, pl.BlockSpec((H, 1), lambda b: (0, 0))], out_specs=spec,
        out_shape=jax.ShapeDtypeStruct((B, H, S), F32),
        compiler_params=_params(("parallel",)),
    )(f_t, b_f)


def _fox_gate_bwd(dcum_t, f_t, b_f):
    B, H, S = f_t.shape

    def body(d_ref, f_ref, b_ref, df_ref, db_ref):
        b = pl.program_id(0)

        @pl.when(b == 0)
        def _():
            db_ref[...] = jnp.zeros_like(db_ref)

        row = lax.broadcasted_iota(jnp.int32, (BLK, BLK), 0)
        col = lax.broadcasted_iota(jnp.int32, (BLK, BLK), 1)
        msuf = (row >= col).astype(BF16)
        carry = jnp.zeros((H, 1), F32)
        dbacc = jnp.zeros((H, 1), F32)
        for n in reversed(range(S // BLK)):
            sl = pl.ds(n * BLK, BLK)
            dv = d_ref[:, sl]
            dlf = _split3_dot(dv, msuf) + carry
            carry = carry + jnp.sum(dv, axis=1, keepdims=True)
            df = dlf * _sigmoid(-(f_ref[:, sl] + b_ref[...]))
            df_ref[:, sl] = df
            dbacc = dbacc + jnp.sum(df, axis=1, keepdims=True)
        db_ref[...] += dbacc

    spec = pl.BlockSpec((None, H, S), lambda b: (b, 0, 0))
    vec = pl.BlockSpec((H, 1), lambda b: (0, 0))
    return pl.pallas_call(
        body, name="fox_gate_bwd", grid=(B,),
        in_specs=[spec, spec, vec], out_specs=[spec, vec],
        out_shape=[jax.ShapeDtypeStruct((B, H, S), F32), jax.ShapeDtypeStruct((H, 1), F32)],
        compiler_params=_params(("arbitrary",)),
    )(dcum_t, f_t, b_f)


def _pick_col(block, idx, lane_iota):
    return jnp.sum(jnp.where(lane_iota == idx, block, 0.0), axis=1, keepdims=True)


def _pick_row(block, idx, sub_iota):
    return jnp.sum(jnp.where(sub_iota == idx, block, 0.0), axis=0, keepdims=True)


def _fox_fwd(proj3, cum_t, W, heads):
    B, S, _ = proj3.shape
    H = heads
    Dh = W // heads
    hpb = LANES // Dh
    P, TQ = W // LANES, _query_tile(S)
    NQ = S // TQ
    scale = 1.0 / math.sqrt(Dh)

    TK = _key_tile(S, FOX_TK)
    RS = hpb * TQ
    NM = max(1, TQ // TK)

    def body(q_ref, k_ref, v_ref, g_ref, ct_ref, o_ref, y_ref, lse_ref):
        p = pl.program_id(1)
        i = pl.program_id(2)
        lane_head = _lane_head(Dh)
        sub_h = lax.broadcasted_iota(jnp.int32, (H, 1), 0)
        qs = (_stack_heads(q_ref[...], lane_head, hpb) * scale).astype(BF16)
        nt = (i * TQ + TQ - 1) // TK + 1

        def tile(jt, carry, masked, r0=0):
            mx, l, acc = carry
            n = TQ - r0
            off = pl.multiple_of(jt * TK, TK)
            kj = k_ref[pl.ds(off, TK), :].astype(BF16)
            vj = v_ref[pl.ds(off, TK), :].astype(BF16)
            ctb = ct_ref[:, pl.ds(off, TK)]
            z = _dot(_live_rows(qs, r0, hpb), kj, _NT)
            s = jnp.concatenate([z[h * n:(h + 1) * n] - _pick_row(ctb, p * hpb + h, sub_h) for h in range(hpb)],
                                axis=0)
            if masked:
                trow, col = _causal_iotas(RS, TK, TQ, r0)
                s = jnp.where(col + (jt * TK - i * TQ) <= trow, s, NEG_BIG)
            mx2 = jnp.maximum(mx, jnp.max(s, axis=1, keepdims=True))
            pe = jnp.exp(s - mx2)
            alpha = jnp.exp(mx - mx2)
            return (mx2, alpha * l + jnp.sum(pe, axis=1, keepdims=True), alpha * acc + _dot(pe.astype(BF16), vj))

        carry = lax.fori_loop(
            0, nt - NM, lambda jt, c: tile(jt, c, False),
            (jnp.full((RS, 1), NEG_BIG, F32), jnp.zeros((RS, 1), F32), jnp.zeros((RS, LANES), F32)))
        for m in reversed(range(NM)):
            carry = tile(nt - 1 - m, carry, True)
        mx, l, acc = carry
        o = _unstack_heads(acc / l, lane_head, hpb)
        o_ref[...] = o
        lse_ref[...] = _unstack_heads(jnp.broadcast_to(mx + jnp.log(l), (RS, LANES)), lane_head, hpb)
        y_ref[...] = (o * _silu(g_ref[...])).astype(BF16)

    blk = lambda sec: pl.BlockSpec((None, TQ, LANES), lambda b, p, i: (b, i, sec * P + p))
    full = lambda sec: pl.BlockSpec((None, S, LANES), lambda b, p, i: (b, 0, sec * P + p))
    out = pl.BlockSpec((None, TQ, LANES), lambda b, p, i: (b, i, p))
    return pl.pallas_call(
        body, name="fox_fwd", grid=(B, P, NQ),
        in_specs=[blk(0), full(1), full(2), blk(3),
                  pl.BlockSpec((None, H, S), lambda b, p, i: (b, 0, 0))],
        out_specs=[out, out, out],
        out_shape=[jax.ShapeDtypeStruct((B, S, W), F32), jax.ShapeDtypeStruct((B, S, W), BF16),
                   jax.ShapeDtypeStruct((B, S, W), F32)],
        compiler_params=_params(("parallel", "parallel", "arbitrary")),
    )(proj3, proj3, proj3, proj3, cum_t)


def _fox_bwd(proj3, cum_t, o, lse, dy, W, heads):
    B, S, _ = proj3.shape
    H = heads
    Dh = W // heads
    hpb = LANES // Dh
    P, TQ = W // LANES, _query_tile(S)
    NQ = S // TQ
    scale = 1.0 / math.sqrt(Dh)

    TK = _key_tile(S)
    RS = hpb * TQ
    NM = max(1, TQ // TK)

    def body(q_ref, k_ref, v_ref, g_ref, ct_ref, o_ref, lse_ref, dy_ref,
             dpj_ref, dc_ref, dk_ref, dv_ref, p_scr, dp_scr):
        p = pl.program_id(1)
        i = pl.program_id(2)
        rows = pl.ds(pl.multiple_of(i * TQ, TQ), TQ)

        @pl.when(i == 0)
        def _():
            dk_ref[...] = jnp.zeros_like(dk_ref)
            dv_ref[...] = jnp.zeros_like(dv_ref)
            dc_ref[...] = jnp.zeros_like(dc_ref)

        lane_head = _lane_head(Dh)
        sub_h = lax.broadcasted_iota(jnp.int32, (H, 1), 0)
        lane = lax.broadcasted_iota(jnp.int32, (1, LANES), 1)
        g = g_ref[...]
        lsev = lse_ref[...]
        dyv = dy_ref[...].astype(F32)
        dpj_ref[3, rows, :] = (dyv * o_ref[...] * _dsilu(g)).astype(dpj_ref.dtype)
        qs = (_stack_heads(q_ref[...], lane_head, hpb) * scale).astype(BF16)
        dos = _stack_heads(dyv * _silu(g), lane_head, hpb).astype(BF16)
        neg_lse = -jnp.concatenate([_pick_col(lsev, h * Dh, lane) for h in range(hpb)], axis=0)
        nt = (i * TQ + TQ - 1) // TK + 1

        def probs(jt, dsum, masked, r0=0):
            n = TQ - r0
            off = pl.multiple_of(jt * TK, TK)
            kj = k_ref[pl.ds(off, TK), :].astype(BF16)
            vj = v_ref[pl.ds(off, TK), :].astype(BF16)
            ctb = ct_ref[:, pl.ds(off, TK)]
            dos_l = _live_rows(dos, r0, hpb)
            z = _dot(_live_rows(qs, r0, hpb), kj, _NT) + _live_rows(neg_lse, r0, hpb)
            s = jnp.concatenate([z[h * n:(h + 1) * n] - _pick_row(ctb, p * hpb + h, sub_h) for h in range(hpb)],
                                axis=0)
            pr = jnp.exp(s)
            if masked:
                trow, col = _causal_iotas(RS, TK, TQ, r0)
                pr = jnp.where(col + (jt * TK - i * TQ) <= trow, pr, 0.0)
            dp = _dot(dos_l, vj, _NT)
            p_scr[jt] = _put_rows(jnp.zeros((RS, TK), F32), pr, r0, hpb)
            dp_scr[jt] = _put_rows(jnp.zeros((RS, TK), F32), dp, r0, hpb)
            dv_ref[pl.ds(off, TK), :] += _dot(pr.astype(BF16), dos_l, _TN)
            return _put_rows(dsum, _live_rows(dsum, r0, hpb) + jnp.sum(pr * dp, axis=1, keepdims=True), r0, hpb)

        dsum = lax.fori_loop(0, nt - NM, lambda jt, d: probs(jt, d, False), jnp.zeros((RS, 1), F32))
        for m in reversed(range(NM)):
            dsum = probs(nt - 1 - m, dsum, True)

        def grads(jt, acc, r0=0):
            n = TQ - r0
            off = pl.multiple_of(jt * TK, TK)
            kj = k_ref[pl.ds(off, TK), :].astype(BF16)
            if r0 == 0:
                pr, dp = p_scr[jt], dp_scr[jt]
            else:
                pr = jnp.concatenate([p_scr[jt, h * TQ + r0:(h + 1) * TQ, :] for h in range(hpb)], axis=0)
                dp = jnp.concatenate([dp_scr[jt, h * TQ + r0:(h + 1) * TQ, :] for h in range(hpb)], axis=0)
            ds = pr * (dp - _live_rows(dsum, r0, hpb))
            for h in range(hpb):
                dc_ref[h:h + 1, pl.ds(off, TK)] -= jnp.sum(ds[h * n:(h + 1) * n], axis=0, keepdims=True)
            dsb = ds.astype(BF16)
            dk_ref[pl.ds(off, TK), :] += _dot(dsb, _live_rows(qs, r0, hpb), _TN)
            return _put_rows(acc, _live_rows(acc, r0, hpb) + _dot(dsb, kj), r0, hpb)

        acc = lax.fori_loop(0, nt - NM, grads, jnp.zeros((RS, LANES), F32))
        for m in reversed(range(NM)):
            acc = grads(nt - 1 - m, acc, _first_live_row(m, TQ, TK))
        dpj_ref[0, rows, :] = (_unstack_heads(acc, lane_head, hpb) * scale).astype(dpj_ref.dtype)

        @pl.when(i == NQ - 1)
        def _():
            dpj_ref[1] = dk_ref[...].astype(dpj_ref.dtype)
            dpj_ref[2] = dv_ref[...].astype(dpj_ref.dtype)

    blk = lambda sec: pl.BlockSpec((None, TQ, LANES), lambda b, p, i: (b, i, sec * P + p))
    full = lambda sec: pl.BlockSpec((None, S, LANES), lambda b, p, i: (b, 0, sec * P + p))
    one = pl.BlockSpec((None, TQ, LANES), lambda b, p, i: (b, i, p))
    return pl.pallas_call(
        body, name="fox_bwd", grid=(B, P, NQ),
        in_specs=[blk(0), full(1), full(2), blk(3),
                  pl.BlockSpec((None, H, S), lambda b, p, i: (b, 0, 0)),
                  one, one, one],
        out_specs=[pl.BlockSpec((None, 4, S, LANES), lambda b, p, i: (b, 0, 0, p)),
                   pl.BlockSpec((None, None, hpb, S), lambda b, p, i: (b, p, 0, 0))],
        out_shape=[jax.ShapeDtypeStruct((B, 4, S, W), BF16), jax.ShapeDtypeStruct((B, P, hpb, S), F32)],
        scratch_shapes=[pltpu.VMEM((S, LANES), F32), pltpu.VMEM((S, LANES), F32),
                        pltpu.VMEM((S // TK, RS, TK), F32), pltpu.VMEM((S // TK, RS, TK), F32)],
        compiler_params=_params(("parallel", "parallel", "arbitrary")),
    )(proj3, proj3, proj3, proj3, cum_t, o, lse, dy)


def _layernorm_rows(v, gamma, beta):
    mu = jnp.mean(v, axis=-1, keepdims=True)
    xc = v - mu
    rstd = lax.rsqrt(jnp.mean(xc * xc, axis=-1, keepdims=True) + EPS)
    xh = xc * rstd
    return xh, rstd, xh * gamma + beta


def _layernorm_rows_bwd(dout, xh, rstd, gamma):
    dxh = dout * gamma
    return rstd * (dxh - jnp.mean(dxh, axis=-1, keepdims=True) - xh * jnp.mean(dxh * xh, axis=-1, keepdims=True))


def _gmlp_fwd(proj, wm, bs_t, ln_g, ln_b, W):
    T = proj.shape[0]
    G = wm.shape[0]
    cg = W // G
    assert cg == LANES

    def body(p_ref, wm_ref, bs_ref, lg_ref, lb_ref, y_ref, vn_ref):
        lane = lax.broadcasted_iota(jnp.int32, (1, LANES), 1)
        _, _, vn = _layernorm_rows(_gelu(p_ref[:, W:2 * W]), lg_ref[...], lb_ref[...])
        vn_ref[...] = vn.astype(BF16)
        bs = bs_ref[...]
        for g in range(G):
            sl = pl.ds(g * cg, cg)
            s = _dot(wm_ref[g], vn_ref[:, sl]) + _pick_col(bs, g, lane)
            gate = p_ref[:, pl.ds(2 * W + g * cg, cg)]
            y_ref[:, sl] = (_gelu(p_ref[:, sl]) * s * _silu(gate)).astype(BF16)

    vec = pl.BlockSpec((1, W), lambda r: (0, 0))
    return pl.pallas_call(
        body, name="gmlp_fwd", grid=(T // BLK,),
        in_specs=[pl.BlockSpec((BLK, 3 * W), lambda r: (r, 0)),
                  pl.BlockSpec((G, BLK, BLK), lambda r: (0, 0, 0)),
                  pl.BlockSpec((BLK, LANES), lambda r: (0, 0)), vec, vec],
        out_specs=pl.BlockSpec((BLK, W), lambda r: (r, 0)),
        out_shape=jax.ShapeDtypeStruct((T, W), BF16),
        scratch_shapes=[pltpu.VMEM((BLK, W), BF16)],
        compiler_params=_params(("parallel",)),
    )(proj, wm, bs_t, ln_g, ln_b)


def _gmlp_bwd(proj, dy, wm, bs_t, ln_g, ln_b, W):
    T = proj.shape[0]
    G = wm.shape[0]
    cg = W // G

    def body(p_ref, dy_ref, wm_ref, bs_ref, lg_ref, lb_ref,
             dp_ref, dwm_ref, dbs_ref, dlg_ref, dlb_ref, vn_ref, dvn_ref):
        r = pl.program_id(0)

        @pl.when(r == 0)
        def _():
            dwm_ref[...] = jnp.zeros_like(dwm_ref)
            dbs_ref[...] = jnp.zeros_like(dbs_ref)
            dlg_ref[...] = jnp.zeros_like(dlg_ref)
            dlb_ref[...] = jnp.zeros_like(dlb_ref)

        lane = lax.broadcasted_iota(jnp.int32, (1, LANES), 1)
        vpre = p_ref[:, W:2 * W]
        gamma = lg_ref[...]
        xh, rstd, vn = _layernorm_rows(_gelu(vpre), gamma, lb_ref[...])
        vn_ref[...] = vn.astype(BF16)
        bs = bs_ref[...]
        dbs = jnp.zeros((BLK, LANES), F32)
        for g in range(G):
            sl = pl.ds(g * cg, cg)
            gsl = pl.ds(2 * W + g * cg, cg)
            vng = vn_ref[:, sl]
            s = _dot(wm_ref[g], vng) + _pick_col(bs, g, lane)
            upre = p_ref[:, sl]
            u = _gelu(upre)
            gate = p_ref[:, gsl]
            dyv = dy_ref[:, sl].astype(F32)
            dp_ref[:, gsl] = (dyv * u * s * _dsilu(gate)).astype(dp_ref.dtype)
            do = dyv * _silu(gate)
            dp_ref[:, sl] = (do * s * _dgelu(upre)).astype(dp_ref.dtype)
            ds = do * u
            dbs = dbs + jnp.where(lane == g, jnp.sum(ds, axis=1, keepdims=True), 0.0)
            dsb = ds.astype(BF16)
            dwm_ref[g] += _dot(dsb, vng, _NT)
            dvn_ref[:, sl] = _dot(wm_ref[g], dsb, _TN)
        dbs_ref[...] += dbs
        dvn = dvn_ref[...]
        dlg_ref[...] += jnp.sum(dvn * xh, axis=0, keepdims=True)
        dlb_ref[...] += jnp.sum(dvn, axis=0, keepdims=True)
        dv = _layernorm_rows_bwd(dvn, xh, rstd, gamma)
        dp_ref[:, W:2 * W] = (dv * _dgelu(vpre)).astype(dp_ref.dtype)

    vec = pl.BlockSpec((1, W), lambda r: (0, 0))
    return pl.pallas_call(
        body, name="gmlp_bwd", grid=(T // BLK,),
        in_specs=[pl.BlockSpec((BLK, 3 * W), lambda r: (r, 0)),
                  pl.BlockSpec((BLK, W), lambda r: (r, 0)),
                  pl.BlockSpec((G, BLK, BLK), lambda r: (0, 0, 0)),
                  pl.BlockSpec((BLK, LANES), lambda r: (0, 0)), vec, vec],
        out_specs=[pl.BlockSpec((BLK, 3 * W), lambda r: (r, 0)),
                   pl.BlockSpec((G, BLK, BLK), lambda r: (0, 0, 0)),
                   pl.BlockSpec((BLK, LANES), lambda r: (0, 0)), vec, vec],
        out_shape=[jax.ShapeDtypeStruct((T, 3 * W), BF16), jax.ShapeDtypeStruct((G, BLK, BLK), F32),
                   jax.ShapeDtypeStruct((BLK, LANES), F32),
                   jax.ShapeDtypeStruct((1, W), F32), jax.ShapeDtypeStruct((1, W), F32)],
        scratch_shapes=[pltpu.VMEM((BLK, W), BF16), pltpu.VMEM((BLK, W), F32)],
        compiler_params=_params(("arbitrary",)),
    )(proj, dy, wm, bs_t, ln_g, ln_b)


SUBLANES = 8
SHIFT_ROWS = CONV_HALO + BLK - SUBLANES


def _shift_rows(ext_ref, sh_ref, off):
    for r in range(1, SUBLANES):
        sh_ref[r - 1] = ext_ref[pl.ds(r, SHIFT_ROWS), pl.ds(off, LANES)]


def _rows_from(ext_ref, sh_ref, off, start):
    r = start % SUBLANES
    if r == 0:
        return ext_ref[pl.ds(start, BLK), pl.ds(off, LANES)]
    return sh_ref[r - 1, pl.ds(start - r, BLK), :]


def _conv_taps(ext_ref, sh_ref, cw_ref, off, n_taps, first):
    acc = jnp.zeros((BLK, LANES), F32)
    for k in range(n_taps):
        acc = acc + cw_ref[k:k + 1, pl.ds(off, LANES)] * _rows_from(ext_ref, sh_ref, off, first + k)
    return acc


def _fill_glu_ext(ext_ref, halo_ref, cur_ref, W, first_block):
    y0h = halo_ref[:, :W] * _sigmoid(halo_ref[:, W:])
    ext_ref[0:CONV_HALO, :] = jnp.where(first_block, 0.0, y0h)
    ext_ref[CONV_HALO:CONV_HALO + BLK, :] = cur_ref[:, :W] * _sigmoid(cur_ref[:, W:])


def _conv_specs(S, W):
    per = BLK // CONV_HALO
    cur = pl.BlockSpec((None, BLK, 2 * W), lambda b, i: (b, i, 0))
    halo = pl.BlockSpec((None, CONV_HALO, 2 * W), lambda b, i: (b, jnp.maximum(i * per - 1, 0), 0))
    gate = pl.BlockSpec((None, BLK, W), lambda b, i: (b, i, 2))
    return cur, halo, gate


def _conv_fwd(proj3, cw, cb, ln_g, ln_b, W, exch):
    B, S, _ = proj3.shape
    K = cw.shape[0]
    first = CONV_HALO - (K - 1)
    assert first >= 0

    def body(cur_ref, halo_ref, g_ref, cw_ref, cb_ref, lg_ref, lb_ref, y_ref, y1_ref, ext_ref, sh_ref):
        i = pl.program_id(1)
        _fill_glu_ext(ext_ref, halo_ref, cur_ref, W, i == 0)

        def chan(c, _):
            off = pl.multiple_of(c * LANES, LANES)
            _shift_rows(ext_ref, sh_ref, off)
            y1_ref[:, pl.ds(off, LANES)] = (_conv_taps(ext_ref, sh_ref, cw_ref, off, K, first)
                                            + cb_ref[:, pl.ds(off, LANES)])
            return 0

        lax.fori_loop(0, W // LANES, chan, 0)
        _, _, ln = _layernorm_rows(y1_ref[...], lg_ref[...], lb_ref[...])
        y_ref[...] = (_silu(ln) * _silu(g_ref[...])).astype(BF16)

    cur, halo, gate = _conv_specs(S, W)
    vec = pl.BlockSpec((1, W), lambda b, i: (0, 0))
    one = pl.BlockSpec((None, BLK, W), lambda b, i: (b, i, 0))
    (y, y1), moved = _call_hosting(
        body, exch, "conv_fwd", (B, S // BLK),
        [cur, halo, gate, pl.BlockSpec((K, W), lambda b, i: (0, 0)), vec, vec, vec],
        [one, one], [jax.ShapeDtypeStruct((B, S, W), BF16), jax.ShapeDtypeStruct((B, S, W), F32)],
        [pltpu.VMEM((CONV_HALO + BLK, W), F32), pltpu.VMEM((SUBLANES - 1, SHIFT_ROWS, LANES), F32)],
        (proj3, proj3, proj3, cw, cb, ln_g, ln_b))
    return y, y1, moved


def _conv_bwd1(proj3, y1, dy, K, ln_g, ln_b, W, exch):
    B, S, _ = proj3.shape
    first = CONV_HALO - (K - 1)

    def body(cur_ref, halo_ref, g_ref, y1_ref, dy_ref, lg_ref, lb_ref,
             dy1_ref, dg_ref, dcw_ref, dcb_ref, dlg_ref, dlb_ref, ext_ref, sh_ref):
        b = pl.program_id(0)
        i = pl.program_id(1)

        @pl.when(jnp.logical_and(b == 0, i == 0))
        def _():
            dcw_ref[...] = jnp.zeros_like(dcw_ref)
            dcb_ref[...] = jnp.zeros_like(dcb_ref)
            dlg_ref[...] = jnp.zeros_like(dlg_ref)
            dlb_ref[...] = jnp.zeros_like(dlb_ref)

        _fill_glu_ext(ext_ref, halo_ref, cur_ref, W, i == 0)
        gamma = lg_ref[...]
        xh, rstd, ln = _layernorm_rows(y1_ref[...], gamma, lb_ref[...])
        g = g_ref[...]
        dyv = dy_ref[...].astype(F32)
        dg_ref[...] = (dyv * _silu(ln) * _dsilu(g)).astype(dg_ref.dtype)
        dln = dyv * _silu(g) * _dsilu(ln)
        dlg_ref[...] += jnp.sum(dln * xh, axis=0, keepdims=True)
        dlb_ref[...] += jnp.sum(dln, axis=0, keepdims=True)
        dy1 = _layernorm_rows_bwd(dln, xh, rstd, gamma)
        dy1_ref[...] = dy1
        dcb_ref[...] += jnp.sum(dy1, axis=0, keepdims=True)

        def chan_w(c, _):
            off = pl.multiple_of(c * LANES, LANES)
            _shift_rows(ext_ref, sh_ref, off)
            d = dy1_ref[:, pl.ds(off, LANES)]
            for k in range(K):
                dcw_ref[k:k + 1, pl.ds(off, LANES)] += jnp.sum(
                    d * _rows_from(ext_ref, sh_ref, off, first + k), axis=0, keepdims=True)
            return 0

        lax.fori_loop(0, W // LANES, chan_w, 0)

    cur, halo, gate = _conv_specs(S, W)
    vec = pl.BlockSpec((1, W), lambda b, i: (0, 0))
    taps = pl.BlockSpec((K, W), lambda b, i: (0, 0))
    one = pl.BlockSpec((None, BLK, W), lambda b, i: (b, i, 0))
    return _call_hosting(
        body, exch, "conv_bwd1", (B, S // BLK), [cur, halo, gate, one, one, vec, vec],
        [one, one, taps, vec, vec, vec],
        [jax.ShapeDtypeStruct((B, S, W), F32), jax.ShapeDtypeStruct((B, S, W), BF16),
         jax.ShapeDtypeStruct((K, W), F32)] + [jax.ShapeDtypeStruct((1, W), F32)] * 3,
        [pltpu.VMEM((CONV_HALO + BLK, W), F32), pltpu.VMEM((SUBLANES - 1, SHIFT_ROWS, LANES), F32)],
        (proj3, proj3, proj3, y1, dy, ln_g, ln_b))


def _conv_bwd2(proj3, dy1, dgate, cw_rev, W):
    B, S, _ = proj3.shape
    K = cw_rev.shape[0]
    NQ = S // BLK
    per = BLK // CONV_HALO

    def body(cur_ref, d_ref, dnext_ref, dgate_ref, cw_ref, dp_ref, ext_ref, dy0_ref, sh_ref):
        i = pl.program_id(1)
        ext_ref[0:BLK, :] = d_ref[...]
        ext_ref[BLK:BLK + CONV_HALO, :] = jnp.where(i == NQ - 1, 0.0, dnext_ref[...])

        def chan(c, _):
            off = pl.multiple_of(c * LANES, LANES)
            _shift_rows(ext_ref, sh_ref, off)
            dy0_ref[:, pl.ds(off, LANES)] = _conv_taps(ext_ref, sh_ref, cw_ref, off, K, 0)
            return 0

        lax.fori_loop(0, W // LANES, chan, 0)
        a = cur_ref[:, :W]
        sg = _sigmoid(cur_ref[:, W:])
        dy0 = dy0_ref[...]
        dp_ref[:, 0:W] = (dy0 * sg).astype(dp_ref.dtype)
        dp_ref[:, W:2 * W] = (dy0 * a * sg * (1.0 - sg)).astype(dp_ref.dtype)
        dp_ref[:, 2 * W:3 * W] = dgate_ref[...]

    cur = pl.BlockSpec((None, BLK, 2 * W), lambda b, i: (b, i, 0))
    one = pl.BlockSpec((None, BLK, W), lambda b, i: (b, i, 0))
    nxt = pl.BlockSpec((None, CONV_HALO, W), lambda b, i: (b, jnp.minimum((i + 1) * per, S // CONV_HALO - 1), 0))
    return pl.pallas_call(
        body, name="conv_bwd2", grid=(B, NQ),
        in_specs=[cur, one, nxt, one, pl.BlockSpec((K, W), lambda b, i: (0, 0))],
        out_specs=pl.BlockSpec((None, BLK, 3 * W), lambda b, i: (b, i, 0)),
        out_shape=jax.ShapeDtypeStruct((B, S, 3 * W), BF16),
        scratch_shapes=[pltpu.VMEM((BLK + CONV_HALO, W), F32), pltpu.VMEM((BLK, W), F32),
                        pltpu.VMEM((SUBLANES - 1, SHIFT_ROWS, LANES), F32)],
        compiler_params=_params(("parallel", "parallel")),
    )(proj3, dy1, dy1, dgate, cw_rev)


def _pack(arrays):
    flat = jnp.concatenate([a.astype(F32).reshape(-1) for a in arrays])
    n = flat.shape[0]
    pad = (-n) % (8 * LANES)
    if pad:
        flat = jnp.concatenate([flat, jnp.zeros((pad,), F32)])
    return flat.reshape(-1, LANES)


def _unpack(packed, shapes, lead=()):
    flat = packed.reshape(lead + (-1,))
    out, off = [], 0
    for shp in shapes:
        n = math.prod(shp)
        out.append(flat[..., off:off + n].reshape(lead + tuple(shp)))
        off += n
    return out


def _cols_from_dev(g):
    g = jnp.moveaxis(g, 0, -2)
    return g.reshape(g.shape[:-2] + (g.shape[-2] * g.shape[-1],))


def _my_cols(full, me):
    n8 = full.shape[-1] // N_DEV
    return lax.dynamic_slice_in_dim(full, me * n8, n8, axis=full.ndim - 1)


def kernel(x, a_norm, a_w_in, a_w_out, b_norm, b_w_in, b_v_ln_g, b_v_ln_b, b_w_s, b_b_s, b_w_out, c_norm, c_w_in, c_conv_w, c_conv_b, c_ln_g, c_ln_b, c_w_out, d_norm, d_w_in, d_b_f, d_w_out, final_norm, loss_target, m_a_norm, m_a_w_in, m_a_w_out, m_b_norm, m_b_w_in, m_b_v_ln_g, m_b_v_ln_b, m_b_w_s, m_b_b_s, m_b_w_out, m_c_norm, m_c_w_in, m_c_conv_w, m_c_conv_b, m_c_ln_g, m_c_ln_b, m_c_w_out, m_d_norm, m_d_w_in, m_d_b_f, m_d_w_out, m_final_norm, v_a_norm, v_a_w_in, v_a_w_out, v_b_norm, v_b_w_in, v_b_v_ln_g, v_b_v_ln_b, v_b_w_s, v_b_b_s, v_b_w_out, v_c_norm, v_c_w_in, v_c_conv_w, v_c_conv_b, v_c_ln_g, v_c_ln_b, v_c_w_out, v_d_norm, v_d_w_in, v_d_b_f, v_d_w_out, v_final_norm):
    B, S, D = x.shape
    T = B * S
    xi, yi, ci = _me()
    me = 4 * xi + 2 * yi + ci

    G = b_w_s.shape[1]
    KC = c_conv_w.shape[1]
    H_D = d_b_f.shape[1]
    W_A = a_w_out.shape[1] * N_DEV
    W_B = b_w_out.shape[1] * N_DEV
    W_C = c_w_out.shape[1] * N_DEV
    W_D = d_w_out.shape[1] * N_DEV
    N_D = d_w_in.shape[2] * N_DEV
    N_D_PAD = -(-N_D // (3 * LANES)) * (3 * LANES)

    big_names = ["a_w_in", "a_w_out", "b_w_in", "b_w_out", "c_w_in", "c_w_out", "d_w_in", "d_w_out"]
    big_w = dict(a_w_in=a_w_in[0], a_w_out=a_w_out[0], b_w_in=b_w_in[0], b_w_out=b_w_out[0],
                 c_w_in=c_w_in[0], c_w_out=c_w_out[0], d_w_in=d_w_in[0], d_w_out=d_w_out[0])
    small_sharded = [b_norm, b_v_ln_g, b_v_ln_b, c_norm, c_conv_w, c_conv_b, c_ln_g, c_ln_b, d_norm]
    first_names, later_names, last_names = big_names[:1], big_names[1:6], big_names[6:]
    x0 = x.reshape(T, D)
    h_a, gathered = _rmsnorm_fwd(
        x0, a_norm, "rms_a", _GatherViaSibling([big_w[n].astype(BF16) for n in first_names] + [_pack(small_sharded)]))
    wg = dict(zip(first_names, gathered[:-1]))
    (b_norm_f, b_lg_f, b_lb_f, c_norm_f, c_cw_f, c_cb_f, c_lg_f, c_lb_f, d_norm_f) = [
        _cols_from_dev(t) for t in _unpack(gathered[-1], [s.shape for s in small_sharded], lead=(N_DEV,))]
    c_cw_f = c_cw_f[0]

    wm = jnp.tril(b_w_s[0]).astype(BF16)
    bs_t = jnp.pad(b_b_s[0].T, ((0, 0), (0, LANES - G)))

    proj_a =_mm_w_dev(h_a, wg["a_w_in"], "proj_a").reshape(B, S, 4 * W_A)
    (o_a, y_a), later = _sb_fwd(proj_a, W_A, SB_HEADS,
                                _Exchange([big_w[n].astype(BF16) for n in later_names], ["gather"] * len(later_names)))
    wg.update(zip(later_names, later))
    a_w_out_f = wg["a_w_out"].reshape(W_A, D)
    b_w_out_f = wg["b_w_out"].reshape(W_B, D)
    c_w_out_f = wg["c_w_out"].reshape(W_C, D)
    y_a = y_a.reshape(T, W_A)
    x1, h_b = _mm(y_a, a_w_out_f, "nn", T, D, W_A, F32, "out_a", 512, D, W_A, res=x0, norm_gain=b_norm_f)
    proj_b = _mm_w_dev(h_b, wg["b_w_in"], "proj_b")
    y_b = _gmlp_fwd(proj_b, wm, bs_t, b_lg_f, b_lb_f, W_B)
    x2, h_c = _mm(y_b, b_w_out_f, "nn", T, D, W_B, F32, "out_b", 512, D, W_B, res=x1, norm_gain=c_norm_f)
    proj_c = _mm_w_dev(h_c, wg["c_w_in"], "proj_c").reshape(B, S, 3 * W_C)
    y_c, y1_c, last = _conv_fwd(
        proj_c, c_cw_f, c_cb_f, c_lg_f, c_lb_f, W_C,
        _Exchange([big_w[n].astype(BF16) for n in last_names], ["gather"] * len(last_names)))
    wg.update(zip(last_names, last))
    d_w_out_f = wg["d_w_out"].reshape(W_D, D)
    d_w_in_f = jnp.pad(_cols_from_dev(wg["d_w_in"]), ((0, 0), (0, N_D_PAD - N_D)))
    y_c = y_c.reshape(T, W_C)
    x3, h_d = _mm(y_c, c_w_out_f, "nn", T, D, W_C, F32, "out_c", 512, D, W_C, res=x2, norm_gain=d_norm_f)
    proj_d = _mm(h_d, d_w_in_f, "nn", T, N_D_PAD, D, F32, "proj_d", 1024, 384, D).reshape(B, S, N_D_PAD)
    f_t = jnp.swapaxes(proj_d[:, :, 4 * W_D:4 * W_D + H_D], 1, 2)
    b_f_col = d_b_f.reshape(H_D, 1)
    cum_t = _fox_gate_fwd(f_t, b_f_col)
    o_d, y_d, lse_d = _fox_fwd(proj_d, cum_t, W_D, H_D)
    y_d = y_d.reshape(T, W_D)

    loss_part, dx, g_final = _loss_head(x3, final_norm.reshape(1, D), loss_target.reshape(T, D), y_d, d_w_out_f)
    loss = lax.psum(loss_part[0, 0], MESH_AXES)

    dy_d = _mm(dx, d_w_out_f, "nt", T, W_D, D, BF16, "dy_d", 512, W_D, D).reshape(B, S, W_D)
    gw_d_out = _mm(y_d, dx, "tn", W_D, D, T, BF16, "gw_d_out", W_D, D, 512).reshape(N_DEV, W_D // N_DEV, D)
    dproj_d, dcum = _fox_bwd(proj_d, cum_t, o_d, lse_d, dy_d, W_D, H_D)
    df_t, g_b_f = _fox_gate_bwd(dcum.reshape(B, H_D, S), f_t, b_f_col)
    F_PAD = N_D_PAD - 4 * W_D
    df = jnp.pad(jnp.swapaxes(df_t, 1, 2), ((0, 0), (0, 0), (0, F_PAD - H_D))).reshape(T, F_PAD)
    tc, tr = min(512, W_D), min(1024, S)
    gw_main = _mm(h_d, dproj_d, "tn", D, 4 * W_D, T, BF16, "gw_d_in", D, tc, tr,
                  b_spec=_sectioned_spec(dproj_d, tr, tc, 2, 1))
    gw_f = _mm(h_d, df, "tn", D, F_PAD, T, BF16, "gw_d_in_f", D, F_PAD, 512)
    gw_d_in = jnp.moveaxis(
        jnp.concatenate([gw_main, gw_f], axis=1)[:, :N_D].reshape(D, N_DEV, N_D // N_DEV), 1, 0)
    dh_f = _mm(df, d_w_in_f[:, 4 * W_D:], "nt", T, D, F_PAD, F32, "dh_d_f", 512, D, F_PAD)
    dx, g_d_norm, _ = _mm(dproj_d, d_w_in_f, "nt", T, D, 4 * W_D, F32, "dh_d", tr, D, tc,
                          a_spec=_sectioned_spec(dproj_d, tr, tc, 0, 2), res=dh_f, norm_bwd=(x3, d_norm_f, dx))

    dy_c = _mm(dx, c_w_out_f, "nt", T, W_C, D, BF16, "dy_c", 512, W_C, D).reshape(B, S, W_C)
    gw_c_out = _mm(y_c, dx, "tn", W_C, D, T, BF16, "gw_c_out", 1024, D, 512).reshape(N_DEV, W_C // N_DEV, D)
    (dy1, dgate_c, g_c_cw, g_c_cb, g_c_lg, g_c_lb), parts_d = _conv_bwd1(
        proj_c, y1_c, dy_c, KC, c_lg_f, c_lb_f, W_C, _Exchange([gw_d_in, gw_d_out], ["scatter"] * 2))
    dproj_c = _conv_bwd2(proj_c, dy1, dgate_c, c_cw_f[::-1], W_C).reshape(T, 3 * W_C)
    gw_c_in = _mm_grad_dev(h_c, dproj_c, "gw_c_in")
    dx, g_c_norm, _ = _mm_wT_dev(dproj_c, wg["c_w_in"], "dh_c", norm_bwd=(x2, c_norm_f, dx))

    dy_b = _mm(dx, b_w_out_f, "nt", T, W_B, D, BF16, "dy_b", 512, W_B, D)
    gw_b_out = _mm(y_b, dx, "tn", W_B, D, T, BF16, "gw_b_out", 1024, D, 512).reshape(N_DEV, W_B // N_DEV, D)
    dproj_b, g_wm, g_bs_t, g_b_lg, g_b_lb = _gmlp_bwd(proj_b, dy_b, wm, bs_t, b_lg_f, b_lb_f, W_B)
    g_b_w_s = jnp.tril(g_wm)
    g_b_b_s = g_bs_t[:, :G].T
    gw_b_in = _mm_grad_dev(h_b, dproj_b, "gw_b_in")
    dx, g_b_norm, _ = _mm_wT_dev(dproj_b, wg["b_w_in"], "dh_b", norm_bwd=(x1, b_norm_f, dx))

    dy_a = _mm(dx, a_w_out_f, "nt", T, W_A, D, BF16, "dy_a", 512, W_A, D).reshape(B, S, W_A)
    gw_a_out = _mm(y_a, dx, "tn", W_A, D, T, BF16, "gw_a_out", W_A, D, 512).reshape(N_DEV, W_A // N_DEV, D)
    small_full = [g_b_norm, g_b_lg, g_b_lb, g_b_b_s, g_c_norm, g_c_cw, g_c_cb, g_c_lg, g_c_lb,
                  g_d_norm, g_b_f, g_final]
    dproj_a, parts_s = _sb_bwd(
        proj_a, o_a, dy_a, W_A, SB_HEADS,
        _Exchange([gw_c_in, gw_c_out, gw_b_in, gw_b_out, gw_a_out, _pack(small_full), g_b_w_s.reshape(-1, LANES)],
                  ["scatter"] * 5 + ["gather"] * 2))
    gw_a_in = _mm_grad_dev(h_a, dproj_a, "gw_a_in")
    dx, g_a_norm, parts_a = _mm_wT_dev(dproj_a, wg["a_w_in"], "dh_a", exch=_Exchange([gw_a_in], ["scatter"]),
                                       norm_bwd=(x0, a_norm, dx))
    grad_x = dx.reshape(B, S, D)

    (parts_n,) = _Exchange([_pack([g_a_norm])], ["gather"]).run("exchange_last")
    big_parts = dict(a_w_in=parts_a[0], a_w_out=parts_s[4], b_w_in=parts_s[2], b_w_out=parts_s[3],
                     c_w_in=parts_s[0], c_w_out=parts_s[1], d_w_in=parts_d[0], d_w_out=parts_d[1])
    (s_b_norm, s_b_lg, s_b_lb, s_b_b_s, s_c_norm, s_c_cw, s_c_cb, s_c_lg, s_c_lb,
     s_d_norm, s_b_f, s_final) = _unpack(_sum_parts(parts_s[5], "sum_small"), [g.shape for g in small_full])
    (s_a_norm,) = _unpack(_sum_parts(parts_n, "sum_a_norm"), [g_a_norm.shape])

    weights = dict(a_norm=a_norm, a_w_in=a_w_in, a_w_out=a_w_out, b_norm=b_norm, b_w_in=b_w_in, b_v_ln_g=b_v_ln_g,
                   b_v_ln_b=b_v_ln_b, b_w_s=b_w_s, b_b_s=b_b_s, b_w_out=b_w_out, c_norm=c_norm, c_w_in=c_w_in,
                   c_conv_w=c_conv_w, c_conv_b=c_conv_b, c_ln_g=c_ln_g, c_ln_b=c_ln_b, c_w_out=c_w_out,
                   d_norm=d_norm, d_w_in=d_w_in, d_b_f=d_b_f, d_w_out=d_w_out, final_norm=final_norm)
    mom_m = dict(a_norm=m_a_norm, a_w_in=m_a_w_in, a_w_out=m_a_w_out, b_norm=m_b_norm, b_w_in=m_b_w_in,
                 b_v_ln_g=m_b_v_ln_g, b_v_ln_b=m_b_v_ln_b, b_w_s=m_b_w_s, b_b_s=m_b_b_s, b_w_out=m_b_w_out,
                 c_norm=m_c_norm, c_w_in=m_c_w_in, c_conv_w=m_c_conv_w, c_conv_b=m_c_conv_b, c_ln_g=m_c_ln_g,
                 c_ln_b=m_c_ln_b, c_w_out=m_c_w_out, d_norm=m_d_norm, d_w_in=m_d_w_in, d_b_f=m_d_b_f,
                 d_w_out=m_d_w_out, final_norm=m_final_norm)
    mom_v = dict(a_norm=v_a_norm, a_w_in=v_a_w_in, a_w_out=v_a_w_out, b_norm=v_b_norm, b_w_in=v_b_w_in,
                 b_v_ln_g=v_b_v_ln_g, b_v_ln_b=v_b_v_ln_b, b_w_s=v_b_w_s, b_b_s=v_b_b_s, b_w_out=v_b_w_out,
                 c_norm=v_c_norm, c_w_in=v_c_w_in, c_conv_w=v_c_conv_w, c_conv_b=v_c_conv_b, c_ln_g=v_c_ln_g,
                 c_ln_b=v_c_ln_b, c_w_out=v_c_w_out, d_norm=v_d_norm, d_w_in=v_d_w_in, d_b_f=v_d_b_f,
                 d_w_out=v_d_w_out, final_norm=v_final_norm)
    order = list(weights)
    grads, deltas, new_m, new_v = {}, {}, {}, {}

    for n in big_names:
        part = big_parts[n]
        shp = weights[n].shape
        R, C = shp[1], shp[2]
        res = _adamw(part, weights[n].reshape(R, C), mom_m[n].reshape(R, C), mom_v[n].reshape(R, C), "adamw_" + n)
        grads[n], deltas[n], new_m[n], new_v[n] = [r.reshape(shp) for r in res]

    res = _adamw(parts_s[6], b_w_s.reshape(-1, LANES), m_b_w_s.reshape(-1, LANES), v_b_w_s.reshape(-1, LANES),
                 "adamw_b_w_s")
    grads["b_w_s"], deltas["b_w_s"], new_m["b_w_s"], new_v["b_w_s"] = [r.reshape(b_w_s.shape) for r in res]

    small_g = dict(
        a_norm=s_a_norm, b_norm=_my_cols(s_b_norm, me), b_v_ln_g=_my_cols(s_b_lg, me),
        b_v_ln_b=_my_cols(s_b_lb, me), b_b_s=s_b_b_s[None], c_norm=_my_cols(s_c_norm, me),
        c_conv_w=_my_cols(s_c_cw, me)[None], c_conv_b=_my_cols(s_c_cb, me), c_ln_g=_my_cols(s_c_lg, me),
        c_ln_b=_my_cols(s_c_lb, me), d_norm=_my_cols(s_d_norm, me), d_b_f=s_b_f.reshape(1, H_D),
        final_norm=s_final.reshape(D))
    small_names = list(small_g)
    sg_p = _pack([small_g[n] for n in small_names])
    res = _adamw(sg_p[None], _pack([weights[n] for n in small_names]), _pack([mom_m[n] for n in small_names]),
                 _pack([mom_v[n] for n in small_names]), "adamw_small")
    shapes = [weights[n].shape for n in small_names]
    for dst, r in zip((grads, deltas, new_m, new_v), res):
        for n, val in zip(small_names, _unpack(r, shapes)):
            dst[n] = val

    return (loss, grad_x, *[grads[n] for n in order], *[deltas[n] for n in order],
            *[new_m[n] for n in order], *[new_v[n] for n in order])
```

```python
import functools
import math

import jax
import jax.numpy as jnp
from jax import lax
from jax.experimental import pallas as pl
from jax.experimental.pallas import tpu as pltpu

F32 = jnp.float32
BF16 = jnp.bfloat16

EPS = 1e-6
SB_HEADS = 16
CONV_HALO = 32
BLK = 128
ATT_TK = 256
FOX_TK = 512
ATT_TQ = 512
ATT_GP = 2
LANES = 128
N_DEV = 8
MESH_AXES = ("x", "y", "c")

ADAM_LR = 0.001
ADAM_B1 = 0.9
ADAM_B2 = 0.999
ADAM_EPS = 1e-08
ADAM_WD = 0.01
ADAM_STEP = 10

VMEM_LIMIT = 56 * 1024 * 1024
NEG_BIG = -1e30

_NN = (((1,), (0,)), ((), ()))
_NT = (((1,), (1,)), ((), ()))
_TN = (((0,), (0,)), ((), ()))


def _dot(a, b, dims=_NN):
    return lax.dot_general(a, b, dims, preferred_element_type=F32)


def _split_dot(x, m):
    hi = lax.bitcast_convert_type(lax.bitcast_convert_type(x, jnp.int32) & jnp.int32(-65536), F32)
    return _dot(hi.astype(BF16), m) + _dot((x - hi).astype(BF16), m)


def _split3_dot(x, m):
    hi = x.astype(BF16)
    r1 = x - hi.astype(F32)
    mid = r1.astype(BF16)
    lo = (r1 - mid.astype(F32)).astype(BF16)
    return _dot(hi, m) + _dot(mid, m) + _dot(lo, m)


def _params(sem=None):
    kw = dict(vmem_limit_bytes=VMEM_LIMIT)
    if sem is not None:
        kw["dimension_semantics"] = sem
    return pltpu.CompilerParams(**kw)


def _sigmoid(x):
    return jax.nn.sigmoid(x)


def _silu(x):
    return x * _sigmoid(x)


def _dsilu(x):
    s = _sigmoid(x)
    return s * (1.0 + x * (1.0 - s))


_GELU_C = math.sqrt(2.0 / math.pi)


def _gelu(x):
    return 0.5 * x * (1.0 + jnp.tanh(_GELU_C * (x + 0.044715 * x * x * x)))


def _dgelu(x):
    th = jnp.tanh(_GELU_C * (x + 0.044715 * x * x * x))
    return 0.5 * (1.0 + th) + 0.5 * x * (1.0 - th * th) * _GELU_C * (1.0 + 3.0 * 0.044715 * x * x)


def _mm(a, b, mode, M, N, K, out_dtype, name, tm, tn, tk, a_spec=None, b_spec=None, o_spec=None, out_shape=None,
        exch=None, res=None, norm_gain=None, norm_bwd=None):
    tm, tn, tk = min(tm, M), min(tn, N), min(tk, K)
    assert M % tm == 0 and N % tn == 0 and K % tk == 0, (name, M, N, K, tm, tn, tk)
    nk = K // tk
    assert norm_gain is None or (nk == 1 and tn == N and exch is None)
    dims = {"nn": _NN, "nt": _NT, "tn": _TN}[mode]
    if a_spec is None:
        a_spec = (pl.BlockSpec((tk, tm), lambda i, j, k: (k, i)) if mode == "tn"
                  else pl.BlockSpec((tm, tk), lambda i, j, k: (i, k)))
    if b_spec is None:
        b_spec = (pl.BlockSpec((tn, tk), lambda i, j, k: (j, k)) if mode == "nt"
                  else pl.BlockSpec((tk, tn), lambda i, j, k: (k, j)))
    if o_spec is None:
        o_spec = pl.BlockSpec((tm, tn), lambda i, j, k: (i, j))
    if out_shape is None:
        out_shape = (M, N)

    def body(a_ref, b_ref, *rest):
        res_ref = rest[0] if res is not None else None
        if nk == 1:
            d = _dot(a_ref[...].astype(BF16), b_ref[...].astype(BF16), dims)
            r = d if res_ref is None else res_ref[...].astype(F32) + d
            if norm_gain is None:
                rest[-1][...] = r.astype(rest[-1].dtype)
            else:
                g_ref, o_ref, h_ref = rest[-3:]
                o_ref[...] = r.astype(o_ref.dtype)
                scale = lax.rsqrt(jnp.mean(r * r, axis=-1, keepdims=True) + EPS)
                h_ref[...] = (r * scale * g_ref[...]).astype(BF16)
            return
        i = pl.program_id(0)
        k = pl.program_id(2)
        if norm_bwd is not None:
            x_ref, g_ref, dres_ref, o_ref, dg_ref, acc_ref = rest[-6:]

            @pl.when(jnp.logical_and(i == 0, k == 0))
            def _():
                dg_ref[...] = jnp.zeros_like(dg_ref)
        else:
            o_ref, acc_ref = rest[-2:]

        @pl.when(k == 0)
        def _():
            acc_ref[...] = jnp.zeros_like(acc_ref) if res_ref is None else res_ref[...].astype(F32)

        acc_ref[...] += _dot(a_ref[...].astype(BF16), b_ref[...].astype(BF16), dims)

        @pl.when(k == nk - 1)
        def _():
            if norm_bwd is None:
                o_ref[...] = acc_ref[...].astype(o_ref.dtype)
            else:
                dh = acc_ref[...]
                xv = x_ref[...]
                r = lax.rsqrt(jnp.mean(xv * xv, axis=-1, keepdims=True) + EPS)
                xh = xv * r
                dxh = dh * g_ref[...]
                o_ref[...] = dres_ref[...] + r * (dxh - xh * jnp.mean(dxh * xh, axis=-1, keepdims=True))
                dg_ref[...] += jnp.sum(dh * xh, axis=0, keepdims=True)

    in_specs, args = [a_spec, b_spec], (a, b)
    if res is not None:
        in_specs, args = in_specs + [o_spec], args + (res,)
    scratch = [pltpu.VMEM((tm, tn), F32)] if nk > 1 else []
    if norm_bwd is not None:
        assert nk > 1 and tn == N and norm_gain is None
        vec = pl.BlockSpec((1, N), lambda i, j, k: (0, 0))
        x_in, g_in, dres_in = norm_bwd
        (dx, dg), moved = _call_hosting(
            body, exch, name, (M // tm, 1, nk), in_specs + [o_spec, vec, o_spec], [o_spec, vec],
            [jax.ShapeDtypeStruct((M, N), F32), jax.ShapeDtypeStruct((1, N), F32)], scratch,
            args + (x_in, g_in, dres_in))
        return dx, dg, moved
    if norm_gain is not None:
        return pl.pallas_call(
            body, name=name, grid=(M // tm, N // tn, nk),
            in_specs=in_specs + [pl.BlockSpec((1, N), lambda i, j, k: (0, 0))], out_specs=[o_spec, o_spec],
            out_shape=[jax.ShapeDtypeStruct(out_shape, out_dtype), jax.ShapeDtypeStruct(out_shape, BF16)],
            compiler_params=_params(("parallel", "parallel", "arbitrary")),
        )(*args, norm_gain)
    if exch is None:
        return pl.pallas_call(
            body, name=name, grid=(M // tm, N // tn, nk),
            in_specs=in_specs, out_specs=o_spec,
            out_shape=jax.ShapeDtypeStruct(out_shape, out_dtype),
            scratch_shapes=scratch,
            compiler_params=_params(("parallel", "parallel", "arbitrary")),
        )(*args)
    (out,), moved = _call_hosting(
        body, exch, name, (M // tm, N // tn, nk), in_specs, [o_spec],
        [jax.ShapeDtypeStruct(out_shape, out_dtype)], scratch, args)
    return out, moved


def _mm_w_dev(a, w3, name, out_dtype=F32, tm=1024):
    M, K = a.shape
    n8 = w3.shape[2]
    tn = n8 if n8 <= 768 else 512
    per = n8 // tn
    b_spec = pl.BlockSpec((None, K, tn), lambda i, j, k: (j // per, 0, j % per))
    return _mm(a, w3, "nn", M, N_DEV * n8, K, out_dtype, name, tm, tn, K, b_spec=b_spec)


def _sectioned_spec(d4, t_rows, t_cols, rows_axis, cols_axis):
    _, _, S, W = d4.shape
    assert S % t_rows == 0 and W % t_cols == 0
    rb, cb = S // t_rows, W // t_cols

    def index(*g):
        r, c = g[rows_axis], g[cols_axis]
        return (r // rb, c // cb, r % rb, c % cb)

    return pl.BlockSpec((None, None, t_rows, t_cols), index)


def _mm_wT_dev(a, w3, name, out_dtype=F32, tm=1024, exch=None, norm_bwd=None):
    K, n8 = w3.shape[1], w3.shape[2]
    tk = n8 if n8 <= 768 else 512
    per = n8 // tk
    b_spec = pl.BlockSpec((None, K, tk), lambda i, j, k: (k // per, 0, k % per))
    if a.ndim == 4:
        M, N = a.shape[0] * a.shape[2], a.shape[1] * a.shape[3]
        tm = min(tm, a.shape[2])
        a_spec = _sectioned_spec(a, tm, tk, 0, 2)
    else:
        (M, N), a_spec = a.shape, None
    return _mm(a, w3, "nt", M, K, N, out_dtype, name, tm, K, tk, a_spec=a_spec, b_spec=b_spec, exch=exch,
               norm_bwd=norm_bwd)


def _mm_grad_dev(h, d, name, out_dtype=BF16):
    T, M = h.shape
    N = d.shape[1] * d.shape[3] if d.ndim == 4 else d.shape[1]
    n8 = N // N_DEV
    tn = n8 if n8 <= 768 else 512
    per = n8 // tn
    tm = min(M, 1024)
    o_spec = pl.BlockSpec((None, tm, tn), lambda i, j, k: (j // per, i, j % per))
    tk = min(1024, d.shape[2] if d.ndim == 4 else T)
    b_spec = _sectioned_spec(d, tk, tn, 2, 1) if d.ndim == 4 else None
    return _mm(h, d, "tn", M, N, T, out_dtype, name, tm, tn, tk, b_spec=b_spec, o_spec=o_spec,
               out_shape=(N_DEV, M, n8))


def _me():
    x, y, c = lax.axis_index("x"), lax.axis_index("y"), lax.axis_index("c")
    return x, y, c


def _peer(r):
    x, y, c = _me()
    px = 1 - x if (r >> 2) & 1 else x
    py = 1 - y if (r >> 1) & 1 else y
    pc = 1 - c if r & 1 else c
    return (px, py, pc), 4 * px + 2 * py + pc


class _Exchange:
    def __init__(self, arrays, kinds):
        self.arrays, self.kinds, self.n = list(arrays), list(kinds), len(arrays)
        self.out_shapes = [
            jax.ShapeDtypeStruct((N_DEV,) + a.shape if kind == "gather" else a.shape, a.dtype)
            for a, kind in zip(arrays, kinds)]
        self.specs = [pl.BlockSpec(memory_space=pl.ANY)] * self.n
        self.sems = [pltpu.SemaphoreType.DMA((self.n, N_DEV - 1)), pltpu.SemaphoreType.DMA((self.n, N_DEV - 1)),
                     pltpu.SemaphoreType.DMA((self.n,))]

    def _copies(self, ins, outs, sems, receiving):
        send_sems, recv_sems, local_sems = sems
        x, y, c = _me()
        me = 4 * x + 2 * y + c

        def src(k, pid):
            return ins[k] if self.kinds[k] == "gather" else ins[k].at[pid]

        local = [pltpu.make_async_copy(src(k, me), outs[k].at[me], local_sems.at[k]) for k in range(self.n)]
        remote = []
        for r in range(1, N_DEV):
            peer, pid = _peer(r)
            for k in range(self.n):
                remote.append(pltpu.make_async_remote_copy(
                    src_ref=src(k, pid), dst_ref=outs[k].at[pid if receiving else me],
                    send_sem=send_sems.at[k, r - 1], recv_sem=recv_sems.at[k, r - 1],
                    device_id=peer, device_id_type=pl.DeviceIdType.MESH))
        return local, remote

    def start(self, ins, outs, sems):
        local, remote = self._copies(ins, outs, sems, False)
        for cp in local + remote:
            cp.start()

    def wait(self, ins, outs, sems):
        local, remote = self._copies(ins, outs, sems, True)
        for cp in remote:
            cp.wait_recv()
        for cp in remote:
            cp.wait_send()
        for cp in local:
            cp.wait()

    def run(self, name):
        n = self.n

        def body(*refs):
            ins, outs, sems = refs[:n], refs[n:2 * n], refs[2 * n:]
            self.start(ins, outs, sems)
            self.wait(ins, outs, sems)

        return pl.pallas_call(
            body, name=name, in_specs=self.specs, out_specs=self.specs, out_shape=self.out_shapes,
            scratch_shapes=self.sems,
        )(*self.arrays)


class _GatherViaSibling(_Exchange):
    ICI = (2, 4, 6)

    def __init__(self, arrays):
        super().__init__(arrays, ["gather"] * len(arrays))

    def _copy(self, ins, outs, sems, k, column, block, to, from_input=False):
        return pltpu.make_async_remote_copy(
            src_ref=ins[k] if from_input else outs[k].at[block], dst_ref=outs[k].at[block],
            send_sem=sems[0].at[k, column], recv_sem=sems[1].at[k, column],
            device_id=to, device_id_type=pl.DeviceIdType.MESH)

    def start(self, ins, outs, sems):
        x, y, c = _me()
        me = 4 * x + 2 * y + c
        for k in range(self.n):
            pltpu.make_async_copy(ins[k], outs[k].at[me], sems[2].at[k]).start()
            self._copy(ins, outs, sems, k, 0, me, _peer(1)[0], True).start()
            for j, r in enumerate(self.ICI):
                self._copy(ins, outs, sems, k, 1 + j, me, _peer(r)[0], True).start()

    def wait(self, ins, outs, sems):
        x, y, c = _me()
        me = 4 * x + 2 * y + c
        sibling, sibling_id = _peer(1)
        for j, r in enumerate(self.ICI):
            peer, pid = _peer(r)
            for k in range(self.n):
                self._copy(ins, outs, sems, k, 1 + j, pid, peer).wait_recv()
                self._copy(ins, outs, sems, k, 4 + j, pid, sibling).start()
        for k in range(self.n):
            self._copy(ins, outs, sems, k, 0, sibling_id, sibling).wait_recv()
            for j, r in enumerate(self.ICI):
                self._copy(ins, outs, sems, k, 4 + j, _peer(r ^ 1)[1], sibling).wait_recv()
            for column in range(N_DEV - 1):
                self._copy(ins, outs, sems, k, column, me, sibling).wait_send()
            pltpu.make_async_copy(ins[k], outs[k].at[me], sems[2].at[k]).wait()


def _call_hosting(body, exch, name, grid, in_specs, out_specs, out_shape, scratch_shapes, args):
    if exch is None:
        res = pl.pallas_call(
            body, name=name, grid=grid, in_specs=list(in_specs), out_specs=list(out_specs),
            out_shape=list(out_shape), scratch_shapes=list(scratch_shapes),
            compiler_params=_params(("arbitrary",) * len(grid)))(*args)
        return res, []
    n_in, n_out, n_scr, nc = len(in_specs), len(out_specs), len(scratch_shapes), exch.n

    def full_body(*refs):
        ins, refs = refs[:n_in], refs[n_in:]
        cins, refs = refs[:nc], refs[nc:]
        outs, refs = refs[:n_out], refs[n_out:]
        couts, refs = refs[:nc], refs[nc:]
        scr, sems = refs[:n_scr], refs[n_scr:]
        ids = [pl.program_id(a) for a in range(len(grid))]
        first = functools.reduce(jnp.logical_and, [i == 0 for i in ids])
        last = functools.reduce(jnp.logical_and, [i == g - 1 for i, g in zip(ids, grid)])

        @pl.when(first)
        def _():
            exch.start(cins, couts, sems)

        body(*ins, *outs, *scr)

        @pl.when(last)
        def _():
            exch.wait(cins, couts, sems)

    res = pl.pallas_call(
        full_body, name=name, grid=grid,
        in_specs=list(in_specs) + exch.specs, out_specs=list(out_specs) + exch.specs,
        out_shape=list(out_shape) + exch.out_shapes,
        scratch_shapes=list(scratch_shapes) + exch.sems,
        compiler_params=_params(("arbitrary",) * len(grid)),
    )(*args, *exch.arrays)
    return res[:n_out], res[n_out:]


def _rmsnorm_fwd(x, g, name, exch=None):
    T, D = x.shape
    tr = min(256, T)

    def body(x_ref, g_ref, h_ref):
        xv = x_ref[...]
        r = lax.rsqrt(jnp.mean(xv * xv, axis=-1, keepdims=True) + EPS)
        h_ref[...] = (xv * r * g_ref[...]).astype(BF16)

    row = pl.BlockSpec((tr, D), lambda i: (i, 0))
    (h,), moved = _call_hosting(
        body, exch, name, (T // tr,), [row, pl.BlockSpec((1, D), lambda i: (0, 0))], [row],
        [jax.ShapeDtypeStruct((T, D), BF16)], [], (x, g))
    return h, moved


def _loss_head(x, g, target, y=None, w=None):
    T, D = x.shape
    tr = min(512 if y is not None else 256, T)

    def body(x_ref, g_ref, t_ref, *rest):
        loss_ref, dx_ref, dg_ref = rest[-3:]
        i = pl.program_id(0)
        xv = x_ref[...]
        if y is not None:
            xv = xv + _dot(rest[0][...].astype(BF16), rest[1][...].astype(BF16))
        gv = g_ref[...]
        r = lax.rsqrt(jnp.mean(xv * xv, axis=-1, keepdims=True) + EPS)
        xh = xv * r
        diff = xh * gv - t_ref[...]
        dy = diff * (1.0 / D)
        dxh = dy * gv
        dx_ref[...] = r * (dxh - xh * jnp.mean(dxh * xh, axis=-1, keepdims=True))

        @pl.when(i == 0)
        def _():
            dg_ref[...] = jnp.zeros_like(dg_ref)
            loss_ref[...] = jnp.zeros_like(loss_ref)

        dg_ref[...] += jnp.sum(dy * xh, axis=0, keepdims=True)
        part = jnp.sum(jnp.sum(diff * diff, axis=1, keepdims=True), axis=0, keepdims=True)
        loss_ref[...] += (0.5 / D) * part

    row = pl.BlockSpec((tr, D), lambda i: (i, 0))
    vec = pl.BlockSpec((1, D), lambda i: (0, 0))
    in_specs, args = [row, vec, row], (x, g, target)
    if y is not None:
        K = y.shape[1]
        in_specs += [pl.BlockSpec((tr, K), lambda i: (i, 0)), pl.BlockSpec((K, D), lambda i: (0, 0))]
        args += (y, w)
    return pl.pallas_call(
        body, name="loss_head", grid=(T // tr,),
        in_specs=in_specs,
        out_specs=[pl.BlockSpec((1, 1), lambda i: (0, 0)), row, vec],
        out_shape=[jax.ShapeDtypeStruct((1, 1), F32), jax.ShapeDtypeStruct((T, D), F32),
                   jax.ShapeDtypeStruct((1, D), F32)],
        compiler_params=_params(("arbitrary",)),
    )(*args)


ELEMS_PER_STEP = 1 << 20


def _row_tile(R, per_row):
    best = None
    for tr in range(8, R + 1, 8):
        if R % tr == 0 and tr * per_row <= ELEMS_PER_STEP:
            best = tr
    return best if best is not None else R


def _adamw(parts, w, m, v, name):
    P, R, C = parts.shape
    tr = _row_tile(R, P * C)

    def body(p_ref, w_ref, m_ref, v_ref, g_out, d_out, m_out, v_out):
        g = p_ref[0].astype(F32)
        for p in range(1, P):
            g = g + p_ref[p].astype(F32)
        wv = w_ref[...]
        mn = ADAM_B1 * m_ref[...] + (1.0 - ADAM_B1) * g
        vn = ADAM_B2 * v_ref[...] + (1.0 - ADAM_B2) * (g * g)
        m_hat = mn / (1.0 - ADAM_B1 ** ADAM_STEP)
        v_hat = vn / (1.0 - ADAM_B2 ** ADAM_STEP)
        g_out[...] = g
        d_out[...] = -ADAM_LR * (m_hat / (jnp.sqrt(v_hat) + ADAM_EPS) + ADAM_WD * wv)
        m_out[...] = mn
        v_out[...] = vn

    row = pl.BlockSpec((tr, C), lambda i: (i, 0))
    return pl.pallas_call(
        body, name=name, grid=(R // tr,),
        in_specs=[pl.BlockSpec((P, tr, C), lambda i: (0, i, 0)), row, row, row],
        out_specs=[row, row, row, row],
        out_shape=[jax.ShapeDtypeStruct((R, C), F32)] * 4,
        compiler_params=_params(("parallel",)),
    )(parts, w, m, v)


def _sum_parts(parts, name):
    P, R, C = parts.shape
    tr = _row_tile(R, P * C)

    def body(p_ref, o_ref):
        g = p_ref[0]
        for p in range(1, P):
            g = g + p_ref[p]
        o_ref[...] = g

    return pl.pallas_call(
        body, name=name, grid=(R // tr,),
        in_specs=[pl.BlockSpec((P, tr, C), lambda i: (0, i, 0))],
        out_specs=pl.BlockSpec((tr, C), lambda i: (i, 0)),
        out_shape=jax.ShapeDtypeStruct((R, C), F32),
        compiler_params=_params(("parallel",)),
    )(parts)


def _lane_head(Dh):
    assert Dh & (Dh - 1) == 0 and Dh <= LANES
    return lax.shift_right_logical(lax.broadcasted_iota(jnp.int32, (1, LANES), 1), Dh.bit_length() - 1)


def _stack_heads(x, lane_head, hpb):
    return jnp.concatenate([jnp.where(lane_head == h, x, 0.0) for h in range(hpb)], axis=0)


def _unstack_heads(acc, lane_head, hpb):
    TQ = acc.shape[0] // hpb
    out = acc[0:TQ]
    for h in range(1, hpb):
        out = jnp.where(lane_head == h, acc[h * TQ:(h + 1) * TQ], out)
    return out


def _live_rows(x, r0, hpb):
    if r0 == 0:
        return x
    TQ = x.shape[0] // hpb
    return jnp.concatenate([x[h * TQ + r0:(h + 1) * TQ] for h in range(hpb)], axis=0)


def _put_rows(full, part, r0, hpb):
    if r0 == 0:
        return part
    TQ = full.shape[0] // hpb
    n = TQ - r0
    return jnp.concatenate(
        [blk for h in range(hpb) for blk in (full[h * TQ:h * TQ + r0], part[h * n:(h + 1) * n])], axis=0)


def _first_live_row(m, TQ, TK):
    return max(0, TQ - (m + 1) * TK)


def _key_tile(S, tk=None):
    tk = ATT_TK if tk is None else tk
    return tk if S % tk == 0 else BLK


def _query_tile(S):
    return ATT_TQ if S % ATT_TQ == 0 else BLK


def _lane_groups(P):
    return ATT_GP if P % ATT_GP == 0 else 1


def _lanes(u):
    return slice(u * LANES, (u + 1) * LANES)


def _causal_iotas(RS, TK, TQ, r0=0):
    n = TQ - r0
    assert n & (n - 1) == 0 and (TK % TQ == 0 or TQ % TK == 0)
    rows = RS // TQ * n
    trow = jnp.bitwise_and(lax.broadcasted_iota(jnp.int32, (rows, TK), 0), n - 1) + r0
    col = lax.broadcasted_iota(jnp.int32, (rows, TK), 1)
    return trow, col


def _tri(TK, op):
    r = lax.broadcasted_iota(jnp.int32, (TK, TK), 0)
    c = lax.broadcasted_iota(jnp.int32, (TK, TK), 1)
    return op(r, c).astype(BF16)


def _logsig_parts(z):
    lb = jnp.minimum(z, 0.0) - jnp.log(1.0 + jnp.exp(-jnp.abs(z)))
    return lb, lb - z


def _sb_fwd(proj3, W, heads, exch):
    B, S, _ = proj3.shape
    Dh = W // heads
    hpb = LANES // Dh
    P, TQ = W // LANES, _query_tile(S)
    NQ = S // TQ
    scale = 1.0 / math.sqrt(Dh)

    TK = _key_tile(S)
    RS = hpb * TQ
    NM = max(1, TQ // TK)
    GP = _lane_groups(P)
    PG = P // GP

    def body(q_ref, k_ref, v_ref, g_ref, o_ref, y_ref):
        i = pl.program_id(2)
        lane_head = _lane_head(Dh)
        msuf = _tri(TK, lambda r, c: r > c)
        qs = [(_stack_heads(q_ref[:, _lanes(u)], lane_head, hpb) * scale).astype(BF16) for u in range(GP)]
        nt = (i * TQ + TQ - 2) // TK + 1

        def tile(jt, carry, masked, r0=0):
            off = pl.multiple_of(jt * TK, TK)
            if masked:
                trow, col = _causal_iotas(RS, TK, TQ, r0)
                msk = col + (jt * TK - i * TQ) < trow
            out = []
            for u, (rem_all, acc_all) in enumerate(carry):
                rem, acc = _live_rows(rem_all, r0, hpb), _live_rows(acc_all, r0, hpb)
                kj = k_ref[pl.ds(off, TK), _lanes(u)].astype(BF16)
                vj = v_ref[pl.ds(off, TK), _lanes(u)].astype(BF16)
                lb, lr = _logsig_parts(_dot(_live_rows(qs[u], r0, hpb), kj, _NT))
                if masked:
                    lr = jnp.where(msk, lr, 0.0)
                w = jnp.exp(lb + _split_dot(lr, msuf) + rem)
                if masked:
                    w = jnp.where(msk, w, 0.0)
                out.append((_put_rows(rem_all, rem + jnp.sum(lr, axis=1, keepdims=True), r0, hpb),
                            _put_rows(acc_all, acc + _dot(w.astype(BF16), vj), r0, hpb)))
            return tuple(out)

        zero = (jnp.zeros((RS, 1), F32), jnp.zeros((RS, LANES), F32))
        carry = (zero,) * GP
        for m in range(NM):
            carry = tile(nt - 1 - m, carry, True, _first_live_row(m, TQ, TK))
        carry = lax.fori_loop(NM, nt, lambda jj, c: tile(nt - 1 - jj, c, False), carry)
        for u in range(GP):
            o = _unstack_heads(carry[u][1], lane_head, hpb)
            o_ref[:, _lanes(u)] = o
            y_ref[:, _lanes(u)] = (o * _silu(g_ref[:, _lanes(u)])).astype(BF16)

    LW = GP * LANES
    blk = lambda sec: pl.BlockSpec((None, TQ, LW), lambda b, p, i: (b, i, sec * PG + p))
    full = lambda sec: pl.BlockSpec((None, S, LW), lambda b, p, i: (b, 0, sec * PG + p))
    out = pl.BlockSpec((None, TQ, LW), lambda b, p, i: (b, i, p))
    return _call_hosting(
        body, exch, "sb_fwd", (B, PG, NQ), [blk(0), full(1), full(2), blk(3)], [out, out],
        [jax.ShapeDtypeStruct((B, S, W), F32), jax.ShapeDtypeStruct((B, S, W), BF16)], [],
        (proj3, proj3, proj3, proj3))


def _sb_bwd(proj3, o, dy, W, heads, exch):
    B, S, _ = proj3.shape
    Dh = W // heads
    hpb = LANES // Dh
    P, TQ = W // LANES, _query_tile(S)
    NQ = S // TQ
    scale = 1.0 / math.sqrt(Dh)

    TK = _key_tile(S)
    RS = hpb * TQ
    NM = max(1, TQ // TK)

    def body(q_ref, k_ref, v_ref, g_ref, o_ref, dy_ref, dp_ref, dk_ref, dv_ref, u_ref, sig_ref, es_ref):
        i = pl.program_id(2)
        rows = pl.ds(pl.multiple_of(i * TQ, TQ), TQ)

        @pl.when(i == 0)
        def _():
            dk_ref[...] = jnp.zeros_like(dk_ref)
            dv_ref[...] = jnp.zeros_like(dv_ref)

        lane_head = _lane_head(Dh)
        msuf = _tri(TK, lambda r, c: r > c)
        mpre = _tri(TK, lambda r, c: r < c)
        g = g_ref[...]
        dyv = dy_ref[...].astype(F32)
        dp_ref[3, rows, :] = (dyv * o_ref[...] * _dsilu(g)).astype(dp_ref.dtype)
        qs = (_stack_heads(q_ref[...], lane_head, hpb) * scale).astype(BF16)
        dos = _stack_heads(dyv * _silu(g), lane_head, hpb).astype(BF16)
        nt = (i * TQ + TQ - 2) // TK + 1

        def weights(jt, rem_all, masked, r0=0):
            off = pl.multiple_of(jt * TK, TK)
            kj = k_ref[pl.ds(off, TK), :].astype(BF16)
            vj = v_ref[pl.ds(off, TK), :].astype(BF16)
            dos_l = _live_rows(dos, r0, hpb)
            lb, lr = _logsig_parts(_dot(_live_rows(qs, r0, hpb), kj, _NT))
            if masked:
                trow, col = _causal_iotas(RS, TK, TQ, r0)
                msk = col + (jt * TK - i * TQ) < trow
                lr = jnp.where(msk, lr, 0.0)
            w = jnp.exp(lb + _split_dot(lr, msuf) + _live_rows(rem_all, r0, hpb))
            if masked:
                w = jnp.where(msk, w, 0.0)
            e = w * _dot(dos_l, vj, _NT)
            sig = jnp.exp(lb)
            u = e - sig * (e + _split_dot(e, mpre))
            if masked:
                u = jnp.where(msk, u, 0.0)
                sig = jnp.where(msk, sig, 0.0)
            n = TQ - r0
            for h in range(hpb):
                u_ref[jt, h * TQ + r0:(h + 1) * TQ, :] = u[h * n:(h + 1) * n]
                sig_ref[jt, h * TQ + r0:(h + 1) * TQ, :] = sig[h * n:(h + 1) * n]
            es_ref[jt] = _put_rows(jnp.zeros((RS, 1), F32), jnp.sum(e, axis=1, keepdims=True), r0, hpb)
            dv_ref[pl.ds(off, TK), :] += _dot(w.astype(BF16), dos_l, _TN)
            return _put_rows(rem_all, _live_rows(rem_all, r0, hpb) + jnp.sum(lr, axis=1, keepdims=True), r0, hpb)

        rem = jnp.zeros((RS, 1), F32)
        for m in range(NM):
            rem = weights(nt - 1 - m, rem, True, _first_live_row(m, TQ, TK))
        lax.fori_loop(NM, nt, lambda jj, r: weights(nt - 1 - jj, r, False), rem)

        def grads(jt, carry, r0=0):
            pre, acc = carry
            off = pl.multiple_of(jt * TK, TK)
            kj = k_ref[pl.ds(off, TK), :].astype(BF16)
            if r0 == 0:
                u, sig = u_ref[jt], sig_ref[jt]
            else:
                u = jnp.concatenate([u_ref[jt, h * TQ + r0:(h + 1) * TQ, :] for h in range(hpb)], axis=0)
                sig = jnp.concatenate([sig_ref[jt, h * TQ + r0:(h + 1) * TQ, :] for h in range(hpb)], axis=0)
            dz = (u - _live_rows(pre, r0, hpb) * sig).astype(BF16)
            dk_ref[pl.ds(off, TK), :] += _dot(dz, _live_rows(qs, r0, hpb), _TN)
            return pre + es_ref[jt], _put_rows(acc, _live_rows(acc, r0, hpb) + _dot(dz, kj), r0, hpb)

        carry = lax.fori_loop(0, nt - NM, grads, (jnp.zeros((RS, 1), F32), jnp.zeros((RS, LANES), F32)))
        for m in reversed(range(NM)):
            carry = grads(nt - 1 - m, carry, _first_live_row(m, TQ, TK))
        _, acc = carry
        dp_ref[0, rows, :] = (_unstack_heads(acc, lane_head, hpb) * scale).astype(dp_ref.dtype)

        @pl.when(i == NQ - 1)
        def _():
            dp_ref[1] = dk_ref[...].astype(dp_ref.dtype)
            dp_ref[2] = dv_ref[...].astype(dp_ref.dtype)

    blk = lambda sec: pl.BlockSpec((None, TQ, LANES), lambda b, p, i: (b, i, sec * P + p))
    full = lambda sec: pl.BlockSpec((None, S, LANES), lambda b, p, i: (b, 0, sec * P + p))
    one = pl.BlockSpec((None, TQ, LANES), lambda b, p, i: (b, i, p))
    (dproj,), moved = _call_hosting(
        body, exch, "sb_bwd", (B, P, NQ), [blk(0), full(1), full(2), blk(3), one, one],
        [pl.BlockSpec((None, 4, S, LANES), lambda b, p, i: (b, 0, 0, p))],
        [jax.ShapeDtypeStruct((B, 4, S, W), BF16)],
        [pltpu.VMEM((S, LANES), F32), pltpu.VMEM((S, LANES), F32),
         pltpu.VMEM((S // TK, RS, TK), F32), pltpu.VMEM((S // TK, RS, TK), F32), pltpu.VMEM((S // TK, RS, 1), F32)],
        (proj3, proj3, proj3, proj3, o, dy))
    return dproj, moved


def _fox_gate_fwd(f_t, b_f):
    B, H, S = f_t.shape

    def body(f_ref, b_ref, c_ref):
        row = lax.broadcasted_iota(jnp.int32, (BLK, BLK), 0)
        col = lax.broadcasted_iota(jnp.int32, (BLK, BLK), 1)
        mpre = (row <= col).astype(BF16)
        carry = jnp.zeros((H, 1), F32)
        for n in range(S // BLK):
            sl = pl.ds(n * BLK, BLK)
            lf, _ = _logsig_parts(f_ref[:, sl] + b_ref[...])
            c_ref[:, sl] = _split3_dot(lf, mpre) + carry
            carry = carry + jnp.sum(lf, axis=1, keepdims=True)

    spec = pl.BlockSpec((None, H, S), lambda b: (b, 0, 0))
    return pl.pallas_call(
        body, name="fox_gate_fwd", grid=(B,),
        in_specs=[spec, pl.BlockSpec((H, 1), lambda b: (0, 0))], out_specs=spec,
        out_shape=jax.ShapeDtypeStruct((B, H, S), F32),
        compiler_params=_params(("parallel",)),
    )(f_t, b_f)


def _fox_gate_bwd(dcum_t, f_t, b_f):
    B, H, S = f_t.shape

    def body(d_ref, f_ref, b_ref, df_ref, db_ref):
        b = pl.program_id(0)

        @pl.when(b == 0)
        def _():
            db_ref[...] = jnp.zeros_like(db_ref)

        row = lax.broadcasted_iota(jnp.int32, (BLK, BLK), 0)
        col = lax.broadcasted_iota(jnp.int32, (BLK, BLK), 1)
        msuf = (row >= col).astype(BF16)
        carry = jnp.zeros((H, 1), F32)
        dbacc = jnp.zeros((H, 1), F32)
        for n in reversed(range(S // BLK)):
            sl = pl.ds(n * BLK, BLK)
            dv = d_ref[:, sl]
            dlf = _split3_dot(dv, msuf) + carry
            carry = carry + jnp.sum(dv, axis=1, keepdims=True)
            df = dlf * _sigmoid(-(f_ref[:, sl] + b_ref[...]))
            df_ref[:, sl] = df
            dbacc = dbacc + jnp.sum(df, axis=1, keepdims=True)
        db_ref[...] += dbacc

    spec = pl.BlockSpec((None, H, S), lambda b: (b, 0, 0))
    vec = pl.BlockSpec((H, 1), lambda b: (0, 0))
    return pl.pallas_call(
        body, name="fox_gate_bwd", grid=(B,),
        in_specs=[spec, spec, vec], out_specs=[spec, vec],
        out_shape=[jax.ShapeDtypeStruct((B, H, S), F32), jax.ShapeDtypeStruct((H, 1), F32)],
        compiler_params=_params(("arbitrary",)),
    )(dcum_t, f_t, b_f)


def _pick_col(block, idx, lane_iota):
    return jnp.sum(jnp.where(lane_iota == idx, block, 0.0), axis=1, keepdims=True)


def _pick_row(block, idx, sub_iota):
    return jnp.sum(jnp.where(sub_iota == idx, block, 0.0), axis=0, keepdims=True)


def _fox_fwd(proj3, cum_t, W, heads):
    B, S, _ = proj3.shape
    H = heads
    Dh = W // heads
    hpb = LANES // Dh
    P, TQ = W // LANES, _query_tile(S)
    NQ = S // TQ
    scale = 1.0 / math.sqrt(Dh)

    TK = _key_tile(S, FOX_TK)
    RS = hpb * TQ
    NM = max(1, TQ // TK)

    def body(q_ref, k_ref, v_ref, g_ref, ct_ref, o_ref, y_ref, lse_ref):
        p = pl.program_id(1)
        i = pl.program_id(2)
        lane_head = _lane_head(Dh)
        sub_h = lax.broadcasted_iota(jnp.int32, (H, 1), 0)
        qs = (_stack_heads(q_ref[...], lane_head, hpb) * scale).astype(BF16)
        nt = (i * TQ + TQ - 1) // TK + 1

        def tile(jt, carry, masked, r0=0):
            mx, l, acc = carry
            n = TQ - r0
            off = pl.multiple_of(jt * TK, TK)
            kj = k_ref[pl.ds(off, TK), :].astype(BF16)
            vj = v_ref[pl.ds(off, TK), :].astype(BF16)
            ctb = ct_ref[:, pl.ds(off, TK)]
            z = _dot(_live_rows(qs, r0, hpb), kj, _NT)
            s = jnp.concatenate([z[h * n:(h + 1) * n] - _pick_row(ctb, p * hpb + h, sub_h) for h in range(hpb)],
                                axis=0)
            if masked:
                trow, col = _causal_iotas(RS, TK, TQ, r0)
                s = jnp.where(col + (jt * TK - i * TQ) <= trow, s, NEG_BIG)
            mx2 = jnp.maximum(mx, jnp.max(s, axis=1, keepdims=True))
            pe = jnp.exp(s - mx2)
            alpha = jnp.exp(mx - mx2)
            return (mx2, alpha * l + jnp.sum(pe, axis=1, keepdims=True), alpha * acc + _dot(pe.astype(BF16), vj))

        carry = lax.fori_loop(
            0, nt - NM, lambda jt, c: tile(jt, c, False),
            (jnp.full((RS, 1), NEG_BIG, F32), jnp.zeros((RS, 1), F32), jnp.zeros((RS, LANES), F32)))
        for m in reversed(range(NM)):
            carry = tile(nt - 1 - m, carry, True)
        mx, l, acc = carry
        o = _unstack_heads(acc / l, lane_head, hpb)
        o_ref[...] = o
        lse_ref[...] = _unstack_heads(jnp.broadcast_to(mx + jnp.log(l), (RS, LANES)), lane_head, hpb)
        y_ref[...] = (o * _silu(g_ref[...])).astype(BF16)

    blk = lambda sec: pl.BlockSpec((None, TQ, LANES), lambda b, p, i: (b, i, sec * P + p))
    full = lambda sec: pl.BlockSpec((None, S, LANES), lambda b, p, i: (b, 0, sec * P + p))
    out = pl.BlockSpec((None, TQ, LANES), lambda b, p, i: (b, i, p))
    return pl.pallas_call(
        body, name="fox_fwd", grid=(B, P, NQ),
        in_specs=[blk(0), full(1), full(2), blk(3),
                  pl.BlockSpec((None, H, S), lambda b, p, i: (b, 0, 0))],
        out_specs=[out, out, out],
        out_shape=[jax.ShapeDtypeStruct((B, S, W), F32), jax.ShapeDtypeStruct((B, S, W), BF16),
                   jax.ShapeDtypeStruct((B, S, W), F32)],
        compiler_params=_params(("parallel", "parallel", "arbitrary")),
    )(proj3, proj3, proj3, proj3, cum_t)


def _fox_bwd(proj3, cum_t, o, lse, dy, W, heads):
    B, S, _ = proj3.shape
    H = heads
    Dh = W // heads
    hpb = LANES // Dh
    P, TQ = W // LANES, _query_tile(S)
    NQ = S // TQ
    scale = 1.0 / math.sqrt(Dh)

    TK = _key_tile(S)
    RS = hpb * TQ
    NM = max(1, TQ // TK)

    def body(q_ref, k_ref, v_ref, g_ref, ct_ref, o_ref, lse_ref, dy_ref,
             dpj_ref, dc_ref, dk_ref, dv_ref, p_scr, dp_scr):
        p = pl.program_id(1)
        i = pl.program_id(2)
        rows = pl.ds(pl.multiple_of(i * TQ, TQ), TQ)

        @pl.when(i == 0)
        def _():
            dk_ref[...] = jnp.zeros_like(dk_ref)
            dv_ref[...] = jnp.zeros_like(dv_ref)
            dc_ref[...] = jnp.zeros_like(dc_ref)

        lane_head = _lane_head(Dh)
        sub_h = lax.broadcasted_iota(jnp.int32, (H, 1), 0)
        lane = lax.broadcasted_iota(jnp.int32, (1, LANES), 1)
        g = g_ref[...]
        lsev = lse_ref[...]
        dyv = dy_ref[...].astype(F32)
        dpj_ref[3, rows, :] = (dyv * o_ref[...] * _dsilu(g)).astype(dpj_ref.dtype)
        qs = (_stack_heads(q_ref[...], lane_head, hpb) * scale).astype(BF16)
        dos = _stack_heads(dyv * _silu(g), lane_head, hpb).astype(BF16)
        neg_lse = -jnp.concatenate([_pick_col(lsev, h * Dh, lane) for h in range(hpb)], axis=0)
        nt = (i * TQ + TQ - 1) // TK + 1

        def probs(jt, dsum, masked, r0=0):
            n = TQ - r0
            off = pl.multiple_of(jt * TK, TK)
            kj = k_ref[pl.ds(off, TK), :].astype(BF16)
            vj = v_ref[pl.ds(off, TK), :].astype(BF16)
            ctb = ct_ref[:, pl.ds(off, TK)]
            dos_l = _live_rows(dos, r0, hpb)
            z = _dot(_live_rows(qs, r0, hpb), kj, _NT) + _live_rows(neg_lse, r0, hpb)
            s = jnp.concatenate([z[h * n:(h + 1) * n] - _pick_row(ctb, p * hpb + h, sub_h) for h in range(hpb)],
                                axis=0)
            pr = jnp.exp(s)
            if masked:
                trow, col = _causal_iotas(RS, TK, TQ, r0)
                pr = jnp.where(col + (jt * TK - i * TQ) <= trow, pr, 0.0)
            dp = _dot(dos_l, vj, _NT)
            p_scr[jt] = _put_rows(jnp.zeros((RS, TK), F32), pr, r0, hpb)
            dp_scr[jt] = _put_rows(jnp.zeros((RS, TK), F32), dp, r0, hpb)
            dv_ref[pl.ds(off, TK), :] += _dot(pr.astype(BF16), dos_l, _TN)
            return _put_rows(dsum, _live_rows(dsum, r0, hpb) + jnp.sum(pr * dp, axis=1, keepdims=True), r0, hpb)

        dsum = lax.fori_loop(0, nt - NM, lambda jt, d: probs(jt, d, False), jnp.zeros((RS, 1), F32))
        for m in reversed(range(NM)):
            dsum = probs(nt - 1 - m, dsum, True)

        def grads(jt, acc, r0=0):
            n = TQ - r0
            off = pl.multiple_of(jt * TK, TK)
            kj = k_ref[pl.ds(off, TK), :].astype(BF16)
            if r0 == 0:
                pr, dp = p_scr[jt], dp_scr[jt]
            else:
                pr = jnp.concatenate([p_scr[jt, h * TQ + r0:(h + 1) * TQ, :] for h in range(hpb)], axis=0)
                dp = jnp.concatenate([dp_scr[jt, h * TQ + r0:(h + 1) * TQ, :] for h in range(hpb)], axis=0)
            ds = pr * (dp - _live_rows(dsum, r0, hpb))
            for h in range(hpb):
                dc_ref[h:h + 1, pl.ds(off, TK)] -= jnp.sum(ds[h * n:(h + 1) * n], axis=0, keepdims=True)
            dsb = ds.astype(BF16)
            dk_ref[pl.ds(off, TK), :] += _dot(dsb, _live_rows(qs, r0, hpb), _TN)
            return _put_rows(acc, _live_rows(acc, r0, hpb) + _dot(dsb, kj), r0, hpb)

        acc = lax.fori_loop(0, nt - NM, grads, jnp.zeros((RS, LANES), F32))
        for m in reversed(range(NM)):
            acc = grads(nt - 1 - m, acc, _first_live_row(m, TQ, TK))
        dpj_ref[0, rows, :] = (_unstack_heads(acc, lane_head, hpb) * scale).astype(dpj_ref.dtype)

        @pl.when(i == NQ - 1)
        def _():
            dpj_ref[1] = dk_ref[...].astype(dpj_ref.dtype)
            dpj_ref[2] = dv_ref[...].astype(dpj_ref.dtype)

    blk = lambda sec: pl.BlockSpec((None, TQ, LANES), lambda b, p, i: (b, i, sec * P + p))
    full = lambda sec: pl.BlockSpec((None, S, LANES), lambda b, p, i: (b, 0, sec * P + p))
    one = pl.BlockSpec((None, TQ, LANES), lambda b, p, i: (b, i, p))
    return pl.pallas_call(
        body, name="fox_bwd", grid=(B, P, NQ),
        in_specs=[blk(0), full(1), full(2), blk(3),
                  pl.BlockSpec((None, H, S), lambda b, p, i: (b, 0, 0)),
                  one, one, one],
        out_specs=[pl.BlockSpec((None, 4, S, LANES), lambda b, p, i: (b, 0, 0, p)),
                   pl.BlockSpec((None, None, hpb, S), lambda b, p, i: (b, p, 0, 0))],
        out_shape=[jax.ShapeDtypeStruct((B, 4, S, W), BF16), jax.ShapeDtypeStruct((B, P, hpb, S), F32)],
        scratch_shapes=[pltpu.VMEM((S, LANES), F32), pltpu.VMEM((S, LANES), F32),
                        pltpu.VMEM((S // TK, RS, TK), F32), pltpu.VMEM((S // TK, RS, TK), F32)],
        compiler_params=_params(("parallel", "parallel", "arbitrary")),
    )(proj3, proj3, proj3, proj3, cum_t, o, lse, dy)


def _layernorm_rows(v, gamma, beta):
    mu = jnp.mean(v, axis=-1, keepdims=True)
    xc = v - mu
    rstd = lax.rsqrt(jnp.mean(xc * xc, axis=-1, keepdims=True) + EPS)
    xh = xc * rstd
    return xh, rstd, xh * gamma + beta


def _layernorm_rows_bwd(dout, xh, rstd, gamma):
    dxh = dout * gamma
    return rstd * (dxh - jnp.mean(dxh, axis=-1, keepdims=True) - xh * jnp.mean(dxh * xh, axis=-1, keepdims=True))


def _gmlp_fwd(proj, wm, bs_t, ln_g, ln_b, W):
    T = proj.shape[0]
    G = wm.shape[0]
    cg = W // G
    assert cg == LANES

    def body(p_ref, wm_ref, bs_ref, lg_ref, lb_ref, y_ref, vn_ref):
        lane = lax.broadcasted_iota(jnp.int32, (1, LANES), 1)
        _, _, vn = _layernorm_rows(_gelu(p_ref[:, W:2 * W]), lg_ref[...], lb_ref[...])
        vn_ref[...] = vn.astype(BF16)
        bs = bs_ref[...]
        for g in range(G):
            sl = pl.ds(g * cg, cg)
            s = _dot(wm_ref[g], vn_ref[:, sl]) + _pick_col(bs, g, lane)
            gate = p_ref[:, pl.ds(2 * W + g * cg, cg)]
            y_ref[:, sl] = (_gelu(p_ref[:, sl]) * s * _silu(gate)).astype(BF16)

    vec = pl.BlockSpec((1, W), lambda r: (0, 0))
    return pl.pallas_call(
        body, name="gmlp_fwd", grid=(T // BLK,),
        in_specs=[pl.BlockSpec((BLK, 3 * W), lambda r: (r, 0)),
                  pl.BlockSpec((G, BLK, BLK), lambda r: (0, 0, 0)),
                  pl.BlockSpec((BLK, LANES), lambda r: (0, 0)), vec, vec],
        out_specs=pl.BlockSpec((BLK, W), lambda r: (r, 0)),
        out_shape=jax.ShapeDtypeStruct((T, W), BF16),
        scratch_shapes=[pltpu.VMEM((BLK, W), BF16)],
        compiler_params=_params(("parallel",)),
    )(proj, wm, bs_t, ln_g, ln_b)


def _gmlp_bwd(proj, dy, wm, bs_t, ln_g, ln_b, W):
    T = proj.shape[0]
    G = wm.shape[0]
    cg = W // G

    def body(p_ref, dy_ref, wm_ref, bs_ref, lg_ref, lb_ref,
             dp_ref, dwm_ref, dbs_ref, dlg_ref, dlb_ref, vn_ref, dvn_ref):
        r = pl.program_id(0)

        @pl.when(r == 0)
        def _():
            dwm_ref[...] = jnp.zeros_like(dwm_ref)
            dbs_ref[...] = jnp.zeros_like(dbs_ref)
            dlg_ref[...] = jnp.zeros_like(dlg_ref)
            dlb_ref[...] = jnp.zeros_like(dlb_ref)

        lane = lax.broadcasted_iota(jnp.int32, (1, LANES), 1)
        vpre = p_ref[:, W:2 * W]
        gamma = lg_ref[...]
        xh, rstd, vn = _layernorm_rows(_gelu(vpre), gamma, lb_ref[...])
        vn_ref[...] = vn.astype(BF16)
        bs = bs_ref[...]
        dbs = jnp.zeros((BLK, LANES), F32)
        for g in range(G):
            sl = pl.ds(g * cg, cg)
            gsl = pl.ds(2 * W + g * cg, cg)
            vng = vn_ref[:, sl]
            s = _dot(wm_ref[g], vng) + _pick_col(bs, g, lane)
            upre = p_ref[:, sl]
            u = _gelu(upre)
            gate = p_ref[:, gsl]
            dyv = dy_ref[:, sl].astype(F32)
            dp_ref[:, gsl] = (dyv * u * s * _dsilu(gate)).astype(dp_ref.dtype)
            do = dyv * _silu(gate)
            dp_ref[:, sl] = (do * s * _dgelu(upre)).astype(dp_ref.dtype)
            ds = do * u
            dbs = dbs + jnp.where(lane == g, jnp.sum(ds, axis=1, keepdims=True), 0.0)
            dsb = ds.astype(BF16)
            dwm_ref[g] += _dot(dsb, vng, _NT)
            dvn_ref[:, sl] = _dot(wm_ref[g], dsb, _TN)
        dbs_ref[...] += dbs
        dvn = dvn_ref[...]
        dlg_ref[...] += jnp.sum(dvn * xh, axis=0, keepdims=True)
        dlb_ref[...] += jnp.sum(dvn, axis=0, keepdims=True)
        dv = _layernorm_rows_bwd(dvn, xh, rstd, gamma)
        dp_ref[:, W:2 * W] = (dv * _dgelu(vpre)).astype(dp_ref.dtype)

    vec = pl.BlockSpec((1, W), lambda r: (0, 0))
    return pl.pallas_call(
        body, name="gmlp_bwd", grid=(T // BLK,),
        in_specs=[pl.BlockSpec((BLK, 3 * W), lambda r: (r, 0)),
                  pl.BlockSpec((BLK, W), lambda r: (r, 0)),
                  pl.BlockSpec((G, BLK, BLK), lambda r: (0, 0, 0)),
                  pl.BlockSpec((BLK, LANES), lambda r: (0, 0)), vec, vec],
        out_specs=[pl.BlockSpec((BLK, 3 * W), lambda r: (r, 0)),
                   pl.BlockSpec((G, BLK, BLK), lambda r: (0, 0, 0)),
                   pl.BlockSpec((BLK, LANES), lambda r: (0, 0)), vec, vec],
        out_shape=[jax.ShapeDtypeStruct((T, 3 * W), BF16), jax.ShapeDtypeStruct((G, BLK, BLK), F32),
                   jax.ShapeDtypeStruct((BLK, LANES), F32),
                   jax.ShapeDtypeStruct((1, W), F32), jax.ShapeDtypeStruct((1, W), F32)],
        scratch_shapes=[pltpu.VMEM((BLK, W), BF16), pltpu.VMEM((BLK, W), F32)],
        compiler_params=_params(("arbitrary",)),
    )(proj, dy, wm, bs_t, ln_g, ln_b)


SUBLANES = 8
SHIFT_ROWS = CONV_HALO + BLK - SUBLANES


def _shift_rows(ext_ref, sh_ref, off):
    for r in range(1, SUBLANES):
        sh_ref[r - 1] = ext_ref[pl.ds(r, SHIFT_ROWS), pl.ds(off, LANES)]


def _rows_from(ext_ref, sh_ref, off, start):
    r = start % SUBLANES
    if r == 0:
        return ext_ref[pl.ds(start, BLK), pl.ds(off, LANES)]
    return sh_ref[r - 1, pl.ds(start - r, BLK), :]


def _conv_taps(ext_ref, sh_ref, cw_ref, off, n_taps, first):
    acc = jnp.zeros((BLK, LANES), F32)
    for k in range(n_taps):
        acc = acc + cw_ref[k:k + 1, pl.ds(off, LANES)] * _rows_from(ext_ref, sh_ref, off, first + k)
    return acc


def _fill_glu_ext(ext_ref, halo_ref, cur_ref, W, first_block):
    y0h = halo_ref[:, :W] * _sigmoid(halo_ref[:, W:])
    ext_ref[0:CONV_HALO, :] = jnp.where(first_block, 0.0, y0h)
    ext_ref[CONV_HALO:CONV_HALO + BLK, :] = cur_ref[:, :W] * _sigmoid(cur_ref[:, W:])


def _conv_specs(S, W):
    per = BLK // CONV_HALO
    cur = pl.BlockSpec((None, BLK, 2 * W), lambda b, i: (b, i, 0))
    halo = pl.BlockSpec((None, CONV_HALO, 2 * W), lambda b, i: (b, jnp.maximum(i * per - 1, 0), 0))
    gate = pl.BlockSpec((None, BLK, W), lambda b, i: (b, i, 2))
    return cur, halo, gate


def _conv_fwd(proj3, cw, cb, ln_g, ln_b, W, exch):
    B, S, _ = proj3.shape
    K = cw.shape[0]
    first = CONV_HALO - (K - 1)
    assert first >= 0

    def body(cur_ref, halo_ref, g_ref, cw_ref, cb_ref, lg_ref, lb_ref, y_ref, y1_ref, ext_ref, sh_ref):
        i = pl.program_id(1)
        _fill_glu_ext(ext_ref, halo_ref, cur_ref, W, i == 0)

        def chan(c, _):
            off = pl.multiple_of(c * LANES, LANES)
            _shift_rows(ext_ref, sh_ref, off)
            y1_ref[:, pl.ds(off, LANES)] = (_conv_taps(ext_ref, sh_ref, cw_ref, off, K, first)
                                            + cb_ref[:, pl.ds(off, LANES)])
            return 0

        lax.fori_loop(0, W // LANES, chan, 0)
        _, _, ln = _layernorm_rows(y1_ref[...], lg_ref[...], lb_ref[...])
        y_ref[...] = (_silu(ln) * _silu(g_ref[...])).astype(BF16)

    cur, halo, gate = _conv_specs(S, W)
    vec = pl.BlockSpec((1, W), lambda b, i: (0, 0))
    one = pl.BlockSpec((None, BLK, W), lambda b, i: (b, i, 0))
    (y, y1), moved = _call_hosting(
        body, exch, "conv_fwd", (B, S // BLK),
        [cur, halo, gate, pl.BlockSpec((K, W), lambda b, i: (0, 0)), vec, vec, vec],
        [one, one], [jax.ShapeDtypeStruct((B, S, W), BF16), jax.ShapeDtypeStruct((B, S, W), F32)],
        [pltpu.VMEM((CONV_HALO + BLK, W), F32), pltpu.VMEM((SUBLANES - 1, SHIFT_ROWS, LANES), F32)],
        (proj3, proj3, proj3, cw, cb, ln_g, ln_b))
    return y, y1, moved


def _conv_bwd1(proj3, y1, dy, K, ln_g, ln_b, W, exch):
    B, S, _ = proj3.shape
    first = CONV_HALO - (K - 1)

    def body(cur_ref, halo_ref, g_ref, y1_ref, dy_ref, lg_ref, lb_ref,
             dy1_ref, dg_ref, dcw_ref, dcb_ref, dlg_ref, dlb_ref, ext_ref, sh_ref):
        b = pl.program_id(0)
        i = pl.program_id(1)

        @pl.when(jnp.logical_and(b == 0, i == 0))
        def _():
            dcw_ref[...] = jnp.zeros_like(dcw_ref)
            dcb_ref[...] = jnp.zeros_like(dcb_ref)
            dlg_ref[...] = jnp.zeros_like(dlg_ref)
            dlb_ref[...] = jnp.zeros_like(dlb_ref)

        _fill_glu_ext(ext_ref, halo_ref, cur_ref, W, i == 0)
        gamma = lg_ref[...]
        xh, rstd, ln = _layernorm_rows(y1_ref[...], gamma, lb_ref[...])
        g = g_ref[...]
        dyv = dy_ref[...].astype(F32)
        dg_ref[...] = (dyv * _silu(ln) * _dsilu(g)).astype(dg_ref.dtype)
        dln = dyv * _silu(g) * _dsilu(ln)
        dlg_ref[...] += jnp.sum(dln * xh, axis=0, keepdims=True)
        dlb_ref[...] += jnp.sum(dln, axis=0, keepdims=True)
        dy1 = _layernorm_rows_bwd(dln, xh, rstd, gamma)
        dy1_ref[...] = dy1
        dcb_ref[...] += jnp.sum(dy1, axis=0, keepdims=True)

        def chan_w(c, _):
            off = pl.multiple_of(c * LANES, LANES)
            _shift_rows(ext_ref, sh_ref, off)
            d = dy1_ref[:, pl.ds(off, LANES)]
            for k in range(K):
                dcw_ref[k:k + 1, pl.ds(off, LANES)] += jnp.sum(
                    d * _rows_from(ext_ref, sh_ref, off, first + k), axis=0, keepdims=True)
            return 0

        lax.fori_loop(0, W // LANES, chan_w, 0)

    cur, halo, gate = _conv_specs(S, W)
    vec = pl.BlockSpec((1, W), lambda b, i: (0, 0))
    taps = pl.BlockSpec((K, W), lambda b, i: (0, 0))
    one = pl.BlockSpec((None, BLK, W), lambda b, i: (b, i, 0))
    return _call_hosting(
        body, exch, "conv_bwd1", (B, S // BLK), [cur, halo, gate, one, one, vec, vec],
        [one, one, taps, vec, vec, vec],
        [jax.ShapeDtypeStruct((B, S, W), F32), jax.ShapeDtypeStruct((B, S, W), BF16),
         jax.ShapeDtypeStruct((K, W), F32)] + [jax.ShapeDtypeStruct((1, W), F32)] * 3,
        [pltpu.VMEM((CONV_HALO + BLK, W), F32), pltpu.VMEM((SUBLANES - 1, SHIFT_ROWS, LANES), F32)],
        (proj3, proj3, proj3, y1, dy, ln_g, ln_b))


def _conv_bwd2(proj3, dy1, dgate, cw_rev, W):
    B, S, _ = proj3.shape
    K = cw_rev.shape[0]
    NQ = S // BLK
    per = BLK // CONV_HALO

    def body(cur_ref, d_ref, dnext_ref, dgate_ref, cw_ref, dp_ref, ext_ref, dy0_ref, sh_ref):
        i = pl.program_id(1)
        ext_ref[0:BLK, :] = d_ref[...]
        ext_ref[BLK:BLK + CONV_HALO, :] = jnp.where(i == NQ - 1, 0.0, dnext_ref[...])

        def chan(c, _):
            off = pl.multiple_of(c * LANES, LANES)
            _shift_rows(ext_ref, sh_ref, off)
            dy0_ref[:, pl.ds(off, LANES)] = _conv_taps(ext_ref, sh_ref, cw_ref, off, K, 0)
            return 0

        lax.fori_loop(0, W // LANES, chan, 0)
        a = cur_ref[:, :W]
        sg = _sigmoid(cur_ref[:, W:])
        dy0 = dy0_ref[...]
        dp_ref[:, 0:W] = (dy0 * sg).astype(dp_ref.dtype)
        dp_ref[:, W:2 * W] = (dy0 * a * sg * (1.0 - sg)).astype(dp_ref.dtype)
        dp_ref[:, 2 * W:3 * W] = dgate_ref[...]

    cur = pl.BlockSpec((None, BLK, 2 * W), lambda b, i: (b, i, 0))
    one = pl.BlockSpec((None, BLK, W), lambda b, i: (b, i, 0))
    nxt = pl.BlockSpec((None, CONV_HALO, W), lambda b, i: (b, jnp.minimum((i + 1) * per, S // CONV_HALO - 1), 0))
    return pl.pallas_call(
        body, name="conv_bwd2", grid=(B, NQ),
        in_specs=[cur, one, nxt, one, pl.BlockSpec((K, W), lambda b, i: (0, 0))],
        out_specs=pl.BlockSpec((None, BLK, 3 * W), lambda b, i: (b, i, 0)),
        out_shape=jax.ShapeDtypeStruct((B, S, 3 * W), BF16),
        scratch_shapes=[pltpu.VMEM((BLK + CONV_HALO, W), F32), pltpu.VMEM((BLK, W), F32),
                        pltpu.VMEM((SUBLANES - 1, SHIFT_ROWS, LANES), F32)],
        compiler_params=_params(("parallel", "parallel")),
    )(proj3, dy1, dy1, dgate, cw_rev)


def _pack(arrays):
    flat = jnp.concatenate([a.astype(F32).reshape(-1) for a in arrays])
    n = flat.shape[0]
    pad = (-n) % (8 * LANES)
    if pad:
        flat = jnp.concatenate([flat, jnp.zeros((pad,), F32)])
    return flat.reshape(-1, LANES)


def _unpack(packed, shapes, lead=()):
    flat = packed.reshape(lead + (-1,))
    out, off = [], 0
    for shp in shapes:
        n = math.prod(shp)
        out.append(flat[..., off:off + n].reshape(lead + tuple(shp)))
        off += n
    return out


def _cols_from_dev(g):
    g = jnp.moveaxis(g, 0, -2)
    return g.reshape(g.shape[:-2] + (g.shape[-2] * g.shape[-1],))


def _my_cols(full, me):
    n8 = full.shape[-1] // N_DEV
    return lax.dynamic_slice_in_dim(full, me * n8, n8, axis=full.ndim - 1)


def kernel(x, a_norm, a_w_in, a_w_out, b_norm, b_w_in, b_v_ln_g, b_v_ln_b, b_w_s, b_b_s, b_w_out, c_norm, c_w_in, c_conv_w, c_conv_b, c_ln_g, c_ln_b, c_w_out, d_norm, d_w_in, d_b_f, d_w_out, final_norm, loss_target, m_a_norm, m_a_w_in, m_a_w_out, m_b_norm, m_b_w_in, m_b_v_ln_g, m_b_v_ln_b, m_b_w_s, m_b_b_s, m_b_w_out, m_c_norm, m_c_w_in, m_c_conv_w, m_c_conv_b, m_c_ln_g, m_c_ln_b, m_c_w_out, m_d_norm, m_d_w_in, m_d_b_f, m_d_w_out, m_final_norm, v_a_norm, v_a_w_in, v_a_w_out, v_b_norm, v_b_w_in, v_b_v_ln_g, v_b_v_ln_b, v_b_w_s, v_b_b_s, v_b_w_out, v_c_norm, v_c_w_in, v_c_conv_w, v_c_conv_b, v_c_ln_g, v_c_ln_b, v_c_w_out, v_d_norm, v_d_w_in, v_d_b_f, v_d_w_out, v_final_norm):
    B, S, D = x.shape
    T = B * S
    xi, yi, ci = _me()
    me = 4 * xi + 2 * yi + ci

    G = b_w_s.shape[1]
    KC = c_conv_w.shape[1]
    H_D = d_b_f.shape[1]
    W_A = a_w_out.shape[1] * N_DEV
    W_B = b_w_out.shape[1] * N_DEV
    W_C = c_w_out.shape[1] * N_DEV
    W_D = d_w_out.shape[1] * N_DEV
    N_D = d_w_in.shape[2] * N_DEV
    N_D_PAD = -(-N_D // (3 * LANES)) * (3 * LANES)

    big_names = ["a_w_in", "a_w_out", "b_w_in", "b_w_out", "c_w_in", "c_w_out", "d_w_in", "d_w_out"]
    big_w = dict(a_w_in=a_w_in[0], a_w_out=a_w_out[0], b_w_in=b_w_in[0], b_w_out=b_w_out[0],
                 c_w_in=c_w_in[0], c_w_out=c_w_out[0], d_w_in=d_w_in[0], d_w_out=d_w_out[0])
    small_sharded = [b_norm, b_v_ln_g, b_v_ln_b, c_norm, c_conv_w, c_conv_b, c_ln_g, c_ln_b, d_norm]
    first_names, later_names, last_names = big_names[:1], big_names[1:6], big_names[6:]
    x0 = x.reshape(T, D)
    h_a, gathered = _rmsnorm_fwd(
        x0, a_norm, "rms_a", _GatherViaSibling([big_w[n].astype(BF16) for n in first_names] + [_pack(small_sharded)]))
    wg = dict(zip(first_names, gathered[:-1]))
    (b_norm_f, b_lg_f, b_lb_f, c_norm_f, c_cw_f, c_cb_f, c_lg_f, c_lb_f, d_norm_f) = [
        _cols_from_dev(t) for t in _unpack(gathered[-1], [s.shape for s in small_sharded], lead=(N_DEV,))]
    c_cw_f = c_cw_f[0]

    wm = jnp.tril(b_w_s[0]).astype(BF16)
    bs_t = jnp.pad(b_b_s[0].T, ((0, 0), (0, LANES - G)))

    proj_a =_mm_w_dev(h_a, wg["a_w_in"], "proj_a").reshape(B, S, 4 * W_A)
    (o_a, y_a), later = _sb_fwd(proj_a, W_A, SB_HEADS,
                                _Exchange([big_w[n].astype(BF16) for n in later_names], ["gather"] * len(later_names)))
    wg.update(zip(later_names, later))
    a_w_out_f = wg["a_w_out"].reshape(W_A, D)
    b_w_out_f = wg["b_w_out"].reshape(W_B, D)
    c_w_out_f = wg["c_w_out"].reshape(W_C, D)
    y_a = y_a.reshape(T, W_A)
    x1, h_b = _mm(y_a, a_w_out_f, "nn", T, D, W_A, F32, "out_a", 512, D, W_A, res=x0, norm_gain=b_norm_f)
    proj_b = _mm_w_dev(h_b, wg["b_w_in"], "proj_b")
    y_b = _gmlp_fwd(proj_b, wm, bs_t, b_lg_f, b_lb_f, W_B)
    x2, h_c = _mm(y_b, b_w_out_f, "nn", T, D, W_B, F32, "out_b", 512, D, W_B, res=x1, norm_gain=c_norm_f)
    proj_c = _mm_w_dev(h_c, wg["c_w_in"], "proj_c").reshape(B, S, 3 * W_C)
    y_c, y1_c, last = _conv_fwd(
        proj_c, c_cw_f, c_cb_f, c_lg_f, c_lb_f, W_C,
        _Exchange([big_w[n].astype(BF16) for n in last_names], ["gather"] * len(last_names)))
    wg.update(zip(last_names, last))
    d_w_out_f = wg["d_w_out"].reshape(W_D, D)
    d_w_in_f = jnp.pad(_cols_from_dev(wg["d_w_in"]), ((0, 0), (0, N_D_PAD - N_D)))
    y_c = y_c.reshape(T, W_C)
    x3, h_d = _mm(y_c, c_w_out_f, "nn", T, D, W_C, F32, "out_c", 512, D, W_C, res=x2, norm_gain=d_norm_f)
    proj_d = _mm(h_d, d_w_in_f, "nn", T, N_D_PAD, D, F32, "proj_d", 1024, 384, D).reshape(B, S, N_D_PAD)
    f_t = jnp.swapaxes(proj_d[:, :, 4 * W_D:4 * W_D + H_D], 1, 2)
    b_f_col = d_b_f.reshape(H_D, 1)
    cum_t = _fox_gate_fwd(f_t, b_f_col)
    o_d, y_d, lse_d = _fox_fwd(proj_d, cum_t, W_D, H_D)
    y_d = y_d.reshape(T, W_D)

    loss_part, dx, g_final = _loss_head(x3, final_norm.reshape(1, D), loss_target.reshape(T, D), y_d, d_w_out_f)
    loss = lax.psum(loss_part[0, 0], MESH_AXES)

    dy_d = _mm(dx, d_w_out_f, "nt", T, W_D, D, BF16, "dy_d", 512, W_D, D).reshape(B, S, W_D)
    gw_d_out = _mm(y_d, dx, "tn", W_D, D, T, BF16, "gw_d_out", W_D, D, 1024).reshape(N_DEV, W_D // N_DEV, D)
    dproj_d, dcum = _fox_bwd(proj_d, cum_t, o_d, lse_d, dy_d, W_D, H_D)
    df_t, g_b_f = _fox_gate_bwd(dcum.reshape(B, H_D, S), f_t, b_f_col)
    F_PAD = N_D_PAD - 4 * W_D
    df = jnp.pad(jnp.swapaxes(df_t, 1, 2), ((0, 0), (0, 0), (0, F_PAD - H_D))).reshape(T, F_PAD)
    tc, tr = min(512, W_D), min(1024, S)
    gw_main = _mm(h_d, dproj_d, "tn", D, 4 * W_D, T, BF16, "gw_d_in", D, tc, tr,
                  b_spec=_sectioned_spec(dproj_d, tr, tc, 2, 1))
    gw_f = _mm(h_d, df, "tn", D, F_PAD, T, BF16, "gw_d_in_f", D, F_PAD, 512)
    gw_d_in = jnp.moveaxis(
        jnp.concatenate([gw_main, gw_f], axis=1)[:, :N_D].reshape(D, N_DEV, N_D // N_DEV), 1, 0)
    dh_f = _mm(df, d_w_in_f[:, 4 * W_D:], "nt", T, D, F_PAD, F32, "dh_d_f", 512, D, F_PAD)
    dx, g_d_norm, _ = _mm(dproj_d, d_w_in_f, "nt", T, D, 4 * W_D, F32, "dh_d", tr, D, tc,
                          a_spec=_sectioned_spec(dproj_d, tr, tc, 0, 2), res=dh_f, norm_bwd=(x3, d_norm_f, dx))

    dy_c = _mm(dx, c_w_out_f, "nt", T, W_C, D, BF16, "dy_c", 512, W_C, D).reshape(B, S, W_C)
    gw_c_out = _mm(y_c, dx, "tn", W_C, D, T, BF16, "gw_c_out", 1024, D, 1024).reshape(N_DEV, W_C // N_DEV, D)
    (dy1, dgate_c, g_c_cw, g_c_cb, g_c_lg, g_c_lb), parts_d = _conv_bwd1(
        proj_c, y1_c, dy_c, KC, c_lg_f, c_lb_f, W_C, _Exchange([gw_d_in, gw_d_out], ["scatter"] * 2))
    dproj_c = _conv_bwd2(proj_c, dy1, dgate_c, c_cw_f[::-1], W_C).reshape(T, 3 * W_C)
    gw_c_in = _mm_grad_dev(h_c, dproj_c, "gw_c_in")
    dx, g_c_norm, _ = _mm_wT_dev(dproj_c, wg["c_w_in"], "dh_c", norm_bwd=(x2, c_norm_f, dx))

    dy_b = _mm(dx, b_w_out_f, "nt", T, W_B, D, BF16, "dy_b", 512, W_B, D)
    gw_b_out = _mm(y_b, dx, "tn", W_B, D, T, BF16, "gw_b_out", 1024, D, 1024).reshape(N_DEV, W_B // N_DEV, D)
    dproj_b, g_wm, g_bs_t, g_b_lg, g_b_lb = _gmlp_bwd(proj_b, dy_b, wm, bs_t, b_lg_f, b_lb_f, W_B)
    g_b_w_s = jnp.tril(g_wm)
    g_b_b_s = g_bs_t[:, :G].T
    gw_b_in = _mm_grad_dev(h_b, dproj_b, "gw_b_in")
    dx, g_b_norm, _ = _mm_wT_dev(dproj_b, wg["b_w_in"], "dh_b", norm_bwd=(x1, b_norm_f, dx))

    dy_a = _mm(dx, a_w_out_f, "nt", T, W_A, D, BF16, "dy_a", 512, W_A, D).reshape(B, S, W_A)
    gw_a_out = _mm(y_a, dx, "tn", W_A, D, T, BF16, "gw_a_out", W_A, D, 1024).reshape(N_DEV, W_A // N_DEV, D)
    small_full = [g_b_norm, g_b_lg, g_b_lb, g_b_b_s, g_c_norm, g_c_cw, g_c_cb, g_c_lg, g_c_lb,
                  g_d_norm, g_b_f, g_final]
    dproj_a, parts_s = _sb_bwd(
        proj_a, o_a, dy_a, W_A, SB_HEADS,
        _Exchange([gw_c_in, gw_c_out, gw_b_in, gw_b_out, gw_a_out, _pack(small_full), g_b_w_s.reshape(-1, LANES)],
                  ["scatter"] * 5 + ["gather"] * 2))
    gw_a_in = _mm_grad_dev(h_a, dproj_a, "gw_a_in")
    dx, g_a_norm, parts_a = _mm_wT_dev(dproj_a, wg["a_w_in"], "dh_a", exch=_Exchange([gw_a_in], ["scatter"]),
                                       norm_bwd=(x0, a_norm, dx))
    grad_x = dx.reshape(B, S, D)

    (parts_n,) = _Exchange([_pack([g_a_norm])], ["gather"]).run("exchange_last")
    big_parts = dict(a_w_in=parts_a[0], a_w_out=parts_s[4], b_w_in=parts_s[2], b_w_out=parts_s[3],
                     c_w_in=parts_s[0], c_w_out=parts_s[1], d_w_in=parts_d[0], d_w_out=parts_d[1])
    (s_b_norm, s_b_lg, s_b_lb, s_b_b_s, s_c_norm, s_c_cw, s_c_cb, s_c_lg, s_c_lb,
     s_d_norm, s_b_f, s_final) = _unpack(_sum_parts(parts_s[5], "sum_small"), [g.shape for g in small_full])
    (s_a_norm,) = _unpack(_sum_parts(parts_n, "sum_a_norm"), [g_a_norm.shape])

    weights = dict(a_norm=a_norm, a_w_in=a_w_in, a_w_out=a_w_out, b_norm=b_norm, b_w_in=b_w_in, b_v_ln_g=b_v_ln_g,
                   b_v_ln_b=b_v_ln_b, b_w_s=b_w_s, b_b_s=b_b_s, b_w_out=b_w_out, c_norm=c_norm, c_w_in=c_w_in,
                   c_conv_w=c_conv_w, c_conv_b=c_conv_b, c_ln_g=c_ln_g, c_ln_b=c_ln_b, c_w_out=c_w_out,
                   d_norm=d_norm, d_w_in=d_w_in, d_b_f=d_b_f, d_w_out=d_w_out, final_norm=final_norm)
    mom_m = dict(a_norm=m_a_norm, a_w_in=m_a_w_in, a_w_out=m_a_w_out, b_norm=m_b_norm, b_w_in=m_b_w_in,
                 b_v_ln_g=m_b_v_ln_g, b_v_ln_b=m_b_v_ln_b, b_w_s=m_b_w_s, b_b_s=m_b_b_s, b_w_out=m_b_w_out,
                 c_norm=m_c_norm, c_w_in=m_c_w_in, c_conv_w=m_c_conv_w, c_conv_b=m_c_conv_b, c_ln_g=m_c_ln_g,
                 c_ln_b=m_c_ln_b, c_w_out=m_c_w_out, d_norm=m_d_norm, d_w_in=m_d_w_in, d_b_f=m_d_b_f,
                 d_w_out=m_d_w_out, final_norm=m_final_norm)
    mom_v = dict(a_norm=v_a_norm, a_w_in=v_a_w_in, a_w_out=v_a_w_out, b_norm=v_b_norm, b_w_in=v_b_w_in,
                 b_v_ln_g=v_b_v_ln_g, b_v_ln_b=v_b_v_ln_b, b_w_s=v_b_w_s, b_b_s=v_b_b_s, b_w_out=v_b_w_out,
                 c_norm=v_c_norm, c_w_in=v_c_w_in, c_conv_w=v_c_conv_w, c_conv_b=v_c_conv_b, c_ln_g=v_c_ln_g,
                 c_ln_b=v_c_ln_b, c_w_out=v_c_w_out, d_norm=v_d_norm, d_w_in=v_d_w_in, d_b_f=v_d_b_f,
                 d_w_out=v_d_w_out, final_norm=v_final_norm)
    order = list(weights)
    grads, deltas, new_m, new_v = {}, {}, {}, {}

    for n in big_names:
        part = big_parts[n]
        shp = weights[n].shape
        R, C = shp[1], shp[2]
        res = _adamw(part, weights[n].reshape(R, C), mom_m[n].reshape(R, C), mom_v[n].reshape(R, C), "adamw_" + n)
        grads[n], deltas[n], new_m[n], new_v[n] = [r.reshape(shp) for r in res]

    res = _adamw(parts_s[6], b_w_s.reshape(-1, LANES), m_b_w_s.reshape(-1, LANES), v_b_w_s.reshape(-1, LANES),
                 "adamw_b_w_s")
    grads["b_w_s"], deltas["b_w_s"], new_m["b_w_s"], new_v["b_w_s"] = [r.reshape(b_w_s.shape) for r in res]

    small_g = dict(
        a_norm=s_a_norm, b_norm=_my_cols(s_b_norm, me), b_v_ln_g=_my_cols(s_b_lg, me),
        b_v_ln_b=_my_cols(s_b_lb, me), b_b_s=s_b_b_s[None], c_norm=_my_cols(s_c_norm, me),
        c_conv_w=_my_cols(s_c_cw, me)[None], c_conv_b=_my_cols(s_c_cb, me), c_ln_g=_my_cols(s_c_lg, me),
        c_ln_b=_my_cols(s_c_lb, me), d_norm=_my_cols(s_d_norm, me), d_b_f=s_b_f.reshape(1, H_D),
        final_norm=s_final.reshape(D))
    small_names = list(small_g)
    sg_p = _pack([small_g[n] for n in small_names])
    res = _adamw(sg_p[None], _pack([weights[n] for n in small_names]), _pack([mom_m[n] for n in small_names]),
                 _pack([mom_v[n] for n in small_names]), "adamw_small")
    shapes = [weights[n].shape for n in small_names]
    for dst, r in zip((grads, deltas, new_m, new_v), res):
        for n, val in zip(small_names, _unpack(r, shapes)):
            dst[n] = val

    return (loss, grad_x, *[grads[n] for n in order], *[deltas[n] for n in order],
            *[new_m[n] for n in order], *[new_v[n] for n in order])
```
